```python
import jax, jax.numpy as jnp
from jax import lax
import numpy as np

D_MODEL = 1024
BATCH = 32
SEQ = 2048
DEPTH = 1

RWKV_WIDTH = 512
HEAD_DIM = 64
N_HEADS = RWKV_WIDTH // HEAD_DIM
CONV_WIDTH = D_MODEL - RWKV_WIDTH
CONV_KERNEL = 31
CONV_PAD = CONV_KERNEL // 2
DECAY_LORA = 64
ICLR_LORA = 64
GATE_LORA = 160
N_DIR = 2
D_FF = 2816
LN_EPS = 1e-5
GN_EPS = 64e-5
NORM_EPS = 1e-12
DEEPNORM_ALPHA = (2.0 * DEPTH) ** 0.25
DEEPNORM_BETA = (8.0 * DEPTH) ** -0.25

SHIFT_COLS = 3 * RWKV_WIDTH + N_DIR * DECAY_LORA + N_DIR * ICLR_LORA + GATE_LORA
IN_COLS = SHIFT_COLS + 2 * CONV_WIDTH
RWKV_SPLITS = (RWKV_WIDTH, 2 * RWKV_WIDTH, 3 * RWKV_WIDTH,
               3 * RWKV_WIDTH + N_DIR * DECAY_LORA,
               3 * RWKV_WIDTH + N_DIR * DECAY_LORA + N_DIR * ICLR_LORA)

kernel_name = "hybrid_rwkv7_conformer_conv_deepnorm_encoder"


def layer_norm(x, g, b, eps=LN_EPS):
    xf = x.astype(jnp.float32)
    mu = jnp.mean(xf, axis=-1, keepdims=True)
    var = jnp.mean(jnp.square(xf - mu), axis=-1, keepdims=True)
    return ((xf - mu) * lax.rsqrt(var + eps) * g + b).astype(x.dtype)


def swiglu_ffn(x, w_in, w_out):
    gate, up = jnp.split(x @ w_in, 2, axis=-1)
    return (jax.nn.silu(gate) * up) @ w_out


def centred_shift(p, mu_prev, mu_next):
    zero = jnp.zeros_like(p[:, :1])
    prev = jnp.concatenate([zero, p[:, :-1]], axis=1)
    nxt = jnp.concatenate([p[:, 1:], zero], axis=1)
    return p + mu_prev * (prev - p) + mu_next * (nxt - p)


def heads(z):
    return z.reshape(*z.shape[:-1], N_HEADS, HEAD_DIM)


def wkv7_bidir_scan(r, w, k, v, kk, a):
    def prep(z):
        z = jnp.stack([z[0], jnp.flip(z[1], axis=1)])
        return jnp.moveaxis(z, 2, 0)

    def both(z):
        return prep(jnp.stack([z, z]))

    xs = (both(r), prep(w), prep(k), both(v), both(kk), prep(a))

    def step(S, inp):
        rt, wt, kt, vt, kkt, at = inp
        sa = jnp.einsum('dbhij,dbhj->dbhi', S, -kkt)
        S = (S * wt[..., None, :] + sa[..., :, None] * (kkt * at)[..., None, :]
             + vt[..., :, None] * kt[..., None, :])
        y = jnp.einsum('dbhij,dbhj->dbhi', S, rt)
        return S, y

    b = r.shape[0]
    s0 = jnp.zeros((N_DIR, b, N_HEADS, HEAD_DIM, HEAD_DIM), jnp.float32)
    _, ys = lax.scan(step, s0, xs)
    y = ys[:, 0] + jnp.flip(ys[:, 1], axis=0)
    return jnp.moveaxis(y, 0, 1)


def rwkv7_mixer(ps, w0, w2, a0, a2, g2, k_k, k_a, r_k, lnx_g, lnx_b):
    ps = ps.astype(jnp.float32)
    bsz, t = ps.shape[0], ps.shape[1]
    r, k, v, wd, ad, gd = jnp.split(ps, RWKV_SPLITS, axis=-1)
    wd = wd.reshape(bsz, t, N_DIR, DECAY_LORA)
    ad = ad.reshape(bsz, t, N_DIR, ICLR_LORA)
    lw = jnp.einsum('btdr,drc->dbtc', jnp.tanh(wd), w2) + w0[:, None, None, :]
    decay = jnp.exp(-jnp.exp(-jax.nn.softplus(-lw) - 0.5))
    a = jax.nn.sigmoid(jnp.einsum('btdr,drc->dbtc', ad, a2) + a0[:, None, None, :])
    g = jax.nn.sigmoid(gd) @ g2
    kk = heads(k * k_k)
    kk = kk / jnp.maximum(jnp.linalg.norm(kk, axis=-1, keepdims=True), NORM_EPS)
    kd = k[None] * (1.0 + (a - 1.0) * k_a)
    rh, vh, kdh = heads(r), heads(v), heads(kd)
    y = wkv7_bidir_scan(rh, heads(decay), kdh, vh, kk, heads(a))
    mu = jnp.mean(y, axis=-1, keepdims=True)
    var = jnp.mean(jnp.square(y - mu), axis=-1, keepdims=True)
    yn = ((y - mu) * lax.rsqrt(var + GN_EPS)).reshape(bsz, t, RWKV_WIDTH) * lnx_g + lnx_b
    bonus = jnp.sum(rh[None] * kdh * r_k, axis=(0, -1))[..., None] * vh
    return (yn + bonus.reshape(bsz, t, RWKV_WIDTH)) * g


def conformer_conv(pc, dw, db, ln_g, ln_b):
    u = pc[..., :CONV_WIDTH] * jax.nn.sigmoid(pc[..., CONV_WIDTH:])
    y = lax.conv_general_dilated(u, dw[:, None, :].astype(u.dtype), window_strides=(1,),
                                 padding=[(CONV_PAD, CONV_PAD)],
                                 dimension_numbers=('NWC', 'WIO', 'NWC'),
                                 feature_group_count=CONV_WIDTH) + db
    return jax.nn.silu(layer_norm(y, ln_g, ln_b))


def _fwd_setup_inputs(seed: int = 0) -> dict:
    key = jax.random.key(seed)
    ks = jax.random.split(key, 32)
    L, D, C = DEPTH, D_MODEL, RWKV_WIDTH
    f32 = jnp.float32

    def nrm(k, shape, scale):
        return jax.random.normal(k, shape, f32) * scale

    col_scale = jnp.ones((IN_COLS,), f32).at[2 * C:3 * C].set(DEEPNORM_BETA)
    return {
        "x": nrm(ks[0], (BATCH, SEQ, D), 1.0),
        "ffn1_w_in": nrm(ks[1], (L, D, 2 * D_FF), D ** -0.5 * DEEPNORM_BETA),
        "ffn1_w_out": nrm(ks[2], (L, D_FF, D), D_FF ** -0.5 * DEEPNORM_BETA),
        "w_in": nrm(ks[3], (L, D, IN_COLS), D ** -0.5) * col_scale,
        "mu_prev": jax.random.uniform(ks[4], (L, SHIFT_COLS), f32, 0.0, 0.4),
        "mu_next": jax.random.uniform(ks[5], (L, SHIFT_COLS), f32, 0.0, 0.4),
        "w0": jax.random.uniform(ks[6], (L, N_DIR, C), f32, -3.0, 1.0),
        "w2": nrm(ks[7], (L, N_DIR, DECAY_LORA, C), 0.1 * DECAY_LORA ** -0.5),
        "a0": nrm(ks[8], (L, N_DIR, C), 0.1),
        "a2": nrm(ks[9], (L, N_DIR, ICLR_LORA, C), 0.5 * ICLR_LORA ** -0.5),
        "g2": nrm(ks[10], (L, GATE_LORA, C), GATE_LORA ** -0.5),
        "k_k": 0.85 + nrm(ks[11], (L, C), 0.05),
        "k_a": 1.0 + nrm(ks[12], (L, C), 0.05),
        "r_k": nrm(ks[13], (L, N_HEADS, HEAD_DIM), 0.1),
        "lnx_g": 1.0 + nrm(ks[14], (L, C), 0.05),
        "lnx_b": nrm(ks[15], (L, C), 0.02),
        "conv_dw": nrm(ks[16], (L, CONV_KERNEL, CONV_WIDTH), CONV_KERNEL ** -0.5),
        "conv_b": nrm(ks[17], (L, CONV_WIDTH), 0.02),
        "conv_ln_g": 1.0 + nrm(ks[18], (L, CONV_WIDTH), 0.05),
        "conv_ln_b": nrm(ks[19], (L, CONV_WIDTH), 0.02),
        "w_out": nrm(ks[20], (L, D, D), D ** -0.5 * DEEPNORM_BETA),
        "ffn2_w_in": nrm(ks[21], (L, D, 2 * D_FF), D ** -0.5 * DEEPNORM_BETA),
        "ffn2_w_out": nrm(ks[22], (L, D_FF, D), D_FF ** -0.5 * DEEPNORM_BETA),
        "ln1_g": 1.0 + nrm(ks[23], (L, D), 0.05),
        "ln1_b": nrm(ks[24], (L, D), 0.02),
        "ln2_g": 1.0 + nrm(ks[25], (L, D), 0.05),
        "ln2_b": nrm(ks[26], (L, D), 0.02),
        "ln3_g": 1.0 + nrm(ks[27], (L, D), 0.05),
        "ln3_b": nrm(ks[28], (L, D), 0.02),
    }


def _fwd_reference(x, ffn1_w_in, ffn1_w_out, w_in, mu_prev, mu_next, w0, w2, a0, a2, g2,
              k_k, k_a, r_k, lnx_g, lnx_b, conv_dw, conv_b, conv_ln_g, conv_ln_b, w_out,
              ffn2_w_in, ffn2_w_out, ln1_g, ln1_b, ln2_g, ln2_b, ln3_g, ln3_b):
    for l in range(DEPTH):
        x = layer_norm(DEEPNORM_ALPHA * x + 0.5 * swiglu_ffn(x, ffn1_w_in[l], ffn1_w_out[l]),
                       ln1_g[l], ln1_b[l])
        p = x @ w_in[l]
        ps = centred_shift(p[..., :SHIFT_COLS], mu_prev[l], mu_next[l])
        y_rwkv = rwkv7_mixer(ps, w0[l], w2[l], a0[l], a2[l], g2[l], k_k[l], k_a[l],
                             r_k[l], lnx_g[l], lnx_b[l]).astype(x.dtype)
        y_conv = conformer_conv(p[..., SHIFT_COLS:], conv_dw[l], conv_b[l],
                                conv_ln_g[l], conv_ln_b[l])
        mix = jnp.concatenate([y_rwkv, y_conv], axis=-1) @ w_out[l]
        x = layer_norm(DEEPNORM_ALPHA * x + mix, ln2_g[l], ln2_b[l])
        x = layer_norm(DEEPNORM_ALPHA * x + 0.5 * swiglu_ffn(x, ffn2_w_in[l], ffn2_w_out[l]),
                       ln3_g[l], ln3_b[l])
    return x


import jax as _jax
import jax.numpy as _jnp

TWIN_FORMAT = 'train_step'
FWD_PARAMS = ['x', 'ffn1_w_in', 'ffn1_w_out', 'w_in', 'mu_prev', 'mu_next', 'w0', 'w2', 'a0', 'a2', 'g2', 'k_k', 'k_a', 'r_k', 'lnx_g', 'lnx_b', 'conv_dw', 'conv_b', 'conv_ln_g', 'conv_ln_b', 'w_out', 'ffn2_w_in', 'ffn2_w_out', 'ln1_g', 'ln1_b', 'ln2_g', 'ln2_b', 'ln3_g', 'ln3_b']
TWIN_WEIGHTS = ['ffn1_w_in', 'ffn1_w_out', 'w_in', 'mu_prev', 'mu_next', 'w0', 'w2', 'a0', 'a2', 'g2', 'k_k', 'k_a', 'r_k', 'lnx_g', 'lnx_b', 'conv_dw', 'conv_b', 'conv_ln_g', 'conv_ln_b', 'w_out', 'ffn2_w_in', 'ffn2_w_out', 'ln1_g', 'ln1_b', 'ln2_g', 'ln2_b', 'ln3_g', 'ln3_b']
TWIN_DIFF_INPUT = 'x'
TWIN_INPUTS = ['x', 'ffn1_w_in', 'ffn1_w_out', 'w_in', 'mu_prev', 'mu_next', 'w0', 'w2', 'a0', 'a2', 'g2', 'k_k', 'k_a', 'r_k', 'lnx_g', 'lnx_b', 'conv_dw', 'conv_b', 'conv_ln_g', 'conv_ln_b', 'w_out', 'ffn2_w_in', 'ffn2_w_out', 'ln1_g', 'ln1_b', 'ln2_g', 'ln2_b', 'ln3_g', 'ln3_b', 'loss_target', 'm_ffn1_w_in', 'm_ffn1_w_out', 'm_w_in', 'm_mu_prev', 'm_mu_next', 'm_w0', 'm_w2', 'm_a0', 'm_a2', 'm_g2', 'm_k_k', 'm_k_a', 'm_r_k', 'm_lnx_g', 'm_lnx_b', 'm_conv_dw', 'm_conv_b', 'm_conv_ln_g', 'm_conv_ln_b', 'm_w_out', 'm_ffn2_w_in', 'm_ffn2_w_out', 'm_ln1_g', 'm_ln1_b', 'm_ln2_g', 'm_ln2_b', 'm_ln3_g', 'm_ln3_b', 'v_ffn1_w_in', 'v_ffn1_w_out', 'v_w_in', 'v_mu_prev', 'v_mu_next', 'v_w0', 'v_w2', 'v_a0', 'v_a2', 'v_g2', 'v_k_k', 'v_k_a', 'v_r_k', 'v_lnx_g', 'v_lnx_b', 'v_conv_dw', 'v_conv_b', 'v_conv_ln_g', 'v_conv_ln_b', 'v_w_out', 'v_ffn2_w_in', 'v_ffn2_w_out', 'v_ln1_g', 'v_ln1_b', 'v_ln2_g', 'v_ln2_b', 'v_ln3_g', 'v_ln3_b']
TWIN_OUTPUTS = ['loss', 'grad_x', 'grad_ffn1_w_in', 'grad_ffn1_w_out', 'grad_w_in', 'grad_mu_prev', 'grad_mu_next', 'grad_w0', 'grad_w2', 'grad_a0', 'grad_a2', 'grad_g2', 'grad_k_k', 'grad_k_a', 'grad_r_k', 'grad_lnx_g', 'grad_lnx_b', 'grad_conv_dw', 'grad_conv_b', 'grad_conv_ln_g', 'grad_conv_ln_b', 'grad_w_out', 'grad_ffn2_w_in', 'grad_ffn2_w_out', 'grad_ln1_g', 'grad_ln1_b', 'grad_ln2_g', 'grad_ln2_b', 'grad_ln3_g', 'grad_ln3_b', 'delta_ffn1_w_in', 'delta_ffn1_w_out', 'delta_w_in', 'delta_mu_prev', 'delta_mu_next', 'delta_w0', 'delta_w2', 'delta_a0', 'delta_a2', 'delta_g2', 'delta_k_k', 'delta_k_a', 'delta_r_k', 'delta_lnx_g', 'delta_lnx_b', 'delta_conv_dw', 'delta_conv_b', 'delta_conv_ln_g', 'delta_conv_ln_b', 'delta_w_out', 'delta_ffn2_w_in', 'delta_ffn2_w_out', 'delta_ln1_g', 'delta_ln1_b', 'delta_ln2_g', 'delta_ln2_b', 'delta_ln3_g', 'delta_ln3_b', 'new_m_ffn1_w_in', 'new_m_ffn1_w_out', 'new_m_w_in', 'new_m_mu_prev', 'new_m_mu_next', 'new_m_w0', 'new_m_w2', 'new_m_a0', 'new_m_a2', 'new_m_g2', 'new_m_k_k', 'new_m_k_a', 'new_m_r_k', 'new_m_lnx_g', 'new_m_lnx_b', 'new_m_conv_dw', 'new_m_conv_b', 'new_m_conv_ln_g', 'new_m_conv_ln_b', 'new_m_w_out', 'new_m_ffn2_w_in', 'new_m_ffn2_w_out', 'new_m_ln1_g', 'new_m_ln1_b', 'new_m_ln2_g', 'new_m_ln2_b', 'new_m_ln3_g', 'new_m_ln3_b', 'new_v_ffn1_w_in', 'new_v_ffn1_w_out', 'new_v_w_in', 'new_v_mu_prev', 'new_v_mu_next', 'new_v_w0', 'new_v_w2', 'new_v_a0', 'new_v_a2', 'new_v_g2', 'new_v_k_k', 'new_v_k_a', 'new_v_r_k', 'new_v_lnx_g', 'new_v_lnx_b', 'new_v_conv_dw', 'new_v_conv_b', 'new_v_conv_ln_g', 'new_v_conv_ln_b', 'new_v_w_out', 'new_v_ffn2_w_in', 'new_v_ffn2_w_out', 'new_v_ln1_g', 'new_v_ln1_b', 'new_v_ln2_g', 'new_v_ln2_b', 'new_v_ln3_g', 'new_v_ln3_b']
TWIN_LEAF_KINDS = {'loss': 'loss', 'grad_x': 'grad_x', 'grad_ffn1_w_in': 'grad_w', 'grad_ffn1_w_out': 'grad_w', 'grad_w_in': 'grad_w', 'grad_mu_prev': 'grad_w', 'grad_mu_next': 'grad_w', 'grad_w0': 'grad_w', 'grad_w2': 'grad_w', 'grad_a0': 'grad_w', 'grad_a2': 'grad_w', 'grad_g2': 'grad_w', 'grad_k_k': 'grad_w', 'grad_k_a': 'grad_w', 'grad_r_k': 'grad_w', 'grad_lnx_g': 'grad_w', 'grad_lnx_b': 'grad_w', 'grad_conv_dw': 'grad_w', 'grad_conv_b': 'grad_w', 'grad_conv_ln_g': 'grad_w', 'grad_conv_ln_b': 'grad_w', 'grad_w_out': 'grad_w', 'grad_ffn2_w_in': 'grad_w', 'grad_ffn2_w_out': 'grad_w', 'grad_ln1_g': 'grad_w', 'grad_ln1_b': 'grad_w', 'grad_ln2_g': 'grad_w', 'grad_ln2_b': 'grad_w', 'grad_ln3_g': 'grad_w', 'grad_ln3_b': 'grad_w', 'delta_ffn1_w_in': 'delta_w', 'delta_ffn1_w_out': 'delta_w', 'delta_w_in': 'delta_w', 'delta_mu_prev': 'delta_w', 'delta_mu_next': 'delta_w', 'delta_w0': 'delta_w', 'delta_w2': 'delta_w', 'delta_a0': 'delta_w', 'delta_a2': 'delta_w', 'delta_g2': 'delta_w', 'delta_k_k': 'delta_w', 'delta_k_a': 'delta_w', 'delta_r_k': 'delta_w', 'delta_lnx_g': 'delta_w', 'delta_lnx_b': 'delta_w', 'delta_conv_dw': 'delta_w', 'delta_conv_b': 'delta_w', 'delta_conv_ln_g': 'delta_w', 'delta_conv_ln_b': 'delta_w', 'delta_w_out': 'delta_w', 'delta_ffn2_w_in': 'delta_w', 'delta_ffn2_w_out': 'delta_w', 'delta_ln1_g': 'delta_w', 'delta_ln1_b': 'delta_w', 'delta_ln2_g': 'delta_w', 'delta_ln2_b': 'delta_w', 'delta_ln3_g': 'delta_w', 'delta_ln3_b': 'delta_w', 'new_m_ffn1_w_in': 'new_m', 'new_m_ffn1_w_out': 'new_m', 'new_m_w_in': 'new_m', 'new_m_mu_prev': 'new_m', 'new_m_mu_next': 'new_m', 'new_m_w0': 'new_m', 'new_m_w2': 'new_m', 'new_m_a0': 'new_m', 'new_m_a2': 'new_m', 'new_m_g2': 'new_m', 'new_m_k_k': 'new_m', 'new_m_k_a': 'new_m', 'new_m_r_k': 'new_m', 'new_m_lnx_g': 'new_m', 'new_m_lnx_b': 'new_m', 'new_m_conv_dw': 'new_m', 'new_m_conv_b': 'new_m', 'new_m_conv_ln_g': 'new_m', 'new_m_conv_ln_b': 'new_m', 'new_m_w_out': 'new_m', 'new_m_ffn2_w_in': 'new_m', 'new_m_ffn2_w_out': 'new_m', 'new_m_ln1_g': 'new_m', 'new_m_ln1_b': 'new_m', 'new_m_ln2_g': 'new_m', 'new_m_ln2_b': 'new_m', 'new_m_ln3_g': 'new_m', 'new_m_ln3_b': 'new_m', 'new_v_ffn1_w_in': 'new_v', 'new_v_ffn1_w_out': 'new_v', 'new_v_w_in': 'new_v', 'new_v_mu_prev': 'new_v', 'new_v_mu_next': 'new_v', 'new_v_w0': 'new_v', 'new_v_w2': 'new_v', 'new_v_a0': 'new_v', 'new_v_a2': 'new_v', 'new_v_g2': 'new_v', 'new_v_k_k': 'new_v', 'new_v_k_a': 'new_v', 'new_v_r_k': 'new_v', 'new_v_lnx_g': 'new_v', 'new_v_lnx_b': 'new_v', 'new_v_conv_dw': 'new_v', 'new_v_conv_b': 'new_v', 'new_v_conv_ln_g': 'new_v', 'new_v_conv_ln_b': 'new_v', 'new_v_w_out': 'new_v', 'new_v_ffn2_w_in': 'new_v', 'new_v_ffn2_w_out': 'new_v', 'new_v_ln1_g': 'new_v', 'new_v_ln1_b': 'new_v', 'new_v_ln2_g': 'new_v', 'new_v_ln2_b': 'new_v', 'new_v_ln3_g': 'new_v', 'new_v_ln3_b': 'new_v'}


def _forward(args):
    return _fwd_reference(*[args[k] for k in FWD_PARAMS])


def _output_shape():
    out = _jax.eval_shape(lambda: _forward(_fwd_setup_inputs(0)))
    return out.shape, out.dtype

N_MICROBATCH = 1
ADAM_LR = 0.001
ADAM_B1 = 0.9
ADAM_B2 = 0.999
ADAM_EPS = 1e-08
ADAM_WD = 0.01
ADAM_STEP = 10
PER_EXAMPLE_BATCH_AXIS = {'x': 0, 'loss_target': 0}
SHARED_INPUTS = []
_WEIGHT_DTYPES = {'ffn1_w_in': _jnp.float32, 'ffn1_w_out': _jnp.float32, 'w_in': _jnp.float32, 'mu_prev': _jnp.float32, 'mu_next': _jnp.float32, 'w0': _jnp.float32, 'w2': _jnp.float32, 'a0': _jnp.float32, 'a2': _jnp.float32, 'g2': _jnp.float32, 'k_k': _jnp.float32, 'k_a': _jnp.float32, 'r_k': _jnp.float32, 'lnx_g': _jnp.float32, 'lnx_b': _jnp.float32, 'conv_dw': _jnp.float32, 'conv_b': _jnp.float32, 'conv_ln_g': _jnp.float32, 'conv_ln_b': _jnp.float32, 'w_out': _jnp.float32, 'ffn2_w_in': _jnp.float32, 'ffn2_w_out': _jnp.float32, 'ln1_g': _jnp.float32, 'ln1_b': _jnp.float32, 'ln2_g': _jnp.float32, 'ln2_b': _jnp.float32, 'ln3_g': _jnp.float32, 'ln3_b': _jnp.float32}
MOMENT_SCALE = {'ffn1_w_in': 1.306417e-02, 'ffn1_w_out': 2.140063e-02, 'w_in': 6.907105e-02, 'mu_prev': 1.083799e-01, 'mu_next': 1.038533e-01, 'w0': 2.307482e-02, 'w2': 4.653132e-03, 'a0': 2.117275e-02, 'a2': 1.404894e-02, 'g2': 6.994254e-02, 'k_k': 2.626671e-02, 'k_a': 7.823305e-02, 'r_k': 9.769283e-02, 'lnx_g': 6.463873e-02, 'lnx_b': 2.560674e-01, 'conv_dw': 7.819308e-02, 'conv_b': 3.593331e-01, 'conv_ln_g': 1.460801e-01, 'conv_ln_b': 2.008627e-01, 'w_out': 1.338249e-01, 'ffn2_w_in': 1.258429e-02, 'ffn2_w_out': 2.063701e-02, 'ln1_g': 5.185371e+00, 'ln1_b': 7.537667e-01, 'ln2_g': 5.634578e+00, 'ln2_b': 8.301598e-01, 'ln3_g': 6.457934e+01, 'ln3_b': 4.087947e+00}


def _to_microbatches(a, axis):
    t = _jnp.moveaxis(a, axis, 0)
    t = t.reshape((N_MICROBATCH, t.shape[0] // N_MICROBATCH) + t.shape[1:])
    return _jnp.moveaxis(t, 1, axis + 1)


def setup_inputs(seed: int = 0) -> dict:
    inp = _fwd_setup_inputs(seed)
    key = _jax.random.fold_in(_jax.random.key(seed), 7919)
    shape, _ = _output_shape()
    out = dict(inp)
    out["loss_target"] = _jax.random.normal(_jax.random.fold_in(key, 0), shape, _jnp.float32)
    for i, name in enumerate(TWIN_WEIGHTS):
        w = inp[name].astype(_jnp.float32)
        if MOMENT_SCALE is None:
            s = _jnp.sqrt(_jnp.mean(_jnp.square(w)) + 1e-30)
        else:
            s = MOMENT_SCALE[name]
        km, kv = _jax.random.split(_jax.random.fold_in(key, i + 1))
        out[name] = w
        out["m_" + name] = s * _jax.random.normal(km, w.shape, _jnp.float32)
        out["v_" + name] = (s * s) * _jax.random.uniform(kv, w.shape, _jnp.float32, 0.5, 1.5)
    if N_MICROBATCH > 1:
        for name, axis in PER_EXAMPLE_BATCH_AXIS.items():
            out[name] = _to_microbatches(out[name], axis)
    return {'x': out['x'], 'ffn1_w_in': out['ffn1_w_in'], 'ffn1_w_out': out['ffn1_w_out'], 'w_in': out['w_in'], 'mu_prev': out['mu_prev'], 'mu_next': out['mu_next'], 'w0': out['w0'], 'w2': out['w2'], 'a0': out['a0'], 'a2': out['a2'], 'g2': out['g2'], 'k_k': out['k_k'], 'k_a': out['k_a'], 'r_k': out['r_k'], 'lnx_g': out['lnx_g'], 'lnx_b': out['lnx_b'], 'conv_dw': out['conv_dw'], 'conv_b': out['conv_b'], 'conv_ln_g': out['conv_ln_g'], 'conv_ln_b': out['conv_ln_b'], 'w_out': out['w_out'], 'ffn2_w_in': out['ffn2_w_in'], 'ffn2_w_out': out['ffn2_w_out'], 'ln1_g': out['ln1_g'], 'ln1_b': out['ln1_b'], 'ln2_g': out['ln2_g'], 'ln2_b': out['ln2_b'], 'ln3_g': out['ln3_g'], 'ln3_b': out['ln3_b'], 'loss_target': out['loss_target'], 'm_ffn1_w_in': out['m_ffn1_w_in'], 'm_ffn1_w_out': out['m_ffn1_w_out'], 'm_w_in': out['m_w_in'], 'm_mu_prev': out['m_mu_prev'], 'm_mu_next': out['m_mu_next'], 'm_w0': out['m_w0'], 'm_w2': out['m_w2'], 'm_a0': out['m_a0'], 'm_a2': out['m_a2'], 'm_g2': out['m_g2'], 'm_k_k': out['m_k_k'], 'm_k_a': out['m_k_a'], 'm_r_k': out['m_r_k'], 'm_lnx_g': out['m_lnx_g'], 'm_lnx_b': out['m_lnx_b'], 'm_conv_dw': out['m_conv_dw'], 'm_conv_b': out['m_conv_b'], 'm_conv_ln_g': out['m_conv_ln_g'], 'm_conv_ln_b': out['m_conv_ln_b'], 'm_w_out': out['m_w_out'], 'm_ffn2_w_in': out['m_ffn2_w_in'], 'm_ffn2_w_out': out['m_ffn2_w_out'], 'm_ln1_g': out['m_ln1_g'], 'm_ln1_b': out['m_ln1_b'], 'm_ln2_g': out['m_ln2_g'], 'm_ln2_b': out['m_ln2_b'], 'm_ln3_g': out['m_ln3_g'], 'm_ln3_b': out['m_ln3_b'], 'v_ffn1_w_in': out['v_ffn1_w_in'], 'v_ffn1_w_out': out['v_ffn1_w_out'], 'v_w_in': out['v_w_in'], 'v_mu_prev': out['v_mu_prev'], 'v_mu_next': out['v_mu_next'], 'v_w0': out['v_w0'], 'v_w2': out['v_w2'], 'v_a0': out['v_a0'], 'v_a2': out['v_a2'], 'v_g2': out['v_g2'], 'v_k_k': out['v_k_k'], 'v_k_a': out['v_k_a'], 'v_r_k': out['v_r_k'], 'v_lnx_g': out['v_lnx_g'], 'v_lnx_b': out['v_lnx_b'], 'v_conv_dw': out['v_conv_dw'], 'v_conv_b': out['v_conv_b'], 'v_conv_ln_g': out['v_conv_ln_g'], 'v_conv_ln_b': out['v_conv_ln_b'], 'v_w_out': out['v_w_out'], 'v_ffn2_w_in': out['v_ffn2_w_in'], 'v_ffn2_w_out': out['v_ffn2_w_out'], 'v_ln1_g': out['v_ln1_g'], 'v_ln1_b': out['v_ln1_b'], 'v_ln2_g': out['v_ln2_g'], 'v_ln2_b': out['v_ln2_b'], 'v_ln3_g': out['v_ln3_g'], 'v_ln3_b': out['v_ln3_b']}


def _loss(weights, diff, rest, loss_target):
    with _jax.named_scope("forward"):
        args = {**rest, TWIN_DIFF_INPUT: diff, **{k: w.astype(_WEIGHT_DTYPES[k]) for k, w in weights.items()}}
        y = _forward(args)
    with _jax.named_scope("loss_head"):
        err = _jnp.square(y.astype(_jnp.float32) - loss_target)
        return 0.5 * _jnp.sum(_jnp.mean(err, axis=-1)) if err.ndim else 0.5 * err


def _adamw(w, g, m, v):
    m = ADAM_B1 * m + (1.0 - ADAM_B1) * g
    v = ADAM_B2 * v + (1.0 - ADAM_B2) * _jnp.square(g)
    m_hat = m / (1.0 - ADAM_B1 ** ADAM_STEP)
    v_hat = v / (1.0 - ADAM_B2 ** ADAM_STEP)
    delta = -ADAM_LR * (m_hat / (_jnp.sqrt(v_hat) + ADAM_EPS) + ADAM_WD * w)
    return delta, m, v


def reference(x, ffn1_w_in, ffn1_w_out, w_in, mu_prev, mu_next, w0, w2, a0, a2, g2, k_k, k_a, r_k, lnx_g, lnx_b, conv_dw, conv_b, conv_ln_g, conv_ln_b, w_out, ffn2_w_in, ffn2_w_out, ln1_g, ln1_b, ln2_g, ln2_b, ln3_g, ln3_b, loss_target, m_ffn1_w_in, m_ffn1_w_out, m_w_in, m_mu_prev, m_mu_next, m_w0, m_w2, m_a0, m_a2, m_g2, m_k_k, m_k_a, m_r_k, m_lnx_g, m_lnx_b, m_conv_dw, m_conv_b, m_conv_ln_g, m_conv_ln_b, m_w_out, m_ffn2_w_in, m_ffn2_w_out, m_ln1_g, m_ln1_b, m_ln2_g, m_ln2_b, m_ln3_g, m_ln3_b, v_ffn1_w_in, v_ffn1_w_out, v_w_in, v_mu_prev, v_mu_next, v_w0, v_w2, v_a0, v_a2, v_g2, v_k_k, v_k_a, v_r_k, v_lnx_g, v_lnx_b, v_conv_dw, v_conv_b, v_conv_ln_g, v_conv_ln_b, v_w_out, v_ffn2_w_in, v_ffn2_w_out, v_ln1_g, v_ln1_b, v_ln2_g, v_ln2_b, v_ln3_g, v_ln3_b):
    given = dict(x=x, ffn1_w_in=ffn1_w_in, ffn1_w_out=ffn1_w_out, w_in=w_in, mu_prev=mu_prev, mu_next=mu_next, w0=w0, w2=w2, a0=a0, a2=a2, g2=g2, k_k=k_k, k_a=k_a, r_k=r_k, lnx_g=lnx_g, lnx_b=lnx_b, conv_dw=conv_dw, conv_b=conv_b, conv_ln_g=conv_ln_g, conv_ln_b=conv_ln_b, w_out=w_out, ffn2_w_in=ffn2_w_in, ffn2_w_out=ffn2_w_out, ln1_g=ln1_g, ln1_b=ln1_b, ln2_g=ln2_g, ln2_b=ln2_b, ln3_g=ln3_g, ln3_b=ln3_b, loss_target=loss_target, m_ffn1_w_in=m_ffn1_w_in, m_ffn1_w_out=m_ffn1_w_out, m_w_in=m_w_in, m_mu_prev=m_mu_prev, m_mu_next=m_mu_next, m_w0=m_w0, m_w2=m_w2, m_a0=m_a0, m_a2=m_a2, m_g2=m_g2, m_k_k=m_k_k, m_k_a=m_k_a, m_r_k=m_r_k, m_lnx_g=m_lnx_g, m_lnx_b=m_lnx_b, m_conv_dw=m_conv_dw, m_conv_b=m_conv_b, m_conv_ln_g=m_conv_ln_g, m_conv_ln_b=m_conv_ln_b, m_w_out=m_w_out, m_ffn2_w_in=m_ffn2_w_in, m_ffn2_w_out=m_ffn2_w_out, m_ln1_g=m_ln1_g, m_ln1_b=m_ln1_b, m_ln2_g=m_ln2_g, m_ln2_b=m_ln2_b, m_ln3_g=m_ln3_g, m_ln3_b=m_ln3_b, v_ffn1_w_in=v_ffn1_w_in, v_ffn1_w_out=v_ffn1_w_out, v_w_in=v_w_in, v_mu_prev=v_mu_prev, v_mu_next=v_mu_next, v_w0=v_w0, v_w2=v_w2, v_a0=v_a0, v_a2=v_a2, v_g2=v_g2, v_k_k=v_k_k, v_k_a=v_k_a, v_r_k=v_r_k, v_lnx_g=v_lnx_g, v_lnx_b=v_lnx_b, v_conv_dw=v_conv_dw, v_conv_b=v_conv_b, v_conv_ln_g=v_conv_ln_g, v_conv_ln_b=v_conv_ln_b, v_w_out=v_w_out, v_ffn2_w_in=v_ffn2_w_in, v_ffn2_w_out=v_ffn2_w_out, v_ln1_g=v_ln1_g, v_ln1_b=v_ln1_b, v_ln2_g=v_ln2_g, v_ln2_b=v_ln2_b, v_ln3_g=v_ln3_g, v_ln3_b=v_ln3_b)
    weights = {n: given[n] for n in TWIN_WEIGHTS}
    shared = {n: given[n] for n in SHARED_INPUTS}
    per_example = {n: given[n] for n in ['x']}
    grad_fn = _jax.value_and_grad(_loss, argnums=(0, 1))

    def one_microbatch(ex, loss_target):
        ex = dict(ex)
        diff = ex.pop(TWIN_DIFF_INPUT)
        return grad_fn(weights, diff, {**shared, **ex}, loss_target)

    if N_MICROBATCH == 1:
        loss, (grad_w, grad_x) = one_microbatch(per_example, given["loss_target"])
    else:
        def body(carry, xs):
            loss_sum, grad_sum = carry
            l_k, (gw_k, gx_k) = one_microbatch(xs[0], xs[1])
            with _jax.named_scope("update"):
                return (loss_sum + l_k, _jax.tree.map(_jnp.add, grad_sum, gw_k)), gx_k

        init = (_jnp.zeros((), _jnp.float32), _jax.tree.map(_jnp.zeros_like, weights))
        (loss, grad_w), grad_x = _jax.lax.scan(body, init, (per_example, given["loss_target"]))
    with _jax.named_scope("update"):
        delta_w, new_m, new_v = {}, {}, {}
        for n in TWIN_WEIGHTS:
            delta_w[n], new_m[n], new_v[n] = _adamw(weights[n], grad_w[n], given["m_" + n], given["v_" + n])
    return (loss, grad_x, *[grad_w[n] for n in TWIN_WEIGHTS], *[delta_w[n] for n in TWIN_WEIGHTS],
            *[new_m[n] for n in TWIN_WEIGHTS], *[new_v[n] for n in TWIN_WEIGHTS])
```

```python
import functools

import jax
import jax.numpy as jnp
from jax import lax
from jax.experimental import pallas as pl
from jax.experimental.pallas import tpu as pltpu

F32 = jnp.float32
BF16 = jnp.bfloat16

D_MODEL = 1024
RW = 512
HEAD = 64
CW = 512
CONV_K = 31
CONV_PAD = 15
D_FF = 2816
LORA = 64
GATE_LORA = 160
GATE_PAD = 256
SHIFT_COLS = 1952
SHIFT_PAD = 2048
IN_COLS = 2976
IN_PAD = 3072
LN_EPS = 1e-5
GN_EPS = 64e-5
NORM_EPS = 1e-12
ALPHA = 2.0 ** 0.25
DECAY_SCALE = 0.6065306597126334
ADAM_LR, ADAM_B1, ADAM_B2, ADAM_EPS, ADAM_WD, ADAM_STEP = 0.001, 0.9, 0.999, 1e-08, 0.01, 10
N_CHIPS = 4
VMEM_LIMIT = 56 * 1024 * 1024

MESH = pl.DeviceIdType.MESH


def _cparams(sem=None, **kw):
    return pltpu.CompilerParams(dimension_semantics=sem, vmem_limit_bytes=VMEM_LIMIT, **kw)


LANES = 128


def _pick_tile(dim, want):
    for t in range(min(want, dim) // LANES * LANES, 0, -LANES):
        if dim % t == 0:
            return t
    return dim


def _matmul(a, b, *, ta=False, tb=False, out_dtype=F32, tm=512, tn=512, tk=512, scale=1.0, name):
    if ta:
        k_dim, m_dim = a.shape
    else:
        m_dim, k_dim = a.shape
    n_dim = b.shape[0] if tb else b.shape[1]
    tm, tn, tk = _pick_tile(m_dim, tm), _pick_tile(n_dim, tn), _pick_tile(k_dim, tk)
    assert m_dim % tm == 0 and n_dim % tn == 0 and k_dim % tk == 0, (name, a.shape, b.shape, tm, tn, tk)
    nk = k_dim // tk
    dims = (((0,) if ta else (1,), (1,) if tb else (0,)), ((), ()))

    def body(a_ref, b_ref, o_ref, acc_ref):
        kk = pl.program_id(2)

        @pl.when(kk == 0)
        def _():
            acc_ref[...] = jnp.zeros_like(acc_ref)

        acc_ref[...] += lax.dot_general(a_ref[...].astype(BF16), b_ref[...].astype(BF16), dims,
                                        preferred_element_type=F32)

        @pl.when(kk == nk - 1)
        def _():
            o_ref[...] = (acc_ref[...] * scale).astype(o_ref.dtype)

    a_spec = pl.BlockSpec((tk, tm), lambda i, j, k: (k, i)) if ta else pl.BlockSpec((tm, tk), lambda i, j, k: (i, k))
    b_spec = pl.BlockSpec((tn, tk), lambda i, j, k: (j, k)) if tb else pl.BlockSpec((tk, tn), lambda i, j, k: (k, j))
    return pl.pallas_call(
        body, name=name,
        out_shape=jax.ShapeDtypeStruct((m_dim, n_dim), out_dtype),
        grid=(m_dim // tm, n_dim // tn, nk),
        in_specs=[a_spec, b_spec],
        out_specs=pl.BlockSpec((tm, tn), lambda i, j, k: (i, j)),
        scratch_shapes=[pltpu.VMEM((tm, tn), F32)],
        compiler_params=_cparams(("parallel", "parallel", "arbitrary")),
    )(a, b)


def _rowcall(fn, tok_in, full_in, tok_out, acc_out, *, tt, name):
    n_tok = tok_in[0].shape[0]
    assert n_tok % tt == 0, (name, n_tok, tt)
    n_ti, n_fi, n_to = len(tok_in), len(full_in), len(tok_out)

    def body(*refs):
        i = pl.program_id(0)
        ins = [r[...] for r in refs[:n_ti + n_fi]]
        outs = fn(i, *ins)
        o_refs = refs[n_ti + n_fi:]
        for r, val in zip(o_refs[:n_to], outs[:n_to]):
            r[...] = val.astype(r.dtype)
        if acc_out:
            @pl.when(i == 0)
            def _():
                for r in o_refs[n_to:]:
                    r[...] = jnp.zeros_like(r)
            for r, val in zip(o_refs[n_to:], outs[n_to:]):
                r[...] += val.reshape(r.shape).astype(F32)

    in_specs = [pl.BlockSpec((tt, a.shape[1]), lambda i: (i, 0)) for a in tok_in]
    in_specs += [pl.BlockSpec(a.shape, lambda i: (0, 0)) for a in full_in]
    out_specs = [pl.BlockSpec((tt, c), lambda i: (i, 0)) for c, _ in tok_out]
    out_specs += [pl.BlockSpec(s, lambda i: (0, 0)) for s in acc_out]
    out_shape = [jax.ShapeDtypeStruct((n_tok, c), dt) for c, dt in tok_out]
    out_shape += [jax.ShapeDtypeStruct(s, F32) for s in acc_out]
    return pl.pallas_call(
        body, name=name, out_shape=out_shape, grid=(n_tok // tt,), in_specs=in_specs, out_specs=out_specs,
        compiler_params=_cparams(("arbitrary",) if acc_out else ("parallel",)),
    )(*tok_in, *full_in)


@jax.custom_vjp
def _bdot(a, b):
    return jnp.dot(a.astype(BF16), b.astype(BF16), preferred_element_type=F32)


def _bdot_fwd(a, b):
    return _bdot(a, b), (a, b)


def _bdot_bwd(res, g):
    a, b = res
    g16 = g.astype(BF16)
    da = lax.dot_general(g16, b.astype(BF16), (((1,), (1,)), ((), ())), preferred_element_type=F32)
    db = lax.dot_general(a.astype(BF16), g16, (((0,), (0,)), ((), ())), preferred_element_type=F32)
    return da, db


_bdot.defvjp(_bdot_fwd, _bdot_bwd)


def _split16(x):
    hi = x.astype(BF16)
    lo = (x - hi.astype(F32)).astype(BF16)
    return hi, lo


def _segsum_raw(x, e2):
    hi, lo = _split16(x)
    outs = []
    for c in range(x.shape[1] // 256):
        lhs = jnp.concatenate([hi[:, 256 * c:256 * (c + 1)], lo[:, 256 * c:256 * (c + 1)]], axis=1)
        outs.append(jnp.dot(lhs, e2, preferred_element_type=F32))
    return jnp.concatenate(outs, axis=1)


@jax.custom_vjp
def _segsum(x, e2):
    return _segsum_raw(x, e2)


def _segsum_fwd(x, e2):
    return _segsum_raw(x, e2), e2


def _segsum_bwd(e2, g):
    return _segsum_raw(g, e2), jnp.zeros_like(e2)


_segsum.defvjp(_segsum_fwd, _segsum_bwd)


def _head_ones():
    r = lax.broadcasted_iota(jnp.int32, (512, 256), 0) % 256
    c = lax.broadcasted_iota(jnp.int32, (512, 256), 1)
    return (r // HEAD == c // HEAD).astype(BF16)


def _sigmoid(x):
    return 1.0 / (1.0 + jnp.exp(-x))


def _silu(x):
    return x * _sigmoid(x)


def _layer_norm(z, g, b, eps=LN_EPS):
    mu = jnp.mean(z, axis=-1, keepdims=True)
    zc = z - mu
    var = jnp.mean(zc * zc, axis=-1, keepdims=True)
    return zc * lax.rsqrt(var + eps) * g + b


def _swiglu(h):
    return _silu(h[:, :D_FF]) * h[:, D_FF:]


def _prep(ps, w2b, w0c, a2b, a0c, g2p, k_k, k_a, e2):
    r, k, v = ps[:, 0:512], ps[:, 512:1024], ps[:, 1024:1536]
    wd, ad, gd = ps[:, 1536:1664], ps[:, 1664:1792], ps[:, 1792:2048]
    lw = _bdot(jnp.tanh(wd), w2b) + w0c
    decay = jnp.exp(-DECAY_SCALE * _sigmoid(lw))
    a = _sigmoid(_bdot(ad, a2b) + a0c)
    g = _bdot(_sigmoid(gd), g2p)
    kkr = k * k_k
    nrm = jnp.sqrt(_segsum(kkr * kkr, e2))
    kk = kkr / jnp.maximum(nrm, NORM_EPS)
    k2 = jnp.concatenate([k, k], axis=1)
    ka2 = jnp.concatenate([k_a, k_a], axis=1)
    kd = k2 * (1.0 + (a - 1.0) * ka2)
    b = jnp.concatenate([kk, kk], axis=1) * a
    return r, v, kk, decay, kd, b, g


def _post(y0, y1, r, v, kd, g, lnx_g, lnx_b, r_k, e2):
    y = y0 + y1
    mu = _segsum(y, e2) * (1.0 / HEAD)
    yc = y - mu
    var = _segsum(yc * yc, e2) * (1.0 / HEAD)
    yn = yc * lax.rsqrt(var + GN_EPS) * lnx_g + lnx_b
    bonus = _segsum(r * (kd[:, :RW] + kd[:, RW:]) * r_k, e2)
    return (yn + bonus * v) * g


def _conv_post(yc, ln_g, ln_b):
    return _silu(_layer_norm(yc, ln_g, ln_b))


def _swiglu_fwd(h, *, tt, name):
    return _rowcall(lambda i, hv: (_swiglu(hv),), [h], [], [(D_FF, BF16)], [], tt=tt, name=name)[0]


def _swiglu_bwd(h, dact, *, tt, name):
    def fn(i, hv, dv):
        _, vjp = jax.vjp(_swiglu, hv)
        return (vjp(dv.astype(F32))[0],)
    return _rowcall(fn, [h, dact], [], [(2 * D_FF, BF16)], [], tt=tt, name=name)[0]


def _ln_fwd(x, f, g, b, fscale, *, tt, name):
    def fn(i, xv, fv, gv, bv):
        z = ALPHA * xv + fscale * fv
        return z, _layer_norm(z, gv, bv)
    return _rowcall(fn, [x, f], [g, b], [(D_MODEL, F32), (D_MODEL, F32)], [], tt=tt, name=name)


def _ln_bwd(z, dy, g, b, *, tt, name):
    def fn(i, zv, dv, gv, bv):
        _, vjp = jax.vjp(_layer_norm, zv, gv, bv)
        return vjp(dv)
    return _rowcall(fn, [z, dy], [g, b], [(D_MODEL, F32)], [(1, D_MODEL), (1, D_MODEL)], tt=tt, name=name)


def _add_scaled(a, b, sa, sb, *, tt, name, out_dtype=F32):
    return _rowcall(lambda i, av, bv: (sa * av.astype(F32) + sb * bv.astype(F32),), [a, b], [],
                    [(a.shape[1], out_dtype)], [], tt=tt, name=name)[0]


def _loss_fwd_bwd(y, target, *, tt, name):
    def fn(i, yv, tv):
        e = yv - tv
        part = 0.5 * jnp.sum(jnp.mean(e * e, axis=-1, keepdims=True), axis=0, keepdims=True)
        return e * (1.0 / D_MODEL), jnp.broadcast_to(part, (8, 128))
    return _rowcall(fn, [y, target], [], [(D_MODEL, F32)], [(8, 128)], tt=tt, name=name)


def _halo_specs(cols_block, hb, tt, n_tok, col_idx):
    nb = n_tok // hb
    prev = pl.BlockSpec((hb, cols_block), lambda i: (jnp.maximum(i * (tt // hb) - 1, 0), col_idx))
    nxt = pl.BlockSpec((hb, cols_block), lambda i: (jnp.minimum((i + 1) * (tt // hb), nb - 1), col_idx))
    return prev, nxt


def _mix_prep(p, mu_p, mu_n, w2b, w0c, a2b, a0c, g2p, k_k, k_a, *, seq, tt, name):
    n_tok = p.shape[0]
    tps = seq // tt
    e2 = _head_ones()

    def body(p_ref, hp_ref, hn_ref, mup_ref, mun_ref, w2b_ref, w0c_ref, a2b_ref, a0c_ref, g2p_ref, kk_ref, ka_ref,
             e2_ref, r_o, v_o, kk_o, w_o, kd_o, b_o, g_o, ext):
        i = pl.program_id(0)
        first = (i % tps) == 0
        last = (i % tps) == tps - 1
        pv = p_ref[...]
        ext[pl.ds(0, 8), :] = jnp.where(first, 0.0, hp_ref[...])
        ext[pl.ds(8, tt), :] = pv
        ext[pl.ds(8 + tt, 8), :] = jnp.where(last, 0.0, hn_ref[...])
        prev = ext[pl.ds(7, tt), :]
        nxt = ext[pl.ds(9, tt), :]
        ps = pv + mup_ref[...] * (prev - pv) + mun_ref[...] * (nxt - pv)
        outs = _prep(ps, w2b_ref[...], w0c_ref[...], a2b_ref[...], a0c_ref[...], g2p_ref[...], kk_ref[...],
                     ka_ref[...], e2_ref[...])
        for o_ref, val in zip((r_o, v_o, kk_o, w_o, kd_o, b_o, g_o), outs):
            o_ref[...] = val

    hp, hn = _halo_specs(SHIFT_PAD, 8, tt, n_tok, 0)
    fulls = [mu_p, mu_n, w2b, w0c, a2b, a0c, g2p, k_k, k_a, e2]
    widths = (RW, RW, RW, 2 * RW, 2 * RW, 2 * RW, RW)
    return pl.pallas_call(
        body, name=name,
        out_shape=[jax.ShapeDtypeStruct((n_tok, c), F32) for c in widths],
        grid=(n_tok // tt,),
        in_specs=[pl.BlockSpec((tt, SHIFT_PAD), lambda i: (i, 0)), hp, hn]
        + [pl.BlockSpec(a.shape, lambda i: (0, 0)) for a in fulls],
        out_specs=[pl.BlockSpec((tt, c), lambda i: (i, 0)) for c in widths],
        scratch_shapes=[pltpu.VMEM((tt + 16, SHIFT_PAD), F32)],
        compiler_params=_cparams(("parallel",)),
    )(p, p, p, *fulls)


def _mix_prep_bwd(p, mu_p, mu_n, w2b, w0c, a2b, a0c, g2p, k_k, k_a, cts, *, seq, tt, name):
    n_tok = p.shape[0]
    tps = seq // tt
    e2 = _head_ones()
    acc_shapes = [w2b.shape, w0c.shape, a2b.shape, a0c.shape, g2p.shape, k_k.shape, k_a.shape]

    def body(p_ref, hp_ref, hn_ref, mup_ref, mun_ref, w2b_ref, w0c_ref, a2b_ref, a0c_ref, g2p_ref, kk_ref, ka_ref,
             e2_ref, dr, dv, dkk, dw, dkd, db, dg, dps_o, *rest):
        acc_refs, ext = rest[:-1], rest[-1]
        i = pl.program_id(0)
        first = (i % tps) == 0
        last = (i % tps) == tps - 1
        pv = p_ref[...]
        ext[pl.ds(0, 8), :] = jnp.where(first, 0.0, hp_ref[...])
        ext[pl.ds(8, tt), :] = pv
        ext[pl.ds(8 + tt, 8), :] = jnp.where(last, 0.0, hn_ref[...])
        prev = ext[pl.ds(7, tt), :]
        nxt = ext[pl.ds(9, tt), :]
        ps = pv + mup_ref[...] * (prev - pv) + mun_ref[...] * (nxt - pv)
        e2v = e2_ref[...]
        _, vjp = jax.vjp(lambda *a: _prep(*a, e2v), ps, w2b_ref[...], w0c_ref[...], a2b_ref[...], a0c_ref[...],
                         g2p_ref[...], kk_ref[...], ka_ref[...])
        grads = vjp((dr[...], dv[...], dkk[...], dw[...], dkd[...], db[...], dg[...]))
        dps_o[...] = grads[0]

        @pl.when(i == 0)
        def _():
            for r in acc_refs:
                r[...] = jnp.zeros_like(r)
        for r, val in zip(acc_refs, grads[1:]):
            r[...] += val

    hp, hn = _halo_specs(SHIFT_PAD, 8, tt, n_tok, 0)
    fulls = [mu_p, mu_n, w2b, w0c, a2b, a0c, g2p, k_k, k_a, e2]
    return pl.pallas_call(
        body, name=name,
        out_shape=[jax.ShapeDtypeStruct((n_tok, SHIFT_PAD), F32)] + [jax.ShapeDtypeStruct(s, F32) for s in acc_shapes],
        grid=(n_tok // tt,),
        in_specs=[pl.BlockSpec((tt, SHIFT_PAD), lambda i: (i, 0)), hp, hn]
        + [pl.BlockSpec(a.shape, lambda i: (0, 0)) for a in fulls]
        + [pl.BlockSpec((tt, c.shape[1]), lambda i: (i, 0)) for c in cts],
        out_specs=[pl.BlockSpec((tt, SHIFT_PAD), lambda i: (i, 0))] + [pl.BlockSpec(s, lambda i: (0, 0)) for s in acc_shapes],
        scratch_shapes=[pltpu.VMEM((tt + 16, SHIFT_PAD), F32)],
        compiler_params=_cparams(("arbitrary",)),
    )(p, p, p, *fulls, *cts)


def _shift_bwd(dps, p, mu_p, mu_n, *, seq, tt, name):
    n_tok = p.shape[0]
    tps = seq // tt

    def body(d_ref, dhp_ref, dhn_ref, p_ref, php_ref, phn_ref, mup_ref, mun_ref, dp_o, dmup_o, dmun_o, ext):
        i = pl.program_id(0)
        first = (i % tps) == 0
        last = (i % tps) == tps - 1
        mup, mun = mup_ref[...], mun_ref[...]
        dv = d_ref[...]
        pv = p_ref[...]
        ext[pl.ds(0, 8), :] = jnp.where(first, 0.0, dhp_ref[...])
        ext[pl.ds(8, tt), :] = dv
        ext[pl.ds(8 + tt, 8), :] = jnp.where(last, 0.0, dhn_ref[...])
        d_prev = ext[pl.ds(7, tt), :]
        d_next = ext[pl.ds(9, tt), :]
        dp_o[...] = (dv * (1.0 - mup - mun) + d_next * mup + d_prev * mun).astype(dp_o.dtype)
        ext[pl.ds(0, 8), :] = jnp.where(first, 0.0, php_ref[...])
        ext[pl.ds(8, tt), :] = pv
        ext[pl.ds(8 + tt, 8), :] = jnp.where(last, 0.0, phn_ref[...])
        p_prev = ext[pl.ds(7, tt), :]
        p_next = ext[pl.ds(9, tt), :]

        @pl.when(i == 0)
        def _():
            dmup_o[...] = jnp.zeros_like(dmup_o)
            dmun_o[...] = jnp.zeros_like(dmun_o)
        dmup_o[...] += jnp.sum(dv * (p_prev - pv), axis=0, keepdims=True)
        dmun_o[...] += jnp.sum(dv * (p_next - pv), axis=0, keepdims=True)

    hp, hn = _halo_specs(SHIFT_PAD, 8, tt, n_tok, 0)
    tile = pl.BlockSpec((tt, SHIFT_PAD), lambda i: (i, 0))
    full = pl.BlockSpec((1, SHIFT_PAD), lambda i: (0, 0))
    return pl.pallas_call(
        body, name=name,
        out_shape=[jax.ShapeDtypeStruct((n_tok, SHIFT_PAD), BF16), jax.ShapeDtypeStruct((1, SHIFT_PAD), F32),
                   jax.ShapeDtypeStruct((1, SHIFT_PAD), F32)],
        grid=(n_tok // tt,),
        in_specs=[tile, hp, hn, tile, hp, hn, full, full],
        out_specs=[tile, full, full],
        scratch_shapes=[pltpu.VMEM((tt + 16, SHIFT_PAD), F32)],
        compiler_params=_cparams(("arbitrary",)),
    )(dps, dps, dps, p, p, p, mu_p, mu_n)


def _mix_post(y0, y1, r, v, kd, g, lnx_g, lnx_b, r_k, *, tt, name):
    e2 = _head_ones()
    return _rowcall(lambda i, *a: (_post(*a),), [y0, y1, r, v, kd, g], [lnx_g, lnx_b, r_k, e2], [(RW, BF16)], [],
                    tt=tt, name=name)[0]


def _mix_post_bwd(y0, y1, r, v, kd, g, lnx_g, lnx_b, r_k, dout, *, tt, name):
    e2 = _head_ones()

    def fn(i, y0v, y1v, rv, vv, kdv, gv, dov, lg, lb, rk, e2v):
        _, vjp = jax.vjp(lambda *a: _post(*a, e2v), y0v, y1v, rv, vv, kdv, gv, lg, lb, rk)
        gr = vjp(dov.astype(F32))
        return gr[0], gr[2], gr[3], gr[4], gr[5], gr[6], gr[7], gr[8]
    return _rowcall(fn, [y0, y1, r, v, kd, g, dout], [lnx_g, lnx_b, r_k, e2],
                    [(RW, F32), (RW, F32), (RW, F32), (2 * RW, F32), (RW, F32)], [(1, RW), (1, RW), (1, RW)],
                    tt=tt, name=name)


def _conv_fwd(p, dw, db, ln_g, ln_b, *, seq, tt, name):
    n_tok = p.shape[0]
    tps = seq // tt

    def glu(x, gate):
        return x * _sigmoid(gate)

    def body(u_ref, g_ref, uhp, ghp, uhn, ghn, dw_ref, db_ref, lg_ref, lb_ref, yc_o, y_o, ext):
        i = pl.program_id(0)
        first = (i % tps) == 0
        last = (i % tps) == tps - 1
        ext[pl.ds(0, 16), :] = jnp.where(first, 0.0, glu(uhp[...], ghp[...]))
        ext[pl.ds(16, tt), :] = glu(u_ref[...], g_ref[...])
        ext[pl.ds(16 + tt, 16), :] = jnp.where(last, 0.0, glu(uhn[...], ghn[...]))
        acc = jnp.zeros((tt, CW), F32) + db_ref[...]
        for k in range(CONV_K):
            acc = acc + ext[pl.ds(k + 1, tt), :] * dw_ref[pl.ds(k, 1), :]
        yc_o[...] = acc
        y_o[...] = _conv_post(acc, lg_ref[...], lb_ref[...]).astype(y_o.dtype)

    uhp_s, uhn_s = _halo_specs(CW, 16, tt, n_tok, 4)
    ghp_s, ghn_s = _halo_specs(CW, 16, tt, n_tok, 5)
    fulls = [dw, db, ln_g, ln_b]
    return pl.pallas_call(
        body, name=name,
        out_shape=[jax.ShapeDtypeStruct((n_tok, CW), F32), jax.ShapeDtypeStruct((n_tok, CW), BF16)],
        grid=(n_tok // tt,),
        in_specs=[pl.BlockSpec((tt, CW), lambda i: (i, 4)), pl.BlockSpec((tt, CW), lambda i: (i, 5)),
                  uhp_s, ghp_s, uhn_s, ghn_s] + [pl.BlockSpec(a.shape, lambda i: (0, 0)) for a in fulls],
        out_specs=[pl.BlockSpec((tt, CW), lambda i: (i, 0)), pl.BlockSpec((tt, CW), lambda i: (i, 0))],
        scratch_shapes=[pltpu.VMEM((tt + 32, CW), F32)],
        compiler_params=_cparams(("parallel",)),
    )(p, p, p, p, p, p, *fulls)


def _conv_post_bwd(yc, dy, ln_g, ln_b, *, tt, name):
    def fn(i, ycv, dyv, lg, lb):
        _, vjp = jax.vjp(_conv_post, ycv, lg, lb)
        dyc, dg, dbb = vjp(dyv.astype(F32))
        return dyc, dg, dbb, jnp.sum(dyc, axis=0, keepdims=True)
    return _rowcall(fn, [yc, dy], [ln_g, ln_b], [(CW, F32)], [(1, CW), (1, CW), (1, CW)], tt=tt, name=name)


def _conv_bwd(dyc, p, dw, *, seq, tt, name):
    n_tok = p.shape[0]
    tps = seq // tt

    def body(d_ref, dhp, dhn, u_ref, g_ref, uhp, ghp, uhn, ghn, dw_ref, dp_o, ddw_o, ext):
        i = pl.program_id(0)
        first = (i % tps) == 0
        last = (i % tps) == tps - 1
        dv = d_ref[...]
        ext[pl.ds(0, 16), :] = jnp.where(first, 0.0, dhp[...])
        ext[pl.ds(16, tt), :] = dv
        ext[pl.ds(16 + tt, 16), :] = jnp.where(last, 0.0, dhn[...])
        du = jnp.zeros((tt, CW), F32)
        for k in range(CONV_K):
            du = du + ext[pl.ds(31 - k, tt), :] * dw_ref[pl.ds(k, 1), :]
        uv, gv = u_ref[...], g_ref[...]
        sg = _sigmoid(gv)
        dp_o[:, 0:CW] = (du * sg).astype(dp_o.dtype)
        dp_o[:, CW:2 * CW] = (du * uv * sg * (1.0 - sg)).astype(dp_o.dtype)
        ext[pl.ds(0, 16), :] = jnp.where(first, 0.0, uhp[...] * _sigmoid(ghp[...]))
        ext[pl.ds(16, tt), :] = uv * sg
        ext[pl.ds(16 + tt, 16), :] = jnp.where(last, 0.0, uhn[...] * _sigmoid(ghn[...]))

        @pl.when(i == 0)
        def _():
            ddw_o[...] = jnp.zeros_like(ddw_o)
        for k in range(CONV_K):
            ddw_o[pl.ds(k, 1), :] += jnp.sum(dv * ext[pl.ds(k + 1, tt), :], axis=0, keepdims=True)

    dhp_s, dhn_s = _halo_specs(CW, 16, tt, n_tok, 0)
    uhp_s, uhn_s = _halo_specs(CW, 16, tt, n_tok, 4)
    ghp_s, ghn_s = _halo_specs(CW, 16, tt, n_tok, 5)
    return pl.pallas_call(
        body, name=name,
        out_shape=[jax.ShapeDtypeStruct((n_tok, 2 * CW), BF16), jax.ShapeDtypeStruct((32, CW), F32)],
        grid=(n_tok // tt,),
        in_specs=[pl.BlockSpec((tt, CW), lambda i: (i, 0)), dhp_s, dhn_s,
                  pl.BlockSpec((tt, CW), lambda i: (i, 4)), pl.BlockSpec((tt, CW), lambda i: (i, 5)),
                  uhp_s, ghp_s, uhn_s, ghn_s, pl.BlockSpec(dw.shape, lambda i: (0, 0))],
        out_specs=[pl.BlockSpec((tt, 2 * CW), lambda i: (i, 0)), pl.BlockSpec((32, CW), lambda i: (0, 0))],
        scratch_shapes=[pltpu.VMEM((tt + 32, CW), F32)],
        compiler_params=_cparams(("arbitrary",)),
    )(dyc, dyc, dyc, p, p, p, p, p, p, dw)


def _segdot(hi, lo, e2):
    outs = []
    for c in range(hi.shape[1] // 256):
        lhs = jnp.concatenate([hi[:, 256 * c:256 * (c + 1)], lo[:, 256 * c:256 * (c + 1)]], axis=1)
        outs.append(jnp.dot(lhs, e2, preferred_element_type=F32))
    return jnp.concatenate(outs, axis=1)


def _seg(x, e2):
    hi, lo = _split16(x)
    return _segdot(hi, lo, e2)


def _diag_mask():
    return lax.broadcasted_iota(jnp.int32, (HEAD, RW), 0) == lax.broadcasted_iota(jnp.int32, (HEAD, RW), 1) % HEAD


def _bcast_rows(rows):
    return jnp.concatenate([jnp.broadcast_to(x, (HEAD, RW)) for x in rows], axis=0)


def _col_form(rows, dmask, e2):
    his, los = [], []
    for x in rows:
        hi = x.astype(BF16).astype(F32)
        lo = x - hi
        his.append(jnp.where(dmask, jnp.broadcast_to(hi, (HEAD, RW)), 0.0).astype(BF16))
        los.append(jnp.where(dmask, jnp.broadcast_to(lo, (HEAD, RW)), 0.0).astype(BF16))
    return _segdot(jnp.concatenate(his, axis=0), jnp.concatenate(los, axis=0), e2)


def _row_form(col, s, dmask):
    return jnp.sum(jnp.where(dmask, col[s * HEAD:(s + 1) * HEAD], 0.0), axis=0, keepdims=True)


def _row_sum(x, s):
    return jnp.sum(x[s * HEAD:(s + 1) * HEAD], axis=0, keepdims=True)


def _wkv_fwd(r, v, kk, w, kd, b, *, tb, name):
    bsz, seq, _ = r.shape
    nb = seq // tb
    ns = 2 * bsz
    e2 = _head_ones()

    def body(r0, r1, v0, v1, k0, k1, w0, w1, kd0, kd1, b0, b1, e2_ref, y0_o, y1_o, sp_o, s_ref):
        i = pl.program_id(0)

        @pl.when(i == 0)
        def _():
            s_ref[...] = jnp.zeros_like(s_ref)

        e2v = e2_ref[...]
        dmask = _diag_mask()
        y_refs = (y0_o, y1_o)

        def step(j, carry):
            tl = (j, tb - 1 - j)

            def rows(refs):
                return [refs[d][bb, pl.ds(tl[d], 1), :] for d in (0, 1) for bb in range(bsz)]

            s_old = s_ref[...]
            for s in range(ns):
                sp_o[s, pl.ds(j, 1), :, :] = s_old[s * HEAD:(s + 1) * HEAD].reshape(1, HEAD, RW)
            sa = -_seg(s_old * _bcast_rows(rows((k0, k1))), e2v)
            vc = _col_form(rows((v0, v1)), dmask, e2v)
            s_new = (s_old * _bcast_rows(rows((w0, w1))) + sa * _bcast_rows(rows((b0, b1)))
                     + vc * _bcast_rows(rows((kd0, kd1))))
            ycol = _seg(s_new * _bcast_rows(rows((r0, r1))), e2v)
            for d in (0, 1):
                for bb in range(bsz):
                    y_refs[d][bb, pl.ds(tl[d], 1), :] = _row_form(ycol, d * bsz + bb, dmask)
            s_ref[...] = s_new
            return carry

        lax.fori_loop(0, tb, step, 0)

    def blk(width_idx, rev):
        if rev:
            return pl.BlockSpec((bsz, tb, RW), lambda i: (0, nb - 1 - i, width_idx))
        return pl.BlockSpec((bsz, tb, RW), lambda i: (0, i, width_idx))

    in_specs = [blk(0, False), blk(0, True)] * 3 + [blk(0, False), blk(1, True)] * 3
    in_specs.append(pl.BlockSpec(e2.shape, lambda i: (0, 0)))
    return pl.pallas_call(
        body, name=name,
        out_shape=[jax.ShapeDtypeStruct((bsz, seq, RW), F32), jax.ShapeDtypeStruct((bsz, seq, RW), F32),
                   jax.ShapeDtypeStruct((ns, seq, HEAD, RW), F32)],
        grid=(nb,),
        in_specs=in_specs,
        out_specs=[blk(0, False), blk(0, True), pl.BlockSpec((ns, tb, HEAD, RW), lambda i: (0, i, 0, 0))],
        scratch_shapes=[pltpu.VMEM((ns * HEAD, RW), F32)],
        compiler_params=_cparams(("arbitrary",)),
    )(r, r, v, v, kk, kk, w, w, kd, kd, b, b, e2)


def _wkv_bwd(r, v, kk, w, kd, b, dy, sp, *, tb, name):
    bsz, seq, _ = r.shape
    nb = seq // tb
    ns = 2 * bsz
    e2 = _head_ones()

    def body(r0, r1, v0, v1, k0, k1, dy0, dy1, w0, w1, kd0, kd1, b0, b1, sp_ref, e2_ref, *rest):
        outs, g_ref = rest[:-1], rest[-1]
        i = pl.program_id(0)

        @pl.when(i == 0)
        def _():
            g_ref[...] = jnp.zeros_like(g_ref)

        e2v = e2_ref[...]
        dmask = _diag_mask()

        def step(jj, carry):
            sl = tb - 1 - jj
            tl = (sl, jj)

            def rows(refs):
                return [refs[d][bb, pl.ds(tl[d], 1), :] for d in (0, 1) for bb in range(bsz)]

            s_old = jnp.concatenate([sp_ref[s, pl.ds(sl, 1), :, :].reshape(HEAD, RW) for s in range(ns)], axis=0)
            neg_kk = -_bcast_rows(rows((k0, k1)))
            wm = _bcast_rows(rows((w0, w1)))
            km = _bcast_rows(rows((kd0, kd1)))
            bm = _bcast_rows(rows((b0, b1)))
            rm = _bcast_rows(rows((r0, r1)))
            sa = _seg(s_old * neg_kk, e2v)
            vc = _col_form(rows((v0, v1)), dmask, e2v)
            dyc = _col_form(rows((dy0, dy1)), dmask, e2v)
            s_new = s_old * wm + sa * bm + vc * km
            gt = g_ref[...] + dyc * rm
            gb = _seg(gt * bm, e2v)
            dvc = _seg(gt * km, e2v)
            vals = (s_new * dyc, None, s_old * gb, s_old * gt, gt * vc, sa * gt)
            for d in (0, 1):
                for bb in range(bsz):
                    s = d * bsz + bb
                    at = (bb, pl.ds(tl[d], 1), slice(None))
                    outs[0 + d][at] = _row_sum(vals[0], s)
                    outs[2 + d][at] = _row_form(dvc, s, dmask)
                    outs[4 + d][at] = -_row_sum(vals[2], s)
                    outs[6 + d][at] = _row_sum(vals[3], s)
                    outs[8 + d][at] = _row_sum(vals[4], s)
                    outs[10 + d][at] = _row_sum(vals[5], s)
            g_ref[...] = gt * wm + gb * neg_kk
            return carry

        lax.fori_loop(0, tb, step, 0)

    def blk(width_idx, rev):
        if rev:
            return pl.BlockSpec((bsz, tb, RW), lambda i: (0, nb - 1 - i, width_idx))
        return pl.BlockSpec((bsz, tb, RW), lambda i: (0, i, width_idx))

    in_specs = [blk(0, True), blk(0, False)] * 4 + [blk(0, True), blk(1, False)] * 3
    in_specs.append(pl.BlockSpec((ns, tb, HEAD, RW), lambda i: (0, nb - 1 - i, 0, 0)))
    in_specs.append(pl.BlockSpec(e2.shape, lambda i: (0, 0)))
    return pl.pallas_call(
        body, name=name,
        out_shape=[jax.ShapeDtypeStruct((bsz, seq, RW), F32)] * 12,
        grid=(nb,),
        in_specs=in_specs,
        out_specs=[blk(0, True), blk(0, False)] * 6,
        scratch_shapes=[pltpu.VMEM((ns * HEAD, RW), F32)],
        compiler_params=_cparams(("arbitrary",)),
    )(r, r, v, v, kk, kk, dy, dy, w, w, kd, kd, b, b, sp, e2)


def _scan_cotangents(post_g, scan_g, *, tt, name):
    dr_p, dv_p, dkd_p = post_g
    cat = functools.partial(jnp.concatenate, axis=1)

    def fn(i, drp, dvp, dkdp, dr0, dr1, dv0, dv1, dk0, dk1, dw0, dw1, dkd0, dkd1, db0, db1):
        return (drp + dr0 + dr1, dvp + dv0 + dv1, dk0 + dk1, cat([dw0, dw1]), dkdp + cat([dkd0, dkd1]), cat([db0, db1]))
    return _rowcall(fn, [dr_p, dv_p, dkd_p, *scan_g], [],
                    [(RW, F32), (RW, F32), (RW, F32), (2 * RW, F32), (2 * RW, F32), (2 * RW, F32)], [], tt=tt, name=name)


def _block_diag2(w):
    z = jnp.zeros_like(w[0])
    return jnp.concatenate([jnp.concatenate([w[0], z], axis=1), jnp.concatenate([z, w[1]], axis=1)], axis=0)


def _pad_in_cols(a):
    z = jnp.zeros(a.shape[:-1] + (SHIFT_PAD - SHIFT_COLS,), a.dtype)
    return jnp.concatenate([a[..., :SHIFT_COLS], z, a[..., SHIFT_COLS:]], axis=-1)


def _unpad_in_cols(a):
    return jnp.concatenate([a[..., :SHIFT_COLS], a[..., SHIFT_PAD:]], axis=-1)


def _local_step(x, target, wts, *, tt, tb):
    bsz, seq, _ = x.shape
    n_tok = bsz * seq
    row = lambda a: a.reshape(1, -1).astype(F32)
    x0 = x.reshape(n_tok, D_MODEL)
    tgt = target.reshape(n_tok, D_MODEL)
    w1i, w1o, wout, w2i, w2o = wts["ffn1_w_in"], wts["ffn1_w_out"], wts["w_out"], wts["ffn2_w_in"], wts["ffn2_w_out"]
    win = _pad_in_cols(wts["w_in"])
    zpad = jnp.zeros((1, SHIFT_PAD - SHIFT_COLS), F32)
    mu_p = jnp.concatenate([row(wts["mu_prev"]), zpad], axis=1)
    mu_n = jnp.concatenate([row(wts["mu_next"]), zpad], axis=1)
    w2b, a2b = _block_diag2(wts["w2"]), _block_diag2(wts["a2"])
    w0c, a0c = row(wts["w0"]), row(wts["a0"])
    g2p = jnp.concatenate([wts["g2"], jnp.zeros((GATE_PAD - GATE_LORA, RW), F32)], axis=0)
    k_k, k_a, r_k = row(wts["k_k"]), row(wts["k_a"]), row(wts["r_k"])
    lnx_g, lnx_b = row(wts["lnx_g"]), row(wts["lnx_b"])
    cdw, cb, clg, clb = wts["conv_dw"], row(wts["conv_b"]), row(wts["conv_ln_g"]), row(wts["conv_ln_b"])
    ln = {k: row(wts[k]) for k in ("ln1_g", "ln1_b", "ln2_g", "ln2_b", "ln3_g", "ln3_b")}
    small = (mu_p, mu_n, w2b, w0c, a2b, a0c, g2p, k_k, k_a)
    seq3 = lambda a: a.reshape(bsz, seq, a.shape[-1])
    flat = lambda a: a.reshape(n_tok, a.shape[-1])

    h1 = _matmul(x0, w1i, name="ffn1_in")
    act1 = _swiglu_fwd(h1, tt=tt, name="ffn1_act")
    f1 = _matmul(act1, w1o, name="ffn1_out")
    z1, x1 = _ln_fwd(x0, f1, ln["ln1_g"], ln["ln1_b"], 0.5, tt=tt, name="ln1")
    p = _matmul(x1, win, name="proj_in")
    r, v, kk, w, kd, b, g = _mix_prep(p, *small, seq=seq, tt=tt, name="mix_prep")
    y0, y1, sp = _wkv_fwd(seq3(r), seq3(v), seq3(kk), seq3(w), seq3(kd), seq3(b), tb=tb, name="wkv_fwd")
    y0, y1 = flat(y0), flat(y1)
    yr = _mix_post(y0, y1, r, v, kd, g, lnx_g, lnx_b, r_k, tt=tt, name="mix_post")
    yc, yv = _conv_fwd(p, cdw, cb, clg, clb, seq=seq, tt=tt, name="conv_fwd")
    mixcat = jnp.concatenate([yr, yv], axis=1)
    mix = _matmul(mixcat, wout, name="proj_out")
    z2, x2 = _ln_fwd(x1, mix, ln["ln2_g"], ln["ln2_b"], 1.0, tt=tt, name="ln2")
    h2 = _matmul(x2, w2i, name="ffn2_in")
    act2 = _swiglu_fwd(h2, tt=tt, name="ffn2_act")
    f2 = _matmul(act2, w2o, name="ffn2_out")
    z3, x3 = _ln_fwd(x2, f2, ln["ln3_g"], ln["ln3_b"], 0.5, tt=tt, name="ln3")
    dx3, loss_part = _loss_fwd_bwd(x3, tgt, tt=tt, name="loss")

    gr = {}
    dz3, gr["ln3_g"], gr["ln3_b"] = _ln_bwd(z3, dx3, ln["ln3_g"], ln["ln3_b"], tt=tt, name="ln3_bwd")
    dact2 = _matmul(dz3, w2o, tb=True, scale=0.5, name="ffn2_out_dx")
    gr["ffn2_w_out"] = _matmul(act2, dz3, ta=True, scale=0.5, name="ffn2_out_dw")
    dh2 = _swiglu_bwd(h2, dact2, tt=tt, name="ffn2_act_bwd")
    dx2 = _add_scaled(dz3, _matmul(dh2, w2i, tb=True, name="ffn2_in_dx"), ALPHA, 1.0, tt=tt, name="dx2")
    gr["ffn2_w_in"] = _matmul(x2, dh2, ta=True, name="ffn2_in_dw")
    dz2, gr["ln2_g"], gr["ln2_b"] = _ln_bwd(z2, dx2, ln["ln2_g"], ln["ln2_b"], tt=tt, name="ln2_bwd")
    dmix = _matmul(dz2, wout, tb=True, name="proj_out_dx")
    gr["w_out"] = _matmul(mixcat, dz2, ta=True, name="proj_out_dw")
    dyr, dyv = dmix[:, :RW], dmix[:, RW:]
    dy, dr_p, dv_p, dkd_p, dg, gr["lnx_g"], gr["lnx_b"], gr["r_k"] = _mix_post_bwd(
        y0, y1, r, v, kd, g, lnx_g, lnx_b, r_k, dyr, tt=tt, name="mix_post_bwd")
    scan_g = _wkv_bwd(seq3(r), seq3(v), seq3(kk), seq3(w), seq3(kd), seq3(b), seq3(dy), sp, tb=tb, name="wkv_bwd")
    cts = _scan_cotangents((dr_p, dv_p, dkd_p), [flat(a) for a in scan_g], tt=tt, name="scan_cts")
    dyc, gr["conv_ln_g"], gr["conv_ln_b"], gr["conv_b"] = _conv_post_bwd(yc, dyv, clg, clb, tt=tt, name="conv_post_bwd")
    dpc, ddw = _conv_bwd(dyc, p, cdw, seq=seq, tt=tt, name="conv_bwd")
    gr["conv_dw"] = ddw[:CONV_K]
    dps, dw2b, dw0c, da2b, da0c, dg2p, gr["k_k"], gr["k_a"] = _mix_prep_bwd(
        p, *small, [*cts, dg], seq=seq, tt=tt, name="mix_prep_bwd")
    gr["w2"] = jnp.stack([dw2b[:LORA, :RW], dw2b[LORA:, RW:]])
    gr["a2"] = jnp.stack([da2b[:LORA, :RW], da2b[LORA:, RW:]])
    gr["w0"], gr["a0"], gr["g2"] = dw0c.reshape(2, RW), da0c.reshape(2, RW), dg2p[:GATE_LORA]
    dpsh, dmu_p, dmu_n = _shift_bwd(dps, p, mu_p, mu_n, seq=seq, tt=tt, name="shift_bwd")
    gr["mu_prev"], gr["mu_next"] = dmu_p[:, :SHIFT_COLS], dmu_n[:, :SHIFT_COLS]
    dp = jnp.concatenate([dpsh, dpc], axis=1)
    dx1 = _add_scaled(dz2, _matmul(dp, win, tb=True, name="proj_in_dx"), ALPHA, 1.0, tt=tt, name="dx1")
    gr["w_in"] = _unpad_in_cols(_matmul(x1, dp, ta=True, name="proj_in_dw"))
    dz1, gr["ln1_g"], gr["ln1_b"] = _ln_bwd(z1, dx1, ln["ln1_g"], ln["ln1_b"], tt=tt, name="ln1_bwd")
    dact1 = _matmul(dz1, w1o, tb=True, scale=0.5, name="ffn1_out_dx")
    gr["ffn1_w_out"] = _matmul(act1, dz1, ta=True, scale=0.5, name="ffn1_out_dw")
    dh1 = _swiglu_bwd(h1, dact1, tt=tt, name="ffn1_act_bwd")
    dx0 = _add_scaled(dz1, _matmul(dh1, w1i, tb=True, name="ffn1_in_dx"), ALPHA, 1.0, tt=tt, name="dx0")
    gr["ffn1_w_in"] = _matmul(x0, dh1, ta=True, name="ffn1_in_dw")
    return loss_part, dx0.reshape(bsz, seq, D_MODEL), gr


def _mesh_pos():
    return lax.axis_index("x"), lax.axis_index("y"), lax.axis_index("c")


def _other_chips(x, y):
    return [(1 - x, y), (x, 1 - y), (1 - x, 1 - y)]


def _gather_chips(shards, *, name):
    n = len(shards)

    def body(*refs):
        ins, outs = refs[:n], refs[n:2 * n]
        send_sems, recv_sems, loc_sems = refs[2 * n:]
        x, y, c = _mesh_pos()
        q = 2 * x + y
        peers = _other_chips(x, y)
        local = [pltpu.make_async_copy(ins[a], outs[a].at[q], loc_sems.at[a]) for a in range(n)]
        for cp in local:
            cp.start()
        sends = [[pltpu.make_async_remote_copy(ins[a], outs[a].at[q], send_sems.at[a, k], recv_sems.at[a, k],
                                               device_id=(px, py, c), device_id_type=MESH)
                  for k, (px, py) in enumerate(peers)] for a in range(n)]
        for a in range(n):
            for cp in sends[a]:
                cp.start()
        for a in range(n):
            for k, (px, py) in enumerate(peers):
                pltpu.make_async_remote_copy(ins[a], outs[a].at[2 * px + py], send_sems.at[a, k], recv_sems.at[a, k],
                                             device_id=(px, py, c), device_id_type=MESH).wait_recv()
        for a in range(n):
            for cp in sends[a]:
                cp.wait_send()
            local[a].wait()

    any_spec = pl.BlockSpec(memory_space=pl.ANY)
    return pl.pallas_call(
        body, name=name,
        out_shape=[jax.ShapeDtypeStruct((N_CHIPS,) + s.shape, s.dtype) for s in shards],
        in_specs=[any_spec] * n, out_specs=[any_spec] * n,
        scratch_shapes=[pltpu.SemaphoreType.DMA((n, 3)), pltpu.SemaphoreType.DMA((n, 3)), pltpu.SemaphoreType.DMA((n,))],
        compiler_params=pltpu.CompilerParams(has_side_effects=True),
    )(*shards)


def _scatter_chips(stacks, *, name):
    n = len(stacks)

    def body(*refs):
        ins, outs = refs[:n], refs[n:2 * n]
        send_sems, recv_sems = refs[2 * n:]
        x, y, c = _mesh_pos()
        peers = _other_chips(x, y)
        sends = [[pltpu.make_async_remote_copy(ins[a].at[2 * px + py], outs[a].at[k], send_sems.at[a, k],
                                               recv_sems.at[a, k], device_id=(px, py, c), device_id_type=MESH)
                  for k, (px, py) in enumerate(peers)] for a in range(n)]
        for a in range(n):
            for cp in sends[a]:
                cp.start()
        for a in range(n):
            for cp in sends[a]:
                cp.wait_recv()
        for a in range(n):
            for cp in sends[a]:
                cp.wait_send()

    any_spec = pl.BlockSpec(memory_space=pl.ANY)
    return pl.pallas_call(
        body, name=name,
        out_shape=[jax.ShapeDtypeStruct((3,) + s.shape[1:], s.dtype) for s in stacks],
        in_specs=[any_spec] * n, out_specs=[any_spec] * n,
        scratch_shapes=[pltpu.SemaphoreType.DMA((n, 3)), pltpu.SemaphoreType.DMA((n, 3))],
        compiler_params=pltpu.CompilerParams(has_side_effects=True),
    )(*stacks)


def _swap_sibling(arrs, *, name):
    n = len(arrs)

    def body(*refs):
        ins, outs = refs[:n], refs[n:2 * n]
        send_sems, recv_sems = refs[2 * n:]
        x, y, c = _mesh_pos()
        cps = [pltpu.make_async_remote_copy(ins[a], outs[a], send_sems.at[a], recv_sems.at[a],
                                            device_id=(x, y, 1 - c), device_id_type=MESH) for a in range(n)]
        for cp in cps:
            cp.start()
        for cp in cps:
            cp.wait_recv()
        for cp in cps:
            cp.wait_send()

    any_spec = pl.BlockSpec(memory_space=pl.ANY)
    return pl.pallas_call(
        body, name=name,
        out_shape=[jax.ShapeDtypeStruct(s.shape, s.dtype) for s in arrs],
        in_specs=[any_spec] * n, out_specs=[any_spec] * n,
        scratch_shapes=[pltpu.SemaphoreType.DMA((n,)), pltpu.SemaphoreType.DMA((n,))],
        compiler_params=pltpu.CompilerParams(has_side_effects=True),
    )(*arrs)


def _all_reduce_rows(vec, *, name):
    rows = vec.shape[0]

    def body(v_ref, o_ref, land, send_sems, recv_sems):
        x, y, c = _mesh_pos()
        me = 4 * x + 2 * y + c
        land[me] = v_ref[...]
        cps = []
        for m in range(1, 8):
            mx, my, mc = (m >> 2) & 1, (m >> 1) & 1, m & 1
            tx, ty, tc = (x + mx) % 2, (y + my) % 2, (c + mc) % 2
            cps.append(pltpu.make_async_remote_copy(v_ref, land.at[me], send_sems.at[m - 1], recv_sems.at[me],
                                                    device_id=(tx, ty, tc), device_id_type=MESH))
        for cp in cps:
            cp.start()
        for m in range(1, 8):
            mx, my, mc = (m >> 2) & 1, (m >> 1) & 1, m & 1
            src = 4 * ((x + mx) % 2) + 2 * ((y + my) % 2) + (c + mc) % 2
            pltpu.make_async_remote_copy(v_ref, land.at[src], send_sems.at[m - 1], recv_sems.at[src],
                                         device_id=(x, y, c), device_id_type=MESH).wait_recv()
        for cp in cps:
            cp.wait_send()
        acc = land[0]
        for d in range(1, 8):
            acc = acc + land[d]
        o_ref[...] = acc

    vm = pl.BlockSpec(memory_space=pltpu.VMEM)
    return pl.pallas_call(
        body, name=name,
        out_shape=jax.ShapeDtypeStruct(vec.shape, F32),
        in_specs=[vm], out_specs=vm,
        scratch_shapes=[pltpu.VMEM((8, rows, LANES), F32), pltpu.SemaphoreType.DMA((7,)), pltpu.SemaphoreType.DMA((8,))],
        compiler_params=pltpu.CompilerParams(has_side_effects=True, vmem_limit_bytes=VMEM_LIMIT),
    )(vec)


def _adamw(w, g, m, v):
    m = ADAM_B1 * m + (1.0 - ADAM_B1) * g
    v = ADAM_B2 * v + (1.0 - ADAM_B2) * (g * g)
    m_hat = m / (1.0 - ADAM_B1 ** ADAM_STEP)
    v_hat = v / (1.0 - ADAM_B2 ** ADAM_STEP)
    delta = -ADAM_LR * (m_hat / (jnp.sqrt(v_hat) + ADAM_EPS) + ADAM_WD * w)
    return delta, m, v


def _sum4(mine, land, *, name):
    rows, cols = mine.shape
    tr = _pick_rows(rows)

    def body(a_ref, l_ref, o_ref):
        o_ref[...] = (a_ref[...].astype(F32) + l_ref[0].astype(F32)) + (l_ref[1].astype(F32) + l_ref[2].astype(F32))

    return pl.pallas_call(
        body, name=name, out_shape=jax.ShapeDtypeStruct((rows, cols), F32), grid=(rows // tr,),
        in_specs=[pl.BlockSpec((tr, cols), lambda i: (i, 0)), pl.BlockSpec((3, tr, cols), lambda i: (0, i, 0))],
        out_specs=pl.BlockSpec((tr, cols), lambda i: (i, 0)),
        compiler_params=_cparams(("parallel",)),
    )(mine, land)


def _pick_rows(rows, want=256):
    for t in range(min(want, rows) // 8 * 8, 0, -8):
        if rows % t == 0:
            return t
    return rows


def _sum_adam(h_mine, h_sib, w, m, v, *, name):
    rows, cols = w.shape
    tr = _pick_rows(rows)

    def body(a_ref, b_ref, w_ref, m_ref, v_ref, g_o, d_o, m_o, v_o):
        g = a_ref[...] + b_ref[...]
        d, mn, vn = _adamw(w_ref[...], g, m_ref[...], v_ref[...])
        g_o[...], d_o[...], m_o[...], v_o[...] = g, d, mn, vn

    spec = pl.BlockSpec((tr, cols), lambda i: (i, 0))
    return pl.pallas_call(
        body, name=name, out_shape=[jax.ShapeDtypeStruct((rows, cols), F32)] * 4, grid=(rows // tr,),
        in_specs=[spec] * 5, out_specs=[spec] * 4, compiler_params=_cparams(("parallel",)),
    )(h_mine, h_sib, w, m, v)


def _adam_rows(w, g, m, v, *, name):
    def body(w_ref, g_ref, m_ref, v_ref, d_o, m_o, v_o):
        d_o[...], m_o[...], v_o[...] = _adamw(w_ref[...], g_ref[...], m_ref[...], v_ref[...])

    vm = pl.BlockSpec(memory_space=pltpu.VMEM)
    return pl.pallas_call(
        body, name=name, out_shape=[jax.ShapeDtypeStruct(w.shape, F32)] * 3,
        in_specs=[vm] * 4, out_specs=[vm] * 3, compiler_params=_cparams(),
    )(w, g, m, v)


def _pack_rows(arrs):
    flat = jnp.concatenate([a.reshape(-1).astype(F32) for a in arrs])
    pad = -flat.shape[0] % (8 * LANES)
    return jnp.concatenate([flat, jnp.zeros((pad,), F32)]).reshape(-1, LANES)


def _unpack_rows(packed, shapes):
    flat = packed.reshape(-1)
    out, off = [], 0
    for s in shapes:
        size = 1
        for d in s:
            size *= d
        out.append(flat[off:off + size].reshape(s))
        off += size
    return out


WEIGHTS = ['ffn1_w_in', 'ffn1_w_out', 'w_in', 'mu_prev', 'mu_next', 'w0', 'w2', 'a0', 'a2', 'g2', 'k_k', 'k_a', 'r_k',
           'lnx_g', 'lnx_b', 'conv_dw', 'conv_b', 'conv_ln_g', 'conv_ln_b', 'w_out', 'ffn2_w_in', 'ffn2_w_out',
           'ln1_g', 'ln1_b', 'ln2_g', 'ln2_b', 'ln3_g', 'ln3_b']
COL_SHARDED = ('ffn1_w_in', 'w_in', 'ffn2_w_in')
ROW_SHARDED = ('ffn1_w_out', 'w_out', 'ffn2_w_out')
BIG = COL_SHARDED + ROW_SHARDED
SMALL_SHARDED = ('w0', 'w2', 'a0', 'a2', 'g2', 'conv_dw')
REPLICATED = tuple(n for n in WEIGHTS if n not in BIG + SMALL_SHARDED)


def _train_step(x, target, w, m, v, *, tt, tb):
    xi, yi, _ = _mesh_pos()
    q = 2 * xi + yi

    shards = [w[n][0].astype(BF16) for n in BIG] + [w[n][0] for n in SMALL_SHARDED]
    gathered = _gather_chips(shards, name="gather_weights")
    full = {}
    for n, gth in zip(BIG + SMALL_SHARDED, gathered):
        if n in ROW_SHARDED:
            full[n] = gth.reshape((-1,) + gth.shape[2:])
        else:
            full[n] = jnp.moveaxis(gth, 0, -2).reshape(gth.shape[1:-1] + (N_CHIPS * gth.shape[-1],))
    for n in REPLICATED:
        full[n] = w[n][0]

    loss_part, grad_x, gr = _local_step(x, target, full, tt=tt, tb=tb)

    stacks = []
    for n in BIG:
        g16 = gr[n].astype(BF16)
        if n in ROW_SHARDED:
            stacks.append(g16.reshape((N_CHIPS, -1) + g16.shape[1:]))
        else:
            stacks.append(jnp.moveaxis(g16.reshape(g16.shape[0], N_CHIPS, -1), 1, 0))
    landed = _scatter_chips(stacks, name="scatter_grads")
    halves = [_sum4(lax.dynamic_index_in_dim(s, q, 0, keepdims=False), l, name="sum4_" + n)
              for n, s, l in zip(BIG, stacks, landed)]
    sib = _swap_sibling(halves, name="swap_halves")
    grad, delta, new_m, new_v = {}, {}, {}, {}
    for n, h, hs in zip(BIG, halves, sib):
        outs = _sum_adam(h, hs, w[n][0], m[n][0], v[n][0], name="adam_" + n)
        grad[n], delta[n], new_m[n], new_v[n] = [o[None] for o in outs]

    small_names = REPLICATED + SMALL_SHARDED
    small_full_shapes = [full[n].shape for n in small_names]
    red = _all_reduce_rows(_pack_rows([gr[n] for n in small_names]), name="reduce_small")
    red = dict(zip(small_names, _unpack_rows(red, small_full_shapes)))
    gsm = {}
    for n in REPLICATED:
        gsm[n] = red[n].reshape(w[n].shape)
    for n in SMALL_SHARDED:
        width = w[n].shape[-1]
        gsm[n] = lax.dynamic_slice_in_dim(red[n], q * width, width, axis=red[n].ndim - 1).reshape(w[n].shape)
    shapes = [w[n].shape for n in small_names]
    d_p, m_p, v_p = _adam_rows(_pack_rows([w[n] for n in small_names]), _pack_rows([gsm[n] for n in small_names]),
                               _pack_rows([m[n] for n in small_names]), _pack_rows([v[n] for n in small_names]),
                               name="adam_small")
    for n, dd, mm, vv in zip(small_names, _unpack_rows(d_p, shapes), _unpack_rows(m_p, shapes), _unpack_rows(v_p, shapes)):
        grad[n], delta[n], new_m[n], new_v[n] = gsm[n], dd, mm, vv
    return loss_part, grad_x, grad, delta, new_m, new_v


def kernel(x, ffn1_w_in, ffn1_w_out, w_in, mu_prev, mu_next, w0, w2, a0, a2, g2, k_k, k_a, r_k, lnx_g, lnx_b, conv_dw, conv_b, conv_ln_g, conv_ln_b, w_out, ffn2_w_in, ffn2_w_out, ln1_g, ln1_b, ln2_g, ln2_b, ln3_g, ln3_b, loss_target, m_ffn1_w_in, m_ffn1_w_out, m_w_in, m_mu_prev, m_mu_next, m_w0, m_w2, m_a0, m_a2, m_g2, m_k_k, m_k_a, m_r_k, m_lnx_g, m_lnx_b, m_conv_dw, m_conv_b, m_conv_ln_g, m_conv_ln_b, m_w_out, m_ffn2_w_in, m_ffn2_w_out, m_ln1_g, m_ln1_b, m_ln2_g, m_ln2_b, m_ln3_g, m_ln3_b, v_ffn1_w_in, v_ffn1_w_out, v_w_in, v_mu_prev, v_mu_next, v_w0, v_w2, v_a0, v_a2, v_g2, v_k_k, v_k_a, v_r_k, v_lnx_g, v_lnx_b, v_conv_dw, v_conv_b, v_conv_ln_g, v_conv_ln_b, v_w_out, v_ffn2_w_in, v_ffn2_w_out, v_ln1_g, v_ln1_b, v_ln2_g, v_ln2_b, v_ln3_g, v_ln3_b):
    args = dict(locals())
    w = {n: args[n] for n in WEIGHTS}
    m = {n: args["m_" + n] for n in WEIGHTS}
    v = {n: args["v_" + n] for n in WEIGHTS}
    seq = x.shape[1]
    loss_part, grad_x, grad, delta, new_m, new_v = _train_step(x, loss_target, w, m, v, tt=min(256, seq), tb=8)
    loss = lax.psum(loss_part[0, 0], ("x", "y", "c"))
    return (loss, grad_x, *[grad[n] for n in WEIGHTS], *[delta[n] for n in WEIGHTS],
            *[new_m[n] for n in WEIGHTS], *[new_v[n] for n in WEIGHTS])
```

```python
import functools

import jax
import jax.numpy as jnp
from jax import lax
from jax.experimental import pallas as pl
from jax.experimental.pallas import tpu as pltpu

F32 = jnp.float32
BF16 = jnp.bfloat16

D_MODEL = 1024
RW = 512
HEAD = 64
CW = 512
CONV_K = 31
CONV_PAD = 15
D_FF = 2816
LORA = 64
GATE_LORA = 160
GATE_PAD = 256
SHIFT_COLS = 1952
SHIFT_PAD = 2048
IN_COLS = 2976
IN_PAD = 3072
LN_EPS = 1e-5
GN_EPS = 64e-5
NORM_EPS = 1e-12
ALPHA = 2.0 ** 0.25
DECAY_SCALE = 0.6065306597126334
ADAM_LR, ADAM_B1, ADAM_B2, ADAM_EPS, ADAM_WD, ADAM_STEP = 0.001, 0.9, 0.999, 1e-08, 0.01, 10
N_CHIPS = 4
VMEM_LIMIT = 56 * 1024 * 1024
TM_FFN = 256
TM_LN = 512

MESH = pl.DeviceIdType.MESH


def _cparams(sem=None, **kw):
    return pltpu.CompilerParams(dimension_semantics=sem, vmem_limit_bytes=VMEM_LIMIT, **kw)


LANES = 128


def _pick_tile(dim, want):
    for t in range(min(want, dim) // LANES * LANES, 0, -LANES):
        if dim % t == 0:
            return t
    return dim


def _matmul(a, b, *, ta=False, tb=False, out_dtype=F32, tm=1024, tn=1024, tk=1024, scale=1.0, col_slabs=False, name):
    if ta:
        k_dim, m_dim = a.shape
    else:
        m_dim, k_dim = a.shape
    n_dim = b.shape[0] if tb else b.shape[1]
    tm, tn, tk = _pick_tile(m_dim, tm), _pick_tile(n_dim, tn), _pick_tile(k_dim, tk)
    assert m_dim % tm == 0 and n_dim % tn == 0 and k_dim % tk == 0, (name, a.shape, b.shape, tm, tn, tk)
    nk = k_dim // tk
    dims = (((0,) if ta else (1,), (1,) if tb else (0,)), ((), ()))
    if col_slabs:
        out_shape = jax.ShapeDtypeStruct((n_dim // tn, m_dim, tn), out_dtype)
        out_spec = pl.BlockSpec((None, tm, tn), lambda i, j, k: (j, i, 0))
    else:
        out_shape = jax.ShapeDtypeStruct((m_dim, n_dim), out_dtype)
        out_spec = pl.BlockSpec((tm, tn), lambda i, j, k: (i, j))

    def body(a_ref, b_ref, o_ref, acc_ref):
        kk = pl.program_id(2)

        @pl.when(kk == 0)
        def _():
            acc_ref[...] = jnp.zeros_like(acc_ref)

        acc_ref[...] += lax.dot_general(a_ref[...].astype(BF16), b_ref[...].astype(BF16), dims,
                                        preferred_element_type=F32)

        @pl.when(kk == nk - 1)
        def _():
            o_ref[...] = (acc_ref[...] * scale).astype(o_ref.dtype)

    a_spec = pl.BlockSpec((tk, tm), lambda i, j, k: (k, i)) if ta else pl.BlockSpec((tm, tk), lambda i, j, k: (i, k))
    b_spec = pl.BlockSpec((tn, tk), lambda i, j, k: (j, k)) if tb else pl.BlockSpec((tk, tn), lambda i, j, k: (k, j))
    return pl.pallas_call(
        body, name=name,
        out_shape=out_shape,
        grid=(m_dim // tm, n_dim // tn, nk),
        in_specs=[a_spec, b_spec],
        out_specs=out_spec,
        scratch_shapes=[pltpu.VMEM((tm, tn), F32)],
        compiler_params=_cparams(("parallel", "parallel", "arbitrary")),
    )(a, b)


def _whole(shape):
    nd = len(shape)
    return pl.BlockSpec(shape, lambda i: (0,) * nd)


def _ffn_in(x, w, *, tm, name):
    n_tok = x.shape[0]
    sw = w.shape[2]
    tm = min(tm, n_tok)

    def body(x_ref, w_ref, h_ref, a_ref):
        xb = x_ref[...].astype(BF16)
        for s in range(2):
            g = jnp.dot(xb, w_ref[s], preferred_element_type=F32)
            u = jnp.dot(xb, w_ref[s + 2], preferred_element_type=F32)
            h_ref[:, s * sw:(s + 1) * sw] = g.astype(BF16)
            h_ref[:, (s + 2) * sw:(s + 3) * sw] = u.astype(BF16)
            a_ref[:, s * sw:(s + 1) * sw] = (_silu(g) * u).astype(BF16)

    return pl.pallas_call(
        body, name=name,
        out_shape=[jax.ShapeDtypeStruct((n_tok, 2 * D_FF), BF16), jax.ShapeDtypeStruct((n_tok, D_FF), BF16)],
        grid=(n_tok // tm,),
        in_specs=[pl.BlockSpec((tm, D_MODEL), lambda i: (i, 0)), _whole(w.shape)],
        out_specs=[pl.BlockSpec((tm, 2 * D_FF), lambda i: (i, 0)), pl.BlockSpec((tm, D_FF), lambda i: (i, 0))],
        compiler_params=_cparams(("parallel",)),
    )(x, w)


def _mm_ln(a_list, w, xres, g, b, fscale, *, tm, name):
    n_tok = xres.shape[0]
    tm = min(tm, n_tok)
    na = len(a_list)

    def body(*refs):
        a_refs = refs[:na]
        w_ref, x_ref, g_ref, b_ref, z_o, y_o, yb_o = refs[na:]
        f, off = None, 0
        for a_ref in a_refs:
            k = a_ref.shape[1]
            t = jnp.dot(a_ref[...].astype(BF16), w_ref[off:off + k, :], preferred_element_type=F32)
            f = t if f is None else f + t
            off += k
        z = ALPHA * x_ref[...] + fscale * f
        y = _layer_norm(z, g_ref[...], b_ref[...])
        z_o[...] = z
        y_o[...] = y
        yb_o[...] = y.astype(BF16)

    tile = pl.BlockSpec((tm, D_MODEL), lambda i: (i, 0))
    return pl.pallas_call(
        body, name=name,
        out_shape=[jax.ShapeDtypeStruct((n_tok, D_MODEL), F32)] * 2 + [jax.ShapeDtypeStruct((n_tok, D_MODEL), BF16)],
        grid=(n_tok // tm,),
        in_specs=[pl.BlockSpec((tm, a.shape[1]), lambda i: (i, 0)) for a in a_list]
        + [_whole(w.shape), tile, _whole(g.shape), _whole(b.shape)],
        out_specs=[tile, tile, tile],
        compiler_params=_cparams(("parallel",)),
    )(*a_list, w, xres, g, b)


def _ffn_out_bwd(dz, w, h, *, tm, name):
    n_tok = dz.shape[0]
    tm = min(tm, n_tok)
    cw = D_FF // 2

    def body(dz_ref, w_ref, h_ref, dh_ref):
        dzb = dz_ref[...].astype(BF16)
        for s in range(2):
            dact = 0.5 * lax.dot_general(dzb, w_ref[s * cw:(s + 1) * cw, :], (((1,), (1,)), ((), ())),
                                         preferred_element_type=F32)
            gate = h_ref[:, s * cw:(s + 1) * cw].astype(F32)
            up = h_ref[:, D_FF + s * cw:D_FF + (s + 1) * cw].astype(F32)
            sg = _sigmoid(gate)
            dh_ref[:, s * cw:(s + 1) * cw] = (dact * up * sg * (1.0 + gate * (1.0 - sg))).astype(BF16)
            dh_ref[:, D_FF + s * cw:D_FF + (s + 1) * cw] = (dact * gate * sg).astype(BF16)

    wide = pl.BlockSpec((tm, 2 * D_FF), lambda i: (i, 0))
    return pl.pallas_call(
        body, name=name,
        out_shape=jax.ShapeDtypeStruct((n_tok, 2 * D_FF), BF16),
        grid=(n_tok // tm,),
        in_specs=[pl.BlockSpec((tm, D_MODEL), lambda i: (i, 0)), _whole(w.shape), wide],
        out_specs=wide,
        compiler_params=_cparams(("parallel",)),
    )(dz, w, h)


def _mm_nt_res(a_list, w, dz, *, tm, name):
    n_tok = dz.shape[0]
    tm = min(tm, n_tok)
    na = len(a_list)
    nt = (((1,), (1,)), ((), ()))

    def body(*refs):
        a_refs = refs[:na]
        w_ref, dz_ref, o_ref = refs[na:]
        acc = ALPHA * dz_ref[...]
        if len(w_ref.shape) == 3:
            cw = w_ref.shape[2]
            for s in range(w_ref.shape[0]):
                acc = acc + lax.dot_general(a_refs[0][:, s * cw:(s + 1) * cw], w_ref[s], nt, preferred_element_type=F32)
        else:
            off = 0
            for a_ref in a_refs:
                k = a_ref.shape[1]
                acc = acc + lax.dot_general(a_ref[...], w_ref[:, off:off + k], nt, preferred_element_type=F32)
                off += k
        o_ref[...] = acc

    tile = pl.BlockSpec((tm, D_MODEL), lambda i: (i, 0))
    return pl.pallas_call(
        body, name=name,
        out_shape=jax.ShapeDtypeStruct((n_tok, D_MODEL), F32),
        grid=(n_tok // tm,),
        in_specs=[pl.BlockSpec((tm, a.shape[1]), lambda i: (i, 0)) for a in a_list] + [_whole(w.shape), tile],
        out_specs=tile,
        compiler_params=_cparams(("parallel",)),
    )(*a_list, w, dz)


def _rowcall(fn, tok_in, full_in, tok_out, acc_out, *, tt, name):
    views = [a if isinstance(a, tuple) else (a, a.shape[1], 0) for a in tok_in]
    tok_in = [a for a, _, _ in views]
    n_tok = tok_in[0].shape[0]
    assert n_tok % tt == 0, (name, n_tok, tt)
    n_ti, n_fi, n_to = len(tok_in), len(full_in), len(tok_out)

    def body(*refs):
        i = pl.program_id(0)
        ins = [r[...] for r in refs[:n_ti + n_fi]]
        outs = fn(i, *ins)
        o_refs = refs[n_ti + n_fi:]
        for r, val in zip(o_refs[:n_to], outs[:n_to]):
            r[...] = val.astype(r.dtype)
        if acc_out:
            @pl.when(i == 0)
            def _():
                for r in o_refs[n_to:]:
                    r[...] = jnp.zeros_like(r)
            for r, val in zip(o_refs[n_to:], outs[n_to:]):
                r[...] += val.reshape(r.shape).astype(F32)

    in_specs = [pl.BlockSpec((tt, width), functools.partial(lambda k, i: (i, k), k)) for _, width, k in views]
    in_specs += [pl.BlockSpec(a.shape, lambda i: (0, 0)) for a in full_in]
    out_specs = [pl.BlockSpec((tt, c), lambda i: (i, 0)) for c, _ in tok_out]
    out_specs += [pl.BlockSpec(s, lambda i: (0, 0)) for s in acc_out]
    out_shape = [jax.ShapeDtypeStruct((n_tok, c), dt) for c, dt in tok_out]
    out_shape += [jax.ShapeDtypeStruct(s, F32) for s in acc_out]
    return pl.pallas_call(
        body, name=name, out_shape=out_shape, grid=(n_tok // tt,), in_specs=in_specs, out_specs=out_specs,
        compiler_params=_cparams(("arbitrary",) if acc_out else ("parallel",)),
    )(*tok_in, *full_in)


@jax.custom_vjp
def _bdot(a, b):
    return jnp.dot(a.astype(BF16), b.astype(BF16), preferred_element_type=F32)


def _bdot_fwd(a, b):
    return _bdot(a, b), (a, b)


def _bdot_bwd(res, g):
    a, b = res
    g16 = g.astype(BF16)
    da = lax.dot_general(g16, b.astype(BF16), (((1,), (1,)), ((), ())), preferred_element_type=F32)
    db = lax.dot_general(a.astype(BF16), g16, (((0,), (0,)), ((), ())), preferred_element_type=F32)
    return da, db


_bdot.defvjp(_bdot_fwd, _bdot_bwd)


def _split16(x):
    hi = x.astype(BF16)
    lo = (x - hi.astype(F32)).astype(BF16)
    return hi, lo


def _segsum_raw(x, e2):
    hi, lo = _split16(x)
    outs = []
    for c in range(x.shape[1] // 256):
        lhs = jnp.concatenate([hi[:, 256 * c:256 * (c + 1)], lo[:, 256 * c:256 * (c + 1)]], axis=1)
        outs.append(jnp.dot(lhs, e2, preferred_element_type=F32))
    return jnp.concatenate(outs, axis=1)


@jax.custom_vjp
def _segsum(x, e2):
    return _segsum_raw(x, e2)


def _segsum_fwd(x, e2):
    return _segsum_raw(x, e2), e2


def _segsum_bwd(e2, g):
    return _segsum_raw(g, e2), jnp.zeros_like(e2)


_segsum.defvjp(_segsum_fwd, _segsum_bwd)


def _head_ones():
    r = lax.broadcasted_iota(jnp.int32, (512, 256), 0) % 256
    c = lax.broadcasted_iota(jnp.int32, (512, 256), 1)
    return (r // HEAD == c // HEAD).astype(BF16)


def _sigmoid(x):
    return 1.0 / (1.0 + jnp.exp(-x))


def _silu(x):
    return x * _sigmoid(x)


def _layer_norm(z, g, b, eps=LN_EPS):
    mu = jnp.mean(z, axis=-1, keepdims=True)
    zc = z - mu
    var = jnp.mean(zc * zc, axis=-1, keepdims=True)
    return zc * lax.rsqrt(var + eps) * g + b


def _prep(ps, w2b, w0c, a2b, a0c, g2p, k_k, k_a, e2):
    r, k, v = ps[:, 0:512], ps[:, 512:1024], ps[:, 1024:1536]
    wd, ad, gd = ps[:, 1536:1664], ps[:, 1664:1792], ps[:, 1792:2048]
    lw = _bdot(jnp.tanh(wd), w2b) + w0c
    decay = jnp.exp(-DECAY_SCALE * _sigmoid(lw))
    a = _sigmoid(_bdot(ad, a2b) + a0c)
    g = _bdot(_sigmoid(gd), g2p)
    kkr = k * k_k
    nrm = jnp.sqrt(_segsum(kkr * kkr, e2))
    kk = kkr / jnp.maximum(nrm, NORM_EPS)
    k2 = jnp.concatenate([k, k], axis=1)
    ka2 = jnp.concatenate([k_a, k_a], axis=1)
    kd = k2 * (1.0 + (a - 1.0) * ka2)
    b = jnp.concatenate([kk, kk], axis=1) * a
    return r, v, kk, decay, kd, b, g


def _post(y0, y1, r, v, kd, g, lnx_g, lnx_b, r_k, e2):
    y = y0 + y1
    mu = _segsum(y, e2) * (1.0 / HEAD)
    yc = y - mu
    var = _segsum(yc * yc, e2) * (1.0 / HEAD)
    yn = yc * lax.rsqrt(var + GN_EPS) * lnx_g + lnx_b
    bonus = _segsum(r * (kd[:, :RW] + kd[:, RW:]) * r_k, e2)
    return (yn + bonus * v) * g


def _conv_post(yc, ln_g, ln_b):
    return _silu(_layer_norm(yc, ln_g, ln_b))


def _ln_bwd(z, dy, g, b, *, tt, name):
    def fn(i, zv, dv, gv, bv):
        _, vjp = jax.vjp(_layer_norm, zv, gv, bv)
        return vjp(dv)
    return _rowcall(fn, [z, dy], [g, b], [(D_MODEL, F32)], [(1, D_MODEL), (1, D_MODEL)], tt=tt, name=name)


def _loss_fwd_bwd(y, target, *, tt, name):
    def fn(i, yv, tv):
        e = yv - tv
        part = 0.5 * jnp.sum(jnp.mean(e * e, axis=-1, keepdims=True), axis=0, keepdims=True)
        return e * (1.0 / D_MODEL), jnp.broadcast_to(part, (8, 128))
    return _rowcall(fn, [y, target], [], [(D_MODEL, F32)], [(8, 128)], tt=tt, name=name)


def _halo_specs(cols_block, hb, tt, n_tok, col_idx):
    nb = n_tok // hb
    prev = pl.BlockSpec((hb, cols_block), lambda i: (jnp.maximum(i * (tt // hb) - 1, 0), col_idx))
    nxt = pl.BlockSpec((hb, cols_block), lambda i: (jnp.minimum((i + 1) * (tt // hb), nb - 1), col_idx))
    return prev, nxt


def _mix_prep(p, mu_p, mu_n, w2b, w0c, a2b, a0c, g2p, k_k, k_a, *, seq, tt, name):
    n_tok = p.shape[0]
    tps = seq // tt
    e2 = _head_ones()

    def body(p_ref, hp_ref, hn_ref, mup_ref, mun_ref, w2b_ref, w0c_ref, a2b_ref, a0c_ref, g2p_ref, kk_ref, ka_ref,
             e2_ref, r_o, v_o, kk_o, w_o, kd_o, b_o, g_o, ext):
        i = pl.program_id(0)
        first = (i % tps) == 0
        last = (i % tps) == tps - 1
        pv = p_ref[...]
        ext[pl.ds(0, 8), :] = jnp.where(first, 0.0, hp_ref[...])
        ext[pl.ds(8, tt), :] = pv
        ext[pl.ds(8 + tt, 8), :] = jnp.where(last, 0.0, hn_ref[...])
        prev = ext[pl.ds(7, tt), :]
        nxt = ext[pl.ds(9, tt), :]
        ps = pv + mup_ref[...] * (prev - pv) + mun_ref[...] * (nxt - pv)
        outs = _prep(ps, w2b_ref[...], w0c_ref[...], a2b_ref[...], a0c_ref[...], g2p_ref[...], kk_ref[...],
                     ka_ref[...], e2_ref[...])
        for o_ref, val in zip((r_o, v_o, kk_o, w_o, kd_o, b_o, g_o), outs):
            o_ref[...] = val

    hp, hn = _halo_specs(SHIFT_PAD, 8, tt, n_tok, 0)
    fulls = [mu_p, mu_n, w2b, w0c, a2b, a0c, g2p, k_k, k_a, e2]
    widths = (RW, RW, RW, 2 * RW, 2 * RW, 2 * RW, RW)
    return pl.pallas_call(
        body, name=name,
        out_shape=[jax.ShapeDtypeStruct((n_tok, c), F32) for c in widths],
        grid=(n_tok // tt,),
        in_specs=[pl.BlockSpec((tt, SHIFT_PAD), lambda i: (i, 0)), hp, hn]
        + [pl.BlockSpec(a.shape, lambda i: (0, 0)) for a in fulls],
        out_specs=[pl.BlockSpec((tt, c), lambda i: (i, 0)) for c in widths],
        scratch_shapes=[pltpu.VMEM((tt + 16, SHIFT_PAD), F32)],
        compiler_params=_cparams(("parallel",)),
    )(p, p, p, *fulls)


def _mix_prep_bwd(p, mu_p, mu_n, w2b, w0c, a2b, a0c, g2p, k_k, k_a, cts, *, seq, tt, name):
    n_tok = p.shape[0]
    tps = seq // tt
    e2 = _head_ones()
    acc_shapes = [w2b.shape, w0c.shape, a2b.shape, a0c.shape, g2p.shape, k_k.shape, k_a.shape]

    def body(p_ref, hp_ref, hn_ref, mup_ref, mun_ref, w2b_ref, w0c_ref, a2b_ref, a0c_ref, g2p_ref, kk_ref, ka_ref,
             e2_ref, dr, dv, dkk, dw, dkd, db, dg, dps_o, *rest):
        acc_refs, ext = rest[:-1], rest[-1]
        i = pl.program_id(0)
        first = (i % tps) == 0
        last = (i % tps) == tps - 1
        pv = p_ref[...]
        ext[pl.ds(0, 8), :] = jnp.where(first, 0.0, hp_ref[...])
        ext[pl.ds(8, tt), :] = pv
        ext[pl.ds(8 + tt, 8), :] = jnp.where(last, 0.0, hn_ref[...])
        prev = ext[pl.ds(7, tt), :]
        nxt = ext[pl.ds(9, tt), :]
        ps = pv + mup_ref[...] * (prev - pv) + mun_ref[...] * (nxt - pv)
        e2v = e2_ref[...]
        _, vjp = jax.vjp(lambda *a: _prep(*a, e2v), ps, w2b_ref[...], w0c_ref[...], a2b_ref[...], a0c_ref[...],
                         g2p_ref[...], kk_ref[...], ka_ref[...])
        grads = vjp((dr[...], dv[...], dkk[...], dw[...], dkd[...], db[...], dg[...]))
        dps_o[...] = grads[0]

        @pl.when(i == 0)
        def _():
            for r in acc_refs:
                r[...] = jnp.zeros_like(r)
        for r, val in zip(acc_refs, grads[1:]):
            r[...] += val

    hp, hn = _halo_specs(SHIFT_PAD, 8, tt, n_tok, 0)
    fulls = [mu_p, mu_n, w2b, w0c, a2b, a0c, g2p, k_k, k_a, e2]
    return pl.pallas_call(
        body, name=name,
        out_shape=[jax.ShapeDtypeStruct((n_tok, SHIFT_PAD), F32)] + [jax.ShapeDtypeStruct(s, F32) for s in acc_shapes],
        grid=(n_tok // tt,),
        in_specs=[pl.BlockSpec((tt, SHIFT_PAD), lambda i: (i, 0)), hp, hn]
        + [pl.BlockSpec(a.shape, lambda i: (0, 0)) for a in fulls]
        + [pl.BlockSpec((tt, c.shape[1]), lambda i: (i, 0)) for c in cts],
        out_specs=[pl.BlockSpec((tt, SHIFT_PAD), lambda i: (i, 0))] + [pl.BlockSpec(s, lambda i: (0, 0)) for s in acc_shapes],
        scratch_shapes=[pltpu.VMEM((tt + 16, SHIFT_PAD), F32)],
        compiler_params=_cparams(("arbitrary",)),
    )(p, p, p, *fulls, *cts)


def _shift_bwd(dps, p, mu_p, mu_n, *, seq, tt, name):
    n_tok = p.shape[0]
    tps = seq // tt

    def body(d_ref, dhp_ref, dhn_ref, p_ref, php_ref, phn_ref, mup_ref, mun_ref, dp_o, dmup_o, dmun_o, ext):
        i = pl.program_id(0)
        first = (i % tps) == 0
        last = (i % tps) == tps - 1
        mup, mun = mup_ref[...], mun_ref[...]
        dv = d_ref[...]
        pv = p_ref[...]
        ext[pl.ds(0, 8), :] = jnp.where(first, 0.0, dhp_ref[...])
        ext[pl.ds(8, tt), :] = dv
        ext[pl.ds(8 + tt, 8), :] = jnp.where(last, 0.0, dhn_ref[...])
        d_prev = ext[pl.ds(7, tt), :]
        d_next = ext[pl.ds(9, tt), :]
        dp_o[...] = (dv * (1.0 - mup - mun) + d_next * mup + d_prev * mun).astype(dp_o.dtype)
        ext[pl.ds(0, 8), :] = jnp.where(first, 0.0, php_ref[...])
        ext[pl.ds(8, tt), :] = pv
        ext[pl.ds(8 + tt, 8), :] = jnp.where(last, 0.0, phn_ref[...])
        p_prev = ext[pl.ds(7, tt), :]
        p_next = ext[pl.ds(9, tt), :]

        @pl.when(i == 0)
        def _():
            dmup_o[...] = jnp.zeros_like(dmup_o)
            dmun_o[...] = jnp.zeros_like(dmun_o)
        dmup_o[...] += jnp.sum(dv * (p_prev - pv), axis=0, keepdims=True)
        dmun_o[...] += jnp.sum(dv * (p_next - pv), axis=0, keepdims=True)

    hp, hn = _halo_specs(SHIFT_PAD, 8, tt, n_tok, 0)
    tile = pl.BlockSpec((tt, SHIFT_PAD), lambda i: (i, 0))
    full = pl.BlockSpec((1, SHIFT_PAD), lambda i: (0, 0))
    return pl.pallas_call(
        body, name=name,
        out_shape=[jax.ShapeDtypeStruct((n_tok, SHIFT_PAD), BF16), jax.ShapeDtypeStruct((1, SHIFT_PAD), F32),
                   jax.ShapeDtypeStruct((1, SHIFT_PAD), F32)],
        grid=(n_tok // tt,),
        in_specs=[tile, hp, hn, tile, hp, hn, full, full],
        out_specs=[tile, full, full],
        scratch_shapes=[pltpu.VMEM((tt + 16, SHIFT_PAD), F32)],
        compiler_params=_cparams(("arbitrary",)),
    )(dps, dps, dps, p, p, p, mu_p, mu_n)


def _mix_post(y0, y1, r, v, kd, g, lnx_g, lnx_b, r_k, *, tt, name):
    e2 = _head_ones()
    return _rowcall(lambda i, *a: (_post(*a),), [y0, y1, r, v, kd, g], [lnx_g, lnx_b, r_k, e2], [(RW, BF16)], [],
                    tt=tt, name=name)[0]


def _mix_post_bwd(y0, y1, r, v, kd, g, lnx_g, lnx_b, r_k, dout, *, tt, name):
    e2 = _head_ones()

    def fn(i, y0v, y1v, rv, vv, kdv, gv, dov, lg, lb, rk, e2v):
        _, vjp = jax.vjp(lambda *a: _post(*a, e2v), y0v, y1v, rv, vv, kdv, gv, lg, lb, rk)
        gr = vjp(dov.astype(F32))
        return gr[0], gr[2], gr[3], gr[4], gr[5], gr[6], gr[7], gr[8]
    return _rowcall(fn, [y0, y1, r, v, kd, g, dout], [lnx_g, lnx_b, r_k, e2],
                    [(RW, F32), (RW, F32), (RW, F32), (2 * RW, F32), (RW, F32)], [(1, RW), (1, RW), (1, RW)],
                    tt=tt, name=name)


def _conv_fwd(p, dw, db, ln_g, ln_b, *, seq, tt, name):
    n_tok = p.shape[0]
    tps = seq // tt

    def glu(x, gate):
        return x * _sigmoid(gate)

    def body(u_ref, g_ref, uhp, ghp, uhn, ghn, dw_ref, db_ref, lg_ref, lb_ref, yc_o, y_o, ext):
        i = pl.program_id(0)
        first = (i % tps) == 0
        last = (i % tps) == tps - 1
        ext[pl.ds(0, 16), :] = jnp.where(first, 0.0, glu(uhp[...], ghp[...]))
        ext[pl.ds(16, tt), :] = glu(u_ref[...], g_ref[...])
        ext[pl.ds(16 + tt, 16), :] = jnp.where(last, 0.0, glu(uhn[...], ghn[...]))
        acc = jnp.zeros((tt, CW), F32) + db_ref[...]
        for k in range(CONV_K):
            acc = acc + ext[pl.ds(k + 1, tt), :] * dw_ref[pl.ds(k, 1), :]
        yc_o[...] = acc
        y_o[...] = _conv_post(acc, lg_ref[...], lb_ref[...]).astype(y_o.dtype)

    uhp_s, uhn_s = _halo_specs(CW, 16, tt, n_tok, 4)
    ghp_s, ghn_s = _halo_specs(CW, 16, tt, n_tok, 5)
    fulls = [dw, db, ln_g, ln_b]
    return pl.pallas_call(
        body, name=name,
        out_shape=[jax.ShapeDtypeStruct((n_tok, CW), F32), jax.ShapeDtypeStruct((n_tok, CW), BF16)],
        grid=(n_tok // tt,),
        in_specs=[pl.BlockSpec((tt, CW), lambda i: (i, 4)), pl.BlockSpec((tt, CW), lambda i: (i, 5)),
                  uhp_s, ghp_s, uhn_s, ghn_s] + [pl.BlockSpec(a.shape, lambda i: (0, 0)) for a in fulls],
        out_specs=[pl.BlockSpec((tt, CW), lambda i: (i, 0)), pl.BlockSpec((tt, CW), lambda i: (i, 0))],
        scratch_shapes=[pltpu.VMEM((tt + 32, CW), F32)],
        compiler_params=_cparams(("parallel",)),
    )(p, p, p, p, p, p, *fulls)


def _conv_post_bwd(yc, dy, ln_g, ln_b, *, tt, name):
    def fn(i, ycv, dyv, lg, lb):
        _, vjp = jax.vjp(_conv_post, ycv, lg, lb)
        dyc, dg, dbb = vjp(dyv.astype(F32))
        return dyc, dg, dbb, jnp.sum(dyc, axis=0, keepdims=True)
    return _rowcall(fn, [yc, dy], [ln_g, ln_b], [(CW, F32)], [(1, CW), (1, CW), (1, CW)], tt=tt, name=name)


def _conv_bwd(dyc, p, dw, *, seq, tt, name):
    n_tok = p.shape[0]
    tps = seq // tt

    def body(d_ref, dhp, dhn, u_ref, g_ref, uhp, ghp, uhn, ghn, dw_ref, dp_o, ddw_o, ext):
        i = pl.program_id(0)
        first = (i % tps) == 0
        last = (i % tps) == tps - 1
        dv = d_ref[...]
        ext[pl.ds(0, 16), :] = jnp.where(first, 0.0, dhp[...])
        ext[pl.ds(16, tt), :] = dv
        ext[pl.ds(16 + tt, 16), :] = jnp.where(last, 0.0, dhn[...])
        du = jnp.zeros((tt, CW), F32)
        for k in range(CONV_K):
            du = du + ext[pl.ds(31 - k, tt), :] * dw_ref[pl.ds(k, 1), :]
        uv, gv = u_ref[...], g_ref[...]
        sg = _sigmoid(gv)
        dp_o[:, 0:CW] = (du * sg).astype(dp_o.dtype)
        dp_o[:, CW:2 * CW] = (du * uv * sg * (1.0 - sg)).astype(dp_o.dtype)
        ext[pl.ds(0, 16), :] = jnp.where(first, 0.0, uhp[...] * _sigmoid(ghp[...]))
        ext[pl.ds(16, tt), :] = uv * sg
        ext[pl.ds(16 + tt, 16), :] = jnp.where(last, 0.0, uhn[...] * _sigmoid(ghn[...]))

        @pl.when(i == 0)
        def _():
            ddw_o[...] = jnp.zeros_like(ddw_o)
        for k in range(CONV_K):
            ddw_o[pl.ds(k, 1), :] += jnp.sum(dv * ext[pl.ds(k + 1, tt), :], axis=0, keepdims=True)

    dhp_s, dhn_s = _halo_specs(CW, 16, tt, n_tok, 0)
    uhp_s, uhn_s = _halo_specs(CW, 16, tt, n_tok, 4)
    ghp_s, ghn_s = _halo_specs(CW, 16, tt, n_tok, 5)
    return pl.pallas_call(
        body, name=name,
        out_shape=[jax.ShapeDtypeStruct((n_tok, 2 * CW), BF16), jax.ShapeDtypeStruct((32, CW), F32)],
        grid=(n_tok // tt,),
        in_specs=[pl.BlockSpec((tt, CW), lambda i: (i, 0)), dhp_s, dhn_s,
                  pl.BlockSpec((tt, CW), lambda i: (i, 4)), pl.BlockSpec((tt, CW), lambda i: (i, 5)),
                  uhp_s, ghp_s, uhn_s, ghn_s, pl.BlockSpec(dw.shape, lambda i: (0, 0))],
        out_specs=[pl.BlockSpec((tt, 2 * CW), lambda i: (i, 0)), pl.BlockSpec((32, CW), lambda i: (0, 0))],
        scratch_shapes=[pltpu.VMEM((tt + 32, CW), F32)],
        compiler_params=_cparams(("arbitrary",)),
    )(dyc, dyc, dyc, p, p, p, p, p, p, dw)


def _segdot(hi, lo, e2):
    outs = []
    for c in range(hi.shape[1] // 256):
        lhs = jnp.concatenate([hi[:, 256 * c:256 * (c + 1)], lo[:, 256 * c:256 * (c + 1)]], axis=1)
        outs.append(jnp.dot(lhs, e2, preferred_element_type=F32))
    return jnp.concatenate(outs, axis=1)


def _seg(x, e2):
    hi, lo = _split16(x)
    return _segdot(hi, lo, e2)


def _diag_mask():
    return lax.broadcasted_iota(jnp.int32, (HEAD, RW), 0) == lax.broadcasted_iota(jnp.int32, (HEAD, RW), 1) % HEAD


def _bcast_rows(rows):
    return jnp.concatenate([jnp.broadcast_to(x, (HEAD, RW)) for x in rows], axis=0)


def _col_form(rows, dmask, e2):
    his, los = [], []
    for x in rows:
        hi = x.astype(BF16).astype(F32)
        lo = x - hi
        his.append(jnp.where(dmask, jnp.broadcast_to(hi, (HEAD, RW)), 0.0).astype(BF16))
        los.append(jnp.where(dmask, jnp.broadcast_to(lo, (HEAD, RW)), 0.0).astype(BF16))
    return _segdot(jnp.concatenate(his, axis=0), jnp.concatenate(los, axis=0), e2)


def _row_form(col, s, dmask):
    return jnp.sum(jnp.where(dmask, col[s * HEAD:(s + 1) * HEAD], 0.0), axis=0, keepdims=True)


def _row_sum(x, s):
    return jnp.sum(x[s * HEAD:(s + 1) * HEAD], axis=0, keepdims=True)


def _wkv_fwd(r, v, kk, w, kd, b, *, tb, name):
    bsz, seq, _ = r.shape
    nb = seq // tb
    ns = 2 * bsz
    e2 = _head_ones()

    def body(r0, r1, v0, v1, k0, k1, w0, w1, kd0, kd1, b0, b1, e2_ref, y0_o, y1_o, sp_o, s_ref):
        i = pl.program_id(0)

        @pl.when(i == 0)
        def _():
            s_ref[...] = jnp.zeros_like(s_ref)

        e2v = e2_ref[...]
        dmask = _diag_mask()
        y_refs = (y0_o, y1_o)

        def step(j, carry):
            tl = (j, tb - 1 - j)

            def rows(refs):
                return [refs[d][bb, pl.ds(tl[d], 1), :] for d in (0, 1) for bb in range(bsz)]

            s_old = s_ref[...]
            for s in range(ns):
                sp_o[s, pl.ds(j, 1), :, :] = s_old[s * HEAD:(s + 1) * HEAD].reshape(1, HEAD, RW)
            sa = -_seg(s_old * _bcast_rows(rows((k0, k1))), e2v)
            vc = _col_form(rows((v0, v1)), dmask, e2v)
            s_new = (s_old * _bcast_rows(rows((w0, w1))) + sa * _bcast_rows(rows((b0, b1)))
                     + vc * _bcast_rows(rows((kd0, kd1))))
            ycol = _seg(s_new * _bcast_rows(rows((r0, r1))), e2v)
            for d in (0, 1):
                for bb in range(bsz):
                    y_refs[d][bb, pl.ds(tl[d], 1), :] = _row_form(ycol, d * bsz + bb, dmask)
            s_ref[...] = s_new
            return carry

        lax.fori_loop(0, tb, step, 0)

    def blk(width_idx, rev):
        if rev:
            return pl.BlockSpec((bsz, tb, RW), lambda i: (0, nb - 1 - i, width_idx))
        return pl.BlockSpec((bsz, tb, RW), lambda i: (0, i, width_idx))

    in_specs = [blk(0, False), blk(0, True)] * 3 + [blk(0, False), blk(1, True)] * 3
    in_specs.append(pl.BlockSpec(e2.shape, lambda i: (0, 0)))
    return pl.pallas_call(
        body, name=name,
        out_shape=[jax.ShapeDtypeStruct((bsz, seq, RW), F32), jax.ShapeDtypeStruct((bsz, seq, RW), F32),
                   jax.ShapeDtypeStruct((ns, seq, HEAD, RW), F32)],
        grid=(nb,),
        in_specs=in_specs,
        out_specs=[blk(0, False), blk(0, True), pl.BlockSpec((ns, tb, HEAD, RW), lambda i: (0, i, 0, 0))],
        scratch_shapes=[pltpu.VMEM((ns * HEAD, RW), F32)],
        compiler_params=_cparams(("arbitrary",)),
    )(r, r, v, v, kk, kk, w, w, kd, kd, b, b, e2)


def _wkv_bwd(r, v, kk, w, kd, b, dy, sp, *, tb, name):
    bsz, seq, _ = r.shape
    nb = seq // tb
    ns = 2 * bsz
    e2 = _head_ones()

    def body(r0, r1, v0, v1, k0, k1, dy0, dy1, w0, w1, kd0, kd1, b0, b1, sp_ref, e2_ref, *rest):
        outs, g_ref = rest[:-1], rest[-1]
        i = pl.program_id(0)

        @pl.when(i == 0)
        def _():
            g_ref[...] = jnp.zeros_like(g_ref)

        e2v = e2_ref[...]
        dmask = _diag_mask()

        def step(jj, carry):
            sl = tb - 1 - jj
            tl = (sl, jj)

            def rows(refs):
                return [refs[d][bb, pl.ds(tl[d], 1), :] for d in (0, 1) for bb in range(bsz)]

            s_old = jnp.concatenate([sp_ref[s, pl.ds(sl, 1), :, :].reshape(HEAD, RW) for s in range(ns)], axis=0)
            neg_kk = -_bcast_rows(rows((k0, k1)))
            wm = _bcast_rows(rows((w0, w1)))
            km = _bcast_rows(rows((kd0, kd1)))
            bm = _bcast_rows(rows((b0, b1)))
            rm = _bcast_rows(rows((r0, r1)))
            sa = _seg(s_old * neg_kk, e2v)
            vc = _col_form(rows((v0, v1)), dmask, e2v)
            dyc = _col_form(rows((dy0, dy1)), dmask, e2v)
            s_new = s_old * wm + sa * bm + vc * km
            gt = g_ref[...] + dyc * rm
            gb = _seg(gt * bm, e2v)
            dvc = _seg(gt * km, e2v)
            vals = (s_new * dyc, None, s_old * gb, s_old * gt, gt * vc, sa * gt)
            for d in (0, 1):
                for bb in range(bsz):
                    s = d * bsz + bb
                    at = (bb, pl.ds(tl[d], 1), slice(None))
                    outs[0 + d][at] = _row_sum(vals[0], s)
                    outs[2 + d][at] = _row_form(dvc, s, dmask)
                    outs[4 + d][at] = -_row_sum(vals[2], s)
                    outs[6 + d][at] = _row_sum(vals[3], s)
                    outs[8 + d][at] = _row_sum(vals[4], s)
                    outs[10 + d][at] = _row_sum(vals[5], s)
            g_ref[...] = gt * wm + gb * neg_kk
            return carry

        lax.fori_loop(0, tb, step, 0)

    def blk(width_idx, rev):
        if rev:
            return pl.BlockSpec((bsz, tb, RW), lambda i: (0, nb - 1 - i, width_idx))
        return pl.BlockSpec((bsz, tb, RW), lambda i: (0, i, width_idx))

    in_specs = [blk(0, True), blk(0, False)] * 4 + [blk(0, True), blk(1, False)] * 3
    in_specs.append(pl.BlockSpec((ns, tb, HEAD, RW), lambda i: (0, nb - 1 - i, 0, 0)))
    in_specs.append(pl.BlockSpec(e2.shape, lambda i: (0, 0)))
    return pl.pallas_call(
        body, name=name,
        out_shape=[jax.ShapeDtypeStruct((bsz, seq, RW), F32)] * 12,
        grid=(nb,),
        in_specs=in_specs,
        out_specs=[blk(0, True), blk(0, False)] * 6,
        scratch_shapes=[pltpu.VMEM((ns * HEAD, RW), F32)],
        compiler_params=_cparams(("arbitrary",)),
    )(r, r, v, v, kk, kk, dy, dy, w, w, kd, kd, b, b, sp, e2)


def _scan_cotangents(post_g, scan_g, *, tt, name):
    dr_p, dv_p, dkd_p = post_g
    cat = functools.partial(jnp.concatenate, axis=1)

    def fn(i, drp, dvp, dkdp, dr0, dr1, dv0, dv1, dk0, dk1, dw0, dw1, dkd0, dkd1, db0, db1):
        return (drp + dr0 + dr1, dvp + dv0 + dv1, dk0 + dk1, cat([dw0, dw1]), dkdp + cat([dkd0, dkd1]), cat([db0, db1]))
    return _rowcall(fn, [dr_p, dv_p, dkd_p, *scan_g], [],
                    [(RW, F32), (RW, F32), (RW, F32), (2 * RW, F32), (2 * RW, F32), (2 * RW, F32)], [], tt=tt, name=name)


def _block_diag2(w):
    z = jnp.zeros_like(w[0])
    return jnp.concatenate([jnp.concatenate([w[0], z], axis=1), jnp.concatenate([z, w[1]], axis=1)], axis=0)


def _pad_in_cols(a):
    z = jnp.zeros(a.shape[:-1] + (SHIFT_PAD - SHIFT_COLS,), a.dtype)
    return jnp.concatenate([a[..., :SHIFT_COLS], z, a[..., SHIFT_COLS:]], axis=-1)


def _unpad_in_cols(a):
    return jnp.concatenate([a[..., :SHIFT_COLS], a[..., SHIFT_PAD:]], axis=-1)


def _local_step(x, target, wts, *, tt, tb):
    bsz, seq, _ = x.shape
    n_tok = bsz * seq
    row = lambda a: a.reshape(1, -1).astype(F32)
    x0 = x.reshape(n_tok, D_MODEL)
    tgt = target.reshape(n_tok, D_MODEL)
    w1i, w1o, wout, w2i, w2o = wts["ffn1_w_in"], wts["ffn1_w_out"], wts["w_out"], wts["ffn2_w_in"], wts["ffn2_w_out"]
    win = _pad_in_cols(wts["w_in"])
    zpad = jnp.zeros((1, SHIFT_PAD - SHIFT_COLS), F32)
    mu_p = jnp.concatenate([row(wts["mu_prev"]), zpad], axis=1)
    mu_n = jnp.concatenate([row(wts["mu_next"]), zpad], axis=1)
    w2b, a2b = _block_diag2(wts["w2"]), _block_diag2(wts["a2"])
    w0c, a0c = row(wts["w0"]), row(wts["a0"])
    g2p = jnp.concatenate([wts["g2"], jnp.zeros((GATE_PAD - GATE_LORA, RW), F32)], axis=0)
    k_k, k_a, r_k = row(wts["k_k"]), row(wts["k_a"]), row(wts["r_k"])
    lnx_g, lnx_b = row(wts["lnx_g"]), row(wts["lnx_b"])
    cdw, cb, clg, clb = wts["conv_dw"], row(wts["conv_b"]), row(wts["conv_ln_g"]), row(wts["conv_ln_b"])
    ln = {k: row(wts[k]) for k in ("ln1_g", "ln1_b", "ln2_g", "ln2_b", "ln3_g", "ln3_b")}
    small = (mu_p, mu_n, w2b, w0c, a2b, a0c, g2p, k_k, k_a)
    seq3 = lambda a: a.reshape(bsz, seq, a.shape[-1])
    flat = lambda a: a.reshape(n_tok, a.shape[-1])

    h1, act1 = _ffn_in(x0, w1i, tm=TM_FFN, name="ffn1_in")
    z1, x1, x1b = _mm_ln([act1], w1o, x0, ln["ln1_g"], ln["ln1_b"], 0.5, tm=TM_LN, name="ffn1_out_ln1")
    p = _matmul(x1b, win, name="proj_in")
    r, v, kk, w, kd, b, g = _mix_prep(p, *small, seq=seq, tt=tt, name="mix_prep")
    y0, y1, sp = _wkv_fwd(seq3(r), seq3(v), seq3(kk), seq3(w), seq3(kd), seq3(b), tb=tb, name="wkv_fwd")
    y0, y1 = flat(y0), flat(y1)
    yr = _mix_post(y0, y1, r, v, kd, g, lnx_g, lnx_b, r_k, tt=tt, name="mix_post")
    yc, yv = _conv_fwd(p, cdw, cb, clg, clb, seq=seq, tt=tt, name="conv_fwd")
    z2, x2, x2b = _mm_ln([yr, yv], wout, x1, ln["ln2_g"], ln["ln2_b"], 1.0, tm=TM_LN, name="proj_out_ln2")
    h2, act2 = _ffn_in(x2b, w2i, tm=TM_FFN, name="ffn2_in")
    z3, x3, _ = _mm_ln([act2], w2o, x2, ln["ln3_g"], ln["ln3_b"], 0.5, tm=TM_LN, name="ffn2_out_ln3")
    dx3, loss_part = _loss_fwd_bwd(x3, tgt, tt=tt, name="loss")

    gr = {}
    slab_rows = lambda a: a.reshape((N_CHIPS, a.shape[0] // N_CHIPS) + a.shape[1:])
    dw_kw = dict(ta=True, out_dtype=BF16)
    dz3, gr["ln3_g"], gr["ln3_b"] = _ln_bwd(z3, dx3, ln["ln3_g"], ln["ln3_b"], tt=tt, name="ln3_bwd")
    dh2 = _ffn_out_bwd(dz3, w2o, h2, tm=TM_FFN, name="ffn2_out_dx")
    gr["ffn2_w_out"] = slab_rows(_matmul(act2, dz3, scale=0.5, tm=D_FF // 2, name="ffn2_out_dw", **dw_kw))
    dx2 = _mm_nt_res([dh2], w2i, dz3, tm=TM_FFN, name="ffn2_in_dx")
    gr["ffn2_w_in"] = _matmul(x2b, dh2, col_slabs=True, tn=2 * D_FF // N_CHIPS, name="ffn2_in_dw", **dw_kw)
    dz2, gr["ln2_g"], gr["ln2_b"] = _ln_bwd(z2, dx2, ln["ln2_g"], ln["ln2_b"], tt=tt, name="ln2_bwd")
    dmix = _matmul(dz2, wout, tb=True, name="proj_out_dx")
    gr["w_out"] = slab_rows(jnp.concatenate([_matmul(yr, dz2, name="proj_out_dw_rwkv", **dw_kw),
                                             _matmul(yv, dz2, name="proj_out_dw_conv", **dw_kw)], axis=0))
    dyr, dyv = (dmix, RW, 0), (dmix, RW, 1)
    dy, dr_p, dv_p, dkd_p, dg, gr["lnx_g"], gr["lnx_b"], gr["r_k"] = _mix_post_bwd(
        y0, y1, r, v, kd, g, lnx_g, lnx_b, r_k, dyr, tt=tt, name="mix_post_bwd")
    scan_g = _wkv_bwd(seq3(r), seq3(v), seq3(kk), seq3(w), seq3(kd), seq3(b), seq3(dy), sp, tb=tb, name="wkv_bwd")
    cts = _scan_cotangents((dr_p, dv_p, dkd_p), [flat(a) for a in scan_g], tt=tt, name="scan_cts")
    dyc, gr["conv_ln_g"], gr["conv_ln_b"], gr["conv_b"] = _conv_post_bwd(yc, dyv, clg, clb, tt=tt, name="conv_post_bwd")
    dpc, ddw = _conv_bwd(dyc, p, cdw, seq=seq, tt=tt, name="conv_bwd")
    gr["conv_dw"] = ddw[:CONV_K]
    dps, dw2b, dw0c, da2b, da0c, dg2p, gr["k_k"], gr["k_a"] = _mix_prep_bwd(
        p, *small, [*cts, dg], seq=seq, tt=tt, name="mix_prep_bwd")
    gr["w2"] = jnp.stack([dw2b[:LORA, :RW], dw2b[LORA:, RW:]])
    gr["a2"] = jnp.stack([da2b[:LORA, :RW], da2b[LORA:, RW:]])
    gr["w0"], gr["a0"], gr["g2"] = dw0c.reshape(2, RW), da0c.reshape(2, RW), dg2p[:GATE_LORA]
    dpsh, dmu_p, dmu_n = _shift_bwd(dps, p, mu_p, mu_n, seq=seq, tt=tt, name="shift_bwd")
    gr["mu_prev"], gr["mu_next"] = dmu_p[:, :SHIFT_COLS], dmu_n[:, :SHIFT_COLS]
    dx1 = _mm_nt_res([dpsh, dpc], win, dz2, tm=TM_FFN, name="proj_in_dx")
    dwin = jnp.concatenate([_matmul(x1b, dpsh, name="proj_in_dw_shift", **dw_kw)[:, :SHIFT_COLS],
                            _matmul(x1b, dpc, name="proj_in_dw_conv", **dw_kw)], axis=1)
    gr["w_in"] = jnp.moveaxis(dwin.reshape(D_MODEL, N_CHIPS, IN_COLS // N_CHIPS), 1, 0)
    dz1, gr["ln1_g"], gr["ln1_b"] = _ln_bwd(z1, dx1, ln["ln1_g"], ln["ln1_b"], tt=tt, name="ln1_bwd")
    dh1 = _ffn_out_bwd(dz1, w1o, h1, tm=TM_FFN, name="ffn1_out_dx")
    gr["ffn1_w_out"] = slab_rows(_matmul(act1, dz1, scale=0.5, tm=D_FF // 2, name="ffn1_out_dw", **dw_kw))
    dx0 = _mm_nt_res([dh1], w1i, dz1, tm=TM_FFN, name="ffn1_in_dx")
    gr["ffn1_w_in"] = _matmul(x0, dh1, col_slabs=True, tn=2 * D_FF // N_CHIPS, name="ffn1_in_dw", **dw_kw)
    return loss_part, dx0.reshape(bsz, seq, D_MODEL), gr


def _mesh_pos():
    return lax.axis_index("x"), lax.axis_index("y"), lax.axis_index("c")


def _other_chips(x, y):
    return [(1 - x, y), (x, 1 - y), (1 - x, 1 - y)]


def _gather_chips(shards, *, name):
    n = len(shards)

    def body(*refs):
        ins, outs = refs[:n], refs[n:2 * n]
        send_sems, recv_sems, loc_sems = refs[2 * n:]
        x, y, c = _mesh_pos()
        q = 2 * x + y
        peers = _other_chips(x, y)
        local = [pltpu.make_async_copy(ins[a], outs[a].at[q], loc_sems.at[a]) for a in range(n)]
        for cp in local:
            cp.start()
        sends = [[pltpu.make_async_remote_copy(ins[a], outs[a].at[q], send_sems.at[a, k], recv_sems.at[a, k],
                                               device_id=(px, py, c), device_id_type=MESH)
                  for k, (px, py) in enumerate(peers)] for a in range(n)]
        for a in range(n):
            for cp in sends[a]:
                cp.start()
        for a in range(n):
            for k, (px, py) in enumerate(peers):
                pltpu.make_async_remote_copy(ins[a], outs[a].at[2 * px + py], send_sems.at[a, k], recv_sems.at[a, k],
                                             device_id=(px, py, c), device_id_type=MESH).wait_recv()
        for a in range(n):
            for cp in sends[a]:
                cp.wait_send()
            local[a].wait()

    any_spec = pl.BlockSpec(memory_space=pl.ANY)
    return pl.pallas_call(
        body, name=name,
        out_shape=[jax.ShapeDtypeStruct((N_CHIPS,) + s.shape, s.dtype) for s in shards],
        in_specs=[any_spec] * n, out_specs=[any_spec] * n,
        scratch_shapes=[pltpu.SemaphoreType.DMA((n, 3)), pltpu.SemaphoreType.DMA((n, 3)), pltpu.SemaphoreType.DMA((n,))],
        compiler_params=pltpu.CompilerParams(has_side_effects=True),
    )(*shards)


def _scatter_chips(stacks, *, name):
    n = len(stacks)

    def body(*refs):
        ins, outs = refs[:n], refs[n:2 * n]
        send_sems, recv_sems = refs[2 * n:]
        x, y, c = _mesh_pos()
        peers = _other_chips(x, y)
        sends = [[pltpu.make_async_remote_copy(ins[a].at[2 * px + py], outs[a].at[k], send_sems.at[a, k],
                                               recv_sems.at[a, k], device_id=(px, py, c), device_id_type=MESH)
                  for k, (px, py) in enumerate(peers)] for a in range(n)]
        for a in range(n):
            for cp in sends[a]:
                cp.start()
        for a in range(n):
            for cp in sends[a]:
                cp.wait_recv()
        for a in range(n):
            for cp in sends[a]:
                cp.wait_send()

    any_spec = pl.BlockSpec(memory_space=pl.ANY)
    return pl.pallas_call(
        body, name=name,
        out_shape=[jax.ShapeDtypeStruct((3,) + s.shape[1:], s.dtype) for s in stacks],
        in_specs=[any_spec] * n, out_specs=[any_spec] * n,
        scratch_shapes=[pltpu.SemaphoreType.DMA((n, 3)), pltpu.SemaphoreType.DMA((n, 3))],
        compiler_params=pltpu.CompilerParams(has_side_effects=True),
    )(*stacks)


def _swap_sibling(arrs, *, name):
    n = len(arrs)

    def body(*refs):
        ins, outs = refs[:n], refs[n:2 * n]
        send_sems, recv_sems = refs[2 * n:]
        x, y, c = _mesh_pos()
        cps = [pltpu.make_async_remote_copy(ins[a], outs[a], send_sems.at[a], recv_sems.at[a],
                                            device_id=(x, y, 1 - c), device_id_type=MESH) for a in range(n)]
        for cp in cps:
            cp.start()
        for cp in cps:
            cp.wait_recv()
        for cp in cps:
            cp.wait_send()

    any_spec = pl.BlockSpec(memory_space=pl.ANY)
    return pl.pallas_call(
        body, name=name,
        out_shape=[jax.ShapeDtypeStruct(s.shape, s.dtype) for s in arrs],
        in_specs=[any_spec] * n, out_specs=[any_spec] * n,
        scratch_shapes=[pltpu.SemaphoreType.DMA((n,)), pltpu.SemaphoreType.DMA((n,))],
        compiler_params=pltpu.CompilerParams(has_side_effects=True),
    )(*arrs)


def _all_reduce_rows(vec, *, name):
    rows = vec.shape[0]

    def body(v_ref, o_ref, land, send_sems, recv_sems):
        x, y, c = _mesh_pos()
        me = 4 * x + 2 * y + c
        land[me] = v_ref[...]
        cps = []
        for m in range(1, 8):
            mx, my, mc = (m >> 2) & 1, (m >> 1) & 1, m & 1
            tx, ty, tc = (x + mx) % 2, (y + my) % 2, (c + mc) % 2
            cps.append(pltpu.make_async_remote_copy(v_ref, land.at[me], send_sems.at[m - 1], recv_sems.at[me],
                                                    device_id=(tx, ty, tc), device_id_type=MESH))
        for cp in cps:
            cp.start()
        for m in range(1, 8):
            mx, my, mc = (m >> 2) & 1, (m >> 1) & 1, m & 1
            src = 4 * ((x + mx) % 2) + 2 * ((y + my) % 2) + (c + mc) % 2
            pltpu.make_async_remote_copy(v_ref, land.at[src], send_sems.at[m - 1], recv_sems.at[src],
                                         device_id=(x, y, c), device_id_type=MESH).wait_recv()
        for cp in cps:
            cp.wait_send()
        acc = land[0]
        for d in range(1, 8):
            acc = acc + land[d]
        o_ref[...] = acc

    vm = pl.BlockSpec(memory_space=pltpu.VMEM)
    return pl.pallas_call(
        body, name=name,
        out_shape=jax.ShapeDtypeStruct(vec.shape, F32),
        in_specs=[vm], out_specs=vm,
        scratch_shapes=[pltpu.VMEM((8, rows, LANES), F32), pltpu.SemaphoreType.DMA((7,)), pltpu.SemaphoreType.DMA((8,))],
        compiler_params=pltpu.CompilerParams(has_side_effects=True, vmem_limit_bytes=VMEM_LIMIT),
    )(vec)


def _adamw(w, g, m, v):
    m = ADAM_B1 * m + (1.0 - ADAM_B1) * g
    v = ADAM_B2 * v + (1.0 - ADAM_B2) * (g * g)
    m_hat = m / (1.0 - ADAM_B1 ** ADAM_STEP)
    v_hat = v / (1.0 - ADAM_B2 ** ADAM_STEP)
    delta = -ADAM_LR * (m_hat / (jnp.sqrt(v_hat) + ADAM_EPS) + ADAM_WD * w)
    return delta, m, v


def _sum4(mine, land, *, name):
    rows, cols = mine.shape
    tr = _pick_rows(rows)

    def body(a_ref, l_ref, o_ref):
        o_ref[...] = (a_ref[...].astype(F32) + l_ref[0].astype(F32)) + (l_ref[1].astype(F32) + l_ref[2].astype(F32))

    return pl.pallas_call(
        body, name=name, out_shape=jax.ShapeDtypeStruct((rows, cols), F32), grid=(rows // tr,),
        in_specs=[pl.BlockSpec((tr, cols), lambda i: (i, 0)), pl.BlockSpec((3, tr, cols), lambda i: (0, i, 0))],
        out_specs=pl.BlockSpec((tr, cols), lambda i: (i, 0)),
        compiler_params=_cparams(("parallel",)),
    )(mine, land)


def _pick_rows(rows, want=256):
    for t in range(min(want, rows) // 8 * 8, 0, -8):
        if rows % t == 0:
            return t
    return rows


def _sum_adam(h_mine, h_sib, w, m, v, *, name):
    rows, cols = w.shape
    tr = _pick_rows(rows)

    def body(a_ref, b_ref, w_ref, m_ref, v_ref, g_o, d_o, m_o, v_o):
        g = a_ref[...] + b_ref[...]
        d, mn, vn = _adamw(w_ref[...], g, m_ref[...], v_ref[...])
        g_o[...], d_o[...], m_o[...], v_o[...] = g, d, mn, vn

    spec = pl.BlockSpec((tr, cols), lambda i: (i, 0))
    return pl.pallas_call(
        body, name=name, out_shape=[jax.ShapeDtypeStruct((rows, cols), F32)] * 4, grid=(rows // tr,),
        in_specs=[spec] * 5, out_specs=[spec] * 4, compiler_params=_cparams(("parallel",)),
    )(h_mine, h_sib, w, m, v)


def _adam_rows(w, g, m, v, *, name):
    def body(w_ref, g_ref, m_ref, v_ref, d_o, m_o, v_o):
        d_o[...], m_o[...], v_o[...] = _adamw(w_ref[...], g_ref[...], m_ref[...], v_ref[...])

    vm = pl.BlockSpec(memory_space=pltpu.VMEM)
    return pl.pallas_call(
        body, name=name, out_shape=[jax.ShapeDtypeStruct(w.shape, F32)] * 3,
        in_specs=[vm] * 4, out_specs=[vm] * 3, compiler_params=_cparams(),
    )(w, g, m, v)


def _pack_rows(arrs):
    flat = jnp.concatenate([a.reshape(-1).astype(F32) for a in arrs])
    pad = -flat.shape[0] % (8 * LANES)
    return jnp.concatenate([flat, jnp.zeros((pad,), F32)]).reshape(-1, LANES)


def _unpack_rows(packed, shapes):
    flat = packed.reshape(-1)
    out, off = [], 0
    for s in shapes:
        size = 1
        for d in s:
            size *= d
        out.append(flat[off:off + size].reshape(s))
        off += size
    return out


WEIGHTS = ['ffn1_w_in', 'ffn1_w_out', 'w_in', 'mu_prev', 'mu_next', 'w0', 'w2', 'a0', 'a2', 'g2', 'k_k', 'k_a', 'r_k',
           'lnx_g', 'lnx_b', 'conv_dw', 'conv_b', 'conv_ln_g', 'conv_ln_b', 'w_out', 'ffn2_w_in', 'ffn2_w_out',
           'ln1_g', 'ln1_b', 'ln2_g', 'ln2_b', 'ln3_g', 'ln3_b']
COL_SHARDED = ('ffn1_w_in', 'w_in', 'ffn2_w_in')
ROW_SHARDED = ('ffn1_w_out', 'w_out', 'ffn2_w_out')
BIG = COL_SHARDED + ROW_SHARDED
SMALL_SHARDED = ('w0', 'w2', 'a0', 'a2', 'g2', 'conv_dw')
REPLICATED = tuple(n for n in WEIGHTS if n not in BIG + SMALL_SHARDED)


def _train_step(x, target, w, m, v, *, tt, tb):
    xi, yi, _ = _mesh_pos()
    q = 2 * xi + yi

    shards = [w[n][0].astype(BF16) for n in BIG] + [w[n][0] for n in SMALL_SHARDED]
    gathered = _gather_chips(shards, name="gather_weights")
    full = {}
    for n, gth in zip(BIG + SMALL_SHARDED, gathered):
        if n in ROW_SHARDED:
            full[n] = gth.reshape((-1,) + gth.shape[2:])
        elif n in ("ffn1_w_in", "ffn2_w_in"):
            full[n] = gth
        else:
            full[n] = jnp.moveaxis(gth, 0, -2).reshape(gth.shape[1:-1] + (N_CHIPS * gth.shape[-1],))
    for n in REPLICATED:
        full[n] = w[n][0]

    loss_part, grad_x, gr = _local_step(x, target, full, tt=tt, tb=tb)

    stacks = [gr[n] for n in BIG]
    landed = _scatter_chips(stacks, name="scatter_grads")
    halves = [_sum4(lax.dynamic_index_in_dim(s, q, 0, keepdims=False), l, name="sum4_" + n)
              for n, s, l in zip(BIG, stacks, landed)]
    sib = _swap_sibling(halves, name="swap_halves")
    grad, delta, new_m, new_v = {}, {}, {}, {}
    for n, h, hs in zip(BIG, halves, sib):
        outs = _sum_adam(h, hs, w[n][0], m[n][0], v[n][0], name="adam_" + n)
        grad[n], delta[n], new_m[n], new_v[n] = [o[None] for o in outs]

    small_names = REPLICATED + SMALL_SHARDED
    small_full_shapes = [full[n].shape for n in small_names]
    red = _all_reduce_rows(_pack_rows([gr[n] for n in small_names]), name="reduce_small")
    red = dict(zip(small_names, _unpack_rows(red, small_full_shapes)))
    gsm = {}
    for n in REPLICATED:
        gsm[n] = red[n].reshape(w[n].shape)
    for n in SMALL_SHARDED:
        width = w[n].shape[-1]
        gsm[n] = lax.dynamic_slice_in_dim(red[n], q * width, width, axis=red[n].ndim - 1).reshape(w[n].shape)
    shapes = [w[n].shape for n in small_names]
    d_p, m_p, v_p = _adam_rows(_pack_rows([w[n] for n in small_names]), _pack_rows([gsm[n] for n in small_names]),
                               _pack_rows([m[n] for n in small_names]), _pack_rows([v[n] for n in small_names]),
                               name="adam_small")
    for n, dd, mm, vv in zip(small_names, _unpack_rows(d_p, shapes), _unpack_rows(m_p, shapes), _unpack_rows(v_p, shapes)):
        grad[n], delta[n], new_m[n], new_v[n] = gsm[n], dd, mm, vv
    return loss_part, grad_x, grad, delta, new_m, new_v


def kernel(x, ffn1_w_in, ffn1_w_out, w_in, mu_prev, mu_next, w0, w2, a0, a2, g2, k_k, k_a, r_k, lnx_g, lnx_b, conv_dw, conv_b, conv_ln_g, conv_ln_b, w_out, ffn2_w_in, ffn2_w_out, ln1_g, ln1_b, ln2_g, ln2_b, ln3_g, ln3_b, loss_target, m_ffn1_w_in, m_ffn1_w_out, m_w_in, m_mu_prev, m_mu_next, m_w0, m_w2, m_a0, m_a2, m_g2, m_k_k, m_k_a, m_r_k, m_lnx_g, m_lnx_b, m_conv_dw, m_conv_b, m_conv_ln_g, m_conv_ln_b, m_w_out, m_ffn2_w_in, m_ffn2_w_out, m_ln1_g, m_ln1_b, m_ln2_g, m_ln2_b, m_ln3_g, m_ln3_b, v_ffn1_w_in, v_ffn1_w_out, v_w_in, v_mu_prev, v_mu_next, v_w0, v_w2, v_a0, v_a2, v_g2, v_k_k, v_k_a, v_r_k, v_lnx_g, v_lnx_b, v_conv_dw, v_conv_b, v_conv_ln_g, v_conv_ln_b, v_w_out, v_ffn2_w_in, v_ffn2_w_out, v_ln1_g, v_ln1_b, v_ln2_g, v_ln2_b, v_ln3_g, v_ln3_b):
    args = dict(locals())
    w = {n: args[n] for n in WEIGHTS}
    m = {n: args["m_" + n] for n in WEIGHTS}
    v = {n: args["v_" + n] for n in WEIGHTS}
    seq = x.shape[1]
    loss_part, grad_x, grad, delta, new_m, new_v = _train_step(x, loss_target, w, m, v, tt=min(256, seq), tb=8)
    loss = lax.psum(loss_part[0, 0], ("x", "y", "c"))
    return (loss, grad_x, *[grad[n] for n in WEIGHTS], *[delta[n] for n in WEIGHTS],
            *[new_m[n] for n in WEIGHTS], *[new_v[n] for n in WEIGHTS])
```

```python
import functools

import jax
import jax.numpy as jnp
from jax import lax
from jax.experimental import pallas as pl
from jax.experimental.pallas import tpu as pltpu

F32 = jnp.float32
BF16 = jnp.bfloat16

D_MODEL = 1024
RW = 512
HEAD = 64
CW = 512
CONV_K = 31
CONV_PAD = 15
D_FF = 2816
LORA = 64
GATE_LORA = 160
GATE_PAD = 256
SHIFT_COLS = 1952
SHIFT_PAD = 2048
IN_COLS = 2976
IN_PAD = 3072
LN_EPS = 1e-5
GN_EPS = 64e-5
NORM_EPS = 1e-12
ALPHA = 2.0 ** 0.25
DECAY_SCALE = 0.6065306597126334
ADAM_LR, ADAM_B1, ADAM_B2, ADAM_EPS, ADAM_WD, ADAM_STEP = 0.001, 0.9, 0.999, 1e-08, 0.01, 10
N_CHIPS = 4
VMEM_LIMIT = 56 * 1024 * 1024
TM_FFN = 256
TM_LN = 512

MESH = pl.DeviceIdType.MESH


def _cparams(sem=None, **kw):
    return pltpu.CompilerParams(dimension_semantics=sem, vmem_limit_bytes=VMEM_LIMIT, **kw)


LANES = 128


def _pick_tile(dim, want):
    for t in range(min(want, dim) // LANES * LANES, 0, -LANES):
        if dim % t == 0:
            return t
    return dim


def _matmul(a, b, *, ta=False, tb=False, out_dtype=F32, tm=1024, tn=1024, tk=1024, scale=1.0, col_slabs=False, name):
    if ta:
        k_dim, m_dim = a.shape
    else:
        m_dim, k_dim = a.shape
    n_dim = b.shape[0] if tb else b.shape[1]
    tm, tn, tk = _pick_tile(m_dim, tm), _pick_tile(n_dim, tn), _pick_tile(k_dim, tk)
    assert m_dim % tm == 0 and n_dim % tn == 0 and k_dim % tk == 0, (name, a.shape, b.shape, tm, tn, tk)
    nk = k_dim // tk
    dims = (((0,) if ta else (1,), (1,) if tb else (0,)), ((), ()))
    if col_slabs:
        out_shape = jax.ShapeDtypeStruct((n_dim // tn, m_dim, tn), out_dtype)
        out_spec = pl.BlockSpec((None, tm, tn), lambda i, j, k: (j, i, 0))
    else:
        out_shape = jax.ShapeDtypeStruct((m_dim, n_dim), out_dtype)
        out_spec = pl.BlockSpec((tm, tn), lambda i, j, k: (i, j))

    def body(a_ref, b_ref, o_ref, acc_ref):
        kk = pl.program_id(2)

        @pl.when(kk == 0)
        def _():
            acc_ref[...] = jnp.zeros_like(acc_ref)

        acc_ref[...] += lax.dot_general(a_ref[...].astype(BF16), b_ref[...].astype(BF16), dims,
                                        preferred_element_type=F32)

        @pl.when(kk == nk - 1)
        def _():
            o_ref[...] = (acc_ref[...] * scale).astype(o_ref.dtype)

    a_spec = pl.BlockSpec((tk, tm), lambda i, j, k: (k, i)) if ta else pl.BlockSpec((tm, tk), lambda i, j, k: (i, k))
    b_spec = pl.BlockSpec((tn, tk), lambda i, j, k: (j, k)) if tb else pl.BlockSpec((tk, tn), lambda i, j, k: (k, j))
    return pl.pallas_call(
        body, name=name,
        out_shape=out_shape,
        grid=(m_dim // tm, n_dim // tn, nk),
        in_specs=[a_spec, b_spec],
        out_specs=out_spec,
        scratch_shapes=[pltpu.VMEM((tm, tn), F32)],
        compiler_params=_cparams(("parallel", "parallel", "arbitrary")),
    )(a, b)


def _whole(shape):
    nd = len(shape)
    return pl.BlockSpec(shape, lambda i: (0,) * nd)


def _ffn_in(x, w, *, tm, name):
    n_tok = x.shape[0]
    sw = w.shape[2]
    tm = min(tm, n_tok)

    def body(x_ref, w_ref, h_ref, a_ref):
        xb = x_ref[...].astype(BF16)
        for s in range(2):
            g = jnp.dot(xb, w_ref[s], preferred_element_type=F32)
            u = jnp.dot(xb, w_ref[s + 2], preferred_element_type=F32)
            h_ref[:, s * sw:(s + 1) * sw] = g.astype(BF16)
            h_ref[:, (s + 2) * sw:(s + 3) * sw] = u.astype(BF16)
            a_ref[:, s * sw:(s + 1) * sw] = (_silu(g) * u).astype(BF16)

    return pl.pallas_call(
        body, name=name,
        out_shape=[jax.ShapeDtypeStruct((n_tok, 2 * D_FF), BF16), jax.ShapeDtypeStruct((n_tok, D_FF), BF16)],
        grid=(n_tok // tm,),
        in_specs=[pl.BlockSpec((tm, D_MODEL), lambda i: (i, 0)), _whole(w.shape)],
        out_specs=[pl.BlockSpec((tm, 2 * D_FF), lambda i: (i, 0)), pl.BlockSpec((tm, D_FF), lambda i: (i, 0))],
        compiler_params=_cparams(("parallel",)),
    )(x, w)


def _mm_ln(a_list, w, xres, g, b, fscale, *, tm, name):
    n_tok = xres.shape[0]
    tm = min(tm, n_tok)
    na = len(a_list)

    def body(*refs):
        a_refs = refs[:na]
        w_ref, x_ref, g_ref, b_ref, z_o, y_o, yb_o = refs[na:]
        f, off = None, 0
        for a_ref in a_refs:
            k = a_ref.shape[1]
            t = jnp.dot(a_ref[...].astype(BF16), w_ref[off:off + k, :], preferred_element_type=F32)
            f = t if f is None else f + t
            off += k
        z = ALPHA * x_ref[...] + fscale * f
        y = _layer_norm(z, g_ref[...], b_ref[...])
        z_o[...] = z
        y_o[...] = y
        yb_o[...] = y.astype(BF16)

    tile = pl.BlockSpec((tm, D_MODEL), lambda i: (i, 0))
    return pl.pallas_call(
        body, name=name,
        out_shape=[jax.ShapeDtypeStruct((n_tok, D_MODEL), F32)] * 2 + [jax.ShapeDtypeStruct((n_tok, D_MODEL), BF16)],
        grid=(n_tok // tm,),
        in_specs=[pl.BlockSpec((tm, a.shape[1]), lambda i: (i, 0)) for a in a_list]
        + [_whole(w.shape), tile, _whole(g.shape), _whole(b.shape)],
        out_specs=[tile, tile, tile],
        compiler_params=_cparams(("parallel",)),
    )(*a_list, w, xres, g, b)


def _ffn_out_bwd(dz, w, h, *, tm, name):
    n_tok = dz.shape[0]
    tm = min(tm, n_tok)
    cw = D_FF // 2

    def body(dz_ref, w_ref, h_ref, dh_ref):
        dzb = dz_ref[...].astype(BF16)
        for s in range(2):
            dact = 0.5 * lax.dot_general(dzb, w_ref[s * cw:(s + 1) * cw, :], (((1,), (1,)), ((), ())),
                                         preferred_element_type=F32)
            gate = h_ref[:, s * cw:(s + 1) * cw].astype(F32)
            up = h_ref[:, D_FF + s * cw:D_FF + (s + 1) * cw].astype(F32)
            sg = _sigmoid(gate)
            dh_ref[:, s * cw:(s + 1) * cw] = (dact * up * sg * (1.0 + gate * (1.0 - sg))).astype(BF16)
            dh_ref[:, D_FF + s * cw:D_FF + (s + 1) * cw] = (dact * gate * sg).astype(BF16)

    wide = pl.BlockSpec((tm, 2 * D_FF), lambda i: (i, 0))
    return pl.pallas_call(
        body, name=name,
        out_shape=jax.ShapeDtypeStruct((n_tok, 2 * D_FF), BF16),
        grid=(n_tok // tm,),
        in_specs=[pl.BlockSpec((tm, D_MODEL), lambda i: (i, 0)), _whole(w.shape), wide],
        out_specs=wide,
        compiler_params=_cparams(("parallel",)),
    )(dz, w, h)


def _mm_nt_res(a_list, w, dz, *, tm, name):
    n_tok = dz.shape[0]
    tm = min(tm, n_tok)
    na = len(a_list)
    nt = (((1,), (1,)), ((), ()))

    def body(*refs):
        a_refs = refs[:na]
        w_ref, dz_ref, o_ref = refs[na:]
        acc = ALPHA * dz_ref[...]
        if len(w_ref.shape) == 3:
            cw = w_ref.shape[2]
            for s in range(w_ref.shape[0]):
                acc = acc + lax.dot_general(a_refs[0][:, s * cw:(s + 1) * cw], w_ref[s], nt, preferred_element_type=F32)
        else:
            off = 0
            for a_ref in a_refs:
                k = a_ref.shape[1]
                acc = acc + lax.dot_general(a_ref[...], w_ref[:, off:off + k], nt, preferred_element_type=F32)
                off += k
        o_ref[...] = acc

    tile = pl.BlockSpec((tm, D_MODEL), lambda i: (i, 0))
    return pl.pallas_call(
        body, name=name,
        out_shape=jax.ShapeDtypeStruct((n_tok, D_MODEL), F32),
        grid=(n_tok // tm,),
        in_specs=[pl.BlockSpec((tm, a.shape[1]), lambda i: (i, 0)) for a in a_list] + [_whole(w.shape), tile],
        out_specs=tile,
        compiler_params=_cparams(("parallel",)),
    )(*a_list, w, dz)


def _rowcall(fn, tok_in, full_in, tok_out, acc_out, *, tt, name):
    views = [a if isinstance(a, tuple) else (a, a.shape[1], 0) for a in tok_in]
    tok_in = [a for a, _, _ in views]
    n_tok = tok_in[0].shape[0]
    assert n_tok % tt == 0, (name, n_tok, tt)
    n_ti, n_fi, n_to = len(tok_in), len(full_in), len(tok_out)

    def body(*refs):
        i = pl.program_id(0)
        ins = [r[...] for r in refs[:n_ti + n_fi]]
        outs = fn(i, *ins)
        o_refs = refs[n_ti + n_fi:]
        for r, val in zip(o_refs[:n_to], outs[:n_to]):
            r[...] = val.astype(r.dtype)
        if acc_out:
            @pl.when(i == 0)
            def _():
                for r in o_refs[n_to:]:
                    r[...] = jnp.zeros_like(r)
            for r, val in zip(o_refs[n_to:], outs[n_to:]):
                r[...] += val.reshape(r.shape).astype(F32)

    in_specs = [pl.BlockSpec((tt, width), functools.partial(lambda k, i: (i, k), k)) for _, width, k in views]
    in_specs += [pl.BlockSpec(a.shape, lambda i: (0, 0)) for a in full_in]
    out_specs = [pl.BlockSpec((tt, c), lambda i: (i, 0)) for c, _ in tok_out]
    out_specs += [pl.BlockSpec(s, lambda i: (0, 0)) for s in acc_out]
    out_shape = [jax.ShapeDtypeStruct((n_tok, c), dt) for c, dt in tok_out]
    out_shape += [jax.ShapeDtypeStruct(s, F32) for s in acc_out]
    return pl.pallas_call(
        body, name=name, out_shape=out_shape, grid=(n_tok // tt,), in_specs=in_specs, out_specs=out_specs,
        compiler_params=_cparams(("arbitrary",) if acc_out else ("parallel",)),
    )(*tok_in, *full_in)


@jax.custom_vjp
def _bdot(a, b):
    return jnp.dot(a.astype(BF16), b.astype(BF16), preferred_element_type=F32)


def _bdot_fwd(a, b):
    return _bdot(a, b), (a, b)


def _bdot_bwd(res, g):
    a, b = res
    g16 = g.astype(BF16)
    da = lax.dot_general(g16, b.astype(BF16), (((1,), (1,)), ((), ())), preferred_element_type=F32)
    db = lax.dot_general(a.astype(BF16), g16, (((0,), (0,)), ((), ())), preferred_element_type=F32)
    return da, db


_bdot.defvjp(_bdot_fwd, _bdot_bwd)


def _split16(x):
    hi = x.astype(BF16)
    lo = (x - hi.astype(F32)).astype(BF16)
    return hi, lo


def _segsum_raw(x, e2):
    hi, lo = _split16(x)
    outs = []
    for c in range(x.shape[1] // 256):
        lhs = jnp.concatenate([hi[:, 256 * c:256 * (c + 1)], lo[:, 256 * c:256 * (c + 1)]], axis=1)
        outs.append(jnp.dot(lhs, e2, preferred_element_type=F32))
    return jnp.concatenate(outs, axis=1)


@jax.custom_vjp
def _segsum(x, e2):
    return _segsum_raw(x, e2)


def _segsum_fwd(x, e2):
    return _segsum_raw(x, e2), e2


def _segsum_bwd(e2, g):
    return _segsum_raw(g, e2), jnp.zeros_like(e2)


_segsum.defvjp(_segsum_fwd, _segsum_bwd)


def _head_ones():
    r = lax.broadcasted_iota(jnp.int32, (512, 256), 0) % 256
    c = lax.broadcasted_iota(jnp.int32, (512, 256), 1)
    return (r // HEAD == c // HEAD).astype(BF16)


def _sigmoid(x):
    return 1.0 / (1.0 + jnp.exp(-x))


def _silu(x):
    return x * _sigmoid(x)


def _layer_norm(z, g, b, eps=LN_EPS):
    mu = jnp.mean(z, axis=-1, keepdims=True)
    zc = z - mu
    var = jnp.mean(zc * zc, axis=-1, keepdims=True)
    return zc * lax.rsqrt(var + eps) * g + b


def _prep(ps, w2b, w0c, a2b, a0c, g2p, k_k, k_a, e2):
    r, k, v = ps[:, 0:512], ps[:, 512:1024], ps[:, 1024:1536]
    wd, ad, gd = ps[:, 1536:1664], ps[:, 1664:1792], ps[:, 1792:2048]
    lw = _bdot(jnp.tanh(wd), w2b) + w0c
    decay = jnp.exp(-DECAY_SCALE * _sigmoid(lw))
    a = _sigmoid(_bdot(ad, a2b) + a0c)
    g = _bdot(_sigmoid(gd), g2p)
    kkr = k * k_k
    nrm = jnp.sqrt(_segsum(kkr * kkr, e2))
    kk = kkr / jnp.maximum(nrm, NORM_EPS)
    k2 = jnp.concatenate([k, k], axis=1)
    ka2 = jnp.concatenate([k_a, k_a], axis=1)
    kd = k2 * (1.0 + (a - 1.0) * ka2)
    b = jnp.concatenate([kk, kk], axis=1) * a
    return r, v, kk, decay, kd, b, g


def _post(y0, y1, r, v, kd, g, lnx_g, lnx_b, r_k, e2):
    y = y0 + y1
    mu = _segsum(y, e2) * (1.0 / HEAD)
    yc = y - mu
    var = _segsum(yc * yc, e2) * (1.0 / HEAD)
    yn = yc * lax.rsqrt(var + GN_EPS) * lnx_g + lnx_b
    bonus = _segsum(r * (kd[:, :RW] + kd[:, RW:]) * r_k, e2)
    return (yn + bonus * v) * g


def _conv_post(yc, ln_g, ln_b):
    return _silu(_layer_norm(yc, ln_g, ln_b))


def _ln_bwd(z, dy, g, b, *, tt, name):
    def fn(i, zv, dv, gv, bv):
        _, vjp = jax.vjp(_layer_norm, zv, gv, bv)
        return vjp(dv)
    return _rowcall(fn, [z, dy], [g, b], [(D_MODEL, F32)], [(1, D_MODEL), (1, D_MODEL)], tt=tt, name=name)


def _loss_fwd_bwd(y, target, *, tt, name):
    def fn(i, yv, tv):
        e = yv - tv
        part = 0.5 * jnp.sum(jnp.mean(e * e, axis=-1, keepdims=True), axis=0, keepdims=True)
        return e * (1.0 / D_MODEL), jnp.broadcast_to(part, (8, 128))
    return _rowcall(fn, [y, target], [], [(D_MODEL, F32)], [(8, 128)], tt=tt, name=name)


def _halo_specs(cols_block, hb, tt, n_tok, col_idx):
    nb = n_tok // hb
    prev = pl.BlockSpec((hb, cols_block), lambda i: (jnp.maximum(i * (tt // hb) - 1, 0), col_idx))
    nxt = pl.BlockSpec((hb, cols_block), lambda i: (jnp.minimum((i + 1) * (tt // hb), nb - 1), col_idx))
    return prev, nxt


def _mix_prep(p, mu_p, mu_n, w2b, w0c, a2b, a0c, g2p, k_k, k_a, *, seq, tt, name):
    n_tok = p.shape[0]
    tps = seq // tt
    e2 = _head_ones()

    def body(p_ref, hp_ref, hn_ref, mup_ref, mun_ref, w2b_ref, w0c_ref, a2b_ref, a0c_ref, g2p_ref, kk_ref, ka_ref,
             e2_ref, r_o, v_o, kk_o, w_o, kd_o, b_o, g_o, ext):
        i = pl.program_id(0)
        first = (i % tps) == 0
        last = (i % tps) == tps - 1
        pv = p_ref[...]
        ext[pl.ds(0, 8), :] = jnp.where(first, 0.0, hp_ref[...])
        ext[pl.ds(8, tt), :] = pv
        ext[pl.ds(8 + tt, 8), :] = jnp.where(last, 0.0, hn_ref[...])
        prev = ext[pl.ds(7, tt), :]
        nxt = ext[pl.ds(9, tt), :]
        ps = pv + mup_ref[...] * (prev - pv) + mun_ref[...] * (nxt - pv)
        outs = _prep(ps, w2b_ref[...], w0c_ref[...], a2b_ref[...], a0c_ref[...], g2p_ref[...], kk_ref[...],
                     ka_ref[...], e2_ref[...])
        for o_ref, val in zip((r_o, v_o, kk_o, w_o, kd_o, b_o, g_o), outs):
            o_ref[...] = val

    hp, hn = _halo_specs(SHIFT_PAD, 8, tt, n_tok, 0)
    fulls = [mu_p, mu_n, w2b, w0c, a2b, a0c, g2p, k_k, k_a, e2]
    widths = (RW, RW, RW, 2 * RW, 2 * RW, 2 * RW, RW)
    return pl.pallas_call(
        body, name=name,
        out_shape=[jax.ShapeDtypeStruct((n_tok, c), F32) for c in widths],
        grid=(n_tok // tt,),
        in_specs=[pl.BlockSpec((tt, SHIFT_PAD), lambda i: (i, 0)), hp, hn]
        + [pl.BlockSpec(a.shape, lambda i: (0, 0)) for a in fulls],
        out_specs=[pl.BlockSpec((tt, c), lambda i: (i, 0)) for c in widths],
        scratch_shapes=[pltpu.VMEM((tt + 16, SHIFT_PAD), F32)],
        compiler_params=_cparams(("parallel",)),
    )(p, p, p, *fulls)


def _mix_prep_bwd(p, mu_p, mu_n, w2b, w0c, a2b, a0c, g2p, k_k, k_a, cts, *, seq, tt, name):
    n_tok = p.shape[0]
    tps = seq // tt
    e2 = _head_ones()
    acc_shapes = [w2b.shape, w0c.shape, a2b.shape, a0c.shape, g2p.shape, k_k.shape, k_a.shape]

    def body(p_ref, hp_ref, hn_ref, mup_ref, mun_ref, w2b_ref, w0c_ref, a2b_ref, a0c_ref, g2p_ref, kk_ref, ka_ref,
             e2_ref, dr, dv, dkk, dw, dkd, db, dg, dps_o, *rest):
        acc_refs, ext = rest[:-1], rest[-1]
        i = pl.program_id(0)
        first = (i % tps) == 0
        last = (i % tps) == tps - 1
        pv = p_ref[...]
        ext[pl.ds(0, 8), :] = jnp.where(first, 0.0, hp_ref[...])
        ext[pl.ds(8, tt), :] = pv
        ext[pl.ds(8 + tt, 8), :] = jnp.where(last, 0.0, hn_ref[...])
        prev = ext[pl.ds(7, tt), :]
        nxt = ext[pl.ds(9, tt), :]
        ps = pv + mup_ref[...] * (prev - pv) + mun_ref[...] * (nxt - pv)
        e2v = e2_ref[...]
        _, vjp = jax.vjp(lambda *a: _prep(*a, e2v), ps, w2b_ref[...], w0c_ref[...], a2b_ref[...], a0c_ref[...],
                         g2p_ref[...], kk_ref[...], ka_ref[...])
        grads = vjp((dr[...], dv[...], dkk[...], dw[...], dkd[...], db[...], dg[...]))
        dps_o[...] = grads[0]

        @pl.when(i == 0)
        def _():
            for r in acc_refs:
                r[...] = jnp.zeros_like(r)
        for r, val in zip(acc_refs, grads[1:]):
            r[...] += val

    hp, hn = _halo_specs(SHIFT_PAD, 8, tt, n_tok, 0)
    fulls = [mu_p, mu_n, w2b, w0c, a2b, a0c, g2p, k_k, k_a, e2]
    return pl.pallas_call(
        body, name=name,
        out_shape=[jax.ShapeDtypeStruct((n_tok, SHIFT_PAD), F32)] + [jax.ShapeDtypeStruct(s, F32) for s in acc_shapes],
        grid=(n_tok // tt,),
        in_specs=[pl.BlockSpec((tt, SHIFT_PAD), lambda i: (i, 0)), hp, hn]
        + [pl.BlockSpec(a.shape, lambda i: (0, 0)) for a in fulls]
        + [pl.BlockSpec((tt, c.shape[1]), lambda i: (i, 0)) for c in cts],
        out_specs=[pl.BlockSpec((tt, SHIFT_PAD), lambda i: (i, 0))] + [pl.BlockSpec(s, lambda i: (0, 0)) for s in acc_shapes],
        scratch_shapes=[pltpu.VMEM((tt + 16, SHIFT_PAD), F32)],
        compiler_params=_cparams(("arbitrary",)),
    )(p, p, p, *fulls, *cts)


def _shift_bwd(dps, p, mu_p, mu_n, *, seq, tt, name):
    n_tok = p.shape[0]
    tps = seq // tt

    def body(d_ref, dhp_ref, dhn_ref, p_ref, php_ref, phn_ref, mup_ref, mun_ref, dp_o, dmup_o, dmun_o, ext):
        i = pl.program_id(0)
        first = (i % tps) == 0
        last = (i % tps) == tps - 1
        mup, mun = mup_ref[...], mun_ref[...]
        dv = d_ref[...]
        pv = p_ref[...]
        ext[pl.ds(0, 8), :] = jnp.where(first, 0.0, dhp_ref[...])
        ext[pl.ds(8, tt), :] = dv
        ext[pl.ds(8 + tt, 8), :] = jnp.where(last, 0.0, dhn_ref[...])
        d_prev = ext[pl.ds(7, tt), :]
        d_next = ext[pl.ds(9, tt), :]
        dp_o[...] = (dv * (1.0 - mup - mun) + d_next * mup + d_prev * mun).astype(dp_o.dtype)
        ext[pl.ds(0, 8), :] = jnp.where(first, 0.0, php_ref[...])
        ext[pl.ds(8, tt), :] = pv
        ext[pl.ds(8 + tt, 8), :] = jnp.where(last, 0.0, phn_ref[...])
        p_prev = ext[pl.ds(7, tt), :]
        p_next = ext[pl.ds(9, tt), :]

        @pl.when(i == 0)
        def _():
            dmup_o[...] = jnp.zeros_like(dmup_o)
            dmun_o[...] = jnp.zeros_like(dmun_o)
        dmup_o[...] += jnp.sum(dv * (p_prev - pv), axis=0, keepdims=True)
        dmun_o[...] += jnp.sum(dv * (p_next - pv), axis=0, keepdims=True)

    hp, hn = _halo_specs(SHIFT_PAD, 8, tt, n_tok, 0)
    tile = pl.BlockSpec((tt, SHIFT_PAD), lambda i: (i, 0))
    full = pl.BlockSpec((1, SHIFT_PAD), lambda i: (0, 0))
    return pl.pallas_call(
        body, name=name,
        out_shape=[jax.ShapeDtypeStruct((n_tok, SHIFT_PAD), BF16), jax.ShapeDtypeStruct((1, SHIFT_PAD), F32),
                   jax.ShapeDtypeStruct((1, SHIFT_PAD), F32)],
        grid=(n_tok // tt,),
        in_specs=[tile, hp, hn, tile, hp, hn, full, full],
        out_specs=[tile, full, full],
        scratch_shapes=[pltpu.VMEM((tt + 16, SHIFT_PAD), F32)],
        compiler_params=_cparams(("arbitrary",)),
    )(dps, dps, dps, p, p, p, mu_p, mu_n)


def _mix_post(y0, y1, r, v, kd, g, lnx_g, lnx_b, r_k, *, tt, name):
    e2 = _head_ones()
    return _rowcall(lambda i, *a: (_post(*a),), [y0, y1, r, v, kd, g], [lnx_g, lnx_b, r_k, e2], [(RW, BF16)], [],
                    tt=tt, name=name)[0]


def _mix_post_bwd(y0, y1, r, v, kd, g, lnx_g, lnx_b, r_k, dout, *, tt, name):
    e2 = _head_ones()

    def fn(i, y0v, y1v, rv, vv, kdv, gv, dov, lg, lb, rk, e2v):
        _, vjp = jax.vjp(lambda *a: _post(*a, e2v), y0v, y1v, rv, vv, kdv, gv, lg, lb, rk)
        gr = vjp(dov.astype(F32))
        return gr[0], gr[2], gr[3], gr[4], gr[5], gr[6], gr[7], gr[8]
    return _rowcall(fn, [y0, y1, r, v, kd, g, dout], [lnx_g, lnx_b, r_k, e2],
                    [(RW, F32), (RW, F32), (RW, F32), (2 * RW, F32), (RW, F32)], [(1, RW), (1, RW), (1, RW)],
                    tt=tt, name=name)


def _conv_fwd(p, dw, db, ln_g, ln_b, *, seq, tt, name):
    n_tok = p.shape[0]
    tps = seq // tt

    def glu(x, gate):
        return x * _sigmoid(gate)

    def body(u_ref, g_ref, uhp, ghp, uhn, ghn, dw_ref, db_ref, lg_ref, lb_ref, yc_o, y_o, ext):
        i = pl.program_id(0)
        first = (i % tps) == 0
        last = (i % tps) == tps - 1
        ext[pl.ds(0, 16), :] = jnp.where(first, 0.0, glu(uhp[...], ghp[...]))
        ext[pl.ds(16, tt), :] = glu(u_ref[...], g_ref[...])
        ext[pl.ds(16 + tt, 16), :] = jnp.where(last, 0.0, glu(uhn[...], ghn[...]))
        acc = jnp.zeros((tt, CW), F32) + db_ref[...]
        for k in range(CONV_K):
            acc = acc + ext[pl.ds(k + 1, tt), :] * dw_ref[pl.ds(k, 1), :]
        yc_o[...] = acc
        y_o[...] = _conv_post(acc, lg_ref[...], lb_ref[...]).astype(y_o.dtype)

    uhp_s, uhn_s = _halo_specs(CW, 16, tt, n_tok, 4)
    ghp_s, ghn_s = _halo_specs(CW, 16, tt, n_tok, 5)
    fulls = [dw, db, ln_g, ln_b]
    return pl.pallas_call(
        body, name=name,
        out_shape=[jax.ShapeDtypeStruct((n_tok, CW), F32), jax.ShapeDtypeStruct((n_tok, CW), BF16)],
        grid=(n_tok // tt,),
        in_specs=[pl.BlockSpec((tt, CW), lambda i: (i, 4)), pl.BlockSpec((tt, CW), lambda i: (i, 5)),
                  uhp_s, ghp_s, uhn_s, ghn_s] + [pl.BlockSpec(a.shape, lambda i: (0, 0)) for a in fulls],
        out_specs=[pl.BlockSpec((tt, CW), lambda i: (i, 0)), pl.BlockSpec((tt, CW), lambda i: (i, 0))],
        scratch_shapes=[pltpu.VMEM((tt + 32, CW), F32)],
        compiler_params=_cparams(("parallel",)),
    )(p, p, p, p, p, p, *fulls)


def _conv_post_bwd(yc, dy, ln_g, ln_b, *, tt, name):
    def fn(i, ycv, dyv, lg, lb):
        _, vjp = jax.vjp(_conv_post, ycv, lg, lb)
        dyc, dg, dbb = vjp(dyv.astype(F32))
        return dyc, dg, dbb, jnp.sum(dyc, axis=0, keepdims=True)
    return _rowcall(fn, [yc, dy], [ln_g, ln_b], [(CW, F32)], [(1, CW), (1, CW), (1, CW)], tt=tt, name=name)


def _conv_bwd(dyc, p, dw, *, seq, tt, name):
    n_tok = p.shape[0]
    tps = seq // tt

    def body(d_ref, dhp, dhn, u_ref, g_ref, uhp, ghp, uhn, ghn, dw_ref, dp_o, ddw_o, ext):
        i = pl.program_id(0)
        first = (i % tps) == 0
        last = (i % tps) == tps - 1
        dv = d_ref[...]
        ext[pl.ds(0, 16), :] = jnp.where(first, 0.0, dhp[...])
        ext[pl.ds(16, tt), :] = dv
        ext[pl.ds(16 + tt, 16), :] = jnp.where(last, 0.0, dhn[...])
        du = jnp.zeros((tt, CW), F32)
        for k in range(CONV_K):
            du = du + ext[pl.ds(31 - k, tt), :] * dw_ref[pl.ds(k, 1), :]
        uv, gv = u_ref[...], g_ref[...]
        sg = _sigmoid(gv)
        dp_o[:, 0:CW] = (du * sg).astype(dp_o.dtype)
        dp_o[:, CW:2 * CW] = (du * uv * sg * (1.0 - sg)).astype(dp_o.dtype)
        ext[pl.ds(0, 16), :] = jnp.where(first, 0.0, uhp[...] * _sigmoid(ghp[...]))
        ext[pl.ds(16, tt), :] = uv * sg
        ext[pl.ds(16 + tt, 16), :] = jnp.where(last, 0.0, uhn[...] * _sigmoid(ghn[...]))

        @pl.when(i == 0)
        def _():
            ddw_o[...] = jnp.zeros_like(ddw_o)
        for k in range(CONV_K):
            ddw_o[pl.ds(k, 1), :] += jnp.sum(dv * ext[pl.ds(k + 1, tt), :], axis=0, keepdims=True)

    dhp_s, dhn_s = _halo_specs(CW, 16, tt, n_tok, 0)
    uhp_s, uhn_s = _halo_specs(CW, 16, tt, n_tok, 4)
    ghp_s, ghn_s = _halo_specs(CW, 16, tt, n_tok, 5)
    return pl.pallas_call(
        body, name=name,
        out_shape=[jax.ShapeDtypeStruct((n_tok, 2 * CW), BF16), jax.ShapeDtypeStruct((32, CW), F32)],
        grid=(n_tok // tt,),
        in_specs=[pl.BlockSpec((tt, CW), lambda i: (i, 0)), dhp_s, dhn_s,
                  pl.BlockSpec((tt, CW), lambda i: (i, 4)), pl.BlockSpec((tt, CW), lambda i: (i, 5)),
                  uhp_s, ghp_s, uhn_s, ghn_s, pl.BlockSpec(dw.shape, lambda i: (0, 0))],
        out_specs=[pl.BlockSpec((tt, 2 * CW), lambda i: (i, 0)), pl.BlockSpec((32, CW), lambda i: (0, 0))],
        scratch_shapes=[pltpu.VMEM((tt + 32, CW), F32)],
        compiler_params=_cparams(("arbitrary",)),
    )(dyc, dyc, dyc, p, p, p, p, p, p, dw)


def _segdot(hi, lo, e2):
    outs = []
    for c in range(hi.shape[1] // 256):
        lhs = jnp.concatenate([hi[:, 256 * c:256 * (c + 1)], lo[:, 256 * c:256 * (c + 1)]], axis=1)
        outs.append(jnp.dot(lhs, e2, preferred_element_type=F32))
    return jnp.concatenate(outs, axis=1)


SCAN_PASSES = 1


def _seg_streams(parts, e2):
    if SCAN_PASSES == 1:
        hi = jnp.concatenate([p.astype(BF16) for p in parts], axis=0)
        full = jnp.concatenate([jnp.dot(hi[:, 256 * c:256 * (c + 1)], e2[:256], preferred_element_type=F32)
                                for c in range(RW // 256)], axis=1)
    else:
        pieces = [_split16(p) for p in parts]
        full = _segdot(jnp.concatenate([h for h, _ in pieces], axis=0), jnp.concatenate([l for _, l in pieces], axis=0), e2)
    return [full[s * HEAD:(s + 1) * HEAD] for s in range(len(parts))]


def _diag_mask():
    return lax.broadcasted_iota(jnp.int32, (HEAD, RW), 0) == lax.broadcasted_iota(jnp.int32, (HEAD, RW), 1) % HEAD


def _col_form(rows, dmask, e2):
    his, los = [], []
    for x in rows:
        hi = x.astype(BF16).astype(F32)
        lo = x - hi
        his.append(jnp.where(dmask, jnp.broadcast_to(hi, (HEAD, RW)), 0.0).astype(BF16))
        los.append(jnp.where(dmask, jnp.broadcast_to(lo, (HEAD, RW)), 0.0).astype(BF16))
    full = _segdot(jnp.concatenate(his, axis=0), jnp.concatenate(los, axis=0), e2)
    return [full[s * HEAD:(s + 1) * HEAD] for s in range(len(rows))]


def _row_form(col, dmask):
    return jnp.sum(jnp.where(dmask, col, 0.0), axis=0, keepdims=True)


def _row_sum(x):
    return jnp.sum(x, axis=0, keepdims=True)


def _wkv_fwd(r, v, kk, w, kd, b, *, tb, name):
    bsz, seq, _ = r.shape
    nb = seq // tb
    ns = 2 * bsz
    e2 = _head_ones()

    def body(r0, r1, v0, v1, k0, k1, w0, w1, kd0, kd1, b0, b1, e2_ref, y0_o, y1_o, sp_o, s_ref):
        i = pl.program_id(0)

        @pl.when(i == 0)
        def _():
            s_ref[...] = jnp.zeros_like(s_ref)

        e2v = e2_ref[...]
        dmask = _diag_mask()
        y_refs = (y0_o, y1_o)

        def step(j, carry):
            tl = (j, tb - 1 - j)

            def rows(refs):
                return [refs[d][bb, pl.ds(tl[d], 1), :] for d in (0, 1) for bb in range(bsz)]

            kk_r, w_r, b_r, kd_r, r_r = rows((k0, k1)), rows((w0, w1)), rows((b0, b1)), rows((kd0, kd1)), rows((r0, r1))
            s_old = [s_ref[s * HEAD:(s + 1) * HEAD, :] for s in range(ns)]
            for s in range(ns):
                sp_o[s, pl.ds(j, 1), :, :] = s_old[s].reshape(1, HEAD, RW)
            sa = _seg_streams([s_old[s] * kk_r[s] for s in range(ns)], e2v)
            vc = _col_form(rows((v0, v1)), dmask, e2v)
            s_new = [s_old[s] * w_r[s] - sa[s] * b_r[s] + vc[s] * kd_r[s] for s in range(ns)]
            for s in range(ns):
                s_ref[s * HEAD:(s + 1) * HEAD, :] = s_new[s]
            ycol = _seg_streams([s_new[s] * r_r[s] for s in range(ns)], e2v)
            for d in (0, 1):
                for bb in range(bsz):
                    y_refs[d][bb, pl.ds(tl[d], 1), :] = _row_form(ycol[d * bsz + bb], dmask)
            return carry

        lax.fori_loop(0, tb, step, 0)

    def blk(width_idx, rev):
        if rev:
            return pl.BlockSpec((bsz, tb, RW), lambda i: (0, nb - 1 - i, width_idx))
        return pl.BlockSpec((bsz, tb, RW), lambda i: (0, i, width_idx))

    in_specs = [blk(0, False), blk(0, True)] * 3 + [blk(0, False), blk(1, True)] * 3
    in_specs.append(pl.BlockSpec(e2.shape, lambda i: (0, 0)))
    return pl.pallas_call(
        body, name=name,
        out_shape=[jax.ShapeDtypeStruct((bsz, seq, RW), F32), jax.ShapeDtypeStruct((bsz, seq, RW), F32),
                   jax.ShapeDtypeStruct((ns, seq, HEAD, RW), F32)],
        grid=(nb,),
        in_specs=in_specs,
        out_specs=[blk(0, False), blk(0, True), pl.BlockSpec((ns, tb, HEAD, RW), lambda i: (0, i, 0, 0))],
        scratch_shapes=[pltpu.VMEM((ns * HEAD, RW), F32)],
        compiler_params=_cparams(("arbitrary",)),
    )(r, r, v, v, kk, kk, w, w, kd, kd, b, b, e2)


def _wkv_bwd(r, v, kk, w, kd, b, dy, sp, *, tb, name):
    bsz, seq, _ = r.shape
    nb = seq // tb
    ns = 2 * bsz
    e2 = _head_ones()

    def body(r0, r1, v0, v1, k0, k1, dy0, dy1, w0, w1, kd0, kd1, b0, b1, sp_ref, e2_ref, *rest):
        outs, g_ref = rest[:-1], rest[-1]
        i = pl.program_id(0)

        @pl.when(i == 0)
        def _():
            g_ref[...] = jnp.zeros_like(g_ref)

        e2v = e2_ref[...]
        dmask = _diag_mask()

        def step(jj, carry):
            sl = tb - 1 - jj
            tl = (sl, jj)

            def rows(refs):
                return [refs[d][bb, pl.ds(tl[d], 1), :] for d in (0, 1) for bb in range(bsz)]

            kk_r, w_r, b_r, kd_r, r_r = rows((k0, k1)), rows((w0, w1)), rows((b0, b1)), rows((kd0, kd1)), rows((r0, r1))
            s_old = [sp_ref[s, pl.ds(sl, 1), :, :].reshape(HEAD, RW) for s in range(ns)]
            sa = _seg_streams([s_old[s] * kk_r[s] for s in range(ns)], e2v)
            vc = _col_form(rows((v0, v1)), dmask, e2v)
            dyc = _col_form(rows((dy0, dy1)), dmask, e2v)
            gt = [g_ref[s * HEAD:(s + 1) * HEAD, :] + dyc[s] * r_r[s] for s in range(ns)]
            both = _seg_streams([gt[s] * b_r[s] for s in range(ns)] + [gt[s] * kd_r[s] for s in range(ns)], e2v)
            gb, dvc = both[:ns], both[ns:]
            for d in (0, 1):
                for bb in range(bsz):
                    s = d * bsz + bb
                    at = (bb, pl.ds(tl[d], 1), slice(None))
                    s_new = s_old[s] * w_r[s] - sa[s] * b_r[s] + vc[s] * kd_r[s]
                    outs[0 + d][at] = _row_sum(s_new * dyc[s])
                    outs[2 + d][at] = _row_form(dvc[s], dmask)
                    outs[4 + d][at] = -_row_sum(s_old[s] * gb[s])
                    outs[6 + d][at] = _row_sum(s_old[s] * gt[s])
                    outs[8 + d][at] = _row_sum(gt[s] * vc[s])
                    outs[10 + d][at] = -_row_sum(sa[s] * gt[s])
                    g_ref[s * HEAD:(s + 1) * HEAD, :] = gt[s] * w_r[s] - gb[s] * kk_r[s]
            return carry

        lax.fori_loop(0, tb, step, 0)

    def blk(width_idx, rev):
        if rev:
            return pl.BlockSpec((bsz, tb, RW), lambda i: (0, nb - 1 - i, width_idx))
        return pl.BlockSpec((bsz, tb, RW), lambda i: (0, i, width_idx))

    in_specs = [blk(0, True), blk(0, False)] * 4 + [blk(0, True), blk(1, False)] * 3
    in_specs.append(pl.BlockSpec((ns, tb, HEAD, RW), lambda i: (0, nb - 1 - i, 0, 0)))
    in_specs.append(pl.BlockSpec(e2.shape, lambda i: (0, 0)))
    return pl.pallas_call(
        body, name=name,
        out_shape=[jax.ShapeDtypeStruct((bsz, seq, RW), F32)] * 12,
        grid=(nb,),
        in_specs=in_specs,
        out_specs=[blk(0, True), blk(0, False)] * 6,
        scratch_shapes=[pltpu.VMEM((ns * HEAD, RW), F32)],
        compiler_params=_cparams(("arbitrary",)),
    )(r, r, v, v, kk, kk, dy, dy, w, w, kd, kd, b, b, sp, e2)


def _scan_cotangents(post_g, scan_g, *, tt, name):
    dr_p, dv_p, dkd_p = post_g
    cat = functools.partial(jnp.concatenate, axis=1)

    def fn(i, drp, dvp, dkdp, dr0, dr1, dv0, dv1, dk0, dk1, dw0, dw1, dkd0, dkd1, db0, db1):
        return (drp + dr0 + dr1, dvp + dv0 + dv1, dk0 + dk1, cat([dw0, dw1]), dkdp + cat([dkd0, dkd1]), cat([db0, db1]))
    return _rowcall(fn, [dr_p, dv_p, dkd_p, *scan_g], [],
                    [(RW, F32), (RW, F32), (RW, F32), (2 * RW, F32), (2 * RW, F32), (2 * RW, F32)], [], tt=tt, name=name)


def _block_diag2(w):
    z = jnp.zeros_like(w[0])
    return jnp.concatenate([jnp.concatenate([w[0], z], axis=1), jnp.concatenate([z, w[1]], axis=1)], axis=0)


def _pad_in_cols(a):
    z = jnp.zeros(a.shape[:-1] + (SHIFT_PAD - SHIFT_COLS,), a.dtype)
    return jnp.concatenate([a[..., :SHIFT_COLS], z, a[..., SHIFT_COLS:]], axis=-1)


def _unpad_in_cols(a):
    return jnp.concatenate([a[..., :SHIFT_COLS], a[..., SHIFT_PAD:]], axis=-1)


def _local_step(x, target, wts, *, tt, tb):
    bsz, seq, _ = x.shape
    n_tok = bsz * seq
    row = lambda a: a.reshape(1, -1).astype(F32)
    x0 = x.reshape(n_tok, D_MODEL)
    tgt = target.reshape(n_tok, D_MODEL)
    w1i, w1o, wout, w2i, w2o = wts["ffn1_w_in"], wts["ffn1_w_out"], wts["w_out"], wts["ffn2_w_in"], wts["ffn2_w_out"]
    win = _pad_in_cols(wts["w_in"])
    zpad = jnp.zeros((1, SHIFT_PAD - SHIFT_COLS), F32)
    mu_p = jnp.concatenate([row(wts["mu_prev"]), zpad], axis=1)
    mu_n = jnp.concatenate([row(wts["mu_next"]), zpad], axis=1)
    w2b, a2b = _block_diag2(wts["w2"]), _block_diag2(wts["a2"])
    w0c, a0c = row(wts["w0"]), row(wts["a0"])
    g2p = jnp.concatenate([wts["g2"], jnp.zeros((GATE_PAD - GATE_LORA, RW), F32)], axis=0)
    k_k, k_a, r_k = row(wts["k_k"]), row(wts["k_a"]), row(wts["r_k"])
    lnx_g, lnx_b = row(wts["lnx_g"]), row(wts["lnx_b"])
    cdw, cb, clg, clb = wts["conv_dw"], row(wts["conv_b"]), row(wts["conv_ln_g"]), row(wts["conv_ln_b"])
    ln = {k: row(wts[k]) for k in ("ln1_g", "ln1_b", "ln2_g", "ln2_b", "ln3_g", "ln3_b")}
    small = (mu_p, mu_n, w2b, w0c, a2b, a0c, g2p, k_k, k_a)
    seq3 = lambda a: a.reshape(bsz, seq, a.shape[-1])
    flat = lambda a: a.reshape(n_tok, a.shape[-1])

    h1, act1 = _ffn_in(x0, w1i, tm=TM_FFN, name="ffn1_in")
    z1, x1, x1b = _mm_ln([act1], w1o, x0, ln["ln1_g"], ln["ln1_b"], 0.5, tm=TM_LN, name="ffn1_out_ln1")
    p = _matmul(x1b, win, name="proj_in")
    r, v, kk, w, kd, b, g = _mix_prep(p, *small, seq=seq, tt=tt, name="mix_prep")
    y0, y1, sp = _wkv_fwd(seq3(r), seq3(v), seq3(kk), seq3(w), seq3(kd), seq3(b), tb=tb, name="wkv_fwd")
    y0, y1 = flat(y0), flat(y1)
    yr = _mix_post(y0, y1, r, v, kd, g, lnx_g, lnx_b, r_k, tt=tt, name="mix_post")
    yc, yv = _conv_fwd(p, cdw, cb, clg, clb, seq=seq, tt=tt, name="conv_fwd")
    z2, x2, x2b = _mm_ln([yr, yv], wout, x1, ln["ln2_g"], ln["ln2_b"], 1.0, tm=TM_LN, name="proj_out_ln2")
    h2, act2 = _ffn_in(x2b, w2i, tm=TM_FFN, name="ffn2_in")
    z3, x3, _ = _mm_ln([act2], w2o, x2, ln["ln3_g"], ln["ln3_b"], 0.5, tm=TM_LN, name="ffn2_out_ln3")
    dx3, loss_part = _loss_fwd_bwd(x3, tgt, tt=tt, name="loss")

    gr = {}
    slab_rows = lambda a: a.reshape((N_CHIPS, a.shape[0] // N_CHIPS) + a.shape[1:])
    dw_kw = dict(ta=True, out_dtype=BF16)
    dz3, gr["ln3_g"], gr["ln3_b"] = _ln_bwd(z3, dx3, ln["ln3_g"], ln["ln3_b"], tt=tt, name="ln3_bwd")
    dh2 = _ffn_out_bwd(dz3, w2o, h2, tm=TM_FFN, name="ffn2_out_dx")
    gr["ffn2_w_out"] = slab_rows(_matmul(act2, dz3, scale=0.5, tm=D_FF // 2, name="ffn2_out_dw", **dw_kw))
    dx2 = _mm_nt_res([dh2], w2i, dz3, tm=TM_FFN, name="ffn2_in_dx")
    gr["ffn2_w_in"] = _matmul(x2b, dh2, col_slabs=True, tn=2 * D_FF // N_CHIPS, name="ffn2_in_dw", **dw_kw)
    dz2, gr["ln2_g"], gr["ln2_b"] = _ln_bwd(z2, dx2, ln["ln2_g"], ln["ln2_b"], tt=tt, name="ln2_bwd")
    dmix = _matmul(dz2, wout, tb=True, name="proj_out_dx")
    gr["w_out"] = slab_rows(jnp.concatenate([_matmul(yr, dz2, name="proj_out_dw_rwkv", **dw_kw),
                                             _matmul(yv, dz2, name="proj_out_dw_conv", **dw_kw)], axis=0))
    dyr, dyv = (dmix, RW, 0), (dmix, RW, 1)
    dy, dr_p, dv_p, dkd_p, dg, gr["lnx_g"], gr["lnx_b"], gr["r_k"] = _mix_post_bwd(
        y0, y1, r, v, kd, g, lnx_g, lnx_b, r_k, dyr, tt=tt, name="mix_post_bwd")
    scan_g = _wkv_bwd(seq3(r), seq3(v), seq3(kk), seq3(w), seq3(kd), seq3(b), seq3(dy), sp, tb=tb, name="wkv_bwd")
    cts = _scan_cotangents((dr_p, dv_p, dkd_p), [flat(a) for a in scan_g], tt=tt, name="scan_cts")
    dyc, gr["conv_ln_g"], gr["conv_ln_b"], gr["conv_b"] = _conv_post_bwd(yc, dyv, clg, clb, tt=tt, name="conv_post_bwd")
    dpc, ddw = _conv_bwd(dyc, p, cdw, seq=seq, tt=tt, name="conv_bwd")
    gr["conv_dw"] = ddw[:CONV_K]
    dps, dw2b, dw0c, da2b, da0c, dg2p, gr["k_k"], gr["k_a"] = _mix_prep_bwd(
        p, *small, [*cts, dg], seq=seq, tt=tt, name="mix_prep_bwd")
    gr["w2"] = jnp.stack([dw2b[:LORA, :RW], dw2b[LORA:, RW:]])
    gr["a2"] = jnp.stack([da2b[:LORA, :RW], da2b[LORA:, RW:]])
    gr["w0"], gr["a0"], gr["g2"] = dw0c.reshape(2, RW), da0c.reshape(2, RW), dg2p[:GATE_LORA]
    dpsh, dmu_p, dmu_n = _shift_bwd(dps, p, mu_p, mu_n, seq=seq, tt=tt, name="shift_bwd")
    gr["mu_prev"], gr["mu_next"] = dmu_p[:, :SHIFT_COLS], dmu_n[:, :SHIFT_COLS]
    dx1 = _mm_nt_res([dpsh, dpc], win, dz2, tm=TM_FFN, name="proj_in_dx")
    dwin = jnp.concatenate([_matmul(x1b, dpsh, name="proj_in_dw_shift", **dw_kw)[:, :SHIFT_COLS],
                            _matmul(x1b, dpc, name="proj_in_dw_conv", **dw_kw)], axis=1)
    gr["w_in"] = jnp.moveaxis(dwin.reshape(D_MODEL, N_CHIPS, IN_COLS // N_CHIPS), 1, 0)
    dz1, gr["ln1_g"], gr["ln1_b"] = _ln_bwd(z1, dx1, ln["ln1_g"], ln["ln1_b"], tt=tt, name="ln1_bwd")
    dh1 = _ffn_out_bwd(dz1, w1o, h1, tm=TM_FFN, name="ffn1_out_dx")
    gr["ffn1_w_out"] = slab_rows(_matmul(act1, dz1, scale=0.5, tm=D_FF // 2, name="ffn1_out_dw", **dw_kw))
    dx0 = _mm_nt_res([dh1], w1i, dz1, tm=TM_FFN, name="ffn1_in_dx")
    gr["ffn1_w_in"] = _matmul(x0, dh1, col_slabs=True, tn=2 * D_FF // N_CHIPS, name="ffn1_in_dw", **dw_kw)
    return loss_part, dx0.reshape(bsz, seq, D_MODEL), gr


def _mesh_pos():
    return lax.axis_index("x"), lax.axis_index("y"), lax.axis_index("c")


def _other_chips(x, y):
    return [(1 - x, y), (x, 1 - y), (1 - x, 1 - y)]


def _gather_chips(shards, *, name):
    n = len(shards)

    def body(*refs):
        ins, outs = refs[:n], refs[n:2 * n]
        send_sems, recv_sems, loc_sems = refs[2 * n:]
        x, y, c = _mesh_pos()
        q = 2 * x + y
        peers = _other_chips(x, y)
        local = [pltpu.make_async_copy(ins[a], outs[a].at[q], loc_sems.at[a]) for a in range(n)]
        for cp in local:
            cp.start()
        sends = [[pltpu.make_async_remote_copy(ins[a], outs[a].at[q], send_sems.at[a, k], recv_sems.at[a, k],
                                               device_id=(px, py, c), device_id_type=MESH)
                  for k, (px, py) in enumerate(peers)] for a in range(n)]
        for a in range(n):
            for cp in sends[a]:
                cp.start()
        for a in range(n):
            for k, (px, py) in enumerate(peers):
                pltpu.make_async_remote_copy(ins[a], outs[a].at[2 * px + py], send_sems.at[a, k], recv_sems.at[a, k],
                                             device_id=(px, py, c), device_id_type=MESH).wait_recv()
        for a in range(n):
            for cp in sends[a]:
                cp.wait_send()
            local[a].wait()

    any_spec = pl.BlockSpec(memory_space=pl.ANY)
    return pl.pallas_call(
        body, name=name,
        out_shape=[jax.ShapeDtypeStruct((N_CHIPS,) + s.shape, s.dtype) for s in shards],
        in_specs=[any_spec] * n, out_specs=[any_spec] * n,
        scratch_shapes=[pltpu.SemaphoreType.DMA((n, 3)), pltpu.SemaphoreType.DMA((n, 3)), pltpu.SemaphoreType.DMA((n,))],
        compiler_params=pltpu.CompilerParams(has_side_effects=True),
    )(*shards)


def _scatter_chips(stacks, *, name):
    n = len(stacks)

    def body(*refs):
        ins, outs = refs[:n], refs[n:2 * n]
        send_sems, recv_sems = refs[2 * n:]
        x, y, c = _mesh_pos()
        peers = _other_chips(x, y)
        sends = [[pltpu.make_async_remote_copy(ins[a].at[2 * px + py], outs[a].at[k], send_sems.at[a, k],
                                               recv_sems.at[a, k], device_id=(px, py, c), device_id_type=MESH)
                  for k, (px, py) in enumerate(peers)] for a in range(n)]
        for a in range(n):
            for cp in sends[a]:
                cp.start()
        for a in range(n):
            for cp in sends[a]:
                cp.wait_recv()
        for a in range(n):
            for cp in sends[a]:
                cp.wait_send()

    any_spec = pl.BlockSpec(memory_space=pl.ANY)
    return pl.pallas_call(
        body, name=name,
        out_shape=[jax.ShapeDtypeStruct((3,) + s.shape[1:], s.dtype) for s in stacks],
        in_specs=[any_spec] * n, out_specs=[any_spec] * n,
        scratch_shapes=[pltpu.SemaphoreType.DMA((n, 3)), pltpu.SemaphoreType.DMA((n, 3))],
        compiler_params=pltpu.CompilerParams(has_side_effects=True),
    )(*stacks)


def _swap_sibling(arrs, *, name):
    n = len(arrs)

    def body(*refs):
        ins, outs = refs[:n], refs[n:2 * n]
        send_sems, recv_sems = refs[2 * n:]
        x, y, c = _mesh_pos()
        cps = [pltpu.make_async_remote_copy(ins[a], outs[a], send_sems.at[a], recv_sems.at[a],
                                            device_id=(x, y, 1 - c), device_id_type=MESH) for a in range(n)]
        for cp in cps:
            cp.start()
        for cp in cps:
            cp.wait_recv()
        for cp in cps:
            cp.wait_send()

    any_spec = pl.BlockSpec(memory_space=pl.ANY)
    return pl.pallas_call(
        body, name=name,
        out_shape=[jax.ShapeDtypeStruct(s.shape, s.dtype) for s in arrs],
        in_specs=[any_spec] * n, out_specs=[any_spec] * n,
        scratch_shapes=[pltpu.SemaphoreType.DMA((n,)), pltpu.SemaphoreType.DMA((n,))],
        compiler_params=pltpu.CompilerParams(has_side_effects=True),
    )(*arrs)


def _all_reduce_rows(vec, *, name):
    rows = vec.shape[0]

    def body(v_ref, o_ref, land, send_sems, recv_sems):
        x, y, c = _mesh_pos()
        me = 4 * x + 2 * y + c
        land[me] = v_ref[...]
        cps = []
        for m in range(1, 8):
            mx, my, mc = (m >> 2) & 1, (m >> 1) & 1, m & 1
            tx, ty, tc = (x + mx) % 2, (y + my) % 2, (c + mc) % 2
            cps.append(pltpu.make_async_remote_copy(v_ref, land.at[me], send_sems.at[m - 1], recv_sems.at[me],
                                                    device_id=(tx, ty, tc), device_id_type=MESH))
        for cp in cps:
            cp.start()
        for m in range(1, 8):
            mx, my, mc = (m >> 2) & 1, (m >> 1) & 1, m & 1
            src = 4 * ((x + mx) % 2) + 2 * ((y + my) % 2) + (c + mc) % 2
            pltpu.make_async_remote_copy(v_ref, land.at[src], send_sems.at[m - 1], recv_sems.at[src],
                                         device_id=(x, y, c), device_id_type=MESH).wait_recv()
        for cp in cps:
            cp.wait_send()
        acc = land[0]
        for d in range(1, 8):
            acc = acc + land[d]
        o_ref[...] = acc

    vm = pl.BlockSpec(memory_space=pltpu.VMEM)
    return pl.pallas_call(
        body, name=name,
        out_shape=jax.ShapeDtypeStruct(vec.shape, F32),
        in_specs=[vm], out_specs=vm,
        scratch_shapes=[pltpu.VMEM((8, rows, LANES), F32), pltpu.SemaphoreType.DMA((7,)), pltpu.SemaphoreType.DMA((8,))],
        compiler_params=pltpu.CompilerParams(has_side_effects=True, vmem_limit_bytes=VMEM_LIMIT),
    )(vec)


def _adamw(w, g, m, v):
    m = ADAM_B1 * m + (1.0 - ADAM_B1) * g
    v = ADAM_B2 * v + (1.0 - ADAM_B2) * (g * g)
    m_hat = m / (1.0 - ADAM_B1 ** ADAM_STEP)
    v_hat = v / (1.0 - ADAM_B2 ** ADAM_STEP)
    delta = -ADAM_LR * (m_hat / (jnp.sqrt(v_hat) + ADAM_EPS) + ADAM_WD * w)
    return delta, m, v


def _sum4(mine, land, *, name):
    rows, cols = mine.shape
    tr = _pick_rows(rows)

    def body(a_ref, l_ref, o_ref):
        o_ref[...] = (a_ref[...].astype(F32) + l_ref[0].astype(F32)) + (l_ref[1].astype(F32) + l_ref[2].astype(F32))

    return pl.pallas_call(
        body, name=name, out_shape=jax.ShapeDtypeStruct((rows, cols), F32), grid=(rows // tr,),
        in_specs=[pl.BlockSpec((tr, cols), lambda i: (i, 0)), pl.BlockSpec((3, tr, cols), lambda i: (0, i, 0))],
        out_specs=pl.BlockSpec((tr, cols), lambda i: (i, 0)),
        compiler_params=_cparams(("parallel",)),
    )(mine, land)


def _pick_rows(rows, want=256):
    for t in range(min(want, rows) // 8 * 8, 0, -8):
        if rows % t == 0:
            return t
    return rows


def _sum_adam(h_mine, h_sib, w, m, v, *, name):
    rows, cols = w.shape
    tr = _pick_rows(rows)

    def body(a_ref, b_ref, w_ref, m_ref, v_ref, g_o, d_o, m_o, v_o):
        g = a_ref[...] + b_ref[...]
        d, mn, vn = _adamw(w_ref[...], g, m_ref[...], v_ref[...])
        g_o[...], d_o[...], m_o[...], v_o[...] = g, d, mn, vn

    spec = pl.BlockSpec((tr, cols), lambda i: (i, 0))
    return pl.pallas_call(
        body, name=name, out_shape=[jax.ShapeDtypeStruct((rows, cols), F32)] * 4, grid=(rows // tr,),
        in_specs=[spec] * 5, out_specs=[spec] * 4, compiler_params=_cparams(("parallel",)),
    )(h_mine, h_sib, w, m, v)


def _adam_rows(w, g, m, v, *, name):
    def body(w_ref, g_ref, m_ref, v_ref, d_o, m_o, v_o):
        d_o[...], m_o[...], v_o[...] = _adamw(w_ref[...], g_ref[...], m_ref[...], v_ref[...])

    vm = pl.BlockSpec(memory_space=pltpu.VMEM)
    return pl.pallas_call(
        body, name=name, out_shape=[jax.ShapeDtypeStruct(w.shape, F32)] * 3,
        in_specs=[vm] * 4, out_specs=[vm] * 3, compiler_params=_cparams(),
    )(w, g, m, v)


def _pack_rows(arrs):
    flat = jnp.concatenate([a.reshape(-1).astype(F32) for a in arrs])
    pad = -flat.shape[0] % (8 * LANES)
    return jnp.concatenate([flat, jnp.zeros((pad,), F32)]).reshape(-1, LANES)


def _unpack_rows(packed, shapes):
    flat = packed.reshape(-1)
    out, off = [], 0
    for s in shapes:
        size = 1
        for d in s:
            size *= d
        out.append(flat[off:off + size].reshape(s))
        off += size
    return out


WEIGHTS = ['ffn1_w_in', 'ffn1_w_out', 'w_in', 'mu_prev', 'mu_next', 'w0', 'w2', 'a0', 'a2', 'g2', 'k_k', 'k_a', 'r_k',
           'lnx_g', 'lnx_b', 'conv_dw', 'conv_b', 'conv_ln_g', 'conv_ln_b', 'w_out', 'ffn2_w_in', 'ffn2_w_out',
           'ln1_g', 'ln1_b', 'ln2_g', 'ln2_b', 'ln3_g', 'ln3_b']
COL_SHARDED = ('ffn1_w_in', 'w_in', 'ffn2_w_in')
ROW_SHARDED = ('ffn1_w_out', 'w_out', 'ffn2_w_out')
BIG = COL_SHARDED + ROW_SHARDED
SMALL_SHARDED = ('w0', 'w2', 'a0', 'a2', 'g2', 'conv_dw')
REPLICATED = tuple(n for n in WEIGHTS if n not in BIG + SMALL_SHARDED)


def _train_step(x, target, w, m, v, *, tt, tb):
    xi, yi, _ = _mesh_pos()
    q = 2 * xi + yi

    shards = [w[n][0].astype(BF16) for n in BIG] + [w[n][0] for n in SMALL_SHARDED]
    gathered = _gather_chips(shards, name="gather_weights")
    full = {}
    for n, gth in zip(BIG + SMALL_SHARDED, gathered):
        if n in ROW_SHARDED:
            full[n] = gth.reshape((-1,) + gth.shape[2:])
        elif n in ("ffn1_w_in", "ffn2_w_in"):
            full[n] = gth
        else:
            full[n] = jnp.moveaxis(gth, 0, -2).reshape(gth.shape[1:-1] + (N_CHIPS * gth.shape[-1],))
    for n in REPLICATED:
        full[n] = w[n][0]

    loss_part, grad_x, gr = _local_step(x, target, full, tt=tt, tb=tb)

    stacks = [gr[n] for n in BIG]
    landed = _scatter_chips(stacks, name="scatter_grads")
    halves = [_sum4(lax.dynamic_index_in_dim(s, q, 0, keepdims=False), l, name="sum4_" + n)
              for n, s, l in zip(BIG, stacks, landed)]
    sib = _swap_sibling(halves, name="swap_halves")
    grad, delta, new_m, new_v = {}, {}, {}, {}
    for n, h, hs in zip(BIG, halves, sib):
        outs = _sum_adam(h, hs, w[n][0], m[n][0], v[n][0], name="adam_" + n)
        grad[n], delta[n], new_m[n], new_v[n] = [o[None] for o in outs]

    small_names = REPLICATED + SMALL_SHARDED
    small_full_shapes = [full[n].shape for n in small_names]
    red = _all_reduce_rows(_pack_rows([gr[n] for n in small_names]), name="reduce_small")
    red = dict(zip(small_names, _unpack_rows(red, small_full_shapes)))
    gsm = {}
    for n in REPLICATED:
        gsm[n] = red[n].reshape(w[n].shape)
    for n in SMALL_SHARDED:
        width = w[n].shape[-1]
        gsm[n] = lax.dynamic_slice_in_dim(red[n], q * width, width, axis=red[n].ndim - 1).reshape(w[n].shape)
    shapes = [w[n].shape for n in small_names]
    d_p, m_p, v_p = _adam_rows(_pack_rows([w[n] for n in small_names]), _pack_rows([gsm[n] for n in small_names]),
                               _pack_rows([m[n] for n in small_names]), _pack_rows([v[n] for n in small_names]),
                               name="adam_small")
    for n, dd, mm, vv in zip(small_names, _unpack_rows(d_p, shapes), _unpack_rows(m_p, shapes), _unpack_rows(v_p, shapes)):
        grad[n], delta[n], new_m[n], new_v[n] = gsm[n], dd, mm, vv
    return loss_part, grad_x, grad, delta, new_m, new_v


def kernel(x, ffn1_w_in, ffn1_w_out, w_in, mu_prev, mu_next, w0, w2, a0, a2, g2, k_k, k_a, r_k, lnx_g, lnx_b, conv_dw, conv_b, conv_ln_g, conv_ln_b, w_out, ffn2_w_in, ffn2_w_out, ln1_g, ln1_b, ln2_g, ln2_b, ln3_g, ln3_b, loss_target, m_ffn1_w_in, m_ffn1_w_out, m_w_in, m_mu_prev, m_mu_next, m_w0, m_w2, m_a0, m_a2, m_g2, m_k_k, m_k_a, m_r_k, m_lnx_g, m_lnx_b, m_conv_dw, m_conv_b, m_conv_ln_g, m_conv_ln_b, m_w_out, m_ffn2_w_in, m_ffn2_w_out, m_ln1_g, m_ln1_b, m_ln2_g, m_ln2_b, m_ln3_g, m_ln3_b, v_ffn1_w_in, v_ffn1_w_out, v_w_in, v_mu_prev, v_mu_next, v_w0, v_w2, v_a0, v_a2, v_g2, v_k_k, v_k_a, v_r_k, v_lnx_g, v_lnx_b, v_conv_dw, v_conv_b, v_conv_ln_g, v_conv_ln_b, v_w_out, v_ffn2_w_in, v_ffn2_w_out, v_ln1_g, v_ln1_b, v_ln2_g, v_ln2_b, v_ln3_g, v_ln3_b):
    args = dict(locals())
    w = {n: args[n] for n in WEIGHTS}
    m = {n: args["m_" + n] for n in WEIGHTS}
    v = {n: args["v_" + n] for n in WEIGHTS}
    seq = x.shape[1]
    loss_part, grad_x, grad, delta, new_m, new_v = _train_step(x, loss_target, w, m, v, tt=min(256, seq), tb=8)
    loss = lax.psum(loss_part[0, 0], ("x", "y", "c"))
    return (loss, grad_x, *[grad[n] for n in WEIGHTS], *[delta[n] for n in WEIGHTS],
            *[new_m[n] for n in WEIGHTS], *[new_v[n] for n in WEIGHTS])
```

```python
import functools

import jax
import jax.numpy as jnp
from jax import lax
from jax.experimental import pallas as pl
from jax.experimental.pallas import tpu as pltpu

F32 = jnp.float32
BF16 = jnp.bfloat16

D_MODEL = 1024
RW = 512
HEAD = 64
CW = 512
CONV_K = 31
CONV_PAD = 15
D_FF = 2816
LORA = 64
GATE_LORA = 160
GATE_PAD = 256
SHIFT_COLS = 1952
SHIFT_PAD = 2048
IN_COLS = 2976
IN_PAD = 3072
LN_EPS = 1e-5
GN_EPS = 64e-5
NORM_EPS = 1e-12
ALPHA = 2.0 ** 0.25
DECAY_SCALE = 0.6065306597126334
ADAM_LR, ADAM_B1, ADAM_B2, ADAM_EPS, ADAM_WD, ADAM_STEP = 0.001, 0.9, 0.999, 1e-08, 0.01, 10
N_CHIPS = 4
VMEM_LIMIT = 56 * 1024 * 1024
TM_FFN = 256
TM_LN = 512

MESH = pl.DeviceIdType.MESH


def _cparams(sem=None, **kw):
    return pltpu.CompilerParams(dimension_semantics=sem, vmem_limit_bytes=VMEM_LIMIT, **kw)


LANES = 128


def _pick_tile(dim, want):
    for t in range(min(want, dim) // LANES * LANES, 0, -LANES):
        if dim % t == 0:
            return t
    return dim


def _after_operand(after):
    return ([], []) if after is None else ([pl.BlockSpec(memory_space=pl.ANY)], [after])


def _matmul(a, b, *, ta=False, tb=False, out_dtype=F32, tm=1024, tn=1024, tk=1024, scale=1.0, col_slabs=False,
            after=None, name):
    after_specs, after_args = _after_operand(after)
    if ta:
        k_dim, m_dim = a.shape
    else:
        m_dim, k_dim = a.shape
    n_dim = b.shape[0] if tb else b.shape[1]
    tm, tn, tk = _pick_tile(m_dim, tm), _pick_tile(n_dim, tn), _pick_tile(k_dim, tk)
    assert m_dim % tm == 0 and n_dim % tn == 0 and k_dim % tk == 0, (name, a.shape, b.shape, tm, tn, tk)
    nk = k_dim // tk
    dims = (((0,) if ta else (1,), (1,) if tb else (0,)), ((), ()))
    if col_slabs:
        out_shape = jax.ShapeDtypeStruct((n_dim // tn, m_dim, tn), out_dtype)
        out_spec = pl.BlockSpec((None, tm, tn), lambda i, j, k: (j, i, 0))
    else:
        out_shape = jax.ShapeDtypeStruct((m_dim, n_dim), out_dtype)
        out_spec = pl.BlockSpec((tm, tn), lambda i, j, k: (i, j))

    def body(a_ref, b_ref, *rest):
        o_ref, acc_ref = rest[-2:]
        kk = pl.program_id(2)

        @pl.when(kk == 0)
        def _():
            acc_ref[...] = jnp.zeros_like(acc_ref)

        acc_ref[...] += lax.dot_general(a_ref[...].astype(BF16), b_ref[...].astype(BF16), dims,
                                        preferred_element_type=F32)

        @pl.when(kk == nk - 1)
        def _():
            o_ref[...] = (acc_ref[...] * scale).astype(o_ref.dtype)

    a_spec = pl.BlockSpec((tk, tm), lambda i, j, k: (k, i)) if ta else pl.BlockSpec((tm, tk), lambda i, j, k: (i, k))
    b_spec = pl.BlockSpec((tn, tk), lambda i, j, k: (j, k)) if tb else pl.BlockSpec((tk, tn), lambda i, j, k: (k, j))
    return pl.pallas_call(
        body, name=name,
        out_shape=out_shape,
        grid=(m_dim // tm, n_dim // tn, nk),
        in_specs=[a_spec, b_spec] + after_specs,
        out_specs=out_spec,
        scratch_shapes=[pltpu.VMEM((tm, tn), F32)],
        compiler_params=_cparams(("parallel", "parallel", "arbitrary")),
    )(a, b, *after_args)


def _whole(shape):
    nd = len(shape)
    return pl.BlockSpec(shape, lambda i: (0,) * nd)


def _ffn_in(x, w, *, tm, after=None, name):
    n_tok = x.shape[0]
    sw = w.shape[2]
    tm = min(tm, n_tok)

    after_specs, after_args = _after_operand(after)

    def body(x_ref, w_ref, *rest):
        h_ref, a_ref = rest[-2:]
        xb = x_ref[...].astype(BF16)
        for s in range(2):
            g = jnp.dot(xb, w_ref[s], preferred_element_type=F32)
            u = jnp.dot(xb, w_ref[s + 2], preferred_element_type=F32)
            h_ref[:, s * sw:(s + 1) * sw] = g.astype(BF16)
            h_ref[:, (s + 2) * sw:(s + 3) * sw] = u.astype(BF16)
            a_ref[:, s * sw:(s + 1) * sw] = (_silu(g) * u).astype(BF16)

    return pl.pallas_call(
        body, name=name,
        out_shape=[jax.ShapeDtypeStruct((n_tok, 2 * D_FF), BF16), jax.ShapeDtypeStruct((n_tok, D_FF), BF16)],
        grid=(n_tok // tm,),
        in_specs=[pl.BlockSpec((tm, D_MODEL), lambda i: (i, 0)), _whole(w.shape)] + after_specs,
        out_specs=[pl.BlockSpec((tm, 2 * D_FF), lambda i: (i, 0)), pl.BlockSpec((tm, D_FF), lambda i: (i, 0))],
        compiler_params=_cparams(("parallel",)),
    )(x, w, *after_args)


def _mm_ln(a_list, w, xres, g, b, fscale, *, tm, name):
    n_tok = xres.shape[0]
    tm = min(tm, n_tok)
    na = len(a_list)

    def body(*refs):
        a_refs = refs[:na]
        w_ref, x_ref, g_ref, b_ref, z_o, y_o, yb_o = refs[na:]
        f, off = None, 0
        for a_ref in a_refs:
            k = a_ref.shape[1]
            t = jnp.dot(a_ref[...].astype(BF16), w_ref[off:off + k, :], preferred_element_type=F32)
            f = t if f is None else f + t
            off += k
        z = ALPHA * x_ref[...] + fscale * f
        y = _layer_norm(z, g_ref[...], b_ref[...])
        z_o[...] = z
        y_o[...] = y
        yb_o[...] = y.astype(BF16)

    tile = pl.BlockSpec((tm, D_MODEL), lambda i: (i, 0))
    return pl.pallas_call(
        body, name=name,
        out_shape=[jax.ShapeDtypeStruct((n_tok, D_MODEL), F32)] * 2 + [jax.ShapeDtypeStruct((n_tok, D_MODEL), BF16)],
        grid=(n_tok // tm,),
        in_specs=[pl.BlockSpec((tm, a.shape[1]), lambda i: (i, 0)) for a in a_list]
        + [_whole(w.shape), tile, _whole(g.shape), _whole(b.shape)],
        out_specs=[tile, tile, tile],
        compiler_params=_cparams(("parallel",)),
    )(*a_list, w, xres, g, b)


def _ffn_out_bwd(dz, w, h, *, tm, name):
    n_tok = dz.shape[0]
    tm = min(tm, n_tok)
    cw = D_FF // 2

    def body(dz_ref, w_ref, h_ref, dh_ref):
        dzb = dz_ref[...].astype(BF16)
        for s in range(2):
            dact = 0.5 * lax.dot_general(dzb, w_ref[s * cw:(s + 1) * cw, :], (((1,), (1,)), ((), ())),
                                         preferred_element_type=F32)
            gate = h_ref[:, s * cw:(s + 1) * cw].astype(F32)
            up = h_ref[:, D_FF + s * cw:D_FF + (s + 1) * cw].astype(F32)
            sg = _sigmoid(gate)
            dh_ref[:, s * cw:(s + 1) * cw] = (dact * up * sg * (1.0 + gate * (1.0 - sg))).astype(BF16)
            dh_ref[:, D_FF + s * cw:D_FF + (s + 1) * cw] = (dact * gate * sg).astype(BF16)

    wide = pl.BlockSpec((tm, 2 * D_FF), lambda i: (i, 0))
    return pl.pallas_call(
        body, name=name,
        out_shape=jax.ShapeDtypeStruct((n_tok, 2 * D_FF), BF16),
        grid=(n_tok // tm,),
        in_specs=[pl.BlockSpec((tm, D_MODEL), lambda i: (i, 0)), _whole(w.shape), wide],
        out_specs=wide,
        compiler_params=_cparams(("parallel",)),
    )(dz, w, h)


def _mm_nt_res(a_list, w, dz, *, tm, after=None, name):
    n_tok = dz.shape[0]
    tm = min(tm, n_tok)
    na = len(a_list)
    nt = (((1,), (1,)), ((), ()))
    after_specs, after_args = _after_operand(after)

    def body(*refs):
        a_refs = refs[:na]
        w_ref, dz_ref, o_ref = refs[na], refs[na + 1], refs[-1]
        acc = ALPHA * dz_ref[...]
        if len(w_ref.shape) == 3:
            cw = w_ref.shape[2]
            for s in range(w_ref.shape[0]):
                acc = acc + lax.dot_general(a_refs[0][:, s * cw:(s + 1) * cw], w_ref[s], nt, preferred_element_type=F32)
        else:
            off = 0
            for a_ref in a_refs:
                k = a_ref.shape[1]
                acc = acc + lax.dot_general(a_ref[...], w_ref[:, off:off + k], nt, preferred_element_type=F32)
                off += k
        o_ref[...] = acc

    tile = pl.BlockSpec((tm, D_MODEL), lambda i: (i, 0))
    return pl.pallas_call(
        body, name=name,
        out_shape=jax.ShapeDtypeStruct((n_tok, D_MODEL), F32),
        grid=(n_tok // tm,),
        in_specs=[pl.BlockSpec((tm, a.shape[1]), lambda i: (i, 0)) for a in a_list] + [_whole(w.shape), tile]
        + after_specs,
        out_specs=tile,
        compiler_params=_cparams(("parallel",)),
    )(*a_list, w, dz, *after_args)


def _rowcall(fn, tok_in, full_in, tok_out, acc_out, *, tt, name):
    views = [a if isinstance(a, tuple) else (a, a.shape[1], 0) for a in tok_in]
    tok_in = [a for a, _, _ in views]
    n_tok = tok_in[0].shape[0]
    assert n_tok % tt == 0, (name, n_tok, tt)
    n_ti, n_fi, n_to = len(tok_in), len(full_in), len(tok_out)

    def body(*refs):
        i = pl.program_id(0)
        ins = [r[...] for r in refs[:n_ti + n_fi]]
        outs = fn(i, *ins)
        o_refs = refs[n_ti + n_fi:]
        for r, val in zip(o_refs[:n_to], outs[:n_to]):
            r[...] = val.astype(r.dtype)
        if acc_out:
            @pl.when(i == 0)
            def _():
                for r in o_refs[n_to:]:
                    r[...] = jnp.zeros_like(r)
            for r, val in zip(o_refs[n_to:], outs[n_to:]):
                r[...] += val.reshape(r.shape).astype(F32)

    in_specs = [pl.BlockSpec((tt, width), functools.partial(lambda k, i: (i, k), k)) for _, width, k in views]
    in_specs += [pl.BlockSpec(a.shape, lambda i: (0, 0)) for a in full_in]
    out_specs = [pl.BlockSpec((tt, c), lambda i: (i, 0)) for c, _ in tok_out]
    out_specs += [pl.BlockSpec(s, lambda i: (0, 0)) for s in acc_out]
    out_shape = [jax.ShapeDtypeStruct((n_tok, c), dt) for c, dt in tok_out]
    out_shape += [jax.ShapeDtypeStruct(s, F32) for s in acc_out]
    return pl.pallas_call(
        body, name=name, out_shape=out_shape, grid=(n_tok // tt,), in_specs=in_specs, out_specs=out_specs,
        compiler_params=_cparams(("arbitrary",) if acc_out else ("parallel",)),
    )(*tok_in, *full_in)


@jax.custom_vjp
def _bdot(a, b):
    return jnp.dot(a.astype(BF16), b.astype(BF16), preferred_element_type=F32)


def _bdot_fwd(a, b):
    return _bdot(a, b), (a, b)


def _bdot_bwd(res, g):
    a, b = res
    g16 = g.astype(BF16)
    da = lax.dot_general(g16, b.astype(BF16), (((1,), (1,)), ((), ())), preferred_element_type=F32)
    db = lax.dot_general(a.astype(BF16), g16, (((0,), (0,)), ((), ())), preferred_element_type=F32)
    return da, db


_bdot.defvjp(_bdot_fwd, _bdot_bwd)


def _split16(x):
    hi = x.astype(BF16)
    lo = (x - hi.astype(F32)).astype(BF16)
    return hi, lo


def _segsum_raw(x, e2):
    hi, lo = _split16(x)
    outs = []
    for c in range(x.shape[1] // 256):
        lhs = jnp.concatenate([hi[:, 256 * c:256 * (c + 1)], lo[:, 256 * c:256 * (c + 1)]], axis=1)
        outs.append(jnp.dot(lhs, e2, preferred_element_type=F32))
    return jnp.concatenate(outs, axis=1)


@jax.custom_vjp
def _segsum(x, e2):
    return _segsum_raw(x, e2)


def _segsum_fwd(x, e2):
    return _segsum_raw(x, e2), e2


def _segsum_bwd(e2, g):
    return _segsum_raw(g, e2), jnp.zeros_like(e2)


_segsum.defvjp(_segsum_fwd, _segsum_bwd)


def _head_ones():
    r = lax.broadcasted_iota(jnp.int32, (512, 256), 0) % 256
    c = lax.broadcasted_iota(jnp.int32, (512, 256), 1)
    return (r // HEAD == c // HEAD).astype(BF16)


def _sigmoid(x):
    return 1.0 / (1.0 + jnp.exp(-x))


def _silu(x):
    return x * _sigmoid(x)


def _layer_norm(z, g, b, eps=LN_EPS):
    mu = jnp.mean(z, axis=-1, keepdims=True)
    zc = z - mu
    var = jnp.mean(zc * zc, axis=-1, keepdims=True)
    return zc * lax.rsqrt(var + eps) * g + b


def _prep(ps, w2b, w0c, a2b, a0c, g2p, k_k, k_a, e2):
    r, k, v = ps[:, 0:512], ps[:, 512:1024], ps[:, 1024:1536]
    wd, ad, gd = ps[:, 1536:1664], ps[:, 1664:1792], ps[:, 1792:2048]
    lw = _bdot(jnp.tanh(wd), w2b) + w0c
    decay = jnp.exp(-DECAY_SCALE * _sigmoid(lw))
    a = _sigmoid(_bdot(ad, a2b) + a0c)
    g = _bdot(_sigmoid(gd), g2p)
    kkr = k * k_k
    nrm = jnp.sqrt(_segsum(kkr * kkr, e2))
    kk = kkr / jnp.maximum(nrm, NORM_EPS)
    k2 = jnp.concatenate([k, k], axis=1)
    ka2 = jnp.concatenate([k_a, k_a], axis=1)
    kd = k2 * (1.0 + (a - 1.0) * ka2)
    b = jnp.concatenate([kk, kk], axis=1) * a
    return r, v, kk, decay, kd, b, g


def _post(y0, y1, r, v, kd, g, lnx_g, lnx_b, r_k, e2):
    y = y0 + y1
    mu = _segsum(y, e2) * (1.0 / HEAD)
    yc = y - mu
    var = _segsum(yc * yc, e2) * (1.0 / HEAD)
    yn = yc * lax.rsqrt(var + GN_EPS) * lnx_g + lnx_b
    bonus = _segsum(r * (kd[:, :RW] + kd[:, RW:]) * r_k, e2)
    return (yn + bonus * v) * g


def _conv_post(yc, ln_g, ln_b):
    return _silu(_layer_norm(yc, ln_g, ln_b))


def _ln_bwd(z, dy, g, b, *, tt, name):
    def fn(i, zv, dv, gv, bv):
        _, vjp = jax.vjp(_layer_norm, zv, gv, bv)
        return vjp(dv)
    return _rowcall(fn, [z, dy], [g, b], [(D_MODEL, F32)], [(1, D_MODEL), (1, D_MODEL)], tt=tt, name=name)


def _loss_fwd_bwd(y, target, *, tt, name):
    def fn(i, yv, tv):
        e = yv - tv
        part = 0.5 * jnp.sum(jnp.mean(e * e, axis=-1, keepdims=True), axis=0, keepdims=True)
        return e * (1.0 / D_MODEL), jnp.broadcast_to(part, (8, 128))
    return _rowcall(fn, [y, target], [], [(D_MODEL, F32)], [(8, 128)], tt=tt, name=name)


def _halo_specs(cols_block, hb, tt, n_tok, col_idx):
    nb = n_tok // hb
    prev = pl.BlockSpec((hb, cols_block), lambda i: (jnp.maximum(i * (tt // hb) - 1, 0), col_idx))
    nxt = pl.BlockSpec((hb, cols_block), lambda i: (jnp.minimum((i + 1) * (tt // hb), nb - 1), col_idx))
    return prev, nxt


def _mix_prep(p, mu_p, mu_n, w2b, w0c, a2b, a0c, g2p, k_k, k_a, *, seq, tt, name):
    n_tok = p.shape[0]
    tps = seq // tt
    e2 = _head_ones()

    def body(p_ref, hp_ref, hn_ref, mup_ref, mun_ref, w2b_ref, w0c_ref, a2b_ref, a0c_ref, g2p_ref, kk_ref, ka_ref,
             e2_ref, r_o, v_o, kk_o, w_o, kd_o, b_o, g_o, ext):
        i = pl.program_id(0)
        first = (i % tps) == 0
        last = (i % tps) == tps - 1
        pv = p_ref[...]
        ext[pl.ds(0, 8), :] = jnp.where(first, 0.0, hp_ref[...])
        ext[pl.ds(8, tt), :] = pv
        ext[pl.ds(8 + tt, 8), :] = jnp.where(last, 0.0, hn_ref[...])
        prev = ext[pl.ds(7, tt), :]
        nxt = ext[pl.ds(9, tt), :]
        ps = pv + mup_ref[...] * (prev - pv) + mun_ref[...] * (nxt - pv)
        outs = _prep(ps, w2b_ref[...], w0c_ref[...], a2b_ref[...], a0c_ref[...], g2p_ref[...], kk_ref[...],
                     ka_ref[...], e2_ref[...])
        for o_ref, val in zip((r_o, v_o, kk_o, w_o, kd_o, b_o, g_o), outs):
            o_ref[...] = val

    hp, hn = _halo_specs(SHIFT_PAD, 8, tt, n_tok, 0)
    fulls = [mu_p, mu_n, w2b, w0c, a2b, a0c, g2p, k_k, k_a, e2]
    widths = (RW, RW, RW, 2 * RW, 2 * RW, 2 * RW, RW)
    return pl.pallas_call(
        body, name=name,
        out_shape=[jax.ShapeDtypeStruct((n_tok, c), F32) for c in widths],
        grid=(n_tok // tt,),
        in_specs=[pl.BlockSpec((tt, SHIFT_PAD), lambda i: (i, 0)), hp, hn]
        + [pl.BlockSpec(a.shape, lambda i: (0, 0)) for a in fulls],
        out_specs=[pl.BlockSpec((tt, c), lambda i: (i, 0)) for c in widths],
        scratch_shapes=[pltpu.VMEM((tt + 16, SHIFT_PAD), F32)],
        compiler_params=_cparams(("parallel",)),
    )(p, p, p, *fulls)


def _mix_prep_bwd(p, mu_p, mu_n, w2b, w0c, a2b, a0c, g2p, k_k, k_a, cts, *, seq, tt, name):
    n_tok = p.shape[0]
    tps = seq // tt
    e2 = _head_ones()
    acc_shapes = [w2b.shape, w0c.shape, a2b.shape, a0c.shape, g2p.shape, k_k.shape, k_a.shape]

    def body(p_ref, hp_ref, hn_ref, mup_ref, mun_ref, w2b_ref, w0c_ref, a2b_ref, a0c_ref, g2p_ref, kk_ref, ka_ref,
             e2_ref, dr, dv, dkk, dw, dkd, db, dg, dps_o, *rest):
        acc_refs, ext = rest[:-1], rest[-1]
        i = pl.program_id(0)
        first = (i % tps) == 0
        last = (i % tps) == tps - 1
        pv = p_ref[...]
        ext[pl.ds(0, 8), :] = jnp.where(first, 0.0, hp_ref[...])
        ext[pl.ds(8, tt), :] = pv
        ext[pl.ds(8 + tt, 8), :] = jnp.where(last, 0.0, hn_ref[...])
        prev = ext[pl.ds(7, tt), :]
        nxt = ext[pl.ds(9, tt), :]
        ps = pv + mup_ref[...] * (prev - pv) + mun_ref[...] * (nxt - pv)
        e2v = e2_ref[...]
        _, vjp = jax.vjp(lambda *a: _prep(*a, e2v), ps, w2b_ref[...], w0c_ref[...], a2b_ref[...], a0c_ref[...],
                         g2p_ref[...], kk_ref[...], ka_ref[...])
        grads = vjp((dr[...], dv[...], dkk[...], dw[...], dkd[...], db[...], dg[...]))
        dps_o[...] = grads[0]

        @pl.when(i == 0)
        def _():
            for r in acc_refs:
                r[...] = jnp.zeros_like(r)
        for r, val in zip(acc_refs, grads[1:]):
            r[...] += val

    hp, hn = _halo_specs(SHIFT_PAD, 8, tt, n_tok, 0)
    fulls = [mu_p, mu_n, w2b, w0c, a2b, a0c, g2p, k_k, k_a, e2]
    return pl.pallas_call(
        body, name=name,
        out_shape=[jax.ShapeDtypeStruct((n_tok, SHIFT_PAD), F32)] + [jax.ShapeDtypeStruct(s, F32) for s in acc_shapes],
        grid=(n_tok // tt,),
        in_specs=[pl.BlockSpec((tt, SHIFT_PAD), lambda i: (i, 0)), hp, hn]
        + [pl.BlockSpec(a.shape, lambda i: (0, 0)) for a in fulls]
        + [pl.BlockSpec((tt, c.shape[1]), lambda i: (i, 0)) for c in cts],
        out_specs=[pl.BlockSpec((tt, SHIFT_PAD), lambda i: (i, 0))] + [pl.BlockSpec(s, lambda i: (0, 0)) for s in acc_shapes],
        scratch_shapes=[pltpu.VMEM((tt + 16, SHIFT_PAD), F32)],
        compiler_params=_cparams(("arbitrary",)),
    )(p, p, p, *fulls, *cts)


def _shift_bwd(dps, p, mu_p, mu_n, *, seq, tt, name):
    n_tok = p.shape[0]
    tps = seq // tt

    def body(d_ref, dhp_ref, dhn_ref, p_ref, php_ref, phn_ref, mup_ref, mun_ref, dp_o, dmup_o, dmun_o, ext):
        i = pl.program_id(0)
        first = (i % tps) == 0
        last = (i % tps) == tps - 1
        mup, mun = mup_ref[...], mun_ref[...]
        dv = d_ref[...]
        pv = p_ref[...]
        ext[pl.ds(0, 8), :] = jnp.where(first, 0.0, dhp_ref[...])
        ext[pl.ds(8, tt), :] = dv
        ext[pl.ds(8 + tt, 8), :] = jnp.where(last, 0.0, dhn_ref[...])
        d_prev = ext[pl.ds(7, tt), :]
        d_next = ext[pl.ds(9, tt), :]
        dp_o[...] = (dv * (1.0 - mup - mun) + d_next * mup + d_prev * mun).astype(dp_o.dtype)
        ext[pl.ds(0, 8), :] = jnp.where(first, 0.0, php_ref[...])
        ext[pl.ds(8, tt), :] = pv
        ext[pl.ds(8 + tt, 8), :] = jnp.where(last, 0.0, phn_ref[...])
        p_prev = ext[pl.ds(7, tt), :]
        p_next = ext[pl.ds(9, tt), :]

        @pl.when(i == 0)
        def _():
            dmup_o[...] = jnp.zeros_like(dmup_o)
            dmun_o[...] = jnp.zeros_like(dmun_o)
        dmup_o[...] += jnp.sum(dv * (p_prev - pv), axis=0, keepdims=True)
        dmun_o[...] += jnp.sum(dv * (p_next - pv), axis=0, keepdims=True)

    hp, hn = _halo_specs(SHIFT_PAD, 8, tt, n_tok, 0)
    tile = pl.BlockSpec((tt, SHIFT_PAD), lambda i: (i, 0))
    full = pl.BlockSpec((1, SHIFT_PAD), lambda i: (0, 0))
    return pl.pallas_call(
        body, name=name,
        out_shape=[jax.ShapeDtypeStruct((n_tok, SHIFT_PAD), BF16), jax.ShapeDtypeStruct((1, SHIFT_PAD), F32),
                   jax.ShapeDtypeStruct((1, SHIFT_PAD), F32)],
        grid=(n_tok // tt,),
        in_specs=[tile, hp, hn, tile, hp, hn, full, full],
        out_specs=[tile, full, full],
        scratch_shapes=[pltpu.VMEM((tt + 16, SHIFT_PAD), F32)],
        compiler_params=_cparams(("arbitrary",)),
    )(dps, dps, dps, p, p, p, mu_p, mu_n)


def _mix_post(y0, y1, r, v, kd, g, lnx_g, lnx_b, r_k, *, tt, name):
    e2 = _head_ones()
    return _rowcall(lambda i, *a: (_post(*a),), [y0, y1, r, v, kd, g], [lnx_g, lnx_b, r_k, e2], [(RW, BF16)], [],
                    tt=tt, name=name)[0]


def _mix_post_bwd(y0, y1, r, v, kd, g, lnx_g, lnx_b, r_k, dout, *, tt, name):
    e2 = _head_ones()

    def fn(i, y0v, y1v, rv, vv, kdv, gv, dov, lg, lb, rk, e2v):
        _, vjp = jax.vjp(lambda *a: _post(*a, e2v), y0v, y1v, rv, vv, kdv, gv, lg, lb, rk)
        gr = vjp(dov.astype(F32))
        return gr[0], gr[2], gr[3], gr[4], gr[5], gr[6], gr[7], gr[8]
    return _rowcall(fn, [y0, y1, r, v, kd, g, dout], [lnx_g, lnx_b, r_k, e2],
                    [(RW, F32), (RW, F32), (RW, F32), (2 * RW, F32), (RW, F32)], [(1, RW), (1, RW), (1, RW)],
                    tt=tt, name=name)


def _conv_fwd(p, dw, db, ln_g, ln_b, *, seq, tt, name):
    n_tok = p.shape[0]
    tps = seq // tt

    def glu(x, gate):
        return x * _sigmoid(gate)

    def body(u_ref, g_ref, uhp, ghp, uhn, ghn, dw_ref, db_ref, lg_ref, lb_ref, yc_o, y_o, ext):
        i = pl.program_id(0)
        first = (i % tps) == 0
        last = (i % tps) == tps - 1
        ext[pl.ds(0, 16), :] = jnp.where(first, 0.0, glu(uhp[...], ghp[...]))
        ext[pl.ds(16, tt), :] = glu(u_ref[...], g_ref[...])
        ext[pl.ds(16 + tt, 16), :] = jnp.where(last, 0.0, glu(uhn[...], ghn[...]))
        acc = jnp.zeros((tt, CW), F32) + db_ref[...]
        for k in range(CONV_K):
            acc = acc + ext[pl.ds(k + 1, tt), :] * dw_ref[pl.ds(k, 1), :]
        yc_o[...] = acc
        y_o[...] = _conv_post(acc, lg_ref[...], lb_ref[...]).astype(y_o.dtype)

    uhp_s, uhn_s = _halo_specs(CW, 16, tt, n_tok, 4)
    ghp_s, ghn_s = _halo_specs(CW, 16, tt, n_tok, 5)
    fulls = [dw, db, ln_g, ln_b]
    return pl.pallas_call(
        body, name=name,
        out_shape=[jax.ShapeDtypeStruct((n_tok, CW), F32), jax.ShapeDtypeStruct((n_tok, CW), BF16)],
        grid=(n_tok // tt,),
        in_specs=[pl.BlockSpec((tt, CW), lambda i: (i, 4)), pl.BlockSpec((tt, CW), lambda i: (i, 5)),
                  uhp_s, ghp_s, uhn_s, ghn_s] + [pl.BlockSpec(a.shape, lambda i: (0, 0)) for a in fulls],
        out_specs=[pl.BlockSpec((tt, CW), lambda i: (i, 0)), pl.BlockSpec((tt, CW), lambda i: (i, 0))],
        scratch_shapes=[pltpu.VMEM((tt + 32, CW), F32)],
        compiler_params=_cparams(("parallel",)),
    )(p, p, p, p, p, p, *fulls)


def _conv_post_bwd(yc, dy, ln_g, ln_b, *, tt, name):
    def fn(i, ycv, dyv, lg, lb):
        _, vjp = jax.vjp(_conv_post, ycv, lg, lb)
        dyc, dg, dbb = vjp(dyv.astype(F32))
        return dyc, dg, dbb, jnp.sum(dyc, axis=0, keepdims=True)
    return _rowcall(fn, [yc, dy], [ln_g, ln_b], [(CW, F32)], [(1, CW), (1, CW), (1, CW)], tt=tt, name=name)


def _conv_bwd(dyc, p, dw, *, seq, tt, name):
    n_tok = p.shape[0]
    tps = seq // tt

    def body(d_ref, dhp, dhn, u_ref, g_ref, uhp, ghp, uhn, ghn, dw_ref, dp_o, ddw_o, ext):
        i = pl.program_id(0)
        first = (i % tps) == 0
        last = (i % tps) == tps - 1
        dv = d_ref[...]
        ext[pl.ds(0, 16), :] = jnp.where(first, 0.0, dhp[...])
        ext[pl.ds(16, tt), :] = dv
        ext[pl.ds(16 + tt, 16), :] = jnp.where(last, 0.0, dhn[...])
        du = jnp.zeros((tt, CW), F32)
        for k in range(CONV_K):
            du = du + ext[pl.ds(31 - k, tt), :] * dw_ref[pl.ds(k, 1), :]
        uv, gv = u_ref[...], g_ref[...]
        sg = _sigmoid(gv)
        dp_o[:, 0:CW] = (du * sg).astype(dp_o.dtype)
        dp_o[:, CW:2 * CW] = (du * uv * sg * (1.0 - sg)).astype(dp_o.dtype)
        ext[pl.ds(0, 16), :] = jnp.where(first, 0.0, uhp[...] * _sigmoid(ghp[...]))
        ext[pl.ds(16, tt), :] = uv * sg
        ext[pl.ds(16 + tt, 16), :] = jnp.where(last, 0.0, uhn[...] * _sigmoid(ghn[...]))

        @pl.when(i == 0)
        def _():
            ddw_o[...] = jnp.zeros_like(ddw_o)
        for k in range(CONV_K):
            ddw_o[pl.ds(k, 1), :] += jnp.sum(dv * ext[pl.ds(k + 1, tt), :], axis=0, keepdims=True)

    dhp_s, dhn_s = _halo_specs(CW, 16, tt, n_tok, 0)
    uhp_s, uhn_s = _halo_specs(CW, 16, tt, n_tok, 4)
    ghp_s, ghn_s = _halo_specs(CW, 16, tt, n_tok, 5)
    return pl.pallas_call(
        body, name=name,
        out_shape=[jax.ShapeDtypeStruct((n_tok, 2 * CW), BF16), jax.ShapeDtypeStruct((32, CW), F32)],
        grid=(n_tok // tt,),
        in_specs=[pl.BlockSpec((tt, CW), lambda i: (i, 0)), dhp_s, dhn_s,
                  pl.BlockSpec((tt, CW), lambda i: (i, 4)), pl.BlockSpec((tt, CW), lambda i: (i, 5)),
                  uhp_s, ghp_s, uhn_s, ghn_s, pl.BlockSpec(dw.shape, lambda i: (0, 0))],
        out_specs=[pl.BlockSpec((tt, 2 * CW), lambda i: (i, 0)), pl.BlockSpec((32, CW), lambda i: (0, 0))],
        scratch_shapes=[pltpu.VMEM((tt + 32, CW), F32)],
        compiler_params=_cparams(("arbitrary",)),
    )(dyc, dyc, dyc, p, p, p, p, p, p, dw)


def _segdot(hi, lo, e2):
    outs = []
    for c in range(hi.shape[1] // 256):
        lhs = jnp.concatenate([hi[:, 256 * c:256 * (c + 1)], lo[:, 256 * c:256 * (c + 1)]], axis=1)
        outs.append(jnp.dot(lhs, e2, preferred_element_type=F32))
    return jnp.concatenate(outs, axis=1)


SCAN_PASSES = 1


def _seg_streams(parts, e2):
    if SCAN_PASSES == 1:
        hi = jnp.concatenate([p.astype(BF16) for p in parts], axis=0)
        full = jnp.concatenate([jnp.dot(hi[:, 256 * c:256 * (c + 1)], e2[:256], preferred_element_type=F32)
                                for c in range(RW // 256)], axis=1)
    else:
        pieces = [_split16(p) for p in parts]
        full = _segdot(jnp.concatenate([h for h, _ in pieces], axis=0), jnp.concatenate([l for _, l in pieces], axis=0), e2)
    return [full[s * HEAD:(s + 1) * HEAD] for s in range(len(parts))]


def _diag_mask():
    return lax.broadcasted_iota(jnp.int32, (HEAD, RW), 0) == lax.broadcasted_iota(jnp.int32, (HEAD, RW), 1) % HEAD


def _col_form(rows, dmask, e2):
    his, los = [], []
    for x in rows:
        hi = x.astype(BF16).astype(F32)
        lo = x - hi
        his.append(jnp.where(dmask, jnp.broadcast_to(hi, (HEAD, RW)), 0.0).astype(BF16))
        los.append(jnp.where(dmask, jnp.broadcast_to(lo, (HEAD, RW)), 0.0).astype(BF16))
    full = _segdot(jnp.concatenate(his, axis=0), jnp.concatenate(los, axis=0), e2)
    return [full[s * HEAD:(s + 1) * HEAD] for s in range(len(rows))]


def _row_form(col, dmask):
    return jnp.sum(jnp.where(dmask, col, 0.0), axis=0, keepdims=True)


def _row_sum(x):
    return jnp.sum(x, axis=0, keepdims=True)


def _wkv_fwd(r, v, kk, w, kd, b, *, tb, name):
    bsz, seq, _ = r.shape
    nb = seq // tb
    ns = 2 * bsz
    e2 = _head_ones()

    def body(r0, r1, v0, v1, k0, k1, w0, w1, kd0, kd1, b0, b1, e2_ref, y0_o, y1_o, sp_o, s_ref):
        i = pl.program_id(0)

        @pl.when(i == 0)
        def _():
            s_ref[...] = jnp.zeros_like(s_ref)

        e2v = e2_ref[...]
        dmask = _diag_mask()
        y_refs = (y0_o, y1_o)

        def step(j, carry):
            tl = (j, tb - 1 - j)

            def rows(refs):
                return [refs[d][bb, pl.ds(tl[d], 1), :] for d in (0, 1) for bb in range(bsz)]

            kk_r, w_r, b_r, kd_r, r_r = rows((k0, k1)), rows((w0, w1)), rows((b0, b1)), rows((kd0, kd1)), rows((r0, r1))
            s_old = [s_ref[s * HEAD:(s + 1) * HEAD, :] for s in range(ns)]
            for s in range(ns):
                sp_o[s, pl.ds(j, 1), :, :] = s_old[s].reshape(1, HEAD, RW)
            sa = _seg_streams([s_old[s] * kk_r[s] for s in range(ns)], e2v)
            vc = _col_form(rows((v0, v1)), dmask, e2v)
            s_new = [s_old[s] * w_r[s] - sa[s] * b_r[s] + vc[s] * kd_r[s] for s in range(ns)]
            for s in range(ns):
                s_ref[s * HEAD:(s + 1) * HEAD, :] = s_new[s]
            ycol = _seg_streams([s_new[s] * r_r[s] for s in range(ns)], e2v)
            for d in (0, 1):
                for bb in range(bsz):
                    y_refs[d][bb, pl.ds(tl[d], 1), :] = _row_form(ycol[d * bsz + bb], dmask)
            return carry

        lax.fori_loop(0, tb, step, 0)

    def blk(width_idx, rev):
        if rev:
            return pl.BlockSpec((bsz, tb, RW), lambda i: (0, nb - 1 - i, width_idx))
        return pl.BlockSpec((bsz, tb, RW), lambda i: (0, i, width_idx))

    in_specs = [blk(0, False), blk(0, True)] * 3 + [blk(0, False), blk(1, True)] * 3
    in_specs.append(pl.BlockSpec(e2.shape, lambda i: (0, 0)))
    return pl.pallas_call(
        body, name=name,
        out_shape=[jax.ShapeDtypeStruct((bsz, seq, RW), F32), jax.ShapeDtypeStruct((bsz, seq, RW), F32),
                   jax.ShapeDtypeStruct((ns, seq, HEAD, RW), F32)],
        grid=(nb,),
        in_specs=in_specs,
        out_specs=[blk(0, False), blk(0, True), pl.BlockSpec((ns, tb, HEAD, RW), lambda i: (0, i, 0, 0))],
        scratch_shapes=[pltpu.VMEM((ns * HEAD, RW), F32)],
        compiler_params=_cparams(("arbitrary",)),
    )(r, r, v, v, kk, kk, w, w, kd, kd, b, b, e2)


def _wkv_bwd(r, v, kk, w, kd, b, dy, sp, *, tb, name):
    bsz, seq, _ = r.shape
    nb = seq // tb
    ns = 2 * bsz
    e2 = _head_ones()

    def body(r0, r1, v0, v1, k0, k1, dy0, dy1, w0, w1, kd0, kd1, b0, b1, sp_ref, e2_ref, *rest):
        outs, g_ref = rest[:-1], rest[-1]
        i = pl.program_id(0)

        @pl.when(i == 0)
        def _():
            g_ref[...] = jnp.zeros_like(g_ref)

        e2v = e2_ref[...]
        dmask = _diag_mask()

        def step(jj, carry):
            sl = tb - 1 - jj
            tl = (sl, jj)

            def rows(refs):
                return [refs[d][bb, pl.ds(tl[d], 1), :] for d in (0, 1) for bb in range(bsz)]

            kk_r, w_r, b_r, kd_r, r_r = rows((k0, k1)), rows((w0, w1)), rows((b0, b1)), rows((kd0, kd1)), rows((r0, r1))
            s_old = [sp_ref[s, pl.ds(sl, 1), :, :].reshape(HEAD, RW) for s in range(ns)]
            sa = _seg_streams([s_old[s] * kk_r[s] for s in range(ns)], e2v)
            vc = _col_form(rows((v0, v1)), dmask, e2v)
            dyc = _col_form(rows((dy0, dy1)), dmask, e2v)
            gt = [g_ref[s * HEAD:(s + 1) * HEAD, :] + dyc[s] * r_r[s] for s in range(ns)]
            both = _seg_streams([gt[s] * b_r[s] for s in range(ns)] + [gt[s] * kd_r[s] for s in range(ns)], e2v)
            gb, dvc = both[:ns], both[ns:]
            for d in (0, 1):
                for bb in range(bsz):
                    s = d * bsz + bb
                    at = (bb, pl.ds(tl[d], 1), slice(None))
                    s_new = s_old[s] * w_r[s] - sa[s] * b_r[s] + vc[s] * kd_r[s]
                    outs[0 + d][at] = _row_sum(s_new * dyc[s])
                    outs[2 + d][at] = _row_form(dvc[s], dmask)
                    outs[4 + d][at] = -_row_sum(s_old[s] * gb[s])
                    outs[6 + d][at] = _row_sum(s_old[s] * gt[s])
                    outs[8 + d][at] = _row_sum(gt[s] * vc[s])
                    outs[10 + d][at] = -_row_sum(sa[s] * gt[s])
                    g_ref[s * HEAD:(s + 1) * HEAD, :] = gt[s] * w_r[s] - gb[s] * kk_r[s]
            return carry

        lax.fori_loop(0, tb, step, 0)

    def blk(width_idx, rev):
        if rev:
            return pl.BlockSpec((bsz, tb, RW), lambda i: (0, nb - 1 - i, width_idx))
        return pl.BlockSpec((bsz, tb, RW), lambda i: (0, i, width_idx))

    in_specs = [blk(0, True), blk(0, False)] * 4 + [blk(0, True), blk(1, False)] * 3
    in_specs.append(pl.BlockSpec((ns, tb, HEAD, RW), lambda i: (0, nb - 1 - i, 0, 0)))
    in_specs.append(pl.BlockSpec(e2.shape, lambda i: (0, 0)))
    return pl.pallas_call(
        body, name=name,
        out_shape=[jax.ShapeDtypeStruct((bsz, seq, RW), F32)] * 12,
        grid=(nb,),
        in_specs=in_specs,
        out_specs=[blk(0, True), blk(0, False)] * 6,
        scratch_shapes=[pltpu.VMEM((ns * HEAD, RW), F32)],
        compiler_params=_cparams(("arbitrary",)),
    )(r, r, v, v, kk, kk, dy, dy, w, w, kd, kd, b, b, sp, e2)


NT_DIMS = (((1,), (1,)), ((), ()))
TN_DIMS = (((0,), (0,)), ((), ()))
NN_DIMS = (((1,), (0,)), ((), ()))


def _dot3(a, b, dims):
    ah, al = _split16(a)
    bh, bl = _split16(b)
    return (lax.dot_general(ah, bh, dims, preferred_element_type=F32)
            + lax.dot_general(ah, bl, dims, preferred_element_type=F32)
            + lax.dot_general(al, bh, dims, preferred_element_type=F32))


def _head_rows(rows, hm8):
    return jnp.concatenate([jnp.where(hm8, jnp.broadcast_to(x, (8, RW)), 0.0) for x in rows], axis=0)


def _scan_masks(ns):
    hm8 = lax.broadcasted_iota(jnp.int32, (8, RW), 0) == lax.broadcasted_iota(jnp.int32, (8, RW), 1) // HEAD
    bd = (lax.broadcasted_iota(jnp.int32, (ns * HEAD, ns * 8), 0) // HEAD
          == lax.broadcasted_iota(jnp.int32, (ns * HEAD, ns * 8), 1) // 8)
    return hm8, bd


def _place_cols(x, bsz):
    seq = x.shape[1]
    xc = jnp.swapaxes(x.reshape(bsz, seq, 8, HEAD), 2, 3)
    streams = jnp.concatenate([xc, jnp.flip(xc, axis=1)], axis=0)
    eye = jnp.eye(2 * bsz, dtype=x.dtype)
    placed = jnp.einsum("stih,sS->tsiSh", streams, eye)
    return placed.reshape(seq, 2 * bsz * HEAD, 2 * bsz * 8)


def _extract_cols(col, bsz):
    seq = col.shape[0]
    c5 = col.reshape(seq, 2 * bsz, HEAD, 2 * bsz, 8)
    diag = jnp.stack([c5[:, s, :, s, :] for s in range(2 * bsz)])
    rows = jnp.swapaxes(diag, 2, 3).reshape(2 * bsz, seq, RW)
    return rows[:bsz], jnp.flip(rows[bsz:], axis=1)


def _wkv2_fwd(r, kk, w, kd, b, vp, *, tb, name):
    bsz, seq, _ = r.shape
    nb = seq // tb
    ns = 2 * bsz
    nr, nc = ns * HEAD, ns * 8

    def body(r0, r1, k0, k1, w0, w1, kd0, kd1, b0, b1, vp_ref, y_o, sp_o, s_ref):
        i = pl.program_id(0)

        @pl.when(i == 0)
        def _():
            s_ref[...] = jnp.zeros_like(s_ref)

        hm8, bd = _scan_masks(ns)

        def step(j, carry):
            tl = (j, tb - 1 - j)

            def rows(refs):
                return [refs[d][bb, pl.ds(tl[d], 1), :] for d in (0, 1) for bb in range(bsz)]

            w_r = rows((w0, w1))
            s_old = s_ref[...]
            for s in range(ns):
                sp_o[s, pl.ds(j, 1), :, :] = s_old[s * HEAD:(s + 1) * HEAD].reshape(1, HEAD, RW)
            kkt = _head_rows(rows((k0, k1)), hm8).astype(BF16)
            sa = jnp.where(bd, lax.dot_general(s_old.astype(BF16), kkt, NT_DIMS, preferred_element_type=F32), 0.0)
            upd = (_dot3(vp_ref[j], _head_rows(rows((kd0, kd1)), hm8), NN_DIMS)
                   - _dot3(sa, _head_rows(rows((b0, b1)), hm8), NN_DIMS))
            s_new = jnp.concatenate([s_old[s * HEAD:(s + 1) * HEAD] * w_r[s] for s in range(ns)], axis=0) + upd
            s_ref[...] = s_new
            rt = _head_rows(rows((r0, r1)), hm8).astype(BF16)
            y_o[j] = lax.dot_general(s_new.astype(BF16), rt, NT_DIMS, preferred_element_type=F32)
            return carry

        lax.fori_loop(0, tb, step, 0)

    def blk(width_idx, rev):
        if rev:
            return pl.BlockSpec((bsz, tb, RW), lambda i: (0, nb - 1 - i, width_idx))
        return pl.BlockSpec((bsz, tb, RW), lambda i: (0, i, width_idx))

    col_blk = pl.BlockSpec((tb, nr, nc), lambda i: (i, 0, 0))
    in_specs = [blk(0, False), blk(0, True)] * 2 + [blk(0, False), blk(1, True)] * 3 + [col_blk]
    return pl.pallas_call(
        body, name=name,
        out_shape=[jax.ShapeDtypeStruct((seq, nr, nc), F32), jax.ShapeDtypeStruct((ns, seq, HEAD, RW), F32)],
        grid=(nb,),
        in_specs=in_specs,
        out_specs=[col_blk, pl.BlockSpec((ns, tb, HEAD, RW), lambda i: (0, i, 0, 0))],
        scratch_shapes=[pltpu.VMEM((nr, RW), F32)],
        compiler_params=_cparams(("arbitrary",)),
    )(r, r, kk, kk, w, w, kd, kd, b, b, vp)


def _scan_cotangents(post_g, scan_g, *, tt, name):
    dr_p, dv_p, dkd_p = post_g
    cat = functools.partial(jnp.concatenate, axis=1)

    def fn(i, drp, dvp, dkdp, dr0, dr1, dv0, dv1, dk0, dk1, dw0, dw1, dkd0, dkd1, db0, db1):
        return (drp + dr0 + dr1, dvp + dv0 + dv1, dk0 + dk1, cat([dw0, dw1]), dkdp + cat([dkd0, dkd1]), cat([db0, db1]))
    return _rowcall(fn, [dr_p, dv_p, dkd_p, *scan_g], [],
                    [(RW, F32), (RW, F32), (RW, F32), (2 * RW, F32), (2 * RW, F32), (2 * RW, F32)], [], tt=tt, name=name)


def _block_diag2(w):
    z = jnp.zeros_like(w[0])
    return jnp.concatenate([jnp.concatenate([w[0], z], axis=1), jnp.concatenate([z, w[1]], axis=1)], axis=0)


def _pad_in_cols(a):
    z = jnp.zeros(a.shape[:-1] + (SHIFT_PAD - SHIFT_COLS,), a.dtype)
    return jnp.concatenate([a[..., :SHIFT_COLS], z, a[..., SHIFT_COLS:]], axis=-1)


def _unpad_in_cols(a):
    return jnp.concatenate([a[..., :SHIFT_COLS], a[..., SHIFT_PAD:]], axis=-1)


def _follow(small, token):
    return small if token is None else small + token[0:1, 0:1]


def _local_step(x, target, wts, *, tt, tb, start_token=None, more_weights=None, grads_ready=None):
    bsz, seq, _ = x.shape
    n_tok = bsz * seq
    row = lambda a: a.reshape(1, -1).astype(F32)
    x0 = x.reshape(n_tok, D_MODEL)
    tgt = target.reshape(n_tok, D_MODEL)
    ln = {k: row(wts[k]) for k in ("ln1_g", "ln1_b", "ln2_g", "ln2_b", "ln3_g", "ln3_b")}
    if grads_ready is None:
        grads_ready = lambda names, slabs: None

    w1i, w1o = wts["ffn1_w_in"], wts["ffn1_w_out"]
    h1, act1 = _ffn_in(x0, w1i, tm=TM_FFN, after=start_token, name="ffn1_in")
    z1, x1, x1b = _mm_ln([act1], w1o, x0, ln["ln1_g"], ln["ln1_b"], 0.5, tm=TM_LN, name="ffn1_out_ln1")
    if more_weights is not None:
        wts = {**wts, **more_weights("mix", x1b)}
    win = _pad_in_cols(wts["w_in"])
    zpad = jnp.zeros((1, SHIFT_PAD - SHIFT_COLS), F32)
    mu_p = jnp.concatenate([row(wts["mu_prev"]), zpad], axis=1)
    mu_n = jnp.concatenate([row(wts["mu_next"]), zpad], axis=1)
    w2b, a2b = _block_diag2(wts["w2"]), _block_diag2(wts["a2"])
    w0c, a0c = row(wts["w0"]), row(wts["a0"])
    g2p = jnp.concatenate([wts["g2"], jnp.zeros((GATE_PAD - GATE_LORA, RW), F32)], axis=0)
    k_k, k_a, r_k = row(wts["k_k"]), row(wts["k_a"]), row(wts["r_k"])
    lnx_g, lnx_b = row(wts["lnx_g"]), row(wts["lnx_b"])
    cdw, cb, clg, clb = wts["conv_dw"], row(wts["conv_b"]), row(wts["conv_ln_g"]), row(wts["conv_ln_b"])
    small = (mu_p, mu_n, w2b, w0c, a2b, a0c, g2p, k_k, k_a)
    seq3 = lambda a: a.reshape(bsz, seq, a.shape[-1])
    flat = lambda a: a.reshape(n_tok, a.shape[-1])

    p = _matmul(x1b, win, name="proj_in")
    r, v, kk, w, kd, b, g = _mix_prep(p, *small, seq=seq, tt=tt, name="mix_prep")
    y0, y1, sp = _wkv_fwd(seq3(r), seq3(v), seq3(kk), seq3(w), seq3(kd), seq3(b), tb=tb, name="wkv_fwd")
    y0, y1 = flat(y0), flat(y1)
    yr = _mix_post(y0, y1, r, v, kd, g, lnx_g, lnx_b, r_k, tt=tt, name="mix_post")
    yc, yv = _conv_fwd(p, cdw, cb, clg, clb, seq=seq, tt=tt, name="conv_fwd")
    if more_weights is not None:
        wts = {**wts, **more_weights("out", yr)}
    wout, w2i, w2o = wts["w_out"], wts["ffn2_w_in"], wts["ffn2_w_out"]
    z2, x2, x2b = _mm_ln([yr, yv], wout, x1, ln["ln2_g"], ln["ln2_b"], 1.0, tm=TM_LN, name="proj_out_ln2")
    h2, act2 = _ffn_in(x2b, w2i, tm=TM_FFN, name="ffn2_in")
    z3, x3, _ = _mm_ln([act2], w2o, x2, ln["ln3_g"], ln["ln3_b"], 0.5, tm=TM_LN, name="ffn2_out_ln3")
    dx3, loss_part = _loss_fwd_bwd(x3, tgt, tt=tt, name="loss")

    gr = {}
    slab_rows = lambda a: a.reshape((N_CHIPS, a.shape[0] // N_CHIPS) + a.shape[1:])
    dw_kw = dict(ta=True, out_dtype=BF16)
    dz3, gr["ln3_g"], gr["ln3_b"] = _ln_bwd(z3, dx3, ln["ln3_g"], ln["ln3_b"], tt=tt, name="ln3_bwd")
    dh2 = _ffn_out_bwd(dz3, w2o, h2, tm=TM_FFN, name="ffn2_out_dx")
    gr["ffn2_w_out"] = slab_rows(_matmul(act2, dz3, scale=0.5, tm=D_FF // 2, name="ffn2_out_dw", **dw_kw))
    dx2 = _mm_nt_res([dh2], w2i, dz3, tm=TM_FFN, name="ffn2_in_dx")
    gr["ffn2_w_in"] = _matmul(x2b, dh2, col_slabs=True, tn=2 * D_FF // N_CHIPS, name="ffn2_in_dw", **dw_kw)
    dz2, gr["ln2_g"], gr["ln2_b"] = _ln_bwd(z2, dx2, ln["ln2_g"], ln["ln2_b"], tt=tt, name="ln2_bwd")
    dmix = _matmul(dz2, wout, tb=True, name="proj_out_dx")
    gr["w_out"] = slab_rows(jnp.concatenate([_matmul(yr, dz2, name="proj_out_dw_rwkv", **dw_kw),
                                             _matmul(yv, dz2, name="proj_out_dw_conv", **dw_kw)], axis=0))
    tok = grads_ready(("ffn2_w_out", "ffn2_w_in", "w_out"), [gr["ffn2_w_out"], gr["ffn2_w_in"], gr["w_out"]])
    dyr, dyv = (dmix, RW, 0), (dmix, RW, 1)
    dy, dr_p, dv_p, dkd_p, dg, gr["lnx_g"], gr["lnx_b"], gr["r_k"] = _mix_post_bwd(
        y0, y1, r, v, kd, g, _follow(lnx_g, tok), lnx_b, r_k, dyr, tt=tt, name="mix_post_bwd")
    scan_g = _wkv_bwd(seq3(r), seq3(v), seq3(kk), seq3(w), seq3(kd), seq3(b), seq3(dy), sp, tb=tb, name="wkv_bwd")
    cts = _scan_cotangents((dr_p, dv_p, dkd_p), [flat(a) for a in scan_g], tt=tt, name="scan_cts")
    dyc, gr["conv_ln_g"], gr["conv_ln_b"], gr["conv_b"] = _conv_post_bwd(yc, dyv, clg, clb, tt=tt, name="conv_post_bwd")
    dpc, ddw = _conv_bwd(dyc, p, cdw, seq=seq, tt=tt, name="conv_bwd")
    gr["conv_dw"] = ddw[:CONV_K]
    dps, dw2b, dw0c, da2b, da0c, dg2p, gr["k_k"], gr["k_a"] = _mix_prep_bwd(
        p, *small, [*cts, dg], seq=seq, tt=tt, name="mix_prep_bwd")
    gr["w2"] = jnp.stack([dw2b[:LORA, :RW], dw2b[LORA:, RW:]])
    gr["a2"] = jnp.stack([da2b[:LORA, :RW], da2b[LORA:, RW:]])
    gr["w0"], gr["a0"], gr["g2"] = dw0c.reshape(2, RW), da0c.reshape(2, RW), dg2p[:GATE_LORA]
    dpsh, dmu_p, dmu_n = _shift_bwd(dps, p, mu_p, mu_n, seq=seq, tt=tt, name="shift_bwd")
    gr["mu_prev"], gr["mu_next"] = dmu_p[:, :SHIFT_COLS], dmu_n[:, :SHIFT_COLS]
    dx1 = _mm_nt_res([dpsh, dpc], win, dz2, tm=TM_FFN, name="proj_in_dx")
    dwin = jnp.concatenate([_matmul(x1b, dpsh, name="proj_in_dw_shift", **dw_kw)[:, :SHIFT_COLS],
                            _matmul(x1b, dpc, name="proj_in_dw_conv", **dw_kw)], axis=1)
    gr["w_in"] = jnp.moveaxis(dwin.reshape(D_MODEL, N_CHIPS, IN_COLS // N_CHIPS), 1, 0)
    tok = grads_ready(("w_in",), [gr["w_in"]])
    dz1, gr["ln1_g"], gr["ln1_b"] = _ln_bwd(z1, dx1, _follow(ln["ln1_g"], tok), ln["ln1_b"], tt=tt, name="ln1_bwd")
    dh1 = _ffn_out_bwd(dz1, w1o, h1, tm=TM_FFN, name="ffn1_out_dx")
    gr["ffn1_w_out"] = slab_rows(_matmul(act1, dz1, scale=0.5, tm=D_FF // 2, name="ffn1_out_dw", **dw_kw))
    tok = grads_ready(("ffn1_w_out",), [gr["ffn1_w_out"]])
    gr["ffn1_w_in"] = _matmul(x0, dh1, col_slabs=True, tn=2 * D_FF // N_CHIPS, after=tok, name="ffn1_in_dw", **dw_kw)
    tok = grads_ready(("ffn1_w_in",), [gr["ffn1_w_in"]])
    dx0 = _mm_nt_res([dh1], w1i, dz1, tm=TM_FFN, after=tok, name="ffn1_in_dx")
    return loss_part, dx0.reshape(bsz, seq, D_MODEL), gr


def _mesh_pos():
    return lax.axis_index("x"), lax.axis_index("y"), lax.axis_index("c")


def _other_chips(x, y):
    return [(1 - x, y), (x, 1 - y), (1 - x, 1 - y)]


def _gather_chips(shards, *, name):
    n = len(shards)

    def body(*refs):
        ins, outs = refs[:n], refs[n:2 * n]
        send_sems, recv_sems, loc_sems = refs[2 * n:]
        x, y, c = _mesh_pos()
        q = 2 * x + y
        peers = _other_chips(x, y)
        local = [pltpu.make_async_copy(ins[a], outs[a].at[q], loc_sems.at[a]) for a in range(n)]
        for cp in local:
            cp.start()
        sends = [[pltpu.make_async_remote_copy(ins[a], outs[a].at[q], send_sems.at[a, k], recv_sems.at[a, k],
                                               device_id=(px, py, c), device_id_type=MESH)
                  for k, (px, py) in enumerate(peers)] for a in range(n)]
        for a in range(n):
            for cp in sends[a]:
                cp.start()
        for a in range(n):
            for k, (px, py) in enumerate(peers):
                pltpu.make_async_remote_copy(ins[a], outs[a].at[2 * px + py], send_sems.at[a, k], recv_sems.at[a, k],
                                             device_id=(px, py, c), device_id_type=MESH).wait_recv()
        for a in range(n):
            for cp in sends[a]:
                cp.wait_send()
            local[a].wait()

    any_spec = pl.BlockSpec(memory_space=pl.ANY)
    return pl.pallas_call(
        body, name=name,
        out_shape=[jax.ShapeDtypeStruct((N_CHIPS,) + s.shape, s.dtype) for s in shards],
        in_specs=[any_spec] * n, out_specs=[any_spec] * n,
        scratch_shapes=[pltpu.SemaphoreType.DMA((n, 3)), pltpu.SemaphoreType.DMA((n, 3)), pltpu.SemaphoreType.DMA((n,))],
        compiler_params=pltpu.CompilerParams(has_side_effects=True),
    )(*shards)


HBM_SPEC = pl.BlockSpec(memory_space=pltpu.HBM)
SEM_SPEC = pl.BlockSpec(memory_space=pltpu.SEMAPHORE)
ANY_SPEC = pl.BlockSpec(memory_space=pl.ANY)
SIDE_EFFECT = pltpu.SideEffectType.DATAFLOW_SIDE_EFFECTING


def _chip_copies(src_refs, land_refs, send_sems, recv_sems, scatter):
    x, y, c = _mesh_pos()
    return [pltpu.make_async_remote_copy(src.at[2 * px + py] if scatter else src, land.at[k], send_sems.at[3 * a + k],
                                         recv_sems.at[3 * a + k], device_id=(px, py, c), device_id_type=MESH)
            for a, (src, land) in enumerate(zip(src_refs, land_refs)) for k, (px, py) in enumerate(_other_chips(x, y))]


def _exchange_start(srcs, *, scatter, after, name):
    n = len(srcs)
    lands = [lax.empty((3,) + (s.shape[1:] if scatter else s.shape), s.dtype) for s in srcs]

    def body(*refs):
        src_refs, land_refs = refs[:n], refs[n:2 * n]
        send_sems, recv_sems = refs[2 * n + 1:2 * n + 3]
        token = refs[-1]
        for cp in _chip_copies(src_refs, land_refs, send_sems, recv_sems, scatter):
            cp.start()
        token[...] = jnp.zeros_like(token)

    hbm = lambda a: pltpu.with_memory_space_constraint(a, pltpu.HBM)
    outs = pl.pallas_call(
        body, name=name,
        out_shape=(pltpu.SemaphoreType.DMA((3 * n,)), pltpu.SemaphoreType.DMA((3 * n,)),
                   *[pltpu.HBM(a.shape, a.dtype) for a in srcs + lands], jax.ShapeDtypeStruct((8, LANES), F32)),
        in_specs=[HBM_SPEC] * (2 * n) + [ANY_SPEC],
        out_specs=(SEM_SPEC, SEM_SPEC, *[HBM_SPEC] * (2 * n), pl.BlockSpec(memory_space=pltpu.VMEM)),
        input_output_aliases={i: 2 + i for i in range(2 * n)},
        compiler_params=pltpu.CompilerParams(has_side_effects=SIDE_EFFECT),
    )(*[hbm(a) for a in srcs + lands], after)
    return outs[0], outs[1], list(outs[2:2 + n]), list(outs[2 + n:2 + 2 * n]), outs[-1]


def _exchange_wait(started, *, scatter, after, name):
    send_sems, recv_sems, srcs, lands, _ = started
    n = len(srcs)

    def body(*refs):
        src_refs, land_refs = refs[:n], refs[n:2 * n]
        send_s, recv_s = refs[2 * n:2 * n + 2]
        for cp in _chip_copies(src_refs, land_refs, send_s, recv_s, scatter):
            cp.wait_send()
            cp.wait_recv()

    outs = pl.pallas_call(
        body, name=name,
        out_shape=tuple(pltpu.HBM(a.shape, a.dtype) for a in srcs + lands),
        in_specs=[HBM_SPEC] * (2 * n) + [SEM_SPEC, SEM_SPEC, ANY_SPEC],
        out_specs=tuple([HBM_SPEC] * (2 * n)),
        input_output_aliases={i: i for i in range(2 * n)},
        compiler_params=pltpu.CompilerParams(has_side_effects=SIDE_EFFECT),
    )(*srcs, *lands, send_sems, recv_sems, after)
    return list(outs[:n]), list(outs[n:])


def _by_chip(own, land):
    xi, yi, _ = _mesh_pos()
    q = 2 * xi + yi
    every = jnp.concatenate([own[None], land], axis=0)
    slabs = []
    for j in range(N_CHIPS):
        m = j ^ q
        slabs.append(lax.dynamic_index_in_dim(every, (m >> 1) + 2 * (m & 1), 0, keepdims=True))
    return jnp.concatenate(slabs, axis=0)


def _scatter_chips(stacks, *, name):
    n = len(stacks)

    def body(*refs):
        ins, outs = refs[:n], refs[n:2 * n]
        send_sems, recv_sems = refs[2 * n:]
        x, y, c = _mesh_pos()
        peers = _other_chips(x, y)
        sends = [[pltpu.make_async_remote_copy(ins[a].at[2 * px + py], outs[a].at[k], send_sems.at[a, k],
                                               recv_sems.at[a, k], device_id=(px, py, c), device_id_type=MESH)
                  for k, (px, py) in enumerate(peers)] for a in range(n)]
        for a in range(n):
            for cp in sends[a]:
                cp.start()
        for a in range(n):
            for cp in sends[a]:
                cp.wait_recv()
        for a in range(n):
            for cp in sends[a]:
                cp.wait_send()

    any_spec = pl.BlockSpec(memory_space=pl.ANY)
    return pl.pallas_call(
        body, name=name,
        out_shape=[jax.ShapeDtypeStruct((3,) + s.shape[1:], s.dtype) for s in stacks],
        in_specs=[any_spec] * n, out_specs=[any_spec] * n,
        scratch_shapes=[pltpu.SemaphoreType.DMA((n, 3)), pltpu.SemaphoreType.DMA((n, 3))],
        compiler_params=pltpu.CompilerParams(has_side_effects=True),
    )(*stacks)


def _swap_sibling(arrs, *, name):
    n = len(arrs)

    def body(*refs):
        ins, outs = refs[:n], refs[n:2 * n]
        send_sems, recv_sems = refs[2 * n:]
        x, y, c = _mesh_pos()
        cps = [pltpu.make_async_remote_copy(ins[a], outs[a], send_sems.at[a], recv_sems.at[a],
                                            device_id=(x, y, 1 - c), device_id_type=MESH) for a in range(n)]
        for cp in cps:
            cp.start()
        for cp in cps:
            cp.wait_recv()
        for cp in cps:
            cp.wait_send()

    any_spec = pl.BlockSpec(memory_space=pl.ANY)
    return pl.pallas_call(
        body, name=name,
        out_shape=[jax.ShapeDtypeStruct(s.shape, s.dtype) for s in arrs],
        in_specs=[any_spec] * n, out_specs=[any_spec] * n,
        scratch_shapes=[pltpu.SemaphoreType.DMA((n,)), pltpu.SemaphoreType.DMA((n,))],
        compiler_params=pltpu.CompilerParams(has_side_effects=True),
    )(*arrs)


def _all_reduce_rows(vec, *, name):
    rows = vec.shape[0]

    def body(v_ref, o_ref, land, send_sems, recv_sems):
        x, y, c = _mesh_pos()
        me = 4 * x + 2 * y + c
        land[me] = v_ref[...]
        cps = []
        for m in range(1, 8):
            mx, my, mc = (m >> 2) & 1, (m >> 1) & 1, m & 1
            tx, ty, tc = (x + mx) % 2, (y + my) % 2, (c + mc) % 2
            cps.append(pltpu.make_async_remote_copy(v_ref, land.at[me], send_sems.at[m - 1], recv_sems.at[me],
                                                    device_id=(tx, ty, tc), device_id_type=MESH))
        for cp in cps:
            cp.start()
        for m in range(1, 8):
            mx, my, mc = (m >> 2) & 1, (m >> 1) & 1, m & 1
            src = 4 * ((x + mx) % 2) + 2 * ((y + my) % 2) + (c + mc) % 2
            pltpu.make_async_remote_copy(v_ref, land.at[src], send_sems.at[m - 1], recv_sems.at[src],
                                         device_id=(x, y, c), device_id_type=MESH).wait_recv()
        for cp in cps:
            cp.wait_send()
        acc = land[0]
        for d in range(1, 8):
            acc = acc + land[d]
        o_ref[...] = acc

    vm = pl.BlockSpec(memory_space=pltpu.VMEM)
    return pl.pallas_call(
        body, name=name,
        out_shape=jax.ShapeDtypeStruct(vec.shape, F32),
        in_specs=[vm], out_specs=vm,
        scratch_shapes=[pltpu.VMEM((8, rows, LANES), F32), pltpu.SemaphoreType.DMA((7,)), pltpu.SemaphoreType.DMA((8,))],
        compiler_params=pltpu.CompilerParams(has_side_effects=True, vmem_limit_bytes=VMEM_LIMIT),
    )(vec)


def _adamw(w, g, m, v):
    m = ADAM_B1 * m + (1.0 - ADAM_B1) * g
    v = ADAM_B2 * v + (1.0 - ADAM_B2) * (g * g)
    m_hat = m / (1.0 - ADAM_B1 ** ADAM_STEP)
    v_hat = v / (1.0 - ADAM_B2 ** ADAM_STEP)
    delta = -ADAM_LR * (m_hat / (jnp.sqrt(v_hat) + ADAM_EPS) + ADAM_WD * w)
    return delta, m, v


def _sum4(mine, land, *, name):
    rows, cols = mine.shape
    tr = _pick_rows(rows)

    def body(a_ref, l_ref, o_ref):
        o_ref[...] = (a_ref[...].astype(F32) + l_ref[0].astype(F32)) + (l_ref[1].astype(F32) + l_ref[2].astype(F32))

    return pl.pallas_call(
        body, name=name, out_shape=jax.ShapeDtypeStruct((rows, cols), F32), grid=(rows // tr,),
        in_specs=[pl.BlockSpec((tr, cols), lambda i: (i, 0)), pl.BlockSpec((3, tr, cols), lambda i: (0, i, 0))],
        out_specs=pl.BlockSpec((tr, cols), lambda i: (i, 0)),
        compiler_params=_cparams(("parallel",)),
    )(mine, land)


def _pick_rows(rows, want=256):
    for t in range(min(want, rows) // 8 * 8, 0, -8):
        if rows % t == 0:
            return t
    return rows


def _sum_adam(h_mine, h_sib, w, m, v, *, name):
    rows, cols = w.shape
    tr = _pick_rows(rows)

    def body(a_ref, b_ref, w_ref, m_ref, v_ref, g_o, d_o, m_o, v_o):
        g = a_ref[...] + b_ref[...]
        d, mn, vn = _adamw(w_ref[...], g, m_ref[...], v_ref[...])
        g_o[...], d_o[...], m_o[...], v_o[...] = g, d, mn, vn

    spec = pl.BlockSpec((tr, cols), lambda i: (i, 0))
    return pl.pallas_call(
        body, name=name, out_shape=[jax.ShapeDtypeStruct((rows, cols), F32)] * 4, grid=(rows // tr,),
        in_specs=[spec] * 5, out_specs=[spec] * 4, compiler_params=_cparams(("parallel",)),
    )(h_mine, h_sib, w, m, v)


def _adam_rows(w, g, m, v, *, name):
    def body(w_ref, g_ref, m_ref, v_ref, d_o, m_o, v_o):
        d_o[...], m_o[...], v_o[...] = _adamw(w_ref[...], g_ref[...], m_ref[...], v_ref[...])

    vm = pl.BlockSpec(memory_space=pltpu.VMEM)
    return pl.pallas_call(
        body, name=name, out_shape=[jax.ShapeDtypeStruct(w.shape, F32)] * 3,
        in_specs=[vm] * 4, out_specs=[vm] * 3, compiler_params=_cparams(),
    )(w, g, m, v)


def _pack_rows(arrs):
    flat = jnp.concatenate([a.reshape(-1).astype(F32) for a in arrs])
    pad = -flat.shape[0] % (8 * LANES)
    return jnp.concatenate([flat, jnp.zeros((pad,), F32)]).reshape(-1, LANES)


def _unpack_rows(packed, shapes):
    flat = packed.reshape(-1)
    out, off = [], 0
    for s in shapes:
        size = 1
        for d in s:
            size *= d
        out.append(flat[off:off + size].reshape(s))
        off += size
    return out


WEIGHTS = ['ffn1_w_in', 'ffn1_w_out', 'w_in', 'mu_prev', 'mu_next', 'w0', 'w2', 'a0', 'a2', 'g2', 'k_k', 'k_a', 'r_k',
           'lnx_g', 'lnx_b', 'conv_dw', 'conv_b', 'conv_ln_g', 'conv_ln_b', 'w_out', 'ffn2_w_in', 'ffn2_w_out',
           'ln1_g', 'ln1_b', 'ln2_g', 'ln2_b', 'ln3_g', 'ln3_b']
COL_SHARDED = ('ffn1_w_in', 'w_in', 'ffn2_w_in')
ROW_SHARDED = ('ffn1_w_out', 'w_out', 'ffn2_w_out')
BIG = COL_SHARDED + ROW_SHARDED
SMALL_SHARDED = ('w0', 'w2', 'a0', 'a2', 'g2', 'conv_dw')
REPLICATED = tuple(n for n in WEIGHTS if n not in BIG + SMALL_SHARDED)


def _train_step(x, target, w, m, v, *, tt, tb):
    xi, yi, _ = _mesh_pos()
    q = 2 * xi + yi

    early, mid, late = ("ffn1_w_in", "ffn1_w_out"), ("w_in",) + SMALL_SHARDED, ("w_out", "ffn2_w_in", "ffn2_w_out")
    shard = lambda n: w[n][0].astype(BF16) if n in BIG else w[n][0]

    def whole(n, slabs):
        if n in ROW_SHARDED:
            return slabs.reshape((-1,) + slabs.shape[2:])
        if n in ("ffn1_w_in", "ffn2_w_in"):
            return slabs
        return jnp.moveaxis(slabs, 0, -2).reshape(slabs.shape[1:-1] + (N_CHIPS * slabs.shape[-1],))

    full = {n: w[n][0] for n in REPLICATED}
    first = _gather_chips([shard(n) for n in early], name="gather_ffn1")
    full.update({n: whole(n, g) for n, g in zip(early, first)})
    mid_started = _exchange_start([shard(n) for n in mid], scatter=False, after=first[0], name="gather_mix_start")
    late_started = _exchange_start([shard(n) for n in late], scatter=False, after=mid_started[-1], name="gather_out_start")

    def more_weights(stage, after):
        names, started = (mid, mid_started) if stage == "mix" else (late, late_started)
        own, land = _exchange_wait(started, scatter=False, after=after, name="gather_%s_wait" % stage)
        got = {n: whole(n, _by_chip(o, l)) for n, o, l in zip(names, own, land)}
        full.update(got)
        return got

    sent = []

    def grads_ready(names, slabs):
        started = _exchange_start(slabs, scatter=True, after=slabs[0], name="scatter_%s_start" % names[0])
        sent.append((names, started))
        return started[-1]

    loss_part, grad_x, gr = _local_step(x, target, full, tt=tt, tb=tb, start_token=late_started[-1],
                                        more_weights=more_weights, grads_ready=grads_ready)

    halves = {}
    for names, started in sent:
        stacks, landed = _exchange_wait(started, scatter=True, after=grad_x, name="scatter_%s_wait" % names[0])
        for n, s, l in zip(names, stacks, landed):
            halves[n] = _sum4(lax.dynamic_index_in_dim(s, q, 0, keepdims=False), l, name="sum4_" + n)
    halves = [halves[n] for n in BIG]
    sib = _swap_sibling(halves, name="swap_halves")
    grad, delta, new_m, new_v = {}, {}, {}, {}
    for n, h, hs in zip(BIG, halves, sib):
        outs = _sum_adam(h, hs, w[n][0], m[n][0], v[n][0], name="adam_" + n)
        grad[n], delta[n], new_m[n], new_v[n] = [o[None] for o in outs]

    small_names = REPLICATED + SMALL_SHARDED
    small_full_shapes = [full[n].shape for n in small_names]
    red = _all_reduce_rows(_pack_rows([gr[n] for n in small_names]), name="reduce_small")
    red = dict(zip(small_names, _unpack_rows(red, small_full_shapes)))
    gsm = {}
    for n in REPLICATED:
        gsm[n] = red[n].reshape(w[n].shape)
    for n in SMALL_SHARDED:
        width = w[n].shape[-1]
        gsm[n] = lax.dynamic_slice_in_dim(red[n], q * width, width, axis=red[n].ndim - 1).reshape(w[n].shape)
    shapes = [w[n].shape for n in small_names]
    d_p, m_p, v_p = _adam_rows(_pack_rows([w[n] for n in small_names]), _pack_rows([gsm[n] for n in small_names]),
                               _pack_rows([m[n] for n in small_names]), _pack_rows([v[n] for n in small_names]),
                               name="adam_small")
    for n, dd, mm, vv in zip(small_names, _unpack_rows(d_p, shapes), _unpack_rows(m_p, shapes), _unpack_rows(v_p, shapes)):
        grad[n], delta[n], new_m[n], new_v[n] = gsm[n], dd, mm, vv
    return loss_part, grad_x, grad, delta, new_m, new_v


def kernel(x, ffn1_w_in, ffn1_w_out, w_in, mu_prev, mu_next, w0, w2, a0, a2, g2, k_k, k_a, r_k, lnx_g, lnx_b, conv_dw, conv_b, conv_ln_g, conv_ln_b, w_out, ffn2_w_in, ffn2_w_out, ln1_g, ln1_b, ln2_g, ln2_b, ln3_g, ln3_b, loss_target, m_ffn1_w_in, m_ffn1_w_out, m_w_in, m_mu_prev, m_mu_next, m_w0, m_w2, m_a0, m_a2, m_g2, m_k_k, m_k_a, m_r_k, m_lnx_g, m_lnx_b, m_conv_dw, m_conv_b, m_conv_ln_g, m_conv_ln_b, m_w_out, m_ffn2_w_in, m_ffn2_w_out, m_ln1_g, m_ln1_b, m_ln2_g, m_ln2_b, m_ln3_g, m_ln3_b, v_ffn1_w_in, v_ffn1_w_out, v_w_in, v_mu_prev, v_mu_next, v_w0, v_w2, v_a0, v_a2, v_g2, v_k_k, v_k_a, v_r_k, v_lnx_g, v_lnx_b, v_conv_dw, v_conv_b, v_conv_ln_g, v_conv_ln_b, v_w_out, v_ffn2_w_in, v_ffn2_w_out, v_ln1_g, v_ln1_b, v_ln2_g, v_ln2_b, v_ln3_g, v_ln3_b):
    args = dict(locals())
    w = {n: args[n] for n in WEIGHTS}
    m = {n: args["m_" + n] for n in WEIGHTS}
    v = {n: args["v_" + n] for n in WEIGHTS}
    seq = x.shape[1]
    loss_part, grad_x, grad, delta, new_m, new_v = _train_step(x, loss_target, w, m, v, tt=min(256, seq), tb=8)
    loss = lax.psum(loss_part[0, 0], ("x", "y", "c"))
    return (loss, grad_x, *[grad[n] for n in WEIGHTS], *[delta[n] for n in WEIGHTS],
            *[new_m[n] for n in WEIGHTS], *[new_v[n] for n in WEIGHTS])
```

```python
import functools

import jax
import jax.numpy as jnp
from jax import lax
from jax.experimental import pallas as pl
from jax.experimental.pallas import tpu as pltpu

F32 = jnp.float32
BF16 = jnp.bfloat16

D_MODEL = 1024
RW = 512
HEAD = 64
CW = 512
CONV_K = 31
CONV_PAD = 15
D_FF = 2816
LORA = 64
GATE_LORA = 160
GATE_PAD = 256
SHIFT_COLS = 1952
SHIFT_PAD = 2048
IN_COLS = 2976
IN_PAD = 3072
LN_EPS = 1e-5
GN_EPS = 64e-5
NORM_EPS = 1e-12
ALPHA = 2.0 ** 0.25
DECAY_SCALE = 0.6065306597126334
ADAM_LR, ADAM_B1, ADAM_B2, ADAM_EPS, ADAM_WD, ADAM_STEP = 0.001, 0.9, 0.999, 1e-08, 0.01, 10
N_CHIPS = 4
VMEM_LIMIT = 56 * 1024 * 1024
TM_FFN = 256
TM_LN = 512
TB_SCAN = 16

MESH = pl.DeviceIdType.MESH


def _cparams(sem=None, **kw):
    return pltpu.CompilerParams(dimension_semantics=sem, vmem_limit_bytes=VMEM_LIMIT, **kw)


LANES = 128


def _pick_tile(dim, want):
    for t in range(min(want, dim) // LANES * LANES, 0, -LANES):
        if dim % t == 0:
            return t
    return dim


def _after_operand(after):
    return ([], []) if after is None else ([pl.BlockSpec(memory_space=pl.ANY)], [after])


def _matmul(a, b, *, ta=False, tb=False, out_dtype=F32, tm=1024, tn=1024, tk=1024, scale=1.0, col_slabs=False,
            after=None, name):
    after_specs, after_args = _after_operand(after)
    if ta:
        k_dim, m_dim = a.shape
    else:
        m_dim, k_dim = a.shape
    n_dim = b.shape[0] if tb else b.shape[1]
    tm, tn, tk = _pick_tile(m_dim, tm), _pick_tile(n_dim, tn), _pick_tile(k_dim, tk)
    assert m_dim % tm == 0 and n_dim % tn == 0 and k_dim % tk == 0, (name, a.shape, b.shape, tm, tn, tk)
    nk = k_dim // tk
    dims = (((0,) if ta else (1,), (1,) if tb else (0,)), ((), ()))
    if col_slabs:
        out_shape = jax.ShapeDtypeStruct((n_dim // tn, m_dim, tn), out_dtype)
        out_spec = pl.BlockSpec((None, tm, tn), lambda i, j, k: (j, i, 0))
    else:
        out_shape = jax.ShapeDtypeStruct((m_dim, n_dim), out_dtype)
        out_spec = pl.BlockSpec((tm, tn), lambda i, j, k: (i, j))

    def body(a_ref, b_ref, *rest):
        o_ref, acc_ref = rest[-2:]
        kk = pl.program_id(2)

        @pl.when(kk == 0)
        def _():
            acc_ref[...] = jnp.zeros_like(acc_ref)

        acc_ref[...] += lax.dot_general(a_ref[...].astype(BF16), b_ref[...].astype(BF16), dims,
                                        preferred_element_type=F32)

        @pl.when(kk == nk - 1)
        def _():
            o_ref[...] = (acc_ref[...] * scale).astype(o_ref.dtype)

    a_spec = pl.BlockSpec((tk, tm), lambda i, j, k: (k, i)) if ta else pl.BlockSpec((tm, tk), lambda i, j, k: (i, k))
    b_spec = pl.BlockSpec((tn, tk), lambda i, j, k: (j, k)) if tb else pl.BlockSpec((tk, tn), lambda i, j, k: (k, j))
    return pl.pallas_call(
        body, name=name,
        out_shape=out_shape,
        grid=(m_dim // tm, n_dim // tn, nk),
        in_specs=[a_spec, b_spec] + after_specs,
        out_specs=out_spec,
        scratch_shapes=[pltpu.VMEM((tm, tn), F32)],
        compiler_params=_cparams(("parallel", "parallel", "arbitrary")),
    )(a, b, *after_args)


def _whole(shape):
    nd = len(shape)
    return pl.BlockSpec(shape, lambda i: (0,) * nd)


def _ffn_in(x, w, *, tm, after=None, name):
    n_tok = x.shape[0]
    sw = w.shape[2]
    tm = min(tm, n_tok)

    after_specs, after_args = _after_operand(after)

    def body(x_ref, w_ref, *rest):
        h_ref, a_ref = rest[-2:]
        xb = x_ref[...].astype(BF16)
        for s in range(2):
            g = jnp.dot(xb, w_ref[s], preferred_element_type=F32)
            u = jnp.dot(xb, w_ref[s + 2], preferred_element_type=F32)
            h_ref[:, s * sw:(s + 1) * sw] = g.astype(BF16)
            h_ref[:, (s + 2) * sw:(s + 3) * sw] = u.astype(BF16)
            a_ref[:, s * sw:(s + 1) * sw] = (_silu(g) * u).astype(BF16)

    return pl.pallas_call(
        body, name=name,
        out_shape=[jax.ShapeDtypeStruct((n_tok, 2 * D_FF), BF16), jax.ShapeDtypeStruct((n_tok, D_FF), BF16)],
        grid=(n_tok // tm,),
        in_specs=[pl.BlockSpec((tm, D_MODEL), lambda i: (i, 0)), _whole(w.shape)] + after_specs,
        out_specs=[pl.BlockSpec((tm, 2 * D_FF), lambda i: (i, 0)), pl.BlockSpec((tm, D_FF), lambda i: (i, 0))],
        compiler_params=_cparams(("parallel",)),
    )(x, w, *after_args)


def _mm_ln(a_list, w, xres, g, b, fscale, *, tm, name):
    n_tok = xres.shape[0]
    tm = min(tm, n_tok)
    na = len(a_list)

    def body(*refs):
        a_refs = refs[:na]
        w_ref, x_ref, g_ref, b_ref, z_o, y_o, yb_o = refs[na:]
        f, off = None, 0
        for a_ref in a_refs:
            k = a_ref.shape[1]
            t = jnp.dot(a_ref[...].astype(BF16), w_ref[off:off + k, :], preferred_element_type=F32)
            f = t if f is None else f + t
            off += k
        z = ALPHA * x_ref[...] + fscale * f
        y = _layer_norm(z, g_ref[...], b_ref[...])
        z_o[...] = z
        y_o[...] = y
        yb_o[...] = y.astype(BF16)

    tile = pl.BlockSpec((tm, D_MODEL), lambda i: (i, 0))
    return pl.pallas_call(
        body, name=name,
        out_shape=[jax.ShapeDtypeStruct((n_tok, D_MODEL), F32)] * 2 + [jax.ShapeDtypeStruct((n_tok, D_MODEL), BF16)],
        grid=(n_tok // tm,),
        in_specs=[pl.BlockSpec((tm, a.shape[1]), lambda i: (i, 0)) for a in a_list]
        + [_whole(w.shape), tile, _whole(g.shape), _whole(b.shape)],
        out_specs=[tile, tile, tile],
        compiler_params=_cparams(("parallel",)),
    )(*a_list, w, xres, g, b)


def _ffn_out_bwd(dz, w, h, *, tm, name):
    n_tok = dz.shape[0]
    tm = min(tm, n_tok)
    cw = D_FF // 2

    def body(dz_ref, w_ref, h_ref, dh_ref):
        dzb = dz_ref[...].astype(BF16)
        for s in range(2):
            dact = 0.5 * lax.dot_general(dzb, w_ref[s * cw:(s + 1) * cw, :], (((1,), (1,)), ((), ())),
                                         preferred_element_type=F32)
            gate = h_ref[:, s * cw:(s + 1) * cw].astype(F32)
            up = h_ref[:, D_FF + s * cw:D_FF + (s + 1) * cw].astype(F32)
            sg = _sigmoid(gate)
            dh_ref[:, s * cw:(s + 1) * cw] = (dact * up * sg * (1.0 + gate * (1.0 - sg))).astype(BF16)
            dh_ref[:, D_FF + s * cw:D_FF + (s + 1) * cw] = (dact * gate * sg).astype(BF16)

    wide = pl.BlockSpec((tm, 2 * D_FF), lambda i: (i, 0))
    return pl.pallas_call(
        body, name=name,
        out_shape=jax.ShapeDtypeStruct((n_tok, 2 * D_FF), BF16),
        grid=(n_tok // tm,),
        in_specs=[pl.BlockSpec((tm, D_MODEL), lambda i: (i, 0)), _whole(w.shape), wide],
        out_specs=wide,
        compiler_params=_cparams(("parallel",)),
    )(dz, w, h)


def _mm_nt_res(a_list, w, dz, *, tm, after=None, name):
    n_tok = dz.shape[0]
    tm = min(tm, n_tok)
    na = len(a_list)
    nt = (((1,), (1,)), ((), ()))
    after_specs, after_args = _after_operand(after)

    def body(*refs):
        a_refs = refs[:na]
        w_ref, dz_ref, o_ref = refs[na], refs[na + 1], refs[-1]
        acc = ALPHA * dz_ref[...]
        if len(w_ref.shape) == 3:
            cw = w_ref.shape[2]
            for s in range(w_ref.shape[0]):
                acc = acc + lax.dot_general(a_refs[0][:, s * cw:(s + 1) * cw], w_ref[s], nt, preferred_element_type=F32)
        else:
            off = 0
            for a_ref in a_refs:
                k = a_ref.shape[1]
                acc = acc + lax.dot_general(a_ref[...], w_ref[:, off:off + k], nt, preferred_element_type=F32)
                off += k
        o_ref[...] = acc

    tile = pl.BlockSpec((tm, D_MODEL), lambda i: (i, 0))
    return pl.pallas_call(
        body, name=name,
        out_shape=jax.ShapeDtypeStruct((n_tok, D_MODEL), F32),
        grid=(n_tok // tm,),
        in_specs=[pl.BlockSpec((tm, a.shape[1]), lambda i: (i, 0)) for a in a_list] + [_whole(w.shape), tile]
        + after_specs,
        out_specs=tile,
        compiler_params=_cparams(("parallel",)),
    )(*a_list, w, dz, *after_args)


def _rowcall(fn, tok_in, full_in, tok_out, acc_out, *, tt, name):
    views = [a if isinstance(a, tuple) else (a, a.shape[1], 0) for a in tok_in]
    tok_in = [a for a, _, _ in views]
    n_tok = tok_in[0].shape[0]
    assert n_tok % tt == 0, (name, n_tok, tt)
    n_ti, n_fi, n_to = len(tok_in), len(full_in), len(tok_out)

    def body(*refs):
        i = pl.program_id(0)
        ins = [r[...] for r in refs[:n_ti + n_fi]]
        outs = fn(i, *ins)
        o_refs = refs[n_ti + n_fi:]
        for r, val in zip(o_refs[:n_to], outs[:n_to]):
            r[...] = val.astype(r.dtype)
        if acc_out:
            @pl.when(i == 0)
            def _():
                for r in o_refs[n_to:]:
                    r[...] = jnp.zeros_like(r)
            for r, val in zip(o_refs[n_to:], outs[n_to:]):
                r[...] += val.reshape(r.shape).astype(F32)

    in_specs = [pl.BlockSpec((tt, width), functools.partial(lambda k, i: (i, k), k)) for _, width, k in views]
    in_specs += [pl.BlockSpec(a.shape, lambda i: (0, 0)) for a in full_in]
    out_specs = [pl.BlockSpec((tt, c), lambda i: (i, 0)) for c, _ in tok_out]
    out_specs += [pl.BlockSpec(s, lambda i: (0, 0)) for s in acc_out]
    out_shape = [jax.ShapeDtypeStruct((n_tok, c), dt) for c, dt in tok_out]
    out_shape += [jax.ShapeDtypeStruct(s, F32) for s in acc_out]
    return pl.pallas_call(
        body, name=name, out_shape=out_shape, grid=(n_tok // tt,), in_specs=in_specs, out_specs=out_specs,
        compiler_params=_cparams(("arbitrary",) if acc_out else ("parallel",)),
    )(*tok_in, *full_in)


@jax.custom_vjp
def _bdot(a, b):
    return jnp.dot(a.astype(BF16), b.astype(BF16), preferred_element_type=F32)


def _bdot_fwd(a, b):
    return _bdot(a, b), (a, b)


def _bdot_bwd(res, g):
    a, b = res
    g16 = g.astype(BF16)
    da = lax.dot_general(g16, b.astype(BF16), (((1,), (1,)), ((), ())), preferred_element_type=F32)
    db = lax.dot_general(a.astype(BF16), g16, (((0,), (0,)), ((), ())), preferred_element_type=F32)
    return da, db


_bdot.defvjp(_bdot_fwd, _bdot_bwd)


def _split16(x):
    hi = x.astype(BF16)
    lo = (x - hi.astype(F32)).astype(BF16)
    return hi, lo


def _segsum_raw(x, e2):
    hi, lo = _split16(x)
    outs = []
    for c in range(x.shape[1] // 256):
        lhs = jnp.concatenate([hi[:, 256 * c:256 * (c + 1)], lo[:, 256 * c:256 * (c + 1)]], axis=1)
        outs.append(jnp.dot(lhs, e2, preferred_element_type=F32))
    return jnp.concatenate(outs, axis=1)


@jax.custom_vjp
def _segsum(x, e2):
    return _segsum_raw(x, e2)


def _segsum_fwd(x, e2):
    return _segsum_raw(x, e2), e2


def _segsum_bwd(e2, g):
    return _segsum_raw(g, e2), jnp.zeros_like(e2)


_segsum.defvjp(_segsum_fwd, _segsum_bwd)


def _head_ones():
    r = lax.broadcasted_iota(jnp.int32, (512, 256), 0) % 256
    c = lax.broadcasted_iota(jnp.int32, (512, 256), 1)
    return (r // HEAD == c // HEAD).astype(BF16)


def _sigmoid(x):
    return 1.0 / (1.0 + jnp.exp(-x))


def _silu(x):
    return x * _sigmoid(x)


def _layer_norm(z, g, b, eps=LN_EPS):
    mu = jnp.mean(z, axis=-1, keepdims=True)
    zc = z - mu
    var = jnp.mean(zc * zc, axis=-1, keepdims=True)
    return zc * lax.rsqrt(var + eps) * g + b


def _prep(ps, w2b, w0c, a2b, a0c, g2p, k_k, k_a, e2):
    r, k, v = ps[:, 0:512], ps[:, 512:1024], ps[:, 1024:1536]
    wd, ad, gd = ps[:, 1536:1664], ps[:, 1664:1792], ps[:, 1792:2048]
    lw = _bdot(jnp.tanh(wd), w2b) + w0c
    decay = jnp.exp(-DECAY_SCALE * _sigmoid(lw))
    a = _sigmoid(_bdot(ad, a2b) + a0c)
    g = _bdot(_sigmoid(gd), g2p)
    kkr = k * k_k
    nrm = jnp.sqrt(_segsum(kkr * kkr, e2))
    kk = kkr / jnp.maximum(nrm, NORM_EPS)
    k2 = jnp.concatenate([k, k], axis=1)
    ka2 = jnp.concatenate([k_a, k_a], axis=1)
    kd = k2 * (1.0 + (a - 1.0) * ka2)
    b = jnp.concatenate([kk, kk], axis=1) * a
    return r, v, kk, decay, kd, b, g


def _post(y0, y1, r, v, kd, g, lnx_g, lnx_b, r_k, e2):
    y = y0 + y1
    mu = _segsum(y, e2) * (1.0 / HEAD)
    yc = y - mu
    var = _segsum(yc * yc, e2) * (1.0 / HEAD)
    yn = yc * lax.rsqrt(var + GN_EPS) * lnx_g + lnx_b
    bonus = _segsum(r * (kd[:, :RW] + kd[:, RW:]) * r_k, e2)
    return (yn + bonus * v) * g


def _conv_post(yc, ln_g, ln_b):
    return _silu(_layer_norm(yc, ln_g, ln_b))


def _ln_bwd(z, dy, g, b, *, tt, name):
    def fn(i, zv, dv, gv, bv):
        _, vjp = jax.vjp(_layer_norm, zv, gv, bv)
        return vjp(dv)
    return _rowcall(fn, [z, dy], [g, b], [(D_MODEL, F32)], [(1, D_MODEL), (1, D_MODEL)], tt=tt, name=name)


def _loss_fwd_bwd(y, target, *, tt, name):
    def fn(i, yv, tv):
        e = yv - tv
        part = 0.5 * jnp.sum(jnp.mean(e * e, axis=-1, keepdims=True), axis=0, keepdims=True)
        return e * (1.0 / D_MODEL), jnp.broadcast_to(part, (8, 128))
    return _rowcall(fn, [y, target], [], [(D_MODEL, F32)], [(8, 128)], tt=tt, name=name)


def _halo_specs(cols_block, hb, tt, n_tok, col_idx):
    nb = n_tok // hb
    prev = pl.BlockSpec((hb, cols_block), lambda i: (jnp.maximum(i * (tt // hb) - 1, 0), col_idx))
    nxt = pl.BlockSpec((hb, cols_block), lambda i: (jnp.minimum((i + 1) * (tt // hb), nb - 1), col_idx))
    return prev, nxt


def _mix_prep(p, mu_p, mu_n, w2b, w0c, a2b, a0c, g2p, k_k, k_a, *, seq, tt, name):
    n_tok = p.shape[0]
    tps = seq // tt
    e2 = _head_ones()

    def body(p_ref, hp_ref, hn_ref, mup_ref, mun_ref, w2b_ref, w0c_ref, a2b_ref, a0c_ref, g2p_ref, kk_ref, ka_ref,
             e2_ref, r_o, v_o, kk_o, w_o, kd_o, b_o, g_o, ext):
        i = pl.program_id(0)
        first = (i % tps) == 0
        last = (i % tps) == tps - 1
        pv = p_ref[...]
        ext[pl.ds(0, 8), :] = jnp.where(first, 0.0, hp_ref[...])
        ext[pl.ds(8, tt), :] = pv
        ext[pl.ds(8 + tt, 8), :] = jnp.where(last, 0.0, hn_ref[...])
        prev = ext[pl.ds(7, tt), :]
        nxt = ext[pl.ds(9, tt), :]
        ps = pv + mup_ref[...] * (prev - pv) + mun_ref[...] * (nxt - pv)
        outs = _prep(ps, w2b_ref[...], w0c_ref[...], a2b_ref[...], a0c_ref[...], g2p_ref[...], kk_ref[...],
                     ka_ref[...], e2_ref[...])
        for o_ref, val in zip((r_o, v_o, kk_o, w_o, kd_o, b_o, g_o), outs):
            o_ref[...] = val

    hp, hn = _halo_specs(SHIFT_PAD, 8, tt, n_tok, 0)
    fulls = [mu_p, mu_n, w2b, w0c, a2b, a0c, g2p, k_k, k_a, e2]
    widths = (RW, RW, RW, 2 * RW, 2 * RW, 2 * RW, RW)
    return pl.pallas_call(
        body, name=name,
        out_shape=[jax.ShapeDtypeStruct((n_tok, c), F32) for c in widths],
        grid=(n_tok // tt,),
        in_specs=[pl.BlockSpec((tt, SHIFT_PAD), lambda i: (i, 0)), hp, hn]
        + [pl.BlockSpec(a.shape, lambda i: (0, 0)) for a in fulls],
        out_specs=[pl.BlockSpec((tt, c), lambda i: (i, 0)) for c in widths],
        scratch_shapes=[pltpu.VMEM((tt + 16, SHIFT_PAD), F32)],
        compiler_params=_cparams(("parallel",)),
    )(p, p, p, *fulls)


def _mix_prep_bwd(p, mu_p, mu_n, w2b, w0c, a2b, a0c, g2p, k_k, k_a, ct_terms, *, seq, tt, name):
    n_tok = p.shape[0]
    tps = seq // tt
    e2 = _head_ones()
    acc_shapes = [w2b.shape, w0c.shape, a2b.shape, a0c.shape, g2p.shape, k_k.shape, k_a.shape]
    cts = [a for terms in ct_terms for t in terms for a in (t if isinstance(t, tuple) else (t,))]

    def body(p_ref, hp_ref, hn_ref, mup_ref, mun_ref, w2b_ref, w0c_ref, a2b_ref, a0c_ref, g2p_ref, kk_ref, ka_ref,
             e2_ref, *rest):
        ct_refs, dps_o, acc_refs, ext = rest[:len(cts)], rest[len(cts)], rest[len(cts) + 1:-1], rest[-1]
        ct_it = iter(ct_refs)
        ct_vals = []
        for terms in ct_terms:
            total = None
            for t in terms:
                if isinstance(t, tuple):
                    val = jnp.concatenate([next(ct_it)[...] for _ in t], axis=1)
                else:
                    val = next(ct_it)[...]
                total = val if total is None else total + val
            ct_vals.append(total)
        i = pl.program_id(0)
        first = (i % tps) == 0
        last = (i % tps) == tps - 1
        pv = p_ref[...]
        ext[pl.ds(0, 8), :] = jnp.where(first, 0.0, hp_ref[...])
        ext[pl.ds(8, tt), :] = pv
        ext[pl.ds(8 + tt, 8), :] = jnp.where(last, 0.0, hn_ref[...])
        prev = ext[pl.ds(7, tt), :]
        nxt = ext[pl.ds(9, tt), :]
        ps = pv + mup_ref[...] * (prev - pv) + mun_ref[...] * (nxt - pv)
        e2v = e2_ref[...]
        _, vjp = jax.vjp(lambda *a: _prep(*a, e2v), ps, w2b_ref[...], w0c_ref[...], a2b_ref[...], a0c_ref[...],
                         g2p_ref[...], kk_ref[...], ka_ref[...])
        grads = vjp(tuple(ct_vals))
        dps_o[...] = grads[0]

        @pl.when(i == 0)
        def _():
            for r in acc_refs:
                r[...] = jnp.zeros_like(r)
        for r, val in zip(acc_refs, grads[1:]):
            r[...] += val

    hp, hn = _halo_specs(SHIFT_PAD, 8, tt, n_tok, 0)
    fulls = [mu_p, mu_n, w2b, w0c, a2b, a0c, g2p, k_k, k_a, e2]
    return pl.pallas_call(
        body, name=name,
        out_shape=[jax.ShapeDtypeStruct((n_tok, SHIFT_PAD), F32)] + [jax.ShapeDtypeStruct(s, F32) for s in acc_shapes],
        grid=(n_tok // tt,),
        in_specs=[pl.BlockSpec((tt, SHIFT_PAD), lambda i: (i, 0)), hp, hn]
        + [pl.BlockSpec(a.shape, lambda i: (0, 0)) for a in fulls]
        + [pl.BlockSpec((tt, c.shape[1]), lambda i: (i, 0)) for c in cts],
        out_specs=[pl.BlockSpec((tt, SHIFT_PAD), lambda i: (i, 0))] + [pl.BlockSpec(s, lambda i: (0, 0)) for s in acc_shapes],
        scratch_shapes=[pltpu.VMEM((tt + 16, SHIFT_PAD), F32)],
        compiler_params=_cparams(("arbitrary",)),
    )(p, p, p, *fulls, *cts)


def _shift_bwd(dps, p, mu_p, mu_n, *, seq, tt, name):
    n_tok = p.shape[0]
    tps = seq // tt

    def body(d_ref, dhp_ref, dhn_ref, p_ref, php_ref, phn_ref, mup_ref, mun_ref, dp_o, dmup_o, dmun_o, ext):
        i = pl.program_id(0)
        first = (i % tps) == 0
        last = (i % tps) == tps - 1
        mup, mun = mup_ref[...], mun_ref[...]
        dv = d_ref[...]
        pv = p_ref[...]
        ext[pl.ds(0, 8), :] = jnp.where(first, 0.0, dhp_ref[...])
        ext[pl.ds(8, tt), :] = dv
        ext[pl.ds(8 + tt, 8), :] = jnp.where(last, 0.0, dhn_ref[...])
        d_prev = ext[pl.ds(7, tt), :]
        d_next = ext[pl.ds(9, tt), :]
        dp_o[...] = (dv * (1.0 - mup - mun) + d_next * mup + d_prev * mun).astype(dp_o.dtype)
        ext[pl.ds(0, 8), :] = jnp.where(first, 0.0, php_ref[...])
        ext[pl.ds(8, tt), :] = pv
        ext[pl.ds(8 + tt, 8), :] = jnp.where(last, 0.0, phn_ref[...])
        p_prev = ext[pl.ds(7, tt), :]
        p_next = ext[pl.ds(9, tt), :]

        @pl.when(i == 0)
        def _():
            dmup_o[...] = jnp.zeros_like(dmup_o)
            dmun_o[...] = jnp.zeros_like(dmun_o)
        dmup_o[...] += jnp.sum(dv * (p_prev - pv), axis=0, keepdims=True)
        dmun_o[...] += jnp.sum(dv * (p_next - pv), axis=0, keepdims=True)

    hp, hn = _halo_specs(SHIFT_PAD, 8, tt, n_tok, 0)
    tile = pl.BlockSpec((tt, SHIFT_PAD), lambda i: (i, 0))
    full = pl.BlockSpec((1, SHIFT_PAD), lambda i: (0, 0))
    return pl.pallas_call(
        body, name=name,
        out_shape=[jax.ShapeDtypeStruct((n_tok, SHIFT_PAD), BF16), jax.ShapeDtypeStruct((1, SHIFT_PAD), F32),
                   jax.ShapeDtypeStruct((1, SHIFT_PAD), F32)],
        grid=(n_tok // tt,),
        in_specs=[tile, hp, hn, tile, hp, hn, full, full],
        out_specs=[tile, full, full],
        scratch_shapes=[pltpu.VMEM((tt + 16, SHIFT_PAD), F32)],
        compiler_params=_cparams(("arbitrary",)),
    )(dps, dps, dps, p, p, p, mu_p, mu_n)


def _mix_post(y0, y1, r, v, kd, g, lnx_g, lnx_b, r_k, *, tt, name):
    e2 = _head_ones()
    return _rowcall(lambda i, *a: (_post(*a),), [y0, y1, r, v, kd, g], [lnx_g, lnx_b, r_k, e2], [(RW, BF16)], [],
                    tt=tt, name=name)[0]


def _mix_post_bwd(y0, y1, r, v, kd, g, lnx_g, lnx_b, r_k, dout, *, tt, name):
    e2 = _head_ones()

    def fn(i, y0v, y1v, rv, vv, kdv, gv, dov, lg, lb, rk, e2v):
        _, vjp = jax.vjp(lambda *a: _post(*a, e2v), y0v, y1v, rv, vv, kdv, gv, lg, lb, rk)
        gr = vjp(dov.astype(F32))
        return gr[0], gr[2], gr[3], gr[4], gr[5], gr[6], gr[7], gr[8]
    return _rowcall(fn, [y0, y1, r, v, kd, g, dout], [lnx_g, lnx_b, r_k, e2],
                    [(RW, F32), (RW, F32), (RW, F32), (2 * RW, F32), (RW, F32)], [(1, RW), (1, RW), (1, RW)],
                    tt=tt, name=name)


def _conv_fwd(p, dw, db, ln_g, ln_b, *, seq, tt, name):
    n_tok = p.shape[0]
    tps = seq // tt

    def glu(x, gate):
        return x * _sigmoid(gate)

    def body(u_ref, g_ref, uhp, ghp, uhn, ghn, dw_ref, db_ref, lg_ref, lb_ref, yc_o, y_o, ext):
        i = pl.program_id(0)
        first = (i % tps) == 0
        last = (i % tps) == tps - 1
        ext[pl.ds(0, 16), :] = jnp.where(first, 0.0, glu(uhp[...], ghp[...]))
        ext[pl.ds(16, tt), :] = glu(u_ref[...], g_ref[...])
        ext[pl.ds(16 + tt, 16), :] = jnp.where(last, 0.0, glu(uhn[...], ghn[...]))
        acc = jnp.zeros((tt, CW), F32) + db_ref[...]
        for k in range(CONV_K):
            acc = acc + ext[pl.ds(k + 1, tt), :] * dw_ref[pl.ds(k, 1), :]
        yc_o[...] = acc
        y_o[...] = _conv_post(acc, lg_ref[...], lb_ref[...]).astype(y_o.dtype)

    uhp_s, uhn_s = _halo_specs(CW, 16, tt, n_tok, 4)
    ghp_s, ghn_s = _halo_specs(CW, 16, tt, n_tok, 5)
    fulls = [dw, db, ln_g, ln_b]
    return pl.pallas_call(
        body, name=name,
        out_shape=[jax.ShapeDtypeStruct((n_tok, CW), F32), jax.ShapeDtypeStruct((n_tok, CW), BF16)],
        grid=(n_tok // tt,),
        in_specs=[pl.BlockSpec((tt, CW), lambda i: (i, 4)), pl.BlockSpec((tt, CW), lambda i: (i, 5)),
                  uhp_s, ghp_s, uhn_s, ghn_s] + [pl.BlockSpec(a.shape, lambda i: (0, 0)) for a in fulls],
        out_specs=[pl.BlockSpec((tt, CW), lambda i: (i, 0)), pl.BlockSpec((tt, CW), lambda i: (i, 0))],
        scratch_shapes=[pltpu.VMEM((tt + 32, CW), F32)],
        compiler_params=_cparams(("parallel",)),
    )(p, p, p, p, p, p, *fulls)


def _conv_post_bwd(yc, dy, ln_g, ln_b, *, tt, name):
    def fn(i, ycv, dyv, lg, lb):
        _, vjp = jax.vjp(_conv_post, ycv, lg, lb)
        dyc, dg, dbb = vjp(dyv.astype(F32))
        return dyc, dg, dbb, jnp.sum(dyc, axis=0, keepdims=True)
    return _rowcall(fn, [yc, dy], [ln_g, ln_b], [(CW, F32)], [(1, CW), (1, CW), (1, CW)], tt=tt, name=name)


def _conv_bwd(dyc, p, dw, *, seq, tt, name):
    n_tok = p.shape[0]
    tps = seq // tt

    def body(d_ref, dhp, dhn, u_ref, g_ref, uhp, ghp, uhn, ghn, dw_ref, dp_o, ddw_o, ext):
        i = pl.program_id(0)
        first = (i % tps) == 0
        last = (i % tps) == tps - 1
        dv = d_ref[...]
        ext[pl.ds(0, 16), :] = jnp.where(first, 0.0, dhp[...])
        ext[pl.ds(16, tt), :] = dv
        ext[pl.ds(16 + tt, 16), :] = jnp.where(last, 0.0, dhn[...])
        du = jnp.zeros((tt, CW), F32)
        for k in range(CONV_K):
            du = du + ext[pl.ds(31 - k, tt), :] * dw_ref[pl.ds(k, 1), :]
        uv, gv = u_ref[...], g_ref[...]
        sg = _sigmoid(gv)
        dp_o[:, 0:CW] = (du * sg).astype(dp_o.dtype)
        dp_o[:, CW:2 * CW] = (du * uv * sg * (1.0 - sg)).astype(dp_o.dtype)
        ext[pl.ds(0, 16), :] = jnp.where(first, 0.0, uhp[...] * _sigmoid(ghp[...]))
        ext[pl.ds(16, tt), :] = uv * sg
        ext[pl.ds(16 + tt, 16), :] = jnp.where(last, 0.0, uhn[...] * _sigmoid(ghn[...]))

        @pl.when(i == 0)
        def _():
            ddw_o[...] = jnp.zeros_like(ddw_o)
        for k in range(CONV_K):
            ddw_o[pl.ds(k, 1), :] += jnp.sum(dv * ext[pl.ds(k + 1, tt), :], axis=0, keepdims=True)

    dhp_s, dhn_s = _halo_specs(CW, 16, tt, n_tok, 0)
    uhp_s, uhn_s = _halo_specs(CW, 16, tt, n_tok, 4)
    ghp_s, ghn_s = _halo_specs(CW, 16, tt, n_tok, 5)
    return pl.pallas_call(
        body, name=name,
        out_shape=[jax.ShapeDtypeStruct((n_tok, 2 * CW), BF16), jax.ShapeDtypeStruct((32, CW), F32)],
        grid=(n_tok // tt,),
        in_specs=[pl.BlockSpec((tt, CW), lambda i: (i, 0)), dhp_s, dhn_s,
                  pl.BlockSpec((tt, CW), lambda i: (i, 4)), pl.BlockSpec((tt, CW), lambda i: (i, 5)),
                  uhp_s, ghp_s, uhn_s, ghn_s, pl.BlockSpec(dw.shape, lambda i: (0, 0))],
        out_specs=[pl.BlockSpec((tt, 2 * CW), lambda i: (i, 0)), pl.BlockSpec((32, CW), lambda i: (0, 0))],
        scratch_shapes=[pltpu.VMEM((tt + 32, CW), F32)],
        compiler_params=_cparams(("arbitrary",)),
    )(dyc, dyc, dyc, p, p, p, p, p, p, dw)


def _segdot(hi, lo, e2):
    outs = []
    for c in range(hi.shape[1] // 256):
        lhs = jnp.concatenate([hi[:, 256 * c:256 * (c + 1)], lo[:, 256 * c:256 * (c + 1)]], axis=1)
        outs.append(jnp.dot(lhs, e2, preferred_element_type=F32))
    return jnp.concatenate(outs, axis=1)


SCAN_PASSES = 1


def _seg_streams(parts, e2):
    if SCAN_PASSES == 1:
        hi = jnp.concatenate([p.astype(BF16) for p in parts], axis=0)
        full = jnp.concatenate([jnp.dot(hi[:, 256 * c:256 * (c + 1)], e2[:256], preferred_element_type=F32)
                                for c in range(RW // 256)], axis=1)
    else:
        pieces = [_split16(p) for p in parts]
        full = _segdot(jnp.concatenate([h for h, _ in pieces], axis=0), jnp.concatenate([l for _, l in pieces], axis=0), e2)
    return [full[s * HEAD:(s + 1) * HEAD] for s in range(len(parts))]


def _diag_mask():
    return lax.broadcasted_iota(jnp.int32, (HEAD, RW), 0) == lax.broadcasted_iota(jnp.int32, (HEAD, RW), 1) % HEAD


def _col_form(rows, dmask, e2):
    his, los = [], []
    for x in rows:
        hi = x.astype(BF16).astype(F32)
        lo = x - hi
        his.append(jnp.where(dmask, jnp.broadcast_to(hi, (HEAD, RW)), 0.0).astype(BF16))
        los.append(jnp.where(dmask, jnp.broadcast_to(lo, (HEAD, RW)), 0.0).astype(BF16))
    full = _segdot(jnp.concatenate(his, axis=0), jnp.concatenate(los, axis=0), e2)
    return [full[s * HEAD:(s + 1) * HEAD] for s in range(len(rows))]


def _row_form(col, dmask):
    return jnp.sum(jnp.where(dmask, col, 0.0), axis=0, keepdims=True)


def _row_sum(x):
    return jnp.sum(x, axis=0, keepdims=True)


def _wkv_fwd(r, v, kk, w, kd, b, *, tb, name):
    bsz, seq, _ = r.shape
    nb = seq // tb
    ns = 2 * bsz
    e2 = _head_ones()

    def body(r0, r1, v0, v1, k0, k1, w0, w1, kd0, kd1, b0, b1, e2_ref, y0_o, y1_o, sp_o, last_o, s_ref):
        i = pl.program_id(0)

        @pl.when(i == 0)
        def _():
            s_ref[...] = jnp.zeros_like(s_ref)

        e2v = e2_ref[...]
        dmask = _diag_mask()
        y_refs = (y0_o, y1_o)

        def step(j, carry):
            tl = (j, tb - 1 - j)

            def rows(refs):
                return [refs[d][bb, pl.ds(tl[d], 1), :] for d in (0, 1) for bb in range(bsz)]

            kk_r, w_r, b_r, kd_r, r_r = rows((k0, k1)), rows((w0, w1)), rows((b0, b1)), rows((kd0, kd1)), rows((r0, r1))
            s_old = [s_ref[s * HEAD:(s + 1) * HEAD, :] for s in range(ns)]
            for s in range(ns):
                sp_o[s, pl.ds(j, 1), :, :] = s_old[s].reshape(1, HEAD, RW)
            sa = _seg_streams([s_old[s] * kk_r[s] for s in range(ns)], e2v)
            vc = _col_form(rows((v0, v1)), dmask, e2v)
            s_new = [s_old[s] * w_r[s] - sa[s] * b_r[s] + vc[s] * kd_r[s] for s in range(ns)]
            for s in range(ns):
                s_ref[s * HEAD:(s + 1) * HEAD, :] = s_new[s]
            ycol = _seg_streams([s_new[s] * r_r[s] for s in range(ns)], e2v)
            for d in (0, 1):
                for bb in range(bsz):
                    y_refs[d][bb, pl.ds(tl[d], 1), :] = _row_form(ycol[d * bsz + bb], dmask)
            return carry

        lax.fori_loop(0, tb, step, 0)
        last_o[...] = s_ref[...]

    def blk(width_idx, rev):
        if rev:
            return pl.BlockSpec((bsz, tb, RW), lambda i: (0, nb - 1 - i, width_idx))
        return pl.BlockSpec((bsz, tb, RW), lambda i: (0, i, width_idx))

    in_specs = [blk(0, False), blk(0, True)] * 3 + [blk(0, False), blk(1, True)] * 3
    in_specs.append(pl.BlockSpec(e2.shape, lambda i: (0, 0)))
    return pl.pallas_call(
        body, name=name,
        out_shape=[jax.ShapeDtypeStruct((bsz, seq, RW), F32), jax.ShapeDtypeStruct((bsz, seq, RW), F32),
                   jax.ShapeDtypeStruct((ns, seq, HEAD, RW), F32), jax.ShapeDtypeStruct((ns * HEAD, RW), F32)],
        grid=(nb,),
        in_specs=in_specs,
        out_specs=[blk(0, False), blk(0, True), pl.BlockSpec((ns, tb, HEAD, RW), lambda i: (0, i, 0, 0)),
                   pl.BlockSpec((ns * HEAD, RW), lambda i: (0, 0))],
        scratch_shapes=[pltpu.VMEM((ns * HEAD, RW), F32)],
        compiler_params=_cparams(("arbitrary",)),
    )(r, r, v, v, kk, kk, w, w, kd, kd, b, b, e2)


def _wkv_bwd(r, v, kk, w, kd, b, dy, sp, s_last, *, tb, name):
    bsz, seq, _ = r.shape
    nb = seq // tb
    ns = 2 * bsz
    e2 = _head_ones()

    def body(r0, r1, v0, v1, k0, k1, dy0, dy1, w0, w1, kd0, kd1, b0, b1, sp_ref, last_ref, e2_ref, *rest):
        outs, g_ref, post_ref = rest[:-2], rest[-2], rest[-1]
        i = pl.program_id(0)

        @pl.when(i == 0)
        def _():
            g_ref[...] = jnp.zeros_like(g_ref)
            post_ref[...] = last_ref[...]

        e2v = e2_ref[...]
        dmask = _diag_mask()

        def step(jj, carry):
            sl = tb - 1 - jj
            tl = (sl, jj)

            def rows(refs):
                return [refs[d][bb, pl.ds(tl[d], 1), :] for d in (0, 1) for bb in range(bsz)]

            kk_r, w_r, b_r, kd_r, r_r = rows((k0, k1)), rows((w0, w1)), rows((b0, b1)), rows((kd0, kd1)), rows((r0, r1))
            s_old = [sp_ref[s, pl.ds(sl, 1), :, :].reshape(HEAD, RW) for s in range(ns)]
            sa = _seg_streams([s_old[s] * kk_r[s] for s in range(ns)], e2v)
            vc = _col_form(rows((v0, v1)), dmask, e2v)
            dyc = _col_form(rows((dy0, dy1)), dmask, e2v)
            gt = [g_ref[s * HEAD:(s + 1) * HEAD, :] + dyc[s] * r_r[s] for s in range(ns)]
            both = _seg_streams([gt[s] * b_r[s] for s in range(ns)] + [gt[s] * kd_r[s] for s in range(ns)], e2v)
            gb, dvc = both[:ns], both[ns:]
            for d in (0, 1):
                for bb in range(bsz):
                    s = d * bsz + bb
                    at = (bb, pl.ds(tl[d], 1), slice(None))
                    outs[0 + d][at] = _row_sum(post_ref[s * HEAD:(s + 1) * HEAD, :] * dyc[s])
                    post_ref[s * HEAD:(s + 1) * HEAD, :] = s_old[s]
                    outs[2 + d][at] = _row_form(dvc[s], dmask)
                    outs[4 + d][at] = -_row_sum(s_old[s] * gb[s])
                    outs[6 + d][at] = _row_sum(s_old[s] * gt[s])
                    outs[8 + d][at] = _row_sum(gt[s] * vc[s])
                    outs[10 + d][at] = -_row_sum(sa[s] * gt[s])
                    g_ref[s * HEAD:(s + 1) * HEAD, :] = gt[s] * w_r[s] - gb[s] * kk_r[s]
            return carry

        lax.fori_loop(0, tb, step, 0)

    def blk(width_idx, rev):
        if rev:
            return pl.BlockSpec((bsz, tb, RW), lambda i: (0, nb - 1 - i, width_idx))
        return pl.BlockSpec((bsz, tb, RW), lambda i: (0, i, width_idx))

    in_specs = [blk(0, True), blk(0, False)] * 4 + [blk(0, True), blk(1, False)] * 3
    in_specs.append(pl.BlockSpec((ns, tb, HEAD, RW), lambda i: (0, nb - 1 - i, 0, 0)))
    in_specs.append(pl.BlockSpec((ns * HEAD, RW), lambda i: (0, 0)))
    in_specs.append(pl.BlockSpec(e2.shape, lambda i: (0, 0)))
    return pl.pallas_call(
        body, name=name,
        out_shape=[jax.ShapeDtypeStruct((bsz, seq, RW), F32)] * 12,
        grid=(nb,),
        in_specs=in_specs,
        out_specs=[blk(0, True), blk(0, False)] * 6,
        scratch_shapes=[pltpu.VMEM((ns * HEAD, RW), F32), pltpu.VMEM((ns * HEAD, RW), F32)],
        compiler_params=_cparams(("arbitrary",)),
    )(r, r, v, v, kk, kk, dy, dy, w, w, kd, kd, b, b, sp, s_last, e2)


def _block_diag2(w):
    z = jnp.zeros_like(w[0])
    return jnp.concatenate([jnp.concatenate([w[0], z], axis=1), jnp.concatenate([z, w[1]], axis=1)], axis=0)


def _pad_in_cols(a):
    z = jnp.zeros(a.shape[:-1] + (SHIFT_PAD - SHIFT_COLS,), a.dtype)
    return jnp.concatenate([a[..., :SHIFT_COLS], z, a[..., SHIFT_COLS:]], axis=-1)


def _unpad_in_cols(a):
    return jnp.concatenate([a[..., :SHIFT_COLS], a[..., SHIFT_PAD:]], axis=-1)


def _follow(small, token):
    return small if token is None else small + token[0:1, 0:1]


def _local_step(x, target, wts, *, tt, tb, start_token=None, more_weights=None, grads_ready=None):
    bsz, seq, _ = x.shape
    n_tok = bsz * seq
    row = lambda a: a.reshape(1, -1).astype(F32)
    x0 = x.reshape(n_tok, D_MODEL)
    tgt = target.reshape(n_tok, D_MODEL)
    ln = {k: row(wts[k]) for k in ("ln1_g", "ln1_b", "ln2_g", "ln2_b", "ln3_g", "ln3_b")}
    if grads_ready is None:
        grads_ready = lambda names, slabs: None

    w1i, w1o = wts["ffn1_w_in"], wts["ffn1_w_out"]
    h1, act1 = _ffn_in(x0, w1i, tm=TM_FFN, after=start_token, name="ffn1_in")
    z1, x1, x1b = _mm_ln([act1], w1o, x0, ln["ln1_g"], ln["ln1_b"], 0.5, tm=TM_LN, name="ffn1_out_ln1")
    if more_weights is not None:
        wts = {**wts, **more_weights("mix", x1b)}
    win = _pad_in_cols(wts["w_in"])
    zpad = jnp.zeros((1, SHIFT_PAD - SHIFT_COLS), F32)
    mu_p = jnp.concatenate([row(wts["mu_prev"]), zpad], axis=1)
    mu_n = jnp.concatenate([row(wts["mu_next"]), zpad], axis=1)
    w2b, a2b = _block_diag2(wts["w2"]), _block_diag2(wts["a2"])
    w0c, a0c = row(wts["w0"]), row(wts["a0"])
    g2p = jnp.concatenate([wts["g2"], jnp.zeros((GATE_PAD - GATE_LORA, RW), F32)], axis=0)
    k_k, k_a, r_k = row(wts["k_k"]), row(wts["k_a"]), row(wts["r_k"])
    lnx_g, lnx_b = row(wts["lnx_g"]), row(wts["lnx_b"])
    cdw, cb, clg, clb = wts["conv_dw"], row(wts["conv_b"]), row(wts["conv_ln_g"]), row(wts["conv_ln_b"])
    small = (mu_p, mu_n, w2b, w0c, a2b, a0c, g2p, k_k, k_a)
    seq3 = lambda a: a.reshape(bsz, seq, a.shape[-1])
    flat = lambda a: a.reshape(n_tok, a.shape[-1])

    p = _matmul(x1b, win, name="proj_in")
    r, v, kk, w, kd, b, g = _mix_prep(p, *small, seq=seq, tt=tt, name="mix_prep")
    y0, y1, sp, s_last = _wkv_fwd(seq3(r), seq3(v), seq3(kk), seq3(w), seq3(kd), seq3(b), tb=tb, name="wkv_fwd")
    y0, y1 = flat(y0), flat(y1)
    yr = _mix_post(y0, y1, r, v, kd, g, lnx_g, lnx_b, r_k, tt=tt, name="mix_post")
    yc, yv = _conv_fwd(p, cdw, cb, clg, clb, seq=seq, tt=tt, name="conv_fwd")
    if more_weights is not None:
        wts = {**wts, **more_weights("out", yr)}
    wout, w2i, w2o = wts["w_out"], wts["ffn2_w_in"], wts["ffn2_w_out"]
    z2, x2, x2b = _mm_ln([yr, yv], wout, x1, ln["ln2_g"], ln["ln2_b"], 1.0, tm=TM_LN, name="proj_out_ln2")
    h2, act2 = _ffn_in(x2b, w2i, tm=TM_FFN, name="ffn2_in")
    z3, x3, _ = _mm_ln([act2], w2o, x2, ln["ln3_g"], ln["ln3_b"], 0.5, tm=TM_LN, name="ffn2_out_ln3")
    dx3, loss_part = _loss_fwd_bwd(x3, tgt, tt=tt, name="loss")

    gr = {}
    slab_rows = lambda a: a.reshape((N_CHIPS, a.shape[0] // N_CHIPS) + a.shape[1:])
    dw_kw = dict(ta=True, out_dtype=BF16)
    dz3, gr["ln3_g"], gr["ln3_b"] = _ln_bwd(z3, dx3, ln["ln3_g"], ln["ln3_b"], tt=tt, name="ln3_bwd")
    dh2 = _ffn_out_bwd(dz3, w2o, h2, tm=TM_FFN, name="ffn2_out_dx")
    gr["ffn2_w_out"] = slab_rows(_matmul(act2, dz3, scale=0.5, tm=D_FF // 2, name="ffn2_out_dw", **dw_kw))
    dx2 = _mm_nt_res([dh2], w2i, dz3, tm=TM_FFN, name="ffn2_in_dx")
    gr["ffn2_w_in"] = _matmul(x2b, dh2, col_slabs=True, tn=2 * D_FF // N_CHIPS, name="ffn2_in_dw", **dw_kw)
    dz2, gr["ln2_g"], gr["ln2_b"] = _ln_bwd(z2, dx2, ln["ln2_g"], ln["ln2_b"], tt=tt, name="ln2_bwd")
    dmix = _matmul(dz2, wout, tb=True, name="proj_out_dx")
    gr["w_out"] = slab_rows(jnp.concatenate([_matmul(yr, dz2, name="proj_out_dw_rwkv", **dw_kw),
                                             _matmul(yv, dz2, name="proj_out_dw_conv", **dw_kw)], axis=0))
    tok = grads_ready(("ffn2_w_out", "ffn2_w_in", "w_out"), [gr["ffn2_w_out"], gr["ffn2_w_in"], gr["w_out"]])
    dyr, dyv = (dmix, RW, 0), (dmix, RW, 1)
    dy, dr_p, dv_p, dkd_p, dg, gr["lnx_g"], gr["lnx_b"], gr["r_k"] = _mix_post_bwd(
        y0, y1, r, v, kd, g, _follow(lnx_g, tok), lnx_b, r_k, dyr, tt=tt, name="mix_post_bwd")
    scan_g = _wkv_bwd(seq3(r), seq3(v), seq3(kk), seq3(w), seq3(kd), seq3(b), seq3(dy), sp, s_last, tb=tb,
                      name="wkv_bwd")
    dr0, dr1, dv0, dv1, dk0, dk1, dw0, dw1, dkd0, dkd1, db0, db1 = [flat(a) for a in scan_g]
    ct_terms = [[dr_p, dr0, dr1], [dv_p, dv0, dv1], [dk0, dk1], [(dw0, dw1)], [dkd_p, (dkd0, dkd1)], [(db0, db1)], [dg]]
    dyc, gr["conv_ln_g"], gr["conv_ln_b"], gr["conv_b"] = _conv_post_bwd(yc, dyv, clg, clb, tt=tt, name="conv_post_bwd")
    dpc, ddw = _conv_bwd(dyc, p, cdw, seq=seq, tt=tt, name="conv_bwd")
    gr["conv_dw"] = ddw[:CONV_K]
    dps, dw2b, dw0c, da2b, da0c, dg2p, gr["k_k"], gr["k_a"] = _mix_prep_bwd(
        p, *small, ct_terms, seq=seq, tt=tt, name="mix_prep_bwd")
    gr["w2"] = jnp.stack([dw2b[:LORA, :RW], dw2b[LORA:, RW:]])
    gr["a2"] = jnp.stack([da2b[:LORA, :RW], da2b[LORA:, RW:]])
    gr["w0"], gr["a0"], gr["g2"] = dw0c.reshape(2, RW), da0c.reshape(2, RW), dg2p[:GATE_LORA]
    dpsh, dmu_p, dmu_n = _shift_bwd(dps, p, mu_p, mu_n, seq=seq, tt=tt, name="shift_bwd")
    gr["mu_prev"], gr["mu_next"] = dmu_p[:, :SHIFT_COLS], dmu_n[:, :SHIFT_COLS]
    dx1 = _mm_nt_res([dpsh, dpc], win, dz2, tm=TM_FFN, name="proj_in_dx")
    dwin = jnp.concatenate([_matmul(x1b, dpsh, name="proj_in_dw_shift", **dw_kw)[:, :SHIFT_COLS],
                            _matmul(x1b, dpc, name="proj_in_dw_conv", **dw_kw)], axis=1)
    gr["w_in"] = jnp.moveaxis(dwin.reshape(D_MODEL, N_CHIPS, IN_COLS // N_CHIPS), 1, 0)
    tok = grads_ready(("w_in",), [gr["w_in"]])
    dz1, gr["ln1_g"], gr["ln1_b"] = _ln_bwd(z1, dx1, _follow(ln["ln1_g"], tok), ln["ln1_b"], tt=tt, name="ln1_bwd")
    dh1 = _ffn_out_bwd(dz1, w1o, h1, tm=TM_FFN, name="ffn1_out_dx")
    gr["ffn1_w_out"] = slab_rows(_matmul(act1, dz1, scale=0.5, tm=D_FF // 2, name="ffn1_out_dw", **dw_kw))
    tok = grads_ready(("ffn1_w_out",), [gr["ffn1_w_out"]])
    gr["ffn1_w_in"] = _matmul(x0, dh1, col_slabs=True, tn=2 * D_FF // N_CHIPS, after=tok, name="ffn1_in_dw", **dw_kw)
    tok = grads_ready(("ffn1_w_in",), [gr["ffn1_w_in"]])
    dx0 = _mm_nt_res([dh1], w1i, dz1, tm=TM_FFN, after=tok, name="ffn1_in_dx")
    return loss_part, dx0.reshape(bsz, seq, D_MODEL), gr


def _mesh_pos():
    return lax.axis_index("x"), lax.axis_index("y"), lax.axis_index("c")


def _other_chips(x, y):
    return [(1 - x, y), (x, 1 - y), (1 - x, 1 - y)]


def _gather_chips(shards, *, name):
    n = len(shards)

    def body(*refs):
        ins, outs = refs[:n], refs[n:2 * n]
        send_sems, recv_sems, loc_sems = refs[2 * n:]
        x, y, c = _mesh_pos()
        q = 2 * x + y
        peers = _other_chips(x, y)
        local = [pltpu.make_async_copy(ins[a], outs[a].at[q], loc_sems.at[a]) for a in range(n)]
        for cp in local:
            cp.start()
        sends = [[pltpu.make_async_remote_copy(ins[a], outs[a].at[q], send_sems.at[a, k], recv_sems.at[a, k],
                                               device_id=(px, py, c), device_id_type=MESH)
                  for k, (px, py) in enumerate(peers)] for a in range(n)]
        for a in range(n):
            for cp in sends[a]:
                cp.start()
        for a in range(n):
            for k, (px, py) in enumerate(peers):
                pltpu.make_async_remote_copy(ins[a], outs[a].at[2 * px + py], send_sems.at[a, k], recv_sems.at[a, k],
                                             device_id=(px, py, c), device_id_type=MESH).wait_recv()
        for a in range(n):
            for cp in sends[a]:
                cp.wait_send()
            local[a].wait()

    any_spec = pl.BlockSpec(memory_space=pl.ANY)
    return pl.pallas_call(
        body, name=name,
        out_shape=[jax.ShapeDtypeStruct((N_CHIPS,) + s.shape, s.dtype) for s in shards],
        in_specs=[any_spec] * n, out_specs=[any_spec] * n,
        scratch_shapes=[pltpu.SemaphoreType.DMA((n, 3)), pltpu.SemaphoreType.DMA((n, 3)), pltpu.SemaphoreType.DMA((n,))],
        compiler_params=pltpu.CompilerParams(has_side_effects=True),
    )(*shards)


HBM_SPEC = pl.BlockSpec(memory_space=pltpu.HBM)
SEM_SPEC = pl.BlockSpec(memory_space=pltpu.SEMAPHORE)
ANY_SPEC = pl.BlockSpec(memory_space=pl.ANY)
SIDE_EFFECT = pltpu.SideEffectType.DATAFLOW_SIDE_EFFECTING


def _chip_copies(src_refs, land_refs, send_sems, recv_sems, scatter, arriving=False):
    x, y, c = _mesh_pos()
    cps = []
    for a, (src, land) in enumerate(zip(src_refs, land_refs)):
        for k, (px, py) in enumerate(_other_chips(x, y)):
            slot = k if scatter else (2 * px + py if arriving else 2 * x + y)
            cps.append(pltpu.make_async_remote_copy(src.at[2 * px + py] if scatter else src, land.at[slot],
                                                    send_sems.at[3 * a + k], recv_sems.at[3 * a + k],
                                                    device_id=(px, py, c), device_id_type=MESH))
    return cps


def _exchange_start(srcs, *, scatter, after, name):
    n = len(srcs)
    lands = [lax.empty((3,) + s.shape[1:] if scatter else (N_CHIPS,) + s.shape, s.dtype) for s in srcs]

    def body(*refs):
        src_refs, land_refs = refs[:n], refs[n:2 * n]
        send_sems, recv_sems = refs[2 * n + 1:2 * n + 3]
        token = refs[-1]
        for cp in _chip_copies(src_refs, land_refs, send_sems, recv_sems, scatter):
            cp.start()
        token[...] = jnp.zeros_like(token)

    hbm = lambda a: pltpu.with_memory_space_constraint(a, pltpu.HBM)
    outs = pl.pallas_call(
        body, name=name,
        out_shape=(pltpu.SemaphoreType.DMA((3 * n,)), pltpu.SemaphoreType.DMA((3 * n,)),
                   *[pltpu.HBM(a.shape, a.dtype) for a in srcs + lands], jax.ShapeDtypeStruct((8, LANES), F32)),
        in_specs=[HBM_SPEC] * (2 * n) + [ANY_SPEC],
        out_specs=(SEM_SPEC, SEM_SPEC, *[HBM_SPEC] * (2 * n), pl.BlockSpec(memory_space=pltpu.VMEM)),
        input_output_aliases={i: 2 + i for i in range(2 * n)},
        compiler_params=pltpu.CompilerParams(has_side_effects=SIDE_EFFECT),
    )(*[hbm(a) for a in srcs + lands], after)
    return outs[0], outs[1], list(outs[2:2 + n]), list(outs[2 + n:2 + 2 * n]), outs[-1]


def _exchange_wait(started, *, scatter, after, name):
    send_sems, recv_sems, srcs, lands, _ = started
    n = len(srcs)

    def body(*refs):
        src_refs, land_refs = refs[:n], refs[n:2 * n]
        send_s, recv_s = refs[2 * n:2 * n + 2]
        for cp in _chip_copies(src_refs, land_refs, send_s, recv_s, scatter, arriving=True):
            cp.wait_send()
            cp.wait_recv()

    outs = pl.pallas_call(
        body, name=name,
        out_shape=tuple(pltpu.HBM(a.shape, a.dtype) for a in srcs + lands),
        in_specs=[HBM_SPEC] * (2 * n) + [SEM_SPEC, SEM_SPEC, ANY_SPEC],
        out_specs=tuple([HBM_SPEC] * (2 * n)),
        input_output_aliases={i: i for i in range(2 * n)},
        compiler_params=pltpu.CompilerParams(has_side_effects=SIDE_EFFECT),
    )(*srcs, *lands, send_sems, recv_sems, after)
    return list(outs[:n]), list(outs[n:])


def _by_chip(own, land):
    xi, yi, _ = _mesh_pos()
    return lax.dynamic_update_index_in_dim(land, own, 2 * xi + yi, 0)


def _scatter_chips(stacks, *, name):
    n = len(stacks)

    def body(*refs):
        ins, outs = refs[:n], refs[n:2 * n]
        send_sems, recv_sems = refs[2 * n:]
        x, y, c = _mesh_pos()
        peers = _other_chips(x, y)
        sends = [[pltpu.make_async_remote_copy(ins[a].at[2 * px + py], outs[a].at[k], send_sems.at[a, k],
                                               recv_sems.at[a, k], device_id=(px, py, c), device_id_type=MESH)
                  for k, (px, py) in enumerate(peers)] for a in range(n)]
        for a in range(n):
            for cp in sends[a]:
                cp.start()
        for a in range(n):
            for cp in sends[a]:
                cp.wait_recv()
        for a in range(n):
            for cp in sends[a]:
                cp.wait_send()

    any_spec = pl.BlockSpec(memory_space=pl.ANY)
    return pl.pallas_call(
        body, name=name,
        out_shape=[jax.ShapeDtypeStruct((3,) + s.shape[1:], s.dtype) for s in stacks],
        in_specs=[any_spec] * n, out_specs=[any_spec] * n,
        scratch_shapes=[pltpu.SemaphoreType.DMA((n, 3)), pltpu.SemaphoreType.DMA((n, 3))],
        compiler_params=pltpu.CompilerParams(has_side_effects=True),
    )(*stacks)


def _swap_sibling(arrs, *, name):
    n = len(arrs)

    def body(*refs):
        ins, outs = refs[:n], refs[n:2 * n]
        send_sems, recv_sems = refs[2 * n:]
        x, y, c = _mesh_pos()
        cps = [pltpu.make_async_remote_copy(ins[a], outs[a], send_sems.at[a], recv_sems.at[a],
                                            device_id=(x, y, 1 - c), device_id_type=MESH) for a in range(n)]
        for cp in cps:
            cp.start()
        for cp in cps:
            cp.wait_recv()
        for cp in cps:
            cp.wait_send()

    any_spec = pl.BlockSpec(memory_space=pl.ANY)
    return pl.pallas_call(
        body, name=name,
        out_shape=[jax.ShapeDtypeStruct(s.shape, s.dtype) for s in arrs],
        in_specs=[any_spec] * n, out_specs=[any_spec] * n,
        scratch_shapes=[pltpu.SemaphoreType.DMA((n,)), pltpu.SemaphoreType.DMA((n,))],
        compiler_params=pltpu.CompilerParams(has_side_effects=True),
    )(*arrs)


def _all_reduce_rows(vec, *, name):
    rows = vec.shape[0]

    def body(v_ref, o_ref, land, send_sems, recv_sems):
        x, y, c = _mesh_pos()
        me = 4 * x + 2 * y + c
        land[me] = v_ref[...]
        cps = []
        for m in range(1, 8):
            mx, my, mc = (m >> 2) & 1, (m >> 1) & 1, m & 1
            tx, ty, tc = (x + mx) % 2, (y + my) % 2, (c + mc) % 2
            cps.append(pltpu.make_async_remote_copy(v_ref, land.at[me], send_sems.at[m - 1], recv_sems.at[me],
                                                    device_id=(tx, ty, tc), device_id_type=MESH))
        for cp in cps:
            cp.start()
        for m in range(1, 8):
            mx, my, mc = (m >> 2) & 1, (m >> 1) & 1, m & 1
            src = 4 * ((x + mx) % 2) + 2 * ((y + my) % 2) + (c + mc) % 2
            pltpu.make_async_remote_copy(v_ref, land.at[src], send_sems.at[m - 1], recv_sems.at[src],
                                         device_id=(x, y, c), device_id_type=MESH).wait_recv()
        for cp in cps:
            cp.wait_send()
        acc = land[0]
        for d in range(1, 8):
            acc = acc + land[d]
        o_ref[...] = acc

    vm = pl.BlockSpec(memory_space=pltpu.VMEM)
    return pl.pallas_call(
        body, name=name,
        out_shape=jax.ShapeDtypeStruct(vec.shape, F32),
        in_specs=[vm], out_specs=vm,
        scratch_shapes=[pltpu.VMEM((8, rows, LANES), F32), pltpu.SemaphoreType.DMA((7,)), pltpu.SemaphoreType.DMA((8,))],
        compiler_params=pltpu.CompilerParams(has_side_effects=True, vmem_limit_bytes=VMEM_LIMIT),
    )(vec)


def _adamw(w, g, m, v):
    m = ADAM_B1 * m + (1.0 - ADAM_B1) * g
    v = ADAM_B2 * v + (1.0 - ADAM_B2) * (g * g)
    m_hat = m / (1.0 - ADAM_B1 ** ADAM_STEP)
    v_hat = v / (1.0 - ADAM_B2 ** ADAM_STEP)
    delta = -ADAM_LR * (m_hat / (jnp.sqrt(v_hat) + ADAM_EPS) + ADAM_WD * w)
    return delta, m, v


def _sum4(mine, land, *, name):
    rows, cols = mine.shape
    tr = _pick_rows(rows)

    def body(a_ref, l_ref, o_ref):
        o_ref[...] = (a_ref[...].astype(F32) + l_ref[0].astype(F32)) + (l_ref[1].astype(F32) + l_ref[2].astype(F32))

    return pl.pallas_call(
        body, name=name, out_shape=jax.ShapeDtypeStruct((rows, cols), F32), grid=(rows // tr,),
        in_specs=[pl.BlockSpec((tr, cols), lambda i: (i, 0)), pl.BlockSpec((3, tr, cols), lambda i: (0, i, 0))],
        out_specs=pl.BlockSpec((tr, cols), lambda i: (i, 0)),
        compiler_params=_cparams(("parallel",)),
    )(mine, land)


def _pick_rows(rows, want=256):
    for t in range(min(want, rows) // 8 * 8, 0, -8):
        if rows % t == 0:
            return t
    return rows


def _sum_adam(h_mine, h_sib, w, m, v, *, name):
    rows, cols = w.shape
    tr = _pick_rows(rows)

    def body(a_ref, b_ref, w_ref, m_ref, v_ref, g_o, d_o, m_o, v_o):
        g = a_ref[...] + b_ref[...]
        d, mn, vn = _adamw(w_ref[...], g, m_ref[...], v_ref[...])
        g_o[...], d_o[...], m_o[...], v_o[...] = g, d, mn, vn

    spec = pl.BlockSpec((tr, cols), lambda i: (i, 0))
    return pl.pallas_call(
        body, name=name, out_shape=[jax.ShapeDtypeStruct((rows, cols), F32)] * 4, grid=(rows // tr,),
        in_specs=[spec] * 5, out_specs=[spec] * 4, compiler_params=_cparams(("parallel",)),
    )(h_mine, h_sib, w, m, v)


def _adam_rows(w, g, m, v, *, name):
    def body(w_ref, g_ref, m_ref, v_ref, d_o, m_o, v_o):
        d_o[...], m_o[...], v_o[...] = _adamw(w_ref[...], g_ref[...], m_ref[...], v_ref[...])

    vm = pl.BlockSpec(memory_space=pltpu.VMEM)
    return pl.pallas_call(
        body, name=name, out_shape=[jax.ShapeDtypeStruct(w.shape, F32)] * 3,
        in_specs=[vm] * 4, out_specs=[vm] * 3, compiler_params=_cparams(),
    )(w, g, m, v)


def _pack_rows(arrs):
    flat = jnp.concatenate([a.reshape(-1).astype(F32) for a in arrs])
    pad = -flat.shape[0] % (8 * LANES)
    return jnp.concatenate([flat, jnp.zeros((pad,), F32)]).reshape(-1, LANES)


def _unpack_rows(packed, shapes):
    flat = packed.reshape(-1)
    out, off = [], 0
    for s in shapes:
        size = 1
        for d in s:
            size *= d
        out.append(flat[off:off + size].reshape(s))
        off += size
    return out


WEIGHTS = ['ffn1_w_in', 'ffn1_w_out', 'w_in', 'mu_prev', 'mu_next', 'w0', 'w2', 'a0', 'a2', 'g2', 'k_k', 'k_a', 'r_k',
           'lnx_g', 'lnx_b', 'conv_dw', 'conv_b', 'conv_ln_g', 'conv_ln_b', 'w_out', 'ffn2_w_in', 'ffn2_w_out',
           'ln1_g', 'ln1_b', 'ln2_g', 'ln2_b', 'ln3_g', 'ln3_b']
COL_SHARDED = ('ffn1_w_in', 'w_in', 'ffn2_w_in')
ROW_SHARDED = ('ffn1_w_out', 'w_out', 'ffn2_w_out')
BIG = COL_SHARDED + ROW_SHARDED
SMALL_SHARDED = ('w0', 'w2', 'a0', 'a2', 'g2', 'conv_dw')
REPLICATED = tuple(n for n in WEIGHTS if n not in BIG + SMALL_SHARDED)


def _train_step(x, target, w, m, v, *, tt, tb):
    xi, yi, _ = _mesh_pos()
    q = 2 * xi + yi

    early, mid, late = ("ffn1_w_in", "ffn1_w_out"), ("w_in",) + SMALL_SHARDED, ("w_out", "ffn2_w_in", "ffn2_w_out")
    shard = lambda n: w[n][0].astype(BF16) if n in BIG else w[n][0]

    def whole(n, slabs):
        if n in ROW_SHARDED:
            return slabs.reshape((-1,) + slabs.shape[2:])
        if n in ("ffn1_w_in", "ffn2_w_in"):
            return slabs
        return jnp.moveaxis(slabs, 0, -2).reshape(slabs.shape[1:-1] + (N_CHIPS * slabs.shape[-1],))

    full = {n: w[n][0] for n in REPLICATED}
    first = _gather_chips([shard(n) for n in early], name="gather_ffn1")
    full.update({n: whole(n, g) for n, g in zip(early, first)})
    mid_started = _exchange_start([shard(n) for n in mid], scatter=False, after=first[0], name="gather_mix_start")
    late_started = _exchange_start([shard(n) for n in late], scatter=False, after=mid_started[-1], name="gather_out_start")

    def more_weights(stage, after):
        names, started = (mid, mid_started) if stage == "mix" else (late, late_started)
        own, land = _exchange_wait(started, scatter=False, after=after, name="gather_%s_wait" % stage)
        got = {n: whole(n, _by_chip(o, l)) for n, o, l in zip(names, own, land)}
        full.update(got)
        return got

    sent = []

    def grads_ready(names, slabs):
        started = _exchange_start(slabs, scatter=True, after=slabs[0], name="scatter_%s_start" % names[0])
        sent.append((names, started))
        return started[-1]

    loss_part, grad_x, gr = _local_step(x, target, full, tt=tt, tb=tb, start_token=late_started[-1],
                                        more_weights=more_weights, grads_ready=grads_ready)

    halves = {}
    for names, started in sent:
        stacks, landed = _exchange_wait(started, scatter=True, after=grad_x, name="scatter_%s_wait" % names[0])
        for n, s, l in zip(names, stacks, landed):
            halves[n] = _sum4(lax.dynamic_index_in_dim(s, q, 0, keepdims=False), l, name="sum4_" + n)
    halves = [halves[n] for n in BIG]
    sib = _swap_sibling(halves, name="swap_halves")
    grad, delta, new_m, new_v = {}, {}, {}, {}
    for n, h, hs in zip(BIG, halves, sib):
        outs = _sum_adam(h, hs, w[n][0], m[n][0], v[n][0], name="adam_" + n)
        grad[n], delta[n], new_m[n], new_v[n] = [o[None] for o in outs]

    small_names = REPLICATED + SMALL_SHARDED
    small_full_shapes = [full[n].shape for n in small_names]
    red = _all_reduce_rows(_pack_rows([gr[n] for n in small_names] + [loss_part[0:1, 0:1]]), name="reduce_small")
    *red, loss = _unpack_rows(red, small_full_shapes + [()])
    red = dict(zip(small_names, red))
    gsm = {}
    for n in REPLICATED:
        gsm[n] = red[n].reshape(w[n].shape)
    for n in SMALL_SHARDED:
        width = w[n].shape[-1]
        gsm[n] = lax.dynamic_slice_in_dim(red[n], q * width, width, axis=red[n].ndim - 1).reshape(w[n].shape)
    shapes = [w[n].shape for n in small_names]
    d_p, m_p, v_p = _adam_rows(_pack_rows([w[n] for n in small_names]), _pack_rows([gsm[n] for n in small_names]),
                               _pack_rows([m[n] for n in small_names]), _pack_rows([v[n] for n in small_names]),
                               name="adam_small")
    for n, dd, mm, vv in zip(small_names, _unpack_rows(d_p, shapes), _unpack_rows(m_p, shapes), _unpack_rows(v_p, shapes)):
        grad[n], delta[n], new_m[n], new_v[n] = gsm[n], dd, mm, vv
    return loss, grad_x, grad, delta, new_m, new_v


def kernel(x, ffn1_w_in, ffn1_w_out, w_in, mu_prev, mu_next, w0, w2, a0, a2, g2, k_k, k_a, r_k, lnx_g, lnx_b, conv_dw, conv_b, conv_ln_g, conv_ln_b, w_out, ffn2_w_in, ffn2_w_out, ln1_g, ln1_b, ln2_g, ln2_b, ln3_g, ln3_b, loss_target, m_ffn1_w_in, m_ffn1_w_out, m_w_in, m_mu_prev, m_mu_next, m_w0, m_w2, m_a0, m_a2, m_g2, m_k_k, m_k_a, m_r_k, m_lnx_g, m_lnx_b, m_conv_dw, m_conv_b, m_conv_ln_g, m_conv_ln_b, m_w_out, m_ffn2_w_in, m_ffn2_w_out, m_ln1_g, m_ln1_b, m_ln2_g, m_ln2_b, m_ln3_g, m_ln3_b, v_ffn1_w_in, v_ffn1_w_out, v_w_in, v_mu_prev, v_mu_next, v_w0, v_w2, v_a0, v_a2, v_g2, v_k_k, v_k_a, v_r_k, v_lnx_g, v_lnx_b, v_conv_dw, v_conv_b, v_conv_ln_g, v_conv_ln_b, v_w_out, v_ffn2_w_in, v_ffn2_w_out, v_ln1_g, v_ln1_b, v_ln2_g, v_ln2_b, v_ln3_g, v_ln3_b):
    args = dict(locals())
    w = {n: args[n] for n in WEIGHTS}
    m = {n: args["m_" + n] for n in WEIGHTS}
    v = {n: args["v_" + n] for n in WEIGHTS}
    seq = x.shape[1]
    loss, grad_x, grad, delta, new_m, new_v = _train_step(x, loss_target, w, m, v, tt=min(256, seq), tb=min(TB_SCAN, seq))
    return (loss, grad_x, *[grad[n] for n in WEIGHTS], *[delta[n] for n in WEIGHTS],
            *[new_m[n] for n in WEIGHTS], *[new_v[n] for n in WEIGHTS])
```

```python
import functools

import jax
import jax.numpy as jnp
from jax import lax
from jax.experimental import pallas as pl
from jax.experimental.pallas import tpu as pltpu

F32 = jnp.float32
BF16 = jnp.bfloat16

D_MODEL = 1024
RW = 512
HEAD = 64
CW = 512
CONV_K = 31
CONV_PAD = 15
D_FF = 2816
LORA = 64
GATE_LORA = 160
GATE_PAD = 256
SHIFT_COLS = 1952
SHIFT_PAD = 2048
IN_COLS = 2976
IN_PAD = 3072
LN_EPS = 1e-5
GN_EPS = 64e-5
NORM_EPS = 1e-12
ALPHA = 2.0 ** 0.25
DECAY_SCALE = 0.6065306597126334
ADAM_LR, ADAM_B1, ADAM_B2, ADAM_EPS, ADAM_WD, ADAM_STEP = 0.001, 0.9, 0.999, 1e-08, 0.01, 10
N_CHIPS = 4
VMEM_LIMIT = 56 * 1024 * 1024
TM_FFN = 256
TM_LN = 512
TB_SCAN = 16

MESH = pl.DeviceIdType.MESH


def _cparams(sem=None, **kw):
    return pltpu.CompilerParams(dimension_semantics=sem, vmem_limit_bytes=VMEM_LIMIT, **kw)


LANES = 128


def _pick_tile(dim, want):
    for t in range(min(want, dim) // LANES * LANES, 0, -LANES):
        if dim % t == 0:
            return t
    return dim


def _after_operand(after):
    return ([], []) if after is None else ([pl.BlockSpec(memory_space=pl.ANY)], [after])


def _matmul(a, b, *, ta=False, tb=False, out_dtype=F32, tm=1024, tn=1024, tk=1024, scale=1.0, col_slabs=False,
            after=None, name):
    after_specs, after_args = _after_operand(after)
    if ta:
        k_dim, m_dim = a.shape
    else:
        m_dim, k_dim = a.shape
    n_dim = b.shape[0] if tb else b.shape[1]
    tm, tn, tk = _pick_tile(m_dim, tm), _pick_tile(n_dim, tn), _pick_tile(k_dim, tk)
    assert m_dim % tm == 0 and n_dim % tn == 0 and k_dim % tk == 0, (name, a.shape, b.shape, tm, tn, tk)
    nk = k_dim // tk
    dims = (((0,) if ta else (1,), (1,) if tb else (0,)), ((), ()))
    if col_slabs:
        out_shape = jax.ShapeDtypeStruct((n_dim // tn, m_dim, tn), out_dtype)
        out_spec = pl.BlockSpec((None, tm, tn), lambda i, j, k: (j, i, 0))
    else:
        out_shape = jax.ShapeDtypeStruct((m_dim, n_dim), out_dtype)
        out_spec = pl.BlockSpec((tm, tn), lambda i, j, k: (i, j))

    def body(a_ref, b_ref, *rest):
        o_ref, acc_ref = rest[-2:]
        kk = pl.program_id(2)

        @pl.when(kk == 0)
        def _():
            acc_ref[...] = jnp.zeros_like(acc_ref)

        acc_ref[...] += lax.dot_general(a_ref[...].astype(BF16), b_ref[...].astype(BF16), dims,
                                        preferred_element_type=F32)

        @pl.when(kk == nk - 1)
        def _():
            o_ref[...] = (acc_ref[...] * scale).astype(o_ref.dtype)

    a_spec = pl.BlockSpec((tk, tm), lambda i, j, k: (k, i)) if ta else pl.BlockSpec((tm, tk), lambda i, j, k: (i, k))
    b_spec = pl.BlockSpec((tn, tk), lambda i, j, k: (j, k)) if tb else pl.BlockSpec((tk, tn), lambda i, j, k: (k, j))
    return pl.pallas_call(
        body, name=name,
        out_shape=out_shape,
        grid=(m_dim // tm, n_dim // tn, nk),
        in_specs=[a_spec, b_spec] + after_specs,
        out_specs=out_spec,
        scratch_shapes=[pltpu.VMEM((tm, tn), F32)],
        compiler_params=_cparams(("parallel", "parallel", "arbitrary")),
    )(a, b, *after_args)


def _whole(shape):
    nd = len(shape)
    return pl.BlockSpec(shape, lambda i: (0,) * nd)


def _ffn_in(x, w, *, tm, after=None, name):
    n_tok = x.shape[0]
    sw = w.shape[2]
    tm = min(tm, n_tok)

    after_specs, after_args = _after_operand(after)

    def body(x_ref, w_ref, *rest):
        h_ref, a_ref = rest[-2:]
        xb = x_ref[...].astype(BF16)
        for s in range(2):
            g = jnp.dot(xb, w_ref[s], preferred_element_type=F32)
            u = jnp.dot(xb, w_ref[s + 2], preferred_element_type=F32)
            h_ref[:, s * sw:(s + 1) * sw] = g.astype(BF16)
            h_ref[:, (s + 2) * sw:(s + 3) * sw] = u.astype(BF16)
            a_ref[:, s * sw:(s + 1) * sw] = (_silu(g) * u).astype(BF16)

    return pl.pallas_call(
        body, name=name,
        out_shape=[jax.ShapeDtypeStruct((n_tok, 2 * D_FF), BF16), jax.ShapeDtypeStruct((n_tok, D_FF), BF16)],
        grid=(n_tok // tm,),
        in_specs=[pl.BlockSpec((tm, D_MODEL), lambda i: (i, 0)), _whole(w.shape)] + after_specs,
        out_specs=[pl.BlockSpec((tm, 2 * D_FF), lambda i: (i, 0)), pl.BlockSpec((tm, D_FF), lambda i: (i, 0))],
        compiler_params=_cparams(("parallel",)),
    )(x, w, *after_args)


def _mm_ln(a_list, w, xres, g, b, fscale, *, tm, name):
    n_tok = xres.shape[0]
    tm = min(tm, n_tok)
    na = len(a_list)

    def body(*refs):
        a_refs = refs[:na]
        w_ref, x_ref, g_ref, b_ref, z_o, y_o, yb_o = refs[na:]
        f, off = None, 0
        for a_ref in a_refs:
            k = a_ref.shape[1]
            t = jnp.dot(a_ref[...].astype(BF16), w_ref[off:off + k, :], preferred_element_type=F32)
            f = t if f is None else f + t
            off += k
        z = ALPHA * x_ref[...] + fscale * f
        y = _layer_norm(z, g_ref[...], b_ref[...])
        z_o[...] = z
        y_o[...] = y
        yb_o[...] = y.astype(BF16)

    tile = pl.BlockSpec((tm, D_MODEL), lambda i: (i, 0))
    return pl.pallas_call(
        body, name=name,
        out_shape=[jax.ShapeDtypeStruct((n_tok, D_MODEL), F32)] * 2 + [jax.ShapeDtypeStruct((n_tok, D_MODEL), BF16)],
        grid=(n_tok // tm,),
        in_specs=[pl.BlockSpec((tm, a.shape[1]), lambda i: (i, 0)) for a in a_list]
        + [_whole(w.shape), tile, _whole(g.shape), _whole(b.shape)],
        out_specs=[tile, tile, tile],
        compiler_params=_cparams(("parallel",)),
    )(*a_list, w, xres, g, b)


def _mm_ln_loss(a, w, xres, g, b, target, fscale, *, tm, name):
    n_tok = xres.shape[0]
    tm = min(tm, n_tok)

    def body(a_ref, w_ref, x_ref, g_ref, b_ref, t_ref, dz_o, dg_o, db_o, loss_o):
        i = pl.program_id(0)
        z = ALPHA * x_ref[...] + fscale * jnp.dot(a_ref[...].astype(BF16), w_ref[...], preferred_element_type=F32)
        y, vjp = jax.vjp(_layer_norm, z, g_ref[...], b_ref[...])
        e = y - t_ref[...]
        dz, dg, db = vjp(e * (1.0 / D_MODEL))

        @pl.when(i == 0)
        def _():
            dg_o[...] = jnp.zeros_like(dg_o)
            db_o[...] = jnp.zeros_like(db_o)
            loss_o[...] = jnp.zeros_like(loss_o)
        dz_o[...] = dz
        dg_o[...] += dg
        db_o[...] += db
        loss_o[...] += 0.5 * jnp.sum(jnp.mean(e * e, axis=-1, keepdims=True), axis=0, keepdims=True)

    tile = pl.BlockSpec((tm, D_MODEL), lambda i: (i, 0))
    row = pl.BlockSpec((1, D_MODEL), lambda i: (0, 0))
    return pl.pallas_call(
        body, name=name,
        out_shape=[jax.ShapeDtypeStruct((n_tok, D_MODEL), F32), jax.ShapeDtypeStruct((1, D_MODEL), F32),
                   jax.ShapeDtypeStruct((1, D_MODEL), F32), jax.ShapeDtypeStruct((8, LANES), F32)],
        grid=(n_tok // tm,),
        in_specs=[pl.BlockSpec((tm, a.shape[1]), lambda i: (i, 0)), _whole(w.shape), tile, row, row, tile],
        out_specs=[tile, row, row, pl.BlockSpec((8, LANES), lambda i: (0, 0))],
        compiler_params=_cparams(("arbitrary",)),
    )(a, w, xres, g, b, target)


def _ffn_out_bwd(dz, w, h, *, tm, name):
    n_tok = dz.shape[0]
    tm = min(tm, n_tok)
    cw = D_FF // 2

    def body(dz_ref, w_ref, h_ref, dh_ref):
        dzb = dz_ref[...].astype(BF16)
        for s in range(2):
            dact = 0.5 * lax.dot_general(dzb, w_ref[s * cw:(s + 1) * cw, :], (((1,), (1,)), ((), ())),
                                         preferred_element_type=F32)
            gate = h_ref[:, s * cw:(s + 1) * cw].astype(F32)
            up = h_ref[:, D_FF + s * cw:D_FF + (s + 1) * cw].astype(F32)
            sg = _sigmoid(gate)
            dh_ref[:, s * cw:(s + 1) * cw] = (dact * up * sg * (1.0 + gate * (1.0 - sg))).astype(BF16)
            dh_ref[:, D_FF + s * cw:D_FF + (s + 1) * cw] = (dact * gate * sg).astype(BF16)

    wide = pl.BlockSpec((tm, 2 * D_FF), lambda i: (i, 0))
    return pl.pallas_call(
        body, name=name,
        out_shape=jax.ShapeDtypeStruct((n_tok, 2 * D_FF), BF16),
        grid=(n_tok // tm,),
        in_specs=[pl.BlockSpec((tm, D_MODEL), lambda i: (i, 0)), _whole(w.shape), wide],
        out_specs=wide,
        compiler_params=_cparams(("parallel",)),
    )(dz, w, h)


def _mm_nt_res(a_list, w, dz, *, tm, ln=None, after=None, name):
    n_tok = dz.shape[0]
    tm = min(tm, n_tok)
    na = len(a_list)
    nt = (((1,), (1,)), ((), ()))
    after_specs, after_args = _after_operand(after)
    n_out = 1 if ln is None else 3

    def body(*refs):
        a_refs = refs[:na]
        w_ref, dz_ref, o_ref = refs[na], refs[na + 1], refs[-n_out]
        acc = ALPHA * dz_ref[...]
        if len(w_ref.shape) == 3:
            cw = w_ref.shape[2]
            for s in range(w_ref.shape[0]):
                acc = acc + lax.dot_general(a_refs[0][:, s * cw:(s + 1) * cw], w_ref[s], nt, preferred_element_type=F32)
        else:
            off = 0
            for a_ref in a_refs:
                k = a_ref.shape[1]
                acc = acc + lax.dot_general(a_ref[...], w_ref[:, off:off + k], nt, preferred_element_type=F32)
                off += k
        if ln is None:
            o_ref[...] = acc
            return
        z_ref, g_ref, b_ref = refs[na + 2:na + 5]
        dg_o, db_o = refs[-2:]
        _, vjp = jax.vjp(_layer_norm, z_ref[...], g_ref[...], b_ref[...])
        o_ref[...], dg, db = vjp(acc)

        @pl.when(pl.program_id(0) == 0)
        def _():
            dg_o[...] = jnp.zeros_like(dg_o)
            db_o[...] = jnp.zeros_like(db_o)
        dg_o[...] += dg
        db_o[...] += db

    tile = pl.BlockSpec((tm, D_MODEL), lambda i: (i, 0))
    row = pl.BlockSpec((1, D_MODEL), lambda i: (0, 0))
    out_shape = [jax.ShapeDtypeStruct((n_tok, D_MODEL), F32)]
    ln_specs, ln_args, out_specs = [], [], [tile]
    if ln is not None:
        ln_specs, ln_args = [tile, row, row], list(ln)
        out_shape += [jax.ShapeDtypeStruct((1, D_MODEL), F32)] * 2
        out_specs += [row, row]
    outs = pl.pallas_call(
        body, name=name,
        out_shape=out_shape,
        grid=(n_tok // tm,),
        in_specs=[pl.BlockSpec((tm, a.shape[1]), lambda i: (i, 0)) for a in a_list] + [_whole(w.shape), tile]
        + ln_specs + after_specs,
        out_specs=out_specs,
        compiler_params=_cparams(("parallel",) if ln is None else ("arbitrary",)),
    )(*a_list, w, dz, *ln_args, *after_args)
    return outs[0] if ln is None else outs


def _rowcall(fn, tok_in, full_in, tok_out, acc_out, *, tt, name):
    views = [a if isinstance(a, tuple) else (a, a.shape[1], 0) for a in tok_in]
    tok_in = [a for a, _, _ in views]
    n_tok = tok_in[0].shape[0]
    assert n_tok % tt == 0, (name, n_tok, tt)
    n_ti, n_fi, n_to = len(tok_in), len(full_in), len(tok_out)

    def body(*refs):
        i = pl.program_id(0)
        ins = [r[...] for r in refs[:n_ti + n_fi]]
        outs = fn(i, *ins)
        o_refs = refs[n_ti + n_fi:]
        for r, val in zip(o_refs[:n_to], outs[:n_to]):
            r[...] = val.astype(r.dtype)
        if acc_out:
            @pl.when(i == 0)
            def _():
                for r in o_refs[n_to:]:
                    r[...] = jnp.zeros_like(r)
            for r, val in zip(o_refs[n_to:], outs[n_to:]):
                r[...] += val.reshape(r.shape).astype(F32)

    in_specs = [pl.BlockSpec((tt, width), functools.partial(lambda k, i: (i, k), k)) for _, width, k in views]
    in_specs += [pl.BlockSpec(a.shape, lambda i: (0, 0)) for a in full_in]
    out_specs = [pl.BlockSpec((tt, c), lambda i: (i, 0)) for c, _ in tok_out]
    out_specs += [pl.BlockSpec(s, lambda i: (0, 0)) for s in acc_out]
    out_shape = [jax.ShapeDtypeStruct((n_tok, c), dt) for c, dt in tok_out]
    out_shape += [jax.ShapeDtypeStruct(s, F32) for s in acc_out]
    return pl.pallas_call(
        body, name=name, out_shape=out_shape, grid=(n_tok // tt,), in_specs=in_specs, out_specs=out_specs,
        compiler_params=_cparams(("arbitrary",) if acc_out else ("parallel",)),
    )(*tok_in, *full_in)


@jax.custom_vjp
def _bdot(a, b):
    return jnp.dot(a.astype(BF16), b.astype(BF16), preferred_element_type=F32)


def _bdot_fwd(a, b):
    return _bdot(a, b), (a, b)


def _bdot_bwd(res, g):
    a, b = res
    g16 = g.astype(BF16)
    da = lax.dot_general(g16, b.astype(BF16), (((1,), (1,)), ((), ())), preferred_element_type=F32)
    db = lax.dot_general(a.astype(BF16), g16, (((0,), (0,)), ((), ())), preferred_element_type=F32)
    return da, db


_bdot.defvjp(_bdot_fwd, _bdot_bwd)


def _split16(x):
    hi = x.astype(BF16)
    lo = (x - hi.astype(F32)).astype(BF16)
    return hi, lo


def _segsum_raw(x, e2):
    hi, lo = _split16(x)
    outs = []
    for c in range(x.shape[1] // 256):
        lhs = jnp.concatenate([hi[:, 256 * c:256 * (c + 1)], lo[:, 256 * c:256 * (c + 1)]], axis=1)
        outs.append(jnp.dot(lhs, e2, preferred_element_type=F32))
    return jnp.concatenate(outs, axis=1)


@jax.custom_vjp
def _segsum(x, e2):
    return _segsum_raw(x, e2)


def _segsum_fwd(x, e2):
    return _segsum_raw(x, e2), e2


def _segsum_bwd(e2, g):
    return _segsum_raw(g, e2), jnp.zeros_like(e2)


_segsum.defvjp(_segsum_fwd, _segsum_bwd)


def _head_ones():
    r = lax.broadcasted_iota(jnp.int32, (512, 256), 0) % 256
    c = lax.broadcasted_iota(jnp.int32, (512, 256), 1)
    return (r // HEAD == c // HEAD).astype(BF16)


def _sigmoid(x):
    return 1.0 / (1.0 + jnp.exp(-x))


def _silu(x):
    return x * _sigmoid(x)


def _layer_norm(z, g, b, eps=LN_EPS):
    mu = jnp.mean(z, axis=-1, keepdims=True)
    zc = z - mu
    var = jnp.mean(zc * zc, axis=-1, keepdims=True)
    return zc * lax.rsqrt(var + eps) * g + b


def _prep(ps, w2b, w0c, a2b, a0c, g2p, k_k, k_a, e2):
    r, k, v = ps[:, 0:512], ps[:, 512:1024], ps[:, 1024:1536]
    wd, ad, gd = ps[:, 1536:1664], ps[:, 1664:1792], ps[:, 1792:2048]
    lw = _bdot(jnp.tanh(wd), w2b) + w0c
    decay = jnp.exp(-DECAY_SCALE * _sigmoid(lw))
    a = _sigmoid(_bdot(ad, a2b) + a0c)
    g = _bdot(_sigmoid(gd), g2p)
    kkr = k * k_k
    nrm = jnp.sqrt(_segsum(kkr * kkr, e2))
    kk = kkr / jnp.maximum(nrm, NORM_EPS)
    k2 = jnp.concatenate([k, k], axis=1)
    ka2 = jnp.concatenate([k_a, k_a], axis=1)
    kd = k2 * (1.0 + (a - 1.0) * ka2)
    b = jnp.concatenate([kk, kk], axis=1) * a
    return r, v, kk, decay, kd, b, g


def _post(y0, y1, r, v, kd, g, lnx_g, lnx_b, r_k, e2):
    y = y0 + y1
    mu = _segsum(y, e2) * (1.0 / HEAD)
    yc = y - mu
    var = _segsum(yc * yc, e2) * (1.0 / HEAD)
    yn = yc * lax.rsqrt(var + GN_EPS) * lnx_g + lnx_b
    bonus = _segsum(r * (kd[:, :RW] + kd[:, RW:]) * r_k, e2)
    return (yn + bonus * v) * g


def _conv_post(yc, ln_g, ln_b):
    return _silu(_layer_norm(yc, ln_g, ln_b))


def _halo_specs(cols_block, hb, tt, n_tok, col_idx):
    nb = n_tok // hb
    prev = pl.BlockSpec((hb, cols_block), lambda i: (jnp.maximum(i * (tt // hb) - 1, 0), col_idx))
    nxt = pl.BlockSpec((hb, cols_block), lambda i: (jnp.minimum((i + 1) * (tt // hb), nb - 1), col_idx))
    return prev, nxt


def _mix_prep(p, mu_p, mu_n, w2b, w0c, a2b, a0c, g2p, k_k, k_a, *, seq, tt, name):
    n_tok = p.shape[0]
    tps = seq // tt
    e2 = _head_ones()

    def body(p_ref, hp_ref, hn_ref, mup_ref, mun_ref, w2b_ref, w0c_ref, a2b_ref, a0c_ref, g2p_ref, kk_ref, ka_ref,
             e2_ref, r_o, v_o, kk_o, w_o, kd_o, b_o, g_o, ext):
        i = pl.program_id(0)
        first = (i % tps) == 0
        last = (i % tps) == tps - 1
        pv = p_ref[...]
        ext[pl.ds(0, 8), :] = jnp.where(first, 0.0, hp_ref[...])
        ext[pl.ds(8, tt), :] = pv
        ext[pl.ds(8 + tt, 8), :] = jnp.where(last, 0.0, hn_ref[...])
        prev = ext[pl.ds(7, tt), :]
        nxt = ext[pl.ds(9, tt), :]
        ps = pv + mup_ref[...] * (prev - pv) + mun_ref[...] * (nxt - pv)
        outs = _prep(ps, w2b_ref[...], w0c_ref[...], a2b_ref[...], a0c_ref[...], g2p_ref[...], kk_ref[...],
                     ka_ref[...], e2_ref[...])
        for o_ref, val in zip((r_o, v_o, kk_o, w_o, kd_o, b_o, g_o), outs):
            o_ref[...] = val

    hp, hn = _halo_specs(SHIFT_PAD, 8, tt, n_tok, 0)
    fulls = [mu_p, mu_n, w2b, w0c, a2b, a0c, g2p, k_k, k_a, e2]
    widths = (RW, RW, RW, 2 * RW, 2 * RW, 2 * RW, RW)
    return pl.pallas_call(
        body, name=name,
        out_shape=[jax.ShapeDtypeStruct((n_tok, c), F32) for c in widths],
        grid=(n_tok // tt,),
        in_specs=[pl.BlockSpec((tt, SHIFT_PAD), lambda i: (i, 0)), hp, hn]
        + [pl.BlockSpec(a.shape, lambda i: (0, 0)) for a in fulls],
        out_specs=[pl.BlockSpec((tt, c), lambda i: (i, 0)) for c in widths],
        scratch_shapes=[pltpu.VMEM((tt + 16, SHIFT_PAD), F32)],
        compiler_params=_cparams(("parallel",)),
    )(p, p, p, *fulls)


def _mix_prep_bwd(p, mu_p, mu_n, w2b, w0c, a2b, a0c, g2p, k_k, k_a, ct_terms, *, seq, tt, name):
    n_tok = p.shape[0]
    tps = seq // tt
    e2 = _head_ones()
    acc_shapes = [w2b.shape, w0c.shape, a2b.shape, a0c.shape, g2p.shape, k_k.shape, k_a.shape]
    cts = [a for terms in ct_terms for t in terms for a in (t if isinstance(t, tuple) else (t,))]

    def body(p_ref, hp_ref, hn_ref, mup_ref, mun_ref, w2b_ref, w0c_ref, a2b_ref, a0c_ref, g2p_ref, kk_ref, ka_ref,
             e2_ref, *rest):
        ct_refs, dps_o, acc_refs, ext = rest[:len(cts)], rest[len(cts)], rest[len(cts) + 1:-1], rest[-1]
        ct_it = iter(ct_refs)
        ct_vals = []
        for terms in ct_terms:
            total = None
            for t in terms:
                if isinstance(t, tuple):
                    val = jnp.concatenate([next(ct_it)[...] for _ in t], axis=1)
                else:
                    val = next(ct_it)[...]
                total = val if total is None else total + val
            ct_vals.append(total)
        i = pl.program_id(0)
        first = (i % tps) == 0
        last = (i % tps) == tps - 1
        pv = p_ref[...]
        ext[pl.ds(0, 8), :] = jnp.where(first, 0.0, hp_ref[...])
        ext[pl.ds(8, tt), :] = pv
        ext[pl.ds(8 + tt, 8), :] = jnp.where(last, 0.0, hn_ref[...])
        prev = ext[pl.ds(7, tt), :]
        nxt = ext[pl.ds(9, tt), :]
        ps = pv + mup_ref[...] * (prev - pv) + mun_ref[...] * (nxt - pv)
        e2v = e2_ref[...]
        _, vjp = jax.vjp(lambda *a: _prep(*a, e2v), ps, w2b_ref[...], w0c_ref[...], a2b_ref[...], a0c_ref[...],
                         g2p_ref[...], kk_ref[...], ka_ref[...])
        grads = vjp(tuple(ct_vals))
        dps_o[...] = grads[0]

        @pl.when(i == 0)
        def _():
            for r in acc_refs:
                r[...] = jnp.zeros_like(r)
        for r, val in zip(acc_refs, grads[1:]):
            r[...] += val

    hp, hn = _halo_specs(SHIFT_PAD, 8, tt, n_tok, 0)
    fulls = [mu_p, mu_n, w2b, w0c, a2b, a0c, g2p, k_k, k_a, e2]
    return pl.pallas_call(
        body, name=name,
        out_shape=[jax.ShapeDtypeStruct((n_tok, SHIFT_PAD), F32)] + [jax.ShapeDtypeStruct(s, F32) for s in acc_shapes],
        grid=(n_tok // tt,),
        in_specs=[pl.BlockSpec((tt, SHIFT_PAD), lambda i: (i, 0)), hp, hn]
        + [pl.BlockSpec(a.shape, lambda i: (0, 0)) for a in fulls]
        + [pl.BlockSpec((tt, c.shape[1]), lambda i: (i, 0)) for c in cts],
        out_specs=[pl.BlockSpec((tt, SHIFT_PAD), lambda i: (i, 0))] + [pl.BlockSpec(s, lambda i: (0, 0)) for s in acc_shapes],
        scratch_shapes=[pltpu.VMEM((tt + 16, SHIFT_PAD), F32)],
        compiler_params=_cparams(("arbitrary",)),
    )(p, p, p, *fulls, *cts)


def _shift_bwd(dps, p, mu_p, mu_n, *, seq, tt, name):
    n_tok = p.shape[0]
    tps = seq // tt

    def body(d_ref, dhp_ref, dhn_ref, p_ref, php_ref, phn_ref, mup_ref, mun_ref, dp_o, dmup_o, dmun_o, ext):
        i = pl.program_id(0)
        first = (i % tps) == 0
        last = (i % tps) == tps - 1
        mup, mun = mup_ref[...], mun_ref[...]
        dv = d_ref[...]
        pv = p_ref[...]
        ext[pl.ds(0, 8), :] = jnp.where(first, 0.0, dhp_ref[...])
        ext[pl.ds(8, tt), :] = dv
        ext[pl.ds(8 + tt, 8), :] = jnp.where(last, 0.0, dhn_ref[...])
        d_prev = ext[pl.ds(7, tt), :]
        d_next = ext[pl.ds(9, tt), :]
        dp_o[...] = (dv * (1.0 - mup - mun) + d_next * mup + d_prev * mun).astype(dp_o.dtype)
        ext[pl.ds(0, 8), :] = jnp.where(first, 0.0, php_ref[...])
        ext[pl.ds(8, tt), :] = pv
        ext[pl.ds(8 + tt, 8), :] = jnp.where(last, 0.0, phn_ref[...])
        p_prev = ext[pl.ds(7, tt), :]
        p_next = ext[pl.ds(9, tt), :]

        @pl.when(i == 0)
        def _():
            dmup_o[...] = jnp.zeros_like(dmup_o)
            dmun_o[...] = jnp.zeros_like(dmun_o)
        dmup_o[...] += jnp.sum(dv * (p_prev - pv), axis=0, keepdims=True)
        dmun_o[...] += jnp.sum(dv * (p_next - pv), axis=0, keepdims=True)

    hp, hn = _halo_specs(SHIFT_PAD, 8, tt, n_tok, 0)
    tile = pl.BlockSpec((tt, SHIFT_PAD), lambda i: (i, 0))
    full = pl.BlockSpec((1, SHIFT_PAD), lambda i: (0, 0))
    return pl.pallas_call(
        body, name=name,
        out_shape=[jax.ShapeDtypeStruct((n_tok, SHIFT_PAD), BF16), jax.ShapeDtypeStruct((1, SHIFT_PAD), F32),
                   jax.ShapeDtypeStruct((1, SHIFT_PAD), F32)],
        grid=(n_tok // tt,),
        in_specs=[tile, hp, hn, tile, hp, hn, full, full],
        out_specs=[tile, full, full],
        scratch_shapes=[pltpu.VMEM((tt + 16, SHIFT_PAD), F32)],
        compiler_params=_cparams(("arbitrary",)),
    )(dps, dps, dps, p, p, p, mu_p, mu_n)


def _mix_post(y0, y1, r, v, kd, g, lnx_g, lnx_b, r_k, *, tt, name):
    e2 = _head_ones()
    return _rowcall(lambda i, *a: (_post(*a),), [y0, y1, r, v, kd, g], [lnx_g, lnx_b, r_k, e2], [(RW, BF16)], [],
                    tt=tt, name=name)[0]


def _mix_post_bwd(y0, y1, r, v, kd, g, lnx_g, lnx_b, r_k, dout, *, tt, name):
    e2 = _head_ones()

    def fn(i, y0v, y1v, rv, vv, kdv, gv, dov, lg, lb, rk, e2v):
        _, vjp = jax.vjp(lambda *a: _post(*a, e2v), y0v, y1v, rv, vv, kdv, gv, lg, lb, rk)
        gr = vjp(dov.astype(F32))
        return gr[0], gr[2], gr[3], gr[4], gr[5], gr[6], gr[7], gr[8]
    return _rowcall(fn, [y0, y1, r, v, kd, g, dout], [lnx_g, lnx_b, r_k, e2],
                    [(RW, F32), (RW, F32), (RW, F32), (2 * RW, F32), (RW, F32)], [(1, RW), (1, RW), (1, RW)],
                    tt=tt, name=name)


def _conv_fwd(p, dw, db, ln_g, ln_b, *, seq, tt, name):
    n_tok = p.shape[0]
    tps = seq // tt

    def glu(x, gate):
        return x * _sigmoid(gate)

    def body(u_ref, g_ref, uhp, ghp, uhn, ghn, dw_ref, db_ref, lg_ref, lb_ref, yc_o, y_o, ext):
        i = pl.program_id(0)
        first = (i % tps) == 0
        last = (i % tps) == tps - 1
        ext[pl.ds(0, 16), :] = jnp.where(first, 0.0, glu(uhp[...], ghp[...]))
        ext[pl.ds(16, tt), :] = glu(u_ref[...], g_ref[...])
        ext[pl.ds(16 + tt, 16), :] = jnp.where(last, 0.0, glu(uhn[...], ghn[...]))
        acc = jnp.zeros((tt, CW), F32) + db_ref[...]
        for k in range(CONV_K):
            acc = acc + ext[pl.ds(k + 1, tt), :] * dw_ref[pl.ds(k, 1), :]
        yc_o[...] = acc
        y_o[...] = _conv_post(acc, lg_ref[...], lb_ref[...]).astype(y_o.dtype)

    uhp_s, uhn_s = _halo_specs(CW, 16, tt, n_tok, 4)
    ghp_s, ghn_s = _halo_specs(CW, 16, tt, n_tok, 5)
    fulls = [dw, db, ln_g, ln_b]
    return pl.pallas_call(
        body, name=name,
        out_shape=[jax.ShapeDtypeStruct((n_tok, CW), F32), jax.ShapeDtypeStruct((n_tok, CW), BF16)],
        grid=(n_tok // tt,),
        in_specs=[pl.BlockSpec((tt, CW), lambda i: (i, 4)), pl.BlockSpec((tt, CW), lambda i: (i, 5)),
                  uhp_s, ghp_s, uhn_s, ghn_s] + [pl.BlockSpec(a.shape, lambda i: (0, 0)) for a in fulls],
        out_specs=[pl.BlockSpec((tt, CW), lambda i: (i, 0)), pl.BlockSpec((tt, CW), lambda i: (i, 0))],
        scratch_shapes=[pltpu.VMEM((tt + 32, CW), F32)],
        compiler_params=_cparams(("parallel",)),
    )(p, p, p, p, p, p, *fulls)


def _conv_post_bwd(yc, dy, ln_g, ln_b, *, tt, name):
    def fn(i, ycv, dyv, lg, lb):
        _, vjp = jax.vjp(_conv_post, ycv, lg, lb)
        dyc, dg, dbb = vjp(dyv.astype(F32))
        return dyc, dg, dbb, jnp.sum(dyc, axis=0, keepdims=True)
    return _rowcall(fn, [yc, dy], [ln_g, ln_b], [(CW, F32)], [(1, CW), (1, CW), (1, CW)], tt=tt, name=name)


def _conv_bwd(dyc, p, dw, *, seq, tt, name):
    n_tok = p.shape[0]
    tps = seq // tt

    def body(d_ref, dhp, dhn, u_ref, g_ref, uhp, ghp, uhn, ghn, dw_ref, dp_o, ddw_o, ext):
        i = pl.program_id(0)
        first = (i % tps) == 0
        last = (i % tps) == tps - 1
        dv = d_ref[...]
        ext[pl.ds(0, 16), :] = jnp.where(first, 0.0, dhp[...])
        ext[pl.ds(16, tt), :] = dv
        ext[pl.ds(16 + tt, 16), :] = jnp.where(last, 0.0, dhn[...])
        du = jnp.zeros((tt, CW), F32)
        for k in range(CONV_K):
            du = du + ext[pl.ds(31 - k, tt), :] * dw_ref[pl.ds(k, 1), :]
        uv, gv = u_ref[...], g_ref[...]
        sg = _sigmoid(gv)
        dp_o[:, 0:CW] = (du * sg).astype(dp_o.dtype)
        dp_o[:, CW:2 * CW] = (du * uv * sg * (1.0 - sg)).astype(dp_o.dtype)
        ext[pl.ds(0, 16), :] = jnp.where(first, 0.0, uhp[...] * _sigmoid(ghp[...]))
        ext[pl.ds(16, tt), :] = uv * sg
        ext[pl.ds(16 + tt, 16), :] = jnp.where(last, 0.0, uhn[...] * _sigmoid(ghn[...]))

        @pl.when(i == 0)
        def _():
            ddw_o[...] = jnp.zeros_like(ddw_o)
        for k in range(CONV_K):
            ddw_o[pl.ds(k, 1), :] += jnp.sum(dv * ext[pl.ds(k + 1, tt), :], axis=0, keepdims=True)

    dhp_s, dhn_s = _halo_specs(CW, 16, tt, n_tok, 0)
    uhp_s, uhn_s = _halo_specs(CW, 16, tt, n_tok, 4)
    ghp_s, ghn_s = _halo_specs(CW, 16, tt, n_tok, 5)
    return pl.pallas_call(
        body, name=name,
        out_shape=[jax.ShapeDtypeStruct((n_tok, 2 * CW), BF16), jax.ShapeDtypeStruct((32, CW), F32)],
        grid=(n_tok // tt,),
        in_specs=[pl.BlockSpec((tt, CW), lambda i: (i, 0)), dhp_s, dhn_s,
                  pl.BlockSpec((tt, CW), lambda i: (i, 4)), pl.BlockSpec((tt, CW), lambda i: (i, 5)),
                  uhp_s, ghp_s, uhn_s, ghn_s, pl.BlockSpec(dw.shape, lambda i: (0, 0))],
        out_specs=[pl.BlockSpec((tt, 2 * CW), lambda i: (i, 0)), pl.BlockSpec((32, CW), lambda i: (0, 0))],
        scratch_shapes=[pltpu.VMEM((tt + 32, CW), F32)],
        compiler_params=_cparams(("arbitrary",)),
    )(dyc, dyc, dyc, p, p, p, p, p, p, dw)


def _segdot(hi, lo, e2):
    outs = []
    for c in range(hi.shape[1] // 256):
        lhs = jnp.concatenate([hi[:, 256 * c:256 * (c + 1)], lo[:, 256 * c:256 * (c + 1)]], axis=1)
        outs.append(jnp.dot(lhs, e2, preferred_element_type=F32))
    return jnp.concatenate(outs, axis=1)


SCAN_PASSES = 1


def _seg_streams(parts, e2):
    if SCAN_PASSES == 1:
        hi = jnp.concatenate([p.astype(BF16) for p in parts], axis=0)
        full = jnp.concatenate([jnp.dot(hi[:, 256 * c:256 * (c + 1)], e2[:256], preferred_element_type=F32)
                                for c in range(RW // 256)], axis=1)
    else:
        pieces = [_split16(p) for p in parts]
        full = _segdot(jnp.concatenate([h for h, _ in pieces], axis=0), jnp.concatenate([l for _, l in pieces], axis=0), e2)
    return [full[s * HEAD:(s + 1) * HEAD] for s in range(len(parts))]


def _diag_mask():
    return lax.broadcasted_iota(jnp.int32, (HEAD, RW), 0) == lax.broadcasted_iota(jnp.int32, (HEAD, RW), 1) % HEAD


def _col_form(rows, dmask, e2):
    his, los = [], []
    for x in rows:
        hi = x.astype(BF16).astype(F32)
        lo = x - hi
        his.append(jnp.where(dmask, jnp.broadcast_to(hi, (HEAD, RW)), 0.0).astype(BF16))
        los.append(jnp.where(dmask, jnp.broadcast_to(lo, (HEAD, RW)), 0.0).astype(BF16))
    full = _segdot(jnp.concatenate(his, axis=0), jnp.concatenate(los, axis=0), e2)
    return [full[s * HEAD:(s + 1) * HEAD] for s in range(len(rows))]


def _row_form(col, dmask):
    return jnp.sum(jnp.where(dmask, col, 0.0), axis=0, keepdims=True)


def _row_sum(x):
    return jnp.sum(x, axis=0, keepdims=True)


def _wkv_fwd(r, v, kk, w, kd, b, *, tb, name):
    bsz, seq, _ = r.shape
    nb = seq // tb
    ns = 2 * bsz
    e2 = _head_ones()

    def body(r0, r1, v0, v1, k0, k1, w0, w1, kd0, kd1, b0, b1, e2_ref, y0_o, y1_o, sp_o, last_o, s_ref):
        i = pl.program_id(0)

        @pl.when(i == 0)
        def _():
            s_ref[...] = jnp.zeros_like(s_ref)

        e2v = e2_ref[...]
        dmask = _diag_mask()
        y_refs = (y0_o, y1_o)

        def step(j, carry):
            tl = (j, tb - 1 - j)

            def rows(refs):
                return [refs[d][bb, pl.ds(tl[d], 1), :] for d in (0, 1) for bb in range(bsz)]

            kk_r, w_r, b_r, kd_r, r_r = rows((k0, k1)), rows((w0, w1)), rows((b0, b1)), rows((kd0, kd1)), rows((r0, r1))
            s_old = [s_ref[s * HEAD:(s + 1) * HEAD, :] for s in range(ns)]
            for s in range(ns):
                sp_o[s, pl.ds(j, 1), :, :] = s_old[s].reshape(1, HEAD, RW)
            sa = _seg_streams([s_old[s] * kk_r[s] for s in range(ns)], e2v)
            vc = _col_form(rows((v0, v1)), dmask, e2v)
            s_new = [s_old[s] * w_r[s] - sa[s] * b_r[s] + vc[s] * kd_r[s] for s in range(ns)]
            for s in range(ns):
                s_ref[s * HEAD:(s + 1) * HEAD, :] = s_new[s]
            ycol = _seg_streams([s_new[s] * r_r[s] for s in range(ns)], e2v)
            for d in (0, 1):
                for bb in range(bsz):
                    y_refs[d][bb, pl.ds(tl[d], 1), :] = _row_form(ycol[d * bsz + bb], dmask)
            return carry

        lax.fori_loop(0, tb, step, 0)
        last_o[...] = s_ref[...]

    def blk(width_idx, rev):
        if rev:
            return pl.BlockSpec((bsz, tb, RW), lambda i: (0, nb - 1 - i, width_idx))
        return pl.BlockSpec((bsz, tb, RW), lambda i: (0, i, width_idx))

    in_specs = [blk(0, False), blk(0, True)] * 3 + [blk(0, False), blk(1, True)] * 3
    in_specs.append(pl.BlockSpec(e2.shape, lambda i: (0, 0)))
    return pl.pallas_call(
        body, name=name,
        out_shape=[jax.ShapeDtypeStruct((bsz, seq, RW), F32), jax.ShapeDtypeStruct((bsz, seq, RW), F32),
                   jax.ShapeDtypeStruct((ns, seq, HEAD, RW), F32), jax.ShapeDtypeStruct((ns * HEAD, RW), F32)],
        grid=(nb,),
        in_specs=in_specs,
        out_specs=[blk(0, False), blk(0, True), pl.BlockSpec((ns, tb, HEAD, RW), lambda i: (0, i, 0, 0)),
                   pl.BlockSpec((ns * HEAD, RW), lambda i: (0, 0))],
        scratch_shapes=[pltpu.VMEM((ns * HEAD, RW), F32)],
        compiler_params=_cparams(("arbitrary",)),
    )(r, r, v, v, kk, kk, w, w, kd, kd, b, b, e2)


def _wkv_bwd(r, v, kk, w, kd, b, dy, sp, s_last, *, tb, name):
    bsz, seq, _ = r.shape
    nb = seq // tb
    ns = 2 * bsz
    e2 = _head_ones()

    def body(r0, r1, v0, v1, k0, k1, dy0, dy1, w0, w1, kd0, kd1, b0, b1, sp_ref, last_ref, e2_ref, *rest):
        outs, g_ref, post_ref = rest[:-2], rest[-2], rest[-1]
        i = pl.program_id(0)

        @pl.when(i == 0)
        def _():
            g_ref[...] = jnp.zeros_like(g_ref)
            post_ref[...] = last_ref[...]

        e2v = e2_ref[...]
        dmask = _diag_mask()

        def step(jj, carry):
            sl = tb - 1 - jj
            tl = (sl, jj)

            def rows(refs):
                return [refs[d][bb, pl.ds(tl[d], 1), :] for d in (0, 1) for bb in range(bsz)]

            kk_r, w_r, b_r, kd_r, r_r = rows((k0, k1)), rows((w0, w1)), rows((b0, b1)), rows((kd0, kd1)), rows((r0, r1))
            s_old = [sp_ref[s, pl.ds(sl, 1), :, :].reshape(HEAD, RW) for s in range(ns)]
            sa = _seg_streams([s_old[s] * kk_r[s] for s in range(ns)], e2v)
            vc = _col_form(rows((v0, v1)), dmask, e2v)
            dyc = _col_form(rows((dy0, dy1)), dmask, e2v)
            gt = [g_ref[s * HEAD:(s + 1) * HEAD, :] + dyc[s] * r_r[s] for s in range(ns)]
            both = _seg_streams([gt[s] * b_r[s] for s in range(ns)] + [gt[s] * kd_r[s] for s in range(ns)], e2v)
            gb, dvc = both[:ns], both[ns:]
            for d in (0, 1):
                for bb in range(bsz):
                    s = d * bsz + bb
                    at = (bb, pl.ds(tl[d], 1), slice(None))
                    outs[0 + d][at] = _row_sum(post_ref[s * HEAD:(s + 1) * HEAD, :] * dyc[s])
                    post_ref[s * HEAD:(s + 1) * HEAD, :] = s_old[s]
                    outs[2 + d][at] = _row_form(dvc[s], dmask)
                    outs[4 + d][at] = -_row_sum(s_old[s] * gb[s])
                    outs[6 + d][at] = _row_sum(s_old[s] * gt[s])
                    outs[8 + d][at] = _row_sum(gt[s] * vc[s])
                    outs[10 + d][at] = -_row_sum(sa[s] * gt[s])
                    g_ref[s * HEAD:(s + 1) * HEAD, :] = gt[s] * w_r[s] - gb[s] * kk_r[s]
            return carry

        lax.fori_loop(0, tb, step, 0)

    def blk(width_idx, rev):
        if rev:
            return pl.BlockSpec((bsz, tb, RW), lambda i: (0, nb - 1 - i, width_idx))
        return pl.BlockSpec((bsz, tb, RW), lambda i: (0, i, width_idx))

    in_specs = [blk(0, True), blk(0, False)] * 4 + [blk(0, True), blk(1, False)] * 3
    in_specs.append(pl.BlockSpec((ns, tb, HEAD, RW), lambda i: (0, nb - 1 - i, 0, 0)))
    in_specs.append(pl.BlockSpec((ns * HEAD, RW), lambda i: (0, 0)))
    in_specs.append(pl.BlockSpec(e2.shape, lambda i: (0, 0)))
    return pl.pallas_call(
        body, name=name,
        out_shape=[jax.ShapeDtypeStruct((bsz, seq, RW), F32)] * 12,
        grid=(nb,),
        in_specs=in_specs,
        out_specs=[blk(0, True), blk(0, False)] * 6,
        scratch_shapes=[pltpu.VMEM((ns * HEAD, RW), F32), pltpu.VMEM((ns * HEAD, RW), F32)],
        compiler_params=_cparams(("arbitrary",)),
    )(r, r, v, v, kk, kk, dy, dy, w, w, kd, kd, b, b, sp, s_last, e2)


def _block_diag2(w):
    z = jnp.zeros_like(w[0])
    return jnp.concatenate([jnp.concatenate([w[0], z], axis=1), jnp.concatenate([z, w[1]], axis=1)], axis=0)


def _pad_in_cols(a):
    z = jnp.zeros(a.shape[:-1] + (SHIFT_PAD - SHIFT_COLS,), a.dtype)
    return jnp.concatenate([a[..., :SHIFT_COLS], z, a[..., SHIFT_COLS:]], axis=-1)


def _follow(small, token):
    return small if token is None else small + token[0:1, 0:1]


def _local_step(x, target, wts, *, tt, tb, start_token=None, more_weights=None, grads_ready=None):
    bsz, seq, _ = x.shape
    n_tok = bsz * seq
    row = lambda a: a.reshape(1, -1).astype(F32)
    x0 = x.reshape(n_tok, D_MODEL)
    tgt = target.reshape(n_tok, D_MODEL)
    ln = {k: row(wts[k]) for k in ("ln1_g", "ln1_b", "ln2_g", "ln2_b", "ln3_g", "ln3_b")}
    if grads_ready is None:
        grads_ready = lambda names, slabs: None

    w1i, w1o = wts["ffn1_w_in"], wts["ffn1_w_out"]
    h1, act1 = _ffn_in(x0, w1i, tm=TM_FFN, after=start_token, name="ffn1_in")
    z1, x1, x1b = _mm_ln([act1], w1o, x0, ln["ln1_g"], ln["ln1_b"], 0.5, tm=TM_LN, name="ffn1_out_ln1")
    if more_weights is not None:
        wts = {**wts, **more_weights("mix", x1b)}
    win = _pad_in_cols(wts["w_in"])
    zpad = jnp.zeros((1, SHIFT_PAD - SHIFT_COLS), F32)
    mu_p = jnp.concatenate([row(wts["mu_prev"]), zpad], axis=1)
    mu_n = jnp.concatenate([row(wts["mu_next"]), zpad], axis=1)
    w2b, a2b = _block_diag2(wts["w2"]), _block_diag2(wts["a2"])
    w0c, a0c = row(wts["w0"]), row(wts["a0"])
    g2p = jnp.concatenate([wts["g2"], jnp.zeros((GATE_PAD - GATE_LORA, RW), F32)], axis=0)
    k_k, k_a, r_k = row(wts["k_k"]), row(wts["k_a"]), row(wts["r_k"])
    lnx_g, lnx_b = row(wts["lnx_g"]), row(wts["lnx_b"])
    cdw, cb, clg, clb = wts["conv_dw"], row(wts["conv_b"]), row(wts["conv_ln_g"]), row(wts["conv_ln_b"])
    small = (mu_p, mu_n, w2b, w0c, a2b, a0c, g2p, k_k, k_a)
    seq3 = lambda a: a.reshape(bsz, seq, a.shape[-1])
    flat = lambda a: a.reshape(n_tok, a.shape[-1])

    p = _matmul(x1b, win, name="proj_in")
    r, v, kk, w, kd, b, g = _mix_prep(p, *small, seq=seq, tt=tt, name="mix_prep")
    y0, y1, sp, s_last = _wkv_fwd(seq3(r), seq3(v), seq3(kk), seq3(w), seq3(kd), seq3(b), tb=tb, name="wkv_fwd")
    y0, y1 = flat(y0), flat(y1)
    yr = _mix_post(y0, y1, r, v, kd, g, lnx_g, lnx_b, r_k, tt=tt, name="mix_post")
    yc, yv = _conv_fwd(p, cdw, cb, clg, clb, seq=seq, tt=tt, name="conv_fwd")
    if more_weights is not None:
        wts = {**wts, **more_weights("out", yr)}
    wout, w2i, w2o = wts["w_out"], wts["ffn2_w_in"], wts["ffn2_w_out"]
    z2, x2, x2b = _mm_ln([yr, yv], wout, x1, ln["ln2_g"], ln["ln2_b"], 1.0, tm=TM_LN, name="proj_out_ln2")
    h2, act2 = _ffn_in(x2b, w2i, tm=TM_FFN, name="ffn2_in")

    gr = {}
    slab_rows = lambda a: a.reshape((N_CHIPS, a.shape[0] // N_CHIPS) + a.shape[1:])
    dw_kw = dict(ta=True, out_dtype=BF16)
    dz3, gr["ln3_g"], gr["ln3_b"], loss_part = _mm_ln_loss(act2, w2o, x2, ln["ln3_g"], ln["ln3_b"], tgt, 0.5, tm=TM_LN,
                                                           name="ffn2_out_ln3_loss")
    dh2 = _ffn_out_bwd(dz3, w2o, h2, tm=TM_FFN, name="ffn2_out_dx")
    gr["ffn2_w_out"] = slab_rows(_matmul(act2, dz3, scale=0.5, tm=D_FF // 2, name="ffn2_out_dw", **dw_kw))
    dz2, gr["ln2_g"], gr["ln2_b"] = _mm_nt_res([dh2], w2i, dz3, ln=(z2, ln["ln2_g"], ln["ln2_b"]), tm=TM_FFN,
                                               name="ffn2_in_dx_ln2")
    gr["ffn2_w_in"] = _matmul(x2b, dh2, col_slabs=True, tn=2 * D_FF // N_CHIPS, name="ffn2_in_dw", **dw_kw)
    dmix = _matmul(dz2, wout, tb=True, name="proj_out_dx")
    gr["w_out"] = slab_rows(jnp.concatenate([_matmul(yr, dz2, name="proj_out_dw_rwkv", **dw_kw),
                                             _matmul(yv, dz2, name="proj_out_dw_conv", **dw_kw)], axis=0))
    tok = grads_ready(("ffn2_w_out", "ffn2_w_in", "w_out"), [gr["ffn2_w_out"], gr["ffn2_w_in"], gr["w_out"]])
    dyr, dyv = (dmix, RW, 0), (dmix, RW, 1)
    dy, dr_p, dv_p, dkd_p, dg, gr["lnx_g"], gr["lnx_b"], gr["r_k"] = _mix_post_bwd(
        y0, y1, r, v, kd, g, _follow(lnx_g, tok), lnx_b, r_k, dyr, tt=tt, name="mix_post_bwd")
    scan_g = _wkv_bwd(seq3(r), seq3(v), seq3(kk), seq3(w), seq3(kd), seq3(b), seq3(dy), sp, s_last, tb=tb,
                      name="wkv_bwd")
    dr0, dr1, dv0, dv1, dk0, dk1, dw0, dw1, dkd0, dkd1, db0, db1 = [flat(a) for a in scan_g]
    ct_terms = [[dr_p, dr0, dr1], [dv_p, dv0, dv1], [dk0, dk1], [(dw0, dw1)], [dkd_p, (dkd0, dkd1)], [(db0, db1)], [dg]]
    dyc, gr["conv_ln_g"], gr["conv_ln_b"], gr["conv_b"] = _conv_post_bwd(yc, dyv, clg, clb, tt=tt, name="conv_post_bwd")
    dpc, ddw = _conv_bwd(dyc, p, cdw, seq=seq, tt=tt, name="conv_bwd")
    gr["conv_dw"] = ddw[:CONV_K]
    dps, dw2b, dw0c, da2b, da0c, dg2p, gr["k_k"], gr["k_a"] = _mix_prep_bwd(
        p, *small, ct_terms, seq=seq, tt=tt, name="mix_prep_bwd")
    gr["w2"] = jnp.stack([dw2b[:LORA, :RW], dw2b[LORA:, RW:]])
    gr["a2"] = jnp.stack([da2b[:LORA, :RW], da2b[LORA:, RW:]])
    gr["w0"], gr["a0"], gr["g2"] = dw0c.reshape(2, RW), da0c.reshape(2, RW), dg2p[:GATE_LORA]
    dpsh, dmu_p, dmu_n = _shift_bwd(dps, p, mu_p, mu_n, seq=seq, tt=tt, name="shift_bwd")
    gr["mu_prev"], gr["mu_next"] = dmu_p[:, :SHIFT_COLS], dmu_n[:, :SHIFT_COLS]
    dwin = jnp.concatenate([_matmul(x1b, dpsh, name="proj_in_dw_shift", **dw_kw)[:, :SHIFT_COLS],
                            _matmul(x1b, dpc, name="proj_in_dw_conv", **dw_kw)], axis=1)
    gr["w_in"] = jnp.moveaxis(dwin.reshape(D_MODEL, N_CHIPS, IN_COLS // N_CHIPS), 1, 0)
    tok = grads_ready(("w_in",), [gr["w_in"]])
    dz1, gr["ln1_g"], gr["ln1_b"] = _mm_nt_res([dpsh, dpc], win, dz2, ln=(z1, ln["ln1_g"], ln["ln1_b"]), tm=TM_FFN,
                                               after=tok, name="proj_in_dx_ln1")
    dh1 = _ffn_out_bwd(dz1, w1o, h1, tm=TM_FFN, name="ffn1_out_dx")
    gr["ffn1_w_out"] = slab_rows(_matmul(act1, dz1, scale=0.5, tm=D_FF // 2, name="ffn1_out_dw", **dw_kw))
    tok = grads_ready(("ffn1_w_out",), [gr["ffn1_w_out"]])
    gr["ffn1_w_in"] = _matmul(x0, dh1, col_slabs=True, tn=2 * D_FF // N_CHIPS, after=tok, name="ffn1_in_dw", **dw_kw)
    tok = grads_ready(("ffn1_w_in",), [gr["ffn1_w_in"]])
    dx0 = _mm_nt_res([dh1], w1i, dz1, tm=TM_FFN, after=tok, name="ffn1_in_dx")
    return loss_part, dx0.reshape(bsz, seq, D_MODEL), gr


def _mesh_pos():
    return lax.axis_index("x"), lax.axis_index("y"), lax.axis_index("c")


def _other_chips(x, y):
    return [(1 - x, y), (x, 1 - y), (1 - x, 1 - y)]


def _gather_chips(shards, *, name):
    n = len(shards)
    halves = [s.shape[0] // 2 for s in shards]
    assert all(2 * h == s.shape[0] for h, s in zip(halves, shards))

    def body(*refs):
        ins, outs = refs[:n], refs[n:2 * n]
        send_sems, recv_sems, fwd_send_sems, fwd_recv_sems, loc_sems = refs[2 * n:]
        x, y, c = _mesh_pos()
        q = 2 * x + y
        peers = _other_chips(x, y)
        local = [pltpu.make_async_copy(ins[a], outs[a].at[q], loc_sems.at[a]) for a in range(n)]
        for cp in local:
            cp.start()

        def half(a, chip, core):
            return outs[a].at[chip, pl.ds(core * halves[a], halves[a])]

        sends = [pltpu.make_async_remote_copy(ins[a].at[pl.ds(c * halves[a], halves[a])], half(a, q, c),
                                              send_sems.at[a, k], recv_sems.at[a, k],
                                              device_id=(px, py, c), device_id_type=MESH)
                 for a in range(n) for k, (px, py) in enumerate(peers)]
        for cp in sends:
            cp.start()
        passed = []
        for a in range(n):
            for k, (px, py) in enumerate(peers):
                mine = half(a, 2 * px + py, c)
                pltpu.make_async_remote_copy(mine, mine, send_sems.at[a, k], recv_sems.at[a, k],
                                             device_id=(px, py, c), device_id_type=MESH).wait_recv()
                cp = pltpu.make_async_remote_copy(mine, mine, fwd_send_sems.at[a, k], fwd_recv_sems.at[a, k],
                                                  device_id=(x, y, 1 - c), device_id_type=MESH)
                cp.start()
                passed.append(cp)
        for a in range(n):
            for k, (px, py) in enumerate(peers):
                theirs = half(a, 2 * px + py, 1 - c)
                pltpu.make_async_remote_copy(theirs, theirs, fwd_send_sems.at[a, k], fwd_recv_sems.at[a, k],
                                             device_id=(x, y, 1 - c), device_id_type=MESH).wait_recv()
        for cp in sends + passed:
            cp.wait_send()
        for cp in local:
            cp.wait()

    any_spec = pl.BlockSpec(memory_space=pl.ANY)
    return pl.pallas_call(
        body, name=name,
        out_shape=[jax.ShapeDtypeStruct((N_CHIPS,) + s.shape, s.dtype) for s in shards],
        in_specs=[any_spec] * n, out_specs=[any_spec] * n,
        scratch_shapes=[pltpu.SemaphoreType.DMA((n, 3))] * 4 + [pltpu.SemaphoreType.DMA((n,))],
        compiler_params=pltpu.CompilerParams(has_side_effects=True),
    )(*shards)


HBM_SPEC = pl.BlockSpec(memory_space=pltpu.HBM)
SEM_SPEC = pl.BlockSpec(memory_space=pltpu.SEMAPHORE)
ANY_SPEC = pl.BlockSpec(memory_space=pl.ANY)
SIDE_EFFECT = pltpu.SideEffectType.DATAFLOW_SIDE_EFFECTING


def _chip_copies(src_refs, land_refs, send_sems, recv_sems, scatter, arriving=False):
    x, y, c = _mesh_pos()
    cps = []
    for a, (src, land) in enumerate(zip(src_refs, land_refs)):
        for k, (px, py) in enumerate(_other_chips(x, y)):
            slot = k if scatter else (2 * px + py if arriving else 2 * x + y)
            cps.append(pltpu.make_async_remote_copy(src.at[2 * px + py] if scatter else src, land.at[slot],
                                                    send_sems.at[3 * a + k], recv_sems.at[3 * a + k],
                                                    device_id=(px, py, c), device_id_type=MESH))
    return cps


def _exchange_start(srcs, *, scatter, after, name):
    n = len(srcs)
    lands = [lax.empty((3,) + s.shape[1:] if scatter else (N_CHIPS,) + s.shape, s.dtype) for s in srcs]

    def body(*refs):
        src_refs, land_refs = refs[:n], refs[n:2 * n]
        send_sems, recv_sems = refs[2 * n + 1:2 * n + 3]
        token = refs[-1]
        for cp in _chip_copies(src_refs, land_refs, send_sems, recv_sems, scatter):
            cp.start()
        token[...] = jnp.zeros_like(token)

    hbm = lambda a: pltpu.with_memory_space_constraint(a, pltpu.HBM)
    outs = pl.pallas_call(
        body, name=name,
        out_shape=(pltpu.SemaphoreType.DMA((3 * n,)), pltpu.SemaphoreType.DMA((3 * n,)),
                   *[pltpu.HBM(a.shape, a.dtype) for a in srcs + lands], jax.ShapeDtypeStruct((8, LANES), F32)),
        in_specs=[HBM_SPEC] * (2 * n) + [ANY_SPEC],
        out_specs=(SEM_SPEC, SEM_SPEC, *[HBM_SPEC] * (2 * n), pl.BlockSpec(memory_space=pltpu.VMEM)),
        input_output_aliases={i: 2 + i for i in range(2 * n)},
        compiler_params=pltpu.CompilerParams(has_side_effects=SIDE_EFFECT),
    )(*[hbm(a) for a in srcs + lands], after)
    return outs[0], outs[1], list(outs[2:2 + n]), list(outs[2 + n:2 + 2 * n]), outs[-1]


def _exchange_wait(started, *, scatter, after, name):
    send_sems, recv_sems, srcs, lands, _ = started
    n = len(srcs)

    def body(*refs):
        src_refs, land_refs = refs[:n], refs[n:2 * n]
        send_s, recv_s = refs[2 * n:2 * n + 2]
        for cp in _chip_copies(src_refs, land_refs, send_s, recv_s, scatter, arriving=True):
            cp.wait_send()
            cp.wait_recv()

    outs = pl.pallas_call(
        body, name=name,
        out_shape=tuple(pltpu.HBM(a.shape, a.dtype) for a in srcs + lands),
        in_specs=[HBM_SPEC] * (2 * n) + [SEM_SPEC, SEM_SPEC, ANY_SPEC],
        out_specs=tuple([HBM_SPEC] * (2 * n)),
        input_output_aliases={i: i for i in range(2 * n)},
        compiler_params=pltpu.CompilerParams(has_side_effects=SIDE_EFFECT),
    )(*srcs, *lands, send_sems, recv_sems, after)
    return list(outs[:n]), list(outs[n:])


def _by_chip(own, land):
    xi, yi, _ = _mesh_pos()
    return lax.dynamic_update_index_in_dim(land, own, 2 * xi + yi, 0)


def _swap_sibling(arrs, *, name):
    n = len(arrs)

    def body(*refs):
        ins, outs = refs[:n], refs[n:2 * n]
        send_sems, recv_sems = refs[2 * n:]
        x, y, c = _mesh_pos()
        cps = [pltpu.make_async_remote_copy(ins[a], outs[a], send_sems.at[a], recv_sems.at[a],
                                            device_id=(x, y, 1 - c), device_id_type=MESH) for a in range(n)]
        for cp in cps:
            cp.start()
        for cp in cps:
            cp.wait_recv()
        for cp in cps:
            cp.wait_send()

    any_spec = pl.BlockSpec(memory_space=pl.ANY)
    return pl.pallas_call(
        body, name=name,
        out_shape=[jax.ShapeDtypeStruct(s.shape, s.dtype) for s in arrs],
        in_specs=[any_spec] * n, out_specs=[any_spec] * n,
        scratch_shapes=[pltpu.SemaphoreType.DMA((n,)), pltpu.SemaphoreType.DMA((n,))],
        compiler_params=pltpu.CompilerParams(has_side_effects=True),
    )(*arrs)


def _all_reduce_rows(vec, *, name):
    rows = vec.shape[0]

    def body(v_ref, o_ref, land, send_sems, recv_sems):
        x, y, c = _mesh_pos()
        me = 4 * x + 2 * y + c
        land[me] = v_ref[...]
        cps = []
        for m in range(1, 8):
            mx, my, mc = (m >> 2) & 1, (m >> 1) & 1, m & 1
            tx, ty, tc = (x + mx) % 2, (y + my) % 2, (c + mc) % 2
            cps.append(pltpu.make_async_remote_copy(v_ref, land.at[me], send_sems.at[m - 1], recv_sems.at[me],
                                                    device_id=(tx, ty, tc), device_id_type=MESH))
        for cp in cps:
            cp.start()
        for m in range(1, 8):
            mx, my, mc = (m >> 2) & 1, (m >> 1) & 1, m & 1
            src = 4 * ((x + mx) % 2) + 2 * ((y + my) % 2) + (c + mc) % 2
            pltpu.make_async_remote_copy(v_ref, land.at[src], send_sems.at[m - 1], recv_sems.at[src],
                                         device_id=(x, y, c), device_id_type=MESH).wait_recv()
        for cp in cps:
            cp.wait_send()
        acc = land[0]
        for d in range(1, 8):
            acc = acc + land[d]
        o_ref[...] = acc

    vm = pl.BlockSpec(memory_space=pltpu.VMEM)
    return pl.pallas_call(
        body, name=name,
        out_shape=jax.ShapeDtypeStruct(vec.shape, F32),
        in_specs=[vm], out_specs=vm,
        scratch_shapes=[pltpu.VMEM((8, rows, LANES), F32), pltpu.SemaphoreType.DMA((7,)), pltpu.SemaphoreType.DMA((8,))],
        compiler_params=pltpu.CompilerParams(has_side_effects=True, vmem_limit_bytes=VMEM_LIMIT),
    )(vec)


def _adamw(w, g, m, v):
    m = ADAM_B1 * m + (1.0 - ADAM_B1) * g
    v = ADAM_B2 * v + (1.0 - ADAM_B2) * (g * g)
    m_hat = m / (1.0 - ADAM_B1 ** ADAM_STEP)
    v_hat = v / (1.0 - ADAM_B2 ** ADAM_STEP)
    delta = -ADAM_LR * (m_hat / (jnp.sqrt(v_hat) + ADAM_EPS) + ADAM_WD * w)
    return delta, m, v


def _sum4(mine, land, *, name):
    rows, cols = mine.shape
    tr = _pick_rows(rows)

    def body(a_ref, l_ref, o_ref):
        o_ref[...] = (a_ref[...].astype(F32) + l_ref[0].astype(F32)) + (l_ref[1].astype(F32) + l_ref[2].astype(F32))

    return pl.pallas_call(
        body, name=name, out_shape=jax.ShapeDtypeStruct((rows, cols), F32), grid=(rows // tr,),
        in_specs=[pl.BlockSpec((tr, cols), lambda i: (i, 0)), pl.BlockSpec((3, tr, cols), lambda i: (0, i, 0))],
        out_specs=pl.BlockSpec((tr, cols), lambda i: (i, 0)),
        compiler_params=_cparams(("parallel",)),
    )(mine, land)


def _pick_rows(rows, want=256):
    for t in range(min(want, rows) // 8 * 8, 0, -8):
        if rows % t == 0:
            return t
    return rows


def _sum_adam(h_mine, h_sib, w, m, v, *, name):
    rows, cols = w.shape
    tr = _pick_rows(rows)

    def body(a_ref, b_ref, w_ref, m_ref, v_ref, g_o, d_o, m_o, v_o):
        g = a_ref[...] + b_ref[...]
        d, mn, vn = _adamw(w_ref[...], g, m_ref[...], v_ref[...])
        g_o[...], d_o[...], m_o[...], v_o[...] = g, d, mn, vn

    spec = pl.BlockSpec((tr, cols), lambda i: (i, 0))
    return pl.pallas_call(
        body, name=name, out_shape=[jax.ShapeDtypeStruct((rows, cols), F32)] * 4, grid=(rows // tr,),
        in_specs=[spec] * 5, out_specs=[spec] * 4, compiler_params=_cparams(("parallel",)),
    )(h_mine, h_sib, w, m, v)


def _adam_rows(w, g, m, v, *, name):
    def body(w_ref, g_ref, m_ref, v_ref, d_o, m_o, v_o):
        d_o[...], m_o[...], v_o[...] = _adamw(w_ref[...], g_ref[...], m_ref[...], v_ref[...])

    vm = pl.BlockSpec(memory_space=pltpu.VMEM)
    return pl.pallas_call(
        body, name=name, out_shape=[jax.ShapeDtypeStruct(w.shape, F32)] * 3,
        in_specs=[vm] * 4, out_specs=[vm] * 3, compiler_params=_cparams(),
    )(w, g, m, v)


def _size(shape):
    size = 1
    for d in shape:
        size *= d
    return size


def _pack_rows(arrs):
    blocks = []
    for a in arrs:
        flat = a.reshape(-1).astype(F32)
        flat = jnp.concatenate([flat, jnp.zeros((-flat.shape[0] % (8 * LANES),), F32)])
        blocks.append(flat.reshape(-1, LANES))
    return jnp.concatenate(blocks, axis=0)


def _unpack_rows(packed, shapes):
    out, row = [], 0
    for s in shapes:
        rows = -(-_size(s) // (8 * LANES)) * 8
        out.append(packed[row:row + rows].reshape(-1)[:_size(s)].reshape(s))
        row += rows
    return out


WEIGHTS = ['ffn1_w_in', 'ffn1_w_out', 'w_in', 'mu_prev', 'mu_next', 'w0', 'w2', 'a0', 'a2', 'g2', 'k_k', 'k_a', 'r_k',
           'lnx_g', 'lnx_b', 'conv_dw', 'conv_b', 'conv_ln_g', 'conv_ln_b', 'w_out', 'ffn2_w_in', 'ffn2_w_out',
           'ln1_g', 'ln1_b', 'ln2_g', 'ln2_b', 'ln3_g', 'ln3_b']
COL_SHARDED = ('ffn1_w_in', 'w_in', 'ffn2_w_in')
ROW_SHARDED = ('ffn1_w_out', 'w_out', 'ffn2_w_out')
BIG = COL_SHARDED + ROW_SHARDED
SMALL_SHARDED = ('w0', 'w2', 'a0', 'a2', 'g2', 'conv_dw')
REPLICATED = tuple(n for n in WEIGHTS if n not in BIG + SMALL_SHARDED)


def _train_step(x, target, w, m, v, *, tt, tb):
    xi, yi, _ = _mesh_pos()
    q = 2 * xi + yi

    early, mid, late = ("ffn1_w_in", "ffn1_w_out"), ("w_in",) + SMALL_SHARDED, ("w_out", "ffn2_w_in", "ffn2_w_out")
    shard = lambda n: w[n][0].astype(BF16) if n in BIG else w[n][0]

    def whole(n, slabs):
        if n in ROW_SHARDED:
            return slabs.reshape((-1,) + slabs.shape[2:])
        if n in ("ffn1_w_in", "ffn2_w_in"):
            return slabs
        return jnp.moveaxis(slabs, 0, -2).reshape(slabs.shape[1:-1] + (N_CHIPS * slabs.shape[-1],))

    full = {n: w[n][0] for n in REPLICATED}
    first = _gather_chips([shard(n) for n in early], name="gather_ffn1")
    full.update({n: whole(n, g) for n, g in zip(early, first)})
    mid_started = _exchange_start([shard(n) for n in mid], scatter=False, after=first[0], name="gather_mix_start")
    late_started = _exchange_start([shard(n) for n in late], scatter=False, after=mid_started[-1], name="gather_out_start")

    def more_weights(stage, after):
        names, started = (mid, mid_started) if stage == "mix" else (late, late_started)
        own, land = _exchange_wait(started, scatter=False, after=after, name="gather_%s_wait" % stage)
        got = {n: whole(n, _by_chip(o, l)) for n, o, l in zip(names, own, land)}
        full.update(got)
        return got

    sent = []

    def grads_ready(names, slabs):
        started = _exchange_start(slabs, scatter=True, after=slabs[0], name="scatter_%s_start" % names[0])
        sent.append((names, started))
        return started[-1]

    loss_part, grad_x, gr = _local_step(x, target, full, tt=tt, tb=tb, start_token=late_started[-1],
                                        more_weights=more_weights, grads_ready=grads_ready)

    halves = {}
    for names, started in sent:
        stacks, landed = _exchange_wait(started, scatter=True, after=grad_x, name="scatter_%s_wait" % names[0])
        for n, s, l in zip(names, stacks, landed):
            halves[n] = _sum4(lax.dynamic_index_in_dim(s, q, 0, keepdims=False), l, name="sum4_" + n)
    halves = [halves[n] for n in BIG]
    sib = _swap_sibling(halves, name="swap_halves")
    grad, delta, new_m, new_v = {}, {}, {}, {}
    for n, h, hs in zip(BIG, halves, sib):
        outs = _sum_adam(h, hs, w[n][0], m[n][0], v[n][0], name="adam_" + n)
        grad[n], delta[n], new_m[n], new_v[n] = [o[None] for o in outs]

    small_names = REPLICATED + SMALL_SHARDED
    small_full_shapes = [full[n].shape for n in small_names]
    red = _all_reduce_rows(_pack_rows([gr[n] for n in small_names] + [loss_part[0:1, 0:1]]), name="reduce_small")
    *red, loss = _unpack_rows(red, small_full_shapes + [()])
    red = dict(zip(small_names, red))
    gsm = {}
    for n in REPLICATED:
        gsm[n] = red[n].reshape(w[n].shape)
    for n in SMALL_SHARDED:
        width = w[n].shape[-1]
        gsm[n] = lax.dynamic_slice_in_dim(red[n], q * width, width, axis=red[n].ndim - 1).reshape(w[n].shape)
    shapes = [w[n].shape for n in small_names]
    d_p, m_p, v_p = _adam_rows(_pack_rows([w[n] for n in small_names]), _pack_rows([gsm[n] for n in small_names]),
                               _pack_rows([m[n] for n in small_names]), _pack_rows([v[n] for n in small_names]),
                               name="adam_small")
    for n, dd, mm, vv in zip(small_names, _unpack_rows(d_p, shapes), _unpack_rows(m_p, shapes), _unpack_rows(v_p, shapes)):
        grad[n], delta[n], new_m[n], new_v[n] = gsm[n], dd, mm, vv
    return loss, grad_x, grad, delta, new_m, new_v


def kernel(x, ffn1_w_in, ffn1_w_out, w_in, mu_prev, mu_next, w0, w2, a0, a2, g2, k_k, k_a, r_k, lnx_g, lnx_b, conv_dw, conv_b, conv_ln_g, conv_ln_b, w_out, ffn2_w_in, ffn2_w_out, ln1_g, ln1_b, ln2_g, ln2_b, ln3_g, ln3_b, loss_target, m_ffn1_w_in, m_ffn1_w_out, m_w_in, m_mu_prev, m_mu_next, m_w0, m_w2, m_a0, m_a2, m_g2, m_k_k, m_k_a, m_r_k, m_lnx_g, m_lnx_b, m_conv_dw, m_conv_b, m_conv_ln_g, m_conv_ln_b, m_w_out, m_ffn2_w_in, m_ffn2_w_out, m_ln1_g, m_ln1_b, m_ln2_g, m_ln2_b, m_ln3_g, m_ln3_b, v_ffn1_w_in, v_ffn1_w_out, v_w_in, v_mu_prev, v_mu_next, v_w0, v_w2, v_a0, v_a2, v_g2, v_k_k, v_k_a, v_r_k, v_lnx_g, v_lnx_b, v_conv_dw, v_conv_b, v_conv_ln_g, v_conv_ln_b, v_w_out, v_ffn2_w_in, v_ffn2_w_out, v_ln1_g, v_ln1_b, v_ln2_g, v_ln2_b, v_ln3_g, v_ln3_b):
    args = dict(locals())
    w = {n: args[n] for n in WEIGHTS}
    m = {n: args["m_" + n] for n in WEIGHTS}
    v = {n: args["v_" + n] for n in WEIGHTS}
    seq = x.shape[1]
    loss, grad_x, grad, delta, new_m, new_v = _train_step(x, loss_target, w, m, v, tt=min(256, seq), tb=min(TB_SCAN, seq))
    return (loss, grad_x, *[grad[n] for n in WEIGHTS], *[delta[n] for n in WEIGHTS],
            *[new_m[n] for n in WEIGHTS], *[new_v[n] for n in WEIGHTS])
```

```python
import functools

import jax
import jax.numpy as jnp
from jax import lax
from jax.experimental import pallas as pl
from jax.experimental.pallas import tpu as pltpu

F32 = jnp.float32
BF16 = jnp.bfloat16

D_MODEL = 1024
RW = 512
HEAD = 64
CW = 512
CONV_K = 31
CONV_PAD = 15
D_FF = 2816
LORA = 64
GATE_LORA = 160
GATE_PAD = 256
SHIFT_COLS = 1952
SHIFT_PAD = 2048
IN_COLS = 2976
IN_PAD = 3072
LN_EPS = 1e-5
GN_EPS = 64e-5
NORM_EPS = 1e-12
ALPHA = 2.0 ** 0.25
DECAY_SCALE = 0.6065306597126334
ADAM_LR, ADAM_B1, ADAM_B2, ADAM_EPS, ADAM_WD, ADAM_STEP = 0.001, 0.9, 0.999, 1e-08, 0.01, 10
N_CHIPS = 4
VMEM_LIMIT = 56 * 1024 * 1024
TM_FFN = 256
TM_LN = 512
TB_SCAN = 16

MESH = pl.DeviceIdType.MESH


def _cparams(sem=None, **kw):
    return pltpu.CompilerParams(dimension_semantics=sem, vmem_limit_bytes=VMEM_LIMIT, **kw)


LANES = 128


def _pick_tile(dim, want):
    for t in range(min(want, dim) // LANES * LANES, 0, -LANES):
        if dim % t == 0:
            return t
    return dim


def _after_operand(after):
    return ([], []) if after is None else ([pl.BlockSpec(memory_space=pl.ANY)], [after])


def _matmul(a, b, *, ta=False, tb=False, out_dtype=F32, tm=1024, tn=1024, tk=1024, scale=1.0, col_slabs=False,
            after=None, name):
    after_specs, after_args = _after_operand(after)
    if ta:
        k_dim, m_dim = a.shape
    else:
        m_dim, k_dim = a.shape
    n_dim = b.shape[0] if tb else b.shape[1]
    tm, tn, tk = _pick_tile(m_dim, tm), _pick_tile(n_dim, tn), _pick_tile(k_dim, tk)
    assert m_dim % tm == 0 and n_dim % tn == 0 and k_dim % tk == 0, (name, a.shape, b.shape, tm, tn, tk)
    nk = k_dim // tk
    dims = (((0,) if ta else (1,), (1,) if tb else (0,)), ((), ()))
    if col_slabs:
        out_shape = jax.ShapeDtypeStruct((n_dim // tn, m_dim, tn), out_dtype)
        out_spec = pl.BlockSpec((None, tm, tn), lambda i, j, k: (j, i, 0))
    else:
        out_shape = jax.ShapeDtypeStruct((m_dim, n_dim), out_dtype)
        out_spec = pl.BlockSpec((tm, tn), lambda i, j, k: (i, j))

    def body(a_ref, b_ref, *rest):
        o_ref, acc_ref = rest[-2:]
        kk = pl.program_id(2)

        @pl.when(kk == 0)
        def _():
            acc_ref[...] = jnp.zeros_like(acc_ref)

        acc_ref[...] += lax.dot_general(a_ref[...].astype(BF16), b_ref[...].astype(BF16), dims,
                                        preferred_element_type=F32)

        @pl.when(kk == nk - 1)
        def _():
            o_ref[...] = (acc_ref[...] * scale).astype(o_ref.dtype)

    a_spec = pl.BlockSpec((tk, tm), lambda i, j, k: (k, i)) if ta else pl.BlockSpec((tm, tk), lambda i, j, k: (i, k))
    b_spec = pl.BlockSpec((tn, tk), lambda i, j, k: (j, k)) if tb else pl.BlockSpec((tk, tn), lambda i, j, k: (k, j))
    return pl.pallas_call(
        body, name=name,
        out_shape=out_shape,
        grid=(m_dim // tm, n_dim // tn, nk),
        in_specs=[a_spec, b_spec] + after_specs,
        out_specs=out_spec,
        scratch_shapes=[pltpu.VMEM((tm, tn), F32)],
        compiler_params=_cparams(("parallel", "parallel", "arbitrary")),
    )(a, b, *after_args)


def _whole(shape):
    nd = len(shape)
    return pl.BlockSpec(shape, lambda i: (0,) * nd)


def _ffn_in(x, w, *, tm, after=None, name):
    n_tok = x.shape[0]
    sw = w.shape[2]
    tm = min(tm, n_tok)

    after_specs, after_args = _after_operand(after)

    def body(x_ref, w_ref, *rest):
        h_ref, a_ref = rest[-2:]
        xb = x_ref[...].astype(BF16)
        for s in range(2):
            g = jnp.dot(xb, w_ref[s], preferred_element_type=F32)
            u = jnp.dot(xb, w_ref[s + 2], preferred_element_type=F32)
            h_ref[:, s * sw:(s + 1) * sw] = g.astype(BF16)
            h_ref[:, (s + 2) * sw:(s + 3) * sw] = u.astype(BF16)
            a_ref[:, s * sw:(s + 1) * sw] = (_silu(g) * u).astype(BF16)

    return pl.pallas_call(
        body, name=name,
        out_shape=[jax.ShapeDtypeStruct((n_tok, 2 * D_FF), BF16), jax.ShapeDtypeStruct((n_tok, D_FF), BF16)],
        grid=(n_tok // tm,),
        in_specs=[pl.BlockSpec((tm, D_MODEL), lambda i: (i, 0)), _whole(w.shape)] + after_specs,
        out_specs=[pl.BlockSpec((tm, 2 * D_FF), lambda i: (i, 0)), pl.BlockSpec((tm, D_FF), lambda i: (i, 0))],
        compiler_params=_cparams(("parallel",)),
    )(x, w, *after_args)


def _mm_ln(a_list, w, xres, g, b, fscale, *, tm, name):
    n_tok = xres.shape[0]
    tm = min(tm, n_tok)
    na = len(a_list)

    def body(*refs):
        a_refs = refs[:na]
        w_ref, x_ref, g_ref, b_ref, z_o, y_o, yb_o = refs[na:]
        f, off = None, 0
        for a_ref in a_refs:
            k = a_ref.shape[1]
            t = jnp.dot(a_ref[...].astype(BF16), w_ref[off:off + k, :], preferred_element_type=F32)
            f = t if f is None else f + t
            off += k
        z = ALPHA * x_ref[...] + fscale * f
        y = _layer_norm(z, g_ref[...], b_ref[...])
        z_o[...] = z
        y_o[...] = y
        yb_o[...] = y.astype(BF16)

    tile = pl.BlockSpec((tm, D_MODEL), lambda i: (i, 0))
    return pl.pallas_call(
        body, name=name,
        out_shape=[jax.ShapeDtypeStruct((n_tok, D_MODEL), F32)] * 2 + [jax.ShapeDtypeStruct((n_tok, D_MODEL), BF16)],
        grid=(n_tok // tm,),
        in_specs=[pl.BlockSpec((tm, a.shape[1]), lambda i: (i, 0)) for a in a_list]
        + [_whole(w.shape), tile, _whole(g.shape), _whole(b.shape)],
        out_specs=[tile, tile, tile],
        compiler_params=_cparams(("parallel",)),
    )(*a_list, w, xres, g, b)


def _mm_ln_loss(a, w, xres, g, b, target, fscale, *, tm, name):
    n_tok = xres.shape[0]
    tm = min(tm, n_tok)

    def body(a_ref, w_ref, x_ref, g_ref, b_ref, t_ref, dz_o, dg_o, db_o, loss_o):
        i = pl.program_id(0)
        z = ALPHA * x_ref[...] + fscale * jnp.dot(a_ref[...].astype(BF16), w_ref[...], preferred_element_type=F32)
        y, vjp = jax.vjp(_layer_norm, z, g_ref[...], b_ref[...])
        e = y - t_ref[...]
        dz, dg, db = vjp(e * (1.0 / D_MODEL))

        @pl.when(i == 0)
        def _():
            dg_o[...] = jnp.zeros_like(dg_o)
            db_o[...] = jnp.zeros_like(db_o)
            loss_o[...] = jnp.zeros_like(loss_o)
        dz_o[...] = dz
        dg_o[...] += dg
        db_o[...] += db
        loss_o[...] += 0.5 * jnp.sum(jnp.mean(e * e, axis=-1, keepdims=True), axis=0, keepdims=True)

    tile = pl.BlockSpec((tm, D_MODEL), lambda i: (i, 0))
    row = pl.BlockSpec((1, D_MODEL), lambda i: (0, 0))
    return pl.pallas_call(
        body, name=name,
        out_shape=[jax.ShapeDtypeStruct((n_tok, D_MODEL), F32), jax.ShapeDtypeStruct((1, D_MODEL), F32),
                   jax.ShapeDtypeStruct((1, D_MODEL), F32), jax.ShapeDtypeStruct((8, LANES), F32)],
        grid=(n_tok // tm,),
        in_specs=[pl.BlockSpec((tm, a.shape[1]), lambda i: (i, 0)), _whole(w.shape), tile, row, row, tile],
        out_specs=[tile, row, row, pl.BlockSpec((8, LANES), lambda i: (0, 0))],
        compiler_params=_cparams(("arbitrary",)),
    )(a, w, xres, g, b, target)


def _ffn_out_bwd(dz, w, h, *, tm, name):
    n_tok = dz.shape[0]
    tm = min(tm, n_tok)
    cw = D_FF // 2

    def body(dz_ref, w_ref, h_ref, dh_ref):
        dzb = dz_ref[...].astype(BF16)
        for s in range(2):
            dact = 0.5 * lax.dot_general(dzb, w_ref[s * cw:(s + 1) * cw, :], (((1,), (1,)), ((), ())),
                                         preferred_element_type=F32)
            gate = h_ref[:, s * cw:(s + 1) * cw].astype(F32)
            up = h_ref[:, D_FF + s * cw:D_FF + (s + 1) * cw].astype(F32)
            sg = _sigmoid(gate)
            dh_ref[:, s * cw:(s + 1) * cw] = (dact * up * sg * (1.0 + gate * (1.0 - sg))).astype(BF16)
            dh_ref[:, D_FF + s * cw:D_FF + (s + 1) * cw] = (dact * gate * sg).astype(BF16)

    wide = pl.BlockSpec((tm, 2 * D_FF), lambda i: (i, 0))
    return pl.pallas_call(
        body, name=name,
        out_shape=jax.ShapeDtypeStruct((n_tok, 2 * D_FF), BF16),
        grid=(n_tok // tm,),
        in_specs=[pl.BlockSpec((tm, D_MODEL), lambda i: (i, 0)), _whole(w.shape), wide],
        out_specs=wide,
        compiler_params=_cparams(("parallel",)),
    )(dz, w, h)


def _mm_nt_res(a_list, w, dz, *, tm, ln=None, after=None, name):
    n_tok = dz.shape[0]
    tm = min(tm, n_tok)
    na = len(a_list)
    nt = (((1,), (1,)), ((), ()))
    after_specs, after_args = _after_operand(after)
    n_out = 1 if ln is None else 3

    def body(*refs):
        a_refs = refs[:na]
        w_ref, dz_ref, o_ref = refs[na], refs[na + 1], refs[-n_out]
        acc = ALPHA * dz_ref[...]
        if len(w_ref.shape) == 3:
            cw = w_ref.shape[2]
            for s in range(w_ref.shape[0]):
                acc = acc + lax.dot_general(a_refs[0][:, s * cw:(s + 1) * cw], w_ref[s], nt, preferred_element_type=F32)
        else:
            off = 0
            for a_ref in a_refs:
                k = a_ref.shape[1]
                acc = acc + lax.dot_general(a_ref[...], w_ref[:, off:off + k], nt, preferred_element_type=F32)
                off += k
        if ln is None:
            o_ref[...] = acc
            return
        z_ref, g_ref, b_ref = refs[na + 2:na + 5]
        dg_o, db_o = refs[-2:]
        _, vjp = jax.vjp(_layer_norm, z_ref[...], g_ref[...], b_ref[...])
        o_ref[...], dg, db = vjp(acc)

        @pl.when(pl.program_id(0) == 0)
        def _():
            dg_o[...] = jnp.zeros_like(dg_o)
            db_o[...] = jnp.zeros_like(db_o)
        dg_o[...] += dg
        db_o[...] += db

    tile = pl.BlockSpec((tm, D_MODEL), lambda i: (i, 0))
    row = pl.BlockSpec((1, D_MODEL), lambda i: (0, 0))
    out_shape = [jax.ShapeDtypeStruct((n_tok, D_MODEL), F32)]
    ln_specs, ln_args, out_specs = [], [], [tile]
    if ln is not None:
        ln_specs, ln_args = [tile, row, row], list(ln)
        out_shape += [jax.ShapeDtypeStruct((1, D_MODEL), F32)] * 2
        out_specs += [row, row]
    outs = pl.pallas_call(
        body, name=name,
        out_shape=out_shape,
        grid=(n_tok // tm,),
        in_specs=[pl.BlockSpec((tm, a.shape[1]), lambda i: (i, 0)) for a in a_list] + [_whole(w.shape), tile]
        + ln_specs + after_specs,
        out_specs=out_specs,
        compiler_params=_cparams(("parallel",) if ln is None else ("arbitrary",)),
    )(*a_list, w, dz, *ln_args, *after_args)
    return outs[0] if ln is None else outs


def _rowcall(fn, tok_in, full_in, tok_out, acc_out, *, tt, name):
    views = [a if isinstance(a, tuple) else (a, a.shape[1], 0) for a in tok_in]
    tok_in = [a for a, _, _ in views]
    n_tok = tok_in[0].shape[0]
    assert n_tok % tt == 0, (name, n_tok, tt)
    n_ti, n_fi, n_to = len(tok_in), len(full_in), len(tok_out)

    def body(*refs):
        i = pl.program_id(0)
        ins = [r[...] for r in refs[:n_ti + n_fi]]
        outs = fn(i, *ins)
        o_refs = refs[n_ti + n_fi:]
        for r, val in zip(o_refs[:n_to], outs[:n_to]):
            r[...] = val.astype(r.dtype)
        if acc_out:
            @pl.when(i == 0)
            def _():
                for r in o_refs[n_to:]:
                    r[...] = jnp.zeros_like(r)
            for r, val in zip(o_refs[n_to:], outs[n_to:]):
                r[...] += val.reshape(r.shape).astype(F32)

    in_specs = [pl.BlockSpec((tt, width), functools.partial(lambda k, i: (i, k), k)) for _, width, k in views]
    in_specs += [pl.BlockSpec(a.shape, lambda i: (0, 0)) for a in full_in]
    out_specs = [pl.BlockSpec((tt, c), lambda i: (i, 0)) for c, _ in tok_out]
    out_specs += [pl.BlockSpec(s, lambda i: (0, 0)) for s in acc_out]
    out_shape = [jax.ShapeDtypeStruct((n_tok, c), dt) for c, dt in tok_out]
    out_shape += [jax.ShapeDtypeStruct(s, F32) for s in acc_out]
    return pl.pallas_call(
        body, name=name, out_shape=out_shape, grid=(n_tok // tt,), in_specs=in_specs, out_specs=out_specs,
        compiler_params=_cparams(("arbitrary",) if acc_out else ("parallel",)),
    )(*tok_in, *full_in)


@jax.custom_vjp
def _bdot(a, b):
    return jnp.dot(a.astype(BF16), b.astype(BF16), preferred_element_type=F32)


def _bdot_fwd(a, b):
    return _bdot(a, b), (a, b)


def _bdot_bwd(res, g):
    a, b = res
    g16 = g.astype(BF16)
    da = lax.dot_general(g16, b.astype(BF16), (((1,), (1,)), ((), ())), preferred_element_type=F32)
    db = lax.dot_general(a.astype(BF16), g16, (((0,), (0,)), ((), ())), preferred_element_type=F32)
    return da, db


_bdot.defvjp(_bdot_fwd, _bdot_bwd)


def _split16(x):
    hi = x.astype(BF16)
    lo = (x - hi.astype(F32)).astype(BF16)
    return hi, lo


def _segsum_raw(x, e2):
    hi, lo = _split16(x)
    outs = []
    for c in range(x.shape[1] // 256):
        lhs = jnp.concatenate([hi[:, 256 * c:256 * (c + 1)], lo[:, 256 * c:256 * (c + 1)]], axis=1)
        outs.append(jnp.dot(lhs, e2, preferred_element_type=F32))
    return jnp.concatenate(outs, axis=1)


@jax.custom_vjp
def _segsum(x, e2):
    return _segsum_raw(x, e2)


def _segsum_fwd(x, e2):
    return _segsum_raw(x, e2), e2


def _segsum_bwd(e2, g):
    return _segsum_raw(g, e2), jnp.zeros_like(e2)


_segsum.defvjp(_segsum_fwd, _segsum_bwd)


def _head_ones():
    r = lax.broadcasted_iota(jnp.int32, (512, 256), 0) % 256
    c = lax.broadcasted_iota(jnp.int32, (512, 256), 1)
    return (r // HEAD == c // HEAD).astype(BF16)


def _sigmoid(x):
    return 1.0 / (1.0 + jnp.exp(-x))


def _silu(x):
    return x * _sigmoid(x)


def _layer_norm(z, g, b, eps=LN_EPS):
    mu = jnp.mean(z, axis=-1, keepdims=True)
    zc = z - mu
    var = jnp.mean(zc * zc, axis=-1, keepdims=True)
    return zc * lax.rsqrt(var + eps) * g + b


def _prep(ps, w2b, w0c, a2b, a0c, g2p, k_k, k_a, e2):
    r, k, v = ps[:, 0:512], ps[:, 512:1024], ps[:, 1024:1536]
    wd, ad, gd = ps[:, 1536:1664], ps[:, 1664:1792], ps[:, 1792:2048]
    lw = _bdot(jnp.tanh(wd), w2b) + w0c
    decay = -DECAY_SCALE * _sigmoid(lw)
    a = _sigmoid(_bdot(ad, a2b) + a0c)
    g = _bdot(_sigmoid(gd), g2p)
    kkr = k * k_k
    nrm = jnp.sqrt(_segsum(kkr * kkr, e2))
    kk = kkr / jnp.maximum(nrm, NORM_EPS)
    k2 = jnp.concatenate([k, k], axis=1)
    ka2 = jnp.concatenate([k_a, k_a], axis=1)
    kd = k2 * (1.0 + (a - 1.0) * ka2)
    b = jnp.concatenate([kk, kk], axis=1) * a
    return r, v, kk, decay, kd, b, g


def _post(y0, y1, r, v, kd, g, lnx_g, lnx_b, r_k, e2):
    y = y0 + y1
    mu = _segsum(y, e2) * (1.0 / HEAD)
    yc = y - mu
    var = _segsum(yc * yc, e2) * (1.0 / HEAD)
    yn = yc * lax.rsqrt(var + GN_EPS) * lnx_g + lnx_b
    bonus = _segsum(r * (kd[:, :RW] + kd[:, RW:]) * r_k, e2)
    return (yn + bonus * v) * g


def _conv_post(yc, ln_g, ln_b):
    return _silu(_layer_norm(yc, ln_g, ln_b))


def _halo_specs(cols_block, hb, tt, n_tok, col_idx):
    nb = n_tok // hb
    prev = pl.BlockSpec((hb, cols_block), lambda i: (jnp.maximum(i * (tt // hb) - 1, 0), col_idx))
    nxt = pl.BlockSpec((hb, cols_block), lambda i: (jnp.minimum((i + 1) * (tt // hb), nb - 1), col_idx))
    return prev, nxt


def _mix_prep(p, mu_p, mu_n, w2b, w0c, a2b, a0c, g2p, k_k, k_a, *, seq, tt, name):
    n_tok = p.shape[0]
    tps = seq // tt
    e2 = _head_ones()

    def body(p_ref, hp_ref, hn_ref, mup_ref, mun_ref, w2b_ref, w0c_ref, a2b_ref, a0c_ref, g2p_ref, kk_ref, ka_ref,
             e2_ref, r_o, v_o, kk_o, w_o, kd_o, b_o, g_o, ext):
        i = pl.program_id(0)
        first = (i % tps) == 0
        last = (i % tps) == tps - 1
        pv = p_ref[...]
        ext[pl.ds(0, 8), :] = jnp.where(first, 0.0, hp_ref[...])
        ext[pl.ds(8, tt), :] = pv
        ext[pl.ds(8 + tt, 8), :] = jnp.where(last, 0.0, hn_ref[...])
        prev = ext[pl.ds(7, tt), :]
        nxt = ext[pl.ds(9, tt), :]
        ps = pv + mup_ref[...] * (prev - pv) + mun_ref[...] * (nxt - pv)
        outs = _prep(ps, w2b_ref[...], w0c_ref[...], a2b_ref[...], a0c_ref[...], g2p_ref[...], kk_ref[...],
                     ka_ref[...], e2_ref[...])
        for o_ref, val in zip((r_o, v_o, kk_o, w_o, kd_o, b_o, g_o), outs):
            o_ref[...] = val

    hp, hn = _halo_specs(SHIFT_PAD, 8, tt, n_tok, 0)
    fulls = [mu_p, mu_n, w2b, w0c, a2b, a0c, g2p, k_k, k_a, e2]
    widths = (RW, RW, RW, 2 * RW, 2 * RW, 2 * RW, RW)
    return pl.pallas_call(
        body, name=name,
        out_shape=[jax.ShapeDtypeStruct((n_tok, c), F32) for c in widths],
        grid=(n_tok // tt,),
        in_specs=[pl.BlockSpec((tt, SHIFT_PAD), lambda i: (i, 0)), hp, hn]
        + [pl.BlockSpec(a.shape, lambda i: (0, 0)) for a in fulls],
        out_specs=[pl.BlockSpec((tt, c), lambda i: (i, 0)) for c in widths],
        scratch_shapes=[pltpu.VMEM((tt + 16, SHIFT_PAD), F32)],
        compiler_params=_cparams(("parallel",)),
    )(p, p, p, *fulls)


def _mix_prep_bwd(p, mu_p, mu_n, w2b, w0c, a2b, a0c, g2p, k_k, k_a, ct_terms, *, seq, tt, name):
    n_tok = p.shape[0]
    tps = seq // tt
    e2 = _head_ones()
    acc_shapes = [w2b.shape, w0c.shape, a2b.shape, a0c.shape, g2p.shape, k_k.shape, k_a.shape]
    cts = [a for terms in ct_terms for t in terms for a in (t if isinstance(t, tuple) else (t,))]

    def body(p_ref, hp_ref, hn_ref, mup_ref, mun_ref, w2b_ref, w0c_ref, a2b_ref, a0c_ref, g2p_ref, kk_ref, ka_ref,
             e2_ref, *rest):
        ct_refs, dps_o, acc_refs, ext = rest[:len(cts)], rest[len(cts)], rest[len(cts) + 1:-1], rest[-1]
        ct_it = iter(ct_refs)
        ct_vals = []
        for terms in ct_terms:
            total = None
            for t in terms:
                if isinstance(t, tuple):
                    val = jnp.concatenate([next(ct_it)[...] for _ in t], axis=1)
                else:
                    val = next(ct_it)[...]
                total = val if total is None else total + val
            ct_vals.append(total)
        i = pl.program_id(0)
        first = (i % tps) == 0
        last = (i % tps) == tps - 1
        pv = p_ref[...]
        ext[pl.ds(0, 8), :] = jnp.where(first, 0.0, hp_ref[...])
        ext[pl.ds(8, tt), :] = pv
        ext[pl.ds(8 + tt, 8), :] = jnp.where(last, 0.0, hn_ref[...])
        prev = ext[pl.ds(7, tt), :]
        nxt = ext[pl.ds(9, tt), :]
        ps = pv + mup_ref[...] * (prev - pv) + mun_ref[...] * (nxt - pv)
        e2v = e2_ref[...]
        _, vjp = jax.vjp(lambda *a: _prep(*a, e2v), ps, w2b_ref[...], w0c_ref[...], a2b_ref[...], a0c_ref[...],
                         g2p_ref[...], kk_ref[...], ka_ref[...])
        grads = vjp(tuple(ct_vals))
        dps_o[...] = grads[0]

        @pl.when(i == 0)
        def _():
            for r in acc_refs:
                r[...] = jnp.zeros_like(r)
        for r, val in zip(acc_refs, grads[1:]):
            r[...] += val

    hp, hn = _halo_specs(SHIFT_PAD, 8, tt, n_tok, 0)
    fulls = [mu_p, mu_n, w2b, w0c, a2b, a0c, g2p, k_k, k_a, e2]
    return pl.pallas_call(
        body, name=name,
        out_shape=[jax.ShapeDtypeStruct((n_tok, SHIFT_PAD), F32)] + [jax.ShapeDtypeStruct(s, F32) for s in acc_shapes],
        grid=(n_tok // tt,),
        in_specs=[pl.BlockSpec((tt, SHIFT_PAD), lambda i: (i, 0)), hp, hn]
        + [pl.BlockSpec(a.shape, lambda i: (0, 0)) for a in fulls]
        + [pl.BlockSpec((tt, c.shape[1]), lambda i: (i, 0)) for c in cts],
        out_specs=[pl.BlockSpec((tt, SHIFT_PAD), lambda i: (i, 0))] + [pl.BlockSpec(s, lambda i: (0, 0)) for s in acc_shapes],
        scratch_shapes=[pltpu.VMEM((tt + 16, SHIFT_PAD), F32)],
        compiler_params=_cparams(("arbitrary",)),
    )(p, p, p, *fulls, *cts)


def _shift_bwd(dps, p, mu_p, mu_n, *, seq, tt, name):
    n_tok = p.shape[0]
    tps = seq // tt

    def body(d_ref, dhp_ref, dhn_ref, p_ref, php_ref, phn_ref, mup_ref, mun_ref, dp_o, dmup_o, dmun_o, ext):
        i = pl.program_id(0)
        first = (i % tps) == 0
        last = (i % tps) == tps - 1
        mup, mun = mup_ref[...], mun_ref[...]
        dv = d_ref[...]
        pv = p_ref[...]
        ext[pl.ds(0, 8), :] = jnp.where(first, 0.0, dhp_ref[...])
        ext[pl.ds(8, tt), :] = dv
        ext[pl.ds(8 + tt, 8), :] = jnp.where(last, 0.0, dhn_ref[...])
        d_prev = ext[pl.ds(7, tt), :]
        d_next = ext[pl.ds(9, tt), :]
        dp_o[...] = (dv * (1.0 - mup - mun) + d_next * mup + d_prev * mun).astype(dp_o.dtype)
        ext[pl.ds(0, 8), :] = jnp.where(first, 0.0, php_ref[...])
        ext[pl.ds(8, tt), :] = pv
        ext[pl.ds(8 + tt, 8), :] = jnp.where(last, 0.0, phn_ref[...])
        p_prev = ext[pl.ds(7, tt), :]
        p_next = ext[pl.ds(9, tt), :]

        @pl.when(i == 0)
        def _():
            dmup_o[...] = jnp.zeros_like(dmup_o)
            dmun_o[...] = jnp.zeros_like(dmun_o)
        dmup_o[...] += jnp.sum(dv * (p_prev - pv), axis=0, keepdims=True)
        dmun_o[...] += jnp.sum(dv * (p_next - pv), axis=0, keepdims=True)

    hp, hn = _halo_specs(SHIFT_PAD, 8, tt, n_tok, 0)
    tile = pl.BlockSpec((tt, SHIFT_PAD), lambda i: (i, 0))
    full = pl.BlockSpec((1, SHIFT_PAD), lambda i: (0, 0))
    return pl.pallas_call(
        body, name=name,
        out_shape=[jax.ShapeDtypeStruct((n_tok, SHIFT_PAD), BF16), jax.ShapeDtypeStruct((1, SHIFT_PAD), F32),
                   jax.ShapeDtypeStruct((1, SHIFT_PAD), F32)],
        grid=(n_tok // tt,),
        in_specs=[tile, hp, hn, tile, hp, hn, full, full],
        out_specs=[tile, full, full],
        scratch_shapes=[pltpu.VMEM((tt + 16, SHIFT_PAD), F32)],
        compiler_params=_cparams(("arbitrary",)),
    )(dps, dps, dps, p, p, p, mu_p, mu_n)


def _mix_post(y0, y1, r, v, kd, g, lnx_g, lnx_b, r_k, *, tt, name):
    e2 = _head_ones()
    return _rowcall(lambda i, *a: (_post(*a),), [y0, y1, r, v, kd, g], [lnx_g, lnx_b, r_k, e2], [(RW, BF16)], [],
                    tt=tt, name=name)[0]


def _mix_post_bwd(y0, y1, r, v, kd, g, lnx_g, lnx_b, r_k, dout, *, tt, name):
    e2 = _head_ones()

    def fn(i, y0v, y1v, rv, vv, kdv, gv, dov, lg, lb, rk, e2v):
        _, vjp = jax.vjp(lambda *a: _post(*a, e2v), y0v, y1v, rv, vv, kdv, gv, lg, lb, rk)
        gr = vjp(dov.astype(F32))
        return gr[0], gr[2], gr[3], gr[4], gr[5], gr[6], gr[7], gr[8]
    return _rowcall(fn, [y0, y1, r, v, kd, g, dout], [lnx_g, lnx_b, r_k, e2],
                    [(RW, F32), (RW, F32), (RW, F32), (2 * RW, F32), (RW, F32)], [(1, RW), (1, RW), (1, RW)],
                    tt=tt, name=name)


def _conv_fwd(p, dw, db, ln_g, ln_b, *, seq, tt, name):
    n_tok = p.shape[0]
    tps = seq // tt

    def glu(x, gate):
        return x * _sigmoid(gate)

    def body(u_ref, g_ref, uhp, ghp, uhn, ghn, dw_ref, db_ref, lg_ref, lb_ref, yc_o, y_o, ext):
        i = pl.program_id(0)
        first = (i % tps) == 0
        last = (i % tps) == tps - 1
        ext[pl.ds(0, 16), :] = jnp.where(first, 0.0, glu(uhp[...], ghp[...]))
        ext[pl.ds(16, tt), :] = glu(u_ref[...], g_ref[...])
        ext[pl.ds(16 + tt, 16), :] = jnp.where(last, 0.0, glu(uhn[...], ghn[...]))
        acc = jnp.zeros((tt, CW), F32) + db_ref[...]
        for k in range(CONV_K):
            acc = acc + ext[pl.ds(k + 1, tt), :] * dw_ref[pl.ds(k, 1), :]
        yc_o[...] = acc
        y_o[...] = _conv_post(acc, lg_ref[...], lb_ref[...]).astype(y_o.dtype)

    uhp_s, uhn_s = _halo_specs(CW, 16, tt, n_tok, 4)
    ghp_s, ghn_s = _halo_specs(CW, 16, tt, n_tok, 5)
    fulls = [dw, db, ln_g, ln_b]
    return pl.pallas_call(
        body, name=name,
        out_shape=[jax.ShapeDtypeStruct((n_tok, CW), F32), jax.ShapeDtypeStruct((n_tok, CW), BF16)],
        grid=(n_tok // tt,),
        in_specs=[pl.BlockSpec((tt, CW), lambda i: (i, 4)), pl.BlockSpec((tt, CW), lambda i: (i, 5)),
                  uhp_s, ghp_s, uhn_s, ghn_s] + [pl.BlockSpec(a.shape, lambda i: (0, 0)) for a in fulls],
        out_specs=[pl.BlockSpec((tt, CW), lambda i: (i, 0)), pl.BlockSpec((tt, CW), lambda i: (i, 0))],
        scratch_shapes=[pltpu.VMEM((tt + 32, CW), F32)],
        compiler_params=_cparams(("parallel",)),
    )(p, p, p, p, p, p, *fulls)


def _conv_post_bwd(yc, dy, ln_g, ln_b, *, tt, name):
    def fn(i, ycv, dyv, lg, lb):
        _, vjp = jax.vjp(_conv_post, ycv, lg, lb)
        dyc, dg, dbb = vjp(dyv.astype(F32))
        return dyc, dg, dbb, jnp.sum(dyc, axis=0, keepdims=True)
    return _rowcall(fn, [yc, dy], [ln_g, ln_b], [(CW, F32)], [(1, CW), (1, CW), (1, CW)], tt=tt, name=name)


def _conv_bwd(dyc, p, dw, *, seq, tt, name):
    n_tok = p.shape[0]
    tps = seq // tt

    def body(d_ref, dhp, dhn, u_ref, g_ref, uhp, ghp, uhn, ghn, dw_ref, dp_o, ddw_o, ext):
        i = pl.program_id(0)
        first = (i % tps) == 0
        last = (i % tps) == tps - 1
        dv = d_ref[...]
        ext[pl.ds(0, 16), :] = jnp.where(first, 0.0, dhp[...])
        ext[pl.ds(16, tt), :] = dv
        ext[pl.ds(16 + tt, 16), :] = jnp.where(last, 0.0, dhn[...])
        du = jnp.zeros((tt, CW), F32)
        for k in range(CONV_K):
            du = du + ext[pl.ds(31 - k, tt), :] * dw_ref[pl.ds(k, 1), :]
        uv, gv = u_ref[...], g_ref[...]
        sg = _sigmoid(gv)
        dp_o[:, 0:CW] = (du * sg).astype(dp_o.dtype)
        dp_o[:, CW:2 * CW] = (du * uv * sg * (1.0 - sg)).astype(dp_o.dtype)
        ext[pl.ds(0, 16), :] = jnp.where(first, 0.0, uhp[...] * _sigmoid(ghp[...]))
        ext[pl.ds(16, tt), :] = uv * sg
        ext[pl.ds(16 + tt, 16), :] = jnp.where(last, 0.0, uhn[...] * _sigmoid(ghn[...]))

        @pl.when(i == 0)
        def _():
            ddw_o[...] = jnp.zeros_like(ddw_o)
        for k in range(CONV_K):
            ddw_o[pl.ds(k, 1), :] += jnp.sum(dv * ext[pl.ds(k + 1, tt), :], axis=0, keepdims=True)

    dhp_s, dhn_s = _halo_specs(CW, 16, tt, n_tok, 0)
    uhp_s, uhn_s = _halo_specs(CW, 16, tt, n_tok, 4)
    ghp_s, ghn_s = _halo_specs(CW, 16, tt, n_tok, 5)
    return pl.pallas_call(
        body, name=name,
        out_shape=[jax.ShapeDtypeStruct((n_tok, 2 * CW), BF16), jax.ShapeDtypeStruct((32, CW), F32)],
        grid=(n_tok // tt,),
        in_specs=[pl.BlockSpec((tt, CW), lambda i: (i, 0)), dhp_s, dhn_s,
                  pl.BlockSpec((tt, CW), lambda i: (i, 4)), pl.BlockSpec((tt, CW), lambda i: (i, 5)),
                  uhp_s, ghp_s, uhn_s, ghn_s, pl.BlockSpec(dw.shape, lambda i: (0, 0))],
        out_specs=[pl.BlockSpec((tt, 2 * CW), lambda i: (i, 0)), pl.BlockSpec((32, CW), lambda i: (0, 0))],
        scratch_shapes=[pltpu.VMEM((tt + 32, CW), F32)],
        compiler_params=_cparams(("arbitrary",)),
    )(dyc, dyc, dyc, p, p, p, p, p, p, dw)


def _segdot(hi, lo, e2):
    outs = []
    for c in range(hi.shape[1] // 256):
        lhs = jnp.concatenate([hi[:, 256 * c:256 * (c + 1)], lo[:, 256 * c:256 * (c + 1)]], axis=1)
        outs.append(jnp.dot(lhs, e2, preferred_element_type=F32))
    return jnp.concatenate(outs, axis=1)


SCAN_PASSES = 1


def _seg_streams(parts, e2):
    if SCAN_PASSES == 1:
        hi = jnp.concatenate([p.astype(BF16) for p in parts], axis=0)
        full = jnp.concatenate([jnp.dot(hi[:, 256 * c:256 * (c + 1)], e2[:256], preferred_element_type=F32)
                                for c in range(RW // 256)], axis=1)
    else:
        pieces = [_split16(p) for p in parts]
        full = _segdot(jnp.concatenate([h for h, _ in pieces], axis=0), jnp.concatenate([l for _, l in pieces], axis=0), e2)
    return [full[s * HEAD:(s + 1) * HEAD] for s in range(len(parts))]


def _diag_mask():
    return lax.broadcasted_iota(jnp.int32, (HEAD, RW), 0) == lax.broadcasted_iota(jnp.int32, (HEAD, RW), 1) % HEAD


def _col_form(rows, dmask, e2):
    his, los = [], []
    for x in rows:
        hi = x.astype(BF16).astype(F32)
        lo = x - hi
        his.append(jnp.where(dmask, jnp.broadcast_to(hi, (HEAD, RW)), 0.0).astype(BF16))
        los.append(jnp.where(dmask, jnp.broadcast_to(lo, (HEAD, RW)), 0.0).astype(BF16))
    full = _segdot(jnp.concatenate(his, axis=0), jnp.concatenate(los, axis=0), e2)
    return [full[s * HEAD:(s + 1) * HEAD] for s in range(len(rows))]


def _row_form(col, dmask):
    return jnp.sum(jnp.where(dmask, col, 0.0), axis=0, keepdims=True)


def _row_sum(x):
    return jnp.sum(x, axis=0, keepdims=True)


def _wkv_fwd(r, v, kk, w, kd, b, *, tb, name):
    bsz, seq, _ = r.shape
    nb = seq // tb
    ns = 2 * bsz
    e2 = _head_ones()

    def body(r0, r1, v0, v1, k0, k1, w0, w1, kd0, kd1, b0, b1, e2_ref, y0_o, y1_o, sp_o, last_o, s_ref):
        i = pl.program_id(0)

        @pl.when(i == 0)
        def _():
            s_ref[...] = jnp.zeros_like(s_ref)

        e2v = e2_ref[...]
        dmask = _diag_mask()
        y_refs = (y0_o, y1_o)

        def step(j, carry):
            tl = (j, tb - 1 - j)

            def rows(refs):
                return [refs[d][bb, pl.ds(tl[d], 1), :] for d in (0, 1) for bb in range(bsz)]

            kk_r, w_r, b_r, kd_r, r_r = rows((k0, k1)), rows((w0, w1)), rows((b0, b1)), rows((kd0, kd1)), rows((r0, r1))
            s_old = [s_ref[s * HEAD:(s + 1) * HEAD, :] for s in range(ns)]
            for s in range(ns):
                sp_o[s, pl.ds(j, 1), :, :] = s_old[s].reshape(1, HEAD, RW)
            sa = _seg_streams([s_old[s] * kk_r[s] for s in range(ns)], e2v)
            vc = _col_form(rows((v0, v1)), dmask, e2v)
            s_new = [s_old[s] * w_r[s] - sa[s] * b_r[s] + vc[s] * kd_r[s] for s in range(ns)]
            for s in range(ns):
                s_ref[s * HEAD:(s + 1) * HEAD, :] = s_new[s]
            ycol = _seg_streams([s_new[s] * r_r[s] for s in range(ns)], e2v)
            for d in (0, 1):
                for bb in range(bsz):
                    y_refs[d][bb, pl.ds(tl[d], 1), :] = _row_form(ycol[d * bsz + bb], dmask)
            return carry

        lax.fori_loop(0, tb, step, 0)
        last_o[...] = s_ref[...]

    def blk(width_idx, rev):
        if rev:
            return pl.BlockSpec((bsz, tb, RW), lambda i: (0, nb - 1 - i, width_idx))
        return pl.BlockSpec((bsz, tb, RW), lambda i: (0, i, width_idx))

    in_specs = [blk(0, False), blk(0, True)] * 3 + [blk(0, False), blk(1, True)] * 3
    in_specs.append(pl.BlockSpec(e2.shape, lambda i: (0, 0)))
    return pl.pallas_call(
        body, name=name,
        out_shape=[jax.ShapeDtypeStruct((bsz, seq, RW), F32), jax.ShapeDtypeStruct((bsz, seq, RW), F32),
                   jax.ShapeDtypeStruct((ns, seq, HEAD, RW), F32), jax.ShapeDtypeStruct((ns * HEAD, RW), F32)],
        grid=(nb,),
        in_specs=in_specs,
        out_specs=[blk(0, False), blk(0, True), pl.BlockSpec((ns, tb, HEAD, RW), lambda i: (0, i, 0, 0)),
                   pl.BlockSpec((ns * HEAD, RW), lambda i: (0, 0))],
        scratch_shapes=[pltpu.VMEM((ns * HEAD, RW), F32)],
        compiler_params=_cparams(("arbitrary",)),
    )(r, r, v, v, kk, kk, w, w, kd, kd, b, b, e2)


def _wkv_bwd(r, v, kk, w, kd, b, dy, sp, s_last, *, tb, name):
    bsz, seq, _ = r.shape
    nb = seq // tb
    ns = 2 * bsz
    e2 = _head_ones()

    def body(r0, r1, v0, v1, k0, k1, dy0, dy1, w0, w1, kd0, kd1, b0, b1, sp_ref, last_ref, e2_ref, *rest):
        outs, g_ref, post_ref = rest[:-2], rest[-2], rest[-1]
        i = pl.program_id(0)

        @pl.when(i == 0)
        def _():
            g_ref[...] = jnp.zeros_like(g_ref)
            post_ref[...] = last_ref[...]

        e2v = e2_ref[...]
        dmask = _diag_mask()

        def step(jj, carry):
            sl = tb - 1 - jj
            tl = (sl, jj)

            def rows(refs):
                return [refs[d][bb, pl.ds(tl[d], 1), :] for d in (0, 1) for bb in range(bsz)]

            kk_r, w_r, b_r, kd_r, r_r = rows((k0, k1)), rows((w0, w1)), rows((b0, b1)), rows((kd0, kd1)), rows((r0, r1))
            s_old = [sp_ref[s, pl.ds(sl, 1), :, :].reshape(HEAD, RW) for s in range(ns)]
            sa = _seg_streams([s_old[s] * kk_r[s] for s in range(ns)], e2v)
            vc = _col_form(rows((v0, v1)), dmask, e2v)
            dyc = _col_form(rows((dy0, dy1)), dmask, e2v)
            gt = [g_ref[s * HEAD:(s + 1) * HEAD, :] + dyc[s] * r_r[s] for s in range(ns)]
            both = _seg_streams([gt[s] * b_r[s] for s in range(ns)] + [gt[s] * kd_r[s] for s in range(ns)], e2v)
            gb, dvc = both[:ns], both[ns:]
            for d in (0, 1):
                for bb in range(bsz):
                    s = d * bsz + bb
                    at = (bb, pl.ds(tl[d], 1), slice(None))
                    outs[0 + d][at] = _row_sum(post_ref[s * HEAD:(s + 1) * HEAD, :] * dyc[s])
                    post_ref[s * HEAD:(s + 1) * HEAD, :] = s_old[s]
                    outs[2 + d][at] = _row_form(dvc[s], dmask)
                    outs[4 + d][at] = -_row_sum(s_old[s] * gb[s])
                    outs[6 + d][at] = _row_sum(s_old[s] * gt[s])
                    outs[8 + d][at] = _row_sum(gt[s] * vc[s])
                    outs[10 + d][at] = -_row_sum(sa[s] * gt[s])
                    g_ref[s * HEAD:(s + 1) * HEAD, :] = gt[s] * w_r[s] - gb[s] * kk_r[s]
            return carry

        lax.fori_loop(0, tb, step, 0)

    def blk(width_idx, rev):
        if rev:
            return pl.BlockSpec((bsz, tb, RW), lambda i: (0, nb - 1 - i, width_idx))
        return pl.BlockSpec((bsz, tb, RW), lambda i: (0, i, width_idx))

    in_specs = [blk(0, True), blk(0, False)] * 4 + [blk(0, True), blk(1, False)] * 3
    in_specs.append(pl.BlockSpec((ns, tb, HEAD, RW), lambda i: (0, nb - 1 - i, 0, 0)))
    in_specs.append(pl.BlockSpec((ns * HEAD, RW), lambda i: (0, 0)))
    in_specs.append(pl.BlockSpec(e2.shape, lambda i: (0, 0)))
    return pl.pallas_call(
        body, name=name,
        out_shape=[jax.ShapeDtypeStruct((bsz, seq, RW), F32)] * 12,
        grid=(nb,),
        in_specs=in_specs,
        out_specs=[blk(0, True), blk(0, False)] * 6,
        scratch_shapes=[pltpu.VMEM((ns * HEAD, RW), F32), pltpu.VMEM((ns * HEAD, RW), F32)],
        compiler_params=_cparams(("arbitrary",)),
    )(r, r, v, v, kk, kk, dy, dy, w, w, kd, kd, b, b, sp, s_last, e2)


CHUNK = 64
SCAN_FINE = True
_MM_DIMS = {"nn": (((2,), (1,)), ((0,), (0,))), "nt": (((2,), (2,)), ((0,), (0,))), "tn": (((1,), (1,)), ((0,), (0,)))}


def _mm16_raw(a, b, mode, fine):
    dot = lambda x, y: lax.dot_general(x, y, _MM_DIMS[mode], preferred_element_type=F32)
    if not fine:
        return dot(a.astype(BF16), b.astype(BF16))
    (ah, al), (bh, bl) = _split16(a), _split16(b)
    return dot(ah, bh) + dot(ah, bl) + dot(al, bh)


@functools.partial(jax.custom_vjp, nondiff_argnums=(2, 3))
def _mm16(a, b, mode, fine=False):
    return _mm16_raw(a, b, mode, fine)


def _mm16_fwd(a, b, mode, fine):
    return _mm16_raw(a, b, mode, fine), (a, b)


def _mm16_bwd(mode, fine, res, g):
    a, b = res
    if mode == "nn":
        return _mm16_raw(g, b, "nt", fine), _mm16_raw(a, g, "tn", fine)
    if mode == "nt":
        return _mm16_raw(g, b, "nn", fine), _mm16_raw(g, a, "tn", fine)
    return _mm16_raw(b, g, "nt", fine), _mm16_raw(a, g, "nn", fine)


_mm16.defvjp(_mm16_fwd, _mm16_bwd)


def _tri_sum_raw(x, tri, mode):
    hi = x.astype(BF16)
    r1 = x - hi.astype(F32)
    mid = r1.astype(BF16)
    lo = (r1 - mid.astype(F32)).astype(BF16)
    dot = lambda p: lax.dot_general(tri, p, _MM_DIMS[mode], preferred_element_type=F32)
    return dot(hi) + dot(mid) + dot(lo)


@jax.custom_vjp
def _tri_sum(x, tri):
    return _tri_sum_raw(x, tri, "nn")


def _tri_sum_fwd(x, tri):
    return _tri_sum_raw(x, tri, "nn"), tri


def _tri_sum_bwd(tri, g):
    return _tri_sum_raw(g, tri, "tn"), jnp.zeros_like(tri)


_tri_sum.defvjp(_tri_sum_fwd, _tri_sum_bwd)


def _chunk_step(s0, r, lw, k, v, kk, b, tri, rev):
    nh, n, _ = r.shape
    row = lax.broadcasted_iota(jnp.int32, (nh, n, n), 1)
    col = lax.broadcasted_iota(jnp.int32, (nh, n, n), 2)
    if rev:
        row, col = col, row
    cum = _tri_sum(lw, tri)
    up, down = jnp.exp(cum), jnp.exp(-cum)
    at, rt = -kk * jnp.exp(cum - lw), r * up
    kt, bt = k * down, b * down
    fine = SCAN_FINE
    a_ab = jnp.where(col < row, _mm16(at, bt, "nt", fine), 0.0)
    a_ak = jnp.where(col < row, _mm16(at, kt, "nt", fine), 0.0)
    a_rb = jnp.where(col <= row, _mm16(rt, bt, "nt", fine), 0.0)
    a_rk = jnp.where(col <= row, _mm16(rt, kt, "nt", fine), 0.0)
    u = _mm16(at, s0, "nt", fine) + _mm16(a_ak, v, "nn")
    power = a_ab
    steps = n.bit_length() - 1
    for it in range(steps):
        u = u + _mm16(power, u, "nn")
        if it + 1 < steps:
            power = _mm16(power, power, "nn")
    y = _mm16(rt, s0, "nt", fine) + _mm16(a_rk, v, "nn") + _mm16(a_rb, u, "nn")
    grown = s0 + _mm16(v, kt, "tn", fine) + _mm16(u, bt, "tn", fine)
    return y, grown * jnp.exp(jnp.sum(lw, axis=1, keepdims=True))


N_HEADS = RW // HEAD


def _tri_ones(rev):
    shape = (N_HEADS, CHUNK, CHUNK)
    row, col = lax.broadcasted_iota(jnp.int32, shape, 1), lax.broadcasted_iota(jnp.int32, shape, 2)
    return ((col >= row) if rev else (col <= row)).astype(BF16)


def _split_heads(ref):
    return jnp.stack([ref[0, :, pl.ds(h * HEAD, HEAD)] for h in range(N_HEADS)])


def _merge_heads(ref, val):
    for h in range(N_HEADS):
        ref[0, :, pl.ds(h * HEAD, HEAD)] = val[h]


def _chunk_specs(bsz, nc, rev, dcol):
    chunk = (lambda c: nc - 1 - c) if rev else (lambda c: c)
    shared = pl.BlockSpec((1, CHUNK, RW), lambda s, c: (s, chunk(c), 0))
    own = pl.BlockSpec((1, CHUNK, RW), lambda s, c: (s, chunk(c), dcol))
    return shared, own, chunk


def _wkv_chunk_fwd(r, lw, k, v, kk, b, *, rev, name):
    bsz, seq, _ = r.shape
    nc = seq // CHUNK
    shared, own, _ = _chunk_specs(bsz, nc, rev, int(rev))

    def body(r_ref, lw_ref, k_ref, v_ref, kk_ref, b_ref, tri_ref, y_o, s0_o, s_ref):
        @pl.when(pl.program_id(1) == 0)
        def _():
            s_ref[...] = jnp.zeros_like(s_ref)

        s0 = s_ref[...]
        s0_o[0, 0] = s0
        y, s_ref[...] = _chunk_step(s0, *[_split_heads(x) for x in (r_ref, lw_ref, k_ref, v_ref, kk_ref, b_ref)],
                                    tri_ref[...], rev)
        _merge_heads(y_o, y)

    return pl.pallas_call(
        body, name=name,
        out_shape=[jax.ShapeDtypeStruct((bsz, seq, RW), F32), jax.ShapeDtypeStruct((bsz, nc, N_HEADS, HEAD, HEAD), F32)],
        grid=(bsz, nc),
        in_specs=[shared, own, own, shared, shared, own, pl.BlockSpec((N_HEADS, CHUNK, CHUNK), lambda s, c: (0, 0, 0))],
        out_specs=[shared, pl.BlockSpec((1, 1, N_HEADS, HEAD, HEAD), lambda s, c: (s, c, 0, 0, 0))],
        scratch_shapes=[pltpu.VMEM((N_HEADS, HEAD, HEAD), F32)],
        compiler_params=_cparams(("parallel", "arbitrary")),
    )(r, lw, k, v, kk, b, _tri_ones(rev))


def _wkv_chunk_bwd(r, lw, k, v, kk, b, dy, s0, *, rev, name):
    bsz, seq, _ = r.shape
    nc = seq // CHUNK
    shared, own, _ = _chunk_specs(bsz, nc, not rev, int(rev))

    def body(r_ref, lw_ref, k_ref, v_ref, kk_ref, b_ref, dy_ref, s0_ref, tri_ref, *rest):
        outs, ds_ref = rest[:-1], rest[-1]

        @pl.when(pl.program_id(1) == 0)
        def _():
            ds_ref[...] = jnp.zeros_like(ds_ref)

        triv = tri_ref[...]
        _, vjp = jax.vjp(lambda *a: _chunk_step(*a, triv, rev), s0_ref[0, 0],
                         *[_split_heads(x) for x in (r_ref, lw_ref, k_ref, v_ref, kk_ref, b_ref)])
        grads = vjp((_split_heads(dy_ref), ds_ref[...]))
        ds_ref[...] = grads[0]
        for o, gval in zip(outs, grads[1:]):
            _merge_heads(o, gval)

    return pl.pallas_call(
        body, name=name,
        out_shape=[jax.ShapeDtypeStruct((bsz, seq, RW), F32)] * 6,
        grid=(bsz, nc),
        in_specs=[shared, own, own, shared, shared, own, shared,
                  pl.BlockSpec((1, 1, N_HEADS, HEAD, HEAD), lambda s, c: (s, nc - 1 - c, 0, 0, 0)),
                  pl.BlockSpec((N_HEADS, CHUNK, CHUNK), lambda s, c: (0, 0, 0))],
        out_specs=[shared] * 6,
        scratch_shapes=[pltpu.VMEM((N_HEADS, HEAD, HEAD), F32)],
        compiler_params=_cparams(("parallel", "arbitrary")),
    )(r, lw, k, v, kk, b, dy, s0, _tri_ones(rev))


def _block_diag2(w):
    z = jnp.zeros_like(w[0])
    return jnp.concatenate([jnp.concatenate([w[0], z], axis=1), jnp.concatenate([z, w[1]], axis=1)], axis=0)


def _pad_in_cols(a):
    z = jnp.zeros(a.shape[:-1] + (SHIFT_PAD - SHIFT_COLS,), a.dtype)
    return jnp.concatenate([a[..., :SHIFT_COLS], z, a[..., SHIFT_COLS:]], axis=-1)


def _follow(small, token):
    return small if token is None else small + token[0:1, 0:1]


def _local_step(x, target, wts, *, tt, tb, start_token=None, more_weights=None, grads_ready=None):
    bsz, seq, _ = x.shape
    n_tok = bsz * seq
    row = lambda a: a.reshape(1, -1).astype(F32)
    x0 = x.reshape(n_tok, D_MODEL)
    tgt = target.reshape(n_tok, D_MODEL)
    ln = {k: row(wts[k]) for k in ("ln1_g", "ln1_b", "ln2_g", "ln2_b", "ln3_g", "ln3_b")}
    if grads_ready is None:
        grads_ready = lambda names, slabs: None

    w1i, w1o = wts["ffn1_w_in"], wts["ffn1_w_out"]
    h1, act1 = _ffn_in(x0, w1i, tm=TM_FFN, after=start_token, name="ffn1_in")
    z1, x1, x1b = _mm_ln([act1], w1o, x0, ln["ln1_g"], ln["ln1_b"], 0.5, tm=TM_LN, name="ffn1_out_ln1")
    if more_weights is not None:
        wts = {**wts, **more_weights("mix", x1b)}
    win = _pad_in_cols(wts["w_in"])
    zpad = jnp.zeros((1, SHIFT_PAD - SHIFT_COLS), F32)
    mu_p = jnp.concatenate([row(wts["mu_prev"]), zpad], axis=1)
    mu_n = jnp.concatenate([row(wts["mu_next"]), zpad], axis=1)
    w2b, a2b = _block_diag2(wts["w2"]), _block_diag2(wts["a2"])
    w0c, a0c = row(wts["w0"]), row(wts["a0"])
    g2p = jnp.concatenate([wts["g2"], jnp.zeros((GATE_PAD - GATE_LORA, RW), F32)], axis=0)
    k_k, k_a, r_k = row(wts["k_k"]), row(wts["k_a"]), row(wts["r_k"])
    lnx_g, lnx_b = row(wts["lnx_g"]), row(wts["lnx_b"])
    cdw, cb, clg, clb = wts["conv_dw"], row(wts["conv_b"]), row(wts["conv_ln_g"]), row(wts["conv_ln_b"])
    small = (mu_p, mu_n, w2b, w0c, a2b, a0c, g2p, k_k, k_a)
    seq3 = lambda a: a.reshape(bsz, seq, a.shape[-1])
    flat = lambda a: a.reshape(n_tok, a.shape[-1])

    p = _matmul(x1b, win, name="proj_in")
    r, v, kk, w, kd, b, g = _mix_prep(p, *small, seq=seq, tt=tt, name="mix_prep")
    scan_in = [seq3(a) for a in (r, w, kd, v, kk, b)]
    y0, s_chunks0 = _wkv_chunk_fwd(*scan_in, rev=False, name="wkv_fwd_dir0")
    y1, s_chunks1 = _wkv_chunk_fwd(*scan_in, rev=True, name="wkv_fwd_dir1")
    y0, y1 = flat(y0), flat(y1)
    yr = _mix_post(y0, y1, r, v, kd, g, lnx_g, lnx_b, r_k, tt=tt, name="mix_post")
    yc, yv = _conv_fwd(p, cdw, cb, clg, clb, seq=seq, tt=tt, name="conv_fwd")
    if more_weights is not None:
        wts = {**wts, **more_weights("out", yr)}
    wout, w2i, w2o = wts["w_out"], wts["ffn2_w_in"], wts["ffn2_w_out"]
    z2, x2, x2b = _mm_ln([yr, yv], wout, x1, ln["ln2_g"], ln["ln2_b"], 1.0, tm=TM_LN, name="proj_out_ln2")
    h2, act2 = _ffn_in(x2b, w2i, tm=TM_FFN, name="ffn2_in")

    gr = {}
    slab_rows = lambda a: a.reshape((N_CHIPS, a.shape[0] // N_CHIPS) + a.shape[1:])
    dw_kw = dict(ta=True, out_dtype=BF16)
    dz3, gr["ln3_g"], gr["ln3_b"], loss_part = _mm_ln_loss(act2, w2o, x2, ln["ln3_g"], ln["ln3_b"], tgt, 0.5, tm=TM_LN,
                                                           name="ffn2_out_ln3_loss")
    dh2 = _ffn_out_bwd(dz3, w2o, h2, tm=TM_FFN, name="ffn2_out_dx")
    gr["ffn2_w_out"] = slab_rows(_matmul(act2, dz3, scale=0.5, tm=D_FF // 2, name="ffn2_out_dw", **dw_kw))
    dz2, gr["ln2_g"], gr["ln2_b"] = _mm_nt_res([dh2], w2i, dz3, ln=(z2, ln["ln2_g"], ln["ln2_b"]), tm=TM_FFN,
                                               name="ffn2_in_dx_ln2")
    gr["ffn2_w_in"] = _matmul(x2b, dh2, col_slabs=True, tn=2 * D_FF // N_CHIPS, name="ffn2_in_dw", **dw_kw)
    dmix = _matmul(dz2, wout, tb=True, name="proj_out_dx")
    gr["w_out"] = slab_rows(jnp.concatenate([_matmul(yr, dz2, name="proj_out_dw_rwkv", **dw_kw),
                                             _matmul(yv, dz2, name="proj_out_dw_conv", **dw_kw)], axis=0))
    tok = grads_ready(("ffn2_w_out", "ffn2_w_in", "w_out"), [gr["ffn2_w_out"], gr["ffn2_w_in"], gr["w_out"]])
    dyr, dyv = (dmix, RW, 0), (dmix, RW, 1)
    dy, dr_p, dv_p, dkd_p, dg, gr["lnx_g"], gr["lnx_b"], gr["r_k"] = _mix_post_bwd(
        y0, y1, r, v, kd, g, _follow(lnx_g, tok), lnx_b, r_k, dyr, tt=tt, name="mix_post_bwd")
    dr0, dw0, dkd0, dv0, dk0, db0 = [flat(a) for a in _wkv_chunk_bwd(*scan_in, seq3(dy), s_chunks0, rev=False,
                                                                      name="wkv_bwd_dir0")]
    dr1, dw1, dkd1, dv1, dk1, db1 = [flat(a) for a in _wkv_chunk_bwd(*scan_in, seq3(dy), s_chunks1, rev=True,
                                                                      name="wkv_bwd_dir1")]
    ct_terms = [[dr_p, dr0, dr1], [dv_p, dv0, dv1], [dk0, dk1], [(dw0, dw1)], [dkd_p, (dkd0, dkd1)], [(db0, db1)], [dg]]
    dyc, gr["conv_ln_g"], gr["conv_ln_b"], gr["conv_b"] = _conv_post_bwd(yc, dyv, clg, clb, tt=tt, name="conv_post_bwd")
    dpc, ddw = _conv_bwd(dyc, p, cdw, seq=seq, tt=tt, name="conv_bwd")
    gr["conv_dw"] = ddw[:CONV_K]
    dps, dw2b, dw0c, da2b, da0c, dg2p, gr["k_k"], gr["k_a"] = _mix_prep_bwd(
        p, *small, ct_terms, seq=seq, tt=tt, name="mix_prep_bwd")
    gr["w2"] = jnp.stack([dw2b[:LORA, :RW], dw2b[LORA:, RW:]])
    gr["a2"] = jnp.stack([da2b[:LORA, :RW], da2b[LORA:, RW:]])
    gr["w0"], gr["a0"], gr["g2"] = dw0c.reshape(2, RW), da0c.reshape(2, RW), dg2p[:GATE_LORA]
    dpsh, dmu_p, dmu_n = _shift_bwd(dps, p, mu_p, mu_n, seq=seq, tt=tt, name="shift_bwd")
    gr["mu_prev"], gr["mu_next"] = dmu_p[:, :SHIFT_COLS], dmu_n[:, :SHIFT_COLS]
    dwin = jnp.concatenate([_matmul(x1b, dpsh, name="proj_in_dw_shift", **dw_kw)[:, :SHIFT_COLS],
                            _matmul(x1b, dpc, name="proj_in_dw_conv", **dw_kw)], axis=1)
    gr["w_in"] = jnp.moveaxis(dwin.reshape(D_MODEL, N_CHIPS, IN_COLS // N_CHIPS), 1, 0)
    tok = grads_ready(("w_in",), [gr["w_in"]])
    dz1, gr["ln1_g"], gr["ln1_b"] = _mm_nt_res([dpsh, dpc], win, dz2, ln=(z1, ln["ln1_g"], ln["ln1_b"]), tm=TM_FFN,
                                               after=tok, name="proj_in_dx_ln1")
    dh1 = _ffn_out_bwd(dz1, w1o, h1, tm=TM_FFN, name="ffn1_out_dx")
    gr["ffn1_w_out"] = slab_rows(_matmul(act1, dz1, scale=0.5, tm=D_FF // 2, name="ffn1_out_dw", **dw_kw))
    tok = grads_ready(("ffn1_w_out",), [gr["ffn1_w_out"]])
    gr["ffn1_w_in"] = _matmul(x0, dh1, col_slabs=True, tn=2 * D_FF // N_CHIPS, after=tok, name="ffn1_in_dw", **dw_kw)
    tok = grads_ready(("ffn1_w_in",), [gr["ffn1_w_in"]])
    dx0 = _mm_nt_res([dh1], w1i, dz1, tm=TM_FFN, after=tok, name="ffn1_in_dx")
    return loss_part, dx0.reshape(bsz, seq, D_MODEL), gr


def _mesh_pos():
    return lax.axis_index("x"), lax.axis_index("y"), lax.axis_index("c")


def _other_chips(x, y):
    return [(1 - x, y), (x, 1 - y), (1 - x, 1 - y)]


def _gather_chips(shards, *, name):
    n = len(shards)
    halves = [s.shape[0] // 2 for s in shards]
    assert all(2 * h == s.shape[0] for h, s in zip(halves, shards))

    def body(*refs):
        ins, outs = refs[:n], refs[n:2 * n]
        send_sems, recv_sems, fwd_send_sems, fwd_recv_sems, loc_sems = refs[2 * n:]
        x, y, c = _mesh_pos()
        q = 2 * x + y
        peers = _other_chips(x, y)
        local = [pltpu.make_async_copy(ins[a], outs[a].at[q], loc_sems.at[a]) for a in range(n)]
        for cp in local:
            cp.start()

        def half(a, chip, core):
            return outs[a].at[chip, pl.ds(core * halves[a], halves[a])]

        sends = [pltpu.make_async_remote_copy(ins[a].at[pl.ds(c * halves[a], halves[a])], half(a, q, c),
                                              send_sems.at[a, k], recv_sems.at[a, k],
                                              device_id=(px, py, c), device_id_type=MESH)
                 for a in range(n) for k, (px, py) in enumerate(peers)]
        for cp in sends:
            cp.start()
        passed = []
        for a in range(n):
            for k, (px, py) in enumerate(peers):
                mine = half(a, 2 * px + py, c)
                pltpu.make_async_remote_copy(mine, mine, send_sems.at[a, k], recv_sems.at[a, k],
                                             device_id=(px, py, c), device_id_type=MESH).wait_recv()
                cp = pltpu.make_async_remote_copy(mine, mine, fwd_send_sems.at[a, k], fwd_recv_sems.at[a, k],
                                                  device_id=(x, y, 1 - c), device_id_type=MESH)
                cp.start()
                passed.append(cp)
        for a in range(n):
            for k, (px, py) in enumerate(peers):
                theirs = half(a, 2 * px + py, 1 - c)
                pltpu.make_async_remote_copy(theirs, theirs, fwd_send_sems.at[a, k], fwd_recv_sems.at[a, k],
                                             device_id=(x, y, 1 - c), device_id_type=MESH).wait_recv()
        for cp in sends + passed:
            cp.wait_send()
        for cp in local:
            cp.wait()

    any_spec = pl.BlockSpec(memory_space=pl.ANY)
    return pl.pallas_call(
        body, name=name,
        out_shape=[jax.ShapeDtypeStruct((N_CHIPS,) + s.shape, s.dtype) for s in shards],
        in_specs=[any_spec] * n, out_specs=[any_spec] * n,
        scratch_shapes=[pltpu.SemaphoreType.DMA((n, 3))] * 4 + [pltpu.SemaphoreType.DMA((n,))],
        compiler_params=pltpu.CompilerParams(has_side_effects=True),
    )(*shards)


HBM_SPEC = pl.BlockSpec(memory_space=pltpu.HBM)
SEM_SPEC = pl.BlockSpec(memory_space=pltpu.SEMAPHORE)
ANY_SPEC = pl.BlockSpec(memory_space=pl.ANY)
SIDE_EFFECT = pltpu.SideEffectType.DATAFLOW_SIDE_EFFECTING


def _chip_copies(src_refs, land_refs, send_sems, recv_sems, scatter, arriving=False):
    x, y, c = _mesh_pos()
    cps = []
    for a, (src, land) in enumerate(zip(src_refs, land_refs)):
        for k, (px, py) in enumerate(_other_chips(x, y)):
            slot = k if scatter else (2 * px + py if arriving else 2 * x + y)
            cps.append(pltpu.make_async_remote_copy(src.at[2 * px + py] if scatter else src, land.at[slot],
                                                    send_sems.at[3 * a + k], recv_sems.at[3 * a + k],
                                                    device_id=(px, py, c), device_id_type=MESH))
    return cps


def _exchange_start(srcs, *, scatter, after, name):
    n = len(srcs)
    lands = [lax.empty((3,) + s.shape[1:] if scatter else (N_CHIPS,) + s.shape, s.dtype) for s in srcs]

    def body(*refs):
        src_refs, land_refs = refs[:n], refs[n:2 * n]
        send_sems, recv_sems = refs[2 * n + 1:2 * n + 3]
        token = refs[-1]
        for cp in _chip_copies(src_refs, land_refs, send_sems, recv_sems, scatter):
            cp.start()
        token[...] = jnp.zeros_like(token)

    hbm = lambda a: pltpu.with_memory_space_constraint(a, pltpu.HBM)
    outs = pl.pallas_call(
        body, name=name,
        out_shape=(pltpu.SemaphoreType.DMA((3 * n,)), pltpu.SemaphoreType.DMA((3 * n,)),
                   *[pltpu.HBM(a.shape, a.dtype) for a in srcs + lands], jax.ShapeDtypeStruct((8, LANES), F32)),
        in_specs=[HBM_SPEC] * (2 * n) + [ANY_SPEC],
        out_specs=(SEM_SPEC, SEM_SPEC, *[HBM_SPEC] * (2 * n), pl.BlockSpec(memory_space=pltpu.VMEM)),
        input_output_aliases={i: 2 + i for i in range(2 * n)},
        compiler_params=pltpu.CompilerParams(has_side_effects=SIDE_EFFECT),
    )(*[hbm(a) for a in srcs + lands], after)
    return outs[0], outs[1], list(outs[2:2 + n]), list(outs[2 + n:2 + 2 * n]), outs[-1]


def _exchange_wait(started, *, scatter, after, name):
    send_sems, recv_sems, srcs, lands, _ = started
    n = len(srcs)

    def body(*refs):
        src_refs, land_refs = refs[:n], refs[n:2 * n]
        send_s, recv_s = refs[2 * n:2 * n + 2]
        for cp in _chip_copies(src_refs, land_refs, send_s, recv_s, scatter, arriving=True):
            cp.wait_send()
            cp.wait_recv()

    outs = pl.pallas_call(
        body, name=name,
        out_shape=tuple(pltpu.HBM(a.shape, a.dtype) for a in srcs + lands),
        in_specs=[HBM_SPEC] * (2 * n) + [SEM_SPEC, SEM_SPEC, ANY_SPEC],
        out_specs=tuple([HBM_SPEC] * (2 * n)),
        input_output_aliases={i: i for i in range(2 * n)},
        compiler_params=pltpu.CompilerParams(has_side_effects=SIDE_EFFECT),
    )(*srcs, *lands, send_sems, recv_sems, after)
    return list(outs[:n]), list(outs[n:])


def _by_chip(own, land):
    xi, yi, _ = _mesh_pos()
    return lax.dynamic_update_index_in_dim(land, own, 2 * xi + yi, 0)


def _swap_sibling(arrs, *, name):
    n = len(arrs)

    def body(*refs):
        ins, outs = refs[:n], refs[n:2 * n]
        send_sems, recv_sems = refs[2 * n:]
        x, y, c = _mesh_pos()
        cps = [pltpu.make_async_remote_copy(ins[a], outs[a], send_sems.at[a], recv_sems.at[a],
                                            device_id=(x, y, 1 - c), device_id_type=MESH) for a in range(n)]
        for cp in cps:
            cp.start()
        for cp in cps:
            cp.wait_recv()
        for cp in cps:
            cp.wait_send()

    any_spec = pl.BlockSpec(memory_space=pl.ANY)
    return pl.pallas_call(
        body, name=name,
        out_shape=[jax.ShapeDtypeStruct(s.shape, s.dtype) for s in arrs],
        in_specs=[any_spec] * n, out_specs=[any_spec] * n,
        scratch_shapes=[pltpu.SemaphoreType.DMA((n,)), pltpu.SemaphoreType.DMA((n,))],
        compiler_params=pltpu.CompilerParams(has_side_effects=True),
    )(*arrs)


def _all_reduce_rows(vec, *, name):
    rows = vec.shape[0]

    def body(v_ref, o_ref, land, send_sems, recv_sems):
        x, y, c = _mesh_pos()
        me = 4 * x + 2 * y + c
        land[me] = v_ref[...]
        cps = []
        for m in range(1, 8):
            mx, my, mc = (m >> 2) & 1, (m >> 1) & 1, m & 1
            tx, ty, tc = (x + mx) % 2, (y + my) % 2, (c + mc) % 2
            cps.append(pltpu.make_async_remote_copy(v_ref, land.at[me], send_sems.at[m - 1], recv_sems.at[me],
                                                    device_id=(tx, ty, tc), device_id_type=MESH))
        for cp in cps:
            cp.start()
        for m in range(1, 8):
            mx, my, mc = (m >> 2) & 1, (m >> 1) & 1, m & 1
            src = 4 * ((x + mx) % 2) + 2 * ((y + my) % 2) + (c + mc) % 2
            pltpu.make_async_remote_copy(v_ref, land.at[src], send_sems.at[m - 1], recv_sems.at[src],
                                         device_id=(x, y, c), device_id_type=MESH).wait_recv()
        for cp in cps:
            cp.wait_send()
        acc = land[0]
        for d in range(1, 8):
            acc = acc + land[d]
        o_ref[...] = acc

    vm = pl.BlockSpec(memory_space=pltpu.VMEM)
    return pl.pallas_call(
        body, name=name,
        out_shape=jax.ShapeDtypeStruct(vec.shape, F32),
        in_specs=[vm], out_specs=vm,
        scratch_shapes=[pltpu.VMEM((8, rows, LANES), F32), pltpu.SemaphoreType.DMA((7,)), pltpu.SemaphoreType.DMA((8,))],
        compiler_params=pltpu.CompilerParams(has_side_effects=True, vmem_limit_bytes=VMEM_LIMIT),
    )(vec)


def _adamw(w, g, m, v):
    m = ADAM_B1 * m + (1.0 - ADAM_B1) * g
    v = ADAM_B2 * v + (1.0 - ADAM_B2) * (g * g)
    m_hat = m / (1.0 - ADAM_B1 ** ADAM_STEP)
    v_hat = v / (1.0 - ADAM_B2 ** ADAM_STEP)
    delta = -ADAM_LR * (m_hat / (jnp.sqrt(v_hat) + ADAM_EPS) + ADAM_WD * w)
    return delta, m, v


def _sum4(mine, land, *, name):
    rows, cols = mine.shape
    tr = _pick_rows(rows)

    def body(a_ref, l_ref, o_ref):
        o_ref[...] = (a_ref[...].astype(F32) + l_ref[0].astype(F32)) + (l_ref[1].astype(F32) + l_ref[2].astype(F32))

    return pl.pallas_call(
        body, name=name, out_shape=jax.ShapeDtypeStruct((rows, cols), F32), grid=(rows // tr,),
        in_specs=[pl.BlockSpec((tr, cols), lambda i: (i, 0)), pl.BlockSpec((3, tr, cols), lambda i: (0, i, 0))],
        out_specs=pl.BlockSpec((tr, cols), lambda i: (i, 0)),
        compiler_params=_cparams(("parallel",)),
    )(mine, land)


def _pick_rows(rows, want=256):
    for t in range(min(want, rows) // 8 * 8, 0, -8):
        if rows % t == 0:
            return t
    return rows


def _sum_adam(h_mine, h_sib, w, m, v, *, name):
    rows, cols = w.shape
    tr = _pick_rows(rows)

    def body(a_ref, b_ref, w_ref, m_ref, v_ref, g_o, d_o, m_o, v_o):
        g = a_ref[...] + b_ref[...]
        d, mn, vn = _adamw(w_ref[...], g, m_ref[...], v_ref[...])
        g_o[...], d_o[...], m_o[...], v_o[...] = g, d, mn, vn

    spec = pl.BlockSpec((tr, cols), lambda i: (i, 0))
    return pl.pallas_call(
        body, name=name, out_shape=[jax.ShapeDtypeStruct((rows, cols), F32)] * 4, grid=(rows // tr,),
        in_specs=[spec] * 5, out_specs=[spec] * 4, compiler_params=_cparams(("parallel",)),
    )(h_mine, h_sib, w, m, v)


def _adam_rows(w, g, m, v, *, name):
    def body(w_ref, g_ref, m_ref, v_ref, d_o, m_o, v_o):
        d_o[...], m_o[...], v_o[...] = _adamw(w_ref[...], g_ref[...], m_ref[...], v_ref[...])

    vm = pl.BlockSpec(memory_space=pltpu.VMEM)
    return pl.pallas_call(
        body, name=name, out_shape=[jax.ShapeDtypeStruct(w.shape, F32)] * 3,
        in_specs=[vm] * 4, out_specs=[vm] * 3, compiler_params=_cparams(),
    )(w, g, m, v)


def _size(shape):
    size = 1
    for d in shape:
        size *= d
    return size


def _pack_rows(arrs):
    blocks = []
    for a in arrs:
        flat = a.reshape(-1).astype(F32)
        flat = jnp.concatenate([flat, jnp.zeros((-flat.shape[0] % (8 * LANES),), F32)])
        blocks.append(flat.reshape(-1, LANES))
    return jnp.concatenate(blocks, axis=0)


def _unpack_rows(packed, shapes):
    out, row = [], 0
    for s in shapes:
        rows = -(-_size(s) // (8 * LANES)) * 8
        out.append(packed[row:row + rows].reshape(-1)[:_size(s)].reshape(s))
        row += rows
    return out


WEIGHTS = ['ffn1_w_in', 'ffn1_w_out', 'w_in', 'mu_prev', 'mu_next', 'w0', 'w2', 'a0', 'a2', 'g2', 'k_k', 'k_a', 'r_k',
           'lnx_g', 'lnx_b', 'conv_dw', 'conv_b', 'conv_ln_g', 'conv_ln_b', 'w_out', 'ffn2_w_in', 'ffn2_w_out',
           'ln1_g', 'ln1_b', 'ln2_g', 'ln2_b', 'ln3_g', 'ln3_b']
COL_SHARDED = ('ffn1_w_in', 'w_in', 'ffn2_w_in')
ROW_SHARDED = ('ffn1_w_out', 'w_out', 'ffn2_w_out')
BIG = COL_SHARDED + ROW_SHARDED
SMALL_SHARDED = ('w0', 'w2', 'a0', 'a2', 'g2', 'conv_dw')
REPLICATED = tuple(n for n in WEIGHTS if n not in BIG + SMALL_SHARDED)


def _train_step(x, target, w, m, v, *, tt, tb):
    xi, yi, _ = _mesh_pos()
    q = 2 * xi + yi

    early, mid, late = ("ffn1_w_in", "ffn1_w_out"), ("w_in",) + SMALL_SHARDED, ("w_out", "ffn2_w_in", "ffn2_w_out")
    shard = lambda n: w[n][0].astype(BF16) if n in BIG else w[n][0]

    def whole(n, slabs):
        if n in ROW_SHARDED:
            return slabs.reshape((-1,) + slabs.shape[2:])
        if n in ("ffn1_w_in", "ffn2_w_in"):
            return slabs
        return jnp.moveaxis(slabs, 0, -2).reshape(slabs.shape[1:-1] + (N_CHIPS * slabs.shape[-1],))

    full = {n: w[n][0] for n in REPLICATED}
    first = _gather_chips([shard(n) for n in early], name="gather_ffn1")
    full.update({n: whole(n, g) for n, g in zip(early, first)})
    mid_started = _exchange_start([shard(n) for n in mid], scatter=False, after=first[0], name="gather_mix_start")
    late_started = _exchange_start([shard(n) for n in late], scatter=False, after=mid_started[-1], name="gather_out_start")

    def more_weights(stage, after):
        names, started = (mid, mid_started) if stage == "mix" else (late, late_started)
        own, land = _exchange_wait(started, scatter=False, after=after, name="gather_%s_wait" % stage)
        got = {n: whole(n, _by_chip(o, l)) for n, o, l in zip(names, own, land)}
        full.update(got)
        return got

    sent = []

    def grads_ready(names, slabs):
        started = _exchange_start(slabs, scatter=True, after=slabs[0], name="scatter_%s_start" % names[0])
        sent.append((names, started))
        return started[-1]

    loss_part, grad_x, gr = _local_step(x, target, full, tt=tt, tb=tb, start_token=late_started[-1],
                                        more_weights=more_weights, grads_ready=grads_ready)

    halves = {}
    for names, started in sent:
        stacks, landed = _exchange_wait(started, scatter=True, after=grad_x, name="scatter_%s_wait" % names[0])
        for n, s, l in zip(names, stacks, landed):
            halves[n] = _sum4(lax.dynamic_index_in_dim(s, q, 0, keepdims=False), l, name="sum4_" + n)
    halves = [halves[n] for n in BIG]
    sib = _swap_sibling(halves, name="swap_halves")
    grad, delta, new_m, new_v = {}, {}, {}, {}
    for n, h, hs in zip(BIG, halves, sib):
        outs = _sum_adam(h, hs, w[n][0], m[n][0], v[n][0], name="adam_" + n)
        grad[n], delta[n], new_m[n], new_v[n] = [o[None] for o in outs]

    small_names = REPLICATED + SMALL_SHARDED
    small_full_shapes = [full[n].shape for n in small_names]
    red = _all_reduce_rows(_pack_rows([gr[n] for n in small_names] + [loss_part[0:1, 0:1]]), name="reduce_small")
    *red, loss = _unpack_rows(red, small_full_shapes + [()])
    red = dict(zip(small_names, red))
    gsm = {}
    for n in REPLICATED:
        gsm[n] = red[n].reshape(w[n].shape)
    for n in SMALL_SHARDED:
        width = w[n].shape[-1]
        gsm[n] = lax.dynamic_slice_in_dim(red[n], q * width, width, axis=red[n].ndim - 1).reshape(w[n].shape)
    shapes = [w[n].shape for n in small_names]
    d_p, m_p, v_p = _adam_rows(_pack_rows([w[n] for n in small_names]), _pack_rows([gsm[n] for n in small_names]),
                               _pack_rows([m[n] for n in small_names]), _pack_rows([v[n] for n in small_names]),
                               name="adam_small")
    for n, dd, mm, vv in zip(small_names, _unpack_rows(d_p, shapes), _unpack_rows(m_p, shapes), _unpack_rows(v_p, shapes)):
        grad[n], delta[n], new_m[n], new_v[n] = gsm[n], dd, mm, vv
    return loss, grad_x, grad, delta, new_m, new_v


def kernel(x, ffn1_w_in, ffn1_w_out, w_in, mu_prev, mu_next, w0, w2, a0, a2, g2, k_k, k_a, r_k, lnx_g, lnx_b, conv_dw, conv_b, conv_ln_g, conv_ln_b, w_out, ffn2_w_in, ffn2_w_out, ln1_g, ln1_b, ln2_g, ln2_b, ln3_g, ln3_b, loss_target, m_ffn1_w_in, m_ffn1_w_out, m_w_in, m_mu_prev, m_mu_next, m_w0, m_w2, m_a0, m_a2, m_g2, m_k_k, m_k_a, m_r_k, m_lnx_g, m_lnx_b, m_conv_dw, m_conv_b, m_conv_ln_g, m_conv_ln_b, m_w_out, m_ffn2_w_in, m_ffn2_w_out, m_ln1_g, m_ln1_b, m_ln2_g, m_ln2_b, m_ln3_g, m_ln3_b, v_ffn1_w_in, v_ffn1_w_out, v_w_in, v_mu_prev, v_mu_next, v_w0, v_w2, v_a0, v_a2, v_g2, v_k_k, v_k_a, v_r_k, v_lnx_g, v_lnx_b, v_conv_dw, v_conv_b, v_conv_ln_g, v_conv_ln_b, v_w_out, v_ffn2_w_in, v_ffn2_w_out, v_ln1_g, v_ln1_b, v_ln2_g, v_ln2_b, v_ln3_g, v_ln3_b):
    args = dict(locals())
    w = {n: args[n] for n in WEIGHTS}
    m = {n: args["m_" + n] for n in WEIGHTS}
    v = {n: args["v_" + n] for n in WEIGHTS}
    seq = x.shape[1]
    loss, grad_x, grad, delta, new_m, new_v = _train_step(x, loss_target, w, m, v, tt=min(256, seq), tb=min(TB_SCAN, seq))
    return (loss, grad_x, *[grad[n] for n in WEIGHTS], *[delta[n] for n in WEIGHTS],
            *[new_m[n] for n in WEIGHTS], *[new_v[n] for n in WEIGHTS])
```

```python
import functools

import jax
import jax.numpy as jnp
from jax import lax
from jax.experimental import pallas as pl
from jax.experimental.pallas import tpu as pltpu

F32 = jnp.float32
BF16 = jnp.bfloat16

D_MODEL = 1024
RW = 512
HEAD = 64
CW = 512
CONV_K = 31
CONV_PAD = 15
D_FF = 2816
LORA = 64
GATE_LORA = 160
GATE_PAD = 256
SHIFT_COLS = 1952
SHIFT_PAD = 2048
IN_COLS = 2976
IN_PAD = 3072
LN_EPS = 1e-5
GN_EPS = 64e-5
NORM_EPS = 1e-12
ALPHA = 2.0 ** 0.25
DECAY_SCALE = 0.6065306597126334
ADAM_LR, ADAM_B1, ADAM_B2, ADAM_EPS, ADAM_WD, ADAM_STEP = 0.001, 0.9, 0.999, 1e-08, 0.01, 10
N_CHIPS = 4
VMEM_LIMIT = 56 * 1024 * 1024
TM_FFN = 256
TM_LN = 512

MESH = pl.DeviceIdType.MESH


def _cparams(sem=None, **kw):
    return pltpu.CompilerParams(dimension_semantics=sem, vmem_limit_bytes=VMEM_LIMIT, **kw)


LANES = 128


def _pick_tile(dim, want):
    for t in range(min(want, dim) // LANES * LANES, 0, -LANES):
        if dim % t == 0:
            return t
    return dim


def _after_operand(after):
    return ([], []) if after is None else ([pl.BlockSpec(memory_space=pl.ANY)], [after])


def _matmul(a, b, *, ta=False, tb=False, out_dtype=F32, tm=1024, tn=1024, tk=1024, scale=1.0, col_slabs=False,
            after=None, name):
    after_specs, after_args = _after_operand(after)
    if ta:
        k_dim, m_dim = a.shape
    else:
        m_dim, k_dim = a.shape
    n_dim = b.shape[0] if tb else b.shape[1]
    tm, tn, tk = _pick_tile(m_dim, tm), _pick_tile(n_dim, tn), _pick_tile(k_dim, tk)
    assert m_dim % tm == 0 and n_dim % tn == 0 and k_dim % tk == 0, (name, a.shape, b.shape, tm, tn, tk)
    nk = k_dim // tk
    dims = (((0,) if ta else (1,), (1,) if tb else (0,)), ((), ()))
    if col_slabs:
        out_shape = jax.ShapeDtypeStruct((n_dim // tn, m_dim, tn), out_dtype)
        out_spec = pl.BlockSpec((None, tm, tn), lambda i, j, k: (j, i, 0))
    else:
        out_shape = jax.ShapeDtypeStruct((m_dim, n_dim), out_dtype)
        out_spec = pl.BlockSpec((tm, tn), lambda i, j, k: (i, j))

    def body(a_ref, b_ref, *rest):
        o_ref, acc_ref = rest[-2:]
        kk = pl.program_id(2)

        @pl.when(kk == 0)
        def _():
            acc_ref[...] = jnp.zeros_like(acc_ref)

        acc_ref[...] += lax.dot_general(a_ref[...].astype(BF16), b_ref[...].astype(BF16), dims,
                                        preferred_element_type=F32)

        @pl.when(kk == nk - 1)
        def _():
            o_ref[...] = (acc_ref[...] * scale).astype(o_ref.dtype)

    a_spec = pl.BlockSpec((tk, tm), lambda i, j, k: (k, i)) if ta else pl.BlockSpec((tm, tk), lambda i, j, k: (i, k))
    b_spec = pl.BlockSpec((tn, tk), lambda i, j, k: (j, k)) if tb else pl.BlockSpec((tk, tn), lambda i, j, k: (k, j))
    return pl.pallas_call(
        body, name=name,
        out_shape=out_shape,
        grid=(m_dim // tm, n_dim // tn, nk),
        in_specs=[a_spec, b_spec] + after_specs,
        out_specs=out_spec,
        scratch_shapes=[pltpu.VMEM((tm, tn), F32)],
        compiler_params=_cparams(("parallel", "parallel", "arbitrary")),
    )(a, b, *after_args)


def _whole(shape):
    nd = len(shape)
    return pl.BlockSpec(shape, lambda i: (0,) * nd)


def _ffn_in(x, w, *, tm, after=None, name):
    n_tok = x.shape[0]
    sw = w.shape[2]
    tm = min(tm, n_tok)

    after_specs, after_args = _after_operand(after)

    def body(x_ref, w_ref, *rest):
        h_ref, a_ref = rest[-2:]
        xb = x_ref[...].astype(BF16)
        for s in range(2):
            g = jnp.dot(xb, w_ref[s], preferred_element_type=F32)
            u = jnp.dot(xb, w_ref[s + 2], preferred_element_type=F32)
            h_ref[:, s * sw:(s + 1) * sw] = g.astype(BF16)
            h_ref[:, (s + 2) * sw:(s + 3) * sw] = u.astype(BF16)
            a_ref[:, s * sw:(s + 1) * sw] = (_silu(g) * u).astype(BF16)

    return pl.pallas_call(
        body, name=name,
        out_shape=[jax.ShapeDtypeStruct((n_tok, 2 * D_FF), BF16), jax.ShapeDtypeStruct((n_tok, D_FF), BF16)],
        grid=(n_tok // tm,),
        in_specs=[pl.BlockSpec((tm, D_MODEL), lambda i: (i, 0)), _whole(w.shape)] + after_specs,
        out_specs=[pl.BlockSpec((tm, 2 * D_FF), lambda i: (i, 0)), pl.BlockSpec((tm, D_FF), lambda i: (i, 0))],
        compiler_params=_cparams(("parallel",)),
    )(x, w, *after_args)


def _mm_ln(a_list, w, xres, g, b, fscale, *, tm, name):
    n_tok = xres.shape[0]
    tm = min(tm, n_tok)
    na = len(a_list)

    def body(*refs):
        a_refs = refs[:na]
        w_ref, x_ref, g_ref, b_ref, z_o, y_o, yb_o = refs[na:]
        f, off = None, 0
        for a_ref in a_refs:
            k = a_ref.shape[1]
            t = jnp.dot(a_ref[...].astype(BF16), w_ref[off:off + k, :], preferred_element_type=F32)
            f = t if f is None else f + t
            off += k
        z = ALPHA * x_ref[...] + fscale * f
        y = _layer_norm(z, g_ref[...], b_ref[...])
        z_o[...] = z
        y_o[...] = y
        yb_o[...] = y.astype(BF16)

    tile = pl.BlockSpec((tm, D_MODEL), lambda i: (i, 0))
    return pl.pallas_call(
        body, name=name,
        out_shape=[jax.ShapeDtypeStruct((n_tok, D_MODEL), F32)] * 2 + [jax.ShapeDtypeStruct((n_tok, D_MODEL), BF16)],
        grid=(n_tok // tm,),
        in_specs=[pl.BlockSpec((tm, a.shape[1]), lambda i: (i, 0)) for a in a_list]
        + [_whole(w.shape), tile, _whole(g.shape), _whole(b.shape)],
        out_specs=[tile, tile, tile],
        compiler_params=_cparams(("parallel",)),
    )(*a_list, w, xres, g, b)


def _mm_ln_loss(a, w, xres, g, b, target, fscale, *, tm, name):
    n_tok = xres.shape[0]
    tm = min(tm, n_tok)

    def body(a_ref, w_ref, x_ref, g_ref, b_ref, t_ref, dz_o, dg_o, db_o, loss_o):
        i = pl.program_id(0)
        z = ALPHA * x_ref[...] + fscale * jnp.dot(a_ref[...].astype(BF16), w_ref[...], preferred_element_type=F32)
        y, vjp = jax.vjp(_layer_norm, z, g_ref[...], b_ref[...])
        e = y - t_ref[...]
        dz, dg, db = vjp(e * (1.0 / D_MODEL))

        @pl.when(i == 0)
        def _():
            dg_o[...] = jnp.zeros_like(dg_o)
            db_o[...] = jnp.zeros_like(db_o)
            loss_o[...] = jnp.zeros_like(loss_o)
        dz_o[...] = dz
        dg_o[...] += dg
        db_o[...] += db
        loss_o[...] += 0.5 * jnp.sum(jnp.mean(e * e, axis=-1, keepdims=True), axis=0, keepdims=True)

    tile = pl.BlockSpec((tm, D_MODEL), lambda i: (i, 0))
    row = pl.BlockSpec((1, D_MODEL), lambda i: (0, 0))
    return pl.pallas_call(
        body, name=name,
        out_shape=[jax.ShapeDtypeStruct((n_tok, D_MODEL), F32), jax.ShapeDtypeStruct((1, D_MODEL), F32),
                   jax.ShapeDtypeStruct((1, D_MODEL), F32), jax.ShapeDtypeStruct((8, LANES), F32)],
        grid=(n_tok // tm,),
        in_specs=[pl.BlockSpec((tm, a.shape[1]), lambda i: (i, 0)), _whole(w.shape), tile, row, row, tile],
        out_specs=[tile, row, row, pl.BlockSpec((8, LANES), lambda i: (0, 0))],
        compiler_params=_cparams(("arbitrary",)),
    )(a, w, xres, g, b, target)


def _ffn_out_bwd(dz, w, h, *, tm, name):
    n_tok = dz.shape[0]
    tm = min(tm, n_tok)
    cw = D_FF // 2

    def body(dz_ref, w_ref, h_ref, dh_ref):
        dzb = dz_ref[...].astype(BF16)
        for s in range(2):
            dact = 0.5 * lax.dot_general(dzb, w_ref[s * cw:(s + 1) * cw, :], (((1,), (1,)), ((), ())),
                                         preferred_element_type=F32)
            gate = h_ref[:, s * cw:(s + 1) * cw].astype(F32)
            up = h_ref[:, D_FF + s * cw:D_FF + (s + 1) * cw].astype(F32)
            sg = _sigmoid(gate)
            dh_ref[:, s * cw:(s + 1) * cw] = (dact * up * sg * (1.0 + gate * (1.0 - sg))).astype(BF16)
            dh_ref[:, D_FF + s * cw:D_FF + (s + 1) * cw] = (dact * gate * sg).astype(BF16)

    wide = pl.BlockSpec((tm, 2 * D_FF), lambda i: (i, 0))
    return pl.pallas_call(
        body, name=name,
        out_shape=jax.ShapeDtypeStruct((n_tok, 2 * D_FF), BF16),
        grid=(n_tok // tm,),
        in_specs=[pl.BlockSpec((tm, D_MODEL), lambda i: (i, 0)), _whole(w.shape), wide],
        out_specs=wide,
        compiler_params=_cparams(("parallel",)),
    )(dz, w, h)


def _mm_nt_res(a_list, w, dz, *, tm, ln=None, after=None, name):
    n_tok = dz.shape[0]
    tm = min(tm, n_tok)
    na = len(a_list)
    nt = (((1,), (1,)), ((), ()))
    after_specs, after_args = _after_operand(after)
    n_out = 1 if ln is None else 3

    def body(*refs):
        a_refs = refs[:na]
        w_ref, dz_ref, o_ref = refs[na], refs[na + 1], refs[-n_out]
        acc = ALPHA * dz_ref[...]
        if len(w_ref.shape) == 3:
            cw = w_ref.shape[2]
            for s in range(w_ref.shape[0]):
                acc = acc + lax.dot_general(a_refs[0][:, s * cw:(s + 1) * cw], w_ref[s], nt, preferred_element_type=F32)
        else:
            off = 0
            for a_ref in a_refs:
                k = a_ref.shape[1]
                acc = acc + lax.dot_general(a_ref[...], w_ref[:, off:off + k], nt, preferred_element_type=F32)
                off += k
        if ln is None:
            o_ref[...] = acc
            return
        z_ref, g_ref, b_ref = refs[na + 2:na + 5]
        dg_o, db_o = refs[-2:]
        _, vjp = jax.vjp(_layer_norm, z_ref[...], g_ref[...], b_ref[...])
        o_ref[...], dg, db = vjp(acc)

        @pl.when(pl.program_id(0) == 0)
        def _():
            dg_o[...] = jnp.zeros_like(dg_o)
            db_o[...] = jnp.zeros_like(db_o)
        dg_o[...] += dg
        db_o[...] += db

    tile = pl.BlockSpec((tm, D_MODEL), lambda i: (i, 0))
    row = pl.BlockSpec((1, D_MODEL), lambda i: (0, 0))
    out_shape = [jax.ShapeDtypeStruct((n_tok, D_MODEL), F32)]
    ln_specs, ln_args, out_specs = [], [], [tile]
    if ln is not None:
        ln_specs, ln_args = [tile, row, row], list(ln)
        out_shape += [jax.ShapeDtypeStruct((1, D_MODEL), F32)] * 2
        out_specs += [row, row]
    outs = pl.pallas_call(
        body, name=name,
        out_shape=out_shape,
        grid=(n_tok // tm,),
        in_specs=[pl.BlockSpec((tm, a.shape[1]), lambda i: (i, 0)) for a in a_list] + [_whole(w.shape), tile]
        + ln_specs + after_specs,
        out_specs=out_specs,
        compiler_params=_cparams(("parallel",) if ln is None else ("arbitrary",)),
    )(*a_list, w, dz, *ln_args, *after_args)
    return outs[0] if ln is None else outs


def _rowcall(fn, tok_in, full_in, tok_out, acc_out, *, tt, name):
    views = [a if isinstance(a, tuple) else (a, a.shape[1], 0) for a in tok_in]
    tok_in = [a for a, _, _ in views]
    n_tok = tok_in[0].shape[0]
    assert n_tok % tt == 0, (name, n_tok, tt)
    n_ti, n_fi, n_to = len(tok_in), len(full_in), len(tok_out)

    def body(*refs):
        i = pl.program_id(0)
        ins = [r[...] for r in refs[:n_ti + n_fi]]
        outs = fn(i, *ins)
        o_refs = refs[n_ti + n_fi:]
        for r, val in zip(o_refs[:n_to], outs[:n_to]):
            r[...] = val.astype(r.dtype)
        if acc_out:
            @pl.when(i == 0)
            def _():
                for r in o_refs[n_to:]:
                    r[...] = jnp.zeros_like(r)
            for r, val in zip(o_refs[n_to:], outs[n_to:]):
                r[...] += val.reshape(r.shape).astype(F32)

    in_specs = [pl.BlockSpec((tt, width), functools.partial(lambda k, i: (i, k), k)) for _, width, k in views]
    in_specs += [pl.BlockSpec(a.shape, lambda i: (0, 0)) for a in full_in]
    out_specs = [pl.BlockSpec((tt, c), lambda i: (i, 0)) for c, _ in tok_out]
    out_specs += [pl.BlockSpec(s, lambda i: (0, 0)) for s in acc_out]
    out_shape = [jax.ShapeDtypeStruct((n_tok, c), dt) for c, dt in tok_out]
    out_shape += [jax.ShapeDtypeStruct(s, F32) for s in acc_out]
    return pl.pallas_call(
        body, name=name, out_shape=out_shape, grid=(n_tok // tt,), in_specs=in_specs, out_specs=out_specs,
        compiler_params=_cparams(("arbitrary",) if acc_out else ("parallel",)),
    )(*tok_in, *full_in)


@jax.custom_vjp
def _bdot(a, b):
    return jnp.dot(a.astype(BF16), b.astype(BF16), preferred_element_type=F32)


def _bdot_fwd(a, b):
    return _bdot(a, b), (a, b)


def _bdot_bwd(res, g):
    a, b = res
    g16 = g.astype(BF16)
    da = lax.dot_general(g16, b.astype(BF16), (((1,), (1,)), ((), ())), preferred_element_type=F32)
    db = lax.dot_general(a.astype(BF16), g16, (((0,), (0,)), ((), ())), preferred_element_type=F32)
    return da, db


_bdot.defvjp(_bdot_fwd, _bdot_bwd)


def _split16(x):
    hi = x.astype(BF16)
    lo = (x - hi.astype(F32)).astype(BF16)
    return hi, lo


def _segsum_raw(x, e2):
    hi, lo = _split16(x)
    outs = []
    for c in range(x.shape[1] // 256):
        lhs = jnp.concatenate([hi[:, 256 * c:256 * (c + 1)], lo[:, 256 * c:256 * (c + 1)]], axis=1)
        outs.append(jnp.dot(lhs, e2, preferred_element_type=F32))
    return jnp.concatenate(outs, axis=1)


@jax.custom_vjp
def _segsum(x, e2):
    return _segsum_raw(x, e2)


def _segsum_fwd(x, e2):
    return _segsum_raw(x, e2), e2


def _segsum_bwd(e2, g):
    return _segsum_raw(g, e2), jnp.zeros_like(e2)


_segsum.defvjp(_segsum_fwd, _segsum_bwd)


def _head_ones():
    r = lax.broadcasted_iota(jnp.int32, (512, 256), 0) % 256
    c = lax.broadcasted_iota(jnp.int32, (512, 256), 1)
    return (r // HEAD == c // HEAD).astype(BF16)


def _sigmoid(x):
    return 1.0 / (1.0 + jnp.exp(-x))


def _silu(x):
    return x * _sigmoid(x)


def _layer_norm(z, g, b, eps=LN_EPS):
    mu = jnp.mean(z, axis=-1, keepdims=True)
    zc = z - mu
    var = jnp.mean(zc * zc, axis=-1, keepdims=True)
    return zc * lax.rsqrt(var + eps) * g + b


def _prep(ps, w2b, w0c, a2b, a0c, g2p, k_k, k_a, e2):
    r, k, v = ps[:, 0:512], ps[:, 512:1024], ps[:, 1024:1536]
    wd, ad, gd = ps[:, 1536:1664], ps[:, 1664:1792], ps[:, 1792:2048]
    lw = _bdot(jnp.tanh(wd), w2b) + w0c
    decay = -DECAY_SCALE * _sigmoid(lw)
    a = _sigmoid(_bdot(ad, a2b) + a0c)
    g = _bdot(_sigmoid(gd), g2p)
    kkr = k * k_k
    nrm = jnp.sqrt(_segsum(kkr * kkr, e2))
    kk = kkr / jnp.maximum(nrm, NORM_EPS)
    k2 = jnp.concatenate([k, k], axis=1)
    ka2 = jnp.concatenate([k_a, k_a], axis=1)
    kd = k2 * (1.0 + (a - 1.0) * ka2)
    b = jnp.concatenate([kk, kk], axis=1) * a
    return r, v, kk, decay, kd, b, g


def _post(y0, y1, r, v, kd, g, lnx_g, lnx_b, r_k, e2):
    y = y0 + y1
    mu = _segsum(y, e2) * (1.0 / HEAD)
    yc = y - mu
    var = _segsum(yc * yc, e2) * (1.0 / HEAD)
    yn = yc * lax.rsqrt(var + GN_EPS) * lnx_g + lnx_b
    bonus = _segsum(r * (kd[:, :RW] + kd[:, RW:]) * r_k, e2)
    return (yn + bonus * v) * g


def _conv_post(yc, ln_g, ln_b):
    return _silu(_layer_norm(yc, ln_g, ln_b))


def _halo_specs(cols_block, hb, tt, n_tok, col_idx):
    nb = n_tok // hb
    prev = pl.BlockSpec((hb, cols_block), lambda i: (jnp.maximum(i * (tt // hb) - 1, 0), col_idx))
    nxt = pl.BlockSpec((hb, cols_block), lambda i: (jnp.minimum((i + 1) * (tt // hb), nb - 1), col_idx))
    return prev, nxt


def _mix_prep(p, mu_p, mu_n, w2b, w0c, a2b, a0c, g2p, k_k, k_a, *, seq, tt, name):
    n_tok = p.shape[0]
    tps = seq // tt
    e2 = _head_ones()

    def body(p_ref, hp_ref, hn_ref, mup_ref, mun_ref, w2b_ref, w0c_ref, a2b_ref, a0c_ref, g2p_ref, kk_ref, ka_ref,
             e2_ref, r_o, v_o, kk_o, w_o, kd_o, b_o, g_o, ext):
        i = pl.program_id(0)
        first = (i % tps) == 0
        last = (i % tps) == tps - 1
        pv = p_ref[...]
        ext[pl.ds(0, 8), :] = jnp.where(first, 0.0, hp_ref[...])
        ext[pl.ds(8, tt), :] = pv
        ext[pl.ds(8 + tt, 8), :] = jnp.where(last, 0.0, hn_ref[...])
        prev = ext[pl.ds(7, tt), :]
        nxt = ext[pl.ds(9, tt), :]
        ps = pv + mup_ref[...] * (prev - pv) + mun_ref[...] * (nxt - pv)
        outs = _prep(ps, w2b_ref[...], w0c_ref[...], a2b_ref[...], a0c_ref[...], g2p_ref[...], kk_ref[...],
                     ka_ref[...], e2_ref[...])
        for o_ref, val in zip((r_o, v_o, kk_o, w_o, kd_o, b_o, g_o), outs):
            o_ref[...] = val

    hp, hn = _halo_specs(SHIFT_PAD, 8, tt, n_tok, 0)
    fulls = [mu_p, mu_n, w2b, w0c, a2b, a0c, g2p, k_k, k_a, e2]
    widths = (RW, RW, RW, 2 * RW, 2 * RW, 2 * RW, RW)
    return pl.pallas_call(
        body, name=name,
        out_shape=[jax.ShapeDtypeStruct((n_tok, c), F32) for c in widths],
        grid=(n_tok // tt,),
        in_specs=[pl.BlockSpec((tt, SHIFT_PAD), lambda i: (i, 0)), hp, hn]
        + [pl.BlockSpec(a.shape, lambda i: (0, 0)) for a in fulls],
        out_specs=[pl.BlockSpec((tt, c), lambda i: (i, 0)) for c in widths],
        scratch_shapes=[pltpu.VMEM((tt + 16, SHIFT_PAD), F32)],
        compiler_params=_cparams(("parallel",)),
    )(p, p, p, *fulls)


def _mix_prep_bwd(p, mu_p, mu_n, w2b, w0c, a2b, a0c, g2p, k_k, k_a, ct_terms, *, seq, tt, name):
    n_tok = p.shape[0]
    tps = seq // tt
    e2 = _head_ones()
    acc_shapes = [w2b.shape, w0c.shape, a2b.shape, a0c.shape, g2p.shape, k_k.shape, k_a.shape]
    cts = [a for terms in ct_terms for t in terms for a in (t if isinstance(t, tuple) else (t,))]

    def body(p_ref, hp_ref, hn_ref, mup_ref, mun_ref, w2b_ref, w0c_ref, a2b_ref, a0c_ref, g2p_ref, kk_ref, ka_ref,
             e2_ref, *rest):
        ct_refs, dps_o, acc_refs, ext = rest[:len(cts)], rest[len(cts)], rest[len(cts) + 1:-1], rest[-1]
        ct_it = iter(ct_refs)
        ct_vals = []
        for terms in ct_terms:
            total = None
            for t in terms:
                if isinstance(t, tuple):
                    val = jnp.concatenate([next(ct_it)[...] for _ in t], axis=1)
                else:
                    val = next(ct_it)[...]
                total = val if total is None else total + val
            ct_vals.append(total)
        i = pl.program_id(0)
        first = (i % tps) == 0
        last = (i % tps) == tps - 1
        pv = p_ref[...]
        ext[pl.ds(0, 8), :] = jnp.where(first, 0.0, hp_ref[...])
        ext[pl.ds(8, tt), :] = pv
        ext[pl.ds(8 + tt, 8), :] = jnp.where(last, 0.0, hn_ref[...])
        prev = ext[pl.ds(7, tt), :]
        nxt = ext[pl.ds(9, tt), :]
        ps = pv + mup_ref[...] * (prev - pv) + mun_ref[...] * (nxt - pv)
        e2v = e2_ref[...]
        _, vjp = jax.vjp(lambda *a: _prep(*a, e2v), ps, w2b_ref[...], w0c_ref[...], a2b_ref[...], a0c_ref[...],
                         g2p_ref[...], kk_ref[...], ka_ref[...])
        grads = vjp(tuple(ct_vals))
        dps_o[...] = grads[0]

        @pl.when(i == 0)
        def _():
            for r in acc_refs:
                r[...] = jnp.zeros_like(r)
        for r, val in zip(acc_refs, grads[1:]):
            r[...] += val

    hp, hn = _halo_specs(SHIFT_PAD, 8, tt, n_tok, 0)
    fulls = [mu_p, mu_n, w2b, w0c, a2b, a0c, g2p, k_k, k_a, e2]
    return pl.pallas_call(
        body, name=name,
        out_shape=[jax.ShapeDtypeStruct((n_tok, SHIFT_PAD), F32)] + [jax.ShapeDtypeStruct(s, F32) for s in acc_shapes],
        grid=(n_tok // tt,),
        in_specs=[pl.BlockSpec((tt, SHIFT_PAD), lambda i: (i, 0)), hp, hn]
        + [pl.BlockSpec(a.shape, lambda i: (0, 0)) for a in fulls]
        + [pl.BlockSpec((tt, c.shape[1]), lambda i: (i, 0)) for c in cts],
        out_specs=[pl.BlockSpec((tt, SHIFT_PAD), lambda i: (i, 0))] + [pl.BlockSpec(s, lambda i: (0, 0)) for s in acc_shapes],
        scratch_shapes=[pltpu.VMEM((tt + 16, SHIFT_PAD), F32)],
        compiler_params=_cparams(("arbitrary",)),
    )(p, p, p, *fulls, *cts)


def _shift_bwd(dps, p, mu_p, mu_n, *, seq, tt, name):
    n_tok = p.shape[0]
    tps = seq // tt

    def body(d_ref, dhp_ref, dhn_ref, p_ref, php_ref, phn_ref, mup_ref, mun_ref, dp_o, dmup_o, dmun_o, ext):
        i = pl.program_id(0)
        first = (i % tps) == 0
        last = (i % tps) == tps - 1
        mup, mun = mup_ref[...], mun_ref[...]
        dv = d_ref[...]
        pv = p_ref[...]
        ext[pl.ds(0, 8), :] = jnp.where(first, 0.0, dhp_ref[...])
        ext[pl.ds(8, tt), :] = dv
        ext[pl.ds(8 + tt, 8), :] = jnp.where(last, 0.0, dhn_ref[...])
        d_prev = ext[pl.ds(7, tt), :]
        d_next = ext[pl.ds(9, tt), :]
        dp_o[...] = (dv * (1.0 - mup - mun) + d_next * mup + d_prev * mun).astype(dp_o.dtype)
        ext[pl.ds(0, 8), :] = jnp.where(first, 0.0, php_ref[...])
        ext[pl.ds(8, tt), :] = pv
        ext[pl.ds(8 + tt, 8), :] = jnp.where(last, 0.0, phn_ref[...])
        p_prev = ext[pl.ds(7, tt), :]
        p_next = ext[pl.ds(9, tt), :]

        @pl.when(i == 0)
        def _():
            dmup_o[...] = jnp.zeros_like(dmup_o)
            dmun_o[...] = jnp.zeros_like(dmun_o)
        dmup_o[...] += jnp.sum(dv * (p_prev - pv), axis=0, keepdims=True)
        dmun_o[...] += jnp.sum(dv * (p_next - pv), axis=0, keepdims=True)

    hp, hn = _halo_specs(SHIFT_PAD, 8, tt, n_tok, 0)
    tile = pl.BlockSpec((tt, SHIFT_PAD), lambda i: (i, 0))
    full = pl.BlockSpec((1, SHIFT_PAD), lambda i: (0, 0))
    return pl.pallas_call(
        body, name=name,
        out_shape=[jax.ShapeDtypeStruct((n_tok, SHIFT_PAD), BF16), jax.ShapeDtypeStruct((1, SHIFT_PAD), F32),
                   jax.ShapeDtypeStruct((1, SHIFT_PAD), F32)],
        grid=(n_tok // tt,),
        in_specs=[tile, hp, hn, tile, hp, hn, full, full],
        out_specs=[tile, full, full],
        scratch_shapes=[pltpu.VMEM((tt + 16, SHIFT_PAD), F32)],
        compiler_params=_cparams(("arbitrary",)),
    )(dps, dps, dps, p, p, p, mu_p, mu_n)


def _mix_post(y0, y1, r, v, kd, g, lnx_g, lnx_b, r_k, *, tt, name):
    e2 = _head_ones()
    return _rowcall(lambda i, *a: (_post(*a),), [y0, y1, r, v, kd, g], [lnx_g, lnx_b, r_k, e2], [(RW, BF16)], [],
                    tt=tt, name=name)[0]


def _mix_post_bwd(y0, y1, r, v, kd, g, lnx_g, lnx_b, r_k, dout, *, tt, name):
    e2 = _head_ones()

    def fn(i, y0v, y1v, rv, vv, kdv, gv, dov, lg, lb, rk, e2v):
        _, vjp = jax.vjp(lambda *a: _post(*a, e2v), y0v, y1v, rv, vv, kdv, gv, lg, lb, rk)
        gr = vjp(dov.astype(F32))
        return gr[0], gr[2], gr[3], gr[4], gr[5], gr[6], gr[7], gr[8]
    return _rowcall(fn, [y0, y1, r, v, kd, g, dout], [lnx_g, lnx_b, r_k, e2],
                    [(RW, F32), (RW, F32), (RW, F32), (2 * RW, F32), (RW, F32)], [(1, RW), (1, RW), (1, RW)],
                    tt=tt, name=name)


def _conv_fwd(p, dw, db, ln_g, ln_b, *, seq, tt, name):
    n_tok = p.shape[0]
    tps = seq // tt

    def glu(x, gate):
        return x * _sigmoid(gate)

    def body(u_ref, g_ref, uhp, ghp, uhn, ghn, dw_ref, db_ref, lg_ref, lb_ref, yc_o, y_o, ext):
        i = pl.program_id(0)
        first = (i % tps) == 0
        last = (i % tps) == tps - 1
        ext[pl.ds(0, 16), :] = jnp.where(first, 0.0, glu(uhp[...], ghp[...]))
        ext[pl.ds(16, tt), :] = glu(u_ref[...], g_ref[...])
        ext[pl.ds(16 + tt, 16), :] = jnp.where(last, 0.0, glu(uhn[...], ghn[...]))
        acc = jnp.zeros((tt, CW), F32) + db_ref[...]
        for k in range(CONV_K):
            acc = acc + ext[pl.ds(k + 1, tt), :] * dw_ref[pl.ds(k, 1), :]
        yc_o[...] = acc
        y_o[...] = _conv_post(acc, lg_ref[...], lb_ref[...]).astype(y_o.dtype)

    uhp_s, uhn_s = _halo_specs(CW, 16, tt, n_tok, 4)
    ghp_s, ghn_s = _halo_specs(CW, 16, tt, n_tok, 5)
    fulls = [dw, db, ln_g, ln_b]
    return pl.pallas_call(
        body, name=name,
        out_shape=[jax.ShapeDtypeStruct((n_tok, CW), F32), jax.ShapeDtypeStruct((n_tok, CW), BF16)],
        grid=(n_tok // tt,),
        in_specs=[pl.BlockSpec((tt, CW), lambda i: (i, 4)), pl.BlockSpec((tt, CW), lambda i: (i, 5)),
                  uhp_s, ghp_s, uhn_s, ghn_s] + [pl.BlockSpec(a.shape, lambda i: (0, 0)) for a in fulls],
        out_specs=[pl.BlockSpec((tt, CW), lambda i: (i, 0)), pl.BlockSpec((tt, CW), lambda i: (i, 0))],
        scratch_shapes=[pltpu.VMEM((tt + 32, CW), F32)],
        compiler_params=_cparams(("parallel",)),
    )(p, p, p, p, p, p, *fulls)


def _conv_post_bwd(yc, dy, ln_g, ln_b, *, tt, name):
    def fn(i, ycv, dyv, lg, lb):
        _, vjp = jax.vjp(_conv_post, ycv, lg, lb)
        dyc, dg, dbb = vjp(dyv.astype(F32))
        return dyc, dg, dbb, jnp.sum(dyc, axis=0, keepdims=True)
    return _rowcall(fn, [yc, dy], [ln_g, ln_b], [(CW, F32)], [(1, CW), (1, CW), (1, CW)], tt=tt, name=name)


def _conv_bwd(dyc, p, dw, *, seq, tt, name):
    n_tok = p.shape[0]
    tps = seq // tt

    def body(d_ref, dhp, dhn, u_ref, g_ref, uhp, ghp, uhn, ghn, dw_ref, dp_o, ddw_o, ext):
        i = pl.program_id(0)
        first = (i % tps) == 0
        last = (i % tps) == tps - 1
        dv = d_ref[...]
        ext[pl.ds(0, 16), :] = jnp.where(first, 0.0, dhp[...])
        ext[pl.ds(16, tt), :] = dv
        ext[pl.ds(16 + tt, 16), :] = jnp.where(last, 0.0, dhn[...])
        du = jnp.zeros((tt, CW), F32)
        for k in range(CONV_K):
            du = du + ext[pl.ds(31 - k, tt), :] * dw_ref[pl.ds(k, 1), :]
        uv, gv = u_ref[...], g_ref[...]
        sg = _sigmoid(gv)
        dp_o[:, 0:CW] = (du * sg).astype(dp_o.dtype)
        dp_o[:, CW:2 * CW] = (du * uv * sg * (1.0 - sg)).astype(dp_o.dtype)
        ext[pl.ds(0, 16), :] = jnp.where(first, 0.0, uhp[...] * _sigmoid(ghp[...]))
        ext[pl.ds(16, tt), :] = uv * sg
        ext[pl.ds(16 + tt, 16), :] = jnp.where(last, 0.0, uhn[...] * _sigmoid(ghn[...]))

        @pl.when(i == 0)
        def _():
            ddw_o[...] = jnp.zeros_like(ddw_o)
        for k in range(CONV_K):
            ddw_o[pl.ds(k, 1), :] += jnp.sum(dv * ext[pl.ds(k + 1, tt), :], axis=0, keepdims=True)

    dhp_s, dhn_s = _halo_specs(CW, 16, tt, n_tok, 0)
    uhp_s, uhn_s = _halo_specs(CW, 16, tt, n_tok, 4)
    ghp_s, ghn_s = _halo_specs(CW, 16, tt, n_tok, 5)
    return pl.pallas_call(
        body, name=name,
        out_shape=[jax.ShapeDtypeStruct((n_tok, 2 * CW), BF16), jax.ShapeDtypeStruct((32, CW), F32)],
        grid=(n_tok // tt,),
        in_specs=[pl.BlockSpec((tt, CW), lambda i: (i, 0)), dhp_s, dhn_s,
                  pl.BlockSpec((tt, CW), lambda i: (i, 4)), pl.BlockSpec((tt, CW), lambda i: (i, 5)),
                  uhp_s, ghp_s, uhn_s, ghn_s, pl.BlockSpec(dw.shape, lambda i: (0, 0))],
        out_specs=[pl.BlockSpec((tt, 2 * CW), lambda i: (i, 0)), pl.BlockSpec((32, CW), lambda i: (0, 0))],
        scratch_shapes=[pltpu.VMEM((tt + 32, CW), F32)],
        compiler_params=_cparams(("arbitrary",)),
    )(dyc, dyc, dyc, p, p, p, p, p, p, dw)


CHUNK = 64
_MM_DIMS = {"nn": (((2,), (1,)), ((0,), (0,))), "nt": (((2,), (2,)), ((0,), (0,))), "tn": (((1,), (1,)), ((0,), (0,)))}


def _mm16_raw(a, b, mode, fine):
    dot = lambda x, y: lax.dot_general(x, y, _MM_DIMS[mode], preferred_element_type=F32)
    if not fine:
        return dot(a.astype(BF16), b.astype(BF16))
    (ah, al), (bh, bl) = _split16(a), _split16(b)
    return dot(ah, bh) + dot(ah, bl) + dot(al, bh)


@functools.partial(jax.custom_vjp, nondiff_argnums=(2, 3))
def _mm16(a, b, mode, fine=False):
    return _mm16_raw(a, b, mode, fine)


def _mm16_fwd(a, b, mode, fine):
    return _mm16_raw(a, b, mode, fine), (a, b)


def _mm16_bwd(mode, fine, res, g):
    a, b = res
    if mode == "nn":
        return _mm16_raw(g, b, "nt", fine), _mm16_raw(a, g, "tn", fine)
    if mode == "nt":
        return _mm16_raw(g, b, "nn", fine), _mm16_raw(g, a, "tn", fine)
    return _mm16_raw(b, g, "nt", fine), _mm16_raw(a, g, "nn", fine)


_mm16.defvjp(_mm16_fwd, _mm16_bwd)


def _tri_sum_raw(x, tri, mode):
    hi = x.astype(BF16)
    r1 = x - hi.astype(F32)
    mid = r1.astype(BF16)
    lo = (r1 - mid.astype(F32)).astype(BF16)
    dot = lambda p: lax.dot_general(tri, p, _MM_DIMS[mode], preferred_element_type=F32)
    return dot(hi) + dot(mid) + dot(lo)


@jax.custom_vjp
def _tri_sum(x, tri):
    return _tri_sum_raw(x, tri, "nn")


def _tri_sum_fwd(x, tri):
    return _tri_sum_raw(x, tri, "nn"), tri


def _tri_sum_bwd(tri, g):
    return _tri_sum_raw(g, tri, "tn"), jnp.zeros_like(tri)


_tri_sum.defvjp(_tri_sum_fwd, _tri_sum_bwd)


def _chunk_step(s0, r, lw, k, v, kk, b, tri, rev):
    nh, n, _ = r.shape
    row = lax.broadcasted_iota(jnp.int32, (nh, n, n), 1)
    col = lax.broadcasted_iota(jnp.int32, (nh, n, n), 2)
    if rev:
        row, col = col, row
    cum = _tri_sum(lw, tri)
    up, down = jnp.exp(cum), jnp.exp(-cum)
    at, rt = -kk * jnp.exp(cum - lw), r * up
    kt, bt = k * down, b * down
    a_ab = jnp.where(col < row, _mm16(at, bt, "nt", True), 0.0)
    a_ak = jnp.where(col < row, _mm16(at, kt, "nt", True), 0.0)
    a_rb = jnp.where(col <= row, _mm16(rt, bt, "nt", True), 0.0)
    a_rk = jnp.where(col <= row, _mm16(rt, kt, "nt", True), 0.0)
    u = _mm16(at, s0, "nt") + _mm16(a_ak, v, "nn")
    power = a_ab
    steps = n.bit_length() - 1
    for it in range(steps):
        u = u + _mm16(power, u, "nn")
        if it + 1 < steps:
            power = _mm16(power, power, "nn")
    y = _mm16(rt, s0, "nt") + _mm16(a_rk, v, "nn") + _mm16(a_rb, u, "nn")
    grown = s0 + _mm16(v, kt, "tn") + _mm16(u, bt, "tn")
    return y, grown * jnp.exp(jnp.sum(lw, axis=1, keepdims=True))


N_HEADS = RW // HEAD


def _tri_ones(rev):
    shape = (N_HEADS, CHUNK, CHUNK)
    row, col = lax.broadcasted_iota(jnp.int32, shape, 1), lax.broadcasted_iota(jnp.int32, shape, 2)
    return ((col >= row) if rev else (col <= row)).astype(BF16)


def _split_heads(ref):
    return jnp.stack([ref[0, :, pl.ds(h * HEAD, HEAD)] for h in range(N_HEADS)])


def _merge_heads(ref, val):
    for h in range(N_HEADS):
        ref[0, :, pl.ds(h * HEAD, HEAD)] = val[h]


def _chunk_specs(bsz, nc, rev, dcol):
    chunk = (lambda c: nc - 1 - c) if rev else (lambda c: c)
    shared = pl.BlockSpec((1, CHUNK, RW), lambda s, c: (s, chunk(c), 0))
    own = pl.BlockSpec((1, CHUNK, RW), lambda s, c: (s, chunk(c), dcol))
    return shared, own, chunk


def _wkv_chunk_fwd(r, lw, k, v, kk, b, *, rev, name):
    bsz, seq, _ = r.shape
    nc = seq // CHUNK
    shared, own, _ = _chunk_specs(bsz, nc, rev, int(rev))

    def body(r_ref, lw_ref, k_ref, v_ref, kk_ref, b_ref, tri_ref, y_o, s0_o, s_ref):
        @pl.when(pl.program_id(1) == 0)
        def _():
            s_ref[...] = jnp.zeros_like(s_ref)

        s0 = s_ref[...]
        s0_o[0, 0] = s0
        y, s_ref[...] = _chunk_step(s0, *[_split_heads(x) for x in (r_ref, lw_ref, k_ref, v_ref, kk_ref, b_ref)],
                                    tri_ref[...], rev)
        _merge_heads(y_o, y)

    return pl.pallas_call(
        body, name=name,
        out_shape=[jax.ShapeDtypeStruct((bsz, seq, RW), F32), jax.ShapeDtypeStruct((bsz, nc, N_HEADS, HEAD, HEAD), F32)],
        grid=(bsz, nc),
        in_specs=[shared, own, own, shared, shared, own, pl.BlockSpec((N_HEADS, CHUNK, CHUNK), lambda s, c: (0, 0, 0))],
        out_specs=[shared, pl.BlockSpec((1, 1, N_HEADS, HEAD, HEAD), lambda s, c: (s, c, 0, 0, 0))],
        scratch_shapes=[pltpu.VMEM((N_HEADS, HEAD, HEAD), F32)],
        compiler_params=_cparams(("parallel", "arbitrary")),
    )(r, lw, k, v, kk, b, _tri_ones(rev))


def _wkv_chunk_bwd(r, lw, k, v, kk, b, dy, s0, *, rev, name):
    bsz, seq, _ = r.shape
    nc = seq // CHUNK
    shared, own, _ = _chunk_specs(bsz, nc, not rev, int(rev))

    def body(r_ref, lw_ref, k_ref, v_ref, kk_ref, b_ref, dy_ref, s0_ref, tri_ref, *rest):
        outs, ds_ref = rest[:-1], rest[-1]

        @pl.when(pl.program_id(1) == 0)
        def _():
            ds_ref[...] = jnp.zeros_like(ds_ref)

        triv = tri_ref[...]
        _, vjp = jax.vjp(lambda *a: _chunk_step(*a, triv, rev), s0_ref[0, 0],
                         *[_split_heads(x) for x in (r_ref, lw_ref, k_ref, v_ref, kk_ref, b_ref)])
        grads = vjp((_split_heads(dy_ref), ds_ref[...]))
        ds_ref[...] = grads[0]
        for o, gval in zip(outs, grads[1:]):
            _merge_heads(o, gval)

    return pl.pallas_call(
        body, name=name,
        out_shape=[jax.ShapeDtypeStruct((bsz, seq, RW), F32)] * 6,
        grid=(bsz, nc),
        in_specs=[shared, own, own, shared, shared, own, shared,
                  pl.BlockSpec((1, 1, N_HEADS, HEAD, HEAD), lambda s, c: (s, nc - 1 - c, 0, 0, 0)),
                  pl.BlockSpec((N_HEADS, CHUNK, CHUNK), lambda s, c: (0, 0, 0))],
        out_specs=[shared] * 6,
        scratch_shapes=[pltpu.VMEM((N_HEADS, HEAD, HEAD), F32)],
        compiler_params=_cparams(("parallel", "arbitrary")),
    )(r, lw, k, v, kk, b, dy, s0, _tri_ones(rev))


def _block_diag2(w):
    z = jnp.zeros_like(w[0])
    return jnp.concatenate([jnp.concatenate([w[0], z], axis=1), jnp.concatenate([z, w[1]], axis=1)], axis=0)


def _pad_in_cols(a):
    z = jnp.zeros(a.shape[:-1] + (SHIFT_PAD - SHIFT_COLS,), a.dtype)
    return jnp.concatenate([a[..., :SHIFT_COLS], z, a[..., SHIFT_COLS:]], axis=-1)


def _follow(small, token):
    return small if token is None else small + token[0:1, 0:1]


def _local_step(x, target, wts, *, tt, start_token=None, more_weights=None, grads_ready=None):
    bsz, seq, _ = x.shape
    n_tok = bsz * seq
    row = lambda a: a.reshape(1, -1).astype(F32)
    x0 = x.reshape(n_tok, D_MODEL)
    tgt = target.reshape(n_tok, D_MODEL)
    ln = {k: row(wts[k]) for k in ("ln1_g", "ln1_b", "ln2_g", "ln2_b", "ln3_g", "ln3_b")}
    if grads_ready is None:
        grads_ready = lambda names, slabs: None

    w1i, w1o = wts["ffn1_w_in"], wts["ffn1_w_out"]
    h1, act1 = _ffn_in(x0, w1i, tm=TM_FFN, after=start_token, name="ffn1_in")
    z1, x1, x1b = _mm_ln([act1], w1o, x0, ln["ln1_g"], ln["ln1_b"], 0.5, tm=TM_LN, name="ffn1_out_ln1")
    if more_weights is not None:
        wts = {**wts, **more_weights("mix", x1b)}
    win = _pad_in_cols(wts["w_in"])
    zpad = jnp.zeros((1, SHIFT_PAD - SHIFT_COLS), F32)
    mu_p = jnp.concatenate([row(wts["mu_prev"]), zpad], axis=1)
    mu_n = jnp.concatenate([row(wts["mu_next"]), zpad], axis=1)
    w2b, a2b = _block_diag2(wts["w2"]), _block_diag2(wts["a2"])
    w0c, a0c = row(wts["w0"]), row(wts["a0"])
    g2p = jnp.concatenate([wts["g2"], jnp.zeros((GATE_PAD - GATE_LORA, RW), F32)], axis=0)
    k_k, k_a, r_k = row(wts["k_k"]), row(wts["k_a"]), row(wts["r_k"])
    lnx_g, lnx_b = row(wts["lnx_g"]), row(wts["lnx_b"])
    cdw, cb, clg, clb = wts["conv_dw"], row(wts["conv_b"]), row(wts["conv_ln_g"]), row(wts["conv_ln_b"])
    small = (mu_p, mu_n, w2b, w0c, a2b, a0c, g2p, k_k, k_a)
    seq3 = lambda a: a.reshape(bsz, seq, a.shape[-1])
    flat = lambda a: a.reshape(n_tok, a.shape[-1])

    p = _matmul(x1b, win, name="proj_in")
    r, v, kk, w, kd, b, g = _mix_prep(p, *small, seq=seq, tt=tt, name="mix_prep")
    scan_in = [seq3(a) for a in (r, w, kd, v, kk, b)]
    y0, s_chunks0 = _wkv_chunk_fwd(*scan_in, rev=False, name="wkv_fwd_dir0")
    y1, s_chunks1 = _wkv_chunk_fwd(*scan_in, rev=True, name="wkv_fwd_dir1")
    y0, y1 = flat(y0), flat(y1)
    yr = _mix_post(y0, y1, r, v, kd, g, lnx_g, lnx_b, r_k, tt=tt, name="mix_post")
    yc, yv = _conv_fwd(p, cdw, cb, clg, clb, seq=seq, tt=tt, name="conv_fwd")
    if more_weights is not None:
        wts = {**wts, **more_weights("out", yr)}
    wout, w2i, w2o = wts["w_out"], wts["ffn2_w_in"], wts["ffn2_w_out"]
    z2, x2, x2b = _mm_ln([yr, yv], wout, x1, ln["ln2_g"], ln["ln2_b"], 1.0, tm=TM_LN, name="proj_out_ln2")
    h2, act2 = _ffn_in(x2b, w2i, tm=TM_FFN, name="ffn2_in")

    gr = {}
    slab_rows = lambda a: a.reshape((N_CHIPS, a.shape[0] // N_CHIPS) + a.shape[1:])
    dw_kw = dict(ta=True, out_dtype=BF16)
    dz3, gr["ln3_g"], gr["ln3_b"], loss_part = _mm_ln_loss(act2, w2o, x2, ln["ln3_g"], ln["ln3_b"], tgt, 0.5, tm=TM_LN,
                                                           name="ffn2_out_ln3_loss")
    dh2 = _ffn_out_bwd(dz3, w2o, h2, tm=TM_FFN, name="ffn2_out_dx")
    gr["ffn2_w_out"] = slab_rows(_matmul(act2, dz3, scale=0.5, tm=D_FF // 2, name="ffn2_out_dw", **dw_kw))
    dz2, gr["ln2_g"], gr["ln2_b"] = _mm_nt_res([dh2], w2i, dz3, ln=(z2, ln["ln2_g"], ln["ln2_b"]), tm=TM_FFN,
                                               name="ffn2_in_dx_ln2")
    gr["ffn2_w_in"] = _matmul(x2b, dh2, col_slabs=True, tn=2 * D_FF // N_CHIPS, name="ffn2_in_dw", **dw_kw)
    dmix = _matmul(dz2, wout, tb=True, name="proj_out_dx")
    gr["w_out"] = slab_rows(jnp.concatenate([_matmul(yr, dz2, name="proj_out_dw_rwkv", **dw_kw),
                                             _matmul(yv, dz2, name="proj_out_dw_conv", **dw_kw)], axis=0))
    tok = grads_ready(("ffn2_w_out", "ffn2_w_in", "w_out"), [gr["ffn2_w_out"], gr["ffn2_w_in"], gr["w_out"]])
    dyr, dyv = (dmix, RW, 0), (dmix, RW, 1)
    dy, dr_p, dv_p, dkd_p, dg, gr["lnx_g"], gr["lnx_b"], gr["r_k"] = _mix_post_bwd(
        y0, y1, r, v, kd, g, _follow(lnx_g, tok), lnx_b, r_k, dyr, tt=tt, name="mix_post_bwd")
    dr0, dw0, dkd0, dv0, dk0, db0 = [flat(a) for a in _wkv_chunk_bwd(*scan_in, seq3(dy), s_chunks0, rev=False,
                                                                      name="wkv_bwd_dir0")]
    dr1, dw1, dkd1, dv1, dk1, db1 = [flat(a) for a in _wkv_chunk_bwd(*scan_in, seq3(dy), s_chunks1, rev=True,
                                                                      name="wkv_bwd_dir1")]
    ct_terms = [[dr_p, dr0, dr1], [dv_p, dv0, dv1], [dk0, dk1], [(dw0, dw1)], [dkd_p, (dkd0, dkd1)], [(db0, db1)], [dg]]
    dyc, gr["conv_ln_g"], gr["conv_ln_b"], gr["conv_b"] = _conv_post_bwd(yc, dyv, clg, clb, tt=tt, name="conv_post_bwd")
    dpc, ddw = _conv_bwd(dyc, p, cdw, seq=seq, tt=tt, name="conv_bwd")
    gr["conv_dw"] = ddw[:CONV_K]
    dps, dw2b, dw0c, da2b, da0c, dg2p, gr["k_k"], gr["k_a"] = _mix_prep_bwd(
        p, *small, ct_terms, seq=seq, tt=tt, name="mix_prep_bwd")
    gr["w2"] = jnp.stack([dw2b[:LORA, :RW], dw2b[LORA:, RW:]])
    gr["a2"] = jnp.stack([da2b[:LORA, :RW], da2b[LORA:, RW:]])
    gr["w0"], gr["a0"], gr["g2"] = dw0c.reshape(2, RW), da0c.reshape(2, RW), dg2p[:GATE_LORA]
    dpsh, dmu_p, dmu_n = _shift_bwd(dps, p, mu_p, mu_n, seq=seq, tt=tt, name="shift_bwd")
    gr["mu_prev"], gr["mu_next"] = dmu_p[:, :SHIFT_COLS], dmu_n[:, :SHIFT_COLS]
    dwin = jnp.concatenate([_matmul(x1b, dpsh, name="proj_in_dw_shift", **dw_kw)[:, :SHIFT_COLS],
                            _matmul(x1b, dpc, name="proj_in_dw_conv", **dw_kw)], axis=1)
    gr["w_in"] = jnp.moveaxis(dwin.reshape(D_MODEL, N_CHIPS, IN_COLS // N_CHIPS), 1, 0)
    tok = grads_ready(("w_in",), [gr["w_in"]])
    dz1, gr["ln1_g"], gr["ln1_b"] = _mm_nt_res([dpsh, dpc], win, dz2, ln=(z1, ln["ln1_g"], ln["ln1_b"]), tm=TM_FFN,
                                               after=tok, name="proj_in_dx_ln1")
    dh1 = _ffn_out_bwd(dz1, w1o, h1, tm=TM_FFN, name="ffn1_out_dx")
    gr["ffn1_w_out"] = slab_rows(_matmul(act1, dz1, scale=0.5, tm=D_FF // 2, name="ffn1_out_dw", **dw_kw))
    tok = grads_ready(("ffn1_w_out",), [gr["ffn1_w_out"]])
    gr["ffn1_w_in"] = _matmul(x0, dh1, col_slabs=True, tn=2 * D_FF // N_CHIPS, after=tok, name="ffn1_in_dw", **dw_kw)
    tok = grads_ready(("ffn1_w_in",), [gr["ffn1_w_in"]])
    dx0 = _mm_nt_res([dh1], w1i, dz1, tm=TM_FFN, after=tok, name="ffn1_in_dx")
    return loss_part, dx0.reshape(bsz, seq, D_MODEL), gr


def _mesh_pos():
    return lax.axis_index("x"), lax.axis_index("y"), lax.axis_index("c")


def _other_chips(x, y):
    return [(1 - x, y), (x, 1 - y), (1 - x, 1 - y)]


def _gather_chips(shards, *, name):
    n = len(shards)
    halves = [s.shape[0] // 2 for s in shards]
    assert all(2 * h == s.shape[0] for h, s in zip(halves, shards))

    def body(*refs):
        ins, outs = refs[:n], refs[n:2 * n]
        send_sems, recv_sems, fwd_send_sems, fwd_recv_sems, loc_sems = refs[2 * n:]
        x, y, c = _mesh_pos()
        q = 2 * x + y
        peers = _other_chips(x, y)
        local = [pltpu.make_async_copy(ins[a], outs[a].at[q], loc_sems.at[a]) for a in range(n)]
        for cp in local:
            cp.start()

        def half(a, chip, core):
            return outs[a].at[chip, pl.ds(core * halves[a], halves[a])]

        sends = [pltpu.make_async_remote_copy(ins[a].at[pl.ds(c * halves[a], halves[a])], half(a, q, c),
                                              send_sems.at[a, k], recv_sems.at[a, k],
                                              device_id=(px, py, c), device_id_type=MESH)
                 for a in range(n) for k, (px, py) in enumerate(peers)]
        for cp in sends:
            cp.start()
        passed = []
        for a in range(n):
            for k, (px, py) in enumerate(peers):
                mine = half(a, 2 * px + py, c)
                pltpu.make_async_remote_copy(mine, mine, send_sems.at[a, k], recv_sems.at[a, k],
                                             device_id=(px, py, c), device_id_type=MESH).wait_recv()
                cp = pltpu.make_async_remote_copy(mine, mine, fwd_send_sems.at[a, k], fwd_recv_sems.at[a, k],
                                                  device_id=(x, y, 1 - c), device_id_type=MESH)
                cp.start()
                passed.append(cp)
        for a in range(n):
            for k, (px, py) in enumerate(peers):
                theirs = half(a, 2 * px + py, 1 - c)
                pltpu.make_async_remote_copy(theirs, theirs, fwd_send_sems.at[a, k], fwd_recv_sems.at[a, k],
                                             device_id=(x, y, 1 - c), device_id_type=MESH).wait_recv()
        for cp in sends + passed:
            cp.wait_send()
        for cp in local:
            cp.wait()

    any_spec = pl.BlockSpec(memory_space=pl.ANY)
    return pl.pallas_call(
        body, name=name,
        out_shape=[jax.ShapeDtypeStruct((N_CHIPS,) + s.shape, s.dtype) for s in shards],
        in_specs=[any_spec] * n, out_specs=[any_spec] * n,
        scratch_shapes=[pltpu.SemaphoreType.DMA((n, 3))] * 4 + [pltpu.SemaphoreType.DMA((n,))],
        compiler_params=pltpu.CompilerParams(has_side_effects=True),
    )(*shards)


HBM_SPEC = pl.BlockSpec(memory_space=pltpu.HBM)
SEM_SPEC = pl.BlockSpec(memory_space=pltpu.SEMAPHORE)
ANY_SPEC = pl.BlockSpec(memory_space=pl.ANY)
SIDE_EFFECT = pltpu.SideEffectType.DATAFLOW_SIDE_EFFECTING


def _chip_copies(src_refs, land_refs, send_sems, recv_sems, scatter, arriving=False):
    x, y, c = _mesh_pos()
    cps = []
    for a, (src, land) in enumerate(zip(src_refs, land_refs)):
        for k, (px, py) in enumerate(_other_chips(x, y)):
            slot = k if scatter else (2 * px + py if arriving else 2 * x + y)
            cps.append(pltpu.make_async_remote_copy(src.at[2 * px + py] if scatter else src, land.at[slot],
                                                    send_sems.at[3 * a + k], recv_sems.at[3 * a + k],
                                                    device_id=(px, py, c), device_id_type=MESH))
    return cps


def _exchange_start(srcs, *, scatter, after, name):
    n = len(srcs)
    lands = [lax.empty((3,) + s.shape[1:] if scatter else (N_CHIPS,) + s.shape, s.dtype) for s in srcs]

    def body(*refs):
        src_refs, land_refs = refs[:n], refs[n:2 * n]
        send_sems, recv_sems = refs[2 * n + 1:2 * n + 3]
        token = refs[-1]
        for cp in _chip_copies(src_refs, land_refs, send_sems, recv_sems, scatter):
            cp.start()
        token[...] = jnp.zeros_like(token)

    hbm = lambda a: pltpu.with_memory_space_constraint(a, pltpu.HBM)
    outs = pl.pallas_call(
        body, name=name,
        out_shape=(pltpu.SemaphoreType.DMA((3 * n,)), pltpu.SemaphoreType.DMA((3 * n,)),
                   *[pltpu.HBM(a.shape, a.dtype) for a in srcs + lands], jax.ShapeDtypeStruct((8, LANES), F32)),
        in_specs=[HBM_SPEC] * (2 * n) + [ANY_SPEC],
        out_specs=(SEM_SPEC, SEM_SPEC, *[HBM_SPEC] * (2 * n), pl.BlockSpec(memory_space=pltpu.VMEM)),
        input_output_aliases={i: 2 + i for i in range(2 * n)},
        compiler_params=pltpu.CompilerParams(has_side_effects=SIDE_EFFECT),
    )(*[hbm(a) for a in srcs + lands], after)
    return outs[0], outs[1], list(outs[2:2 + n]), list(outs[2 + n:2 + 2 * n]), outs[-1]


def _exchange_wait(started, *, scatter, after, name):
    send_sems, recv_sems, srcs, lands, _ = started
    n = len(srcs)

    def body(*refs):
        src_refs, land_refs = refs[:n], refs[n:2 * n]
        send_s, recv_s = refs[2 * n:2 * n + 2]
        for cp in _chip_copies(src_refs, land_refs, send_s, recv_s, scatter, arriving=True):
            cp.wait_send()
            cp.wait_recv()

    outs = pl.pallas_call(
        body, name=name,
        out_shape=tuple(pltpu.HBM(a.shape, a.dtype) for a in srcs + lands),
        in_specs=[HBM_SPEC] * (2 * n) + [SEM_SPEC, SEM_SPEC, ANY_SPEC],
        out_specs=tuple([HBM_SPEC] * (2 * n)),
        input_output_aliases={i: i for i in range(2 * n)},
        compiler_params=pltpu.CompilerParams(has_side_effects=SIDE_EFFECT),
    )(*srcs, *lands, send_sems, recv_sems, after)
    return list(outs[:n]), list(outs[n:])


def _by_chip(own, land):
    xi, yi, _ = _mesh_pos()
    return lax.dynamic_update_index_in_dim(land, own, 2 * xi + yi, 0)


def _swap_sibling(arrs, *, name):
    n = len(arrs)

    def body(*refs):
        ins, outs = refs[:n], refs[n:2 * n]
        send_sems, recv_sems = refs[2 * n:]
        x, y, c = _mesh_pos()
        cps = [pltpu.make_async_remote_copy(ins[a], outs[a], send_sems.at[a], recv_sems.at[a],
                                            device_id=(x, y, 1 - c), device_id_type=MESH) for a in range(n)]
        for cp in cps:
            cp.start()
        for cp in cps:
            cp.wait_recv()
        for cp in cps:
            cp.wait_send()

    any_spec = pl.BlockSpec(memory_space=pl.ANY)
    return pl.pallas_call(
        body, name=name,
        out_shape=[jax.ShapeDtypeStruct(s.shape, s.dtype) for s in arrs],
        in_specs=[any_spec] * n, out_specs=[any_spec] * n,
        scratch_shapes=[pltpu.SemaphoreType.DMA((n,)), pltpu.SemaphoreType.DMA((n,))],
        compiler_params=pltpu.CompilerParams(has_side_effects=True),
    )(*arrs)


def _all_reduce_rows(vec, *, name):
    rows = vec.shape[0]

    def body(v_ref, o_ref, land, send_sems, recv_sems):
        x, y, c = _mesh_pos()
        me = 4 * x + 2 * y + c
        land[me] = v_ref[...]
        cps = []
        for m in range(1, 8):
            mx, my, mc = (m >> 2) & 1, (m >> 1) & 1, m & 1
            tx, ty, tc = (x + mx) % 2, (y + my) % 2, (c + mc) % 2
            cps.append(pltpu.make_async_remote_copy(v_ref, land.at[me], send_sems.at[m - 1], recv_sems.at[me],
                                                    device_id=(tx, ty, tc), device_id_type=MESH))
        for cp in cps:
            cp.start()
        for m in range(1, 8):
            mx, my, mc = (m >> 2) & 1, (m >> 1) & 1, m & 1
            src = 4 * ((x + mx) % 2) + 2 * ((y + my) % 2) + (c + mc) % 2
            pltpu.make_async_remote_copy(v_ref, land.at[src], send_sems.at[m - 1], recv_sems.at[src],
                                         device_id=(x, y, c), device_id_type=MESH).wait_recv()
        for cp in cps:
            cp.wait_send()
        acc = land[0]
        for d in range(1, 8):
            acc = acc + land[d]
        o_ref[...] = acc

    vm = pl.BlockSpec(memory_space=pltpu.VMEM)
    return pl.pallas_call(
        body, name=name,
        out_shape=jax.ShapeDtypeStruct(vec.shape, F32),
        in_specs=[vm], out_specs=vm,
        scratch_shapes=[pltpu.VMEM((8, rows, LANES), F32), pltpu.SemaphoreType.DMA((7,)), pltpu.SemaphoreType.DMA((8,))],
        compiler_params=pltpu.CompilerParams(has_side_effects=True, vmem_limit_bytes=VMEM_LIMIT),
    )(vec)


def _adamw(w, g, m, v):
    m = ADAM_B1 * m + (1.0 - ADAM_B1) * g
    v = ADAM_B2 * v + (1.0 - ADAM_B2) * (g * g)
    m_hat = m / (1.0 - ADAM_B1 ** ADAM_STEP)
    v_hat = v / (1.0 - ADAM_B2 ** ADAM_STEP)
    delta = -ADAM_LR * (m_hat / (jnp.sqrt(v_hat) + ADAM_EPS) + ADAM_WD * w)
    return delta, m, v


def _sum4(mine, land, *, name):
    rows, cols = mine.shape
    tr = _pick_rows(rows)

    def body(a_ref, l_ref, o_ref):
        o_ref[...] = (a_ref[...].astype(F32) + l_ref[0].astype(F32)) + (l_ref[1].astype(F32) + l_ref[2].astype(F32))

    return pl.pallas_call(
        body, name=name, out_shape=jax.ShapeDtypeStruct((rows, cols), F32), grid=(rows // tr,),
        in_specs=[pl.BlockSpec((tr, cols), lambda i: (i, 0)), pl.BlockSpec((3, tr, cols), lambda i: (0, i, 0))],
        out_specs=pl.BlockSpec((tr, cols), lambda i: (i, 0)),
        compiler_params=_cparams(("parallel",)),
    )(mine, land)


def _pick_rows(rows, want=256):
    for t in range(min(want, rows) // 8 * 8, 0, -8):
        if rows % t == 0:
            return t
    return rows


def _sum_adam(h_mine, h_sib, w, m, v, *, name):
    rows, cols = w.shape
    tr = _pick_rows(rows)

    def body(a_ref, b_ref, w_ref, m_ref, v_ref, g_o, d_o, m_o, v_o):
        g = a_ref[...] + b_ref[...]
        d, mn, vn = _adamw(w_ref[...], g, m_ref[...], v_ref[...])
        g_o[...], d_o[...], m_o[...], v_o[...] = g, d, mn, vn

    spec = pl.BlockSpec((tr, cols), lambda i: (i, 0))
    return pl.pallas_call(
        body, name=name, out_shape=[jax.ShapeDtypeStruct((rows, cols), F32)] * 4, grid=(rows // tr,),
        in_specs=[spec] * 5, out_specs=[spec] * 4, compiler_params=_cparams(("parallel",)),
    )(h_mine, h_sib, w, m, v)


def _adam_rows(w, g, m, v, *, name):
    def body(w_ref, g_ref, m_ref, v_ref, d_o, m_o, v_o):
        d_o[...], m_o[...], v_o[...] = _adamw(w_ref[...], g_ref[...], m_ref[...], v_ref[...])

    vm = pl.BlockSpec(memory_space=pltpu.VMEM)
    return pl.pallas_call(
        body, name=name, out_shape=[jax.ShapeDtypeStruct(w.shape, F32)] * 3,
        in_specs=[vm] * 4, out_specs=[vm] * 3, compiler_params=_cparams(),
    )(w, g, m, v)


def _size(shape):
    size = 1
    for d in shape:
        size *= d
    return size


def _pack_rows(arrs):
    blocks = []
    for a in arrs:
        flat = a.reshape(-1).astype(F32)
        flat = jnp.concatenate([flat, jnp.zeros((-flat.shape[0] % (8 * LANES),), F32)])
        blocks.append(flat.reshape(-1, LANES))
    return jnp.concatenate(blocks, axis=0)


def _unpack_rows(packed, shapes):
    out, row = [], 0
    for s in shapes:
        rows = -(-_size(s) // (8 * LANES)) * 8
        out.append(packed[row:row + rows].reshape(-1)[:_size(s)].reshape(s))
        row += rows
    return out


WEIGHTS = ['ffn1_w_in', 'ffn1_w_out', 'w_in', 'mu_prev', 'mu_next', 'w0', 'w2', 'a0', 'a2', 'g2', 'k_k', 'k_a', 'r_k',
           'lnx_g', 'lnx_b', 'conv_dw', 'conv_b', 'conv_ln_g', 'conv_ln_b', 'w_out', 'ffn2_w_in', 'ffn2_w_out',
           'ln1_g', 'ln1_b', 'ln2_g', 'ln2_b', 'ln3_g', 'ln3_b']
COL_SHARDED = ('ffn1_w_in', 'w_in', 'ffn2_w_in')
ROW_SHARDED = ('ffn1_w_out', 'w_out', 'ffn2_w_out')
BIG = COL_SHARDED + ROW_SHARDED
SMALL_SHARDED = ('w0', 'w2', 'a0', 'a2', 'g2', 'conv_dw')
REPLICATED = tuple(n for n in WEIGHTS if n not in BIG + SMALL_SHARDED)


def _train_step(x, target, w, m, v, *, tt):
    xi, yi, _ = _mesh_pos()
    q = 2 * xi + yi

    early, mid, late = ("ffn1_w_in", "ffn1_w_out"), ("w_in",) + SMALL_SHARDED, ("w_out", "ffn2_w_in", "ffn2_w_out")
    shard = lambda n: w[n][0].astype(BF16) if n in BIG else w[n][0]

    def whole(n, slabs):
        if n in ROW_SHARDED:
            return slabs.reshape((-1,) + slabs.shape[2:])
        if n in ("ffn1_w_in", "ffn2_w_in"):
            return slabs
        return jnp.moveaxis(slabs, 0, -2).reshape(slabs.shape[1:-1] + (N_CHIPS * slabs.shape[-1],))

    full = {n: w[n][0] for n in REPLICATED}
    first = _gather_chips([shard(n) for n in early], name="gather_ffn1")
    full.update({n: whole(n, g) for n, g in zip(early, first)})
    mid_started = _exchange_start([shard(n) for n in mid], scatter=False, after=first[0], name="gather_mix_start")
    late_started = _exchange_start([shard(n) for n in late], scatter=False, after=mid_started[-1], name="gather_out_start")

    def more_weights(stage, after):
        names, started = (mid, mid_started) if stage == "mix" else (late, late_started)
        own, land = _exchange_wait(started, scatter=False, after=after, name="gather_%s_wait" % stage)
        got = {n: whole(n, _by_chip(o, l)) for n, o, l in zip(names, own, land)}
        full.update(got)
        return got

    sent = []

    def grads_ready(names, slabs):
        started = _exchange_start(slabs, scatter=True, after=slabs[0], name="scatter_%s_start" % names[0])
        sent.append((names, started))
        return started[-1]

    loss_part, grad_x, gr = _local_step(x, target, full, tt=tt, start_token=late_started[-1],
                                        more_weights=more_weights, grads_ready=grads_ready)

    halves = {}
    for names, started in sent:
        stacks, landed = _exchange_wait(started, scatter=True, after=grad_x, name="scatter_%s_wait" % names[0])
        for n, s, l in zip(names, stacks, landed):
            halves[n] = _sum4(lax.dynamic_index_in_dim(s, q, 0, keepdims=False), l, name="sum4_" + n)
    halves = [halves[n] for n in BIG]
    sib = _swap_sibling(halves, name="swap_halves")
    grad, delta, new_m, new_v = {}, {}, {}, {}
    for n, h, hs in zip(BIG, halves, sib):
        outs = _sum_adam(h, hs, w[n][0], m[n][0], v[n][0], name="adam_" + n)
        grad[n], delta[n], new_m[n], new_v[n] = [o[None] for o in outs]

    small_names = REPLICATED + SMALL_SHARDED
    small_full_shapes = [full[n].shape for n in small_names]
    red = _all_reduce_rows(_pack_rows([gr[n] for n in small_names] + [loss_part[0:1, 0:1]]), name="reduce_small")
    *red, loss = _unpack_rows(red, small_full_shapes + [()])
    red = dict(zip(small_names, red))
    gsm = {}
    for n in REPLICATED:
        gsm[n] = red[n].reshape(w[n].shape)
    for n in SMALL_SHARDED:
        width = w[n].shape[-1]
        gsm[n] = lax.dynamic_slice_in_dim(red[n], q * width, width, axis=red[n].ndim - 1).reshape(w[n].shape)
    shapes = [w[n].shape for n in small_names]
    d_p, m_p, v_p = _adam_rows(_pack_rows([w[n] for n in small_names]), _pack_rows([gsm[n] for n in small_names]),
                               _pack_rows([m[n] for n in small_names]), _pack_rows([v[n] for n in small_names]),
                               name="adam_small")
    for n, dd, mm, vv in zip(small_names, _unpack_rows(d_p, shapes), _unpack_rows(m_p, shapes), _unpack_rows(v_p, shapes)):
        grad[n], delta[n], new_m[n], new_v[n] = gsm[n], dd, mm, vv
    return loss, grad_x, grad, delta, new_m, new_v


def kernel(x, ffn1_w_in, ffn1_w_out, w_in, mu_prev, mu_next, w0, w2, a0, a2, g2, k_k, k_a, r_k, lnx_g, lnx_b, conv_dw, conv_b, conv_ln_g, conv_ln_b, w_out, ffn2_w_in, ffn2_w_out, ln1_g, ln1_b, ln2_g, ln2_b, ln3_g, ln3_b, loss_target, m_ffn1_w_in, m_ffn1_w_out, m_w_in, m_mu_prev, m_mu_next, m_w0, m_w2, m_a0, m_a2, m_g2, m_k_k, m_k_a, m_r_k, m_lnx_g, m_lnx_b, m_conv_dw, m_conv_b, m_conv_ln_g, m_conv_ln_b, m_w_out, m_ffn2_w_in, m_ffn2_w_out, m_ln1_g, m_ln1_b, m_ln2_g, m_ln2_b, m_ln3_g, m_ln3_b, v_ffn1_w_in, v_ffn1_w_out, v_w_in, v_mu_prev, v_mu_next, v_w0, v_w2, v_a0, v_a2, v_g2, v_k_k, v_k_a, v_r_k, v_lnx_g, v_lnx_b, v_conv_dw, v_conv_b, v_conv_ln_g, v_conv_ln_b, v_w_out, v_ffn2_w_in, v_ffn2_w_out, v_ln1_g, v_ln1_b, v_ln2_g, v_ln2_b, v_ln3_g, v_ln3_b):
    args = dict(locals())
    w = {n: args[n] for n in WEIGHTS}
    m = {n: args["m_" + n] for n in WEIGHTS}
    v = {n: args["v_" + n] for n in WEIGHTS}
    seq = x.shape[1]
    loss, grad_x, grad, delta, new_m, new_v = _train_step(x, loss_target, w, m, v, tt=min(256, seq))
    return (loss, grad_x, *[grad[n] for n in WEIGHTS], *[delta[n] for n in WEIGHTS],
            *[new_m[n] for n in WEIGHTS], *[new_v[n] for n in WEIGHTS])
```

```python
import functools

import jax
import jax.numpy as jnp
from jax import lax
from jax.experimental import pallas as pl
from jax.experimental.pallas import tpu as pltpu

F32 = jnp.float32
BF16 = jnp.bfloat16

D_MODEL = 1024
RW = 512
HEAD = 64
CW = 512
CONV_K = 31
CONV_PAD = 15
D_FF = 2816
LORA = 64
GATE_LORA = 160
GATE_PAD = 256
SHIFT_COLS = 1952
SHIFT_PAD = 2048
IN_COLS = 2976
IN_PAD = 3072
LN_EPS = 1e-5
GN_EPS = 64e-5
NORM_EPS = 1e-12
ALPHA = 2.0 ** 0.25
DECAY_SCALE = 0.6065306597126334
ADAM_LR, ADAM_B1, ADAM_B2, ADAM_EPS, ADAM_WD, ADAM_STEP = 0.001, 0.9, 0.999, 1e-08, 0.01, 10
N_CHIPS = 4
VMEM_LIMIT = 56 * 1024 * 1024
TM_FFN = 256
TM_LN = 512

MESH = pl.DeviceIdType.MESH


def _cparams(sem=None, **kw):
    return pltpu.CompilerParams(dimension_semantics=sem, vmem_limit_bytes=VMEM_LIMIT, **kw)


LANES = 128


def _pick_tile(dim, want):
    for t in range(min(want, dim) // LANES * LANES, 0, -LANES):
        if dim % t == 0:
            return t
    return dim


def _after_operand(after):
    return ([], []) if after is None else ([pl.BlockSpec(memory_space=pl.ANY)], [after])


def _matmul(a, b, *, ta=False, tb=False, out_dtype=F32, tm=1024, tn=1024, tk=1024, scale=1.0, col_slabs=False,
            after=None, name):
    after_specs, after_args = _after_operand(after)
    if ta:
        k_dim, m_dim = a.shape
    else:
        m_dim, k_dim = a.shape
    n_dim = b.shape[0] if tb else b.shape[1]
    tm, tn, tk = _pick_tile(m_dim, tm), _pick_tile(n_dim, tn), _pick_tile(k_dim, tk)
    assert m_dim % tm == 0 and n_dim % tn == 0 and k_dim % tk == 0, (name, a.shape, b.shape, tm, tn, tk)
    nk = k_dim // tk
    dims = (((0,) if ta else (1,), (1,) if tb else (0,)), ((), ()))
    if col_slabs:
        out_shape = jax.ShapeDtypeStruct((n_dim // tn, m_dim, tn), out_dtype)
        out_spec = pl.BlockSpec((None, tm, tn), lambda i, j, k: (j, i, 0))
    else:
        out_shape = jax.ShapeDtypeStruct((m_dim, n_dim), out_dtype)
        out_spec = pl.BlockSpec((tm, tn), lambda i, j, k: (i, j))

    def body(a_ref, b_ref, *rest):
        o_ref, acc_ref = rest[-2:]
        kk = pl.program_id(2)

        @pl.when(kk == 0)
        def _():
            acc_ref[...] = jnp.zeros_like(acc_ref)

        acc_ref[...] += lax.dot_general(a_ref[...].astype(BF16), b_ref[...].astype(BF16), dims,
                                        preferred_element_type=F32)

        @pl.when(kk == nk - 1)
        def _():
            o_ref[...] = (acc_ref[...] * scale).astype(o_ref.dtype)

    a_spec = pl.BlockSpec((tk, tm), lambda i, j, k: (k, i)) if ta else pl.BlockSpec((tm, tk), lambda i, j, k: (i, k))
    b_spec = pl.BlockSpec((tn, tk), lambda i, j, k: (j, k)) if tb else pl.BlockSpec((tk, tn), lambda i, j, k: (k, j))
    return pl.pallas_call(
        body, name=name,
        out_shape=out_shape,
        grid=(m_dim // tm, n_dim // tn, nk),
        in_specs=[a_spec, b_spec] + after_specs,
        out_specs=out_spec,
        scratch_shapes=[pltpu.VMEM((tm, tn), F32)],
        compiler_params=_cparams(("parallel", "parallel", "arbitrary")),
    )(a, b, *after_args)


def _whole(shape):
    nd = len(shape)
    return pl.BlockSpec(shape, lambda i: (0,) * nd)


def _ffn_in(x, w, *, tm, after=None, name):
    n_tok = x.shape[0]
    sw = w.shape[2]
    tm = min(tm, n_tok)

    after_specs, after_args = _after_operand(after)

    def body(x_ref, w_ref, *rest):
        h_ref, a_ref = rest[-2:]
        xb = x_ref[...].astype(BF16)
        for s in range(2):
            g = jnp.dot(xb, w_ref[s], preferred_element_type=F32)
            u = jnp.dot(xb, w_ref[s + 2], preferred_element_type=F32)
            h_ref[:, s * sw:(s + 1) * sw] = g.astype(BF16)
            h_ref[:, (s + 2) * sw:(s + 3) * sw] = u.astype(BF16)
            a_ref[:, s * sw:(s + 1) * sw] = (_silu(g) * u).astype(BF16)

    return pl.pallas_call(
        body, name=name,
        out_shape=[jax.ShapeDtypeStruct((n_tok, 2 * D_FF), BF16), jax.ShapeDtypeStruct((n_tok, D_FF), BF16)],
        grid=(n_tok // tm,),
        in_specs=[pl.BlockSpec((tm, D_MODEL), lambda i: (i, 0)), _whole(w.shape)] + after_specs,
        out_specs=[pl.BlockSpec((tm, 2 * D_FF), lambda i: (i, 0)), pl.BlockSpec((tm, D_FF), lambda i: (i, 0))],
        compiler_params=_cparams(("parallel",)),
    )(x, w, *after_args)


def _mm_ln(a_list, w, xres, g, b, fscale, *, tm, name):
    n_tok = xres.shape[0]
    tm = min(tm, n_tok)
    na = len(a_list)

    def body(*refs):
        a_refs = refs[:na]
        w_ref, x_ref, g_ref, b_ref, z_o, y_o, yb_o = refs[na:]
        f, off = None, 0
        for a_ref in a_refs:
            k = a_ref.shape[1]
            t = jnp.dot(a_ref[...].astype(BF16), w_ref[off:off + k, :], preferred_element_type=F32)
            f = t if f is None else f + t
            off += k
        z = ALPHA * x_ref[...] + fscale * f
        y = _layer_norm(z, g_ref[...], b_ref[...])
        z_o[...] = z
        y_o[...] = y
        yb_o[...] = y.astype(BF16)

    tile = pl.BlockSpec((tm, D_MODEL), lambda i: (i, 0))
    return pl.pallas_call(
        body, name=name,
        out_shape=[jax.ShapeDtypeStruct((n_tok, D_MODEL), F32)] * 2 + [jax.ShapeDtypeStruct((n_tok, D_MODEL), BF16)],
        grid=(n_tok // tm,),
        in_specs=[pl.BlockSpec((tm, a.shape[1]), lambda i: (i, 0)) for a in a_list]
        + [_whole(w.shape), tile, _whole(g.shape), _whole(b.shape)],
        out_specs=[tile, tile, tile],
        compiler_params=_cparams(("parallel",)),
    )(*a_list, w, xres, g, b)


def _mm_ln_loss(a, w, xres, g, b, target, fscale, *, tm, name):
    n_tok = xres.shape[0]
    tm = min(tm, n_tok)

    def body(a_ref, w_ref, x_ref, g_ref, b_ref, t_ref, dz_o, dg_o, db_o, loss_o):
        i = pl.program_id(0)
        z = ALPHA * x_ref[...] + fscale * jnp.dot(a_ref[...].astype(BF16), w_ref[...], preferred_element_type=F32)
        y, vjp = jax.vjp(_layer_norm, z, g_ref[...], b_ref[...])
        e = y - t_ref[...]
        dz, dg, db = vjp(e * (1.0 / D_MODEL))

        @pl.when(i == 0)
        def _():
            dg_o[...] = jnp.zeros_like(dg_o)
            db_o[...] = jnp.zeros_like(db_o)
            loss_o[...] = jnp.zeros_like(loss_o)
        dz_o[...] = dz
        dg_o[...] += dg
        db_o[...] += db
        loss_o[...] += 0.5 * jnp.sum(jnp.mean(e * e, axis=-1, keepdims=True), axis=0, keepdims=True)

    tile = pl.BlockSpec((tm, D_MODEL), lambda i: (i, 0))
    row = pl.BlockSpec((1, D_MODEL), lambda i: (0, 0))
    return pl.pallas_call(
        body, name=name,
        out_shape=[jax.ShapeDtypeStruct((n_tok, D_MODEL), F32), jax.ShapeDtypeStruct((1, D_MODEL), F32),
                   jax.ShapeDtypeStruct((1, D_MODEL), F32), jax.ShapeDtypeStruct((8, LANES), F32)],
        grid=(n_tok // tm,),
        in_specs=[pl.BlockSpec((tm, a.shape[1]), lambda i: (i, 0)), _whole(w.shape), tile, row, row, tile],
        out_specs=[tile, row, row, pl.BlockSpec((8, LANES), lambda i: (0, 0))],
        compiler_params=_cparams(("arbitrary",)),
    )(a, w, xres, g, b, target)


def _ffn_out_bwd(dz, w, h, *, tm, name):
    n_tok = dz.shape[0]
    tm = min(tm, n_tok)
    cw = D_FF // 2

    def body(dz_ref, w_ref, h_ref, dh_ref):
        dzb = dz_ref[...].astype(BF16)
        for s in range(2):
            dact = 0.5 * lax.dot_general(dzb, w_ref[s * cw:(s + 1) * cw, :], (((1,), (1,)), ((), ())),
                                         preferred_element_type=F32)
            gate = h_ref[:, s * cw:(s + 1) * cw].astype(F32)
            up = h_ref[:, D_FF + s * cw:D_FF + (s + 1) * cw].astype(F32)
            sg = _sigmoid(gate)
            dh_ref[:, s * cw:(s + 1) * cw] = (dact * up * sg * (1.0 + gate * (1.0 - sg))).astype(BF16)
            dh_ref[:, D_FF + s * cw:D_FF + (s + 1) * cw] = (dact * gate * sg).astype(BF16)

    wide = pl.BlockSpec((tm, 2 * D_FF), lambda i: (i, 0))
    return pl.pallas_call(
        body, name=name,
        out_shape=jax.ShapeDtypeStruct((n_tok, 2 * D_FF), BF16),
        grid=(n_tok // tm,),
        in_specs=[pl.BlockSpec((tm, D_MODEL), lambda i: (i, 0)), _whole(w.shape), wide],
        out_specs=wide,
        compiler_params=_cparams(("parallel",)),
    )(dz, w, h)


def _mm_nt_res(a_list, w, dz, *, tm, ln=None, after=None, name):
    n_tok = dz.shape[0]
    tm = min(tm, n_tok)
    na = len(a_list)
    nt = (((1,), (1,)), ((), ()))
    after_specs, after_args = _after_operand(after)
    n_out = 1 if ln is None else 3

    def body(*refs):
        a_refs = refs[:na]
        w_ref, dz_ref, o_ref = refs[na], refs[na + 1], refs[-n_out]
        acc = ALPHA * dz_ref[...]
        if len(w_ref.shape) == 3:
            cw = w_ref.shape[2]
            for s in range(w_ref.shape[0]):
                acc = acc + lax.dot_general(a_refs[0][:, s * cw:(s + 1) * cw], w_ref[s], nt, preferred_element_type=F32)
        else:
            off = 0
            for a_ref in a_refs:
                k = a_ref.shape[1]
                acc = acc + lax.dot_general(a_ref[...], w_ref[:, off:off + k], nt, preferred_element_type=F32)
                off += k
        if ln is None:
            o_ref[...] = acc
            return
        z_ref, g_ref, b_ref = refs[na + 2:na + 5]
        dg_o, db_o = refs[-2:]
        _, vjp = jax.vjp(_layer_norm, z_ref[...], g_ref[...], b_ref[...])
        o_ref[...], dg, db = vjp(acc)

        @pl.when(pl.program_id(0) == 0)
        def _():
            dg_o[...] = jnp.zeros_like(dg_o)
            db_o[...] = jnp.zeros_like(db_o)
        dg_o[...] += dg
        db_o[...] += db

    tile = pl.BlockSpec((tm, D_MODEL), lambda i: (i, 0))
    row = pl.BlockSpec((1, D_MODEL), lambda i: (0, 0))
    out_shape = [jax.ShapeDtypeStruct((n_tok, D_MODEL), F32)]
    ln_specs, ln_args, out_specs = [], [], [tile]
    if ln is not None:
        ln_specs, ln_args = [tile, row, row], list(ln)
        out_shape += [jax.ShapeDtypeStruct((1, D_MODEL), F32)] * 2
        out_specs += [row, row]
    outs = pl.pallas_call(
        body, name=name,
        out_shape=out_shape,
        grid=(n_tok // tm,),
        in_specs=[pl.BlockSpec((tm, a.shape[1]), lambda i: (i, 0)) for a in a_list] + [_whole(w.shape), tile]
        + ln_specs + after_specs,
        out_specs=out_specs,
        compiler_params=_cparams(("parallel",) if ln is None else ("arbitrary",)),
    )(*a_list, w, dz, *ln_args, *after_args)
    return outs[0] if ln is None else outs


def _rowcall(fn, tok_in, full_in, tok_out, acc_out, *, tt, name):
    views = [a if isinstance(a, tuple) else (a, a.shape[1], 0) for a in tok_in]
    tok_in = [a for a, _, _ in views]
    n_tok = tok_in[0].shape[0]
    assert n_tok % tt == 0, (name, n_tok, tt)
    n_ti, n_fi, n_to = len(tok_in), len(full_in), len(tok_out)

    def body(*refs):
        i = pl.program_id(0)
        ins = [r[...] for r in refs[:n_ti + n_fi]]
        outs = fn(i, *ins)
        o_refs = refs[n_ti + n_fi:]
        for r, val in zip(o_refs[:n_to], outs[:n_to]):
            r[...] = val.astype(r.dtype)
        if acc_out:
            @pl.when(i == 0)
            def _():
                for r in o_refs[n_to:]:
                    r[...] = jnp.zeros_like(r)
            for r, val in zip(o_refs[n_to:], outs[n_to:]):
                r[...] += val.reshape(r.shape).astype(F32)

    in_specs = [pl.BlockSpec((tt, width), functools.partial(lambda k, i: (i, k), k)) for _, width, k in views]
    in_specs += [pl.BlockSpec(a.shape, lambda i: (0, 0)) for a in full_in]
    out_specs = [pl.BlockSpec((tt, c), lambda i: (i, 0)) for c, _ in tok_out]
    out_specs += [pl.BlockSpec(s, lambda i: (0, 0)) for s in acc_out]
    out_shape = [jax.ShapeDtypeStruct((n_tok, c), dt) for c, dt in tok_out]
    out_shape += [jax.ShapeDtypeStruct(s, F32) for s in acc_out]
    return pl.pallas_call(
        body, name=name, out_shape=out_shape, grid=(n_tok // tt,), in_specs=in_specs, out_specs=out_specs,
        compiler_params=_cparams(("arbitrary",) if acc_out else ("parallel",)),
    )(*tok_in, *full_in)


@jax.custom_vjp
def _bdot(a, b):
    return jnp.dot(a.astype(BF16), b.astype(BF16), preferred_element_type=F32)


def _bdot_fwd(a, b):
    return _bdot(a, b), (a, b)


def _bdot_bwd(res, g):
    a, b = res
    g16 = g.astype(BF16)
    da = lax.dot_general(g16, b.astype(BF16), (((1,), (1,)), ((), ())), preferred_element_type=F32)
    db = lax.dot_general(a.astype(BF16), g16, (((0,), (0,)), ((), ())), preferred_element_type=F32)
    return da, db


_bdot.defvjp(_bdot_fwd, _bdot_bwd)


def _split16(x):
    hi = x.astype(BF16)
    lo = (x - hi.astype(F32)).astype(BF16)
    return hi, lo


def _segsum_raw(x, e2):
    hi, lo = _split16(x)
    outs = []
    for c in range(x.shape[1] // 256):
        lhs = jnp.concatenate([hi[:, 256 * c:256 * (c + 1)], lo[:, 256 * c:256 * (c + 1)]], axis=1)
        outs.append(jnp.dot(lhs, e2, preferred_element_type=F32))
    return jnp.concatenate(outs, axis=1)


@jax.custom_vjp
def _segsum(x, e2):
    return _segsum_raw(x, e2)


def _segsum_fwd(x, e2):
    return _segsum_raw(x, e2), e2


def _segsum_bwd(e2, g):
    return _segsum_raw(g, e2), jnp.zeros_like(e2)


_segsum.defvjp(_segsum_fwd, _segsum_bwd)


def _head_ones():
    r = lax.broadcasted_iota(jnp.int32, (512, 256), 0) % 256
    c = lax.broadcasted_iota(jnp.int32, (512, 256), 1)
    return (r // HEAD == c // HEAD).astype(BF16)


def _sigmoid(x):
    return 1.0 / (1.0 + jnp.exp(-x))


def _silu(x):
    return x * _sigmoid(x)


def _layer_norm(z, g, b, eps=LN_EPS):
    mu = jnp.mean(z, axis=-1, keepdims=True)
    zc = z - mu
    var = jnp.mean(zc * zc, axis=-1, keepdims=True)
    return zc * lax.rsqrt(var + eps) * g + b


def _prep(ps, w2b, w0c, a2b, a0c, g2p, k_k, k_a, e2):
    r, k, v = ps[:, 0:512], ps[:, 512:1024], ps[:, 1024:1536]
    wd, ad, gd = ps[:, 1536:1664], ps[:, 1664:1792], ps[:, 1792:2048]
    lw = _bdot(jnp.tanh(wd), w2b) + w0c
    decay = -DECAY_SCALE * _sigmoid(lw)
    a = _sigmoid(_bdot(ad, a2b) + a0c)
    g = _bdot(_sigmoid(gd), g2p)
    kkr = k * k_k
    nrm = jnp.sqrt(_segsum(kkr * kkr, e2))
    kk = kkr / jnp.maximum(nrm, NORM_EPS)
    k2 = jnp.concatenate([k, k], axis=1)
    ka2 = jnp.concatenate([k_a, k_a], axis=1)
    kd = k2 * (1.0 + (a - 1.0) * ka2)
    b = jnp.concatenate([kk, kk], axis=1) * a
    return r, v, kk, decay, kd, b, g


def _post(y0, y1, r, v, kd, g, lnx_g, lnx_b, r_k, e2):
    y = y0 + y1
    mu = _segsum(y, e2) * (1.0 / HEAD)
    yc = y - mu
    var = _segsum(yc * yc, e2) * (1.0 / HEAD)
    yn = yc * lax.rsqrt(var + GN_EPS) * lnx_g + lnx_b
    bonus = _segsum(r * (kd[:, :RW] + kd[:, RW:]) * r_k, e2)
    return (yn + bonus * v) * g


def _conv_post(yc, ln_g, ln_b):
    return _silu(_layer_norm(yc, ln_g, ln_b))


def _halo_specs(cols_block, hb, tt, n_tok, col_idx):
    nb = n_tok // hb
    prev = pl.BlockSpec((hb, cols_block), lambda i: (jnp.maximum(i * (tt // hb) - 1, 0), col_idx))
    nxt = pl.BlockSpec((hb, cols_block), lambda i: (jnp.minimum((i + 1) * (tt // hb), nb - 1), col_idx))
    return prev, nxt


def _mix_prep(p, mu_p, mu_n, w2b, w0c, a2b, a0c, g2p, k_k, k_a, *, seq, tt, name):
    n_tok = p.shape[0]
    tps = seq // tt
    e2 = _head_ones()

    def body(p_ref, hp_ref, hn_ref, mup_ref, mun_ref, w2b_ref, w0c_ref, a2b_ref, a0c_ref, g2p_ref, kk_ref, ka_ref,
             e2_ref, r_o, v_o, kk_o, w_o, kd_o, b_o, g_o, ext):
        i = pl.program_id(0)
        first = (i % tps) == 0
        last = (i % tps) == tps - 1
        pv = p_ref[...]
        ext[pl.ds(0, 8), :] = jnp.where(first, 0.0, hp_ref[...])
        ext[pl.ds(8, tt), :] = pv
        ext[pl.ds(8 + tt, 8), :] = jnp.where(last, 0.0, hn_ref[...])
        prev = ext[pl.ds(7, tt), :]
        nxt = ext[pl.ds(9, tt), :]
        ps = pv + mup_ref[...] * (prev - pv) + mun_ref[...] * (nxt - pv)
        outs = _prep(ps, w2b_ref[...], w0c_ref[...], a2b_ref[...], a0c_ref[...], g2p_ref[...], kk_ref[...],
                     ka_ref[...], e2_ref[...])
        for o_ref, val in zip((r_o, v_o, kk_o, w_o, kd_o, b_o, g_o), outs):
            o_ref[...] = val

    hp, hn = _halo_specs(SHIFT_PAD, 8, tt, n_tok, 0)
    fulls = [mu_p, mu_n, w2b, w0c, a2b, a0c, g2p, k_k, k_a, e2]
    widths = (RW, RW, RW, 2 * RW, 2 * RW, 2 * RW, RW)
    return pl.pallas_call(
        body, name=name,
        out_shape=[jax.ShapeDtypeStruct((n_tok, c), F32) for c in widths],
        grid=(n_tok // tt,),
        in_specs=[pl.BlockSpec((tt, SHIFT_PAD), lambda i: (i, 0)), hp, hn]
        + [pl.BlockSpec(a.shape, lambda i: (0, 0)) for a in fulls],
        out_specs=[pl.BlockSpec((tt, c), lambda i: (i, 0)) for c in widths],
        scratch_shapes=[pltpu.VMEM((tt + 16, SHIFT_PAD), F32)],
        compiler_params=_cparams(("parallel",)),
    )(p, p, p, *fulls)


def _mix_prep_bwd(p, mu_p, mu_n, w2b, w0c, a2b, a0c, g2p, k_k, k_a, ct_terms, *, seq, tt, name):
    n_tok = p.shape[0]
    tps = seq // tt
    e2 = _head_ones()
    acc_shapes = [w2b.shape, w0c.shape, a2b.shape, a0c.shape, g2p.shape, k_k.shape, k_a.shape]
    cts = [a for terms in ct_terms for t in terms for a in (t if isinstance(t, tuple) else (t,))]

    def body(p_ref, hp_ref, hn_ref, mup_ref, mun_ref, w2b_ref, w0c_ref, a2b_ref, a0c_ref, g2p_ref, kk_ref, ka_ref,
             e2_ref, *rest):
        ct_refs, dps_o, acc_refs, ext = rest[:len(cts)], rest[len(cts)], rest[len(cts) + 1:-1], rest[-1]
        ct_it = iter(ct_refs)
        ct_vals = []
        for terms in ct_terms:
            total = None
            for t in terms:
                if isinstance(t, tuple):
                    val = jnp.concatenate([next(ct_it)[...] for _ in t], axis=1)
                else:
                    val = next(ct_it)[...]
                total = val if total is None else total + val
            ct_vals.append(total)
        i = pl.program_id(0)
        first = (i % tps) == 0
        last = (i % tps) == tps - 1
        pv = p_ref[...]
        ext[pl.ds(0, 8), :] = jnp.where(first, 0.0, hp_ref[...])
        ext[pl.ds(8, tt), :] = pv
        ext[pl.ds(8 + tt, 8), :] = jnp.where(last, 0.0, hn_ref[...])
        prev = ext[pl.ds(7, tt), :]
        nxt = ext[pl.ds(9, tt), :]
        ps = pv + mup_ref[...] * (prev - pv) + mun_ref[...] * (nxt - pv)
        e2v = e2_ref[...]
        _, vjp = jax.vjp(lambda *a: _prep(*a, e2v), ps, w2b_ref[...], w0c_ref[...], a2b_ref[...], a0c_ref[...],
                         g2p_ref[...], kk_ref[...], ka_ref[...])
        grads = vjp(tuple(ct_vals))
        dps_o[...] = grads[0]

        @pl.when(i == 0)
        def _():
            for r in acc_refs:
                r[...] = jnp.zeros_like(r)
        for r, val in zip(acc_refs, grads[1:]):
            r[...] += val

    hp, hn = _halo_specs(SHIFT_PAD, 8, tt, n_tok, 0)
    fulls = [mu_p, mu_n, w2b, w0c, a2b, a0c, g2p, k_k, k_a, e2]
    return pl.pallas_call(
        body, name=name,
        out_shape=[jax.ShapeDtypeStruct((n_tok, SHIFT_PAD), F32)] + [jax.ShapeDtypeStruct(s, F32) for s in acc_shapes],
        grid=(n_tok // tt,),
        in_specs=[pl.BlockSpec((tt, SHIFT_PAD), lambda i: (i, 0)), hp, hn]
        + [pl.BlockSpec(a.shape, lambda i: (0, 0)) for a in fulls]
        + [pl.BlockSpec((tt, c.shape[1]), lambda i: (i, 0)) for c in cts],
        out_specs=[pl.BlockSpec((tt, SHIFT_PAD), lambda i: (i, 0))] + [pl.BlockSpec(s, lambda i: (0, 0)) for s in acc_shapes],
        scratch_shapes=[pltpu.VMEM((tt + 16, SHIFT_PAD), F32)],
        compiler_params=_cparams(("arbitrary",)),
    )(p, p, p, *fulls, *cts)


def _shift_bwd(dps, p, mu_p, mu_n, *, seq, tt, name):
    n_tok = p.shape[0]
    tps = seq // tt

    def body(d_ref, dhp_ref, dhn_ref, p_ref, php_ref, phn_ref, mup_ref, mun_ref, dp_o, dmup_o, dmun_o, ext):
        i = pl.program_id(0)
        first = (i % tps) == 0
        last = (i % tps) == tps - 1
        mup, mun = mup_ref[...], mun_ref[...]
        dv = d_ref[...]
        pv = p_ref[...]
        ext[pl.ds(0, 8), :] = jnp.where(first, 0.0, dhp_ref[...])
        ext[pl.ds(8, tt), :] = dv
        ext[pl.ds(8 + tt, 8), :] = jnp.where(last, 0.0, dhn_ref[...])
        d_prev = ext[pl.ds(7, tt), :]
        d_next = ext[pl.ds(9, tt), :]
        dp_o[...] = (dv * (1.0 - mup - mun) + d_next * mup + d_prev * mun).astype(dp_o.dtype)
        ext[pl.ds(0, 8), :] = jnp.where(first, 0.0, php_ref[...])
        ext[pl.ds(8, tt), :] = pv
        ext[pl.ds(8 + tt, 8), :] = jnp.where(last, 0.0, phn_ref[...])
        p_prev = ext[pl.ds(7, tt), :]
        p_next = ext[pl.ds(9, tt), :]

        @pl.when(i == 0)
        def _():
            dmup_o[...] = jnp.zeros_like(dmup_o)
            dmun_o[...] = jnp.zeros_like(dmun_o)
        dmup_o[...] += jnp.sum(dv * (p_prev - pv), axis=0, keepdims=True)
        dmun_o[...] += jnp.sum(dv * (p_next - pv), axis=0, keepdims=True)

    hp, hn = _halo_specs(SHIFT_PAD, 8, tt, n_tok, 0)
    tile = pl.BlockSpec((tt, SHIFT_PAD), lambda i: (i, 0))
    full = pl.BlockSpec((1, SHIFT_PAD), lambda i: (0, 0))
    return pl.pallas_call(
        body, name=name,
        out_shape=[jax.ShapeDtypeStruct((n_tok, SHIFT_PAD), BF16), jax.ShapeDtypeStruct((1, SHIFT_PAD), F32),
                   jax.ShapeDtypeStruct((1, SHIFT_PAD), F32)],
        grid=(n_tok // tt,),
        in_specs=[tile, hp, hn, tile, hp, hn, full, full],
        out_specs=[tile, full, full],
        scratch_shapes=[pltpu.VMEM((tt + 16, SHIFT_PAD), F32)],
        compiler_params=_cparams(("arbitrary",)),
    )(dps, dps, dps, p, p, p, mu_p, mu_n)


def _mix_post(y0, y1, r, v, kd, g, lnx_g, lnx_b, r_k, *, tt, name):
    e2 = _head_ones()
    return _rowcall(lambda i, *a: (_post(*a),), [y0, y1, r, v, kd, g], [lnx_g, lnx_b, r_k, e2], [(RW, BF16)], [],
                    tt=tt, name=name)[0]


def _mix_post_bwd(y0, y1, r, v, kd, g, lnx_g, lnx_b, r_k, dout, *, tt, name):
    e2 = _head_ones()

    def fn(i, y0v, y1v, rv, vv, kdv, gv, dov, lg, lb, rk, e2v):
        _, vjp = jax.vjp(lambda *a: _post(*a, e2v), y0v, y1v, rv, vv, kdv, gv, lg, lb, rk)
        gr = vjp(dov.astype(F32))
        return gr[0], gr[2], gr[3], gr[4], gr[5], gr[6], gr[7], gr[8]
    return _rowcall(fn, [y0, y1, r, v, kd, g, dout], [lnx_g, lnx_b, r_k, e2],
                    [(RW, F32), (RW, F32), (RW, F32), (2 * RW, F32), (RW, F32)], [(1, RW), (1, RW), (1, RW)],
                    tt=tt, name=name)


def _conv_fwd(p, dw, db, ln_g, ln_b, *, seq, tt, name):
    n_tok = p.shape[0]
    tps = seq // tt

    def glu(x, gate):
        return x * _sigmoid(gate)

    def body(u_ref, g_ref, uhp, ghp, uhn, ghn, dw_ref, db_ref, lg_ref, lb_ref, yc_o, y_o, ext):
        i = pl.program_id(0)
        first = (i % tps) == 0
        last = (i % tps) == tps - 1
        ext[pl.ds(0, 16), :] = jnp.where(first, 0.0, glu(uhp[...], ghp[...]))
        ext[pl.ds(16, tt), :] = glu(u_ref[...], g_ref[...])
        ext[pl.ds(16 + tt, 16), :] = jnp.where(last, 0.0, glu(uhn[...], ghn[...]))
        acc = jnp.zeros((tt, CW), F32) + db_ref[...]
        for k in range(CONV_K):
            acc = acc + ext[pl.ds(k + 1, tt), :] * dw_ref[pl.ds(k, 1), :]
        yc_o[...] = acc
        y_o[...] = _conv_post(acc, lg_ref[...], lb_ref[...]).astype(y_o.dtype)

    uhp_s, uhn_s = _halo_specs(CW, 16, tt, n_tok, 4)
    ghp_s, ghn_s = _halo_specs(CW, 16, tt, n_tok, 5)
    fulls = [dw, db, ln_g, ln_b]
    return pl.pallas_call(
        body, name=name,
        out_shape=[jax.ShapeDtypeStruct((n_tok, CW), F32), jax.ShapeDtypeStruct((n_tok, CW), BF16)],
        grid=(n_tok // tt,),
        in_specs=[pl.BlockSpec((tt, CW), lambda i: (i, 4)), pl.BlockSpec((tt, CW), lambda i: (i, 5)),
                  uhp_s, ghp_s, uhn_s, ghn_s] + [pl.BlockSpec(a.shape, lambda i: (0, 0)) for a in fulls],
        out_specs=[pl.BlockSpec((tt, CW), lambda i: (i, 0)), pl.BlockSpec((tt, CW), lambda i: (i, 0))],
        scratch_shapes=[pltpu.VMEM((tt + 32, CW), F32)],
        compiler_params=_cparams(("parallel",)),
    )(p, p, p, p, p, p, *fulls)


def _conv_post_bwd(yc, dy, ln_g, ln_b, *, tt, name):
    def fn(i, ycv, dyv, lg, lb):
        _, vjp = jax.vjp(_conv_post, ycv, lg, lb)
        dyc, dg, dbb = vjp(dyv.astype(F32))
        return dyc, dg, dbb, jnp.sum(dyc, axis=0, keepdims=True)
    return _rowcall(fn, [yc, dy], [ln_g, ln_b], [(CW, F32)], [(1, CW), (1, CW), (1, CW)], tt=tt, name=name)


def _conv_bwd(dyc, p, dw, *, seq, tt, name):
    n_tok = p.shape[0]
    tps = seq // tt

    def body(d_ref, dhp, dhn, u_ref, g_ref, uhp, ghp, uhn, ghn, dw_ref, dp_o, ddw_o, ext):
        i = pl.program_id(0)
        first = (i % tps) == 0
        last = (i % tps) == tps - 1
        dv = d_ref[...]
        ext[pl.ds(0, 16), :] = jnp.where(first, 0.0, dhp[...])
        ext[pl.ds(16, tt), :] = dv
        ext[pl.ds(16 + tt, 16), :] = jnp.where(last, 0.0, dhn[...])
        du = jnp.zeros((tt, CW), F32)
        for k in range(CONV_K):
            du = du + ext[pl.ds(31 - k, tt), :] * dw_ref[pl.ds(k, 1), :]
        uv, gv = u_ref[...], g_ref[...]
        sg = _sigmoid(gv)
        dp_o[:, 0:CW] = (du * sg).astype(dp_o.dtype)
        dp_o[:, CW:2 * CW] = (du * uv * sg * (1.0 - sg)).astype(dp_o.dtype)
        ext[pl.ds(0, 16), :] = jnp.where(first, 0.0, uhp[...] * _sigmoid(ghp[...]))
        ext[pl.ds(16, tt), :] = uv * sg
        ext[pl.ds(16 + tt, 16), :] = jnp.where(last, 0.0, uhn[...] * _sigmoid(ghn[...]))

        @pl.when(i == 0)
        def _():
            ddw_o[...] = jnp.zeros_like(ddw_o)
        for k in range(CONV_K):
            ddw_o[pl.ds(k, 1), :] += jnp.sum(dv * ext[pl.ds(k + 1, tt), :], axis=0, keepdims=True)

    dhp_s, dhn_s = _halo_specs(CW, 16, tt, n_tok, 0)
    uhp_s, uhn_s = _halo_specs(CW, 16, tt, n_tok, 4)
    ghp_s, ghn_s = _halo_specs(CW, 16, tt, n_tok, 5)
    return pl.pallas_call(
        body, name=name,
        out_shape=[jax.ShapeDtypeStruct((n_tok, 2 * CW), BF16), jax.ShapeDtypeStruct((32, CW), F32)],
        grid=(n_tok // tt,),
        in_specs=[pl.BlockSpec((tt, CW), lambda i: (i, 0)), dhp_s, dhn_s,
                  pl.BlockSpec((tt, CW), lambda i: (i, 4)), pl.BlockSpec((tt, CW), lambda i: (i, 5)),
                  uhp_s, ghp_s, uhn_s, ghn_s, pl.BlockSpec(dw.shape, lambda i: (0, 0))],
        out_specs=[pl.BlockSpec((tt, 2 * CW), lambda i: (i, 0)), pl.BlockSpec((32, CW), lambda i: (0, 0))],
        scratch_shapes=[pltpu.VMEM((tt + 32, CW), F32)],
        compiler_params=_cparams(("arbitrary",)),
    )(dyc, dyc, dyc, p, p, p, p, p, p, dw)


CHUNK = 64
_MM_DIMS = {"nn": (((2,), (1,)), ((0,), (0,))), "nt": (((2,), (2,)), ((0,), (0,))), "tn": (((1,), (1,)), ((0,), (0,)))}


def _mm16_raw(a, b, mode, fine):
    dot = lambda x, y: lax.dot_general(x, y, _MM_DIMS[mode], preferred_element_type=F32)
    if not fine:
        return dot(a.astype(BF16), b.astype(BF16))
    ah, (bh, bl) = a.astype(BF16), _split16(b)
    return dot(ah, bh) + dot(ah, bl)


@functools.partial(jax.custom_vjp, nondiff_argnums=(2, 3))
def _mm16(a, b, mode, fine=False):
    return _mm16_raw(a, b, mode, fine)


def _mm16_fwd(a, b, mode, fine):
    return _mm16_raw(a, b, mode, fine), (a, b)


def _mm16_bwd(mode, fine, res, g):
    a, b = res
    if mode == "nn":
        return _mm16_raw(g, b, "nt", fine), _mm16_raw(a, g, "tn", fine)
    if mode == "nt":
        return _mm16_raw(g, b, "nn", fine), _mm16_raw(g, a, "tn", fine)
    return _mm16_raw(b, g, "nt", fine), _mm16_raw(a, g, "nn", fine)


_mm16.defvjp(_mm16_fwd, _mm16_bwd)


def _tri_sum_raw(x, tri, mode):
    hi = x.astype(BF16)
    r1 = x - hi.astype(F32)
    mid = r1.astype(BF16)
    lo = (r1 - mid.astype(F32)).astype(BF16)
    dot = lambda p: lax.dot_general(tri, p, _MM_DIMS[mode], preferred_element_type=F32)
    return dot(hi) + dot(mid) + dot(lo)


@jax.custom_vjp
def _tri_sum(x, tri):
    return _tri_sum_raw(x, tri, "nn")


def _tri_sum_fwd(x, tri):
    return _tri_sum_raw(x, tri, "nn"), tri


def _tri_sum_bwd(tri, g):
    return _tri_sum_raw(g, tri, "tn"), jnp.zeros_like(tri)


_tri_sum.defvjp(_tri_sum_fwd, _tri_sum_bwd)


def _chunk_step(s0, r, lw, k, v, kk, b, tri, rev):
    nh, n, _ = r.shape
    row = lax.broadcasted_iota(jnp.int32, (nh, n, n), 1)
    col = lax.broadcasted_iota(jnp.int32, (nh, n, n), 2)
    if rev:
        row, col = col, row
    cum = _tri_sum(lw, tri)
    up, down = jnp.exp(cum), jnp.exp(-cum)
    at, rt = -kk * jnp.exp(cum - lw), r * up
    kt, bt = k * down, b * down
    a_ab = jnp.where(col < row, _mm16(at, bt, "nt", True), 0.0)
    a_ak = jnp.where(col < row, _mm16(at, kt, "nt", True), 0.0)
    a_rb = jnp.where(col <= row, _mm16(rt, bt, "nt", True), 0.0)
    a_rk = jnp.where(col <= row, _mm16(rt, kt, "nt", True), 0.0)
    u = _mm16(at, s0, "nt") + _mm16(a_ak, v, "nn")
    power = a_ab
    steps = n.bit_length() - 1
    for it in range(steps):
        u = u + _mm16(power, u, "nn")
        if it + 1 < steps:
            power = _mm16(power, power, "nn")
    y = _mm16(rt, s0, "nt") + _mm16(a_rk, v, "nn") + _mm16(a_rb, u, "nn")
    grown = s0 + _mm16(v, kt, "tn") + _mm16(u, bt, "tn")
    return y, grown * jnp.exp(jnp.sum(lw, axis=1, keepdims=True))


N_HEADS = RW // HEAD


SCAN_SEQS = 2


def _tri_ones(rev, nseq):
    shape = (nseq * N_HEADS, CHUNK, CHUNK)
    row, col = lax.broadcasted_iota(jnp.int32, shape, 1), lax.broadcasted_iota(jnp.int32, shape, 2)
    return ((col >= row) if rev else (col <= row)).astype(BF16)


def _split_heads(ref):
    return jnp.stack([ref[q, :, pl.ds(h * HEAD, HEAD)] for q in range(ref.shape[0]) for h in range(N_HEADS)])


def _merge_heads(ref, val):
    for q in range(ref.shape[0]):
        for h in range(N_HEADS):
            ref[q, :, pl.ds(h * HEAD, HEAD)] = val[q * N_HEADS + h]


def _chunk_specs(nseq, nc, rev, dcol):
    chunk = (lambda c: nc - 1 - c) if rev else (lambda c: c)
    shared = pl.BlockSpec((nseq, CHUNK, RW), lambda s, c: (s, chunk(c), 0))
    own = pl.BlockSpec((nseq, CHUNK, RW), lambda s, c: (s, chunk(c), dcol))
    return shared, own


def _wkv_chunk_fwd(r, lw, k, v, kk, b, *, rev, name):
    bsz, seq, _ = r.shape
    nc = seq // CHUNK
    nseq = SCAN_SEQS if bsz % SCAN_SEQS == 0 else 1
    shared, own = _chunk_specs(nseq, nc, rev, int(rev))

    def body(r_ref, lw_ref, k_ref, v_ref, kk_ref, b_ref, tri_ref, y_o, s0_o, s_ref):
        @pl.when(pl.program_id(1) == 0)
        def _():
            s_ref[...] = jnp.zeros_like(s_ref)

        s0 = s_ref[...]
        s0_o[:, 0] = s0.reshape(nseq, N_HEADS, HEAD, HEAD)
        y, s_ref[...] = _chunk_step(s0, *[_split_heads(x) for x in (r_ref, lw_ref, k_ref, v_ref, kk_ref, b_ref)],
                                    tri_ref[...], rev)
        _merge_heads(y_o, y)

    return pl.pallas_call(
        body, name=name,
        out_shape=[jax.ShapeDtypeStruct((bsz, seq, RW), F32), jax.ShapeDtypeStruct((bsz, nc, N_HEADS, HEAD, HEAD), F32)],
        grid=(bsz // nseq, nc),
        in_specs=[shared, own, own, shared, shared, own,
                  pl.BlockSpec((nseq * N_HEADS, CHUNK, CHUNK), lambda s, c: (0, 0, 0))],
        out_specs=[shared, pl.BlockSpec((nseq, 1, N_HEADS, HEAD, HEAD), lambda s, c: (s, c, 0, 0, 0))],
        scratch_shapes=[pltpu.VMEM((nseq * N_HEADS, HEAD, HEAD), F32)],
        compiler_params=_cparams(("parallel", "arbitrary")),
    )(r, lw, k, v, kk, b, _tri_ones(rev, nseq))


def _wkv_chunk_bwd(r, lw, k, v, kk, b, dy, s0, *, rev, name):
    bsz, seq, _ = r.shape
    nc = seq // CHUNK
    nseq = SCAN_SEQS if bsz % SCAN_SEQS == 0 else 1
    shared, own = _chunk_specs(nseq, nc, not rev, int(rev))

    def body(r_ref, lw_ref, k_ref, v_ref, kk_ref, b_ref, dy_ref, s0_ref, tri_ref, *rest):
        outs, ds_ref = rest[:-1], rest[-1]

        @pl.when(pl.program_id(1) == 0)
        def _():
            ds_ref[...] = jnp.zeros_like(ds_ref)

        triv = tri_ref[...]
        _, vjp = jax.vjp(lambda *a: _chunk_step(*a, triv, rev), s0_ref[:, 0].reshape(nseq * N_HEADS, HEAD, HEAD),
                         *[_split_heads(x) for x in (r_ref, lw_ref, k_ref, v_ref, kk_ref, b_ref)])
        grads = vjp((_split_heads(dy_ref), ds_ref[...]))
        ds_ref[...] = grads[0]
        for o, gval in zip(outs, grads[1:]):
            _merge_heads(o, gval)

    return pl.pallas_call(
        body, name=name,
        out_shape=[jax.ShapeDtypeStruct((bsz, seq, RW), F32)] * 6,
        grid=(bsz // nseq, nc),
        in_specs=[shared, own, own, shared, shared, own, shared,
                  pl.BlockSpec((nseq, 1, N_HEADS, HEAD, HEAD), lambda s, c: (s, nc - 1 - c, 0, 0, 0)),
                  pl.BlockSpec((nseq * N_HEADS, CHUNK, CHUNK), lambda s, c: (0, 0, 0))],
        out_specs=[shared] * 6,
        scratch_shapes=[pltpu.VMEM((nseq * N_HEADS, HEAD, HEAD), F32)],
        compiler_params=_cparams(("parallel", "arbitrary")),
    )(r, lw, k, v, kk, b, dy, s0, _tri_ones(rev, nseq))


def _block_diag2(w):
    z = jnp.zeros_like(w[0])
    return jnp.concatenate([jnp.concatenate([w[0], z], axis=1), jnp.concatenate([z, w[1]], axis=1)], axis=0)


def _pad_in_cols(a):
    z = jnp.zeros(a.shape[:-1] + (SHIFT_PAD - SHIFT_COLS,), a.dtype)
    return jnp.concatenate([a[..., :SHIFT_COLS], z, a[..., SHIFT_COLS:]], axis=-1)


def _follow(small, token):
    return small if token is None else small + token[0:1, 0:1]


def _local_step(x, target, wts, *, tt, start_token=None, more_weights=None, grads_ready=None):
    bsz, seq, _ = x.shape
    n_tok = bsz * seq
    row = lambda a: a.reshape(1, -1).astype(F32)
    x0 = x.reshape(n_tok, D_MODEL)
    tgt = target.reshape(n_tok, D_MODEL)
    ln = {k: row(wts[k]) for k in ("ln1_g", "ln1_b", "ln2_g", "ln2_b", "ln3_g", "ln3_b")}
    if grads_ready is None:
        grads_ready = lambda names, slabs: None

    w1i, w1o = wts["ffn1_w_in"], wts["ffn1_w_out"]
    h1, act1 = _ffn_in(x0, w1i, tm=TM_FFN, after=start_token, name="ffn1_in")
    z1, x1, x1b = _mm_ln([act1], w1o, x0, ln["ln1_g"], ln["ln1_b"], 0.5, tm=TM_LN, name="ffn1_out_ln1")
    if more_weights is not None:
        wts = {**wts, **more_weights("mix", x1b)}
    win = _pad_in_cols(wts["w_in"])
    zpad = jnp.zeros((1, SHIFT_PAD - SHIFT_COLS), F32)
    mu_p = jnp.concatenate([row(wts["mu_prev"]), zpad], axis=1)
    mu_n = jnp.concatenate([row(wts["mu_next"]), zpad], axis=1)
    w2b, a2b = _block_diag2(wts["w2"]), _block_diag2(wts["a2"])
    w0c, a0c = row(wts["w0"]), row(wts["a0"])
    g2p = jnp.concatenate([wts["g2"], jnp.zeros((GATE_PAD - GATE_LORA, RW), F32)], axis=0)
    k_k, k_a, r_k = row(wts["k_k"]), row(wts["k_a"]), row(wts["r_k"])
    lnx_g, lnx_b = row(wts["lnx_g"]), row(wts["lnx_b"])
    cdw, cb, clg, clb = wts["conv_dw"], row(wts["conv_b"]), row(wts["conv_ln_g"]), row(wts["conv_ln_b"])
    small = (mu_p, mu_n, w2b, w0c, a2b, a0c, g2p, k_k, k_a)
    seq3 = lambda a: a.reshape(bsz, seq, a.shape[-1])
    flat = lambda a: a.reshape(n_tok, a.shape[-1])

    p = _matmul(x1b, win, name="proj_in")
    r, v, kk, w, kd, b, g = _mix_prep(p, *small, seq=seq, tt=tt, name="mix_prep")
    scan_in = [seq3(a) for a in (r, w, kd, v, kk, b)]
    y0, s_chunks0 = _wkv_chunk_fwd(*scan_in, rev=False, name="wkv_fwd_dir0")
    y1, s_chunks1 = _wkv_chunk_fwd(*scan_in, rev=True, name="wkv_fwd_dir1")
    y0, y1 = flat(y0), flat(y1)
    yr = _mix_post(y0, y1, r, v, kd, g, lnx_g, lnx_b, r_k, tt=tt, name="mix_post")
    yc, yv = _conv_fwd(p, cdw, cb, clg, clb, seq=seq, tt=tt, name="conv_fwd")
    if more_weights is not None:
        wts = {**wts, **more_weights("out", yr)}
    wout, w2i, w2o = wts["w_out"], wts["ffn2_w_in"], wts["ffn2_w_out"]
    z2, x2, x2b = _mm_ln([yr, yv], wout, x1, ln["ln2_g"], ln["ln2_b"], 1.0, tm=TM_LN, name="proj_out_ln2")
    h2, act2 = _ffn_in(x2b, w2i, tm=TM_FFN, name="ffn2_in")

    gr = {}
    slab_rows = lambda a: a.reshape((N_CHIPS, a.shape[0] // N_CHIPS) + a.shape[1:])
    dw_kw = dict(ta=True, out_dtype=BF16)
    dz3, gr["ln3_g"], gr["ln3_b"], loss_part = _mm_ln_loss(act2, w2o, x2, ln["ln3_g"], ln["ln3_b"], tgt, 0.5, tm=TM_LN,
                                                           name="ffn2_out_ln3_loss")
    dh2 = _ffn_out_bwd(dz3, w2o, h2, tm=TM_FFN, name="ffn2_out_dx")
    gr["ffn2_w_out"] = slab_rows(_matmul(act2, dz3, scale=0.5, tm=D_FF // 2, name="ffn2_out_dw", **dw_kw))
    dz2, gr["ln2_g"], gr["ln2_b"] = _mm_nt_res([dh2], w2i, dz3, ln=(z2, ln["ln2_g"], ln["ln2_b"]), tm=TM_FFN,
                                               name="ffn2_in_dx_ln2")
    gr["ffn2_w_in"] = _matmul(x2b, dh2, col_slabs=True, tn=2 * D_FF // N_CHIPS, name="ffn2_in_dw", **dw_kw)
    dmix = _matmul(dz2, wout, tb=True, name="proj_out_dx")
    gr["w_out"] = slab_rows(jnp.concatenate([_matmul(yr, dz2, name="proj_out_dw_rwkv", **dw_kw),
                                             _matmul(yv, dz2, name="proj_out_dw_conv", **dw_kw)], axis=0))
    tok = grads_ready(("ffn2_w_out", "ffn2_w_in", "w_out"), [gr["ffn2_w_out"], gr["ffn2_w_in"], gr["w_out"]])
    dyr, dyv = (dmix, RW, 0), (dmix, RW, 1)
    dy, dr_p, dv_p, dkd_p, dg, gr["lnx_g"], gr["lnx_b"], gr["r_k"] = _mix_post_bwd(
        y0, y1, r, v, kd, g, _follow(lnx_g, tok), lnx_b, r_k, dyr, tt=tt, name="mix_post_bwd")
    dr0, dw0, dkd0, dv0, dk0, db0 = [flat(a) for a in _wkv_chunk_bwd(*scan_in, seq3(dy), s_chunks0, rev=False,
                                                                      name="wkv_bwd_dir0")]
    dr1, dw1, dkd1, dv1, dk1, db1 = [flat(a) for a in _wkv_chunk_bwd(*scan_in, seq3(dy), s_chunks1, rev=True,
                                                                      name="wkv_bwd_dir1")]
    ct_terms = [[dr_p, dr0, dr1], [dv_p, dv0, dv1], [dk0, dk1], [(dw0, dw1)], [dkd_p, (dkd0, dkd1)], [(db0, db1)], [dg]]
    dyc, gr["conv_ln_g"], gr["conv_ln_b"], gr["conv_b"] = _conv_post_bwd(yc, dyv, clg, clb, tt=tt, name="conv_post_bwd")
    dpc, ddw = _conv_bwd(dyc, p, cdw, seq=seq, tt=tt, name="conv_bwd")
    gr["conv_dw"] = ddw[:CONV_K]
    dps, dw2b, dw0c, da2b, da0c, dg2p, gr["k_k"], gr["k_a"] = _mix_prep_bwd(
        p, *small, ct_terms, seq=seq, tt=tt, name="mix_prep_bwd")
    gr["w2"] = jnp.stack([dw2b[:LORA, :RW], dw2b[LORA:, RW:]])
    gr["a2"] = jnp.stack([da2b[:LORA, :RW], da2b[LORA:, RW:]])
    gr["w0"], gr["a0"], gr["g2"] = dw0c.reshape(2, RW), da0c.reshape(2, RW), dg2p[:GATE_LORA]
    dpsh, dmu_p, dmu_n = _shift_bwd(dps, p, mu_p, mu_n, seq=seq, tt=tt, name="shift_bwd")
    gr["mu_prev"], gr["mu_next"] = dmu_p[:, :SHIFT_COLS], dmu_n[:, :SHIFT_COLS]
    dwin = jnp.concatenate([_matmul(x1b, dpsh, name="proj_in_dw_shift", **dw_kw)[:, :SHIFT_COLS],
                            _matmul(x1b, dpc, name="proj_in_dw_conv", **dw_kw)], axis=1)
    gr["w_in"] = jnp.moveaxis(dwin.reshape(D_MODEL, N_CHIPS, IN_COLS // N_CHIPS), 1, 0)
    tok = grads_ready(("w_in",), [gr["w_in"]])
    dz1, gr["ln1_g"], gr["ln1_b"] = _mm_nt_res([dpsh, dpc], win, dz2, ln=(z1, ln["ln1_g"], ln["ln1_b"]), tm=TM_FFN,
                                               after=tok, name="proj_in_dx_ln1")
    dh1 = _ffn_out_bwd(dz1, w1o, h1, tm=TM_FFN, name="ffn1_out_dx")
    gr["ffn1_w_out"] = slab_rows(_matmul(act1, dz1, scale=0.5, tm=D_FF // 2, name="ffn1_out_dw", **dw_kw))
    tok = grads_ready(("ffn1_w_out",), [gr["ffn1_w_out"]])
    gr["ffn1_w_in"] = _matmul(x0, dh1, col_slabs=True, tn=2 * D_FF // N_CHIPS, after=tok, name="ffn1_in_dw", **dw_kw)
    tok = grads_ready(("ffn1_w_in",), [gr["ffn1_w_in"]])
    dx0 = _mm_nt_res([dh1], w1i, dz1, tm=TM_FFN, after=tok, name="ffn1_in_dx")
    return loss_part, dx0.reshape(bsz, seq, D_MODEL), gr


def _mesh_pos():
    return lax.axis_index("x"), lax.axis_index("y"), lax.axis_index("c")


def _other_chips(x, y):
    return [(1 - x, y), (x, 1 - y), (1 - x, 1 - y)]


def _gather_chips(shards, *, name):
    n = len(shards)
    halves = [s.shape[0] // 2 for s in shards]
    assert all(2 * h == s.shape[0] for h, s in zip(halves, shards))

    def body(*refs):
        ins, outs = refs[:n], refs[n:2 * n]
        send_sems, recv_sems, fwd_send_sems, fwd_recv_sems, loc_sems = refs[2 * n:]
        x, y, c = _mesh_pos()
        q = 2 * x + y
        peers = _other_chips(x, y)
        local = [pltpu.make_async_copy(ins[a], outs[a].at[q], loc_sems.at[a]) for a in range(n)]
        for cp in local:
            cp.start()

        def half(a, chip, core):
            return outs[a].at[chip, pl.ds(core * halves[a], halves[a])]

        sends = [pltpu.make_async_remote_copy(ins[a].at[pl.ds(c * halves[a], halves[a])], half(a, q, c),
                                              send_sems.at[a, k], recv_sems.at[a, k],
                                              device_id=(px, py, c), device_id_type=MESH)
                 for a in range(n) for k, (px, py) in enumerate(peers)]
        for cp in sends:
            cp.start()
        passed = []
        for a in range(n):
            for k, (px, py) in enumerate(peers):
                mine = half(a, 2 * px + py, c)
                pltpu.make_async_remote_copy(mine, mine, send_sems.at[a, k], recv_sems.at[a, k],
                                             device_id=(px, py, c), device_id_type=MESH).wait_recv()
                cp = pltpu.make_async_remote_copy(mine, mine, fwd_send_sems.at[a, k], fwd_recv_sems.at[a, k],
                                                  device_id=(x, y, 1 - c), device_id_type=MESH)
                cp.start()
                passed.append(cp)
        for a in range(n):
            for k, (px, py) in enumerate(peers):
                theirs = half(a, 2 * px + py, 1 - c)
                pltpu.make_async_remote_copy(theirs, theirs, fwd_send_sems.at[a, k], fwd_recv_sems.at[a, k],
                                             device_id=(x, y, 1 - c), device_id_type=MESH).wait_recv()
        for cp in sends + passed:
            cp.wait_send()
        for cp in local:
            cp.wait()

    any_spec = pl.BlockSpec(memory_space=pl.ANY)
    return pl.pallas_call(
        body, name=name,
        out_shape=[jax.ShapeDtypeStruct((N_CHIPS,) + s.shape, s.dtype) for s in shards],
        in_specs=[any_spec] * n, out_specs=[any_spec] * n,
        scratch_shapes=[pltpu.SemaphoreType.DMA((n, 3))] * 4 + [pltpu.SemaphoreType.DMA((n,))],
        compiler_params=pltpu.CompilerParams(has_side_effects=True),
    )(*shards)


HBM_SPEC = pl.BlockSpec(memory_space=pltpu.HBM)
SEM_SPEC = pl.BlockSpec(memory_space=pltpu.SEMAPHORE)
ANY_SPEC = pl.BlockSpec(memory_space=pl.ANY)
SIDE_EFFECT = pltpu.SideEffectType.DATAFLOW_SIDE_EFFECTING


def _chip_copies(src_refs, land_refs, send_sems, recv_sems, scatter, arriving=False):
    x, y, c = _mesh_pos()
    cps = []
    for a, (src, land) in enumerate(zip(src_refs, land_refs)):
        for k, (px, py) in enumerate(_other_chips(x, y)):
            slot = k if scatter else (2 * px + py if arriving else 2 * x + y)
            cps.append(pltpu.make_async_remote_copy(src.at[2 * px + py] if scatter else src, land.at[slot],
                                                    send_sems.at[3 * a + k], recv_sems.at[3 * a + k],
                                                    device_id=(px, py, c), device_id_type=MESH))
    return cps


def _exchange_start(srcs, *, scatter, after, name):
    n = len(srcs)
    lands = [lax.empty((3,) + s.shape[1:] if scatter else (N_CHIPS,) + s.shape, s.dtype) for s in srcs]

    def body(*refs):
        src_refs, land_refs = refs[:n], refs[n:2 * n]
        send_sems, recv_sems = refs[2 * n + 1:2 * n + 3]
        token = refs[-1]
        for cp in _chip_copies(src_refs, land_refs, send_sems, recv_sems, scatter):
            cp.start()
        token[...] = jnp.zeros_like(token)

    hbm = lambda a: pltpu.with_memory_space_constraint(a, pltpu.HBM)
    outs = pl.pallas_call(
        body, name=name,
        out_shape=(pltpu.SemaphoreType.DMA((3 * n,)), pltpu.SemaphoreType.DMA((3 * n,)),
                   *[pltpu.HBM(a.shape, a.dtype) for a in srcs + lands], jax.ShapeDtypeStruct((8, LANES), F32)),
        in_specs=[HBM_SPEC] * (2 * n) + [ANY_SPEC],
        out_specs=(SEM_SPEC, SEM_SPEC, *[HBM_SPEC] * (2 * n), pl.BlockSpec(memory_space=pltpu.VMEM)),
        input_output_aliases={i: 2 + i for i in range(2 * n)},
        compiler_params=pltpu.CompilerParams(has_side_effects=SIDE_EFFECT),
    )(*[hbm(a) for a in srcs + lands], after)
    return outs[0], outs[1], list(outs[2:2 + n]), list(outs[2 + n:2 + 2 * n]), outs[-1]


def _exchange_wait(started, *, scatter, after, name):
    send_sems, recv_sems, srcs, lands, _ = started
    n = len(srcs)

    def body(*refs):
        src_refs, land_refs = refs[:n], refs[n:2 * n]
        send_s, recv_s = refs[2 * n:2 * n + 2]
        for cp in _chip_copies(src_refs, land_refs, send_s, recv_s, scatter, arriving=True):
            cp.wait_send()
            cp.wait_recv()

    outs = pl.pallas_call(
        body, name=name,
        out_shape=tuple(pltpu.HBM(a.shape, a.dtype) for a in srcs + lands),
        in_specs=[HBM_SPEC] * (2 * n) + [SEM_SPEC, SEM_SPEC, ANY_SPEC],
        out_specs=tuple([HBM_SPEC] * (2 * n)),
        input_output_aliases={i: i for i in range(2 * n)},
        compiler_params=pltpu.CompilerParams(has_side_effects=SIDE_EFFECT),
    )(*srcs, *lands, send_sems, recv_sems, after)
    return list(outs[:n]), list(outs[n:])


def _by_chip(own, land):
    xi, yi, _ = _mesh_pos()
    return lax.dynamic_update_index_in_dim(land, own, 2 * xi + yi, 0)


def _swap_sibling(arrs, *, name):
    n = len(arrs)

    def body(*refs):
        ins, outs = refs[:n], refs[n:2 * n]
        send_sems, recv_sems = refs[2 * n:]
        x, y, c = _mesh_pos()
        cps = [pltpu.make_async_remote_copy(ins[a], outs[a], send_sems.at[a], recv_sems.at[a],
                                            device_id=(x, y, 1 - c), device_id_type=MESH) for a in range(n)]
        for cp in cps:
            cp.start()
        for cp in cps:
            cp.wait_recv()
        for cp in cps:
            cp.wait_send()

    any_spec = pl.BlockSpec(memory_space=pl.ANY)
    return pl.pallas_call(
        body, name=name,
        out_shape=[jax.ShapeDtypeStruct(s.shape, s.dtype) for s in arrs],
        in_specs=[any_spec] * n, out_specs=[any_spec] * n,
        scratch_shapes=[pltpu.SemaphoreType.DMA((n,)), pltpu.SemaphoreType.DMA((n,))],
        compiler_params=pltpu.CompilerParams(has_side_effects=True),
    )(*arrs)


def _all_reduce_rows(vec, *, name):
    rows = vec.shape[0]

    def body(v_ref, o_ref, land, send_sems, recv_sems):
        x, y, c = _mesh_pos()
        me = 4 * x + 2 * y + c
        land[me] = v_ref[...]
        cps = []
        for m in range(1, 8):
            mx, my, mc = (m >> 2) & 1, (m >> 1) & 1, m & 1
            tx, ty, tc = (x + mx) % 2, (y + my) % 2, (c + mc) % 2
            cps.append(pltpu.make_async_remote_copy(v_ref, land.at[me], send_sems.at[m - 1], recv_sems.at[me],
                                                    device_id=(tx, ty, tc), device_id_type=MESH))
        for cp in cps:
            cp.start()
        for m in range(1, 8):
            mx, my, mc = (m >> 2) & 1, (m >> 1) & 1, m & 1
            src = 4 * ((x + mx) % 2) + 2 * ((y + my) % 2) + (c + mc) % 2
            pltpu.make_async_remote_copy(v_ref, land.at[src], send_sems.at[m - 1], recv_sems.at[src],
                                         device_id=(x, y, c), device_id_type=MESH).wait_recv()
        for cp in cps:
            cp.wait_send()
        acc = land[0]
        for d in range(1, 8):
            acc = acc + land[d]
        o_ref[...] = acc

    vm = pl.BlockSpec(memory_space=pltpu.VMEM)
    return pl.pallas_call(
        body, name=name,
        out_shape=jax.ShapeDtypeStruct(vec.shape, F32),
        in_specs=[vm], out_specs=vm,
        scratch_shapes=[pltpu.VMEM((8, rows, LANES), F32), pltpu.SemaphoreType.DMA((7,)), pltpu.SemaphoreType.DMA((8,))],
        compiler_params=pltpu.CompilerParams(has_side_effects=True, vmem_limit_bytes=VMEM_LIMIT),
    )(vec)


def _adamw(w, g, m, v):
    m = ADAM_B1 * m + (1.0 - ADAM_B1) * g
    v = ADAM_B2 * v + (1.0 - ADAM_B2) * (g * g)
    m_hat = m / (1.0 - ADAM_B1 ** ADAM_STEP)
    v_hat = v / (1.0 - ADAM_B2 ** ADAM_STEP)
    delta = -ADAM_LR * (m_hat / (jnp.sqrt(v_hat) + ADAM_EPS) + ADAM_WD * w)
    return delta, m, v


def _sum4(mine, land, *, name):
    rows, cols = mine.shape
    tr = _pick_rows(rows)

    def body(a_ref, l_ref, o_ref):
        o_ref[...] = (a_ref[...].astype(F32) + l_ref[0].astype(F32)) + (l_ref[1].astype(F32) + l_ref[2].astype(F32))

    return pl.pallas_call(
        body, name=name, out_shape=jax.ShapeDtypeStruct((rows, cols), F32), grid=(rows // tr,),
        in_specs=[pl.BlockSpec((tr, cols), lambda i: (i, 0)), pl.BlockSpec((3, tr, cols), lambda i: (0, i, 0))],
        out_specs=pl.BlockSpec((tr, cols), lambda i: (i, 0)),
        compiler_params=_cparams(("parallel",)),
    )(mine, land)


def _pick_rows(rows, want=256):
    for t in range(min(want, rows) // 8 * 8, 0, -8):
        if rows % t == 0:
            return t
    return rows


def _sum_adam(h_mine, h_sib, w, m, v, *, name):
    rows, cols = w.shape
    tr = _pick_rows(rows)

    def body(a_ref, b_ref, w_ref, m_ref, v_ref, g_o, d_o, m_o, v_o):
        g = a_ref[...] + b_ref[...]
        d, mn, vn = _adamw(w_ref[...], g, m_ref[...], v_ref[...])
        g_o[...], d_o[...], m_o[...], v_o[...] = g, d, mn, vn

    spec = pl.BlockSpec((tr, cols), lambda i: (i, 0))
    return pl.pallas_call(
        body, name=name, out_shape=[jax.ShapeDtypeStruct((rows, cols), F32)] * 4, grid=(rows // tr,),
        in_specs=[spec] * 5, out_specs=[spec] * 4, compiler_params=_cparams(("parallel",)),
    )(h_mine, h_sib, w, m, v)


def _adam_rows(w, g, m, v, *, name):
    def body(w_ref, g_ref, m_ref, v_ref, d_o, m_o, v_o):
        d_o[...], m_o[...], v_o[...] = _adamw(w_ref[...], g_ref[...], m_ref[...], v_ref[...])

    vm = pl.BlockSpec(memory_space=pltpu.VMEM)
    return pl.pallas_call(
        body, name=name, out_shape=[jax.ShapeDtypeStruct(w.shape, F32)] * 3,
        in_specs=[vm] * 4, out_specs=[vm] * 3, compiler_params=_cparams(),
    )(w, g, m, v)


def _size(shape):
    size = 1
    for d in shape:
        size *= d
    return size


def _pack_rows(arrs):
    blocks = []
    for a in arrs:
        flat = a.reshape(-1).astype(F32)
        flat = jnp.concatenate([flat, jnp.zeros((-flat.shape[0] % (8 * LANES),), F32)])
        blocks.append(flat.reshape(-1, LANES))
    return jnp.concatenate(blocks, axis=0)


def _unpack_rows(packed, shapes):
    out, row = [], 0
    for s in shapes:
        rows = -(-_size(s) // (8 * LANES)) * 8
        out.append(packed[row:row + rows].reshape(-1)[:_size(s)].reshape(s))
        row += rows
    return out


WEIGHTS = ['ffn1_w_in', 'ffn1_w_out', 'w_in', 'mu_prev', 'mu_next', 'w0', 'w2', 'a0', 'a2', 'g2', 'k_k', 'k_a', 'r_k',
           'lnx_g', 'lnx_b', 'conv_dw', 'conv_b', 'conv_ln_g', 'conv_ln_b', 'w_out', 'ffn2_w_in', 'ffn2_w_out',
           'ln1_g', 'ln1_b', 'ln2_g', 'ln2_b', 'ln3_g', 'ln3_b']
COL_SHARDED = ('ffn1_w_in', 'w_in', 'ffn2_w_in')
ROW_SHARDED = ('ffn1_w_out', 'w_out', 'ffn2_w_out')
BIG = COL_SHARDED + ROW_SHARDED
SMALL_SHARDED = ('w0', 'w2', 'a0', 'a2', 'g2', 'conv_dw')
REPLICATED = tuple(n for n in WEIGHTS if n not in BIG + SMALL_SHARDED)


def _train_step(x, target, w, m, v, *, tt):
    xi, yi, _ = _mesh_pos()
    q = 2 * xi + yi

    early, mid, late = ("ffn1_w_in", "ffn1_w_out"), ("w_in",) + SMALL_SHARDED, ("w_out", "ffn2_w_in", "ffn2_w_out")
    shard = lambda n: w[n][0].astype(BF16) if n in BIG else w[n][0]

    def whole(n, slabs):
        if n in ROW_SHARDED:
            return slabs.reshape((-1,) + slabs.shape[2:])
        if n in ("ffn1_w_in", "ffn2_w_in"):
            return slabs
        return jnp.moveaxis(slabs, 0, -2).reshape(slabs.shape[1:-1] + (N_CHIPS * slabs.shape[-1],))

    full = {n: w[n][0] for n in REPLICATED}
    first = _gather_chips([shard(n) for n in early], name="gather_ffn1")
    full.update({n: whole(n, g) for n, g in zip(early, first)})
    mid_started = _exchange_start([shard(n) for n in mid], scatter=False, after=first[0], name="gather_mix_start")
    late_started = _exchange_start([shard(n) for n in late], scatter=False, after=mid_started[-1], name="gather_out_start")

    def more_weights(stage, after):
        names, started = (mid, mid_started) if stage == "mix" else (late, late_started)
        own, land = _exchange_wait(started, scatter=False, after=after, name="gather_%s_wait" % stage)
        got = {n: whole(n, _by_chip(o, l)) for n, o, l in zip(names, own, land)}
        full.update(got)
        return got

    sent = []

    def grads_ready(names, slabs):
        started = _exchange_start(slabs, scatter=True, after=slabs[0], name="scatter_%s_start" % names[0])
        sent.append((names, started))
        return started[-1]

    loss_part, grad_x, gr = _local_step(x, target, full, tt=tt, start_token=late_started[-1],
                                        more_weights=more_weights, grads_ready=grads_ready)

    halves = {}
    for names, started in sent:
        stacks, landed = _exchange_wait(started, scatter=True, after=grad_x, name="scatter_%s_wait" % names[0])
        for n, s, l in zip(names, stacks, landed):
            halves[n] = _sum4(lax.dynamic_index_in_dim(s, q, 0, keepdims=False), l, name="sum4_" + n)
    halves = [halves[n] for n in BIG]
    sib = _swap_sibling(halves, name="swap_halves")
    grad, delta, new_m, new_v = {}, {}, {}, {}
    for n, h, hs in zip(BIG, halves, sib):
        outs = _sum_adam(h, hs, w[n][0], m[n][0], v[n][0], name="adam_" + n)
        grad[n], delta[n], new_m[n], new_v[n] = [o[None] for o in outs]

    small_names = REPLICATED + SMALL_SHARDED
    small_full_shapes = [full[n].shape for n in small_names]
    red = _all_reduce_rows(_pack_rows([gr[n] for n in small_names] + [loss_part[0:1, 0:1]]), name="reduce_small")
    *red, loss = _unpack_rows(red, small_full_shapes + [()])
    red = dict(zip(small_names, red))
    gsm = {}
    for n in REPLICATED:
        gsm[n] = red[n].reshape(w[n].shape)
    for n in SMALL_SHARDED:
        width = w[n].shape[-1]
        gsm[n] = lax.dynamic_slice_in_dim(red[n], q * width, width, axis=red[n].ndim - 1).reshape(w[n].shape)
    shapes = [w[n].shape for n in small_names]
    d_p, m_p, v_p = _adam_rows(_pack_rows([w[n] for n in small_names]), _pack_rows([gsm[n] for n in small_names]),
                               _pack_rows([m[n] for n in small_names]), _pack_rows([v[n] for n in small_names]),
                               name="adam_small")
    for n, dd, mm, vv in zip(small_names, _unpack_rows(d_p, shapes), _unpack_rows(m_p, shapes), _unpack_rows(v_p, shapes)):
        grad[n], delta[n], new_m[n], new_v[n] = gsm[n], dd, mm, vv
    return loss, grad_x, grad, delta, new_m, new_v


def kernel(x, ffn1_w_in, ffn1_w_out, w_in, mu_prev, mu_next, w0, w2, a0, a2, g2, k_k, k_a, r_k, lnx_g, lnx_b, conv_dw, conv_b, conv_ln_g, conv_ln_b, w_out, ffn2_w_in, ffn2_w_out, ln1_g, ln1_b, ln2_g, ln2_b, ln3_g, ln3_b, loss_target, m_ffn1_w_in, m_ffn1_w_out, m_w_in, m_mu_prev, m_mu_next, m_w0, m_w2, m_a0, m_a2, m_g2, m_k_k, m_k_a, m_r_k, m_lnx_g, m_lnx_b, m_conv_dw, m_conv_b, m_conv_ln_g, m_conv_ln_b, m_w_out, m_ffn2_w_in, m_ffn2_w_out, m_ln1_g, m_ln1_b, m_ln2_g, m_ln2_b, m_ln3_g, m_ln3_b, v_ffn1_w_in, v_ffn1_w_out, v_w_in, v_mu_prev, v_mu_next, v_w0, v_w2, v_a0, v_a2, v_g2, v_k_k, v_k_a, v_r_k, v_lnx_g, v_lnx_b, v_conv_dw, v_conv_b, v_conv_ln_g, v_conv_ln_b, v_w_out, v_ffn2_w_in, v_ffn2_w_out, v_ln1_g, v_ln1_b, v_ln2_g, v_ln2_b, v_ln3_g, v_ln3_b):
    args = dict(locals())
    w = {n: args[n] for n in WEIGHTS}
    m = {n: args["m_" + n] for n in WEIGHTS}
    v = {n: args["v_" + n] for n in WEIGHTS}
    seq = x.shape[1]
    loss, grad_x, grad, delta, new_m, new_v = _train_step(x, loss_target, w, m, v, tt=min(256, seq))
    return (loss, grad_x, *[grad[n] for n in WEIGHTS], *[delta[n] for n in WEIGHTS],
            *[new_m[n] for n in WEIGHTS], *[new_v[n] for n in WEIGHTS])
```

```python
import functools

import jax
import jax.numpy as jnp
from jax import lax
from jax.experimental import pallas as pl
from jax.experimental.pallas import tpu as pltpu

F32 = jnp.float32
BF16 = jnp.bfloat16

D_MODEL = 1024
RW = 512
HEAD = 64
CW = 512
CONV_K = 31
CONV_PAD = 15
D_FF = 2816
LORA = 64
GATE_LORA = 160
GATE_PAD = 256
SHIFT_COLS = 1952
SHIFT_PAD = 2048
IN_COLS = 2976
IN_PAD = 3072
LN_EPS = 1e-5
GN_EPS = 64e-5
NORM_EPS = 1e-12
ALPHA = 2.0 ** 0.25
DECAY_SCALE = 0.6065306597126334
ADAM_LR, ADAM_B1, ADAM_B2, ADAM_EPS, ADAM_WD, ADAM_STEP = 0.001, 0.9, 0.999, 1e-08, 0.01, 10
N_CHIPS = 4
VMEM_LIMIT = 56 * 1024 * 1024
TM_FFN = 256
TM_LN = 512

MESH = pl.DeviceIdType.MESH


def _cparams(sem=None, **kw):
    return pltpu.CompilerParams(dimension_semantics=sem, vmem_limit_bytes=VMEM_LIMIT, **kw)


LANES = 128


def _pick_tile(dim, want):
    for t in range(min(want, dim) // LANES * LANES, 0, -LANES):
        if dim % t == 0:
            return t
    return dim


def _after_operand(after):
    return ([], []) if after is None else ([pl.BlockSpec(memory_space=pl.ANY)], [after])


def _matmul(a, b, *, ta=False, tb=False, out_dtype=F32, tm=1024, tn=1024, tk=1024, scale=1.0, col_slabs=False,
            after=None, name):
    after_specs, after_args = _after_operand(after)
    if ta:
        k_dim, m_dim = a.shape
    else:
        m_dim, k_dim = a.shape
    n_dim = b.shape[0] if tb else b.shape[1]
    tm, tn, tk = _pick_tile(m_dim, tm), _pick_tile(n_dim, tn), _pick_tile(k_dim, tk)
    assert m_dim % tm == 0 and n_dim % tn == 0 and k_dim % tk == 0, (name, a.shape, b.shape, tm, tn, tk)
    nk = k_dim // tk
    dims = (((0,) if ta else (1,), (1,) if tb else (0,)), ((), ()))
    if col_slabs:
        out_shape = jax.ShapeDtypeStruct((n_dim // tn, m_dim, tn), out_dtype)
        out_spec = pl.BlockSpec((None, tm, tn), lambda i, j, k: (j, i, 0))
    else:
        out_shape = jax.ShapeDtypeStruct((m_dim, n_dim), out_dtype)
        out_spec = pl.BlockSpec((tm, tn), lambda i, j, k: (i, j))

    def body(a_ref, b_ref, *rest):
        o_ref, acc_ref = rest[-2:]
        kk = pl.program_id(2)

        @pl.when(kk == 0)
        def _():
            acc_ref[...] = jnp.zeros_like(acc_ref)

        acc_ref[...] += lax.dot_general(a_ref[...].astype(BF16), b_ref[...].astype(BF16), dims,
                                        preferred_element_type=F32)

        @pl.when(kk == nk - 1)
        def _():
            o_ref[...] = (acc_ref[...] * scale).astype(o_ref.dtype)

    a_spec = pl.BlockSpec((tk, tm), lambda i, j, k: (k, i)) if ta else pl.BlockSpec((tm, tk), lambda i, j, k: (i, k))
    b_spec = pl.BlockSpec((tn, tk), lambda i, j, k: (j, k)) if tb else pl.BlockSpec((tk, tn), lambda i, j, k: (k, j))
    return pl.pallas_call(
        body, name=name,
        out_shape=out_shape,
        grid=(m_dim // tm, n_dim // tn, nk),
        in_specs=[a_spec, b_spec] + after_specs,
        out_specs=out_spec,
        scratch_shapes=[pltpu.VMEM((tm, tn), F32)],
        compiler_params=_cparams(("parallel", "parallel", "arbitrary")),
    )(a, b, *after_args)


def _whole(shape):
    nd = len(shape)
    return pl.BlockSpec(shape, lambda i: (0,) * nd)


def _ffn_in(x, w, *, tm, after=None, name):
    n_tok = x.shape[0]
    sw = w.shape[2]
    tm = min(tm, n_tok)

    after_specs, after_args = _after_operand(after)

    def body(x_ref, w_ref, *rest):
        h_ref, a_ref = rest[-2:]
        xb = x_ref[...].astype(BF16)
        for s in range(2):
            g = jnp.dot(xb, w_ref[s], preferred_element_type=F32)
            u = jnp.dot(xb, w_ref[s + 2], preferred_element_type=F32)
            h_ref[:, s * sw:(s + 1) * sw] = g.astype(BF16)
            h_ref[:, (s + 2) * sw:(s + 3) * sw] = u.astype(BF16)
            a_ref[:, s * sw:(s + 1) * sw] = (_silu(g) * u).astype(BF16)

    return pl.pallas_call(
        body, name=name,
        out_shape=[jax.ShapeDtypeStruct((n_tok, 2 * D_FF), BF16), jax.ShapeDtypeStruct((n_tok, D_FF), BF16)],
        grid=(n_tok // tm,),
        in_specs=[pl.BlockSpec((tm, D_MODEL), lambda i: (i, 0)), _whole(w.shape)] + after_specs,
        out_specs=[pl.BlockSpec((tm, 2 * D_FF), lambda i: (i, 0)), pl.BlockSpec((tm, D_FF), lambda i: (i, 0))],
        compiler_params=_cparams(("parallel",)),
    )(x, w, *after_args)


def _mm_ln(a_list, w, xres, g, b, fscale, *, tm, name):
    n_tok = xres.shape[0]
    tm = min(tm, n_tok)
    na = len(a_list)

    def body(*refs):
        a_refs = refs[:na]
        w_ref, x_ref, g_ref, b_ref, z_o, y_o, yb_o = refs[na:]
        f, off = None, 0
        for a_ref in a_refs:
            k = a_ref.shape[1]
            t = jnp.dot(a_ref[...].astype(BF16), w_ref[off:off + k, :], preferred_element_type=F32)
            f = t if f is None else f + t
            off += k
        z = ALPHA * x_ref[...] + fscale * f
        y = _layer_norm(z, g_ref[...], b_ref[...])
        z_o[...] = z
        y_o[...] = y
        yb_o[...] = y.astype(BF16)

    tile = pl.BlockSpec((tm, D_MODEL), lambda i: (i, 0))
    return pl.pallas_call(
        body, name=name,
        out_shape=[jax.ShapeDtypeStruct((n_tok, D_MODEL), F32)] * 2 + [jax.ShapeDtypeStruct((n_tok, D_MODEL), BF16)],
        grid=(n_tok // tm,),
        in_specs=[pl.BlockSpec((tm, a.shape[1]), lambda i: (i, 0)) for a in a_list]
        + [_whole(w.shape), tile, _whole(g.shape), _whole(b.shape)],
        out_specs=[tile, tile, tile],
        compiler_params=_cparams(("parallel",)),
    )(*a_list, w, xres, g, b)


def _mm_ln_loss(a, w, xres, g, b, target, fscale, *, tm, name):
    n_tok = xres.shape[0]
    tm = min(tm, n_tok)

    def body(a_ref, w_ref, x_ref, g_ref, b_ref, t_ref, dz_o, dg_o, db_o, loss_o):
        i = pl.program_id(0)
        z = ALPHA * x_ref[...] + fscale * jnp.dot(a_ref[...].astype(BF16), w_ref[...], preferred_element_type=F32)
        y, vjp = jax.vjp(_layer_norm, z, g_ref[...], b_ref[...])
        e = y - t_ref[...]
        dz, dg, db = vjp(e * (1.0 / D_MODEL))

        @pl.when(i == 0)
        def _():
            dg_o[...] = jnp.zeros_like(dg_o)
            db_o[...] = jnp.zeros_like(db_o)
            loss_o[...] = jnp.zeros_like(loss_o)
        dz_o[...] = dz
        dg_o[...] += dg
        db_o[...] += db
        loss_o[...] += 0.5 * jnp.sum(jnp.mean(e * e, axis=-1, keepdims=True), axis=0, keepdims=True)

    tile = pl.BlockSpec((tm, D_MODEL), lambda i: (i, 0))
    row = pl.BlockSpec((1, D_MODEL), lambda i: (0, 0))
    return pl.pallas_call(
        body, name=name,
        out_shape=[jax.ShapeDtypeStruct((n_tok, D_MODEL), F32), jax.ShapeDtypeStruct((1, D_MODEL), F32),
                   jax.ShapeDtypeStruct((1, D_MODEL), F32), jax.ShapeDtypeStruct((8, LANES), F32)],
        grid=(n_tok // tm,),
        in_specs=[pl.BlockSpec((tm, a.shape[1]), lambda i: (i, 0)), _whole(w.shape), tile, row, row, tile],
        out_specs=[tile, row, row, pl.BlockSpec((8, LANES), lambda i: (0, 0))],
        compiler_params=_cparams(("arbitrary",)),
    )(a, w, xres, g, b, target)


def _ffn_out_bwd(dz, w, h, *, tm, after=None, name):
    n_tok = dz.shape[0]
    tm = min(tm, n_tok)
    cw = D_FF // 2
    after_specs, after_args = _after_operand(after)

    def body(dz_ref, w_ref, h_ref, *rest):
        dh_ref = rest[-1]
        dzb = dz_ref[...].astype(BF16)
        for s in range(2):
            dact = 0.5 * lax.dot_general(dzb, w_ref[s * cw:(s + 1) * cw, :], (((1,), (1,)), ((), ())),
                                         preferred_element_type=F32)
            gate = h_ref[:, s * cw:(s + 1) * cw].astype(F32)
            up = h_ref[:, D_FF + s * cw:D_FF + (s + 1) * cw].astype(F32)
            sg = _sigmoid(gate)
            dh_ref[:, s * cw:(s + 1) * cw] = (dact * up * sg * (1.0 + gate * (1.0 - sg))).astype(BF16)
            dh_ref[:, D_FF + s * cw:D_FF + (s + 1) * cw] = (dact * gate * sg).astype(BF16)

    wide = pl.BlockSpec((tm, 2 * D_FF), lambda i: (i, 0))
    return pl.pallas_call(
        body, name=name,
        out_shape=jax.ShapeDtypeStruct((n_tok, 2 * D_FF), BF16),
        grid=(n_tok // tm,),
        in_specs=[pl.BlockSpec((tm, D_MODEL), lambda i: (i, 0)), _whole(w.shape), wide] + after_specs,
        out_specs=wide,
        compiler_params=_cparams(("parallel",)),
    )(dz, w, h, *after_args)


def _mm_nt_res(a_list, w, dz, *, tm, ln=None, after=None, name):
    n_tok = dz.shape[0]
    tm = min(tm, n_tok)
    na = len(a_list)
    nt = (((1,), (1,)), ((), ()))
    after_specs, after_args = _after_operand(after)
    n_out = 1 if ln is None else 3

    def body(*refs):
        a_refs = refs[:na]
        w_ref, dz_ref, o_ref = refs[na], refs[na + 1], refs[-n_out]
        acc = ALPHA * dz_ref[...]
        if len(w_ref.shape) == 3:
            cw = w_ref.shape[2]
            for s in range(w_ref.shape[0]):
                acc = acc + lax.dot_general(a_refs[0][:, s * cw:(s + 1) * cw], w_ref[s], nt, preferred_element_type=F32)
        else:
            off = 0
            for a_ref in a_refs:
                k = a_ref.shape[1]
                acc = acc + lax.dot_general(a_ref[...], w_ref[:, off:off + k], nt, preferred_element_type=F32)
                off += k
        if ln is None:
            o_ref[...] = acc
            return
        z_ref, g_ref, b_ref = refs[na + 2:na + 5]
        dg_o, db_o = refs[-2:]
        _, vjp = jax.vjp(_layer_norm, z_ref[...], g_ref[...], b_ref[...])
        o_ref[...], dg, db = vjp(acc)

        @pl.when(pl.program_id(0) == 0)
        def _():
            dg_o[...] = jnp.zeros_like(dg_o)
            db_o[...] = jnp.zeros_like(db_o)
        dg_o[...] += dg
        db_o[...] += db

    tile = pl.BlockSpec((tm, D_MODEL), lambda i: (i, 0))
    row = pl.BlockSpec((1, D_MODEL), lambda i: (0, 0))
    out_shape = [jax.ShapeDtypeStruct((n_tok, D_MODEL), F32)]
    ln_specs, ln_args, out_specs = [], [], [tile]
    if ln is not None:
        ln_specs, ln_args = [tile, row, row], list(ln)
        out_shape += [jax.ShapeDtypeStruct((1, D_MODEL), F32)] * 2
        out_specs += [row, row]
    outs = pl.pallas_call(
        body, name=name,
        out_shape=out_shape,
        grid=(n_tok // tm,),
        in_specs=[pl.BlockSpec((tm, a.shape[1]), lambda i: (i, 0)) for a in a_list] + [_whole(w.shape), tile]
        + ln_specs + after_specs,
        out_specs=out_specs,
        compiler_params=_cparams(("parallel",) if ln is None else ("arbitrary",)),
    )(*a_list, w, dz, *ln_args, *after_args)
    return outs[0] if ln is None else outs


def _rowcall(fn, tok_in, full_in, tok_out, acc_out, *, tt, name):
    views = [a if isinstance(a, tuple) else (a, a.shape[1], 0) for a in tok_in]
    tok_in = [a for a, _, _ in views]
    n_tok = tok_in[0].shape[0]
    assert n_tok % tt == 0, (name, n_tok, tt)
    n_ti, n_fi, n_to = len(tok_in), len(full_in), len(tok_out)

    def body(*refs):
        i = pl.program_id(0)
        ins = [r[...] for r in refs[:n_ti + n_fi]]
        outs = fn(i, *ins)
        o_refs = refs[n_ti + n_fi:]
        for r, val in zip(o_refs[:n_to], outs[:n_to]):
            r[...] = val.astype(r.dtype)
        if acc_out:
            @pl.when(i == 0)
            def _():
                for r in o_refs[n_to:]:
                    r[...] = jnp.zeros_like(r)
            for r, val in zip(o_refs[n_to:], outs[n_to:]):
                r[...] += val.reshape(r.shape).astype(F32)

    in_specs = [pl.BlockSpec((tt, width), functools.partial(lambda k, i: (i, k), k)) for _, width, k in views]
    in_specs += [pl.BlockSpec(a.shape, lambda i: (0, 0)) for a in full_in]
    out_specs = [pl.BlockSpec((tt, c), lambda i: (i, 0)) for c, _ in tok_out]
    out_specs += [pl.BlockSpec(s, lambda i: (0, 0)) for s in acc_out]
    out_shape = [jax.ShapeDtypeStruct((n_tok, c), dt) for c, dt in tok_out]
    out_shape += [jax.ShapeDtypeStruct(s, F32) for s in acc_out]
    return pl.pallas_call(
        body, name=name, out_shape=out_shape, grid=(n_tok // tt,), in_specs=in_specs, out_specs=out_specs,
        compiler_params=_cparams(("arbitrary",) if acc_out else ("parallel",)),
    )(*tok_in, *full_in)


@jax.custom_vjp
def _bdot(a, b):
    return jnp.dot(a.astype(BF16), b.astype(BF16), preferred_element_type=F32)


def _bdot_fwd(a, b):
    return _bdot(a, b), (a, b)


def _bdot_bwd(res, g):
    a, b = res
    g16 = g.astype(BF16)
    da = lax.dot_general(g16, b.astype(BF16), (((1,), (1,)), ((), ())), preferred_element_type=F32)
    db = lax.dot_general(a.astype(BF16), g16, (((0,), (0,)), ((), ())), preferred_element_type=F32)
    return da, db


_bdot.defvjp(_bdot_fwd, _bdot_bwd)


def _split16(x):
    hi = x.astype(BF16)
    lo = (x - hi.astype(F32)).astype(BF16)
    return hi, lo


def _segsum_raw(x, e2):
    hi, lo = _split16(x)
    outs = []
    for c in range(x.shape[1] // 256):
        lhs = jnp.concatenate([hi[:, 256 * c:256 * (c + 1)], lo[:, 256 * c:256 * (c + 1)]], axis=1)
        outs.append(jnp.dot(lhs, e2, preferred_element_type=F32))
    return jnp.concatenate(outs, axis=1)


@jax.custom_vjp
def _segsum(x, e2):
    return _segsum_raw(x, e2)


def _segsum_fwd(x, e2):
    return _segsum_raw(x, e2), e2


def _segsum_bwd(e2, g):
    return _segsum_raw(g, e2), jnp.zeros_like(e2)


_segsum.defvjp(_segsum_fwd, _segsum_bwd)


def _head_ones():
    r = lax.broadcasted_iota(jnp.int32, (512, 256), 0) % 256
    c = lax.broadcasted_iota(jnp.int32, (512, 256), 1)
    return (r // HEAD == c // HEAD).astype(BF16)


def _sigmoid(x):
    return 1.0 / (1.0 + jnp.exp(-x))


def _silu(x):
    return x * _sigmoid(x)


def _layer_norm(z, g, b, eps=LN_EPS):
    mu = jnp.mean(z, axis=-1, keepdims=True)
    zc = z - mu
    var = jnp.mean(zc * zc, axis=-1, keepdims=True)
    return zc * lax.rsqrt(var + eps) * g + b


def _prep(ps, w2b, w0c, a2b, a0c, g2p, k_k, k_a, e2):
    r, k, v = ps[:, 0:512], ps[:, 512:1024], ps[:, 1024:1536]
    wd, ad, gd = ps[:, 1536:1664], ps[:, 1664:1792], ps[:, 1792:2048]
    lw = _bdot(jnp.tanh(wd), w2b) + w0c
    decay = -DECAY_SCALE * _sigmoid(lw)
    a = _sigmoid(_bdot(ad, a2b) + a0c)
    g = _bdot(_sigmoid(gd), g2p)
    kkr = k * k_k
    nrm = jnp.sqrt(_segsum(kkr * kkr, e2))
    kk = kkr / jnp.maximum(nrm, NORM_EPS)
    k2 = jnp.concatenate([k, k], axis=1)
    ka2 = jnp.concatenate([k_a, k_a], axis=1)
    kd = k2 * (1.0 + (a - 1.0) * ka2)
    b = jnp.concatenate([kk, kk], axis=1) * a
    return r, v, kk, decay, kd, b, g


def _post(y0, y1, r, v, kd, g, lnx_g, lnx_b, r_k, e2):
    y = y0 + y1
    mu = _segsum(y, e2) * (1.0 / HEAD)
    yc = y - mu
    var = _segsum(yc * yc, e2) * (1.0 / HEAD)
    yn = yc * lax.rsqrt(var + GN_EPS) * lnx_g + lnx_b
    bonus = _segsum(r * (kd[:, :RW] + kd[:, RW:]) * r_k, e2)
    return (yn + bonus * v) * g


def _conv_post(yc, ln_g, ln_b):
    return _silu(_layer_norm(yc, ln_g, ln_b))


def _halo_specs(cols_block, hb, tt, n_tok, col_idx):
    nb = n_tok // hb
    prev = pl.BlockSpec((hb, cols_block), lambda i: (jnp.maximum(i * (tt // hb) - 1, 0), col_idx))
    nxt = pl.BlockSpec((hb, cols_block), lambda i: (jnp.minimum((i + 1) * (tt // hb), nb - 1), col_idx))
    return prev, nxt


def _mix_prep(p, mu_p, mu_n, w2b, w0c, a2b, a0c, g2p, k_k, k_a, *, seq, tt, name):
    n_tok = p.shape[0]
    tps = seq // tt
    e2 = _head_ones()

    def body(p_ref, hp_ref, hn_ref, mup_ref, mun_ref, w2b_ref, w0c_ref, a2b_ref, a0c_ref, g2p_ref, kk_ref, ka_ref,
             e2_ref, r_o, v_o, kk_o, w_o, kd_o, b_o, g_o, ext):
        i = pl.program_id(0)
        first = (i % tps) == 0
        last = (i % tps) == tps - 1
        pv = p_ref[...]
        ext[pl.ds(0, 8), :] = jnp.where(first, 0.0, hp_ref[...])
        ext[pl.ds(8, tt), :] = pv
        ext[pl.ds(8 + tt, 8), :] = jnp.where(last, 0.0, hn_ref[...])
        prev = ext[pl.ds(7, tt), :]
        nxt = ext[pl.ds(9, tt), :]
        ps = pv + mup_ref[...] * (prev - pv) + mun_ref[...] * (nxt - pv)
        outs = _prep(ps, w2b_ref[...], w0c_ref[...], a2b_ref[...], a0c_ref[...], g2p_ref[...], kk_ref[...],
                     ka_ref[...], e2_ref[...])
        for o_ref, val in zip((r_o, v_o, kk_o, w_o, kd_o, b_o, g_o), outs):
            o_ref[...] = val

    hp, hn = _halo_specs(SHIFT_PAD, 8, tt, n_tok, 0)
    fulls = [mu_p, mu_n, w2b, w0c, a2b, a0c, g2p, k_k, k_a, e2]
    widths = (RW, RW, RW, 2 * RW, 2 * RW, 2 * RW, RW)
    return pl.pallas_call(
        body, name=name,
        out_shape=[jax.ShapeDtypeStruct((n_tok, c), F32) for c in widths],
        grid=(n_tok // tt,),
        in_specs=[pl.BlockSpec((tt, SHIFT_PAD), lambda i: (i, 0)), hp, hn]
        + [pl.BlockSpec(a.shape, lambda i: (0, 0)) for a in fulls],
        out_specs=[pl.BlockSpec((tt, c), lambda i: (i, 0)) for c in widths],
        scratch_shapes=[pltpu.VMEM((tt + 16, SHIFT_PAD), F32)],
        compiler_params=_cparams(("parallel",)),
    )(p, p, p, *fulls)


def _mix_prep_bwd(p, mu_p, mu_n, w2b, w0c, a2b, a0c, g2p, k_k, k_a, ct_terms, *, seq, tt, name):
    n_tok = p.shape[0]
    tps = seq // tt
    e2 = _head_ones()
    acc_shapes = [w2b.shape, w0c.shape, a2b.shape, a0c.shape, g2p.shape, k_k.shape, k_a.shape]
    cts = [a for terms in ct_terms for t in terms for a in (t if isinstance(t, tuple) else (t,))]

    def body(p_ref, hp_ref, hn_ref, mup_ref, mun_ref, w2b_ref, w0c_ref, a2b_ref, a0c_ref, g2p_ref, kk_ref, ka_ref,
             e2_ref, *rest):
        ct_refs, dps_o, acc_refs, ext = rest[:len(cts)], rest[len(cts)], rest[len(cts) + 1:-1], rest[-1]
        ct_it = iter(ct_refs)
        ct_vals = []
        for terms in ct_terms:
            total = None
            for t in terms:
                if isinstance(t, tuple):
                    val = jnp.concatenate([next(ct_it)[...] for _ in t], axis=1)
                else:
                    val = next(ct_it)[...]
                total = val if total is None else total + val
            ct_vals.append(total)
        i = pl.program_id(0)
        first = (i % tps) == 0
        last = (i % tps) == tps - 1
        pv = p_ref[...]
        ext[pl.ds(0, 8), :] = jnp.where(first, 0.0, hp_ref[...])
        ext[pl.ds(8, tt), :] = pv
        ext[pl.ds(8 + tt, 8), :] = jnp.where(last, 0.0, hn_ref[...])
        prev = ext[pl.ds(7, tt), :]
        nxt = ext[pl.ds(9, tt), :]
        ps = pv + mup_ref[...] * (prev - pv) + mun_ref[...] * (nxt - pv)
        e2v = e2_ref[...]
        _, vjp = jax.vjp(lambda *a: _prep(*a, e2v), ps, w2b_ref[...], w0c_ref[...], a2b_ref[...], a0c_ref[...],
                         g2p_ref[...], kk_ref[...], ka_ref[...])
        grads = vjp(tuple(ct_vals))
        dps_o[...] = grads[0]

        @pl.when(i == 0)
        def _():
            for r in acc_refs:
                r[...] = jnp.zeros_like(r)
        for r, val in zip(acc_refs, grads[1:]):
            r[...] += val

    hp, hn = _halo_specs(SHIFT_PAD, 8, tt, n_tok, 0)
    fulls = [mu_p, mu_n, w2b, w0c, a2b, a0c, g2p, k_k, k_a, e2]
    return pl.pallas_call(
        body, name=name,
        out_shape=[jax.ShapeDtypeStruct((n_tok, SHIFT_PAD), F32)] + [jax.ShapeDtypeStruct(s, F32) for s in acc_shapes],
        grid=(n_tok // tt,),
        in_specs=[pl.BlockSpec((tt, SHIFT_PAD), lambda i: (i, 0)), hp, hn]
        + [pl.BlockSpec(a.shape, lambda i: (0, 0)) for a in fulls]
        + [pl.BlockSpec((tt, c.shape[1]), lambda i: (i, 0)) for c in cts],
        out_specs=[pl.BlockSpec((tt, SHIFT_PAD), lambda i: (i, 0))] + [pl.BlockSpec(s, lambda i: (0, 0)) for s in acc_shapes],
        scratch_shapes=[pltpu.VMEM((tt + 16, SHIFT_PAD), F32)],
        compiler_params=_cparams(("arbitrary",)),
    )(p, p, p, *fulls, *cts)


def _shift_bwd(dps, p, mu_p, mu_n, *, seq, tt, name):
    n_tok = p.shape[0]
    tps = seq // tt

    def body(d_ref, dhp_ref, dhn_ref, p_ref, php_ref, phn_ref, mup_ref, mun_ref, dp_o, dmup_o, dmun_o, ext):
        i = pl.program_id(0)
        first = (i % tps) == 0
        last = (i % tps) == tps - 1
        mup, mun = mup_ref[...], mun_ref[...]
        dv = d_ref[...]
        pv = p_ref[...]
        ext[pl.ds(0, 8), :] = jnp.where(first, 0.0, dhp_ref[...])
        ext[pl.ds(8, tt), :] = dv
        ext[pl.ds(8 + tt, 8), :] = jnp.where(last, 0.0, dhn_ref[...])
        d_prev = ext[pl.ds(7, tt), :]
        d_next = ext[pl.ds(9, tt), :]
        dp_o[...] = (dv * (1.0 - mup - mun) + d_next * mup + d_prev * mun).astype(dp_o.dtype)
        ext[pl.ds(0, 8), :] = jnp.where(first, 0.0, php_ref[...])
        ext[pl.ds(8, tt), :] = pv
        ext[pl.ds(8 + tt, 8), :] = jnp.where(last, 0.0, phn_ref[...])
        p_prev = ext[pl.ds(7, tt), :]
        p_next = ext[pl.ds(9, tt), :]

        @pl.when(i == 0)
        def _():
            dmup_o[...] = jnp.zeros_like(dmup_o)
            dmun_o[...] = jnp.zeros_like(dmun_o)
        dmup_o[...] += jnp.sum(dv * (p_prev - pv), axis=0, keepdims=True)
        dmun_o[...] += jnp.sum(dv * (p_next - pv), axis=0, keepdims=True)

    hp, hn = _halo_specs(SHIFT_PAD, 8, tt, n_tok, 0)
    tile = pl.BlockSpec((tt, SHIFT_PAD), lambda i: (i, 0))
    full = pl.BlockSpec((1, SHIFT_PAD), lambda i: (0, 0))
    return pl.pallas_call(
        body, name=name,
        out_shape=[jax.ShapeDtypeStruct((n_tok, SHIFT_PAD), BF16), jax.ShapeDtypeStruct((1, SHIFT_PAD), F32),
                   jax.ShapeDtypeStruct((1, SHIFT_PAD), F32)],
        grid=(n_tok // tt,),
        in_specs=[tile, hp, hn, tile, hp, hn, full, full],
        out_specs=[tile, full, full],
        scratch_shapes=[pltpu.VMEM((tt + 16, SHIFT_PAD), F32)],
        compiler_params=_cparams(("arbitrary",)),
    )(dps, dps, dps, p, p, p, mu_p, mu_n)


def _mix_post(y0, y1, r, v, kd, g, lnx_g, lnx_b, r_k, *, tt, name):
    e2 = _head_ones()
    return _rowcall(lambda i, *a: (_post(*a),), [y0, y1, r, v, kd, g], [lnx_g, lnx_b, r_k, e2], [(RW, BF16)], [],
                    tt=tt, name=name)[0]


def _mix_post_bwd(y0, y1, r, v, kd, g, lnx_g, lnx_b, r_k, dout, *, tt, name):
    e2 = _head_ones()

    def fn(i, y0v, y1v, rv, vv, kdv, gv, dov, lg, lb, rk, e2v):
        _, vjp = jax.vjp(lambda *a: _post(*a, e2v), y0v, y1v, rv, vv, kdv, gv, lg, lb, rk)
        gr = vjp(dov.astype(F32))
        return gr[0], gr[2], gr[3], gr[4], gr[5], gr[6], gr[7], gr[8]
    return _rowcall(fn, [y0, y1, r, v, kd, g, dout], [lnx_g, lnx_b, r_k, e2],
                    [(RW, F32), (RW, F32), (RW, F32), (2 * RW, F32), (RW, F32)], [(1, RW), (1, RW), (1, RW)],
                    tt=tt, name=name)


def _conv_fwd(p, dw, db, ln_g, ln_b, *, seq, tt, name):
    n_tok = p.shape[0]
    tps = seq // tt

    def glu(x, gate):
        return x * _sigmoid(gate)

    def body(u_ref, g_ref, uhp, ghp, uhn, ghn, dw_ref, db_ref, lg_ref, lb_ref, yc_o, y_o, ext):
        i = pl.program_id(0)
        first = (i % tps) == 0
        last = (i % tps) == tps - 1
        ext[pl.ds(0, 16), :] = jnp.where(first, 0.0, glu(uhp[...], ghp[...]))
        ext[pl.ds(16, tt), :] = glu(u_ref[...], g_ref[...])
        ext[pl.ds(16 + tt, 16), :] = jnp.where(last, 0.0, glu(uhn[...], ghn[...]))
        acc = jnp.zeros((tt, CW), F32) + db_ref[...]
        for k in range(CONV_K):
            acc = acc + ext[pl.ds(k + 1, tt), :] * dw_ref[pl.ds(k, 1), :]
        yc_o[...] = acc
        y_o[...] = _conv_post(acc, lg_ref[...], lb_ref[...]).astype(y_o.dtype)

    uhp_s, uhn_s = _halo_specs(CW, 16, tt, n_tok, 4)
    ghp_s, ghn_s = _halo_specs(CW, 16, tt, n_tok, 5)
    fulls = [dw, db, ln_g, ln_b]
    return pl.pallas_call(
        body, name=name,
        out_shape=[jax.ShapeDtypeStruct((n_tok, CW), F32), jax.ShapeDtypeStruct((n_tok, CW), BF16)],
        grid=(n_tok // tt,),
        in_specs=[pl.BlockSpec((tt, CW), lambda i: (i, 4)), pl.BlockSpec((tt, CW), lambda i: (i, 5)),
                  uhp_s, ghp_s, uhn_s, ghn_s] + [pl.BlockSpec(a.shape, lambda i: (0, 0)) for a in fulls],
        out_specs=[pl.BlockSpec((tt, CW), lambda i: (i, 0)), pl.BlockSpec((tt, CW), lambda i: (i, 0))],
        scratch_shapes=[pltpu.VMEM((tt + 32, CW), F32)],
        compiler_params=_cparams(("parallel",)),
    )(p, p, p, p, p, p, *fulls)


def _conv_post_bwd(yc, dy, ln_g, ln_b, *, tt, name):
    def fn(i, ycv, dyv, lg, lb):
        _, vjp = jax.vjp(_conv_post, ycv, lg, lb)
        dyc, dg, dbb = vjp(dyv.astype(F32))
        return dyc, dg, dbb, jnp.sum(dyc, axis=0, keepdims=True)
    return _rowcall(fn, [yc, dy], [ln_g, ln_b], [(CW, F32)], [(1, CW), (1, CW), (1, CW)], tt=tt, name=name)


def _conv_bwd(dyc, p, dw, *, seq, tt, name):
    n_tok = p.shape[0]
    tps = seq // tt

    def body(d_ref, dhp, dhn, u_ref, g_ref, uhp, ghp, uhn, ghn, dw_ref, dp_o, ddw_o, ext):
        i = pl.program_id(0)
        first = (i % tps) == 0
        last = (i % tps) == tps - 1
        dv = d_ref[...]
        ext[pl.ds(0, 16), :] = jnp.where(first, 0.0, dhp[...])
        ext[pl.ds(16, tt), :] = dv
        ext[pl.ds(16 + tt, 16), :] = jnp.where(last, 0.0, dhn[...])
        du = jnp.zeros((tt, CW), F32)
        for k in range(CONV_K):
            du = du + ext[pl.ds(31 - k, tt), :] * dw_ref[pl.ds(k, 1), :]
        uv, gv = u_ref[...], g_ref[...]
        sg = _sigmoid(gv)
        dp_o[:, 0:CW] = (du * sg).astype(dp_o.dtype)
        dp_o[:, CW:2 * CW] = (du * uv * sg * (1.0 - sg)).astype(dp_o.dtype)
        ext[pl.ds(0, 16), :] = jnp.where(first, 0.0, uhp[...] * _sigmoid(ghp[...]))
        ext[pl.ds(16, tt), :] = uv * sg
        ext[pl.ds(16 + tt, 16), :] = jnp.where(last, 0.0, uhn[...] * _sigmoid(ghn[...]))

        @pl.when(i == 0)
        def _():
            ddw_o[...] = jnp.zeros_like(ddw_o)
        for k in range(CONV_K):
            ddw_o[pl.ds(k, 1), :] += jnp.sum(dv * ext[pl.ds(k + 1, tt), :], axis=0, keepdims=True)

    dhp_s, dhn_s = _halo_specs(CW, 16, tt, n_tok, 0)
    uhp_s, uhn_s = _halo_specs(CW, 16, tt, n_tok, 4)
    ghp_s, ghn_s = _halo_specs(CW, 16, tt, n_tok, 5)
    return pl.pallas_call(
        body, name=name,
        out_shape=[jax.ShapeDtypeStruct((n_tok, 2 * CW), BF16), jax.ShapeDtypeStruct((32, CW), F32)],
        grid=(n_tok // tt,),
        in_specs=[pl.BlockSpec((tt, CW), lambda i: (i, 0)), dhp_s, dhn_s,
                  pl.BlockSpec((tt, CW), lambda i: (i, 4)), pl.BlockSpec((tt, CW), lambda i: (i, 5)),
                  uhp_s, ghp_s, uhn_s, ghn_s, pl.BlockSpec(dw.shape, lambda i: (0, 0))],
        out_specs=[pl.BlockSpec((tt, 2 * CW), lambda i: (i, 0)), pl.BlockSpec((32, CW), lambda i: (0, 0))],
        scratch_shapes=[pltpu.VMEM((tt + 32, CW), F32)],
        compiler_params=_cparams(("arbitrary",)),
    )(dyc, dyc, dyc, p, p, p, p, p, p, dw)


CHUNK = 64
_MM_DIMS = {"nn": (((2,), (1,)), ((0,), (0,))), "nt": (((2,), (2,)), ((0,), (0,))), "tn": (((1,), (1,)), ((0,), (0,)))}


def _mm16_raw(a, b, mode, fine):
    dot = lambda x, y: lax.dot_general(x, y, _MM_DIMS[mode], preferred_element_type=F32)
    if not fine:
        return dot(a.astype(BF16), b.astype(BF16))
    ah, (bh, bl) = a.astype(BF16), _split16(b)
    return dot(ah, bh) + dot(ah, bl)


@functools.partial(jax.custom_vjp, nondiff_argnums=(2, 3))
def _mm16(a, b, mode, fine=False):
    return _mm16_raw(a, b, mode, fine)


def _mm16_fwd(a, b, mode, fine):
    return _mm16_raw(a, b, mode, fine), (a, b)


def _mm16_bwd(mode, fine, res, g):
    a, b = res
    if mode == "nn":
        return _mm16_raw(g, b, "nt", fine), _mm16_raw(a, g, "tn", fine)
    if mode == "nt":
        return _mm16_raw(g, b, "nn", fine), _mm16_raw(g, a, "tn", fine)
    return _mm16_raw(b, g, "nt", fine), _mm16_raw(a, g, "nn", fine)


_mm16.defvjp(_mm16_fwd, _mm16_bwd)


def _tri_sum_raw(x, tri, mode):
    hi = x.astype(BF16)
    r1 = x - hi.astype(F32)
    mid = r1.astype(BF16)
    lo = (r1 - mid.astype(F32)).astype(BF16)
    dot = lambda p: lax.dot_general(tri, p, _MM_DIMS[mode], preferred_element_type=F32)
    return dot(hi) + dot(mid) + dot(lo)


@jax.custom_vjp
def _tri_sum(x, tri):
    return _tri_sum_raw(x, tri, "nn")


def _tri_sum_fwd(x, tri):
    return _tri_sum_raw(x, tri, "nn"), tri


def _tri_sum_bwd(tri, g):
    return _tri_sum_raw(g, tri, "tn"), jnp.zeros_like(tri)


_tri_sum.defvjp(_tri_sum_fwd, _tri_sum_bwd)


def _chunk_step(s0, r, lw, k, v, kk, b, tri, rev):
    nh, n, _ = r.shape
    row = lax.broadcasted_iota(jnp.int32, (nh, n, n), 1)
    col = lax.broadcasted_iota(jnp.int32, (nh, n, n), 2)
    if rev:
        row, col = col, row
    cum = _tri_sum(lw, tri)
    up, down = jnp.exp(cum), jnp.exp(-cum)
    at, rt = -kk * jnp.exp(cum - lw), r * up
    kt, bt = k * down, b * down
    a_ab = jnp.where(col < row, _mm16(at, bt, "nt", True), 0.0)
    a_ak = jnp.where(col < row, _mm16(at, kt, "nt", True), 0.0)
    a_rb = jnp.where(col <= row, _mm16(rt, bt, "nt", True), 0.0)
    a_rk = jnp.where(col <= row, _mm16(rt, kt, "nt", True), 0.0)
    u = _mm16(at, s0, "nt") + _mm16(a_ak, v, "nn")
    power = a_ab
    steps = n.bit_length() - 1
    for it in range(steps):
        u = u + _mm16(power, u, "nn")
        if it + 1 < steps:
            power = _mm16(power, power, "nn")
    y = _mm16(rt, s0, "nt") + _mm16(a_rk, v, "nn") + _mm16(a_rb, u, "nn")
    grown = s0 + _mm16(v, kt, "tn") + _mm16(u, bt, "tn")
    return y, grown * jnp.exp(jnp.sum(lw, axis=1, keepdims=True))


N_HEADS = RW // HEAD


SCAN_SEQS = 4


def _tri_ones(rev, nseq):
    shape = (nseq * N_HEADS, CHUNK, CHUNK)
    row, col = lax.broadcasted_iota(jnp.int32, shape, 1), lax.broadcasted_iota(jnp.int32, shape, 2)
    return ((col >= row) if rev else (col <= row)).astype(BF16)


def _split_heads(ref):
    return jnp.stack([ref[q, :, pl.ds(h * HEAD, HEAD)] for q in range(ref.shape[0]) for h in range(N_HEADS)])


def _merge_heads(ref, val):
    for q in range(ref.shape[0]):
        for h in range(N_HEADS):
            ref[q, :, pl.ds(h * HEAD, HEAD)] = val[q * N_HEADS + h]


def _chunk_specs(nseq, nc, rev, dcol):
    chunk = (lambda c: nc - 1 - c) if rev else (lambda c: c)
    shared = pl.BlockSpec((nseq, CHUNK, RW), lambda s, c: (s, chunk(c), 0))
    own = pl.BlockSpec((nseq, CHUNK, RW), lambda s, c: (s, chunk(c), dcol))
    return shared, own


def _wkv_chunk_fwd(r, lw, k, v, kk, b, *, rev, name):
    bsz, seq, _ = r.shape
    nc = seq // CHUNK
    nseq = SCAN_SEQS if bsz % SCAN_SEQS == 0 else 1
    shared, own = _chunk_specs(nseq, nc, rev, int(rev))

    def body(r_ref, lw_ref, k_ref, v_ref, kk_ref, b_ref, tri_ref, y_o, s0_o, s_ref):
        @pl.when(pl.program_id(1) == 0)
        def _():
            s_ref[...] = jnp.zeros_like(s_ref)

        s0 = s_ref[...]
        s0_o[:, 0] = s0.reshape(nseq, N_HEADS, HEAD, HEAD)
        y, s_ref[...] = _chunk_step(s0, *[_split_heads(x) for x in (r_ref, lw_ref, k_ref, v_ref, kk_ref, b_ref)],
                                    tri_ref[...], rev)
        _merge_heads(y_o, y)

    return pl.pallas_call(
        body, name=name,
        out_shape=[jax.ShapeDtypeStruct((bsz, seq, RW), F32), jax.ShapeDtypeStruct((bsz, nc, N_HEADS, HEAD, HEAD), F32)],
        grid=(bsz // nseq, nc),
        in_specs=[shared, own, own, shared, shared, own,
                  pl.BlockSpec((nseq * N_HEADS, CHUNK, CHUNK), lambda s, c: (0, 0, 0))],
        out_specs=[shared, pl.BlockSpec((nseq, 1, N_HEADS, HEAD, HEAD), lambda s, c: (s, c, 0, 0, 0))],
        scratch_shapes=[pltpu.VMEM((nseq * N_HEADS, HEAD, HEAD), F32)],
        compiler_params=_cparams(("parallel", "arbitrary")),
    )(r, lw, k, v, kk, b, _tri_ones(rev, nseq))


def _wkv_chunk_bwd(r, lw, k, v, kk, b, dy, s0, *, rev, name):
    bsz, seq, _ = r.shape
    nc = seq // CHUNK
    nseq = SCAN_SEQS if bsz % SCAN_SEQS == 0 else 1
    shared, own = _chunk_specs(nseq, nc, not rev, int(rev))

    def body(r_ref, lw_ref, k_ref, v_ref, kk_ref, b_ref, dy_ref, s0_ref, tri_ref, *rest):
        outs, ds_ref = rest[:-1], rest[-1]

        @pl.when(pl.program_id(1) == 0)
        def _():
            ds_ref[...] = jnp.zeros_like(ds_ref)

        triv = tri_ref[...]
        _, vjp = jax.vjp(lambda *a: _chunk_step(*a, triv, rev), s0_ref[:, 0].reshape(nseq * N_HEADS, HEAD, HEAD),
                         *[_split_heads(x) for x in (r_ref, lw_ref, k_ref, v_ref, kk_ref, b_ref)])
        grads = vjp((_split_heads(dy_ref), ds_ref[...]))
        ds_ref[...] = grads[0]
        for o, gval in zip(outs, grads[1:]):
            _merge_heads(o, gval)

    return pl.pallas_call(
        body, name=name,
        out_shape=[jax.ShapeDtypeStruct((bsz, seq, RW), F32)] * 6,
        grid=(bsz // nseq, nc),
        in_specs=[shared, own, own, shared, shared, own, shared,
                  pl.BlockSpec((nseq, 1, N_HEADS, HEAD, HEAD), lambda s, c: (s, nc - 1 - c, 0, 0, 0)),
                  pl.BlockSpec((nseq * N_HEADS, CHUNK, CHUNK), lambda s, c: (0, 0, 0))],
        out_specs=[shared] * 6,
        scratch_shapes=[pltpu.VMEM((nseq * N_HEADS, HEAD, HEAD), F32)],
        compiler_params=_cparams(("parallel", "arbitrary")),
    )(r, lw, k, v, kk, b, dy, s0, _tri_ones(rev, nseq))


def _block_diag2(w):
    z = jnp.zeros_like(w[0])
    return jnp.concatenate([jnp.concatenate([w[0], z], axis=1), jnp.concatenate([z, w[1]], axis=1)], axis=0)


def _pad_in_cols(a):
    z = jnp.zeros(a.shape[:-1] + (SHIFT_PAD - SHIFT_COLS,), a.dtype)
    return jnp.concatenate([a[..., :SHIFT_COLS], z, a[..., SHIFT_COLS:]], axis=-1)


def _follow(small, token):
    return small if token is None else small + token[0:1, 0:1]


def _local_step(x, target, wts, *, tt, start_token=None, more_weights=None, grads_ready=None, small_ready=None):
    bsz, seq, _ = x.shape
    n_tok = bsz * seq
    row = lambda a: a.reshape(1, -1).astype(F32)
    x0 = x.reshape(n_tok, D_MODEL)
    tgt = target.reshape(n_tok, D_MODEL)
    ln = {k: row(wts[k]) for k in ("ln1_g", "ln1_b", "ln2_g", "ln2_b", "ln3_g", "ln3_b")}
    if grads_ready is None:
        grads_ready = lambda names, slabs: None

    w1i = wts["ffn1_w_in"]
    h1, act1 = _ffn_in(x0, w1i, tm=TM_FFN, after=start_token, name="ffn1_in")
    if more_weights is not None:
        wts = {**wts, **more_weights("ffn1_out", act1)}
    w1o = wts["ffn1_w_out"]
    z1, x1, x1b = _mm_ln([act1], w1o, x0, ln["ln1_g"], ln["ln1_b"], 0.5, tm=TM_LN, name="ffn1_out_ln1")
    if more_weights is not None:
        wts = {**wts, **more_weights("mix", x1b)}
    win = _pad_in_cols(wts["w_in"])
    zpad = jnp.zeros((1, SHIFT_PAD - SHIFT_COLS), F32)
    mu_p = jnp.concatenate([row(wts["mu_prev"]), zpad], axis=1)
    mu_n = jnp.concatenate([row(wts["mu_next"]), zpad], axis=1)
    w2b, a2b = _block_diag2(wts["w2"]), _block_diag2(wts["a2"])
    w0c, a0c = row(wts["w0"]), row(wts["a0"])
    g2p = jnp.concatenate([wts["g2"], jnp.zeros((GATE_PAD - GATE_LORA, RW), F32)], axis=0)
    k_k, k_a, r_k = row(wts["k_k"]), row(wts["k_a"]), row(wts["r_k"])
    lnx_g, lnx_b = row(wts["lnx_g"]), row(wts["lnx_b"])
    cdw, cb, clg, clb = wts["conv_dw"], row(wts["conv_b"]), row(wts["conv_ln_g"]), row(wts["conv_ln_b"])
    small = (mu_p, mu_n, w2b, w0c, a2b, a0c, g2p, k_k, k_a)
    seq3 = lambda a: a.reshape(bsz, seq, a.shape[-1])
    flat = lambda a: a.reshape(n_tok, a.shape[-1])

    p = _matmul(x1b, win, name="proj_in")
    r, v, kk, w, kd, b, g = _mix_prep(p, *small, seq=seq, tt=tt, name="mix_prep")
    scan_in = [seq3(a) for a in (r, w, kd, v, kk, b)]
    y0, s_chunks0 = _wkv_chunk_fwd(*scan_in, rev=False, name="wkv_fwd_dir0")
    y1, s_chunks1 = _wkv_chunk_fwd(*scan_in, rev=True, name="wkv_fwd_dir1")
    y0, y1 = flat(y0), flat(y1)
    yr = _mix_post(y0, y1, r, v, kd, g, lnx_g, lnx_b, r_k, tt=tt, name="mix_post")
    yc, yv = _conv_fwd(p, cdw, cb, clg, clb, seq=seq, tt=tt, name="conv_fwd")
    if more_weights is not None:
        wts = {**wts, **more_weights("out", yr)}
    wout, w2i, w2o = wts["w_out"], wts["ffn2_w_in"], wts["ffn2_w_out"]
    z2, x2, x2b = _mm_ln([yr, yv], wout, x1, ln["ln2_g"], ln["ln2_b"], 1.0, tm=TM_LN, name="proj_out_ln2")
    h2, act2 = _ffn_in(x2b, w2i, tm=TM_FFN, name="ffn2_in")

    gr = {}
    slab_rows = lambda a: a.reshape((N_CHIPS, a.shape[0] // N_CHIPS) + a.shape[1:])
    dw_kw = dict(ta=True, out_dtype=BF16)
    dz3, gr["ln3_g"], gr["ln3_b"], loss_part = _mm_ln_loss(act2, w2o, x2, ln["ln3_g"], ln["ln3_b"], tgt, 0.5, tm=TM_LN,
                                                           name="ffn2_out_ln3_loss")
    dh2 = _ffn_out_bwd(dz3, w2o, h2, tm=TM_FFN, name="ffn2_out_dx")
    gr["ffn2_w_out"] = slab_rows(_matmul(act2, dz3, scale=0.5, tm=D_FF // 2, name="ffn2_out_dw", **dw_kw))
    dz2, gr["ln2_g"], gr["ln2_b"] = _mm_nt_res([dh2], w2i, dz3, ln=(z2, ln["ln2_g"], ln["ln2_b"]), tm=TM_FFN,
                                               name="ffn2_in_dx_ln2")
    gr["ffn2_w_in"] = _matmul(x2b, dh2, col_slabs=True, tn=2 * D_FF // N_CHIPS, name="ffn2_in_dw", **dw_kw)
    dmix = _matmul(dz2, wout, tb=True, name="proj_out_dx")
    gr["w_out"] = slab_rows(jnp.concatenate([_matmul(yr, dz2, name="proj_out_dw_rwkv", **dw_kw),
                                             _matmul(yv, dz2, name="proj_out_dw_conv", **dw_kw)], axis=0))
    tok = grads_ready(("ffn2_w_out", "ffn2_w_in", "w_out"), [gr["ffn2_w_out"], gr["ffn2_w_in"], gr["w_out"]])
    dyr, dyv = (dmix, RW, 0), (dmix, RW, 1)
    dy, dr_p, dv_p, dkd_p, dg, gr["lnx_g"], gr["lnx_b"], gr["r_k"] = _mix_post_bwd(
        y0, y1, r, v, kd, g, _follow(lnx_g, tok), lnx_b, r_k, dyr, tt=tt, name="mix_post_bwd")
    dr0, dw0, dkd0, dv0, dk0, db0 = [flat(a) for a in _wkv_chunk_bwd(*scan_in, seq3(dy), s_chunks0, rev=False,
                                                                      name="wkv_bwd_dir0")]
    dr1, dw1, dkd1, dv1, dk1, db1 = [flat(a) for a in _wkv_chunk_bwd(*scan_in, seq3(dy), s_chunks1, rev=True,
                                                                      name="wkv_bwd_dir1")]
    ct_terms = [[dr_p, dr0, dr1], [dv_p, dv0, dv1], [dk0, dk1], [(dw0, dw1)], [dkd_p, (dkd0, dkd1)], [(db0, db1)], [dg]]
    dyc, gr["conv_ln_g"], gr["conv_ln_b"], gr["conv_b"] = _conv_post_bwd(yc, dyv, clg, clb, tt=tt, name="conv_post_bwd")
    dpc, ddw = _conv_bwd(dyc, p, cdw, seq=seq, tt=tt, name="conv_bwd")
    gr["conv_dw"] = ddw[:CONV_K]
    dps, dw2b, dw0c, da2b, da0c, dg2p, gr["k_k"], gr["k_a"] = _mix_prep_bwd(
        p, *small, ct_terms, seq=seq, tt=tt, name="mix_prep_bwd")
    gr["w2"] = jnp.stack([dw2b[:LORA, :RW], dw2b[LORA:, RW:]])
    gr["a2"] = jnp.stack([da2b[:LORA, :RW], da2b[LORA:, RW:]])
    gr["w0"], gr["a0"], gr["g2"] = dw0c.reshape(2, RW), da0c.reshape(2, RW), dg2p[:GATE_LORA]
    dpsh, dmu_p, dmu_n = _shift_bwd(dps, p, mu_p, mu_n, seq=seq, tt=tt, name="shift_bwd")
    gr["mu_prev"], gr["mu_next"] = dmu_p[:, :SHIFT_COLS], dmu_n[:, :SHIFT_COLS]
    dwin = jnp.concatenate([_matmul(x1b, dpsh, name="proj_in_dw_shift", **dw_kw)[:, :SHIFT_COLS],
                            _matmul(x1b, dpc, name="proj_in_dw_conv", **dw_kw)], axis=1)
    gr["w_in"] = jnp.moveaxis(dwin.reshape(D_MODEL, N_CHIPS, IN_COLS // N_CHIPS), 1, 0)
    tok = grads_ready(("w_in",), [gr["w_in"]])
    dz1, gr["ln1_g"], gr["ln1_b"] = _mm_nt_res([dpsh, dpc], win, dz2, ln=(z1, ln["ln1_g"], ln["ln1_b"]), tm=TM_FFN,
                                               after=tok, name="proj_in_dx_ln1")
    gr["loss"] = loss_part
    tok = small_ready(gr, loss_part) if small_ready is not None else None
    dh1 = _ffn_out_bwd(dz1, w1o, h1, tm=TM_FFN, after=tok, name="ffn1_out_dx")
    gr["ffn1_w_out"] = slab_rows(_matmul(act1, dz1, scale=0.5, tm=D_FF // 2, name="ffn1_out_dw", **dw_kw))
    tok = grads_ready(("ffn1_w_out",), [gr["ffn1_w_out"]])
    gr["ffn1_w_in"] = _matmul(x0, dh1, col_slabs=True, tn=2 * D_FF // N_CHIPS, after=tok, name="ffn1_in_dw", **dw_kw)
    tok = grads_ready(("ffn1_w_in",), [gr["ffn1_w_in"]])
    dx0 = _mm_nt_res([dh1], w1i, dz1, tm=TM_FFN, after=tok, name="ffn1_in_dx")
    return dx0.reshape(bsz, seq, D_MODEL), gr


def _mesh_pos():
    return lax.axis_index("x"), lax.axis_index("y"), lax.axis_index("c")


def _other_chips(x, y):
    return [(1 - x, y), (x, 1 - y), (1 - x, 1 - y)]


def _gather_chips(shards, *, name):
    n = len(shards)
    halves = [s.shape[0] // 2 for s in shards]
    assert all(2 * h == s.shape[0] for h, s in zip(halves, shards))

    def body(*refs):
        ins, outs = refs[:n], refs[n:2 * n]
        send_sems, recv_sems, fwd_send_sems, fwd_recv_sems, loc_sems = refs[2 * n:]
        x, y, c = _mesh_pos()
        q = 2 * x + y
        peers = _other_chips(x, y)
        local = [pltpu.make_async_copy(ins[a], outs[a].at[q], loc_sems.at[a]) for a in range(n)]
        for cp in local:
            cp.start()

        def half(a, chip, core):
            return outs[a].at[chip, pl.ds(core * halves[a], halves[a])]

        sends = [pltpu.make_async_remote_copy(ins[a].at[pl.ds(c * halves[a], halves[a])], half(a, q, c),
                                              send_sems.at[a, k], recv_sems.at[a, k],
                                              device_id=(px, py, c), device_id_type=MESH)
                 for a in range(n) for k, (px, py) in enumerate(peers)]
        for cp in sends:
            cp.start()
        passed = []
        for a in range(n):
            for k, (px, py) in enumerate(peers):
                mine = half(a, 2 * px + py, c)
                pltpu.make_async_remote_copy(mine, mine, send_sems.at[a, k], recv_sems.at[a, k],
                                             device_id=(px, py, c), device_id_type=MESH).wait_recv()
                cp = pltpu.make_async_remote_copy(mine, mine, fwd_send_sems.at[a, k], fwd_recv_sems.at[a, k],
                                                  device_id=(x, y, 1 - c), device_id_type=MESH)
                cp.start()
                passed.append(cp)
        for a in range(n):
            for k, (px, py) in enumerate(peers):
                theirs = half(a, 2 * px + py, 1 - c)
                pltpu.make_async_remote_copy(theirs, theirs, fwd_send_sems.at[a, k], fwd_recv_sems.at[a, k],
                                             device_id=(x, y, 1 - c), device_id_type=MESH).wait_recv()
        for cp in sends + passed:
            cp.wait_send()
        for cp in local:
            cp.wait()

    any_spec = pl.BlockSpec(memory_space=pl.ANY)
    return pl.pallas_call(
        body, name=name,
        out_shape=[jax.ShapeDtypeStruct((N_CHIPS,) + s.shape, s.dtype) for s in shards],
        in_specs=[any_spec] * n, out_specs=[any_spec] * n,
        scratch_shapes=[pltpu.SemaphoreType.DMA((n, 3))] * 4 + [pltpu.SemaphoreType.DMA((n,))],
        compiler_params=pltpu.CompilerParams(has_side_effects=True),
    )(*shards)


HBM_SPEC = pl.BlockSpec(memory_space=pltpu.HBM)
SEM_SPEC = pl.BlockSpec(memory_space=pltpu.SEMAPHORE)
ANY_SPEC = pl.BlockSpec(memory_space=pl.ANY)
SIDE_EFFECT = pltpu.SideEffectType.DATAFLOW_SIDE_EFFECTING


def _chip_copies(src_refs, land_refs, send_sems, recv_sems, scatter, arriving=False):
    x, y, c = _mesh_pos()
    cps = []
    for a, (src, land) in enumerate(zip(src_refs, land_refs)):
        for k, (px, py) in enumerate(_other_chips(x, y)):
            slot = k if scatter else (2 * px + py if arriving else 2 * x + y)
            cps.append(pltpu.make_async_remote_copy(src.at[2 * px + py] if scatter else src, land.at[slot],
                                                    send_sems.at[3 * a + k], recv_sems.at[3 * a + k],
                                                    device_id=(px, py, c), device_id_type=MESH))
    return cps


def _exchange_start(srcs, *, scatter, after, name):
    n = len(srcs)
    lands = [lax.empty((3,) + s.shape[1:] if scatter else (N_CHIPS,) + s.shape, s.dtype) for s in srcs]

    def body(*refs):
        src_refs, land_refs = refs[:n], refs[n:2 * n]
        send_sems, recv_sems = refs[2 * n + 1:2 * n + 3]
        token = refs[-1]
        for cp in _chip_copies(src_refs, land_refs, send_sems, recv_sems, scatter):
            cp.start()
        token[...] = jnp.zeros_like(token)

    hbm = lambda a: pltpu.with_memory_space_constraint(a, pltpu.HBM)
    outs = pl.pallas_call(
        body, name=name,
        out_shape=(pltpu.SemaphoreType.DMA((3 * n,)), pltpu.SemaphoreType.DMA((3 * n,)),
                   *[pltpu.HBM(a.shape, a.dtype) for a in srcs + lands], jax.ShapeDtypeStruct((8, LANES), F32)),
        in_specs=[HBM_SPEC] * (2 * n) + [ANY_SPEC],
        out_specs=(SEM_SPEC, SEM_SPEC, *[HBM_SPEC] * (2 * n), pl.BlockSpec(memory_space=pltpu.VMEM)),
        input_output_aliases={i: 2 + i for i in range(2 * n)},
        compiler_params=pltpu.CompilerParams(has_side_effects=SIDE_EFFECT),
    )(*[hbm(a) for a in srcs + lands], after)
    return outs[0], outs[1], list(outs[2:2 + n]), list(outs[2 + n:2 + 2 * n]), outs[-1]


def _exchange_wait(started, *, scatter, after, name):
    send_sems, recv_sems, srcs, lands, _ = started
    n = len(srcs)

    def body(*refs):
        src_refs, land_refs = refs[:n], refs[n:2 * n]
        send_s, recv_s = refs[2 * n:2 * n + 2]
        for cp in _chip_copies(src_refs, land_refs, send_s, recv_s, scatter, arriving=True):
            cp.wait_send()
            cp.wait_recv()

    outs = pl.pallas_call(
        body, name=name,
        out_shape=tuple(pltpu.HBM(a.shape, a.dtype) for a in srcs + lands),
        in_specs=[HBM_SPEC] * (2 * n) + [SEM_SPEC, SEM_SPEC, ANY_SPEC],
        out_specs=tuple([HBM_SPEC] * (2 * n)),
        input_output_aliases={i: i for i in range(2 * n)},
        compiler_params=pltpu.CompilerParams(has_side_effects=SIDE_EFFECT),
    )(*srcs, *lands, send_sems, recv_sems, after)
    return list(outs[:n]), list(outs[n:])


def _by_chip(own, land):
    xi, yi, _ = _mesh_pos()
    return lax.dynamic_update_index_in_dim(land, own, 2 * xi + yi, 0)


def _swap_sibling(arrs, *, name):
    n = len(arrs)

    def body(*refs):
        ins, outs = refs[:n], refs[n:2 * n]
        send_sems, recv_sems = refs[2 * n:]
        x, y, c = _mesh_pos()
        cps = [pltpu.make_async_remote_copy(ins[a], outs[a], send_sems.at[a], recv_sems.at[a],
                                            device_id=(x, y, 1 - c), device_id_type=MESH) for a in range(n)]
        for cp in cps:
            cp.start()
        for cp in cps:
            cp.wait_recv()
        for cp in cps:
            cp.wait_send()

    any_spec = pl.BlockSpec(memory_space=pl.ANY)
    return pl.pallas_call(
        body, name=name,
        out_shape=[jax.ShapeDtypeStruct(s.shape, s.dtype) for s in arrs],
        in_specs=[any_spec] * n, out_specs=[any_spec] * n,
        scratch_shapes=[pltpu.SemaphoreType.DMA((n,)), pltpu.SemaphoreType.DMA((n,))],
        compiler_params=pltpu.CompilerParams(has_side_effects=True),
    )(*arrs)


def _device_copies(v_ref, land_ref, send_sems, recv_sems, arriving=False):
    x, y, c = _mesh_pos()
    me = 4 * x + 2 * y + c
    cps = []
    for m in range(1, 8):
        px, py, pc = (x + ((m >> 2) & 1)) % 2, (y + ((m >> 1) & 1)) % 2, (c + (m & 1)) % 2
        slot = 4 * px + 2 * py + pc if arriving else me
        cps.append(pltpu.make_async_remote_copy(v_ref, land_ref.at[slot], send_sems.at[m - 1], recv_sems.at[m - 1],
                                                device_id=(px, py, pc), device_id_type=MESH))
    return cps


def _allsum_start(vec, *, after, name):
    land = lax.empty((8,) + vec.shape, F32)

    def body(v_ref, land_ref, _after, send_sems, recv_sems, v_thru, land_thru, token):
        for cp in _device_copies(v_ref, land_ref, send_sems, recv_sems):
            cp.start()
        token[...] = jnp.zeros_like(token)

    hbm = lambda a: pltpu.with_memory_space_constraint(a, pltpu.HBM)
    return pl.pallas_call(
        body, name=name,
        out_shape=(pltpu.SemaphoreType.DMA((7,)), pltpu.SemaphoreType.DMA((7,)), pltpu.HBM(vec.shape, F32),
                   pltpu.HBM(land.shape, F32), jax.ShapeDtypeStruct((8, LANES), F32)),
        in_specs=[HBM_SPEC, HBM_SPEC, ANY_SPEC],
        out_specs=(SEM_SPEC, SEM_SPEC, HBM_SPEC, HBM_SPEC, pl.BlockSpec(memory_space=pltpu.VMEM)),
        input_output_aliases={0: 2, 1: 3},
        compiler_params=pltpu.CompilerParams(has_side_effects=SIDE_EFFECT),
    )(hbm(vec), hbm(land), after)


def _allsum_wait(started, *, after, name):
    send_sems, recv_sems, vec, land, _ = started

    def body(v_ref, land_ref, send_s, recv_s, _after, v_dead, got):
        for cp in _device_copies(v_ref, land_ref, send_s, recv_s, arriving=True):
            cp.wait_send()
            cp.wait_recv()

    vec, land = pl.pallas_call(
        body, name=name,
        out_shape=(pltpu.HBM(vec.shape, F32), pltpu.HBM(land.shape, F32)),
        in_specs=[HBM_SPEC, HBM_SPEC, SEM_SPEC, SEM_SPEC, ANY_SPEC],
        out_specs=(HBM_SPEC, HBM_SPEC),
        input_output_aliases={0: 0, 1: 1},
        compiler_params=pltpu.CompilerParams(has_side_effects=SIDE_EFFECT),
    )(vec, land, send_sems, recv_sems, after)
    xi, yi, ci = _mesh_pos()
    every = lax.dynamic_update_index_in_dim(land, vec, 4 * xi + 2 * yi + ci, 0)

    def add(e_ref, o_ref):
        acc = e_ref[0]
        for d in range(1, 8):
            acc = acc + e_ref[d]
        o_ref[...] = acc

    vm = pl.BlockSpec(memory_space=pltpu.VMEM)
    return pl.pallas_call(add, name=name + "_sum", out_shape=jax.ShapeDtypeStruct(vec.shape, F32), in_specs=[vm],
                          out_specs=vm, compiler_params=_cparams())(every)


def _adamw(w, g, m, v):
    m = ADAM_B1 * m + (1.0 - ADAM_B1) * g
    v = ADAM_B2 * v + (1.0 - ADAM_B2) * (g * g)
    m_hat = m / (1.0 - ADAM_B1 ** ADAM_STEP)
    v_hat = v / (1.0 - ADAM_B2 ** ADAM_STEP)
    delta = -ADAM_LR * (m_hat / (jnp.sqrt(v_hat) + ADAM_EPS) + ADAM_WD * w)
    return delta, m, v


def _sum4(mine, land, *, name):
    rows, cols = mine.shape
    tr = _pick_rows(rows)

    def body(a_ref, l_ref, o_ref):
        o_ref[...] = (a_ref[...].astype(F32) + l_ref[0].astype(F32)) + (l_ref[1].astype(F32) + l_ref[2].astype(F32))

    return pl.pallas_call(
        body, name=name, out_shape=jax.ShapeDtypeStruct((rows, cols), F32), grid=(rows // tr,),
        in_specs=[pl.BlockSpec((tr, cols), lambda i: (i, 0)), pl.BlockSpec((3, tr, cols), lambda i: (0, i, 0))],
        out_specs=pl.BlockSpec((tr, cols), lambda i: (i, 0)),
        compiler_params=_cparams(("parallel",)),
    )(mine, land)


def _pick_rows(rows, want=256):
    for t in range(min(want, rows) // 8 * 8, 0, -8):
        if rows % t == 0:
            return t
    return rows


def _sum_adam(h_mine, h_sib, w, m, v, *, name):
    rows, cols = w.shape
    tr = _pick_rows(rows)

    def body(a_ref, b_ref, w_ref, m_ref, v_ref, g_o, d_o, m_o, v_o):
        g = a_ref[...] + b_ref[...]
        d, mn, vn = _adamw(w_ref[...], g, m_ref[...], v_ref[...])
        g_o[...], d_o[...], m_o[...], v_o[...] = g, d, mn, vn

    spec = pl.BlockSpec((tr, cols), lambda i: (i, 0))
    return pl.pallas_call(
        body, name=name, out_shape=[jax.ShapeDtypeStruct((rows, cols), F32)] * 4, grid=(rows // tr,),
        in_specs=[spec] * 5, out_specs=[spec] * 4, compiler_params=_cparams(("parallel",)),
    )(h_mine, h_sib, w, m, v)


def _adam_rows(w, g, m, v, *, name):
    def body(w_ref, g_ref, m_ref, v_ref, d_o, m_o, v_o):
        d_o[...], m_o[...], v_o[...] = _adamw(w_ref[...], g_ref[...], m_ref[...], v_ref[...])

    vm = pl.BlockSpec(memory_space=pltpu.VMEM)
    return pl.pallas_call(
        body, name=name, out_shape=[jax.ShapeDtypeStruct(w.shape, F32)] * 3,
        in_specs=[vm] * 4, out_specs=[vm] * 3, compiler_params=_cparams(),
    )(w, g, m, v)


def _size(shape):
    size = 1
    for d in shape:
        size *= d
    return size


def _pack_rows(arrs):
    blocks = []
    for a in arrs:
        flat = a.reshape(-1).astype(F32)
        flat = jnp.concatenate([flat, jnp.zeros((-flat.shape[0] % (8 * LANES),), F32)])
        blocks.append(flat.reshape(-1, LANES))
    return jnp.concatenate(blocks, axis=0)


def _unpack_rows(packed, shapes):
    out, row = [], 0
    for s in shapes:
        rows = -(-_size(s) // (8 * LANES)) * 8
        out.append(packed[row:row + rows].reshape(-1)[:_size(s)].reshape(s))
        row += rows
    return out


WEIGHTS = ['ffn1_w_in', 'ffn1_w_out', 'w_in', 'mu_prev', 'mu_next', 'w0', 'w2', 'a0', 'a2', 'g2', 'k_k', 'k_a', 'r_k',
           'lnx_g', 'lnx_b', 'conv_dw', 'conv_b', 'conv_ln_g', 'conv_ln_b', 'w_out', 'ffn2_w_in', 'ffn2_w_out',
           'ln1_g', 'ln1_b', 'ln2_g', 'ln2_b', 'ln3_g', 'ln3_b']
COL_SHARDED = ('ffn1_w_in', 'w_in', 'ffn2_w_in')
ROW_SHARDED = ('ffn1_w_out', 'w_out', 'ffn2_w_out')
BIG = COL_SHARDED + ROW_SHARDED
SMALL_SHARDED = ('w0', 'w2', 'a0', 'a2', 'g2', 'conv_dw')
REPLICATED = tuple(n for n in WEIGHTS if n not in BIG + SMALL_SHARDED)


def _train_step(x, target, w, m, v, *, tt):
    xi, yi, _ = _mesh_pos()
    q = 2 * xi + yi

    later = {"ffn1_out": ("ffn1_w_out",), "mix": ("w_in",) + SMALL_SHARDED, "out": ("w_out", "ffn2_w_in", "ffn2_w_out")}
    shard = lambda n: w[n][0].astype(BF16) if n in BIG else w[n][0]
    small_names = REPLICATED + SMALL_SHARDED

    def whole(n, slabs):
        if n in ROW_SHARDED:
            return slabs.reshape((-1,) + slabs.shape[2:])
        if n in ("ffn1_w_in", "ffn2_w_in"):
            return slabs
        return jnp.moveaxis(slabs, 0, -2).reshape(slabs.shape[1:-1] + (N_CHIPS * slabs.shape[-1],))

    full = {n: w[n][0] for n in REPLICATED}
    first = _gather_chips([shard("ffn1_w_in")], name="gather_ffn1_in")
    full["ffn1_w_in"] = whole("ffn1_w_in", first[0])
    started, token = {}, first[0]
    for stage, names in later.items():
        started[stage] = _exchange_start([shard(n) for n in names], scatter=False, after=token,
                                         name="gather_%s_start" % stage)
        token = started[stage][-1]

    def more_weights(stage, after):
        own, land = _exchange_wait(started[stage], scatter=False, after=after, name="gather_%s_wait" % stage)
        got = {n: whole(n, _by_chip(o, l)) for n, o, l in zip(later[stage], own, land)}
        full.update(got)
        return got

    small_sent = []

    def small_ready(gr, loss_part):
        vec = _pack_rows([gr[n] for n in small_names] + [loss_part[0:1, 0:1]])
        small_sent.append(_allsum_start(vec, after=vec, name="reduce_small_start"))
        return small_sent[0][-1]

    sent = []

    def grads_ready(names, slabs):
        started = _exchange_start(slabs, scatter=True, after=slabs[0], name="scatter_%s_start" % names[0])
        sent.append((names, started))
        return started[-1]

    grad_x, gr = _local_step(x, target, full, tt=tt, start_token=token, more_weights=more_weights,
                             grads_ready=grads_ready, small_ready=small_ready)

    halves = {}
    for names, started in sent:
        stacks, landed = _exchange_wait(started, scatter=True, after=grad_x, name="scatter_%s_wait" % names[0])
        for n, s, l in zip(names, stacks, landed):
            halves[n] = _sum4(lax.dynamic_index_in_dim(s, q, 0, keepdims=False), l, name="sum4_" + n)
    halves = [halves[n] for n in BIG]
    sib = _swap_sibling(halves, name="swap_halves")
    grad, delta, new_m, new_v = {}, {}, {}, {}
    for n, h, hs in zip(BIG, halves, sib):
        outs = _sum_adam(h, hs, w[n][0], m[n][0], v[n][0], name="adam_" + n)
        grad[n], delta[n], new_m[n], new_v[n] = [o[None] for o in outs]

    small_full_shapes = [full[n].shape for n in small_names]
    red = _allsum_wait(small_sent[0], after=grad_x, name="reduce_small_wait")
    *red, loss = _unpack_rows(red, small_full_shapes + [()])
    red = dict(zip(small_names, red))
    gsm = {}
    for n in REPLICATED:
        gsm[n] = red[n].reshape(w[n].shape)
    for n in SMALL_SHARDED:
        width = w[n].shape[-1]
        gsm[n] = lax.dynamic_slice_in_dim(red[n], q * width, width, axis=red[n].ndim - 1).reshape(w[n].shape)
    shapes = [w[n].shape for n in small_names]
    d_p, m_p, v_p = _adam_rows(_pack_rows([w[n] for n in small_names]), _pack_rows([gsm[n] for n in small_names]),
                               _pack_rows([m[n] for n in small_names]), _pack_rows([v[n] for n in small_names]),
                               name="adam_small")
    for n, dd, mm, vv in zip(small_names, _unpack_rows(d_p, shapes), _unpack_rows(m_p, shapes), _unpack_rows(v_p, shapes)):
        grad[n], delta[n], new_m[n], new_v[n] = gsm[n], dd, mm, vv
    return loss, grad_x, grad, delta, new_m, new_v


def kernel(x, ffn1_w_in, ffn1_w_out, w_in, mu_prev, mu_next, w0, w2, a0, a2, g2, k_k, k_a, r_k, lnx_g, lnx_b, conv_dw, conv_b, conv_ln_g, conv_ln_b, w_out, ffn2_w_in, ffn2_w_out, ln1_g, ln1_b, ln2_g, ln2_b, ln3_g, ln3_b, loss_target, m_ffn1_w_in, m_ffn1_w_out, m_w_in, m_mu_prev, m_mu_next, m_w0, m_w2, m_a0, m_a2, m_g2, m_k_k, m_k_a, m_r_k, m_lnx_g, m_lnx_b, m_conv_dw, m_conv_b, m_conv_ln_g, m_conv_ln_b, m_w_out, m_ffn2_w_in, m_ffn2_w_out, m_ln1_g, m_ln1_b, m_ln2_g, m_ln2_b, m_ln3_g, m_ln3_b, v_ffn1_w_in, v_ffn1_w_out, v_w_in, v_mu_prev, v_mu_next, v_w0, v_w2, v_a0, v_a2, v_g2, v_k_k, v_k_a, v_r_k, v_lnx_g, v_lnx_b, v_conv_dw, v_conv_b, v_conv_ln_g, v_conv_ln_b, v_w_out, v_ffn2_w_in, v_ffn2_w_out, v_ln1_g, v_ln1_b, v_ln2_g, v_ln2_b, v_ln3_g, v_ln3_b):
    args = dict(locals())
    w = {n: args[n] for n in WEIGHTS}
    m = {n: args["m_" + n] for n in WEIGHTS}
    v = {n: args["v_" + n] for n in WEIGHTS}
    seq = x.shape[1]
    loss, grad_x, grad, delta, new_m, new_v = _train_step(x, loss_target, w, m, v, tt=min(256, seq))
    return (loss, grad_x, *[grad[n] for n in WEIGHTS], *[delta[n] for n in WEIGHTS],
            *[new_m[n] for n in WEIGHTS], *[new_v[n] for n in WEIGHTS])
```

```python
import functools

import jax
import jax.numpy as jnp
from jax import lax
from jax.experimental import pallas as pl
from jax.experimental.pallas import tpu as pltpu

F32 = jnp.float32
BF16 = jnp.bfloat16

D_MODEL = 1024
RW = 512
HEAD = 64
CW = 512
CONV_K = 31
CONV_PAD = 15
D_FF = 2816
LORA = 64
GATE_LORA = 160
GATE_PAD = 256
SHIFT_COLS = 1952
SHIFT_PAD = 2048
IN_COLS = 2976
IN_PAD = 3072
LN_EPS = 1e-5
GN_EPS = 64e-5
NORM_EPS = 1e-12
ALPHA = 2.0 ** 0.25
DECAY_SCALE = 0.6065306597126334
ADAM_LR, ADAM_B1, ADAM_B2, ADAM_EPS, ADAM_WD, ADAM_STEP = 0.001, 0.9, 0.999, 1e-08, 0.01, 10
N_CHIPS = 4
VMEM_LIMIT = 56 * 1024 * 1024
TM_FFN = 256
TM_LN = 512

MESH = pl.DeviceIdType.MESH


def _cparams(sem=None, **kw):
    return pltpu.CompilerParams(dimension_semantics=sem, vmem_limit_bytes=VMEM_LIMIT, **kw)


LANES = 128


def _pick_tile(dim, want):
    for t in range(min(want, dim) // LANES * LANES, 0, -LANES):
        if dim % t == 0:
            return t
    return dim


def _after_operand(after):
    return ([], []) if after is None else ([pl.BlockSpec(memory_space=pl.ANY)], [after])


def _matmul(a, b, *, ta=False, tb=False, out_dtype=F32, tm=1024, tn=1024, tk=1024, scale=1.0, col_slabs=False,
            after=None, name):
    after_specs, after_args = _after_operand(after)
    if ta:
        k_dim, m_dim = a.shape
    else:
        m_dim, k_dim = a.shape
    n_dim = b.shape[0] if tb else b.shape[1]
    tm, tn, tk = _pick_tile(m_dim, tm), _pick_tile(n_dim, tn), _pick_tile(k_dim, tk)
    assert m_dim % tm == 0 and n_dim % tn == 0 and k_dim % tk == 0, (name, a.shape, b.shape, tm, tn, tk)
    nk = k_dim // tk
    dims = (((0,) if ta else (1,), (1,) if tb else (0,)), ((), ()))
    if col_slabs:
        out_shape = jax.ShapeDtypeStruct((n_dim // tn, m_dim, tn), out_dtype)
        out_spec = pl.BlockSpec((None, tm, tn), lambda i, j, k: (j, i, 0))
    else:
        out_shape = jax.ShapeDtypeStruct((m_dim, n_dim), out_dtype)
        out_spec = pl.BlockSpec((tm, tn), lambda i, j, k: (i, j))

    def body(a_ref, b_ref, *rest):
        o_ref, acc_ref = rest[-2:]
        kk = pl.program_id(2)

        @pl.when(kk == 0)
        def _():
            acc_ref[...] = jnp.zeros_like(acc_ref)

        acc_ref[...] += lax.dot_general(a_ref[...].astype(BF16), b_ref[...].astype(BF16), dims,
                                        preferred_element_type=F32)

        @pl.when(kk == nk - 1)
        def _():
            o_ref[...] = (acc_ref[...] * scale).astype(o_ref.dtype)

    a_spec = pl.BlockSpec((tk, tm), lambda i, j, k: (k, i)) if ta else pl.BlockSpec((tm, tk), lambda i, j, k: (i, k))
    b_spec = pl.BlockSpec((tn, tk), lambda i, j, k: (j, k)) if tb else pl.BlockSpec((tk, tn), lambda i, j, k: (k, j))
    return pl.pallas_call(
        body, name=name,
        out_shape=out_shape,
        grid=(m_dim // tm, n_dim // tn, nk),
        in_specs=[a_spec, b_spec] + after_specs,
        out_specs=out_spec,
        scratch_shapes=[pltpu.VMEM((tm, tn), F32)],
        compiler_params=_cparams(("parallel", "parallel", "arbitrary")),
    )(a, b, *after_args)


def _whole(shape):
    nd = len(shape)
    return pl.BlockSpec(shape, lambda i: (0,) * nd)


def _ffn_in(x, w, *, tm, after=None, name):
    n_tok = x.shape[0]
    sw = w.shape[2]
    tm = min(tm, n_tok)

    after_specs, after_args = _after_operand(after)

    def body(x_ref, w_ref, *rest):
        h_ref, a_ref = rest[-2:]
        xb = x_ref[...].astype(BF16)
        for s in range(2):
            g = jnp.dot(xb, w_ref[s], preferred_element_type=F32)
            u = jnp.dot(xb, w_ref[s + 2], preferred_element_type=F32)
            h_ref[:, s * sw:(s + 1) * sw] = g.astype(BF16)
            h_ref[:, (s + 2) * sw:(s + 3) * sw] = u.astype(BF16)
            a_ref[:, s * sw:(s + 1) * sw] = (_silu(g) * u).astype(BF16)

    return pl.pallas_call(
        body, name=name,
        out_shape=[jax.ShapeDtypeStruct((n_tok, 2 * D_FF), BF16), jax.ShapeDtypeStruct((n_tok, D_FF), BF16)],
        grid=(n_tok // tm,),
        in_specs=[pl.BlockSpec((tm, D_MODEL), lambda i: (i, 0)), _whole(w.shape)] + after_specs,
        out_specs=[pl.BlockSpec((tm, 2 * D_FF), lambda i: (i, 0)), pl.BlockSpec((tm, D_FF), lambda i: (i, 0))],
        compiler_params=_cparams(("parallel",)),
    )(x, w, *after_args)


def _mm_ln(a_list, w, xres, g, b, fscale, *, tm, name):
    n_tok = xres.shape[0]
    tm = min(tm, n_tok)
    na = len(a_list)

    def body(*refs):
        a_refs = refs[:na]
        w_ref, x_ref, g_ref, b_ref, z_o, y_o, yb_o = refs[na:]
        f, off = None, 0
        for a_ref in a_refs:
            k = a_ref.shape[1]
            t = jnp.dot(a_ref[...].astype(BF16), w_ref[off:off + k, :], preferred_element_type=F32)
            f = t if f is None else f + t
            off += k
        z = ALPHA * x_ref[...] + fscale * f
        y = _layer_norm(z, g_ref[...], b_ref[...])
        z_o[...] = z
        y_o[...] = y
        yb_o[...] = y.astype(BF16)

    tile = pl.BlockSpec((tm, D_MODEL), lambda i: (i, 0))
    return pl.pallas_call(
        body, name=name,
        out_shape=[jax.ShapeDtypeStruct((n_tok, D_MODEL), F32)] * 2 + [jax.ShapeDtypeStruct((n_tok, D_MODEL), BF16)],
        grid=(n_tok // tm,),
        in_specs=[pl.BlockSpec((tm, a.shape[1]), lambda i: (i, 0)) for a in a_list]
        + [_whole(w.shape), tile, _whole(g.shape), _whole(b.shape)],
        out_specs=[tile, tile, tile],
        compiler_params=_cparams(("parallel",)),
    )(*a_list, w, xres, g, b)


def _mm_ln_loss(a, w, xres, g, b, target, fscale, *, tm, name):
    n_tok = xres.shape[0]
    tm = min(tm, n_tok)

    def body(a_ref, w_ref, x_ref, g_ref, b_ref, t_ref, dz_o, dg_o, db_o, loss_o):
        i = pl.program_id(0)
        z = ALPHA * x_ref[...] + fscale * jnp.dot(a_ref[...].astype(BF16), w_ref[...], preferred_element_type=F32)
        y, vjp = jax.vjp(_layer_norm, z, g_ref[...], b_ref[...])
        e = y - t_ref[...]
        dz, dg, db = vjp(e * (1.0 / D_MODEL))

        @pl.when(i == 0)
        def _():
            dg_o[...] = jnp.zeros_like(dg_o)
            db_o[...] = jnp.zeros_like(db_o)
            loss_o[...] = jnp.zeros_like(loss_o)
        dz_o[...] = dz
        dg_o[...] += dg
        db_o[...] += db
        loss_o[...] += 0.5 * jnp.sum(jnp.mean(e * e, axis=-1, keepdims=True), axis=0, keepdims=True)

    tile = pl.BlockSpec((tm, D_MODEL), lambda i: (i, 0))
    row = pl.BlockSpec((1, D_MODEL), lambda i: (0, 0))
    return pl.pallas_call(
        body, name=name,
        out_shape=[jax.ShapeDtypeStruct((n_tok, D_MODEL), F32), jax.ShapeDtypeStruct((1, D_MODEL), F32),
                   jax.ShapeDtypeStruct((1, D_MODEL), F32), jax.ShapeDtypeStruct((8, LANES), F32)],
        grid=(n_tok // tm,),
        in_specs=[pl.BlockSpec((tm, a.shape[1]), lambda i: (i, 0)), _whole(w.shape), tile, row, row, tile],
        out_specs=[tile, row, row, pl.BlockSpec((8, LANES), lambda i: (0, 0))],
        compiler_params=_cparams(("arbitrary",)),
    )(a, w, xres, g, b, target)


def _ffn_out_bwd(dz, w, h, *, tm, after=None, name):
    n_tok = dz.shape[0]
    tm = min(tm, n_tok)
    cw = D_FF // 2
    after_specs, after_args = _after_operand(after)

    def body(dz_ref, w_ref, h_ref, *rest):
        dh_ref = rest[-1]
        dzb = dz_ref[...].astype(BF16)
        for s in range(2):
            dact = 0.5 * lax.dot_general(dzb, w_ref[s * cw:(s + 1) * cw, :], (((1,), (1,)), ((), ())),
                                         preferred_element_type=F32)
            gate = h_ref[:, s * cw:(s + 1) * cw].astype(F32)
            up = h_ref[:, D_FF + s * cw:D_FF + (s + 1) * cw].astype(F32)
            sg = _sigmoid(gate)
            dh_ref[:, s * cw:(s + 1) * cw] = (dact * up * sg * (1.0 + gate * (1.0 - sg))).astype(BF16)
            dh_ref[:, D_FF + s * cw:D_FF + (s + 1) * cw] = (dact * gate * sg).astype(BF16)

    wide = pl.BlockSpec((tm, 2 * D_FF), lambda i: (i, 0))
    return pl.pallas_call(
        body, name=name,
        out_shape=jax.ShapeDtypeStruct((n_tok, 2 * D_FF), BF16),
        grid=(n_tok // tm,),
        in_specs=[pl.BlockSpec((tm, D_MODEL), lambda i: (i, 0)), _whole(w.shape), wide] + after_specs,
        out_specs=wide,
        compiler_params=_cparams(("parallel",)),
    )(dz, w, h, *after_args)


def _mm_nt_res(a_list, w, dz, *, tm, ln=None, after=None, name):
    n_tok = dz.shape[0]
    tm = min(tm, n_tok)
    na = len(a_list)
    nt = (((1,), (1,)), ((), ()))
    after_specs, after_args = _after_operand(after)
    n_out = 1 if ln is None else 3

    def body(*refs):
        a_refs = refs[:na]
        w_ref, dz_ref, o_ref = refs[na], refs[na + 1], refs[-n_out]
        acc = ALPHA * dz_ref[...]
        if len(w_ref.shape) == 3:
            cw = w_ref.shape[2]
            for s in range(w_ref.shape[0]):
                acc = acc + lax.dot_general(a_refs[0][:, s * cw:(s + 1) * cw], w_ref[s], nt, preferred_element_type=F32)
        else:
            off = 0
            for a_ref in a_refs:
                k = a_ref.shape[1]
                acc = acc + lax.dot_general(a_ref[...], w_ref[:, off:off + k], nt, preferred_element_type=F32)
                off += k
        if ln is None:
            o_ref[...] = acc
            return
        z_ref, g_ref, b_ref = refs[na + 2:na + 5]
        dg_o, db_o = refs[-2:]
        _, vjp = jax.vjp(_layer_norm, z_ref[...], g_ref[...], b_ref[...])
        o_ref[...], dg, db = vjp(acc)

        @pl.when(pl.program_id(0) == 0)
        def _():
            dg_o[...] = jnp.zeros_like(dg_o)
            db_o[...] = jnp.zeros_like(db_o)
        dg_o[...] += dg
        db_o[...] += db

    tile = pl.BlockSpec((tm, D_MODEL), lambda i: (i, 0))
    row = pl.BlockSpec((1, D_MODEL), lambda i: (0, 0))
    out_shape = [jax.ShapeDtypeStruct((n_tok, D_MODEL), F32)]
    ln_specs, ln_args, out_specs = [], [], [tile]
    if ln is not None:
        ln_specs, ln_args = [tile, row, row], list(ln)
        out_shape += [jax.ShapeDtypeStruct((1, D_MODEL), F32)] * 2
        out_specs += [row, row]
    outs = pl.pallas_call(
        body, name=name,
        out_shape=out_shape,
        grid=(n_tok // tm,),
        in_specs=[pl.BlockSpec((tm, a.shape[1]), lambda i: (i, 0)) for a in a_list] + [_whole(w.shape), tile]
        + ln_specs + after_specs,
        out_specs=out_specs,
        compiler_params=_cparams(("parallel",) if ln is None else ("arbitrary",)),
    )(*a_list, w, dz, *ln_args, *after_args)
    return outs[0] if ln is None else outs


def _rowcall(fn, tok_in, full_in, tok_out, acc_out, *, tt, name):
    views = [a if isinstance(a, tuple) else (a, a.shape[1], 0) for a in tok_in]
    tok_in = [a for a, _, _ in views]
    n_tok = tok_in[0].shape[0]
    assert n_tok % tt == 0, (name, n_tok, tt)
    n_ti, n_fi, n_to = len(tok_in), len(full_in), len(tok_out)

    def body(*refs):
        i = pl.program_id(0)
        ins = [r[...] for r in refs[:n_ti + n_fi]]
        outs = fn(i, *ins)
        o_refs = refs[n_ti + n_fi:]
        for r, val in zip(o_refs[:n_to], outs[:n_to]):
            r[...] = val.astype(r.dtype)
        if acc_out:
            @pl.when(i == 0)
            def _():
                for r in o_refs[n_to:]:
                    r[...] = jnp.zeros_like(r)
            for r, val in zip(o_refs[n_to:], outs[n_to:]):
                r[...] += val.reshape(r.shape).astype(F32)

    in_specs = [pl.BlockSpec((tt, width), functools.partial(lambda k, i: (i, k), k)) for _, width, k in views]
    in_specs += [pl.BlockSpec(a.shape, lambda i: (0, 0)) for a in full_in]
    out_specs = [pl.BlockSpec((tt, c), lambda i: (i, 0)) for c, _ in tok_out]
    out_specs += [pl.BlockSpec(s, lambda i: (0, 0)) for s in acc_out]
    out_shape = [jax.ShapeDtypeStruct((n_tok, c), dt) for c, dt in tok_out]
    out_shape += [jax.ShapeDtypeStruct(s, F32) for s in acc_out]
    return pl.pallas_call(
        body, name=name, out_shape=out_shape, grid=(n_tok // tt,), in_specs=in_specs, out_specs=out_specs,
        compiler_params=_cparams(("arbitrary",) if acc_out else ("parallel",)),
    )(*tok_in, *full_in)


@jax.custom_vjp
def _bdot(a, b):
    return jnp.dot(a.astype(BF16), b.astype(BF16), preferred_element_type=F32)


def _bdot_fwd(a, b):
    return _bdot(a, b), (a, b)


def _bdot_bwd(res, g):
    a, b = res
    g16 = g.astype(BF16)
    da = lax.dot_general(g16, b.astype(BF16), (((1,), (1,)), ((), ())), preferred_element_type=F32)
    db = lax.dot_general(a.astype(BF16), g16, (((0,), (0,)), ((), ())), preferred_element_type=F32)
    return da, db


_bdot.defvjp(_bdot_fwd, _bdot_bwd)


def _split16(x):
    hi = x.astype(BF16)
    lo = (x - hi.astype(F32)).astype(BF16)
    return hi, lo


def _segsum_raw(x, e2):
    hi, lo = _split16(x)
    outs = []
    for c in range(x.shape[1] // 256):
        lhs = jnp.concatenate([hi[:, 256 * c:256 * (c + 1)], lo[:, 256 * c:256 * (c + 1)]], axis=1)
        outs.append(jnp.dot(lhs, e2, preferred_element_type=F32))
    return jnp.concatenate(outs, axis=1)


@jax.custom_vjp
def _segsum(x, e2):
    return _segsum_raw(x, e2)


def _segsum_fwd(x, e2):
    return _segsum_raw(x, e2), e2


def _segsum_bwd(e2, g):
    return _segsum_raw(g, e2), jnp.zeros_like(e2)


_segsum.defvjp(_segsum_fwd, _segsum_bwd)


def _head_ones():
    r = lax.broadcasted_iota(jnp.int32, (512, 256), 0) % 256
    c = lax.broadcasted_iota(jnp.int32, (512, 256), 1)
    return (r // HEAD == c // HEAD).astype(BF16)


def _sigmoid(x):
    return 1.0 / (1.0 + jnp.exp(-x))


def _silu(x):
    return x * _sigmoid(x)


def _layer_norm(z, g, b, eps=LN_EPS):
    mu = jnp.mean(z, axis=-1, keepdims=True)
    zc = z - mu
    var = jnp.mean(zc * zc, axis=-1, keepdims=True)
    return zc * lax.rsqrt(var + eps) * g + b


def _prep(ps, w2b, w0c, a2b, a0c, g2p, k_k, k_a, e2):
    r, k, v = ps[:, 0:512], ps[:, 512:1024], ps[:, 1024:1536]
    wd, ad, gd = ps[:, 1536:1664], ps[:, 1664:1792], ps[:, 1792:2048]
    lw = _bdot(jnp.tanh(wd), w2b) + w0c
    decay = -DECAY_SCALE * _sigmoid(lw)
    a = _sigmoid(_bdot(ad, a2b) + a0c)
    g = _bdot(_sigmoid(gd), g2p)
    kkr = k * k_k
    nrm = jnp.sqrt(_segsum(kkr * kkr, e2))
    kk = kkr / jnp.maximum(nrm, NORM_EPS)
    k2 = jnp.concatenate([k, k], axis=1)
    ka2 = jnp.concatenate([k_a, k_a], axis=1)
    kd = k2 * (1.0 + (a - 1.0) * ka2)
    b = jnp.concatenate([kk, kk], axis=1) * a
    return r, v, kk, decay, kd, b, g


def _post(y0, y1, r, v, kd, g, lnx_g, lnx_b, r_k, e2):
    y = y0 + y1
    mu = _segsum(y, e2) * (1.0 / HEAD)
    yc = y - mu
    var = _segsum(yc * yc, e2) * (1.0 / HEAD)
    yn = yc * lax.rsqrt(var + GN_EPS) * lnx_g + lnx_b
    bonus = _segsum(r * (kd[:, :RW] + kd[:, RW:]) * r_k, e2)
    return (yn + bonus * v) * g


def _conv_post(yc, ln_g, ln_b):
    return _silu(_layer_norm(yc, ln_g, ln_b))


def _halo_specs(cols_block, hb, tt, n_tok, col_idx):
    nb = n_tok // hb
    prev = pl.BlockSpec((hb, cols_block), lambda i: (jnp.maximum(i * (tt // hb) - 1, 0), col_idx))
    nxt = pl.BlockSpec((hb, cols_block), lambda i: (jnp.minimum((i + 1) * (tt // hb), nb - 1), col_idx))
    return prev, nxt


def _mix_prep(p, mu_p, mu_n, w2b, w0c, a2b, a0c, g2p, k_k, k_a, *, seq, tt, name):
    n_tok = p.shape[0]
    tps = seq // tt
    e2 = _head_ones()

    def body(p_ref, hp_ref, hn_ref, mup_ref, mun_ref, w2b_ref, w0c_ref, a2b_ref, a0c_ref, g2p_ref, kk_ref, ka_ref,
             e2_ref, r_o, v_o, kk_o, w_o, kd_o, b_o, g_o, ext):
        i = pl.program_id(0)
        first = (i % tps) == 0
        last = (i % tps) == tps - 1
        pv = p_ref[...]
        ext[pl.ds(0, 8), :] = jnp.where(first, 0.0, hp_ref[...])
        ext[pl.ds(8, tt), :] = pv
        ext[pl.ds(8 + tt, 8), :] = jnp.where(last, 0.0, hn_ref[...])
        prev = ext[pl.ds(7, tt), :]
        nxt = ext[pl.ds(9, tt), :]
        ps = pv + mup_ref[...] * (prev - pv) + mun_ref[...] * (nxt - pv)
        outs = _prep(ps, w2b_ref[...], w0c_ref[...], a2b_ref[...], a0c_ref[...], g2p_ref[...], kk_ref[...],
                     ka_ref[...], e2_ref[...])
        for o_ref, val in zip((r_o, v_o, kk_o, w_o, kd_o, b_o, g_o), outs):
            o_ref[...] = val

    hp, hn = _halo_specs(SHIFT_PAD, 8, tt, n_tok, 0)
    fulls = [mu_p, mu_n, w2b, w0c, a2b, a0c, g2p, k_k, k_a, e2]
    widths = (RW, RW, RW, 2 * RW, 2 * RW, 2 * RW, RW)
    return pl.pallas_call(
        body, name=name,
        out_shape=[jax.ShapeDtypeStruct((n_tok, c), F32) for c in widths],
        grid=(n_tok // tt,),
        in_specs=[pl.BlockSpec((tt, SHIFT_PAD), lambda i: (i, 0)), hp, hn]
        + [pl.BlockSpec(a.shape, lambda i: (0, 0)) for a in fulls],
        out_specs=[pl.BlockSpec((tt, c), lambda i: (i, 0)) for c in widths],
        scratch_shapes=[pltpu.VMEM((tt + 16, SHIFT_PAD), F32)],
        compiler_params=_cparams(("parallel",)),
    )(p, p, p, *fulls)


def _mix_prep_bwd(p, mu_p, mu_n, w2b, w0c, a2b, a0c, g2p, k_k, k_a, ct_terms, *, seq, tt, name):
    n_tok = p.shape[0]
    tps = seq // tt
    e2 = _head_ones()
    acc_shapes = [w2b.shape, w0c.shape, a2b.shape, a0c.shape, g2p.shape, k_k.shape, k_a.shape]
    cts = [a for terms in ct_terms for t in terms for a in (t if isinstance(t, tuple) else (t,))]

    def body(p_ref, hp_ref, hn_ref, mup_ref, mun_ref, w2b_ref, w0c_ref, a2b_ref, a0c_ref, g2p_ref, kk_ref, ka_ref,
             e2_ref, *rest):
        ct_refs, dps_o, acc_refs, ext = rest[:len(cts)], rest[len(cts)], rest[len(cts) + 1:-1], rest[-1]
        ct_it = iter(ct_refs)
        ct_vals = []
        for terms in ct_terms:
            total = None
            for t in terms:
                if isinstance(t, tuple):
                    val = jnp.concatenate([next(ct_it)[...] for _ in t], axis=1)
                else:
                    val = next(ct_it)[...]
                total = val if total is None else total + val
            ct_vals.append(total)
        i = pl.program_id(0)
        first = (i % tps) == 0
        last = (i % tps) == tps - 1
        pv = p_ref[...]
        ext[pl.ds(0, 8), :] = jnp.where(first, 0.0, hp_ref[...])
        ext[pl.ds(8, tt), :] = pv
        ext[pl.ds(8 + tt, 8), :] = jnp.where(last, 0.0, hn_ref[...])
        prev = ext[pl.ds(7, tt), :]
        nxt = ext[pl.ds(9, tt), :]
        ps = pv + mup_ref[...] * (prev - pv) + mun_ref[...] * (nxt - pv)
        e2v = e2_ref[...]
        _, vjp = jax.vjp(lambda *a: _prep(*a, e2v), ps, w2b_ref[...], w0c_ref[...], a2b_ref[...], a0c_ref[...],
                         g2p_ref[...], kk_ref[...], ka_ref[...])
        grads = vjp(tuple(ct_vals))
        dps_o[...] = grads[0]

        @pl.when(i == 0)
        def _():
            for r in acc_refs:
                r[...] = jnp.zeros_like(r)
        for r, val in zip(acc_refs, grads[1:]):
            r[...] += val

    hp, hn = _halo_specs(SHIFT_PAD, 8, tt, n_tok, 0)
    fulls = [mu_p, mu_n, w2b, w0c, a2b, a0c, g2p, k_k, k_a, e2]
    return pl.pallas_call(
        body, name=name,
        out_shape=[jax.ShapeDtypeStruct((n_tok, SHIFT_PAD), F32)] + [jax.ShapeDtypeStruct(s, F32) for s in acc_shapes],
        grid=(n_tok // tt,),
        in_specs=[pl.BlockSpec((tt, SHIFT_PAD), lambda i: (i, 0)), hp, hn]
        + [pl.BlockSpec(a.shape, lambda i: (0, 0)) for a in fulls]
        + [pl.BlockSpec((tt, c.shape[1]), lambda i: (i, 0)) for c in cts],
        out_specs=[pl.BlockSpec((tt, SHIFT_PAD), lambda i: (i, 0))] + [pl.BlockSpec(s, lambda i: (0, 0)) for s in acc_shapes],
        scratch_shapes=[pltpu.VMEM((tt + 16, SHIFT_PAD), F32)],
        compiler_params=_cparams(("arbitrary",)),
    )(p, p, p, *fulls, *cts)


def _shift_bwd(dps, p, mu_p, mu_n, *, seq, tt, name):
    n_tok = p.shape[0]
    tps = seq // tt

    def body(d_ref, dhp_ref, dhn_ref, p_ref, php_ref, phn_ref, mup_ref, mun_ref, dp_o, dmup_o, dmun_o, ext):
        i = pl.program_id(0)
        first = (i % tps) == 0
        last = (i % tps) == tps - 1
        mup, mun = mup_ref[...], mun_ref[...]
        dv = d_ref[...]
        pv = p_ref[...]
        ext[pl.ds(0, 8), :] = jnp.where(first, 0.0, dhp_ref[...])
        ext[pl.ds(8, tt), :] = dv
        ext[pl.ds(8 + tt, 8), :] = jnp.where(last, 0.0, dhn_ref[...])
        d_prev = ext[pl.ds(7, tt), :]
        d_next = ext[pl.ds(9, tt), :]
        dp_o[...] = (dv * (1.0 - mup - mun) + d_next * mup + d_prev * mun).astype(dp_o.dtype)
        ext[pl.ds(0, 8), :] = jnp.where(first, 0.0, php_ref[...])
        ext[pl.ds(8, tt), :] = pv
        ext[pl.ds(8 + tt, 8), :] = jnp.where(last, 0.0, phn_ref[...])
        p_prev = ext[pl.ds(7, tt), :]
        p_next = ext[pl.ds(9, tt), :]

        @pl.when(i == 0)
        def _():
            dmup_o[...] = jnp.zeros_like(dmup_o)
            dmun_o[...] = jnp.zeros_like(dmun_o)
        dmup_o[...] += jnp.sum(dv * (p_prev - pv), axis=0, keepdims=True)
        dmun_o[...] += jnp.sum(dv * (p_next - pv), axis=0, keepdims=True)

    hp, hn = _halo_specs(SHIFT_PAD, 8, tt, n_tok, 0)
    tile = pl.BlockSpec((tt, SHIFT_PAD), lambda i: (i, 0))
    full = pl.BlockSpec((1, SHIFT_PAD), lambda i: (0, 0))
    return pl.pallas_call(
        body, name=name,
        out_shape=[jax.ShapeDtypeStruct((n_tok, SHIFT_PAD), BF16), jax.ShapeDtypeStruct((1, SHIFT_PAD), F32),
                   jax.ShapeDtypeStruct((1, SHIFT_PAD), F32)],
        grid=(n_tok // tt,),
        in_specs=[tile, hp, hn, tile, hp, hn, full, full],
        out_specs=[tile, full, full],
        scratch_shapes=[pltpu.VMEM((tt + 16, SHIFT_PAD), F32)],
        compiler_params=_cparams(("arbitrary",)),
    )(dps, dps, dps, p, p, p, mu_p, mu_n)


def _mix_post(y0, y1, r, v, kd, g, lnx_g, lnx_b, r_k, *, tt, name):
    e2 = _head_ones()
    return _rowcall(lambda i, *a: (_post(*a),), [y0, y1, r, v, kd, g], [lnx_g, lnx_b, r_k, e2], [(RW, BF16)], [],
                    tt=tt, name=name)[0]


def _mix_post_bwd(y0, y1, r, v, kd, g, lnx_g, lnx_b, r_k, dout, *, tt, name):
    e2 = _head_ones()

    def fn(i, y0v, y1v, rv, vv, kdv, gv, dov, lg, lb, rk, e2v):
        _, vjp = jax.vjp(lambda *a: _post(*a, e2v), y0v, y1v, rv, vv, kdv, gv, lg, lb, rk)
        gr = vjp(dov.astype(F32))
        return gr[0], gr[2], gr[3], gr[4], gr[5], gr[6], gr[7], gr[8]
    return _rowcall(fn, [y0, y1, r, v, kd, g, dout], [lnx_g, lnx_b, r_k, e2],
                    [(RW, F32), (RW, F32), (RW, F32), (2 * RW, F32), (RW, F32)], [(1, RW), (1, RW), (1, RW)],
                    tt=tt, name=name)


def _conv_fwd(p, dw, db, ln_g, ln_b, *, seq, tt, name):
    n_tok = p.shape[0]
    tps = seq // tt

    def glu(x, gate):
        return x * _sigmoid(gate)

    def body(u_ref, g_ref, uhp, ghp, uhn, ghn, dw_ref, db_ref, lg_ref, lb_ref, yc_o, y_o, ext):
        i = pl.program_id(0)
        first = (i % tps) == 0
        last = (i % tps) == tps - 1
        ext[pl.ds(0, 16), :] = jnp.where(first, 0.0, glu(uhp[...], ghp[...]))
        ext[pl.ds(16, tt), :] = glu(u_ref[...], g_ref[...])
        ext[pl.ds(16 + tt, 16), :] = jnp.where(last, 0.0, glu(uhn[...], ghn[...]))
        acc = jnp.zeros((tt, CW), F32) + db_ref[...]
        for k in range(CONV_K):
            acc = acc + ext[pl.ds(k + 1, tt), :] * dw_ref[pl.ds(k, 1), :]
        yc_o[...] = acc
        y_o[...] = _conv_post(acc, lg_ref[...], lb_ref[...]).astype(y_o.dtype)

    uhp_s, uhn_s = _halo_specs(CW, 16, tt, n_tok, 4)
    ghp_s, ghn_s = _halo_specs(CW, 16, tt, n_tok, 5)
    fulls = [dw, db, ln_g, ln_b]
    return pl.pallas_call(
        body, name=name,
        out_shape=[jax.ShapeDtypeStruct((n_tok, CW), F32), jax.ShapeDtypeStruct((n_tok, CW), BF16)],
        grid=(n_tok // tt,),
        in_specs=[pl.BlockSpec((tt, CW), lambda i: (i, 4)), pl.BlockSpec((tt, CW), lambda i: (i, 5)),
                  uhp_s, ghp_s, uhn_s, ghn_s] + [pl.BlockSpec(a.shape, lambda i: (0, 0)) for a in fulls],
        out_specs=[pl.BlockSpec((tt, CW), lambda i: (i, 0)), pl.BlockSpec((tt, CW), lambda i: (i, 0))],
        scratch_shapes=[pltpu.VMEM((tt + 32, CW), F32)],
        compiler_params=_cparams(("parallel",)),
    )(p, p, p, p, p, p, *fulls)


def _conv_post_bwd(yc, dy, ln_g, ln_b, *, tt, name):
    def fn(i, ycv, dyv, lg, lb):
        _, vjp = jax.vjp(_conv_post, ycv, lg, lb)
        dyc, dg, dbb = vjp(dyv.astype(F32))
        return dyc, dg, dbb, jnp.sum(dyc, axis=0, keepdims=True)
    return _rowcall(fn, [yc, dy], [ln_g, ln_b], [(CW, F32)], [(1, CW), (1, CW), (1, CW)], tt=tt, name=name)


def _conv_bwd(dyc, p, dw, *, seq, tt, name):
    n_tok = p.shape[0]
    tps = seq // tt

    def body(d_ref, dhp, dhn, u_ref, g_ref, uhp, ghp, uhn, ghn, dw_ref, dp_o, ddw_o, ext):
        i = pl.program_id(0)
        first = (i % tps) == 0
        last = (i % tps) == tps - 1
        dv = d_ref[...]
        ext[pl.ds(0, 16), :] = jnp.where(first, 0.0, dhp[...])
        ext[pl.ds(16, tt), :] = dv
        ext[pl.ds(16 + tt, 16), :] = jnp.where(last, 0.0, dhn[...])
        du = jnp.zeros((tt, CW), F32)
        for k in range(CONV_K):
            du = du + ext[pl.ds(31 - k, tt), :] * dw_ref[pl.ds(k, 1), :]
        uv, gv = u_ref[...], g_ref[...]
        sg = _sigmoid(gv)
        dp_o[:, 0:CW] = (du * sg).astype(dp_o.dtype)
        dp_o[:, CW:2 * CW] = (du * uv * sg * (1.0 - sg)).astype(dp_o.dtype)
        ext[pl.ds(0, 16), :] = jnp.where(first, 0.0, uhp[...] * _sigmoid(ghp[...]))
        ext[pl.ds(16, tt), :] = uv * sg
        ext[pl.ds(16 + tt, 16), :] = jnp.where(last, 0.0, uhn[...] * _sigmoid(ghn[...]))

        @pl.when(i == 0)
        def _():
            ddw_o[...] = jnp.zeros_like(ddw_o)
        for k in range(CONV_K):
            ddw_o[pl.ds(k, 1), :] += jnp.sum(dv * ext[pl.ds(k + 1, tt), :], axis=0, keepdims=True)

    dhp_s, dhn_s = _halo_specs(CW, 16, tt, n_tok, 0)
    uhp_s, uhn_s = _halo_specs(CW, 16, tt, n_tok, 4)
    ghp_s, ghn_s = _halo_specs(CW, 16, tt, n_tok, 5)
    return pl.pallas_call(
        body, name=name,
        out_shape=[jax.ShapeDtypeStruct((n_tok, 2 * CW), BF16), jax.ShapeDtypeStruct((32, CW), F32)],
        grid=(n_tok // tt,),
        in_specs=[pl.BlockSpec((tt, CW), lambda i: (i, 0)), dhp_s, dhn_s,
                  pl.BlockSpec((tt, CW), lambda i: (i, 4)), pl.BlockSpec((tt, CW), lambda i: (i, 5)),
                  uhp_s, ghp_s, uhn_s, ghn_s, pl.BlockSpec(dw.shape, lambda i: (0, 0))],
        out_specs=[pl.BlockSpec((tt, 2 * CW), lambda i: (i, 0)), pl.BlockSpec((32, CW), lambda i: (0, 0))],
        scratch_shapes=[pltpu.VMEM((tt + 32, CW), F32)],
        compiler_params=_cparams(("arbitrary",)),
    )(dyc, dyc, dyc, p, p, p, p, p, p, dw)


CHUNK = 64
_MM_DIMS = {"nn": (((2,), (1,)), ((0,), (0,))), "nt": (((2,), (2,)), ((0,), (0,))), "tn": (((1,), (1,)), ((0,), (0,)))}


def _mm16_raw(a, b, mode, fine):
    dot = lambda x, y: lax.dot_general(x, y, _MM_DIMS[mode], preferred_element_type=F32)
    if not fine:
        return dot(a.astype(BF16), b.astype(BF16))
    ah, (bh, bl) = a.astype(BF16), _split16(b)
    return dot(ah, bh) + dot(ah, bl)


@functools.partial(jax.custom_vjp, nondiff_argnums=(2, 3))
def _mm16(a, b, mode, fine=False):
    return _mm16_raw(a, b, mode, fine)


def _mm16_fwd(a, b, mode, fine):
    return _mm16_raw(a, b, mode, fine), (a, b)


def _mm16_bwd(mode, fine, res, g):
    a, b = res
    if mode == "nn":
        return _mm16_raw(g, b, "nt", fine), _mm16_raw(a, g, "tn", fine)
    if mode == "nt":
        return _mm16_raw(g, b, "nn", fine), _mm16_raw(g, a, "tn", fine)
    return _mm16_raw(b, g, "nt", fine), _mm16_raw(a, g, "nn", fine)


_mm16.defvjp(_mm16_fwd, _mm16_bwd)


def _tri_sum_raw(x, tri, mode):
    hi = x.astype(BF16)
    r1 = x - hi.astype(F32)
    mid = r1.astype(BF16)
    lo = (r1 - mid.astype(F32)).astype(BF16)
    dot = lambda p: lax.dot_general(tri, p, _MM_DIMS[mode], preferred_element_type=F32)
    return dot(hi) + dot(mid) + dot(lo)


@jax.custom_vjp
def _tri_sum(x, tri):
    return _tri_sum_raw(x, tri, "nn")


def _tri_sum_fwd(x, tri):
    return _tri_sum_raw(x, tri, "nn"), tri


def _tri_sum_bwd(tri, g):
    return _tri_sum_raw(g, tri, "tn"), jnp.zeros_like(tri)


_tri_sum.defvjp(_tri_sum_fwd, _tri_sum_bwd)


PAIR = 2 * HEAD
N_PAIRS = RW // PAIR


def _pair_rows(x):
    first = lax.broadcasted_iota(jnp.int32, x.shape, 2) < HEAD
    return jnp.concatenate([jnp.where(first, x, 0.0), jnp.where(first, 0.0, x)], axis=1)


def _chunk_step_pairs(s0, r, lw, k, v, kk, b, tri, rev):
    g, n, _ = r.shape
    row = lax.broadcasted_iota(jnp.int32, (g, n, PAIR), 1)
    col = lax.broadcasted_iota(jnp.int32, (g, n, PAIR), 2) % HEAD
    if rev:
        row, col = col, row
    diag = (lax.broadcasted_iota(jnp.int32, (g, PAIR, PAIR), 1) // HEAD
            == lax.broadcasted_iota(jnp.int32, (g, PAIR, PAIR), 2) // HEAD)
    cum = _tri_sum(lw, tri)
    up, down = jnp.exp(cum), jnp.exp(-cum)
    at, rt = -kk * jnp.exp(cum - lw), r * up
    kt, bt = k * down, b * down
    bt_rows, kt_rows = _pair_rows(bt), _pair_rows(kt)
    a_ab = jnp.where(col < row, _mm16(at, bt_rows, "nt"), 0.0)
    a_ak = jnp.where(col < row, _mm16(at, kt_rows, "nt"), 0.0)
    a_rb = jnp.where(col <= row, _mm16(rt, bt_rows, "nt"), 0.0)
    a_rk = jnp.where(col <= row, _mm16(rt, kt_rows, "nt", True), 0.0)
    v_rows = _pair_rows(v)
    u = _mm16(at, s0, "nt") + _mm16(a_ak, v_rows, "nn")
    power = a_ab
    steps = n.bit_length() - 1
    for it in range(steps):
        u = u + _mm16(power, _pair_rows(u), "nn")
        if it + 1 < steps:
            power = _mm16(power, _pair_rows(power), "nn")
    y = _mm16(rt, s0, "nt") + _mm16(a_rk, v_rows, "nn") + _mm16(a_rb, _pair_rows(u), "nn")
    grown = s0 + jnp.where(diag, _mm16(v, kt, "tn") + _mm16(u, bt, "tn"), 0.0)
    return y, grown * jnp.exp(jnp.sum(lw, axis=1, keepdims=True))


SCAN_SEQS = 4


def _tri_ones(rev, nseq):
    shape = (nseq * N_PAIRS, CHUNK, CHUNK)
    row, col = lax.broadcasted_iota(jnp.int32, shape, 1), lax.broadcasted_iota(jnp.int32, shape, 2)
    return ((col >= row) if rev else (col <= row)).astype(BF16)


def _split_heads(ref):
    return jnp.stack([ref[q, :, pl.ds(h * PAIR, PAIR)] for q in range(ref.shape[0]) for h in range(N_PAIRS)])


def _merge_heads(ref, val):
    for q in range(ref.shape[0]):
        for h in range(N_PAIRS):
            ref[q, :, pl.ds(h * PAIR, PAIR)] = val[q * N_PAIRS + h]


def _chunk_specs(nseq, nc, rev, dcol):
    chunk = (lambda c: nc - 1 - c) if rev else (lambda c: c)
    shared = pl.BlockSpec((nseq, CHUNK, RW), lambda s, c: (s, chunk(c), 0))
    own = pl.BlockSpec((nseq, CHUNK, RW), lambda s, c: (s, chunk(c), dcol))
    return shared, own


def _wkv_chunk_fwd(r, lw, k, v, kk, b, *, rev, name):
    bsz, seq, _ = r.shape
    nc = seq // CHUNK
    nseq = SCAN_SEQS if bsz % SCAN_SEQS == 0 else 1
    shared, own = _chunk_specs(nseq, nc, rev, int(rev))

    def body(r_ref, lw_ref, k_ref, v_ref, kk_ref, b_ref, tri_ref, y_o, s0_o, s_ref):
        @pl.when(pl.program_id(1) == 0)
        def _():
            s_ref[...] = jnp.zeros_like(s_ref)

        s0 = s_ref[...]
        s0_o[:, 0] = s0.reshape(nseq, N_PAIRS, PAIR, PAIR)
        y, s_ref[...] = _chunk_step_pairs(s0, *[_split_heads(x) for x in (r_ref, lw_ref, k_ref, v_ref, kk_ref, b_ref)],
                                    tri_ref[...], rev)
        _merge_heads(y_o, y)

    return pl.pallas_call(
        body, name=name,
        out_shape=[jax.ShapeDtypeStruct((bsz, seq, RW), F32), jax.ShapeDtypeStruct((bsz, nc, N_PAIRS, PAIR, PAIR), F32)],
        grid=(bsz // nseq, nc),
        in_specs=[shared, own, own, shared, shared, own,
                  pl.BlockSpec((nseq * N_PAIRS, CHUNK, CHUNK), lambda s, c: (0, 0, 0))],
        out_specs=[shared, pl.BlockSpec((nseq, 1, N_PAIRS, PAIR, PAIR), lambda s, c: (s, c, 0, 0, 0))],
        scratch_shapes=[pltpu.VMEM((nseq * N_PAIRS, PAIR, PAIR), F32)],
        compiler_params=_cparams(("parallel", "arbitrary")),
    )(r, lw, k, v, kk, b, _tri_ones(rev, nseq))


def _wkv_chunk_bwd(r, lw, k, v, kk, b, dy, s0, *, rev, name):
    bsz, seq, _ = r.shape
    nc = seq // CHUNK
    nseq = SCAN_SEQS if bsz % SCAN_SEQS == 0 else 1
    shared, own = _chunk_specs(nseq, nc, not rev, int(rev))

    def body(r_ref, lw_ref, k_ref, v_ref, kk_ref, b_ref, dy_ref, s0_ref, tri_ref, *rest):
        outs, ds_ref = rest[:-1], rest[-1]

        @pl.when(pl.program_id(1) == 0)
        def _():
            ds_ref[...] = jnp.zeros_like(ds_ref)

        triv = tri_ref[...]
        _, vjp = jax.vjp(lambda *a: _chunk_step_pairs(*a, triv, rev), s0_ref[:, 0].reshape(nseq * N_PAIRS, PAIR, PAIR),
                         *[_split_heads(x) for x in (r_ref, lw_ref, k_ref, v_ref, kk_ref, b_ref)])
        grads = vjp((_split_heads(dy_ref), ds_ref[...]))
        ds_ref[...] = grads[0]
        for o, gval in zip(outs, grads[1:]):
            _merge_heads(o, gval)

    return pl.pallas_call(
        body, name=name,
        out_shape=[jax.ShapeDtypeStruct((bsz, seq, RW), F32)] * 6,
        grid=(bsz // nseq, nc),
        in_specs=[shared, own, own, shared, shared, own, shared,
                  pl.BlockSpec((nseq, 1, N_PAIRS, PAIR, PAIR), lambda s, c: (s, nc - 1 - c, 0, 0, 0)),
                  pl.BlockSpec((nseq * N_PAIRS, CHUNK, CHUNK), lambda s, c: (0, 0, 0))],
        out_specs=[shared] * 6,
        scratch_shapes=[pltpu.VMEM((nseq * N_PAIRS, PAIR, PAIR), F32)],
        compiler_params=_cparams(("parallel", "arbitrary")),
    )(r, lw, k, v, kk, b, dy, s0, _tri_ones(rev, nseq))


def _block_diag2(w):
    z = jnp.zeros_like(w[0])
    return jnp.concatenate([jnp.concatenate([w[0], z], axis=1), jnp.concatenate([z, w[1]], axis=1)], axis=0)


def _pad_in_cols(a):
    z = jnp.zeros(a.shape[:-1] + (SHIFT_PAD - SHIFT_COLS,), a.dtype)
    return jnp.concatenate([a[..., :SHIFT_COLS], z, a[..., SHIFT_COLS:]], axis=-1)


def _follow(small, token):
    return small if token is None else small + token[0:1, 0:1]


def _local_step(x, target, wts, *, tt, start_token=None, more_weights=None, grads_ready=None, small_ready=None):
    bsz, seq, _ = x.shape
    n_tok = bsz * seq
    row = lambda a: a.reshape(1, -1).astype(F32)
    x0 = x.reshape(n_tok, D_MODEL)
    tgt = target.reshape(n_tok, D_MODEL)
    ln = {k: row(wts[k]) for k in ("ln1_g", "ln1_b", "ln2_g", "ln2_b", "ln3_g", "ln3_b")}
    if grads_ready is None:
        grads_ready = lambda names, slabs: None

    w1i = wts["ffn1_w_in"]
    h1, act1 = _ffn_in(x0, w1i, tm=TM_FFN, after=start_token, name="ffn1_in")
    if more_weights is not None:
        wts = {**wts, **more_weights("ffn1_out", act1)}
    w1o = wts["ffn1_w_out"]
    z1, x1, x1b = _mm_ln([act1], w1o, x0, ln["ln1_g"], ln["ln1_b"], 0.5, tm=TM_LN, name="ffn1_out_ln1")
    if more_weights is not None:
        wts = {**wts, **more_weights("mix", x1b)}
    win = _pad_in_cols(wts["w_in"])
    zpad = jnp.zeros((1, SHIFT_PAD - SHIFT_COLS), F32)
    mu_p = jnp.concatenate([row(wts["mu_prev"]), zpad], axis=1)
    mu_n = jnp.concatenate([row(wts["mu_next"]), zpad], axis=1)
    w2b, a2b = _block_diag2(wts["w2"]), _block_diag2(wts["a2"])
    w0c, a0c = row(wts["w0"]), row(wts["a0"])
    g2p = jnp.concatenate([wts["g2"], jnp.zeros((GATE_PAD - GATE_LORA, RW), F32)], axis=0)
    k_k, k_a, r_k = row(wts["k_k"]), row(wts["k_a"]), row(wts["r_k"])
    lnx_g, lnx_b = row(wts["lnx_g"]), row(wts["lnx_b"])
    cdw, cb, clg, clb = wts["conv_dw"], row(wts["conv_b"]), row(wts["conv_ln_g"]), row(wts["conv_ln_b"])
    small = (mu_p, mu_n, w2b, w0c, a2b, a0c, g2p, k_k, k_a)
    seq3 = lambda a: a.reshape(bsz, seq, a.shape[-1])
    flat = lambda a: a.reshape(n_tok, a.shape[-1])

    p = _matmul(x1b, win, name="proj_in")
    r, v, kk, w, kd, b, g = _mix_prep(p, *small, seq=seq, tt=tt, name="mix_prep")
    scan_in = [seq3(a) for a in (r, w, kd, v, kk, b)]
    y0, s_chunks0 = _wkv_chunk_fwd(*scan_in, rev=False, name="wkv_fwd_dir0")
    y1, s_chunks1 = _wkv_chunk_fwd(*scan_in, rev=True, name="wkv_fwd_dir1")
    y0, y1 = flat(y0), flat(y1)
    yr = _mix_post(y0, y1, r, v, kd, g, lnx_g, lnx_b, r_k, tt=tt, name="mix_post")
    yc, yv = _conv_fwd(p, cdw, cb, clg, clb, seq=seq, tt=tt, name="conv_fwd")
    if more_weights is not None:
        wts = {**wts, **more_weights("out", yr)}
    wout, w2i, w2o = wts["w_out"], wts["ffn2_w_in"], wts["ffn2_w_out"]
    z2, x2, x2b = _mm_ln([yr, yv], wout, x1, ln["ln2_g"], ln["ln2_b"], 1.0, tm=TM_LN, name="proj_out_ln2")
    h2, act2 = _ffn_in(x2b, w2i, tm=TM_FFN, name="ffn2_in")

    gr = {}
    slab_rows = lambda a: a.reshape((N_CHIPS, a.shape[0] // N_CHIPS) + a.shape[1:])
    dw_kw = dict(ta=True, out_dtype=BF16)
    dz3, gr["ln3_g"], gr["ln3_b"], loss_part = _mm_ln_loss(act2, w2o, x2, ln["ln3_g"], ln["ln3_b"], tgt, 0.5, tm=TM_LN,
                                                           name="ffn2_out_ln3_loss")
    dh2 = _ffn_out_bwd(dz3, w2o, h2, tm=TM_FFN, name="ffn2_out_dx")
    gr["ffn2_w_out"] = slab_rows(_matmul(act2, dz3, scale=0.5, tm=D_FF // 2, name="ffn2_out_dw", **dw_kw))
    dz2, gr["ln2_g"], gr["ln2_b"] = _mm_nt_res([dh2], w2i, dz3, ln=(z2, ln["ln2_g"], ln["ln2_b"]), tm=TM_FFN,
                                               name="ffn2_in_dx_ln2")
    gr["ffn2_w_in"] = _matmul(x2b, dh2, col_slabs=True, tn=2 * D_FF // N_CHIPS, name="ffn2_in_dw", **dw_kw)
    dmix = _matmul(dz2, wout, tb=True, name="proj_out_dx")
    gr["w_out"] = slab_rows(jnp.concatenate([_matmul(yr, dz2, name="proj_out_dw_rwkv", **dw_kw),
                                             _matmul(yv, dz2, name="proj_out_dw_conv", **dw_kw)], axis=0))
    tok = grads_ready(("ffn2_w_out", "ffn2_w_in", "w_out"), [gr["ffn2_w_out"], gr["ffn2_w_in"], gr["w_out"]])
    dyr, dyv = (dmix, RW, 0), (dmix, RW, 1)
    dy, dr_p, dv_p, dkd_p, dg, gr["lnx_g"], gr["lnx_b"], gr["r_k"] = _mix_post_bwd(
        y0, y1, r, v, kd, g, _follow(lnx_g, tok), lnx_b, r_k, dyr, tt=tt, name="mix_post_bwd")
    dr0, dw0, dkd0, dv0, dk0, db0 = [flat(a) for a in _wkv_chunk_bwd(*scan_in, seq3(dy), s_chunks0, rev=False,
                                                                      name="wkv_bwd_dir0")]
    dr1, dw1, dkd1, dv1, dk1, db1 = [flat(a) for a in _wkv_chunk_bwd(*scan_in, seq3(dy), s_chunks1, rev=True,
                                                                      name="wkv_bwd_dir1")]
    ct_terms = [[dr_p, dr0, dr1], [dv_p, dv0, dv1], [dk0, dk1], [(dw0, dw1)], [dkd_p, (dkd0, dkd1)], [(db0, db1)], [dg]]
    dyc, gr["conv_ln_g"], gr["conv_ln_b"], gr["conv_b"] = _conv_post_bwd(yc, dyv, clg, clb, tt=tt, name="conv_post_bwd")
    dpc, ddw = _conv_bwd(dyc, p, cdw, seq=seq, tt=tt, name="conv_bwd")
    gr["conv_dw"] = ddw[:CONV_K]
    dps, dw2b, dw0c, da2b, da0c, dg2p, gr["k_k"], gr["k_a"] = _mix_prep_bwd(
        p, *small, ct_terms, seq=seq, tt=tt, name="mix_prep_bwd")
    gr["w2"] = jnp.stack([dw2b[:LORA, :RW], dw2b[LORA:, RW:]])
    gr["a2"] = jnp.stack([da2b[:LORA, :RW], da2b[LORA:, RW:]])
    gr["w0"], gr["a0"], gr["g2"] = dw0c.reshape(2, RW), da0c.reshape(2, RW), dg2p[:GATE_LORA]
    dpsh, dmu_p, dmu_n = _shift_bwd(dps, p, mu_p, mu_n, seq=seq, tt=tt, name="shift_bwd")
    gr["mu_prev"], gr["mu_next"] = dmu_p[:, :SHIFT_COLS], dmu_n[:, :SHIFT_COLS]
    dwin = jnp.concatenate([_matmul(x1b, dpsh, name="proj_in_dw_shift", **dw_kw)[:, :SHIFT_COLS],
                            _matmul(x1b, dpc, name="proj_in_dw_conv", **dw_kw)], axis=1)
    gr["w_in"] = jnp.moveaxis(dwin.reshape(D_MODEL, N_CHIPS, IN_COLS // N_CHIPS), 1, 0)
    tok = grads_ready(("w_in",), [gr["w_in"]])
    dz1, gr["ln1_g"], gr["ln1_b"] = _mm_nt_res([dpsh, dpc], win, dz2, ln=(z1, ln["ln1_g"], ln["ln1_b"]), tm=TM_FFN,
                                               after=tok, name="proj_in_dx_ln1")
    gr["loss"] = loss_part
    tok = small_ready(gr, loss_part) if small_ready is not None else None
    dh1 = _ffn_out_bwd(dz1, w1o, h1, tm=TM_FFN, after=tok, name="ffn1_out_dx")
    gr["ffn1_w_out"] = slab_rows(_matmul(act1, dz1, scale=0.5, tm=D_FF // 2, name="ffn1_out_dw", **dw_kw))
    tok = grads_ready(("ffn1_w_out",), [gr["ffn1_w_out"]])
    gr["ffn1_w_in"] = _matmul(x0, dh1, col_slabs=True, tn=2 * D_FF // N_CHIPS, after=tok, name="ffn1_in_dw", **dw_kw)
    tok = grads_ready(("ffn1_w_in",), [gr["ffn1_w_in"]])
    dx0 = _mm_nt_res([dh1], w1i, dz1, tm=TM_FFN, after=tok, name="ffn1_in_dx")
    return dx0.reshape(bsz, seq, D_MODEL), gr


def _mesh_pos():
    return lax.axis_index("x"), lax.axis_index("y"), lax.axis_index("c")


def _other_chips(x, y):
    return [(1 - x, y), (x, 1 - y), (1 - x, 1 - y)]


def _gather_chips(shards, *, name):
    n = len(shards)
    halves = [s.shape[0] // 2 for s in shards]
    assert all(2 * h == s.shape[0] for h, s in zip(halves, shards))

    def body(*refs):
        ins, outs = refs[:n], refs[n:2 * n]
        send_sems, recv_sems, fwd_send_sems, fwd_recv_sems, loc_sems = refs[2 * n:]
        x, y, c = _mesh_pos()
        q = 2 * x + y
        peers = _other_chips(x, y)
        local = [pltpu.make_async_copy(ins[a], outs[a].at[q], loc_sems.at[a]) for a in range(n)]
        for cp in local:
            cp.start()

        def half(a, chip, core):
            return outs[a].at[chip, pl.ds(core * halves[a], halves[a])]

        sends = [pltpu.make_async_remote_copy(ins[a].at[pl.ds(c * halves[a], halves[a])], half(a, q, c),
                                              send_sems.at[a, k], recv_sems.at[a, k],
                                              device_id=(px, py, c), device_id_type=MESH)
                 for a in range(n) for k, (px, py) in enumerate(peers)]
        for cp in sends:
            cp.start()
        passed = []
        for a in range(n):
            for k, (px, py) in enumerate(peers):
                mine = half(a, 2 * px + py, c)
                pltpu.make_async_remote_copy(mine, mine, send_sems.at[a, k], recv_sems.at[a, k],
                                             device_id=(px, py, c), device_id_type=MESH).wait_recv()
                cp = pltpu.make_async_remote_copy(mine, mine, fwd_send_sems.at[a, k], fwd_recv_sems.at[a, k],
                                                  device_id=(x, y, 1 - c), device_id_type=MESH)
                cp.start()
                passed.append(cp)
        for a in range(n):
            for k, (px, py) in enumerate(peers):
                theirs = half(a, 2 * px + py, 1 - c)
                pltpu.make_async_remote_copy(theirs, theirs, fwd_send_sems.at[a, k], fwd_recv_sems.at[a, k],
                                             device_id=(x, y, 1 - c), device_id_type=MESH).wait_recv()
        for cp in sends + passed:
            cp.wait_send()
        for cp in local:
            cp.wait()

    any_spec = pl.BlockSpec(memory_space=pl.ANY)
    return pl.pallas_call(
        body, name=name,
        out_shape=[jax.ShapeDtypeStruct((N_CHIPS,) + s.shape, s.dtype) for s in shards],
        in_specs=[any_spec] * n, out_specs=[any_spec] * n,
        scratch_shapes=[pltpu.SemaphoreType.DMA((n, 3))] * 4 + [pltpu.SemaphoreType.DMA((n,))],
        compiler_params=pltpu.CompilerParams(has_side_effects=True),
    )(*shards)


HBM_SPEC = pl.BlockSpec(memory_space=pltpu.HBM)
SEM_SPEC = pl.BlockSpec(memory_space=pltpu.SEMAPHORE)
ANY_SPEC = pl.BlockSpec(memory_space=pl.ANY)
SIDE_EFFECT = pltpu.SideEffectType.DATAFLOW_SIDE_EFFECTING


def _chip_copies(src_refs, land_refs, send_sems, recv_sems, scatter, arriving=False):
    x, y, c = _mesh_pos()
    cps = []
    for a, (src, land) in enumerate(zip(src_refs, land_refs)):
        for k, (px, py) in enumerate(_other_chips(x, y)):
            slot = k if scatter else (2 * px + py if arriving else 2 * x + y)
            cps.append(pltpu.make_async_remote_copy(src.at[2 * px + py] if scatter else src, land.at[slot],
                                                    send_sems.at[3 * a + k], recv_sems.at[3 * a + k],
                                                    device_id=(px, py, c), device_id_type=MESH))
    return cps


def _exchange_start(srcs, *, scatter, after, name):
    n = len(srcs)
    lands = [lax.empty((3,) + s.shape[1:] if scatter else (N_CHIPS,) + s.shape, s.dtype) for s in srcs]

    def body(*refs):
        src_refs, land_refs = refs[:n], refs[n:2 * n]
        send_sems, recv_sems = refs[2 * n + 1:2 * n + 3]
        token = refs[-1]
        for cp in _chip_copies(src_refs, land_refs, send_sems, recv_sems, scatter):
            cp.start()
        token[...] = jnp.zeros_like(token)

    hbm = lambda a: pltpu.with_memory_space_constraint(a, pltpu.HBM)
    outs = pl.pallas_call(
        body, name=name,
        out_shape=(pltpu.SemaphoreType.DMA((3 * n,)), pltpu.SemaphoreType.DMA((3 * n,)),
                   *[pltpu.HBM(a.shape, a.dtype) for a in srcs + lands], jax.ShapeDtypeStruct((8, LANES), F32)),
        in_specs=[HBM_SPEC] * (2 * n) + [ANY_SPEC],
        out_specs=(SEM_SPEC, SEM_SPEC, *[HBM_SPEC] * (2 * n), pl.BlockSpec(memory_space=pltpu.VMEM)),
        input_output_aliases={i: 2 + i for i in range(2 * n)},
        compiler_params=pltpu.CompilerParams(has_side_effects=SIDE_EFFECT),
    )(*[hbm(a) for a in srcs + lands], after)
    return outs[0], outs[1], list(outs[2:2 + n]), list(outs[2 + n:2 + 2 * n]), outs[-1]


def _exchange_wait(started, *, scatter, after, name):
    send_sems, recv_sems, srcs, lands, _ = started
    n = len(srcs)

    def body(*refs):
        src_refs, land_refs = refs[:n], refs[n:2 * n]
        send_s, recv_s = refs[2 * n:2 * n + 2]
        for cp in _chip_copies(src_refs, land_refs, send_s, recv_s, scatter, arriving=True):
            cp.wait_send()
            cp.wait_recv()

    outs = pl.pallas_call(
        body, name=name,
        out_shape=tuple(pltpu.HBM(a.shape, a.dtype) for a in srcs + lands),
        in_specs=[HBM_SPEC] * (2 * n) + [SEM_SPEC, SEM_SPEC, ANY_SPEC],
        out_specs=tuple([HBM_SPEC] * (2 * n)),
        input_output_aliases={i: i for i in range(2 * n)},
        compiler_params=pltpu.CompilerParams(has_side_effects=SIDE_EFFECT),
    )(*srcs, *lands, send_sems, recv_sems, after)
    return list(outs[:n]), list(outs[n:])


def _by_chip(own, land):
    xi, yi, _ = _mesh_pos()
    return lax.dynamic_update_index_in_dim(land, own, 2 * xi + yi, 0)


def _swap_sibling(arrs, *, name):
    n = len(arrs)

    def body(*refs):
        ins, outs = refs[:n], refs[n:2 * n]
        send_sems, recv_sems = refs[2 * n:]
        x, y, c = _mesh_pos()
        cps = [pltpu.make_async_remote_copy(ins[a], outs[a], send_sems.at[a], recv_sems.at[a],
                                            device_id=(x, y, 1 - c), device_id_type=MESH) for a in range(n)]
        for cp in cps:
            cp.start()
        for cp in cps:
            cp.wait_recv()
        for cp in cps:
            cp.wait_send()

    any_spec = pl.BlockSpec(memory_space=pl.ANY)
    return pl.pallas_call(
        body, name=name,
        out_shape=[jax.ShapeDtypeStruct(s.shape, s.dtype) for s in arrs],
        in_specs=[any_spec] * n, out_specs=[any_spec] * n,
        scratch_shapes=[pltpu.SemaphoreType.DMA((n,)), pltpu.SemaphoreType.DMA((n,))],
        compiler_params=pltpu.CompilerParams(has_side_effects=True),
    )(*arrs)


def _device_copies(v_ref, land_ref, send_sems, recv_sems, arriving=False):
    x, y, c = _mesh_pos()
    me = 4 * x + 2 * y + c
    cps = []
    for m in range(1, 8):
        px, py, pc = (x + ((m >> 2) & 1)) % 2, (y + ((m >> 1) & 1)) % 2, (c + (m & 1)) % 2
        slot = 4 * px + 2 * py + pc if arriving else me
        cps.append(pltpu.make_async_remote_copy(v_ref, land_ref.at[slot], send_sems.at[m - 1], recv_sems.at[m - 1],
                                                device_id=(px, py, pc), device_id_type=MESH))
    return cps


def _allsum_start(vec, *, after, name):
    land = lax.empty((8,) + vec.shape, F32)

    def body(v_ref, land_ref, _after, send_sems, recv_sems, v_thru, land_thru, token):
        for cp in _device_copies(v_ref, land_ref, send_sems, recv_sems):
            cp.start()
        token[...] = jnp.zeros_like(token)

    hbm = lambda a: pltpu.with_memory_space_constraint(a, pltpu.HBM)
    return pl.pallas_call(
        body, name=name,
        out_shape=(pltpu.SemaphoreType.DMA((7,)), pltpu.SemaphoreType.DMA((7,)), pltpu.HBM(vec.shape, F32),
                   pltpu.HBM(land.shape, F32), jax.ShapeDtypeStruct((8, LANES), F32)),
        in_specs=[HBM_SPEC, HBM_SPEC, ANY_SPEC],
        out_specs=(SEM_SPEC, SEM_SPEC, HBM_SPEC, HBM_SPEC, pl.BlockSpec(memory_space=pltpu.VMEM)),
        input_output_aliases={0: 2, 1: 3},
        compiler_params=pltpu.CompilerParams(has_side_effects=SIDE_EFFECT),
    )(hbm(vec), hbm(land), after)


def _allsum_wait(started, *, after, name):
    send_sems, recv_sems, vec, land, _ = started

    def body(v_ref, land_ref, send_s, recv_s, _after, v_dead, got):
        for cp in _device_copies(v_ref, land_ref, send_s, recv_s, arriving=True):
            cp.wait_send()
            cp.wait_recv()

    vec, land = pl.pallas_call(
        body, name=name,
        out_shape=(pltpu.HBM(vec.shape, F32), pltpu.HBM(land.shape, F32)),
        in_specs=[HBM_SPEC, HBM_SPEC, SEM_SPEC, SEM_SPEC, ANY_SPEC],
        out_specs=(HBM_SPEC, HBM_SPEC),
        input_output_aliases={0: 0, 1: 1},
        compiler_params=pltpu.CompilerParams(has_side_effects=SIDE_EFFECT),
    )(vec, land, send_sems, recv_sems, after)
    xi, yi, ci = _mesh_pos()
    every = lax.dynamic_update_index_in_dim(land, vec, 4 * xi + 2 * yi + ci, 0)

    def add(e_ref, o_ref):
        acc = e_ref[0]
        for d in range(1, 8):
            acc = acc + e_ref[d]
        o_ref[...] = acc

    vm = pl.BlockSpec(memory_space=pltpu.VMEM)
    return pl.pallas_call(add, name=name + "_sum", out_shape=jax.ShapeDtypeStruct(vec.shape, F32), in_specs=[vm],
                          out_specs=vm, compiler_params=_cparams())(every)


def _adamw(w, g, m, v):
    m = ADAM_B1 * m + (1.0 - ADAM_B1) * g
    v = ADAM_B2 * v + (1.0 - ADAM_B2) * (g * g)
    m_hat = m / (1.0 - ADAM_B1 ** ADAM_STEP)
    v_hat = v / (1.0 - ADAM_B2 ** ADAM_STEP)
    delta = -ADAM_LR * (m_hat / (jnp.sqrt(v_hat) + ADAM_EPS) + ADAM_WD * w)
    return delta, m, v


def _sum4(mine, land, *, name):
    rows, cols = mine.shape
    tr = _pick_rows(rows)

    def body(a_ref, l_ref, o_ref):
        o_ref[...] = (a_ref[...].astype(F32) + l_ref[0].astype(F32)) + (l_ref[1].astype(F32) + l_ref[2].astype(F32))

    return pl.pallas_call(
        body, name=name, out_shape=jax.ShapeDtypeStruct((rows, cols), F32), grid=(rows // tr,),
        in_specs=[pl.BlockSpec((tr, cols), lambda i: (i, 0)), pl.BlockSpec((3, tr, cols), lambda i: (0, i, 0))],
        out_specs=pl.BlockSpec((tr, cols), lambda i: (i, 0)),
        compiler_params=_cparams(("parallel",)),
    )(mine, land)


def _pick_rows(rows, want=256):
    for t in range(min(want, rows) // 8 * 8, 0, -8):
        if rows % t == 0:
            return t
    return rows


def _sum_adam(h_mine, h_sib, w, m, v, *, name):
    rows, cols = w.shape
    tr = _pick_rows(rows)

    def body(a_ref, b_ref, w_ref, m_ref, v_ref, g_o, d_o, m_o, v_o):
        g = a_ref[...] + b_ref[...]
        d, mn, vn = _adamw(w_ref[...], g, m_ref[...], v_ref[...])
        g_o[...], d_o[...], m_o[...], v_o[...] = g, d, mn, vn

    spec = pl.BlockSpec((tr, cols), lambda i: (i, 0))
    return pl.pallas_call(
        body, name=name, out_shape=[jax.ShapeDtypeStruct((rows, cols), F32)] * 4, grid=(rows // tr,),
        in_specs=[spec] * 5, out_specs=[spec] * 4, compiler_params=_cparams(("parallel",)),
    )(h_mine, h_sib, w, m, v)


def _adam_rows(w, g, m, v, *, name):
    def body(w_ref, g_ref, m_ref, v_ref, d_o, m_o, v_o):
        d_o[...], m_o[...], v_o[...] = _adamw(w_ref[...], g_ref[...], m_ref[...], v_ref[...])

    vm = pl.BlockSpec(memory_space=pltpu.VMEM)
    return pl.pallas_call(
        body, name=name, out_shape=[jax.ShapeDtypeStruct(w.shape, F32)] * 3,
        in_specs=[vm] * 4, out_specs=[vm] * 3, compiler_params=_cparams(),
    )(w, g, m, v)


def _size(shape):
    size = 1
    for d in shape:
        size *= d
    return size


def _pack_rows(arrs):
    blocks = []
    for a in arrs:
        flat = a.reshape(-1).astype(F32)
        flat = jnp.concatenate([flat, jnp.zeros((-flat.shape[0] % (8 * LANES),), F32)])
        blocks.append(flat.reshape(-1, LANES))
    return jnp.concatenate(blocks, axis=0)


def _unpack_rows(packed, shapes):
    out, row = [], 0
    for s in shapes:
        rows = -(-_size(s) // (8 * LANES)) * 8
        out.append(packed[row:row + rows].reshape(-1)[:_size(s)].reshape(s))
        row += rows
    return out


WEIGHTS = ['ffn1_w_in', 'ffn1_w_out', 'w_in', 'mu_prev', 'mu_next', 'w0', 'w2', 'a0', 'a2', 'g2', 'k_k', 'k_a', 'r_k',
           'lnx_g', 'lnx_b', 'conv_dw', 'conv_b', 'conv_ln_g', 'conv_ln_b', 'w_out', 'ffn2_w_in', 'ffn2_w_out',
           'ln1_g', 'ln1_b', 'ln2_g', 'ln2_b', 'ln3_g', 'ln3_b']
COL_SHARDED = ('ffn1_w_in', 'w_in', 'ffn2_w_in')
ROW_SHARDED = ('ffn1_w_out', 'w_out', 'ffn2_w_out')
BIG = COL_SHARDED + ROW_SHARDED
SMALL_SHARDED = ('w0', 'w2', 'a0', 'a2', 'g2', 'conv_dw')
REPLICATED = tuple(n for n in WEIGHTS if n not in BIG + SMALL_SHARDED)


def _train_step(x, target, w, m, v, *, tt):
    xi, yi, _ = _mesh_pos()
    q = 2 * xi + yi

    later = {"ffn1_out": ("ffn1_w_out",), "mix": ("w_in",) + SMALL_SHARDED, "out": ("w_out", "ffn2_w_in", "ffn2_w_out")}
    shard = lambda n: w[n][0].astype(BF16) if n in BIG else w[n][0]
    small_names = REPLICATED + SMALL_SHARDED

    def whole(n, slabs):
        if n in ROW_SHARDED:
            return slabs.reshape((-1,) + slabs.shape[2:])
        if n in ("ffn1_w_in", "ffn2_w_in"):
            return slabs
        return jnp.moveaxis(slabs, 0, -2).reshape(slabs.shape[1:-1] + (N_CHIPS * slabs.shape[-1],))

    full = {n: w[n][0] for n in REPLICATED}
    first = _gather_chips([shard("ffn1_w_in")], name="gather_ffn1_in")
    full["ffn1_w_in"] = whole("ffn1_w_in", first[0])
    started, token = {}, first[0]
    for stage, names in later.items():
        started[stage] = _exchange_start([shard(n) for n in names], scatter=False, after=token,
                                         name="gather_%s_start" % stage)
        token = started[stage][-1]

    def more_weights(stage, after):
        own, land = _exchange_wait(started[stage], scatter=False, after=after, name="gather_%s_wait" % stage)
        got = {n: whole(n, _by_chip(o, l)) for n, o, l in zip(later[stage], own, land)}
        full.update(got)
        return got

    small_sent = []

    def small_ready(gr, loss_part):
        vec = _pack_rows([gr[n] for n in small_names] + [loss_part[0:1, 0:1]])
        small_sent.append(_allsum_start(vec, after=vec, name="reduce_small_start"))
        return small_sent[0][-1]

    sent = []

    def grads_ready(names, slabs):
        started = _exchange_start(slabs, scatter=True, after=slabs[0], name="scatter_%s_start" % names[0])
        sent.append((names, started))
        return started[-1]

    grad_x, gr = _local_step(x, target, full, tt=tt, start_token=token, more_weights=more_weights,
                             grads_ready=grads_ready, small_ready=small_ready)

    halves = {}
    for names, started in sent:
        stacks, landed = _exchange_wait(started, scatter=True, after=grad_x, name="scatter_%s_wait" % names[0])
        for n, s, l in zip(names, stacks, landed):
            halves[n] = _sum4(lax.dynamic_index_in_dim(s, q, 0, keepdims=False), l, name="sum4_" + n)
    halves = [halves[n] for n in BIG]
    sib = _swap_sibling(halves, name="swap_halves")
    grad, delta, new_m, new_v = {}, {}, {}, {}
    for n, h, hs in zip(BIG, halves, sib):
        outs = _sum_adam(h, hs, w[n][0], m[n][0], v[n][0], name="adam_" + n)
        grad[n], delta[n], new_m[n], new_v[n] = [o[None] for o in outs]

    small_full_shapes = [full[n].shape for n in small_names]
    red = _allsum_wait(small_sent[0], after=grad_x, name="reduce_small_wait")
    *red, loss = _unpack_rows(red, small_full_shapes + [()])
    red = dict(zip(small_names, red))
    gsm = {}
    for n in REPLICATED:
        gsm[n] = red[n].reshape(w[n].shape)
    for n in SMALL_SHARDED:
        width = w[n].shape[-1]
        gsm[n] = lax.dynamic_slice_in_dim(red[n], q * width, width, axis=red[n].ndim - 1).reshape(w[n].shape)
    shapes = [w[n].shape for n in small_names]
    d_p, m_p, v_p = _adam_rows(_pack_rows([w[n] for n in small_names]), _pack_rows([gsm[n] for n in small_names]),
                               _pack_rows([m[n] for n in small_names]), _pack_rows([v[n] for n in small_names]),
                               name="adam_small")
    for n, dd, mm, vv in zip(small_names, _unpack_rows(d_p, shapes), _unpack_rows(m_p, shapes), _unpack_rows(v_p, shapes)):
        grad[n], delta[n], new_m[n], new_v[n] = gsm[n], dd, mm, vv
    return loss, grad_x, grad, delta, new_m, new_v


def kernel(x, ffn1_w_in, ffn1_w_out, w_in, mu_prev, mu_next, w0, w2, a0, a2, g2, k_k, k_a, r_k, lnx_g, lnx_b, conv_dw, conv_b, conv_ln_g, conv_ln_b, w_out, ffn2_w_in, ffn2_w_out, ln1_g, ln1_b, ln2_g, ln2_b, ln3_g, ln3_b, loss_target, m_ffn1_w_in, m_ffn1_w_out, m_w_in, m_mu_prev, m_mu_next, m_w0, m_w2, m_a0, m_a2, m_g2, m_k_k, m_k_a, m_r_k, m_lnx_g, m_lnx_b, m_conv_dw, m_conv_b, m_conv_ln_g, m_conv_ln_b, m_w_out, m_ffn2_w_in, m_ffn2_w_out, m_ln1_g, m_ln1_b, m_ln2_g, m_ln2_b, m_ln3_g, m_ln3_b, v_ffn1_w_in, v_ffn1_w_out, v_w_in, v_mu_prev, v_mu_next, v_w0, v_w2, v_a0, v_a2, v_g2, v_k_k, v_k_a, v_r_k, v_lnx_g, v_lnx_b, v_conv_dw, v_conv_b, v_conv_ln_g, v_conv_ln_b, v_w_out, v_ffn2_w_in, v_ffn2_w_out, v_ln1_g, v_ln1_b, v_ln2_g, v_ln2_b, v_ln3_g, v_ln3_b):
    args = dict(locals())
    w = {n: args[n] for n in WEIGHTS}
    m = {n: args["m_" + n] for n in WEIGHTS}
    v = {n: args["v_" + n] for n in WEIGHTS}
    seq = x.shape[1]
    loss, grad_x, grad, delta, new_m, new_v = _train_step(x, loss_target, w, m, v, tt=min(256, seq))
    return (loss, grad_x, *[grad[n] for n in WEIGHTS], *[delta[n] for n in WEIGHTS],
            *[new_m[n] for n in WEIGHTS], *[new_v[n] for n in WEIGHTS])
```

```python
import functools

import jax
import jax.numpy as jnp
from jax import lax
from jax.experimental import pallas as pl
from jax.experimental.pallas import tpu as pltpu

F32 = jnp.float32
BF16 = jnp.bfloat16

D_MODEL = 1024
RW = 512
HEAD = 64
CW = 512
CONV_K = 31
CONV_ROWS = 32
SHIFT_ROWS = 16
CONV_PAD = 15
D_FF = 2816
LORA = 64
GATE_LORA = 160
GATE_PAD = 256
SHIFT_COLS = 1952
SHIFT_PAD = 2048
IN_COLS = 2976
IN_PAD = 3072
LN_EPS = 1e-5
GN_EPS = 64e-5
NORM_EPS = 1e-12
ALPHA = 2.0 ** 0.25
DECAY_SCALE = 0.6065306597126334
ADAM_LR, ADAM_B1, ADAM_B2, ADAM_EPS, ADAM_WD, ADAM_STEP = 0.001, 0.9, 0.999, 1e-08, 0.01, 10
N_CHIPS = 4
VMEM_LIMIT = 56 * 1024 * 1024
TM_FFN = 256
TM_LN = 512

MESH = pl.DeviceIdType.MESH


def _cparams(sem=None, **kw):
    return pltpu.CompilerParams(dimension_semantics=sem, vmem_limit_bytes=VMEM_LIMIT, **kw)


LANES = 128


def _pick_tile(dim, want):
    for t in range(min(want, dim) // LANES * LANES, 0, -LANES):
        if dim % t == 0:
            return t
    return dim


def _after_operand(after):
    return ([], []) if after is None else ([pl.BlockSpec(memory_space=pl.ANY)], [after])


def _matmul(a, b, *, ta=False, tb=False, out_dtype=F32, tm=1024, tn=1024, tk=1024, scale=1.0, col_slabs=False,
            after=None, name):
    after_specs, after_args = _after_operand(after)
    if ta:
        k_dim, m_dim = a.shape
    else:
        m_dim, k_dim = a.shape
    n_dim = b.shape[0] if tb else b.shape[1]
    tm, tn, tk = _pick_tile(m_dim, tm), _pick_tile(n_dim, tn), _pick_tile(k_dim, tk)
    assert m_dim % tm == 0 and n_dim % tn == 0 and k_dim % tk == 0, (name, a.shape, b.shape, tm, tn, tk)
    nk = k_dim // tk
    dims = (((0,) if ta else (1,), (1,) if tb else (0,)), ((), ()))
    if col_slabs:
        out_shape = jax.ShapeDtypeStruct((n_dim // tn, m_dim, tn), out_dtype)
        out_spec = pl.BlockSpec((None, tm, tn), lambda i, j, k: (j, i, 0))
    else:
        out_shape = jax.ShapeDtypeStruct((m_dim, n_dim), out_dtype)
        out_spec = pl.BlockSpec((tm, tn), lambda i, j, k: (i, j))

    def body(a_ref, b_ref, *rest):
        o_ref, acc_ref = rest[-2:]
        kk = pl.program_id(2)

        @pl.when(kk == 0)
        def _():
            acc_ref[...] = jnp.zeros_like(acc_ref)

        acc_ref[...] += lax.dot_general(a_ref[...].astype(BF16), b_ref[...].astype(BF16), dims,
                                        preferred_element_type=F32)

        @pl.when(kk == nk - 1)
        def _():
            o_ref[...] = (acc_ref[...] * scale).astype(o_ref.dtype)

    a_spec = pl.BlockSpec((tk, tm), lambda i, j, k: (k, i)) if ta else pl.BlockSpec((tm, tk), lambda i, j, k: (i, k))
    b_spec = pl.BlockSpec((tn, tk), lambda i, j, k: (j, k)) if tb else pl.BlockSpec((tk, tn), lambda i, j, k: (k, j))
    return pl.pallas_call(
        body, name=name,
        out_shape=out_shape,
        grid=(m_dim // tm, n_dim // tn, nk),
        in_specs=[a_spec, b_spec] + after_specs,
        out_specs=out_spec,
        scratch_shapes=[pltpu.VMEM((tm, tn), F32)],
        compiler_params=_cparams(("parallel", "parallel", "arbitrary")),
    )(a, b, *after_args)


def _whole(shape):
    nd = len(shape)
    return pl.BlockSpec(shape, lambda i: (0,) * nd)


def _ffn_in(x, w, *, tm, after=None, name):
    n_tok = x.shape[0]
    sw = w.shape[2]
    tm = min(tm, n_tok)

    after_specs, after_args = _after_operand(after)

    def body(x_ref, w_ref, *rest):
        h_ref, a_ref = rest[-2:]
        xb = x_ref[...].astype(BF16)
        for s in range(2):
            g = jnp.dot(xb, w_ref[s], preferred_element_type=F32)
            u = jnp.dot(xb, w_ref[s + 2], preferred_element_type=F32)
            h_ref[:, s * sw:(s + 1) * sw] = g.astype(BF16)
            h_ref[:, (s + 2) * sw:(s + 3) * sw] = u.astype(BF16)
            a_ref[:, s * sw:(s + 1) * sw] = (_silu(g) * u).astype(BF16)

    return pl.pallas_call(
        body, name=name,
        out_shape=[jax.ShapeDtypeStruct((n_tok, 2 * D_FF), BF16), jax.ShapeDtypeStruct((n_tok, D_FF), BF16)],
        grid=(n_tok // tm,),
        in_specs=[pl.BlockSpec((tm, D_MODEL), lambda i: (i, 0)), _whole(w.shape)] + after_specs,
        out_specs=[pl.BlockSpec((tm, 2 * D_FF), lambda i: (i, 0)), pl.BlockSpec((tm, D_FF), lambda i: (i, 0))],
        compiler_params=_cparams(("parallel",)),
    )(x, w, *after_args)


def _mm_ln(a_list, w, xres, g, b, fscale, *, tm, name):
    n_tok = xres.shape[0]
    tm = min(tm, n_tok)
    na = len(a_list)

    def body(*refs):
        a_refs = refs[:na]
        w_ref, x_ref, g_ref, b_ref, z_o, y_o, yb_o = refs[na:]
        f, off = None, 0
        for a_ref in a_refs:
            k = a_ref.shape[1]
            t = jnp.dot(a_ref[...].astype(BF16), w_ref[off:off + k, :], preferred_element_type=F32)
            f = t if f is None else f + t
            off += k
        z = ALPHA * x_ref[...] + fscale * f
        y = _layer_norm(z, g_ref[...], b_ref[...])
        z_o[...] = z
        y_o[...] = y
        yb_o[...] = y.astype(BF16)

    tile = pl.BlockSpec((tm, D_MODEL), lambda i: (i, 0))
    return pl.pallas_call(
        body, name=name,
        out_shape=[jax.ShapeDtypeStruct((n_tok, D_MODEL), F32)] * 2 + [jax.ShapeDtypeStruct((n_tok, D_MODEL), BF16)],
        grid=(n_tok // tm,),
        in_specs=[pl.BlockSpec((tm, a.shape[1]), lambda i: (i, 0)) for a in a_list]
        + [_whole(w.shape), tile, _whole(g.shape), _whole(b.shape)],
        out_specs=[tile, tile, tile],
        compiler_params=_cparams(("parallel",)),
    )(*a_list, w, xres, g, b)


def _mm_ln_loss(a, w, xres, g, b, target, fscale, *, tm, name):
    n_tok = xres.shape[0]
    tm = min(tm, n_tok)

    def body(a_ref, w_ref, x_ref, g_ref, b_ref, t_ref, dz_o, dg_o, db_o, loss_o):
        i = pl.program_id(0)
        z = ALPHA * x_ref[...] + fscale * jnp.dot(a_ref[...].astype(BF16), w_ref[...], preferred_element_type=F32)
        y, vjp = jax.vjp(_layer_norm, z, g_ref[...], b_ref[...])
        e = y - t_ref[...]
        dz, dg, db = vjp(e * (1.0 / D_MODEL))

        @pl.when(i == 0)
        def _():
            dg_o[...] = jnp.zeros_like(dg_o)
            db_o[...] = jnp.zeros_like(db_o)
            loss_o[...] = jnp.zeros_like(loss_o)
        dz_o[...] = dz
        dg_o[...] += dg
        db_o[...] += db
        loss_o[...] += 0.5 * jnp.sum(jnp.mean(e * e, axis=-1, keepdims=True), axis=0, keepdims=True)

    tile = pl.BlockSpec((tm, D_MODEL), lambda i: (i, 0))
    row = pl.BlockSpec((1, D_MODEL), lambda i: (0, 0))
    return pl.pallas_call(
        body, name=name,
        out_shape=[jax.ShapeDtypeStruct((n_tok, D_MODEL), F32), jax.ShapeDtypeStruct((1, D_MODEL), F32),
                   jax.ShapeDtypeStruct((1, D_MODEL), F32), jax.ShapeDtypeStruct((8, LANES), F32)],
        grid=(n_tok // tm,),
        in_specs=[pl.BlockSpec((tm, a.shape[1]), lambda i: (i, 0)), _whole(w.shape), tile, row, row, tile],
        out_specs=[tile, row, row, pl.BlockSpec((8, LANES), lambda i: (0, 0))],
        compiler_params=_cparams(("arbitrary",)),
    )(a, w, xres, g, b, target)


def _ffn_out_bwd(dz, w, h, *, tm, after=None, name):
    n_tok = dz.shape[0]
    tm = min(tm, n_tok)
    cw = D_FF // 2
    after_specs, after_args = _after_operand(after)

    def body(dz_ref, w_ref, h_ref, *rest):
        dh_ref = rest[-1]
        dzb = dz_ref[...].astype(BF16)
        for s in range(2):
            dact = 0.5 * lax.dot_general(dzb, w_ref[s * cw:(s + 1) * cw, :], (((1,), (1,)), ((), ())),
                                         preferred_element_type=F32)
            gate = h_ref[:, s * cw:(s + 1) * cw].astype(F32)
            up = h_ref[:, D_FF + s * cw:D_FF + (s + 1) * cw].astype(F32)
            sg = _sigmoid(gate)
            dh_ref[:, s * cw:(s + 1) * cw] = (dact * up * sg * (1.0 + gate * (1.0 - sg))).astype(BF16)
            dh_ref[:, D_FF + s * cw:D_FF + (s + 1) * cw] = (dact * gate * sg).astype(BF16)

    wide = pl.BlockSpec((tm, 2 * D_FF), lambda i: (i, 0))
    return pl.pallas_call(
        body, name=name,
        out_shape=jax.ShapeDtypeStruct((n_tok, 2 * D_FF), BF16),
        grid=(n_tok // tm,),
        in_specs=[pl.BlockSpec((tm, D_MODEL), lambda i: (i, 0)), _whole(w.shape), wide] + after_specs,
        out_specs=wide,
        compiler_params=_cparams(("parallel",)),
    )(dz, w, h, *after_args)


def _mm_nt_res(a_list, w, dz, *, tm, ln=None, after=None, name):
    n_tok = dz.shape[0]
    tm = min(tm, n_tok)
    na = len(a_list)
    nt = (((1,), (1,)), ((), ()))
    after_specs, after_args = _after_operand(after)
    n_out = 1 if ln is None else 3

    def body(*refs):
        a_refs = refs[:na]
        w_ref, dz_ref, o_ref = refs[na], refs[na + 1], refs[-n_out]
        acc = ALPHA * dz_ref[...]
        if len(w_ref.shape) == 3:
            cw = w_ref.shape[2]
            for s in range(w_ref.shape[0]):
                acc = acc + lax.dot_general(a_refs[0][:, s * cw:(s + 1) * cw], w_ref[s], nt, preferred_element_type=F32)
        else:
            off = 0
            for a_ref in a_refs:
                k = a_ref.shape[1]
                acc = acc + lax.dot_general(a_ref[...], w_ref[:, off:off + k], nt, preferred_element_type=F32)
                off += k
        if ln is None:
            o_ref[...] = acc
            return
        z_ref, g_ref, b_ref = refs[na + 2:na + 5]
        dg_o, db_o = refs[-2:]
        _, vjp = jax.vjp(_layer_norm, z_ref[...], g_ref[...], b_ref[...])
        o_ref[...], dg, db = vjp(acc)

        @pl.when(pl.program_id(0) == 0)
        def _():
            dg_o[...] = jnp.zeros_like(dg_o)
            db_o[...] = jnp.zeros_like(db_o)
        dg_o[...] += dg
        db_o[...] += db

    tile = pl.BlockSpec((tm, D_MODEL), lambda i: (i, 0))
    row = pl.BlockSpec((1, D_MODEL), lambda i: (0, 0))
    out_shape = [jax.ShapeDtypeStruct((n_tok, D_MODEL), F32)]
    ln_specs, ln_args, out_specs = [], [], [tile]
    if ln is not None:
        ln_specs, ln_args = [tile, row, row], list(ln)
        out_shape += [jax.ShapeDtypeStruct((1, D_MODEL), F32)] * 2
        out_specs += [row, row]
    outs = pl.pallas_call(
        body, name=name,
        out_shape=out_shape,
        grid=(n_tok // tm,),
        in_specs=[pl.BlockSpec((tm, a.shape[1]), lambda i: (i, 0)) for a in a_list] + [_whole(w.shape), tile]
        + ln_specs + after_specs,
        out_specs=out_specs,
        compiler_params=_cparams(("parallel",) if ln is None else ("arbitrary",)),
    )(*a_list, w, dz, *ln_args, *after_args)
    return outs[0] if ln is None else outs


def _rowcall(fn, tok_in, full_in, tok_out, acc_out, *, tt, name):
    views = [a if isinstance(a, tuple) else (a, a.shape[1], 0) for a in tok_in]
    tok_in = [a for a, _, _ in views]
    n_tok = tok_in[0].shape[0]
    assert n_tok % tt == 0, (name, n_tok, tt)
    n_ti, n_fi, n_to = len(tok_in), len(full_in), len(tok_out)

    def body(*refs):
        i = pl.program_id(0)
        ins = [r[...] for r in refs[:n_ti + n_fi]]
        outs = fn(i, *ins)
        o_refs = refs[n_ti + n_fi:]
        for r, val in zip(o_refs[:n_to], outs[:n_to]):
            r[...] = val.astype(r.dtype)
        if acc_out:
            @pl.when(i == 0)
            def _():
                for r in o_refs[n_to:]:
                    r[...] = jnp.zeros_like(r)
            for r, val in zip(o_refs[n_to:], outs[n_to:]):
                r[...] += val.reshape(r.shape).astype(F32)

    in_specs = [pl.BlockSpec((tt, width), functools.partial(lambda k, i: (i, k), k)) for _, width, k in views]
    in_specs += [pl.BlockSpec(a.shape, lambda i: (0, 0)) for a in full_in]
    out_specs = [pl.BlockSpec((tt, c), lambda i: (i, 0)) for c, _ in tok_out]
    out_specs += [pl.BlockSpec(s, lambda i: (0, 0)) for s in acc_out]
    out_shape = [jax.ShapeDtypeStruct((n_tok, c), dt) for c, dt in tok_out]
    out_shape += [jax.ShapeDtypeStruct(s, F32) for s in acc_out]
    return pl.pallas_call(
        body, name=name, out_shape=out_shape, grid=(n_tok // tt,), in_specs=in_specs, out_specs=out_specs,
        compiler_params=_cparams(("arbitrary",) if acc_out else ("parallel",)),
    )(*tok_in, *full_in)


@jax.custom_vjp
def _bdot(a, b):
    return jnp.dot(a.astype(BF16), b.astype(BF16), preferred_element_type=F32)


def _bdot_fwd(a, b):
    return _bdot(a, b), (a, b)


def _bdot_bwd(res, g):
    a, b = res
    g16 = g.astype(BF16)
    da = lax.dot_general(g16, b.astype(BF16), (((1,), (1,)), ((), ())), preferred_element_type=F32)
    db = lax.dot_general(a.astype(BF16), g16, (((0,), (0,)), ((), ())), preferred_element_type=F32)
    return da, db


_bdot.defvjp(_bdot_fwd, _bdot_bwd)


def _split16(x):
    hi = x.astype(BF16)
    lo = (x - hi.astype(F32)).astype(BF16)
    return hi, lo


def _segsum_raw(x, e2):
    hi, lo = _split16(x)
    outs = []
    for c in range(x.shape[1] // 256):
        lhs = jnp.concatenate([hi[:, 256 * c:256 * (c + 1)], lo[:, 256 * c:256 * (c + 1)]], axis=1)
        outs.append(jnp.dot(lhs, e2, preferred_element_type=F32))
    return jnp.concatenate(outs, axis=1)


@jax.custom_vjp
def _segsum(x, e2):
    return _segsum_raw(x, e2)


def _segsum_fwd(x, e2):
    return _segsum_raw(x, e2), e2


def _segsum_bwd(e2, g):
    return _segsum_raw(g, e2), jnp.zeros_like(e2)


_segsum.defvjp(_segsum_fwd, _segsum_bwd)


def _head_ones():
    r = lax.broadcasted_iota(jnp.int32, (512, 256), 0) % 256
    c = lax.broadcasted_iota(jnp.int32, (512, 256), 1)
    return (r // HEAD == c // HEAD).astype(BF16)


def _sigmoid(x):
    return 1.0 / (1.0 + jnp.exp(-x))


def _silu(x):
    return x * _sigmoid(x)


def _layer_norm(z, g, b, eps=LN_EPS):
    mu = jnp.mean(z, axis=-1, keepdims=True)
    zc = z - mu
    var = jnp.mean(zc * zc, axis=-1, keepdims=True)
    return zc * lax.rsqrt(var + eps) * g + b


def _prep(ps, w2b, w0c, a2b, a0c, g2p, k_k, k_a, e2):
    r, k, v = ps[:, 0:512], ps[:, 512:1024], ps[:, 1024:1536]
    wd, ad, gd = ps[:, 1536:1664], ps[:, 1664:1792], ps[:, 1792:2048]
    lw = _bdot(jnp.tanh(wd), w2b) + w0c
    decay = -DECAY_SCALE * _sigmoid(lw)
    a = _sigmoid(_bdot(ad, a2b) + a0c)
    g = _bdot(_sigmoid(gd), g2p)
    kkr = k * k_k
    nrm = jnp.sqrt(_segsum(kkr * kkr, e2))
    kk = kkr / jnp.maximum(nrm, NORM_EPS)
    k2 = jnp.concatenate([k, k], axis=1)
    ka2 = jnp.concatenate([k_a, k_a], axis=1)
    kd = k2 * (1.0 + (a - 1.0) * ka2)
    b = jnp.concatenate([kk, kk], axis=1) * a
    return r, v, kk, decay, kd, b, g


def _post(y0, y1, r, v, kd, g, lnx_g, lnx_b, r_k, e2):
    y = y0 + y1
    mu = _segsum(y, e2) * (1.0 / HEAD)
    yc = y - mu
    var = _segsum(yc * yc, e2) * (1.0 / HEAD)
    yn = yc * lax.rsqrt(var + GN_EPS) * lnx_g + lnx_b
    bonus = _segsum(r * (kd[:, :RW] + kd[:, RW:]) * r_k, e2)
    return (yn + bonus * v) * g


def _conv_post(yc, ln_g, ln_b):
    return _silu(_layer_norm(yc, ln_g, ln_b))


def _halo_specs(cols_block, hb, tt, n_tok, col_idx):
    nb = n_tok // hb
    prev = pl.BlockSpec((hb, cols_block), lambda i: (jnp.maximum(i * (tt // hb) - 1, 0), col_idx))
    nxt = pl.BlockSpec((hb, cols_block), lambda i: (jnp.minimum((i + 1) * (tt // hb), nb - 1), col_idx))
    return prev, nxt


def _mix_prep(p, mu_p, mu_n, w2b, w0c, a2b, a0c, g2p, k_k, k_a, *, seq, tt, name):
    n_tok = p.shape[0]
    tps = seq // tt
    e2 = _head_ones()

    def body(p_ref, hp_ref, hn_ref, mup_ref, mun_ref, w2b_ref, w0c_ref, a2b_ref, a0c_ref, g2p_ref, kk_ref, ka_ref,
             e2_ref, r_o, v_o, kk_o, w_o, kd_o, b_o, g_o, ext):
        i = pl.program_id(0)
        first = (i % tps) == 0
        last = (i % tps) == tps - 1
        pv = p_ref[...]
        ext[pl.ds(0, 8), :] = jnp.where(first, 0.0, hp_ref[...])
        ext[pl.ds(8, tt), :] = pv
        ext[pl.ds(8 + tt, 8), :] = jnp.where(last, 0.0, hn_ref[...])
        prev = ext[pl.ds(7, tt), :]
        nxt = ext[pl.ds(9, tt), :]
        ps = pv + mup_ref[...] * (prev - pv) + mun_ref[...] * (nxt - pv)
        outs = _prep(ps, w2b_ref[...], w0c_ref[...], a2b_ref[...], a0c_ref[...], g2p_ref[...], kk_ref[...],
                     ka_ref[...], e2_ref[...])
        for o_ref, val in zip((r_o, v_o, kk_o, w_o, kd_o, b_o, g_o), outs):
            o_ref[...] = val

    hp, hn = _halo_specs(SHIFT_PAD, 8, tt, n_tok, 0)
    fulls = [mu_p, mu_n, w2b, w0c, a2b, a0c, g2p, k_k, k_a, e2]
    widths = (RW, RW, RW, 2 * RW, 2 * RW, 2 * RW, RW)
    return pl.pallas_call(
        body, name=name,
        out_shape=[jax.ShapeDtypeStruct((n_tok, c), F32) for c in widths],
        grid=(n_tok // tt,),
        in_specs=[pl.BlockSpec((tt, SHIFT_PAD), lambda i: (i, 0)), hp, hn]
        + [pl.BlockSpec(a.shape, lambda i: (0, 0)) for a in fulls],
        out_specs=[pl.BlockSpec((tt, c), lambda i: (i, 0)) for c in widths],
        scratch_shapes=[pltpu.VMEM((tt + 16, SHIFT_PAD), F32)],
        compiler_params=_cparams(("parallel",)),
    )(p, p, p, *fulls)


def _mix_prep_bwd(p, mu_p, mu_n, w2b, w0c, a2b, a0c, g2p, k_k, k_a, ct_terms, *, seq, tt, name):
    n_tok = p.shape[0]
    tps = seq // tt
    e2 = _head_ones()
    acc_shapes = [w2b.shape, w0c.shape, a2b.shape, a0c.shape, g2p.shape, k_k.shape, k_a.shape]
    cts = [a for terms in ct_terms for t in terms for a in (t if isinstance(t, tuple) else (t,))]

    def body(p_ref, hp_ref, hn_ref, mup_ref, mun_ref, w2b_ref, w0c_ref, a2b_ref, a0c_ref, g2p_ref, kk_ref, ka_ref,
             e2_ref, *rest):
        ct_refs, dps_o, acc_refs, ext = rest[:len(cts)], rest[len(cts)], rest[len(cts) + 1:-1], rest[-1]
        ct_it = iter(ct_refs)
        ct_vals = []
        for terms in ct_terms:
            total = None
            for t in terms:
                if isinstance(t, tuple):
                    val = jnp.concatenate([next(ct_it)[...] for _ in t], axis=1)
                else:
                    val = next(ct_it)[...]
                total = val if total is None else total + val
            ct_vals.append(total)
        i = pl.program_id(0)
        first = (i % tps) == 0
        last = (i % tps) == tps - 1
        pv = p_ref[...]
        ext[pl.ds(0, 8), :] = jnp.where(first, 0.0, hp_ref[...])
        ext[pl.ds(8, tt), :] = pv
        ext[pl.ds(8 + tt, 8), :] = jnp.where(last, 0.0, hn_ref[...])
        prev = ext[pl.ds(7, tt), :]
        nxt = ext[pl.ds(9, tt), :]
        ps = pv + mup_ref[...] * (prev - pv) + mun_ref[...] * (nxt - pv)
        e2v = e2_ref[...]
        _, vjp = jax.vjp(lambda *a: _prep(*a, e2v), ps, w2b_ref[...], w0c_ref[...], a2b_ref[...], a0c_ref[...],
                         g2p_ref[...], kk_ref[...], ka_ref[...])
        grads = vjp(tuple(ct_vals))
        dps_o[...] = grads[0]

        @pl.when(i == 0)
        def _():
            for r in acc_refs:
                r[...] = jnp.zeros_like(r)
        for r, val in zip(acc_refs, grads[1:]):
            r[...] += val

    hp, hn = _halo_specs(SHIFT_PAD, 8, tt, n_tok, 0)
    fulls = [mu_p, mu_n, w2b, w0c, a2b, a0c, g2p, k_k, k_a, e2]
    return pl.pallas_call(
        body, name=name,
        out_shape=[jax.ShapeDtypeStruct((n_tok, SHIFT_PAD), F32)] + [jax.ShapeDtypeStruct(s, F32) for s in acc_shapes],
        grid=(n_tok // tt,),
        in_specs=[pl.BlockSpec((tt, SHIFT_PAD), lambda i: (i, 0)), hp, hn]
        + [pl.BlockSpec(a.shape, lambda i: (0, 0)) for a in fulls]
        + [pl.BlockSpec((tt, c.shape[1]), lambda i: (i, 0)) for c in cts],
        out_specs=[pl.BlockSpec((tt, SHIFT_PAD), lambda i: (i, 0))] + [pl.BlockSpec(s, lambda i: (0, 0)) for s in acc_shapes],
        scratch_shapes=[pltpu.VMEM((tt + 16, SHIFT_PAD), F32)],
        compiler_params=_cparams(("arbitrary",)),
    )(p, p, p, *fulls, *cts)


def _shift_bwd(dps, p, mu_p, mu_n, *, seq, tt, name):
    n_tok = p.shape[0]
    tps = seq // tt

    def body(d_ref, dhp_ref, dhn_ref, p_ref, php_ref, phn_ref, mup_ref, mun_ref, dp_o, dmup_o, dmun_o, ext):
        i = pl.program_id(0)
        first = (i % tps) == 0
        last = (i % tps) == tps - 1
        mup, mun = mup_ref[...], mun_ref[...]
        rb = min(SHIFT_ROWS, tt)
        ext[pl.ds(0, 8), :] = jnp.where(first, 0.0, dhp_ref[...])
        ext[pl.ds(8, tt), :] = d_ref[...]
        ext[pl.ds(8 + tt, 8), :] = jnp.where(last, 0.0, dhn_ref[...])
        for r0 in range(0, tt, rb):
            dv = d_ref[pl.ds(r0, rb), :]
            dp_o[pl.ds(r0, rb), :] = (dv * (1.0 - mup - mun) + ext[pl.ds(r0 + 9, rb), :] * mup
                                      + ext[pl.ds(r0 + 7, rb), :] * mun).astype(dp_o.dtype)
        ext[pl.ds(0, 8), :] = jnp.where(first, 0.0, php_ref[...])
        ext[pl.ds(8, tt), :] = p_ref[...]
        ext[pl.ds(8 + tt, 8), :] = jnp.where(last, 0.0, phn_ref[...])

        @pl.when(i == 0)
        def _():
            dmup_o[...] = jnp.zeros_like(dmup_o)
            dmun_o[...] = jnp.zeros_like(dmun_o)
        sum_p = jnp.zeros_like(mup)
        sum_n = jnp.zeros_like(mun)
        for r0 in range(0, tt, rb):
            dv, pv = d_ref[pl.ds(r0, rb), :], p_ref[pl.ds(r0, rb), :]
            sum_p = sum_p + jnp.sum(dv * (ext[pl.ds(r0 + 7, rb), :] - pv), axis=0, keepdims=True)
            sum_n = sum_n + jnp.sum(dv * (ext[pl.ds(r0 + 9, rb), :] - pv), axis=0, keepdims=True)
        dmup_o[...] += sum_p
        dmun_o[...] += sum_n

    hp, hn = _halo_specs(SHIFT_PAD, 8, tt, n_tok, 0)
    tile = pl.BlockSpec((tt, SHIFT_PAD), lambda i: (i, 0))
    full = pl.BlockSpec((1, SHIFT_PAD), lambda i: (0, 0))
    return pl.pallas_call(
        body, name=name,
        out_shape=[jax.ShapeDtypeStruct((n_tok, SHIFT_PAD), BF16), jax.ShapeDtypeStruct((1, SHIFT_PAD), F32),
                   jax.ShapeDtypeStruct((1, SHIFT_PAD), F32)],
        grid=(n_tok // tt,),
        in_specs=[tile, hp, hn, tile, hp, hn, full, full],
        out_specs=[tile, full, full],
        scratch_shapes=[pltpu.VMEM((tt + 16, SHIFT_PAD), F32)],
        compiler_params=_cparams(("arbitrary",)),
    )(dps, dps, dps, p, p, p, mu_p, mu_n)


def _mix_post(y0, y1, r, v, kd, g, lnx_g, lnx_b, r_k, *, tt, name):
    e2 = _head_ones()
    return _rowcall(lambda i, *a: (_post(*a),), [y0, y1, r, v, kd, g], [lnx_g, lnx_b, r_k, e2], [(RW, BF16)], [],
                    tt=tt, name=name)[0]


def _mix_post_bwd(y0, y1, r, v, kd, g, lnx_g, lnx_b, r_k, dout, *, tt, name):
    e2 = _head_ones()

    def fn(i, y0v, y1v, rv, vv, kdv, gv, dov, lg, lb, rk, e2v):
        _, vjp = jax.vjp(lambda *a: _post(*a, e2v), y0v, y1v, rv, vv, kdv, gv, lg, lb, rk)
        gr = vjp(dov.astype(F32))
        return gr[0], gr[2], gr[3], gr[4], gr[5], gr[6], gr[7], gr[8]
    return _rowcall(fn, [y0, y1, r, v, kd, g, dout], [lnx_g, lnx_b, r_k, e2],
                    [(RW, F32), (RW, F32), (RW, F32), (2 * RW, F32), (RW, F32)], [(1, RW), (1, RW), (1, RW)],
                    tt=tt, name=name)


def _conv_fwd(p, dw, db, ln_g, ln_b, *, seq, tt, name):
    n_tok = p.shape[0]
    tps = seq // tt

    def glu(x, gate):
        return x * _sigmoid(gate)

    def body(u_ref, g_ref, uhp, ghp, uhn, ghn, dw_ref, db_ref, lg_ref, lb_ref, yc_o, y_o, ext):
        i = pl.program_id(0)
        first = (i % tps) == 0
        last = (i % tps) == tps - 1
        ext[pl.ds(0, 16), :] = jnp.where(first, 0.0, glu(uhp[...], ghp[...]))
        ext[pl.ds(16, tt), :] = glu(u_ref[...], g_ref[...])
        ext[pl.ds(16 + tt, 16), :] = jnp.where(last, 0.0, glu(uhn[...], ghn[...]))
        taps = [dw_ref[pl.ds(k, 1), :] for k in range(CONV_K)]
        for r0 in range(0, tt, CONV_ROWS):
            acc = jnp.zeros((CONV_ROWS, CW), F32) + db_ref[...]
            for k in range(CONV_K):
                acc = acc + ext[pl.ds(r0 + k + 1, CONV_ROWS), :] * taps[k]
            yc_o[pl.ds(r0, CONV_ROWS), :] = acc
        y_o[...] = _conv_post(yc_o[...], lg_ref[...], lb_ref[...]).astype(y_o.dtype)

    uhp_s, uhn_s = _halo_specs(CW, 16, tt, n_tok, 4)
    ghp_s, ghn_s = _halo_specs(CW, 16, tt, n_tok, 5)
    fulls = [dw, db, ln_g, ln_b]
    return pl.pallas_call(
        body, name=name,
        out_shape=[jax.ShapeDtypeStruct((n_tok, CW), F32), jax.ShapeDtypeStruct((n_tok, CW), BF16)],
        grid=(n_tok // tt,),
        in_specs=[pl.BlockSpec((tt, CW), lambda i: (i, 4)), pl.BlockSpec((tt, CW), lambda i: (i, 5)),
                  uhp_s, ghp_s, uhn_s, ghn_s] + [pl.BlockSpec(a.shape, lambda i: (0, 0)) for a in fulls],
        out_specs=[pl.BlockSpec((tt, CW), lambda i: (i, 0)), pl.BlockSpec((tt, CW), lambda i: (i, 0))],
        scratch_shapes=[pltpu.VMEM((tt + 32, CW), F32)],
        compiler_params=_cparams(("parallel",)),
    )(p, p, p, p, p, p, *fulls)


def _conv_post_bwd(yc, dy, ln_g, ln_b, *, tt, name):
    def fn(i, ycv, dyv, lg, lb):
        _, vjp = jax.vjp(_conv_post, ycv, lg, lb)
        dyc, dg, dbb = vjp(dyv.astype(F32))
        return dyc, dg, dbb, jnp.sum(dyc, axis=0, keepdims=True)
    return _rowcall(fn, [yc, dy], [ln_g, ln_b], [(CW, F32)], [(1, CW), (1, CW), (1, CW)], tt=tt, name=name)


def _conv_bwd(dyc, p, dw, *, seq, tt, name):
    n_tok = p.shape[0]
    tps = seq // tt

    def body(d_ref, dhp, dhn, u_ref, g_ref, uhp, ghp, uhn, ghn, dw_ref, dp_o, ddw_o, ext):
        i = pl.program_id(0)
        first = (i % tps) == 0
        last = (i % tps) == tps - 1
        dv = d_ref[...]
        ext[pl.ds(0, 16), :] = jnp.where(first, 0.0, dhp[...])
        ext[pl.ds(16, tt), :] = dv
        ext[pl.ds(16 + tt, 16), :] = jnp.where(last, 0.0, dhn[...])
        taps = [dw_ref[pl.ds(k, 1), :] for k in range(CONV_K)]
        for r0 in range(0, tt, CONV_ROWS):
            du = jnp.zeros((CONV_ROWS, CW), F32)
            for k in range(CONV_K):
                du = du + ext[pl.ds(r0 + 31 - k, CONV_ROWS), :] * taps[k]
            rows = pl.ds(r0, CONV_ROWS)
            sg_r = _sigmoid(g_ref[rows, :])
            dp_o[rows, 0:CW] = (du * sg_r).astype(dp_o.dtype)
            dp_o[rows, CW:2 * CW] = (du * u_ref[rows, :] * sg_r * (1.0 - sg_r)).astype(dp_o.dtype)
        uv, gv = u_ref[...], g_ref[...]
        sg = _sigmoid(gv)
        ext[pl.ds(0, 16), :] = jnp.where(first, 0.0, uhp[...] * _sigmoid(ghp[...]))
        ext[pl.ds(16, tt), :] = uv * sg
        ext[pl.ds(16 + tt, 16), :] = jnp.where(last, 0.0, uhn[...] * _sigmoid(ghn[...]))

        @pl.when(i == 0)
        def _():
            ddw_o[...] = jnp.zeros_like(ddw_o)
        for k in range(CONV_K):
            ddw_o[pl.ds(k, 1), :] += jnp.sum(dv * ext[pl.ds(k + 1, tt), :], axis=0, keepdims=True)

    dhp_s, dhn_s = _halo_specs(CW, 16, tt, n_tok, 0)
    uhp_s, uhn_s = _halo_specs(CW, 16, tt, n_tok, 4)
    ghp_s, ghn_s = _halo_specs(CW, 16, tt, n_tok, 5)
    return pl.pallas_call(
        body, name=name,
        out_shape=[jax.ShapeDtypeStruct((n_tok, 2 * CW), BF16), jax.ShapeDtypeStruct((32, CW), F32)],
        grid=(n_tok // tt,),
        in_specs=[pl.BlockSpec((tt, CW), lambda i: (i, 0)), dhp_s, dhn_s,
                  pl.BlockSpec((tt, CW), lambda i: (i, 4)), pl.BlockSpec((tt, CW), lambda i: (i, 5)),
                  uhp_s, ghp_s, uhn_s, ghn_s, pl.BlockSpec(dw.shape, lambda i: (0, 0))],
        out_specs=[pl.BlockSpec((tt, 2 * CW), lambda i: (i, 0)), pl.BlockSpec((32, CW), lambda i: (0, 0))],
        scratch_shapes=[pltpu.VMEM((tt + 32, CW), F32)],
        compiler_params=_cparams(("arbitrary",)),
    )(dyc, dyc, dyc, p, p, p, p, p, p, dw)


CHUNK = 64
_MM_DIMS = {"nn": (((2,), (1,)), ((0,), (0,))), "nt": (((2,), (2,)), ((0,), (0,))), "tn": (((1,), (1,)), ((0,), (0,)))}


def _mm16_raw(a, b, mode, fine):
    dot = lambda x, y: lax.dot_general(x, y, _MM_DIMS[mode], preferred_element_type=F32)
    if not fine:
        return dot(a.astype(BF16), b.astype(BF16))
    ah, (bh, bl) = a.astype(BF16), _split16(b)
    return dot(ah, bh) + dot(ah, bl)


@functools.partial(jax.custom_vjp, nondiff_argnums=(2, 3))
def _mm16(a, b, mode, fine=False):
    return _mm16_raw(a, b, mode, fine)


def _mm16_fwd(a, b, mode, fine):
    return _mm16_raw(a, b, mode, fine), (a, b)


def _mm16_bwd(mode, fine, res, g):
    a, b = res
    if mode == "nn":
        return _mm16_raw(g, b, "nt", fine), _mm16_raw(a, g, "tn", fine)
    if mode == "nt":
        return _mm16_raw(g, b, "nn", fine), _mm16_raw(g, a, "tn", fine)
    return _mm16_raw(b, g, "nt", fine), _mm16_raw(a, g, "nn", fine)


_mm16.defvjp(_mm16_fwd, _mm16_bwd)


def _tri_sum_raw(x, tri, mode):
    hi = x.astype(BF16)
    r1 = x - hi.astype(F32)
    mid = r1.astype(BF16)
    lo = (r1 - mid.astype(F32)).astype(BF16)
    dot = lambda p: lax.dot_general(tri, p, _MM_DIMS[mode], preferred_element_type=F32)
    return dot(hi) + dot(mid) + dot(lo)


@jax.custom_vjp
def _tri_sum(x, tri):
    return _tri_sum_raw(x, tri, "nn")


def _tri_sum_fwd(x, tri):
    return _tri_sum_raw(x, tri, "nn"), tri


def _tri_sum_bwd(tri, g):
    return _tri_sum_raw(g, tri, "tn"), jnp.zeros_like(tri)


_tri_sum.defvjp(_tri_sum_fwd, _tri_sum_bwd)


PAIR = 2 * HEAD
N_PAIRS = RW // PAIR


def _pair_rows(x):
    first = lax.broadcasted_iota(jnp.int32, x.shape, 2) < HEAD
    return jnp.concatenate([jnp.where(first, x, 0.0), jnp.where(first, 0.0, x)], axis=1)


def _chunk_step_pairs(s0, r, lw, k, v, kk, b, tri, rev):
    g, n, _ = r.shape
    row = lax.broadcasted_iota(jnp.int32, (g, n, PAIR), 1)
    col = lax.broadcasted_iota(jnp.int32, (g, n, PAIR), 2) % HEAD
    if rev:
        row, col = col, row
    diag = (lax.broadcasted_iota(jnp.int32, (g, PAIR, PAIR), 1) // HEAD
            == lax.broadcasted_iota(jnp.int32, (g, PAIR, PAIR), 2) // HEAD)
    cum = _tri_sum(lw, tri)
    up, down = jnp.exp(cum), jnp.exp(-cum)
    at, rt = -kk * jnp.exp(cum - lw), r * up
    kt, bt = k * down, b * down
    bt_rows, kt_rows = _pair_rows(bt), _pair_rows(kt)
    a_ab = jnp.where(col < row, _mm16(at, bt_rows, "nt"), 0.0)
    a_ak = jnp.where(col < row, _mm16(at, kt_rows, "nt"), 0.0)
    a_rb = jnp.where(col <= row, _mm16(rt, bt_rows, "nt"), 0.0)
    a_rk = jnp.where(col <= row, _mm16(rt, kt_rows, "nt", True), 0.0)
    v_rows = _pair_rows(v)
    u = _mm16(at, s0, "nt") + _mm16(a_ak, v_rows, "nn")
    power = a_ab
    steps = n.bit_length() - 1
    for it in range(steps):
        u = u + _mm16(power, _pair_rows(u), "nn")
        if it + 1 < steps:
            power = _mm16(power, _pair_rows(power), "nn")
    y = _mm16(rt, s0, "nt") + _mm16(a_rk, v_rows, "nn") + _mm16(a_rb, _pair_rows(u), "nn")
    grown = s0 + jnp.where(diag, _mm16(v, kt, "tn") + _mm16(u, bt, "tn"), 0.0)
    return y, grown * jnp.exp(jnp.sum(lw, axis=1, keepdims=True))


SCAN_SEQS = 4


def _tri_ones(rev, nseq):
    shape = (nseq * N_PAIRS, CHUNK, CHUNK)
    row, col = lax.broadcasted_iota(jnp.int32, shape, 1), lax.broadcasted_iota(jnp.int32, shape, 2)
    return ((col >= row) if rev else (col <= row)).astype(BF16)


def _split_heads(ref):
    return jnp.stack([ref[q, :, pl.ds(h * PAIR, PAIR)] for q in range(ref.shape[0]) for h in range(N_PAIRS)])


def _merge_heads(ref, val):
    for q in range(ref.shape[0]):
        for h in range(N_PAIRS):
            ref[q, :, pl.ds(h * PAIR, PAIR)] = val[q * N_PAIRS + h]


def _chunk_specs(nseq, nc, rev, dcol):
    chunk = (lambda c: nc - 1 - c) if rev else (lambda c: c)
    shared = pl.BlockSpec((nseq, CHUNK, RW), lambda s, c: (s, chunk(c), 0))
    own = pl.BlockSpec((nseq, CHUNK, RW), lambda s, c: (s, chunk(c), dcol))
    return shared, own


def _wkv_chunk_fwd(r, lw, k, v, kk, b, *, rev, name):
    bsz, seq, _ = r.shape
    nc = seq // CHUNK
    nseq = SCAN_SEQS if bsz % SCAN_SEQS == 0 else 1
    shared, own = _chunk_specs(nseq, nc, rev, int(rev))

    def body(r_ref, lw_ref, k_ref, v_ref, kk_ref, b_ref, tri_ref, y_o, s0_o, s_ref):
        @pl.when(pl.program_id(1) == 0)
        def _():
            s_ref[...] = jnp.zeros_like(s_ref)

        s0 = s_ref[...]
        s0_o[:, 0] = s0.reshape(nseq, N_PAIRS, PAIR, PAIR)
        y, s_ref[...] = _chunk_step_pairs(s0, *[_split_heads(x) for x in (r_ref, lw_ref, k_ref, v_ref, kk_ref, b_ref)],
                                    tri_ref[...], rev)
        _merge_heads(y_o, y)

    return pl.pallas_call(
        body, name=name,
        out_shape=[jax.ShapeDtypeStruct((bsz, seq, RW), F32), jax.ShapeDtypeStruct((bsz, nc, N_PAIRS, PAIR, PAIR), F32)],
        grid=(bsz // nseq, nc),
        in_specs=[shared, own, own, shared, shared, own,
                  pl.BlockSpec((nseq * N_PAIRS, CHUNK, CHUNK), lambda s, c: (0, 0, 0))],
        out_specs=[shared, pl.BlockSpec((nseq, 1, N_PAIRS, PAIR, PAIR), lambda s, c: (s, c, 0, 0, 0))],
        scratch_shapes=[pltpu.VMEM((nseq * N_PAIRS, PAIR, PAIR), F32)],
        compiler_params=_cparams(("parallel", "arbitrary")),
    )(r, lw, k, v, kk, b, _tri_ones(rev, nseq))


def _wkv_chunk_bwd(r, lw, k, v, kk, b, dy, s0, *, rev, name):
    bsz, seq, _ = r.shape
    nc = seq // CHUNK
    nseq = SCAN_SEQS if bsz % SCAN_SEQS == 0 else 1
    shared, own = _chunk_specs(nseq, nc, not rev, int(rev))

    def body(r_ref, lw_ref, k_ref, v_ref, kk_ref, b_ref, dy_ref, s0_ref, tri_ref, *rest):
        outs, ds_ref = rest[:-1], rest[-1]

        @pl.when(pl.program_id(1) == 0)
        def _():
            ds_ref[...] = jnp.zeros_like(ds_ref)

        triv = tri_ref[...]
        _, vjp = jax.vjp(lambda *a: _chunk_step_pairs(*a, triv, rev), s0_ref[:, 0].reshape(nseq * N_PAIRS, PAIR, PAIR),
                         *[_split_heads(x) for x in (r_ref, lw_ref, k_ref, v_ref, kk_ref, b_ref)])
        grads = vjp((_split_heads(dy_ref), ds_ref[...]))
        ds_ref[...] = grads[0]
        for o, gval in zip(outs, grads[1:]):
            _merge_heads(o, gval)

    return pl.pallas_call(
        body, name=name,
        out_shape=[jax.ShapeDtypeStruct((bsz, seq, RW), F32)] * 6,
        grid=(bsz // nseq, nc),
        in_specs=[shared, own, own, shared, shared, own, shared,
                  pl.BlockSpec((nseq, 1, N_PAIRS, PAIR, PAIR), lambda s, c: (s, nc - 1 - c, 0, 0, 0)),
                  pl.BlockSpec((nseq * N_PAIRS, CHUNK, CHUNK), lambda s, c: (0, 0, 0))],
        out_specs=[shared] * 6,
        scratch_shapes=[pltpu.VMEM((nseq * N_PAIRS, PAIR, PAIR), F32)],
        compiler_params=_cparams(("parallel", "arbitrary")),
    )(r, lw, k, v, kk, b, dy, s0, _tri_ones(rev, nseq))


def _block_diag2(w):
    z = jnp.zeros_like(w[0])
    return jnp.concatenate([jnp.concatenate([w[0], z], axis=1), jnp.concatenate([z, w[1]], axis=1)], axis=0)


def _pad_in_cols(a):
    z = jnp.zeros(a.shape[:-1] + (SHIFT_PAD - SHIFT_COLS,), a.dtype)
    return jnp.concatenate([a[..., :SHIFT_COLS], z, a[..., SHIFT_COLS:]], axis=-1)


def _follow(small, token):
    return small if token is None else small + token[0:1, 0:1]


def _local_step(x, target, wts, *, tt, start_token=None, more_weights=None, grads_ready=None, small_ready=None):
    bsz, seq, _ = x.shape
    n_tok = bsz * seq
    row = lambda a: a.reshape(1, -1).astype(F32)
    x0 = x.reshape(n_tok, D_MODEL)
    tgt = target.reshape(n_tok, D_MODEL)
    ln = {k: row(wts[k]) for k in ("ln1_g", "ln1_b", "ln2_g", "ln2_b", "ln3_g", "ln3_b")}
    if grads_ready is None:
        grads_ready = lambda names, slabs: None

    w1i = wts["ffn1_w_in"]
    h1, act1 = _ffn_in(x0, w1i, tm=TM_FFN, after=start_token, name="ffn1_in")
    if more_weights is not None:
        wts = {**wts, **more_weights("ffn1_out", act1)}
    w1o = wts["ffn1_w_out"]
    z1, x1, x1b = _mm_ln([act1], w1o, x0, ln["ln1_g"], ln["ln1_b"], 0.5, tm=TM_LN, name="ffn1_out_ln1")
    if more_weights is not None:
        wts = {**wts, **more_weights("mix", x1b)}
    win = _pad_in_cols(wts["w_in"])
    zpad = jnp.zeros((1, SHIFT_PAD - SHIFT_COLS), F32)
    mu_p = jnp.concatenate([row(wts["mu_prev"]), zpad], axis=1)
    mu_n = jnp.concatenate([row(wts["mu_next"]), zpad], axis=1)
    w2b, a2b = _block_diag2(wts["w2"]), _block_diag2(wts["a2"])
    w0c, a0c = row(wts["w0"]), row(wts["a0"])
    g2p = jnp.concatenate([wts["g2"], jnp.zeros((GATE_PAD - GATE_LORA, RW), F32)], axis=0)
    k_k, k_a, r_k = row(wts["k_k"]), row(wts["k_a"]), row(wts["r_k"])
    lnx_g, lnx_b = row(wts["lnx_g"]), row(wts["lnx_b"])
    cdw, cb, clg, clb = wts["conv_dw"], row(wts["conv_b"]), row(wts["conv_ln_g"]), row(wts["conv_ln_b"])
    small = (mu_p, mu_n, w2b, w0c, a2b, a0c, g2p, k_k, k_a)
    seq3 = lambda a: a.reshape(bsz, seq, a.shape[-1])
    flat = lambda a: a.reshape(n_tok, a.shape[-1])

    p = _matmul(x1b, win, name="proj_in")
    r, v, kk, w, kd, b, g = _mix_prep(p, *small, seq=seq, tt=tt, name="mix_prep")
    scan_in = [seq3(a) for a in (r, w, kd, v, kk, b)]
    y0, s_chunks0 = _wkv_chunk_fwd(*scan_in, rev=False, name="wkv_fwd_dir0")
    y1, s_chunks1 = _wkv_chunk_fwd(*scan_in, rev=True, name="wkv_fwd_dir1")
    y0, y1 = flat(y0), flat(y1)
    yr = _mix_post(y0, y1, r, v, kd, g, lnx_g, lnx_b, r_k, tt=tt, name="mix_post")
    yc, yv = _conv_fwd(p, cdw, cb, clg, clb, seq=seq, tt=tt, name="conv_fwd")
    if more_weights is not None:
        wts = {**wts, **more_weights("out", yr)}
    wout, w2i, w2o = wts["w_out"], wts["ffn2_w_in"], wts["ffn2_w_out"]
    z2, x2, x2b = _mm_ln([yr, yv], wout, x1, ln["ln2_g"], ln["ln2_b"], 1.0, tm=TM_LN, name="proj_out_ln2")
    h2, act2 = _ffn_in(x2b, w2i, tm=TM_FFN, name="ffn2_in")

    gr = {}
    slab_rows = lambda a: a.reshape((N_CHIPS, a.shape[0] // N_CHIPS) + a.shape[1:])
    dw_kw = dict(ta=True, out_dtype=BF16)
    dz3, gr["ln3_g"], gr["ln3_b"], loss_part = _mm_ln_loss(act2, w2o, x2, ln["ln3_g"], ln["ln3_b"], tgt, 0.5, tm=TM_LN,
                                                           name="ffn2_out_ln3_loss")
    dh2 = _ffn_out_bwd(dz3, w2o, h2, tm=TM_FFN, name="ffn2_out_dx")
    gr["ffn2_w_out"] = slab_rows(_matmul(act2, dz3, scale=0.5, tm=D_FF // 2, name="ffn2_out_dw", **dw_kw))
    dz2, gr["ln2_g"], gr["ln2_b"] = _mm_nt_res([dh2], w2i, dz3, ln=(z2, ln["ln2_g"], ln["ln2_b"]), tm=TM_FFN,
                                               name="ffn2_in_dx_ln2")
    gr["ffn2_w_in"] = _matmul(x2b, dh2, col_slabs=True, tn=2 * D_FF // N_CHIPS, name="ffn2_in_dw", **dw_kw)
    dmix = _matmul(dz2, wout, tb=True, name="proj_out_dx")
    gr["w_out"] = slab_rows(jnp.concatenate([_matmul(yr, dz2, name="proj_out_dw_rwkv", **dw_kw),
                                             _matmul(yv, dz2, name="proj_out_dw_conv", **dw_kw)], axis=0))
    tok = grads_ready(("ffn2_w_out", "ffn2_w_in", "w_out"), [gr["ffn2_w_out"], gr["ffn2_w_in"], gr["w_out"]])
    dyr, dyv = (dmix, RW, 0), (dmix, RW, 1)
    dy, dr_p, dv_p, dkd_p, dg, gr["lnx_g"], gr["lnx_b"], gr["r_k"] = _mix_post_bwd(
        y0, y1, r, v, kd, g, _follow(lnx_g, tok), lnx_b, r_k, dyr, tt=tt, name="mix_post_bwd")
    dr0, dw0, dkd0, dv0, dk0, db0 = [flat(a) for a in _wkv_chunk_bwd(*scan_in, seq3(dy), s_chunks0, rev=False,
                                                                      name="wkv_bwd_dir0")]
    dr1, dw1, dkd1, dv1, dk1, db1 = [flat(a) for a in _wkv_chunk_bwd(*scan_in, seq3(dy), s_chunks1, rev=True,
                                                                      name="wkv_bwd_dir1")]
    ct_terms = [[dr_p, dr0, dr1], [dv_p, dv0, dv1], [dk0, dk1], [(dw0, dw1)], [dkd_p, (dkd0, dkd1)], [(db0, db1)], [dg]]
    dyc, gr["conv_ln_g"], gr["conv_ln_b"], gr["conv_b"] = _conv_post_bwd(yc, dyv, clg, clb, tt=tt, name="conv_post_bwd")
    dpc, ddw = _conv_bwd(dyc, p, cdw, seq=seq, tt=tt, name="conv_bwd")
    gr["conv_dw"] = ddw[:CONV_K]
    dps, dw2b, dw0c, da2b, da0c, dg2p, gr["k_k"], gr["k_a"] = _mix_prep_bwd(
        p, *small, ct_terms, seq=seq, tt=tt, name="mix_prep_bwd")
    gr["w2"] = jnp.stack([dw2b[:LORA, :RW], dw2b[LORA:, RW:]])
    gr["a2"] = jnp.stack([da2b[:LORA, :RW], da2b[LORA:, RW:]])
    gr["w0"], gr["a0"], gr["g2"] = dw0c.reshape(2, RW), da0c.reshape(2, RW), dg2p[:GATE_LORA]
    dpsh, dmu_p, dmu_n = _shift_bwd(dps, p, mu_p, mu_n, seq=seq, tt=tt, name="shift_bwd")
    gr["mu_prev"], gr["mu_next"] = dmu_p[:, :SHIFT_COLS], dmu_n[:, :SHIFT_COLS]
    dwin = jnp.concatenate([_matmul(x1b, dpsh, name="proj_in_dw_shift", **dw_kw)[:, :SHIFT_COLS],
                            _matmul(x1b, dpc, name="proj_in_dw_conv", **dw_kw)], axis=1)
    gr["w_in"] = jnp.moveaxis(dwin.reshape(D_MODEL, N_CHIPS, IN_COLS // N_CHIPS), 1, 0)
    tok = grads_ready(("w_in",), [gr["w_in"]])
    dz1, gr["ln1_g"], gr["ln1_b"] = _mm_nt_res([dpsh, dpc], win, dz2, ln=(z1, ln["ln1_g"], ln["ln1_b"]), tm=TM_FFN,
                                               after=tok, name="proj_in_dx_ln1")
    gr["loss"] = loss_part
    tok = small_ready(gr, loss_part) if small_ready is not None else None
    dh1 = _ffn_out_bwd(dz1, w1o, h1, tm=TM_FFN, after=tok, name="ffn1_out_dx")
    gr["ffn1_w_out"] = slab_rows(_matmul(act1, dz1, scale=0.5, tm=D_FF // 2, name="ffn1_out_dw", **dw_kw))
    tok = grads_ready(("ffn1_w_out",), [gr["ffn1_w_out"]])
    gr["ffn1_w_in"] = _matmul(x0, dh1, col_slabs=True, tn=2 * D_FF // N_CHIPS, after=tok, name="ffn1_in_dw", **dw_kw)
    tok = grads_ready(("ffn1_w_in",), [gr["ffn1_w_in"]])
    dx0 = _mm_nt_res([dh1], w1i, dz1, tm=TM_FFN, after=tok, name="ffn1_in_dx")
    return dx0.reshape(bsz, seq, D_MODEL), gr


def _mesh_pos():
    return lax.axis_index("x"), lax.axis_index("y"), lax.axis_index("c")


def _other_chips(x, y):
    return [(1 - x, y), (x, 1 - y), (1 - x, 1 - y)]


def _gather_chips(shards, *, name):
    n = len(shards)
    halves = [s.shape[0] // 2 for s in shards]
    assert all(2 * h == s.shape[0] for h, s in zip(halves, shards))

    def body(*refs):
        ins, outs = refs[:n], refs[n:2 * n]
        send_sems, recv_sems, fwd_send_sems, fwd_recv_sems, loc_sems = refs[2 * n:]
        x, y, c = _mesh_pos()
        q = 2 * x + y
        peers = _other_chips(x, y)
        local = [pltpu.make_async_copy(ins[a], outs[a].at[q], loc_sems.at[a]) for a in range(n)]
        for cp in local:
            cp.start()

        def half(a, chip, core):
            return outs[a].at[chip, pl.ds(core * halves[a], halves[a])]

        sends = [pltpu.make_async_remote_copy(ins[a].at[pl.ds(c * halves[a], halves[a])], half(a, q, c),
                                              send_sems.at[a, k], recv_sems.at[a, k],
                                              device_id=(px, py, c), device_id_type=MESH)
                 for a in range(n) for k, (px, py) in enumerate(peers)]
        for cp in sends:
            cp.start()
        passed = []
        for a in range(n):
            for k, (px, py) in enumerate(peers):
                mine = half(a, 2 * px + py, c)
                pltpu.make_async_remote_copy(mine, mine, send_sems.at[a, k], recv_sems.at[a, k],
                                             device_id=(px, py, c), device_id_type=MESH).wait_recv()
                cp = pltpu.make_async_remote_copy(mine, mine, fwd_send_sems.at[a, k], fwd_recv_sems.at[a, k],
                                                  device_id=(x, y, 1 - c), device_id_type=MESH)
                cp.start()
                passed.append(cp)
        for a in range(n):
            for k, (px, py) in enumerate(peers):
                theirs = half(a, 2 * px + py, 1 - c)
                pltpu.make_async_remote_copy(theirs, theirs, fwd_send_sems.at[a, k], fwd_recv_sems.at[a, k],
                                             device_id=(x, y, 1 - c), device_id_type=MESH).wait_recv()
        for cp in sends + passed:
            cp.wait_send()
        for cp in local:
            cp.wait()

    any_spec = pl.BlockSpec(memory_space=pl.ANY)
    return pl.pallas_call(
        body, name=name,
        out_shape=[jax.ShapeDtypeStruct((N_CHIPS,) + s.shape, s.dtype) for s in shards],
        in_specs=[any_spec] * n, out_specs=[any_spec] * n,
        scratch_shapes=[pltpu.SemaphoreType.DMA((n, 3))] * 4 + [pltpu.SemaphoreType.DMA((n,))],
        compiler_params=pltpu.CompilerParams(has_side_effects=True),
    )(*shards)


HBM_SPEC = pl.BlockSpec(memory_space=pltpu.HBM)
SEM_SPEC = pl.BlockSpec(memory_space=pltpu.SEMAPHORE)
ANY_SPEC = pl.BlockSpec(memory_space=pl.ANY)
SIDE_EFFECT = pltpu.SideEffectType.DATAFLOW_SIDE_EFFECTING


def _chip_copies(src_refs, land_refs, send_sems, recv_sems, scatter, arriving=False):
    x, y, c = _mesh_pos()
    cps = []
    for a, (src, land) in enumerate(zip(src_refs, land_refs)):
        for k, (px, py) in enumerate(_other_chips(x, y)):
            slot = k if scatter else (2 * px + py if arriving else 2 * x + y)
            cps.append(pltpu.make_async_remote_copy(src.at[2 * px + py] if scatter else src, land.at[slot],
                                                    send_sems.at[3 * a + k], recv_sems.at[3 * a + k],
                                                    device_id=(px, py, c), device_id_type=MESH))
    return cps


def _exchange_start(srcs, *, scatter, after, name):
    n = len(srcs)
    lands = [lax.empty((3,) + s.shape[1:] if scatter else (N_CHIPS,) + s.shape, s.dtype) for s in srcs]

    def body(*refs):
        src_refs, land_refs = refs[:n], refs[n:2 * n]
        send_sems, recv_sems = refs[2 * n + 1:2 * n + 3]
        token = refs[-1]
        for cp in _chip_copies(src_refs, land_refs, send_sems, recv_sems, scatter):
            cp.start()
        token[...] = jnp.zeros_like(token)

    hbm = lambda a: pltpu.with_memory_space_constraint(a, pltpu.HBM)
    outs = pl.pallas_call(
        body, name=name,
        out_shape=(pltpu.SemaphoreType.DMA((3 * n,)), pltpu.SemaphoreType.DMA((3 * n,)),
                   *[pltpu.HBM(a.shape, a.dtype) for a in srcs + lands], jax.ShapeDtypeStruct((8, LANES), F32)),
        in_specs=[HBM_SPEC] * (2 * n) + [ANY_SPEC],
        out_specs=(SEM_SPEC, SEM_SPEC, *[HBM_SPEC] * (2 * n), pl.BlockSpec(memory_space=pltpu.VMEM)),
        input_output_aliases={i: 2 + i for i in range(2 * n)},
        compiler_params=pltpu.CompilerParams(has_side_effects=SIDE_EFFECT),
    )(*[hbm(a) for a in srcs + lands], after)
    return outs[0], outs[1], list(outs[2:2 + n]), list(outs[2 + n:2 + 2 * n]), outs[-1]


def _exchange_wait(started, *, scatter, after, name):
    send_sems, recv_sems, srcs, lands, _ = started
    n = len(srcs)

    def body(*refs):
        src_refs, land_refs = refs[:n], refs[n:2 * n]
        send_s, recv_s = refs[2 * n:2 * n + 2]
        for cp in _chip_copies(src_refs, land_refs, send_s, recv_s, scatter, arriving=True):
            cp.wait_send()
            cp.wait_recv()

    outs = pl.pallas_call(
        body, name=name,
        out_shape=tuple(pltpu.HBM(a.shape, a.dtype) for a in srcs + lands),
        in_specs=[HBM_SPEC] * (2 * n) + [SEM_SPEC, SEM_SPEC, ANY_SPEC],
        out_specs=tuple([HBM_SPEC] * (2 * n)),
        input_output_aliases={i: i for i in range(2 * n)},
        compiler_params=pltpu.CompilerParams(has_side_effects=SIDE_EFFECT),
    )(*srcs, *lands, send_sems, recv_sems, after)
    return list(outs[:n]), list(outs[n:])


def _by_chip(own, land):
    xi, yi, _ = _mesh_pos()
    return lax.dynamic_update_index_in_dim(land, own, 2 * xi + yi, 0)


def _swap_sibling(arrs, *, name):
    n = len(arrs)

    def body(*refs):
        ins, outs = refs[:n], refs[n:2 * n]
        send_sems, recv_sems = refs[2 * n:]
        x, y, c = _mesh_pos()
        cps = [pltpu.make_async_remote_copy(ins[a], outs[a], send_sems.at[a], recv_sems.at[a],
                                            device_id=(x, y, 1 - c), device_id_type=MESH) for a in range(n)]
        for cp in cps:
            cp.start()
        for cp in cps:
            cp.wait_recv()
        for cp in cps:
            cp.wait_send()

    any_spec = pl.BlockSpec(memory_space=pl.ANY)
    return pl.pallas_call(
        body, name=name,
        out_shape=[jax.ShapeDtypeStruct(s.shape, s.dtype) for s in arrs],
        in_specs=[any_spec] * n, out_specs=[any_spec] * n,
        scratch_shapes=[pltpu.SemaphoreType.DMA((n,)), pltpu.SemaphoreType.DMA((n,))],
        compiler_params=pltpu.CompilerParams(has_side_effects=True),
    )(*arrs)


def _device_copies(v_ref, land_ref, send_sems, recv_sems, arriving=False):
    x, y, c = _mesh_pos()
    me = 4 * x + 2 * y + c
    cps = []
    for m in range(1, 8):
        px, py, pc = (x + ((m >> 2) & 1)) % 2, (y + ((m >> 1) & 1)) % 2, (c + (m & 1)) % 2
        slot = 4 * px + 2 * py + pc if arriving else me
        cps.append(pltpu.make_async_remote_copy(v_ref, land_ref.at[slot], send_sems.at[m - 1], recv_sems.at[m - 1],
                                                device_id=(px, py, pc), device_id_type=MESH))
    return cps


def _allsum_start(vec, *, after, name):
    land = lax.empty((8,) + vec.shape, F32)

    def body(v_ref, land_ref, _after, send_sems, recv_sems, v_thru, land_thru, token):
        for cp in _device_copies(v_ref, land_ref, send_sems, recv_sems):
            cp.start()
        token[...] = jnp.zeros_like(token)

    hbm = lambda a: pltpu.with_memory_space_constraint(a, pltpu.HBM)
    return pl.pallas_call(
        body, name=name,
        out_shape=(pltpu.SemaphoreType.DMA((7,)), pltpu.SemaphoreType.DMA((7,)), pltpu.HBM(vec.shape, F32),
                   pltpu.HBM(land.shape, F32), jax.ShapeDtypeStruct((8, LANES), F32)),
        in_specs=[HBM_SPEC, HBM_SPEC, ANY_SPEC],
        out_specs=(SEM_SPEC, SEM_SPEC, HBM_SPEC, HBM_SPEC, pl.BlockSpec(memory_space=pltpu.VMEM)),
        input_output_aliases={0: 2, 1: 3},
        compiler_params=pltpu.CompilerParams(has_side_effects=SIDE_EFFECT),
    )(hbm(vec), hbm(land), after)


def _allsum_wait(started, *, after, name):
    send_sems, recv_sems, vec, land, _ = started

    def body(v_ref, land_ref, send_s, recv_s, _after, v_dead, got):
        for cp in _device_copies(v_ref, land_ref, send_s, recv_s, arriving=True):
            cp.wait_send()
            cp.wait_recv()

    vec, land = pl.pallas_call(
        body, name=name,
        out_shape=(pltpu.HBM(vec.shape, F32), pltpu.HBM(land.shape, F32)),
        in_specs=[HBM_SPEC, HBM_SPEC, SEM_SPEC, SEM_SPEC, ANY_SPEC],
        out_specs=(HBM_SPEC, HBM_SPEC),
        input_output_aliases={0: 0, 1: 1},
        compiler_params=pltpu.CompilerParams(has_side_effects=SIDE_EFFECT),
    )(vec, land, send_sems, recv_sems, after)
    xi, yi, ci = _mesh_pos()
    every = lax.dynamic_update_index_in_dim(land, vec, 4 * xi + 2 * yi + ci, 0)

    def add(e_ref, o_ref):
        acc = e_ref[0]
        for d in range(1, 8):
            acc = acc + e_ref[d]
        o_ref[...] = acc

    vm = pl.BlockSpec(memory_space=pltpu.VMEM)
    return pl.pallas_call(add, name=name + "_sum", out_shape=jax.ShapeDtypeStruct(vec.shape, F32), in_specs=[vm],
                          out_specs=vm, compiler_params=_cparams())(every)


def _adamw(w, g, m, v):
    m = ADAM_B1 * m + (1.0 - ADAM_B1) * g
    v = ADAM_B2 * v + (1.0 - ADAM_B2) * (g * g)
    m_hat = m / (1.0 - ADAM_B1 ** ADAM_STEP)
    v_hat = v / (1.0 - ADAM_B2 ** ADAM_STEP)
    delta = -ADAM_LR * (m_hat / (jnp.sqrt(v_hat) + ADAM_EPS) + ADAM_WD * w)
    return delta, m, v


def _sum4(mine, land, *, name):
    rows, cols = mine.shape
    tr = _pick_rows(rows)

    def body(a_ref, l_ref, o_ref):
        o_ref[...] = (a_ref[...].astype(F32) + l_ref[0].astype(F32)) + (l_ref[1].astype(F32) + l_ref[2].astype(F32))

    return pl.pallas_call(
        body, name=name, out_shape=jax.ShapeDtypeStruct((rows, cols), F32), grid=(rows // tr,),
        in_specs=[pl.BlockSpec((tr, cols), lambda i: (i, 0)), pl.BlockSpec((3, tr, cols), lambda i: (0, i, 0))],
        out_specs=pl.BlockSpec((tr, cols), lambda i: (i, 0)),
        compiler_params=_cparams(("parallel",)),
    )(mine, land)


def _pick_rows(rows, want=256):
    for t in range(min(want, rows) // 8 * 8, 0, -8):
        if rows % t == 0:
            return t
    return rows


def _sum_adam(h_mine, h_sib, w, m, v, *, name):
    rows, cols = w.shape
    tr = _pick_rows(rows)

    def body(a_ref, b_ref, w_ref, m_ref, v_ref, g_o, d_o, m_o, v_o):
        g = a_ref[...] + b_ref[...]
        d, mn, vn = _adamw(w_ref[...], g, m_ref[...], v_ref[...])
        g_o[...], d_o[...], m_o[...], v_o[...] = g, d, mn, vn

    spec = pl.BlockSpec((tr, cols), lambda i: (i, 0))
    return pl.pallas_call(
        body, name=name, out_shape=[jax.ShapeDtypeStruct((rows, cols), F32)] * 4, grid=(rows // tr,),
        in_specs=[spec] * 5, out_specs=[spec] * 4, compiler_params=_cparams(("parallel",)),
    )(h_mine, h_sib, w, m, v)


def _adam_rows(w, g, m, v, *, name):
    def body(w_ref, g_ref, m_ref, v_ref, d_o, m_o, v_o):
        d_o[...], m_o[...], v_o[...] = _adamw(w_ref[...], g_ref[...], m_ref[...], v_ref[...])

    vm = pl.BlockSpec(memory_space=pltpu.VMEM)
    return pl.pallas_call(
        body, name=name, out_shape=[jax.ShapeDtypeStruct(w.shape, F32)] * 3,
        in_specs=[vm] * 4, out_specs=[vm] * 3, compiler_params=_cparams(),
    )(w, g, m, v)


def _size(shape):
    size = 1
    for d in shape:
        size *= d
    return size


def _pack_rows(arrs):
    blocks = []
    for a in arrs:
        flat = a.reshape(-1).astype(F32)
        flat = jnp.concatenate([flat, jnp.zeros((-flat.shape[0] % (8 * LANES),), F32)])
        blocks.append(flat.reshape(-1, LANES))
    return jnp.concatenate(blocks, axis=0)


def _unpack_rows(packed, shapes):
    out, row = [], 0
    for s in shapes:
        rows = -(-_size(s) // (8 * LANES)) * 8
        out.append(packed[row:row + rows].reshape(-1)[:_size(s)].reshape(s))
        row += rows
    return out


WEIGHTS = ['ffn1_w_in', 'ffn1_w_out', 'w_in', 'mu_prev', 'mu_next', 'w0', 'w2', 'a0', 'a2', 'g2', 'k_k', 'k_a', 'r_k',
           'lnx_g', 'lnx_b', 'conv_dw', 'conv_b', 'conv_ln_g', 'conv_ln_b', 'w_out', 'ffn2_w_in', 'ffn2_w_out',
           'ln1_g', 'ln1_b', 'ln2_g', 'ln2_b', 'ln3_g', 'ln3_b']
COL_SHARDED = ('ffn1_w_in', 'w_in', 'ffn2_w_in')
ROW_SHARDED = ('ffn1_w_out', 'w_out', 'ffn2_w_out')
BIG = COL_SHARDED + ROW_SHARDED
SMALL_SHARDED = ('w0', 'w2', 'a0', 'a2', 'g2', 'conv_dw')
REPLICATED = tuple(n for n in WEIGHTS if n not in BIG + SMALL_SHARDED)


def _train_step(x, target, w, m, v, *, tt):
    xi, yi, _ = _mesh_pos()
    q = 2 * xi + yi

    later = {"ffn1_out": ("ffn1_w_out",), "mix": ("w_in",) + SMALL_SHARDED, "out": ("w_out", "ffn2_w_in", "ffn2_w_out")}
    shard = lambda n: w[n][0].astype(BF16) if n in BIG else w[n][0]
    small_names = REPLICATED + SMALL_SHARDED

    def whole(n, slabs):
        if n in ROW_SHARDED:
            return slabs.reshape((-1,) + slabs.shape[2:])
        if n in ("ffn1_w_in", "ffn2_w_in"):
            return slabs
        return jnp.moveaxis(slabs, 0, -2).reshape(slabs.shape[1:-1] + (N_CHIPS * slabs.shape[-1],))

    full = {n: w[n][0] for n in REPLICATED}
    first = _gather_chips([shard("ffn1_w_in")], name="gather_ffn1_in")
    full["ffn1_w_in"] = whole("ffn1_w_in", first[0])
    started, token = {}, first[0]
    for stage, names in later.items():
        started[stage] = _exchange_start([shard(n) for n in names], scatter=False, after=token,
                                         name="gather_%s_start" % stage)
        token = started[stage][-1]

    def more_weights(stage, after):
        own, land = _exchange_wait(started[stage], scatter=False, after=after, name="gather_%s_wait" % stage)
        got = {n: whole(n, _by_chip(o, l)) for n, o, l in zip(later[stage], own, land)}
        full.update(got)
        return got

    small_sent = []

    def small_ready(gr, loss_part):
        vec = _pack_rows([gr[n] for n in small_names] + [loss_part[0:1, 0:1]])
        small_sent.append(_allsum_start(vec, after=vec, name="reduce_small_start"))
        return small_sent[0][-1]

    sent = []

    def grads_ready(names, slabs):
        started = _exchange_start(slabs, scatter=True, after=slabs[0], name="scatter_%s_start" % names[0])
        sent.append((names, started))
        return started[-1]

    grad_x, gr = _local_step(x, target, full, tt=tt, start_token=token, more_weights=more_weights,
                             grads_ready=grads_ready, small_ready=small_ready)

    halves = {}
    for names, started in sent:
        stacks, landed = _exchange_wait(started, scatter=True, after=grad_x, name="scatter_%s_wait" % names[0])
        for n, s, l in zip(names, stacks, landed):
            halves[n] = _sum4(lax.dynamic_index_in_dim(s, q, 0, keepdims=False), l, name="sum4_" + n)
    halves = [halves[n] for n in BIG]
    sib = _swap_sibling(halves, name="swap_halves")
    grad, delta, new_m, new_v = {}, {}, {}, {}
    for n, h, hs in zip(BIG, halves, sib):
        outs = _sum_adam(h, hs, w[n][0], m[n][0], v[n][0], name="adam_" + n)
        grad[n], delta[n], new_m[n], new_v[n] = [o[None] for o in outs]

    small_full_shapes = [full[n].shape for n in small_names]
    red = _allsum_wait(small_sent[0], after=grad_x, name="reduce_small_wait")
    *red, loss = _unpack_rows(red, small_full_shapes + [()])
    red = dict(zip(small_names, red))
    gsm = {}
    for n in REPLICATED:
        gsm[n] = red[n].reshape(w[n].shape)
    for n in SMALL_SHARDED:
        width = w[n].shape[-1]
        gsm[n] = lax.dynamic_slice_in_dim(red[n], q * width, width, axis=red[n].ndim - 1).reshape(w[n].shape)
    shapes = [w[n].shape for n in small_names]
    d_p, m_p, v_p = _adam_rows(_pack_rows([w[n] for n in small_names]), _pack_rows([gsm[n] for n in small_names]),
                               _pack_rows([m[n] for n in small_names]), _pack_rows([v[n] for n in small_names]),
                               name="adam_small")
    for n, dd, mm, vv in zip(small_names, _unpack_rows(d_p, shapes), _unpack_rows(m_p, shapes), _unpack_rows(v_p, shapes)):
        grad[n], delta[n], new_m[n], new_v[n] = gsm[n], dd, mm, vv
    return loss, grad_x, grad, delta, new_m, new_v


def kernel(x, ffn1_w_in, ffn1_w_out, w_in, mu_prev, mu_next, w0, w2, a0, a2, g2, k_k, k_a, r_k, lnx_g, lnx_b, conv_dw, conv_b, conv_ln_g, conv_ln_b, w_out, ffn2_w_in, ffn2_w_out, ln1_g, ln1_b, ln2_g, ln2_b, ln3_g, ln3_b, loss_target, m_ffn1_w_in, m_ffn1_w_out, m_w_in, m_mu_prev, m_mu_next, m_w0, m_w2, m_a0, m_a2, m_g2, m_k_k, m_k_a, m_r_k, m_lnx_g, m_lnx_b, m_conv_dw, m_conv_b, m_conv_ln_g, m_conv_ln_b, m_w_out, m_ffn2_w_in, m_ffn2_w_out, m_ln1_g, m_ln1_b, m_ln2_g, m_ln2_b, m_ln3_g, m_ln3_b, v_ffn1_w_in, v_ffn1_w_out, v_w_in, v_mu_prev, v_mu_next, v_w0, v_w2, v_a0, v_a2, v_g2, v_k_k, v_k_a, v_r_k, v_lnx_g, v_lnx_b, v_conv_dw, v_conv_b, v_conv_ln_g, v_conv_ln_b, v_w_out, v_ffn2_w_in, v_ffn2_w_out, v_ln1_g, v_ln1_b, v_ln2_g, v_ln2_b, v_ln3_g, v_ln3_b):
    args = dict(locals())
    w = {n: args[n] for n in WEIGHTS}
    m = {n: args["m_" + n] for n in WEIGHTS}
    v = {n: args["v_" + n] for n in WEIGHTS}
    seq = x.shape[1]
    loss, grad_x, grad, delta, new_m, new_v = _train_step(x, loss_target, w, m, v, tt=min(256, seq))
    return (loss, grad_x, *[grad[n] for n in WEIGHTS], *[delta[n] for n in WEIGHTS],
            *[new_m[n] for n in WEIGHTS], *[new_v[n] for n in WEIGHTS])
```

```python
import functools

import jax
import jax.numpy as jnp
from jax import lax
from jax.experimental import pallas as pl
from jax.experimental.pallas import tpu as pltpu

F32 = jnp.float32
BF16 = jnp.bfloat16

D_MODEL = 1024
RW = 512
HEAD = 64
CW = 512
CONV_K = 31
CONV_ROWS = 32
SHIFT_ROWS = 16
CONV_PAD = 15
D_FF = 2816
LORA = 64
GATE_LORA = 160
GATE_PAD = 256
SHIFT_COLS = 1952
SHIFT_PAD = 2048
IN_COLS = 2976
IN_PAD = 3072
LN_EPS = 1e-5
GN_EPS = 64e-5
NORM_EPS = 1e-12
ALPHA = 2.0 ** 0.25
DECAY_SCALE = 0.6065306597126334
ADAM_LR, ADAM_B1, ADAM_B2, ADAM_EPS, ADAM_WD, ADAM_STEP = 0.001, 0.9, 0.999, 1e-08, 0.01, 10
N_CHIPS = 4
VMEM_LIMIT = 56 * 1024 * 1024
TM_FFN = 256
TM_LN = 512

MESH = pl.DeviceIdType.MESH


def _cparams(sem=None, **kw):
    return pltpu.CompilerParams(dimension_semantics=sem, vmem_limit_bytes=VMEM_LIMIT, **kw)


LANES = 128


def _pick_tile(dim, want):
    for t in range(min(want, dim) // LANES * LANES, 0, -LANES):
        if dim % t == 0:
            return t
    return dim


def _after_operand(after):
    return ([], []) if after is None else ([pl.BlockSpec(memory_space=pl.ANY)], [after])


def _matmul(a, b, *, ta=False, tb=False, out_dtype=F32, tm=1024, tn=1024, tk=1024, scale=1.0, col_slabs=False,
            after=None, name):
    after_specs, after_args = _after_operand(after)
    if ta:
        k_dim, m_dim = a.shape
    else:
        m_dim, k_dim = a.shape
    n_dim = b.shape[0] if tb else b.shape[1]
    tm, tn, tk = _pick_tile(m_dim, tm), _pick_tile(n_dim, tn), _pick_tile(k_dim, tk)
    assert m_dim % tm == 0 and n_dim % tn == 0 and k_dim % tk == 0, (name, a.shape, b.shape, tm, tn, tk)
    nk = k_dim // tk
    dims = (((0,) if ta else (1,), (1,) if tb else (0,)), ((), ()))
    if col_slabs:
        out_shape = jax.ShapeDtypeStruct((n_dim // tn, m_dim, tn), out_dtype)
        out_spec = pl.BlockSpec((None, tm, tn), lambda i, j, k: (j, i, 0))
    else:
        out_shape = jax.ShapeDtypeStruct((m_dim, n_dim), out_dtype)
        out_spec = pl.BlockSpec((tm, tn), lambda i, j, k: (i, j))

    def body(a_ref, b_ref, *rest):
        o_ref, acc_ref = rest[-2:]
        kk = pl.program_id(2)

        @pl.when(kk == 0)
        def _():
            acc_ref[...] = jnp.zeros_like(acc_ref)

        acc_ref[...] += lax.dot_general(a_ref[...].astype(BF16), b_ref[...].astype(BF16), dims,
                                        preferred_element_type=F32)

        @pl.when(kk == nk - 1)
        def _():
            o_ref[...] = (acc_ref[...] * scale).astype(o_ref.dtype)

    a_spec = pl.BlockSpec((tk, tm), lambda i, j, k: (k, i)) if ta else pl.BlockSpec((tm, tk), lambda i, j, k: (i, k))
    b_spec = pl.BlockSpec((tn, tk), lambda i, j, k: (j, k)) if tb else pl.BlockSpec((tk, tn), lambda i, j, k: (k, j))
    return pl.pallas_call(
        body, name=name,
        out_shape=out_shape,
        grid=(m_dim // tm, n_dim // tn, nk),
        in_specs=[a_spec, b_spec] + after_specs,
        out_specs=out_spec,
        scratch_shapes=[pltpu.VMEM((tm, tn), F32)],
        compiler_params=_cparams(("parallel", "parallel", "arbitrary")),
    )(a, b, *after_args)


def _whole(shape):
    nd = len(shape)
    return pl.BlockSpec(shape, lambda i: (0,) * nd)


def _ffn_in(x, w, *, tm, after=None, name):
    n_tok = x.shape[0]
    sw = w.shape[2]
    tm = min(tm, n_tok)

    after_specs, after_args = _after_operand(after)

    def body(x_ref, w_ref, *rest):
        h_ref, a_ref = rest[-2:]
        xb = x_ref[...].astype(BF16)
        for s in range(2):
            g = jnp.dot(xb, w_ref[s], preferred_element_type=F32)
            u = jnp.dot(xb, w_ref[s + 2], preferred_element_type=F32)
            h_ref[:, s * sw:(s + 1) * sw] = g.astype(BF16)
            h_ref[:, (s + 2) * sw:(s + 3) * sw] = u.astype(BF16)
            a_ref[:, s * sw:(s + 1) * sw] = (_silu(g) * u).astype(BF16)

    return pl.pallas_call(
        body, name=name,
        out_shape=[jax.ShapeDtypeStruct((n_tok, 2 * D_FF), BF16), jax.ShapeDtypeStruct((n_tok, D_FF), BF16)],
        grid=(n_tok // tm,),
        in_specs=[pl.BlockSpec((tm, D_MODEL), lambda i: (i, 0)), _whole(w.shape)] + after_specs,
        out_specs=[pl.BlockSpec((tm, 2 * D_FF), lambda i: (i, 0)), pl.BlockSpec((tm, D_FF), lambda i: (i, 0))],
        compiler_params=_cparams(("parallel",)),
    )(x, w, *after_args)


def _mm_ln(a_list, w, xres, g, b, fscale, *, tm, name):
    n_tok = xres.shape[0]
    tm = min(tm, n_tok)
    na = len(a_list)

    def body(*refs):
        a_refs = refs[:na]
        w_ref, x_ref, g_ref, b_ref, z_o, y_o, yb_o = refs[na:]
        f, off = None, 0
        for a_ref in a_refs:
            k = a_ref.shape[1]
            t = jnp.dot(a_ref[...].astype(BF16), w_ref[off:off + k, :], preferred_element_type=F32)
            f = t if f is None else f + t
            off += k
        z = ALPHA * x_ref[...] + fscale * f
        y = _layer_norm(z, g_ref[...], b_ref[...])
        z_o[...] = z
        y_o[...] = y
        yb_o[...] = y.astype(BF16)

    tile = pl.BlockSpec((tm, D_MODEL), lambda i: (i, 0))
    return pl.pallas_call(
        body, name=name,
        out_shape=[jax.ShapeDtypeStruct((n_tok, D_MODEL), F32)] * 2 + [jax.ShapeDtypeStruct((n_tok, D_MODEL), BF16)],
        grid=(n_tok // tm,),
        in_specs=[pl.BlockSpec((tm, a.shape[1]), lambda i: (i, 0)) for a in a_list]
        + [_whole(w.shape), tile, _whole(g.shape), _whole(b.shape)],
        out_specs=[tile, tile, tile],
        compiler_params=_cparams(("parallel",)),
    )(*a_list, w, xres, g, b)


def _mm_ln_loss(a, w, xres, g, b, target, fscale, *, tm, name):
    n_tok = xres.shape[0]
    tm = min(tm, n_tok)

    def body(a_ref, w_ref, x_ref, g_ref, b_ref, t_ref, dz_o, dg_o, db_o, loss_o):
        i = pl.program_id(0)
        z = ALPHA * x_ref[...] + fscale * jnp.dot(a_ref[...].astype(BF16), w_ref[...], preferred_element_type=F32)
        y, vjp = jax.vjp(_layer_norm, z, g_ref[...], b_ref[...])
        e = y - t_ref[...]
        dz, dg, db = vjp(e * (1.0 / D_MODEL))

        @pl.when(i == 0)
        def _():
            dg_o[...] = jnp.zeros_like(dg_o)
            db_o[...] = jnp.zeros_like(db_o)
            loss_o[...] = jnp.zeros_like(loss_o)
        dz_o[...] = dz
        dg_o[...] += dg
        db_o[...] += db
        loss_o[...] += 0.5 * jnp.sum(jnp.mean(e * e, axis=-1, keepdims=True), axis=0, keepdims=True)

    tile = pl.BlockSpec((tm, D_MODEL), lambda i: (i, 0))
    row = pl.BlockSpec((1, D_MODEL), lambda i: (0, 0))
    return pl.pallas_call(
        body, name=name,
        out_shape=[jax.ShapeDtypeStruct((n_tok, D_MODEL), F32), jax.ShapeDtypeStruct((1, D_MODEL), F32),
                   jax.ShapeDtypeStruct((1, D_MODEL), F32), jax.ShapeDtypeStruct((8, LANES), F32)],
        grid=(n_tok // tm,),
        in_specs=[pl.BlockSpec((tm, a.shape[1]), lambda i: (i, 0)), _whole(w.shape), tile, row, row, tile],
        out_specs=[tile, row, row, pl.BlockSpec((8, LANES), lambda i: (0, 0))],
        compiler_params=_cparams(("arbitrary",)),
    )(a, w, xres, g, b, target)


def _ffn_out_bwd(dz, w, h, *, tm, after=None, name):
    n_tok = dz.shape[0]
    tm = min(tm, n_tok)
    cw = D_FF // 2
    after_specs, after_args = _after_operand(after)

    def body(dz_ref, w_ref, h_ref, *rest):
        dh_ref = rest[-1]
        dzb = dz_ref[...].astype(BF16)
        for s in range(2):
            dact = 0.5 * lax.dot_general(dzb, w_ref[s * cw:(s + 1) * cw, :], (((1,), (1,)), ((), ())),
                                         preferred_element_type=F32)
            gate = h_ref[:, s * cw:(s + 1) * cw].astype(F32)
            up = h_ref[:, D_FF + s * cw:D_FF + (s + 1) * cw].astype(F32)
            sg = _sigmoid(gate)
            dh_ref[:, s * cw:(s + 1) * cw] = (dact * up * sg * (1.0 + gate * (1.0 - sg))).astype(BF16)
            dh_ref[:, D_FF + s * cw:D_FF + (s + 1) * cw] = (dact * gate * sg).astype(BF16)

    wide = pl.BlockSpec((tm, 2 * D_FF), lambda i: (i, 0))
    return pl.pallas_call(
        body, name=name,
        out_shape=jax.ShapeDtypeStruct((n_tok, 2 * D_FF), BF16),
        grid=(n_tok // tm,),
        in_specs=[pl.BlockSpec((tm, D_MODEL), lambda i: (i, 0)), _whole(w.shape), wide] + after_specs,
        out_specs=wide,
        compiler_params=_cparams(("parallel",)),
    )(dz, w, h, *after_args)


def _mm_nt_res(a_list, w, dz, *, tm, ln=None, after=None, name):
    n_tok = dz.shape[0]
    tm = min(tm, n_tok)
    na = len(a_list)
    nt = (((1,), (1,)), ((), ()))
    after_specs, after_args = _after_operand(after)
    n_out = 1 if ln is None else 3

    def body(*refs):
        a_refs = refs[:na]
        w_ref, dz_ref, o_ref = refs[na], refs[na + 1], refs[-n_out]
        acc = ALPHA * dz_ref[...]
        if len(w_ref.shape) == 3:
            cw = w_ref.shape[2]
            for s in range(w_ref.shape[0]):
                acc = acc + lax.dot_general(a_refs[0][:, s * cw:(s + 1) * cw], w_ref[s], nt, preferred_element_type=F32)
        else:
            off = 0
            for a_ref in a_refs:
                k = a_ref.shape[1]
                acc = acc + lax.dot_general(a_ref[...], w_ref[:, off:off + k], nt, preferred_element_type=F32)
                off += k
        if ln is None:
            o_ref[...] = acc
            return
        z_ref, g_ref, b_ref = refs[na + 2:na + 5]
        dg_o, db_o = refs[-2:]
        _, vjp = jax.vjp(_layer_norm, z_ref[...], g_ref[...], b_ref[...])
        o_ref[...], dg, db = vjp(acc)

        @pl.when(pl.program_id(0) == 0)
        def _():
            dg_o[...] = jnp.zeros_like(dg_o)
            db_o[...] = jnp.zeros_like(db_o)
        dg_o[...] += dg
        db_o[...] += db

    tile = pl.BlockSpec((tm, D_MODEL), lambda i: (i, 0))
    row = pl.BlockSpec((1, D_MODEL), lambda i: (0, 0))
    out_shape = [jax.ShapeDtypeStruct((n_tok, D_MODEL), F32)]
    ln_specs, ln_args, out_specs = [], [], [tile]
    if ln is not None:
        ln_specs, ln_args = [tile, row, row], list(ln)
        out_shape += [jax.ShapeDtypeStruct((1, D_MODEL), F32)] * 2
        out_specs += [row, row]
    outs = pl.pallas_call(
        body, name=name,
        out_shape=out_shape,
        grid=(n_tok // tm,),
        in_specs=[pl.BlockSpec((tm, a.shape[1]), lambda i: (i, 0)) for a in a_list] + [_whole(w.shape), tile]
        + ln_specs + after_specs,
        out_specs=out_specs,
        compiler_params=_cparams(("parallel",) if ln is None else ("arbitrary",)),
    )(*a_list, w, dz, *ln_args, *after_args)
    return outs[0] if ln is None else outs


def _rowcall(fn, tok_in, full_in, tok_out, acc_out, *, tt, name):
    views = [a if isinstance(a, tuple) else (a, a.shape[1], 0) for a in tok_in]
    tok_in = [a for a, _, _ in views]
    n_tok = tok_in[0].shape[0]
    assert n_tok % tt == 0, (name, n_tok, tt)
    n_ti, n_fi, n_to = len(tok_in), len(full_in), len(tok_out)

    def body(*refs):
        i = pl.program_id(0)
        ins = [r[...] for r in refs[:n_ti + n_fi]]
        outs = fn(i, *ins)
        o_refs = refs[n_ti + n_fi:]
        for r, val in zip(o_refs[:n_to], outs[:n_to]):
            r[...] = val.astype(r.dtype)
        if acc_out:
            @pl.when(i == 0)
            def _():
                for r in o_refs[n_to:]:
                    r[...] = jnp.zeros_like(r)
            for r, val in zip(o_refs[n_to:], outs[n_to:]):
                r[...] += val.reshape(r.shape).astype(F32)

    in_specs = [pl.BlockSpec((tt, width), functools.partial(lambda k, i: (i, k), k)) for _, width, k in views]
    in_specs += [pl.BlockSpec(a.shape, lambda i: (0, 0)) for a in full_in]
    out_specs = [pl.BlockSpec((tt, c), lambda i: (i, 0)) for c, _ in tok_out]
    out_specs += [pl.BlockSpec(s, lambda i: (0, 0)) for s in acc_out]
    out_shape = [jax.ShapeDtypeStruct((n_tok, c), dt) for c, dt in tok_out]
    out_shape += [jax.ShapeDtypeStruct(s, F32) for s in acc_out]
    return pl.pallas_call(
        body, name=name, out_shape=out_shape, grid=(n_tok // tt,), in_specs=in_specs, out_specs=out_specs,
        compiler_params=_cparams(("arbitrary",) if acc_out else ("parallel",)),
    )(*tok_in, *full_in)


@jax.custom_vjp
def _bdot(a, b):
    return jnp.dot(a.astype(BF16), b.astype(BF16), preferred_element_type=F32)


def _bdot_fwd(a, b):
    return _bdot(a, b), (a, b)


def _bdot_bwd(res, g):
    a, b = res
    g16 = g.astype(BF16)
    da = lax.dot_general(g16, b.astype(BF16), (((1,), (1,)), ((), ())), preferred_element_type=F32)
    db = lax.dot_general(a.astype(BF16), g16, (((0,), (0,)), ((), ())), preferred_element_type=F32)
    return da, db


_bdot.defvjp(_bdot_fwd, _bdot_bwd)


def _split16(x):
    hi = x.astype(BF16)
    lo = (x - hi.astype(F32)).astype(BF16)
    return hi, lo


def _segsum_raw(x, e2):
    hi, lo = _split16(x)
    outs = []
    for c in range(x.shape[1] // 256):
        lhs = jnp.concatenate([hi[:, 256 * c:256 * (c + 1)], lo[:, 256 * c:256 * (c + 1)]], axis=1)
        outs.append(jnp.dot(lhs, e2, preferred_element_type=F32))
    return jnp.concatenate(outs, axis=1)


@jax.custom_vjp
def _segsum(x, e2):
    return _segsum_raw(x, e2)


def _segsum_fwd(x, e2):
    return _segsum_raw(x, e2), e2


def _segsum_bwd(e2, g):
    return _segsum_raw(g, e2), jnp.zeros_like(e2)


_segsum.defvjp(_segsum_fwd, _segsum_bwd)


def _head_ones():
    r = lax.broadcasted_iota(jnp.int32, (512, 256), 0) % 256
    c = lax.broadcasted_iota(jnp.int32, (512, 256), 1)
    return (r // HEAD == c // HEAD).astype(BF16)


def _sigmoid(x):
    return 1.0 / (1.0 + jnp.exp(-x))


def _silu(x):
    return x * _sigmoid(x)


def _layer_norm(z, g, b, eps=LN_EPS):
    mu = jnp.mean(z, axis=-1, keepdims=True)
    zc = z - mu
    var = jnp.mean(zc * zc, axis=-1, keepdims=True)
    return zc * lax.rsqrt(var + eps) * g + b


def _prep(ps, w2b, w0c, a2b, a0c, g2p, k_k, k_a, e2):
    r, k, v = ps[:, 0:512], ps[:, 512:1024], ps[:, 1024:1536]
    wd, ad, gd = ps[:, 1536:1664], ps[:, 1664:1792], ps[:, 1792:2048]
    lw = _bdot(jnp.tanh(wd), w2b) + w0c
    decay = -DECAY_SCALE * _sigmoid(lw)
    a = _sigmoid(_bdot(ad, a2b) + a0c)
    g = _bdot(_sigmoid(gd), g2p)
    kkr = k * k_k
    nrm = jnp.sqrt(_segsum(kkr * kkr, e2))
    kk = kkr / jnp.maximum(nrm, NORM_EPS)
    k2 = jnp.concatenate([k, k], axis=1)
    ka2 = jnp.concatenate([k_a, k_a], axis=1)
    kd = k2 * (1.0 + (a - 1.0) * ka2)
    b = jnp.concatenate([kk, kk], axis=1) * a
    return r, v, kk, decay, kd, b, g


def _post(y0, y1, r, v, kd, g, lnx_g, lnx_b, r_k, e2):
    y = y0 + y1
    mu = _segsum(y, e2) * (1.0 / HEAD)
    yc = y - mu
    var = _segsum(yc * yc, e2) * (1.0 / HEAD)
    yn = yc * lax.rsqrt(var + GN_EPS) * lnx_g + lnx_b
    bonus = _segsum(r * (kd[:, :RW] + kd[:, RW:]) * r_k, e2)
    return (yn + bonus * v) * g


def _conv_post(yc, ln_g, ln_b):
    return _silu(_layer_norm(yc, ln_g, ln_b))


def _halo_specs(cols_block, hb, tt, n_tok, col_idx):
    nb = n_tok // hb
    prev = pl.BlockSpec((hb, cols_block), lambda i: (jnp.maximum(i * (tt // hb) - 1, 0), col_idx))
    nxt = pl.BlockSpec((hb, cols_block), lambda i: (jnp.minimum((i + 1) * (tt // hb), nb - 1), col_idx))
    return prev, nxt


def _mix_prep(p, mu_p, mu_n, w2b, w0c, a2b, a0c, g2p, k_k, k_a, *, seq, tt, name):
    n_tok = p.shape[0]
    tps = seq // tt
    e2 = _head_ones()

    def body(p_ref, hp_ref, hn_ref, mup_ref, mun_ref, w2b_ref, w0c_ref, a2b_ref, a0c_ref, g2p_ref, kk_ref, ka_ref,
             e2_ref, r_o, v_o, kk_o, w_o, kd_o, b_o, g_o, ext):
        i = pl.program_id(0)
        first = (i % tps) == 0
        last = (i % tps) == tps - 1
        pv = p_ref[...]
        ext[pl.ds(0, 8), :] = jnp.where(first, 0.0, hp_ref[...])
        ext[pl.ds(8, tt), :] = pv
        ext[pl.ds(8 + tt, 8), :] = jnp.where(last, 0.0, hn_ref[...])
        prev = ext[pl.ds(7, tt), :]
        nxt = ext[pl.ds(9, tt), :]
        ps = pv + mup_ref[...] * (prev - pv) + mun_ref[...] * (nxt - pv)
        outs = _prep(ps, w2b_ref[...], w0c_ref[...], a2b_ref[...], a0c_ref[...], g2p_ref[...], kk_ref[...],
                     ka_ref[...], e2_ref[...])
        for o_ref, val in zip((r_o, v_o, kk_o, w_o, kd_o, b_o, g_o), outs):
            o_ref[...] = val

    hp, hn = _halo_specs(SHIFT_PAD, 8, tt, n_tok, 0)
    fulls = [mu_p, mu_n, w2b, w0c, a2b, a0c, g2p, k_k, k_a, e2]
    widths = (RW, RW, RW, 2 * RW, 2 * RW, 2 * RW, RW)
    return pl.pallas_call(
        body, name=name,
        out_shape=[jax.ShapeDtypeStruct((n_tok, c), F32) for c in widths],
        grid=(n_tok // tt,),
        in_specs=[pl.BlockSpec((tt, SHIFT_PAD), lambda i: (i, 0)), hp, hn]
        + [pl.BlockSpec(a.shape, lambda i: (0, 0)) for a in fulls],
        out_specs=[pl.BlockSpec((tt, c), lambda i: (i, 0)) for c in widths],
        scratch_shapes=[pltpu.VMEM((tt + 16, SHIFT_PAD), F32)],
        compiler_params=_cparams(("parallel",)),
    )(p, p, p, *fulls)


def _mix_prep_bwd(p, mu_p, mu_n, w2b, w0c, a2b, a0c, g2p, k_k, k_a, ct_terms, *, seq, tt, name):
    n_tok = p.shape[0]
    tps = seq // tt
    e2 = _head_ones()
    acc_shapes = [w2b.shape, w0c.shape, a2b.shape, a0c.shape, g2p.shape, k_k.shape, k_a.shape]
    cts = [a for terms in ct_terms for t in terms for a in (t if isinstance(t, tuple) else (t,))]

    def body(p_ref, hp_ref, hn_ref, mup_ref, mun_ref, w2b_ref, w0c_ref, a2b_ref, a0c_ref, g2p_ref, kk_ref, ka_ref,
             e2_ref, *rest):
        ct_refs, dps_o, acc_refs, ext = rest[:len(cts)], rest[len(cts)], rest[len(cts) + 1:-1], rest[-1]
        ct_it = iter(ct_refs)
        ct_vals = []
        for terms in ct_terms:
            total = None
            for t in terms:
                if isinstance(t, tuple):
                    val = jnp.concatenate([next(ct_it)[...] for _ in t], axis=1)
                else:
                    val = next(ct_it)[...]
                total = val if total is None else total + val
            ct_vals.append(total)
        i = pl.program_id(0)
        first = (i % tps) == 0
        last = (i % tps) == tps - 1
        pv = p_ref[...]
        ext[pl.ds(0, 8), :] = jnp.where(first, 0.0, hp_ref[...])
        ext[pl.ds(8, tt), :] = pv
        ext[pl.ds(8 + tt, 8), :] = jnp.where(last, 0.0, hn_ref[...])
        prev = ext[pl.ds(7, tt), :]
        nxt = ext[pl.ds(9, tt), :]
        ps = pv + mup_ref[...] * (prev - pv) + mun_ref[...] * (nxt - pv)
        e2v = e2_ref[...]
        _, vjp = jax.vjp(lambda *a: _prep(*a, e2v), ps, w2b_ref[...], w0c_ref[...], a2b_ref[...], a0c_ref[...],
                         g2p_ref[...], kk_ref[...], ka_ref[...])
        grads = vjp(tuple(ct_vals))
        dps_o[...] = grads[0]

        @pl.when(i == 0)
        def _():
            for r in acc_refs:
                r[...] = jnp.zeros_like(r)
        for r, val in zip(acc_refs, grads[1:]):
            r[...] += val

    hp, hn = _halo_specs(SHIFT_PAD, 8, tt, n_tok, 0)
    fulls = [mu_p, mu_n, w2b, w0c, a2b, a0c, g2p, k_k, k_a, e2]
    return pl.pallas_call(
        body, name=name,
        out_shape=[jax.ShapeDtypeStruct((n_tok, SHIFT_PAD), F32)] + [jax.ShapeDtypeStruct(s, F32) for s in acc_shapes],
        grid=(n_tok // tt,),
        in_specs=[pl.BlockSpec((tt, SHIFT_PAD), lambda i: (i, 0)), hp, hn]
        + [pl.BlockSpec(a.shape, lambda i: (0, 0)) for a in fulls]
        + [pl.BlockSpec((tt, c.shape[1]), lambda i: (i, 0)) for c in cts],
        out_specs=[pl.BlockSpec((tt, SHIFT_PAD), lambda i: (i, 0))] + [pl.BlockSpec(s, lambda i: (0, 0)) for s in acc_shapes],
        scratch_shapes=[pltpu.VMEM((tt + 16, SHIFT_PAD), F32)],
        compiler_params=_cparams(("arbitrary",)),
    )(p, p, p, *fulls, *cts)


def _shift_bwd(dps, p, mu_p, mu_n, *, seq, tt, name):
    n_tok = p.shape[0]
    tps = seq // tt

    def body(d_ref, dhp_ref, dhn_ref, p_ref, php_ref, phn_ref, mup_ref, mun_ref, dp_o, dmup_o, dmun_o, ext):
        i = pl.program_id(0)
        first = (i % tps) == 0
        last = (i % tps) == tps - 1
        mup, mun = mup_ref[...], mun_ref[...]
        rb = min(SHIFT_ROWS, tt)
        ext[pl.ds(0, 8), :] = jnp.where(first, 0.0, dhp_ref[...])
        ext[pl.ds(8, tt), :] = d_ref[...]
        ext[pl.ds(8 + tt, 8), :] = jnp.where(last, 0.0, dhn_ref[...])
        for r0 in range(0, tt, rb):
            dv = d_ref[pl.ds(r0, rb), :]
            dp_o[pl.ds(r0, rb), :] = (dv * (1.0 - mup - mun) + ext[pl.ds(r0 + 9, rb), :] * mup
                                      + ext[pl.ds(r0 + 7, rb), :] * mun).astype(dp_o.dtype)
        ext[pl.ds(0, 8), :] = jnp.where(first, 0.0, php_ref[...])
        ext[pl.ds(8, tt), :] = p_ref[...]
        ext[pl.ds(8 + tt, 8), :] = jnp.where(last, 0.0, phn_ref[...])

        @pl.when(i == 0)
        def _():
            dmup_o[...] = jnp.zeros_like(dmup_o)
            dmun_o[...] = jnp.zeros_like(dmun_o)
        sum_p = jnp.zeros_like(mup)
        sum_n = jnp.zeros_like(mun)
        for r0 in range(0, tt, rb):
            dv, pv = d_ref[pl.ds(r0, rb), :], p_ref[pl.ds(r0, rb), :]
            sum_p = sum_p + jnp.sum(dv * (ext[pl.ds(r0 + 7, rb), :] - pv), axis=0, keepdims=True)
            sum_n = sum_n + jnp.sum(dv * (ext[pl.ds(r0 + 9, rb), :] - pv), axis=0, keepdims=True)
        dmup_o[...] += sum_p
        dmun_o[...] += sum_n

    hp, hn = _halo_specs(SHIFT_PAD, 8, tt, n_tok, 0)
    tile = pl.BlockSpec((tt, SHIFT_PAD), lambda i: (i, 0))
    full = pl.BlockSpec((1, SHIFT_PAD), lambda i: (0, 0))
    return pl.pallas_call(
        body, name=name,
        out_shape=[jax.ShapeDtypeStruct((n_tok, SHIFT_PAD), BF16), jax.ShapeDtypeStruct((1, SHIFT_PAD), F32),
                   jax.ShapeDtypeStruct((1, SHIFT_PAD), F32)],
        grid=(n_tok // tt,),
        in_specs=[tile, hp, hn, tile, hp, hn, full, full],
        out_specs=[tile, full, full],
        scratch_shapes=[pltpu.VMEM((tt + 16, SHIFT_PAD), F32)],
        compiler_params=_cparams(("arbitrary",)),
    )(dps, dps, dps, p, p, p, mu_p, mu_n)


def _mix_post(y0, y1, r, v, kd, g, lnx_g, lnx_b, r_k, *, tt, name):
    e2 = _head_ones()
    return _rowcall(lambda i, *a: (_post(*a),), [y0, y1, r, v, kd, g], [lnx_g, lnx_b, r_k, e2], [(RW, BF16)], [],
                    tt=tt, name=name)[0]


def _mix_post_bwd(y0, y1, r, v, kd, g, lnx_g, lnx_b, r_k, dout, *, tt, name):
    e2 = _head_ones()

    def fn(i, y0v, y1v, rv, vv, kdv, gv, dov, lg, lb, rk, e2v):
        _, vjp = jax.vjp(lambda *a: _post(*a, e2v), y0v, y1v, rv, vv, kdv, gv, lg, lb, rk)
        gr = vjp(dov.astype(F32))
        return gr[0], gr[2], gr[3], gr[4], gr[5], gr[6], gr[7], gr[8]
    return _rowcall(fn, [y0, y1, r, v, kd, g, dout], [lnx_g, lnx_b, r_k, e2],
                    [(RW, F32), (RW, F32), (RW, F32), (2 * RW, F32), (RW, F32)], [(1, RW), (1, RW), (1, RW)],
                    tt=tt, name=name)


def _conv_fwd(p, dw, db, ln_g, ln_b, *, seq, tt, name):
    n_tok = p.shape[0]
    tps = seq // tt

    def glu(x, gate):
        return x * _sigmoid(gate)

    def body(u_ref, g_ref, uhp, ghp, uhn, ghn, dw_ref, db_ref, lg_ref, lb_ref, yc_o, y_o, ext):
        i = pl.program_id(0)
        first = (i % tps) == 0
        last = (i % tps) == tps - 1
        ext[pl.ds(0, 16), :] = jnp.where(first, 0.0, glu(uhp[...], ghp[...]))
        ext[pl.ds(16, tt), :] = glu(u_ref[...], g_ref[...])
        ext[pl.ds(16 + tt, 16), :] = jnp.where(last, 0.0, glu(uhn[...], ghn[...]))
        taps = [dw_ref[pl.ds(k, 1), :] for k in range(CONV_K)]
        for r0 in range(0, tt, CONV_ROWS):
            acc = jnp.zeros((CONV_ROWS, CW), F32) + db_ref[...]
            for k in range(CONV_K):
                acc = acc + ext[pl.ds(r0 + k + 1, CONV_ROWS), :] * taps[k]
            yc_o[pl.ds(r0, CONV_ROWS), :] = acc
        y_o[...] = _conv_post(yc_o[...], lg_ref[...], lb_ref[...]).astype(y_o.dtype)

    uhp_s, uhn_s = _halo_specs(CW, 16, tt, n_tok, 4)
    ghp_s, ghn_s = _halo_specs(CW, 16, tt, n_tok, 5)
    fulls = [dw, db, ln_g, ln_b]
    return pl.pallas_call(
        body, name=name,
        out_shape=[jax.ShapeDtypeStruct((n_tok, CW), F32), jax.ShapeDtypeStruct((n_tok, CW), BF16)],
        grid=(n_tok // tt,),
        in_specs=[pl.BlockSpec((tt, CW), lambda i: (i, 4)), pl.BlockSpec((tt, CW), lambda i: (i, 5)),
                  uhp_s, ghp_s, uhn_s, ghn_s] + [pl.BlockSpec(a.shape, lambda i: (0, 0)) for a in fulls],
        out_specs=[pl.BlockSpec((tt, CW), lambda i: (i, 0)), pl.BlockSpec((tt, CW), lambda i: (i, 0))],
        scratch_shapes=[pltpu.VMEM((tt + 32, CW), F32)],
        compiler_params=_cparams(("parallel",)),
    )(p, p, p, p, p, p, *fulls)


def _conv_post_bwd(yc, dy, ln_g, ln_b, *, tt, name):
    def fn(i, ycv, dyv, lg, lb):
        _, vjp = jax.vjp(_conv_post, ycv, lg, lb)
        dyc, dg, dbb = vjp(dyv.astype(F32))
        return dyc, dg, dbb, jnp.sum(dyc, axis=0, keepdims=True)
    return _rowcall(fn, [yc, dy], [ln_g, ln_b], [(CW, F32)], [(1, CW), (1, CW), (1, CW)], tt=tt, name=name)


def _conv_bwd(dyc, p, dw, *, seq, tt, name):
    n_tok = p.shape[0]
    tps = seq // tt

    def body(d_ref, dhp, dhn, u_ref, g_ref, uhp, ghp, uhn, ghn, dw_ref, dp_o, ddw_o, ext):
        i = pl.program_id(0)
        first = (i % tps) == 0
        last = (i % tps) == tps - 1
        dv = d_ref[...]
        ext[pl.ds(0, 16), :] = jnp.where(first, 0.0, dhp[...])
        ext[pl.ds(16, tt), :] = dv
        ext[pl.ds(16 + tt, 16), :] = jnp.where(last, 0.0, dhn[...])
        taps = [dw_ref[pl.ds(k, 1), :] for k in range(CONV_K)]
        for r0 in range(0, tt, CONV_ROWS):
            du = jnp.zeros((CONV_ROWS, CW), F32)
            for k in range(CONV_K):
                du = du + ext[pl.ds(r0 + 31 - k, CONV_ROWS), :] * taps[k]
            rows = pl.ds(r0, CONV_ROWS)
            sg_r = _sigmoid(g_ref[rows, :])
            dp_o[rows, 0:CW] = (du * sg_r).astype(dp_o.dtype)
            dp_o[rows, CW:2 * CW] = (du * u_ref[rows, :] * sg_r * (1.0 - sg_r)).astype(dp_o.dtype)
        uv, gv = u_ref[...], g_ref[...]
        sg = _sigmoid(gv)
        ext[pl.ds(0, 16), :] = jnp.where(first, 0.0, uhp[...] * _sigmoid(ghp[...]))
        ext[pl.ds(16, tt), :] = uv * sg
        ext[pl.ds(16 + tt, 16), :] = jnp.where(last, 0.0, uhn[...] * _sigmoid(ghn[...]))

        @pl.when(i == 0)
        def _():
            ddw_o[...] = jnp.zeros_like(ddw_o)
        for k in range(CONV_K):
            ddw_o[pl.ds(k, 1), :] += jnp.sum(dv * ext[pl.ds(k + 1, tt), :], axis=0, keepdims=True)

    dhp_s, dhn_s = _halo_specs(CW, 16, tt, n_tok, 0)
    uhp_s, uhn_s = _halo_specs(CW, 16, tt, n_tok, 4)
    ghp_s, ghn_s = _halo_specs(CW, 16, tt, n_tok, 5)
    return pl.pallas_call(
        body, name=name,
        out_shape=[jax.ShapeDtypeStruct((n_tok, 2 * CW), BF16), jax.ShapeDtypeStruct((32, CW), F32)],
        grid=(n_tok // tt,),
        in_specs=[pl.BlockSpec((tt, CW), lambda i: (i, 0)), dhp_s, dhn_s,
                  pl.BlockSpec((tt, CW), lambda i: (i, 4)), pl.BlockSpec((tt, CW), lambda i: (i, 5)),
                  uhp_s, ghp_s, uhn_s, ghn_s, pl.BlockSpec(dw.shape, lambda i: (0, 0))],
        out_specs=[pl.BlockSpec((tt, 2 * CW), lambda i: (i, 0)), pl.BlockSpec((32, CW), lambda i: (0, 0))],
        scratch_shapes=[pltpu.VMEM((tt + 32, CW), F32)],
        compiler_params=_cparams(("arbitrary",)),
    )(dyc, dyc, dyc, p, p, p, p, p, p, dw)


CHUNK = 64
_MM_DIMS = {"nn": (((2,), (1,)), ((0,), (0,))), "nt": (((2,), (2,)), ((0,), (0,))), "tn": (((1,), (1,)), ((0,), (0,)))}


def _mm16_raw(a, b, mode, fine):
    dot = lambda x, y: lax.dot_general(x, y, _MM_DIMS[mode], preferred_element_type=F32)
    if not fine:
        return dot(a.astype(BF16), b.astype(BF16))
    ah, (bh, bl) = a.astype(BF16), _split16(b)
    return dot(ah, bh) + dot(ah, bl)


@functools.partial(jax.custom_vjp, nondiff_argnums=(2, 3))
def _mm16(a, b, mode, fine=False):
    return _mm16_raw(a, b, mode, fine)


def _mm16_fwd(a, b, mode, fine):
    return _mm16_raw(a, b, mode, fine), (a, b)


def _mm16_bwd(mode, fine, res, g):
    a, b = res
    if mode == "nn":
        return _mm16_raw(g, b, "nt", fine), _mm16_raw(a, g, "tn", fine)
    if mode == "nt":
        return _mm16_raw(g, b, "nn", fine), _mm16_raw(g, a, "tn", fine)
    return _mm16_raw(b, g, "nt", fine), _mm16_raw(a, g, "nn", fine)


_mm16.defvjp(_mm16_fwd, _mm16_bwd)


def _tri_sum_raw(x, tri, mode):
    hi = x.astype(BF16)
    r1 = x - hi.astype(F32)
    mid = r1.astype(BF16)
    lo = (r1 - mid.astype(F32)).astype(BF16)
    dot = lambda p: lax.dot_general(tri, p, _MM_DIMS[mode], preferred_element_type=F32)
    return dot(hi) + dot(mid) + dot(lo)


@jax.custom_vjp
def _tri_sum(x, tri):
    return _tri_sum_raw(x, tri, "nn")


def _tri_sum_fwd(x, tri):
    return _tri_sum_raw(x, tri, "nn"), tri


def _tri_sum_bwd(tri, g):
    return _tri_sum_raw(g, tri, "tn"), jnp.zeros_like(tri)


_tri_sum.defvjp(_tri_sum_fwd, _tri_sum_bwd)


PAIR = 2 * HEAD
N_PAIRS = RW // PAIR


def _pair_rows(x):
    first = lax.broadcasted_iota(jnp.int32, x.shape, 2) < HEAD
    return jnp.concatenate([jnp.where(first, x, 0.0), jnp.where(first, 0.0, x)], axis=1)


def _chunk_step_pairs(s0, r, lw, k, v, kk, b, tri, rev):
    g, n, _ = r.shape
    row = lax.broadcasted_iota(jnp.int32, (g, n, PAIR), 1)
    col = lax.broadcasted_iota(jnp.int32, (g, n, PAIR), 2) % HEAD
    if rev:
        row, col = col, row
    diag = (lax.broadcasted_iota(jnp.int32, (g, PAIR, PAIR), 1) // HEAD
            == lax.broadcasted_iota(jnp.int32, (g, PAIR, PAIR), 2) // HEAD)
    cum = _tri_sum(lw, tri)
    up, down = jnp.exp(cum), jnp.exp(-cum)
    at, rt = -kk * jnp.exp(cum - lw), r * up
    kt, bt = k * down, b * down
    bt_rows, kt_rows = _pair_rows(bt), _pair_rows(kt)
    ar = jnp.concatenate([at, rt], axis=1)
    with_b = _mm16(ar, bt_rows, "nt")
    a_ab = jnp.where(col < row, with_b[:, :n], 0.0)
    a_rb = jnp.where(col <= row, with_b[:, n:], 0.0)
    a_ak = jnp.where(col < row, _mm16(at, kt_rows, "nt"), 0.0)
    a_rk = jnp.where(col <= row, _mm16(rt, kt_rows, "nt", True), 0.0)
    v_rows = _pair_rows(v)
    from_state = _mm16(ar, s0, "nt")
    u = from_state[:, :n] + _mm16(a_ak, v_rows, "nn")
    power = a_ab
    steps = n.bit_length() - 1
    for it in range(steps):
        u = u + _mm16(power, _pair_rows(u), "nn")
        if it + 1 < steps:
            power = _mm16(power, _pair_rows(power), "nn")
    y = from_state[:, n:] + _mm16(a_rk, v_rows, "nn") + _mm16(a_rb, _pair_rows(u), "nn")
    grown = s0 + jnp.where(diag, _mm16(v, kt, "tn") + _mm16(u, bt, "tn"), 0.0)
    return y, grown * jnp.exp(jnp.sum(lw, axis=1, keepdims=True))


SCAN_SEQS = 4


def _tri_ones(rev, nseq):
    shape = (nseq * N_PAIRS, CHUNK, CHUNK)
    row, col = lax.broadcasted_iota(jnp.int32, shape, 1), lax.broadcasted_iota(jnp.int32, shape, 2)
    return ((col >= row) if rev else (col <= row)).astype(BF16)


def _split_heads(ref):
    return jnp.stack([ref[q, :, pl.ds(h * PAIR, PAIR)] for q in range(ref.shape[0]) for h in range(N_PAIRS)])


def _merge_heads(ref, val):
    for q in range(ref.shape[0]):
        for h in range(N_PAIRS):
            ref[q, :, pl.ds(h * PAIR, PAIR)] = val[q * N_PAIRS + h]


def _chunk_specs(nseq, nc, rev, dcol):
    chunk = (lambda c: nc - 1 - c) if rev else (lambda c: c)
    shared = pl.BlockSpec((nseq, CHUNK, RW), lambda s, c: (s, chunk(c), 0))
    own = pl.BlockSpec((nseq, CHUNK, RW), lambda s, c: (s, chunk(c), dcol))
    return shared, own


def _wkv_chunk_fwd(r, lw, k, v, kk, b, *, rev, name):
    bsz, seq, _ = r.shape
    nc = seq // CHUNK
    nseq = SCAN_SEQS if bsz % SCAN_SEQS == 0 else 1
    shared, own = _chunk_specs(nseq, nc, rev, int(rev))

    def body(r_ref, lw_ref, k_ref, v_ref, kk_ref, b_ref, tri_ref, y_o, s0_o, s_ref):
        @pl.when(pl.program_id(1) == 0)
        def _():
            s_ref[...] = jnp.zeros_like(s_ref)

        s0 = s_ref[...]
        s0_o[:, 0] = s0.reshape(nseq, N_PAIRS, PAIR, PAIR)
        y, s_ref[...] = _chunk_step_pairs(s0, *[_split_heads(x) for x in (r_ref, lw_ref, k_ref, v_ref, kk_ref, b_ref)],
                                    tri_ref[...], rev)
        _merge_heads(y_o, y)

    return pl.pallas_call(
        body, name=name,
        out_shape=[jax.ShapeDtypeStruct((bsz, seq, RW), F32), jax.ShapeDtypeStruct((bsz, nc, N_PAIRS, PAIR, PAIR), F32)],
        grid=(bsz // nseq, nc),
        in_specs=[shared, own, own, shared, shared, own,
                  pl.BlockSpec((nseq * N_PAIRS, CHUNK, CHUNK), lambda s, c: (0, 0, 0))],
        out_specs=[shared, pl.BlockSpec((nseq, 1, N_PAIRS, PAIR, PAIR), lambda s, c: (s, c, 0, 0, 0))],
        scratch_shapes=[pltpu.VMEM((nseq * N_PAIRS, PAIR, PAIR), F32)],
        compiler_params=_cparams(("parallel", "arbitrary")),
    )(r, lw, k, v, kk, b, _tri_ones(rev, nseq))


def _wkv_chunk_bwd(r, lw, k, v, kk, b, dy, s0, *, rev, name):
    bsz, seq, _ = r.shape
    nc = seq // CHUNK
    nseq = SCAN_SEQS if bsz % SCAN_SEQS == 0 else 1
    shared, own = _chunk_specs(nseq, nc, not rev, int(rev))

    def body(r_ref, lw_ref, k_ref, v_ref, kk_ref, b_ref, dy_ref, s0_ref, tri_ref, *rest):
        outs, ds_ref = rest[:-1], rest[-1]

        @pl.when(pl.program_id(1) == 0)
        def _():
            ds_ref[...] = jnp.zeros_like(ds_ref)

        triv = tri_ref[...]
        _, vjp = jax.vjp(lambda *a: _chunk_step_pairs(*a, triv, rev), s0_ref[:, 0].reshape(nseq * N_PAIRS, PAIR, PAIR),
                         *[_split_heads(x) for x in (r_ref, lw_ref, k_ref, v_ref, kk_ref, b_ref)])
        grads = vjp((_split_heads(dy_ref), ds_ref[...]))
        ds_ref[...] = grads[0]
        for o, gval in zip(outs, grads[1:]):
            _merge_heads(o, gval)

    return pl.pallas_call(
        body, name=name,
        out_shape=[jax.ShapeDtypeStruct((bsz, seq, RW), F32)] * 6,
        grid=(bsz // nseq, nc),
        in_specs=[shared, own, own, shared, shared, own, shared,
                  pl.BlockSpec((nseq, 1, N_PAIRS, PAIR, PAIR), lambda s, c: (s, nc - 1 - c, 0, 0, 0)),
                  pl.BlockSpec((nseq * N_PAIRS, CHUNK, CHUNK), lambda s, c: (0, 0, 0))],
        out_specs=[shared] * 6,
        scratch_shapes=[pltpu.VMEM((nseq * N_PAIRS, PAIR, PAIR), F32)],
        compiler_params=_cparams(("parallel", "arbitrary")),
    )(r, lw, k, v, kk, b, dy, s0, _tri_ones(rev, nseq))


def _block_diag2(w):
    z = jnp.zeros_like(w[0])
    return jnp.concatenate([jnp.concatenate([w[0], z], axis=1), jnp.concatenate([z, w[1]], axis=1)], axis=0)


def _pad_in_cols(a):
    z = jnp.zeros(a.shape[:-1] + (SHIFT_PAD - SHIFT_COLS,), a.dtype)
    return jnp.concatenate([a[..., :SHIFT_COLS], z, a[..., SHIFT_COLS:]], axis=-1)


def _follow(small, token):
    return small if token is None else small + token[0:1, 0:1]


def _local_step(x, target, wts, *, tt, start_token=None, more_weights=None, grads_ready=None, small_ready=None):
    bsz, seq, _ = x.shape
    n_tok = bsz * seq
    row = lambda a: a.reshape(1, -1).astype(F32)
    x0 = x.reshape(n_tok, D_MODEL)
    tgt = target.reshape(n_tok, D_MODEL)
    ln = {k: row(wts[k]) for k in ("ln1_g", "ln1_b", "ln2_g", "ln2_b", "ln3_g", "ln3_b")}
    if grads_ready is None:
        grads_ready = lambda names, slabs: None

    w1i = wts["ffn1_w_in"]
    h1, act1 = _ffn_in(x0, w1i, tm=TM_FFN, after=start_token, name="ffn1_in")
    if more_weights is not None:
        wts = {**wts, **more_weights("ffn1_out", act1)}
    w1o = wts["ffn1_w_out"]
    z1, x1, x1b = _mm_ln([act1], w1o, x0, ln["ln1_g"], ln["ln1_b"], 0.5, tm=TM_LN, name="ffn1_out_ln1")
    if more_weights is not None:
        wts = {**wts, **more_weights("mix", x1b)}
    win = _pad_in_cols(wts["w_in"])
    zpad = jnp.zeros((1, SHIFT_PAD - SHIFT_COLS), F32)
    mu_p = jnp.concatenate([row(wts["mu_prev"]), zpad], axis=1)
    mu_n = jnp.concatenate([row(wts["mu_next"]), zpad], axis=1)
    w2b, a2b = _block_diag2(wts["w2"]), _block_diag2(wts["a2"])
    w0c, a0c = row(wts["w0"]), row(wts["a0"])
    g2p = jnp.concatenate([wts["g2"], jnp.zeros((GATE_PAD - GATE_LORA, RW), F32)], axis=0)
    k_k, k_a, r_k = row(wts["k_k"]), row(wts["k_a"]), row(wts["r_k"])
    lnx_g, lnx_b = row(wts["lnx_g"]), row(wts["lnx_b"])
    cdw, cb, clg, clb = wts["conv_dw"], row(wts["conv_b"]), row(wts["conv_ln_g"]), row(wts["conv_ln_b"])
    small = (mu_p, mu_n, w2b, w0c, a2b, a0c, g2p, k_k, k_a)
    seq3 = lambda a: a.reshape(bsz, seq, a.shape[-1])
    flat = lambda a: a.reshape(n_tok, a.shape[-1])

    p = _matmul(x1b, win, name="proj_in")
    r, v, kk, w, kd, b, g = _mix_prep(p, *small, seq=seq, tt=tt, name="mix_prep")
    scan_in = [seq3(a) for a in (r, w, kd, v, kk, b)]
    y0, s_chunks0 = _wkv_chunk_fwd(*scan_in, rev=False, name="wkv_fwd_dir0")
    y1, s_chunks1 = _wkv_chunk_fwd(*scan_in, rev=True, name="wkv_fwd_dir1")
    y0, y1 = flat(y0), flat(y1)
    yr = _mix_post(y0, y1, r, v, kd, g, lnx_g, lnx_b, r_k, tt=tt, name="mix_post")
    yc, yv = _conv_fwd(p, cdw, cb, clg, clb, seq=seq, tt=tt, name="conv_fwd")
    if more_weights is not None:
        wts = {**wts, **more_weights("out", yr)}
    wout, w2i, w2o = wts["w_out"], wts["ffn2_w_in"], wts["ffn2_w_out"]
    z2, x2, x2b = _mm_ln([yr, yv], wout, x1, ln["ln2_g"], ln["ln2_b"], 1.0, tm=TM_LN, name="proj_out_ln2")
    h2, act2 = _ffn_in(x2b, w2i, tm=TM_FFN, name="ffn2_in")

    gr = {}
    slab_rows = lambda a: a.reshape((N_CHIPS, a.shape[0] // N_CHIPS) + a.shape[1:])
    dw_kw = dict(ta=True, out_dtype=BF16)
    dz3, gr["ln3_g"], gr["ln3_b"], loss_part = _mm_ln_loss(act2, w2o, x2, ln["ln3_g"], ln["ln3_b"], tgt, 0.5, tm=TM_LN,
                                                           name="ffn2_out_ln3_loss")
    dh2 = _ffn_out_bwd(dz3, w2o, h2, tm=TM_FFN, name="ffn2_out_dx")
    gr["ffn2_w_out"] = slab_rows(_matmul(act2, dz3, scale=0.5, tm=D_FF // 2, name="ffn2_out_dw", **dw_kw))
    dz2, gr["ln2_g"], gr["ln2_b"] = _mm_nt_res([dh2], w2i, dz3, ln=(z2, ln["ln2_g"], ln["ln2_b"]), tm=TM_FFN,
                                               name="ffn2_in_dx_ln2")
    gr["ffn2_w_in"] = _matmul(x2b, dh2, col_slabs=True, tn=2 * D_FF // N_CHIPS, name="ffn2_in_dw", **dw_kw)
    dmix = _matmul(dz2, wout, tb=True, name="proj_out_dx")
    gr["w_out"] = slab_rows(jnp.concatenate([_matmul(yr, dz2, name="proj_out_dw_rwkv", **dw_kw),
                                             _matmul(yv, dz2, name="proj_out_dw_conv", **dw_kw)], axis=0))
    tok = grads_ready(("ffn2_w_out", "ffn2_w_in", "w_out"), [gr["ffn2_w_out"], gr["ffn2_w_in"], gr["w_out"]])
    dyr, dyv = (dmix, RW, 0), (dmix, RW, 1)
    dy, dr_p, dv_p, dkd_p, dg, gr["lnx_g"], gr["lnx_b"], gr["r_k"] = _mix_post_bwd(
        y0, y1, r, v, kd, g, _follow(lnx_g, tok), lnx_b, r_k, dyr, tt=tt, name="mix_post_bwd")
    dr0, dw0, dkd0, dv0, dk0, db0 = [flat(a) for a in _wkv_chunk_bwd(*scan_in, seq3(dy), s_chunks0, rev=False,
                                                                      name="wkv_bwd_dir0")]
    dr1, dw1, dkd1, dv1, dk1, db1 = [flat(a) for a in _wkv_chunk_bwd(*scan_in, seq3(dy), s_chunks1, rev=True,
                                                                      name="wkv_bwd_dir1")]
    ct_terms = [[dr_p, dr0, dr1], [dv_p, dv0, dv1], [dk0, dk1], [(dw0, dw1)], [dkd_p, (dkd0, dkd1)], [(db0, db1)], [dg]]
    dyc, gr["conv_ln_g"], gr["conv_ln_b"], gr["conv_b"] = _conv_post_bwd(yc, dyv, clg, clb, tt=tt, name="conv_post_bwd")
    dpc, ddw = _conv_bwd(dyc, p, cdw, seq=seq, tt=tt, name="conv_bwd")
    gr["conv_dw"] = ddw[:CONV_K]
    dps, dw2b, dw0c, da2b, da0c, dg2p, gr["k_k"], gr["k_a"] = _mix_prep_bwd(
        p, *small, ct_terms, seq=seq, tt=tt, name="mix_prep_bwd")
    gr["w2"] = jnp.stack([dw2b[:LORA, :RW], dw2b[LORA:, RW:]])
    gr["a2"] = jnp.stack([da2b[:LORA, :RW], da2b[LORA:, RW:]])
    gr["w0"], gr["a0"], gr["g2"] = dw0c.reshape(2, RW), da0c.reshape(2, RW), dg2p[:GATE_LORA]
    dpsh, dmu_p, dmu_n = _shift_bwd(dps, p, mu_p, mu_n, seq=seq, tt=tt, name="shift_bwd")
    gr["mu_prev"], gr["mu_next"] = dmu_p[:, :SHIFT_COLS], dmu_n[:, :SHIFT_COLS]
    dwin = jnp.concatenate([_matmul(x1b, dpsh, name="proj_in_dw_shift", **dw_kw)[:, :SHIFT_COLS],
                            _matmul(x1b, dpc, name="proj_in_dw_conv", **dw_kw)], axis=1)
    gr["w_in"] = jnp.moveaxis(dwin.reshape(D_MODEL, N_CHIPS, IN_COLS // N_CHIPS), 1, 0)
    tok = grads_ready(("w_in",), [gr["w_in"]])
    dz1, gr["ln1_g"], gr["ln1_b"] = _mm_nt_res([dpsh, dpc], win, dz2, ln=(z1, ln["ln1_g"], ln["ln1_b"]), tm=TM_FFN,
                                               after=tok, name="proj_in_dx_ln1")
    gr["loss"] = loss_part
    tok = small_ready(gr, loss_part) if small_ready is not None else None
    dh1 = _ffn_out_bwd(dz1, w1o, h1, tm=TM_FFN, after=tok, name="ffn1_out_dx")
    gr["ffn1_w_out"] = slab_rows(_matmul(act1, dz1, scale=0.5, tm=D_FF // 2, name="ffn1_out_dw", **dw_kw))
    tok = grads_ready(("ffn1_w_out",), [gr["ffn1_w_out"]])
    gr["ffn1_w_in"] = _matmul(x0, dh1, col_slabs=True, tn=2 * D_FF // N_CHIPS, after=tok, name="ffn1_in_dw", **dw_kw)
    tok = grads_ready(("ffn1_w_in",), [gr["ffn1_w_in"]])
    dx0 = _mm_nt_res([dh1], w1i, dz1, tm=TM_FFN, after=tok, name="ffn1_in_dx")
    return dx0.reshape(bsz, seq, D_MODEL), gr


def _mesh_pos():
    return lax.axis_index("x"), lax.axis_index("y"), lax.axis_index("c")


def _other_chips(x, y):
    return [(1 - x, y), (x, 1 - y), (1 - x, 1 - y)]


def _gather_chips(shards, *, name):
    n = len(shards)
    halves = [s.shape[0] // 2 for s in shards]
    assert all(2 * h == s.shape[0] for h, s in zip(halves, shards))

    def body(*refs):
        ins, outs = refs[:n], refs[n:2 * n]
        send_sems, recv_sems, fwd_send_sems, fwd_recv_sems, loc_sems = refs[2 * n:]
        x, y, c = _mesh_pos()
        q = 2 * x + y
        peers = _other_chips(x, y)
        local = [pltpu.make_async_copy(ins[a], outs[a].at[q], loc_sems.at[a]) for a in range(n)]
        for cp in local:
            cp.start()

        def half(a, chip, core):
            return outs[a].at[chip, pl.ds(core * halves[a], halves[a])]

        sends = [pltpu.make_async_remote_copy(ins[a].at[pl.ds(c * halves[a], halves[a])], half(a, q, c),
                                              send_sems.at[a, k], recv_sems.at[a, k],
                                              device_id=(px, py, c), device_id_type=MESH)
                 for a in range(n) for k, (px, py) in enumerate(peers)]
        for cp in sends:
            cp.start()
        passed = []
        for a in range(n):
            for k, (px, py) in enumerate(peers):
                mine = half(a, 2 * px + py, c)
                pltpu.make_async_remote_copy(mine, mine, send_sems.at[a, k], recv_sems.at[a, k],
                                             device_id=(px, py, c), device_id_type=MESH).wait_recv()
                cp = pltpu.make_async_remote_copy(mine, mine, fwd_send_sems.at[a, k], fwd_recv_sems.at[a, k],
                                                  device_id=(x, y, 1 - c), device_id_type=MESH)
                cp.start()
                passed.append(cp)
        for a in range(n):
            for k, (px, py) in enumerate(peers):
                theirs = half(a, 2 * px + py, 1 - c)
                pltpu.make_async_remote_copy(theirs, theirs, fwd_send_sems.at[a, k], fwd_recv_sems.at[a, k],
                                             device_id=(x, y, 1 - c), device_id_type=MESH).wait_recv()
        for cp in sends + passed:
            cp.wait_send()
        for cp in local:
            cp.wait()

    any_spec = pl.BlockSpec(memory_space=pl.ANY)
    return pl.pallas_call(
        body, name=name,
        out_shape=[jax.ShapeDtypeStruct((N_CHIPS,) + s.shape, s.dtype) for s in shards],
        in_specs=[any_spec] * n, out_specs=[any_spec] * n,
        scratch_shapes=[pltpu.SemaphoreType.DMA((n, 3))] * 4 + [pltpu.SemaphoreType.DMA((n,))],
        compiler_params=pltpu.CompilerParams(has_side_effects=True),
    )(*shards)


HBM_SPEC = pl.BlockSpec(memory_space=pltpu.HBM)
SEM_SPEC = pl.BlockSpec(memory_space=pltpu.SEMAPHORE)
ANY_SPEC = pl.BlockSpec(memory_space=pl.ANY)
SIDE_EFFECT = pltpu.SideEffectType.DATAFLOW_SIDE_EFFECTING


def _chip_copies(src_refs, land_refs, send_sems, recv_sems, scatter, arriving=False):
    x, y, c = _mesh_pos()
    cps = []
    for a, (src, land) in enumerate(zip(src_refs, land_refs)):
        for k, (px, py) in enumerate(_other_chips(x, y)):
            slot = k if scatter else (2 * px + py if arriving else 2 * x + y)
            cps.append(pltpu.make_async_remote_copy(src.at[2 * px + py] if scatter else src, land.at[slot],
                                                    send_sems.at[3 * a + k], recv_sems.at[3 * a + k],
                                                    device_id=(px, py, c), device_id_type=MESH))
    return cps


def _exchange_start(srcs, *, scatter, after, name):
    n = len(srcs)
    lands = [lax.empty((3,) + s.shape[1:] if scatter else (N_CHIPS,) + s.shape, s.dtype) for s in srcs]

    def body(*refs):
        src_refs, land_refs = refs[:n], refs[n:2 * n]
        send_sems, recv_sems = refs[2 * n + 1:2 * n + 3]
        token = refs[-1]
        for cp in _chip_copies(src_refs, land_refs, send_sems, recv_sems, scatter):
            cp.start()
        token[...] = jnp.zeros_like(token)

    hbm = lambda a: pltpu.with_memory_space_constraint(a, pltpu.HBM)
    outs = pl.pallas_call(
        body, name=name,
        out_shape=(pltpu.SemaphoreType.DMA((3 * n,)), pltpu.SemaphoreType.DMA((3 * n,)),
                   *[pltpu.HBM(a.shape, a.dtype) for a in srcs + lands], jax.ShapeDtypeStruct((8, LANES), F32)),
        in_specs=[HBM_SPEC] * (2 * n) + [ANY_SPEC],
        out_specs=(SEM_SPEC, SEM_SPEC, *[HBM_SPEC] * (2 * n), pl.BlockSpec(memory_space=pltpu.VMEM)),
        input_output_aliases={i: 2 + i for i in range(2 * n)},
        compiler_params=pltpu.CompilerParams(has_side_effects=SIDE_EFFECT),
    )(*[hbm(a) for a in srcs + lands], after)
    return outs[0], outs[1], list(outs[2:2 + n]), list(outs[2 + n:2 + 2 * n]), outs[-1]


def _exchange_wait(started, *, scatter, after, name):
    send_sems, recv_sems, srcs, lands, _ = started
    n = len(srcs)

    def body(*refs):
        src_refs, land_refs = refs[:n], refs[n:2 * n]
        send_s, recv_s = refs[2 * n:2 * n + 2]
        for cp in _chip_copies(src_refs, land_refs, send_s, recv_s, scatter, arriving=True):
            cp.wait_send()
            cp.wait_recv()

    outs = pl.pallas_call(
        body, name=name,
        out_shape=tuple(pltpu.HBM(a.shape, a.dtype) for a in srcs + lands),
        in_specs=[HBM_SPEC] * (2 * n) + [SEM_SPEC, SEM_SPEC, ANY_SPEC],
        out_specs=tuple([HBM_SPEC] * (2 * n)),
        input_output_aliases={i: i for i in range(2 * n)},
        compiler_params=pltpu.CompilerParams(has_side_effects=SIDE_EFFECT),
    )(*srcs, *lands, send_sems, recv_sems, after)
    return list(outs[:n]), list(outs[n:])


def _by_chip(own, land):
    xi, yi, _ = _mesh_pos()
    return lax.dynamic_update_index_in_dim(land, own, 2 * xi + yi, 0)


def _swap_sibling(arrs, *, name):
    n = len(arrs)

    def body(*refs):
        ins, outs = refs[:n], refs[n:2 * n]
        send_sems, recv_sems = refs[2 * n:]
        x, y, c = _mesh_pos()
        cps = [pltpu.make_async_remote_copy(ins[a], outs[a], send_sems.at[a], recv_sems.at[a],
                                            device_id=(x, y, 1 - c), device_id_type=MESH) for a in range(n)]
        for cp in cps:
            cp.start()
        for cp in cps:
            cp.wait_recv()
        for cp in cps:
            cp.wait_send()

    any_spec = pl.BlockSpec(memory_space=pl.ANY)
    return pl.pallas_call(
        body, name=name,
        out_shape=[jax.ShapeDtypeStruct(s.shape, s.dtype) for s in arrs],
        in_specs=[any_spec] * n, out_specs=[any_spec] * n,
        scratch_shapes=[pltpu.SemaphoreType.DMA((n,)), pltpu.SemaphoreType.DMA((n,))],
        compiler_params=pltpu.CompilerParams(has_side_effects=True),
    )(*arrs)


def _device_copies(v_ref, land_ref, send_sems, recv_sems, arriving=False):
    x, y, c = _mesh_pos()
    me = 4 * x + 2 * y + c
    cps = []
    for m in range(1, 8):
        px, py, pc = (x + ((m >> 2) & 1)) % 2, (y + ((m >> 1) & 1)) % 2, (c + (m & 1)) % 2
        slot = 4 * px + 2 * py + pc if arriving else me
        cps.append(pltpu.make_async_remote_copy(v_ref, land_ref.at[slot], send_sems.at[m - 1], recv_sems.at[m - 1],
                                                device_id=(px, py, pc), device_id_type=MESH))
    return cps


def _allsum_start(vec, *, after, name):
    land = lax.empty((8,) + vec.shape, F32)

    def body(v_ref, land_ref, _after, send_sems, recv_sems, v_thru, land_thru, token):
        for cp in _device_copies(v_ref, land_ref, send_sems, recv_sems):
            cp.start()
        token[...] = jnp.zeros_like(token)

    hbm = lambda a: pltpu.with_memory_space_constraint(a, pltpu.HBM)
    return pl.pallas_call(
        body, name=name,
        out_shape=(pltpu.SemaphoreType.DMA((7,)), pltpu.SemaphoreType.DMA((7,)), pltpu.HBM(vec.shape, F32),
                   pltpu.HBM(land.shape, F32), jax.ShapeDtypeStruct((8, LANES), F32)),
        in_specs=[HBM_SPEC, HBM_SPEC, ANY_SPEC],
        out_specs=(SEM_SPEC, SEM_SPEC, HBM_SPEC, HBM_SPEC, pl.BlockSpec(memory_space=pltpu.VMEM)),
        input_output_aliases={0: 2, 1: 3},
        compiler_params=pltpu.CompilerParams(has_side_effects=SIDE_EFFECT),
    )(hbm(vec), hbm(land), after)


def _allsum_wait(started, *, after, name):
    send_sems, recv_sems, vec, land, _ = started

    def body(v_ref, land_ref, send_s, recv_s, _after, v_dead, got):
        for cp in _device_copies(v_ref, land_ref, send_s, recv_s, arriving=True):
            cp.wait_send()
            cp.wait_recv()

    vec, land = pl.pallas_call(
        body, name=name,
        out_shape=(pltpu.HBM(vec.shape, F32), pltpu.HBM(land.shape, F32)),
        in_specs=[HBM_SPEC, HBM_SPEC, SEM_SPEC, SEM_SPEC, ANY_SPEC],
        out_specs=(HBM_SPEC, HBM_SPEC),
        input_output_aliases={0: 0, 1: 1},
        compiler_params=pltpu.CompilerParams(has_side_effects=SIDE_EFFECT),
    )(vec, land, send_sems, recv_sems, after)
    xi, yi, ci = _mesh_pos()
    every = lax.dynamic_update_index_in_dim(land, vec, 4 * xi + 2 * yi + ci, 0)

    def add(e_ref, o_ref):
        acc = e_ref[0]
        for d in range(1, 8):
            acc = acc + e_ref[d]
        o_ref[...] = acc

    vm = pl.BlockSpec(memory_space=pltpu.VMEM)
    return pl.pallas_call(add, name=name + "_sum", out_shape=jax.ShapeDtypeStruct(vec.shape, F32), in_specs=[vm],
                          out_specs=vm, compiler_params=_cparams())(every)


def _adamw(w, g, m, v):
    m = ADAM_B1 * m + (1.0 - ADAM_B1) * g
    v = ADAM_B2 * v + (1.0 - ADAM_B2) * (g * g)
    m_hat = m / (1.0 - ADAM_B1 ** ADAM_STEP)
    v_hat = v / (1.0 - ADAM_B2 ** ADAM_STEP)
    delta = -ADAM_LR * (m_hat / (jnp.sqrt(v_hat) + ADAM_EPS) + ADAM_WD * w)
    return delta, m, v


def _sum4(mine, land, *, name):
    rows, cols = mine.shape
    tr = _pick_rows(rows)

    def body(a_ref, l_ref, o_ref):
        o_ref[...] = (a_ref[...].astype(F32) + l_ref[0].astype(F32)) + (l_ref[1].astype(F32) + l_ref[2].astype(F32))

    return pl.pallas_call(
        body, name=name, out_shape=jax.ShapeDtypeStruct((rows, cols), F32), grid=(rows // tr,),
        in_specs=[pl.BlockSpec((tr, cols), lambda i: (i, 0)), pl.BlockSpec((3, tr, cols), lambda i: (0, i, 0))],
        out_specs=pl.BlockSpec((tr, cols), lambda i: (i, 0)),
        compiler_params=_cparams(("parallel",)),
    )(mine, land)


def _pick_rows(rows, want=256):
    for t in range(min(want, rows) // 8 * 8, 0, -8):
        if rows % t == 0:
            return t
    return rows


def _sum_adam(h_mine, h_sib, w, m, v, *, name):
    rows, cols = w.shape
    tr = _pick_rows(rows)

    def body(a_ref, b_ref, w_ref, m_ref, v_ref, g_o, d_o, m_o, v_o):
        g = a_ref[...] + b_ref[...]
        d, mn, vn = _adamw(w_ref[...], g, m_ref[...], v_ref[...])
        g_o[...], d_o[...], m_o[...], v_o[...] = g, d, mn, vn

    spec = pl.BlockSpec((tr, cols), lambda i: (i, 0))
    return pl.pallas_call(
        body, name=name, out_shape=[jax.ShapeDtypeStruct((rows, cols), F32)] * 4, grid=(rows // tr,),
        in_specs=[spec] * 5, out_specs=[spec] * 4, compiler_params=_cparams(("parallel",)),
    )(h_mine, h_sib, w, m, v)


def _adam_rows(w, g, m, v, *, name):
    def body(w_ref, g_ref, m_ref, v_ref, d_o, m_o, v_o):
        d_o[...], m_o[...], v_o[...] = _adamw(w_ref[...], g_ref[...], m_ref[...], v_ref[...])

    vm = pl.BlockSpec(memory_space=pltpu.VMEM)
    return pl.pallas_call(
        body, name=name, out_shape=[jax.ShapeDtypeStruct(w.shape, F32)] * 3,
        in_specs=[vm] * 4, out_specs=[vm] * 3, compiler_params=_cparams(),
    )(w, g, m, v)


def _size(shape):
    size = 1
    for d in shape:
        size *= d
    return size


def _pack_rows(arrs):
    blocks = []
    for a in arrs:
        flat = a.reshape(-1).astype(F32)
        flat = jnp.concatenate([flat, jnp.zeros((-flat.shape[0] % (8 * LANES),), F32)])
        blocks.append(flat.reshape(-1, LANES))
    return jnp.concatenate(blocks, axis=0)


def _unpack_rows(packed, shapes):
    out, row = [], 0
    for s in shapes:
        rows = -(-_size(s) // (8 * LANES)) * 8
        out.append(packed[row:row + rows].reshape(-1)[:_size(s)].reshape(s))
        row += rows
    return out


WEIGHTS = ['ffn1_w_in', 'ffn1_w_out', 'w_in', 'mu_prev', 'mu_next', 'w0', 'w2', 'a0', 'a2', 'g2', 'k_k', 'k_a', 'r_k',
           'lnx_g', 'lnx_b', 'conv_dw', 'conv_b', 'conv_ln_g', 'conv_ln_b', 'w_out', 'ffn2_w_in', 'ffn2_w_out',
           'ln1_g', 'ln1_b', 'ln2_g', 'ln2_b', 'ln3_g', 'ln3_b']
COL_SHARDED = ('ffn1_w_in', 'w_in', 'ffn2_w_in')
ROW_SHARDED = ('ffn1_w_out', 'w_out', 'ffn2_w_out')
BIG = COL_SHARDED + ROW_SHARDED
SMALL_SHARDED = ('w0', 'w2', 'a0', 'a2', 'g2', 'conv_dw')
REPLICATED = tuple(n for n in WEIGHTS if n not in BIG + SMALL_SHARDED)


def _train_step(x, target, w, m, v, *, tt):
    xi, yi, _ = _mesh_pos()
    q = 2 * xi + yi

    later = {"ffn1_out": ("ffn1_w_out",), "mix": ("w_in",) + SMALL_SHARDED, "out": ("w_out", "ffn2_w_in", "ffn2_w_out")}
    shard = lambda n: w[n][0].astype(BF16) if n in BIG else w[n][0]
    small_names = REPLICATED + SMALL_SHARDED

    def whole(n, slabs):
        if n in ROW_SHARDED:
            return slabs.reshape((-1,) + slabs.shape[2:])
        if n in ("ffn1_w_in", "ffn2_w_in"):
            return slabs
        return jnp.moveaxis(slabs, 0, -2).reshape(slabs.shape[1:-1] + (N_CHIPS * slabs.shape[-1],))

    full = {n: w[n][0] for n in REPLICATED}
    first = _gather_chips([shard("ffn1_w_in")], name="gather_ffn1_in")
    full["ffn1_w_in"] = whole("ffn1_w_in", first[0])
    started, token = {}, first[0]
    for stage, names in later.items():
        started[stage] = _exchange_start([shard(n) for n in names], scatter=False, after=token,
                                         name="gather_%s_start" % stage)
        token = started[stage][-1]

    def more_weights(stage, after):
        own, land = _exchange_wait(started[stage], scatter=False, after=after, name="gather_%s_wait" % stage)
        got = {n: whole(n, _by_chip(o, l)) for n, o, l in zip(later[stage], own, land)}
        full.update(got)
        return got

    small_sent = []

    def small_ready(gr, loss_part):
        vec = _pack_rows([gr[n] for n in small_names] + [loss_part[0:1, 0:1]])
        small_sent.append(_allsum_start(vec, after=vec, name="reduce_small_start"))
        return small_sent[0][-1]

    sent = []

    def grads_ready(names, slabs):
        started = _exchange_start(slabs, scatter=True, after=slabs[0], name="scatter_%s_start" % names[0])
        sent.append((names, started))
        return started[-1]

    grad_x, gr = _local_step(x, target, full, tt=tt, start_token=token, more_weights=more_weights,
                             grads_ready=grads_ready, small_ready=small_ready)

    halves = {}
    for names, started in sent:
        stacks, landed = _exchange_wait(started, scatter=True, after=grad_x, name="scatter_%s_wait" % names[0])
        for n, s, l in zip(names, stacks, landed):
            halves[n] = _sum4(lax.dynamic_index_in_dim(s, q, 0, keepdims=False), l, name="sum4_" + n)
    halves = [halves[n] for n in BIG]
    sib = _swap_sibling(halves, name="swap_halves")
    grad, delta, new_m, new_v = {}, {}, {}, {}
    for n, h, hs in zip(BIG, halves, sib):
        outs = _sum_adam(h, hs, w[n][0], m[n][0], v[n][0], name="adam_" + n)
        grad[n], delta[n], new_m[n], new_v[n] = [o[None] for o in outs]

    small_full_shapes = [full[n].shape for n in small_names]
    red = _allsum_wait(small_sent[0], after=grad_x, name="reduce_small_wait")
    *red, loss = _unpack_rows(red, small_full_shapes + [()])
    red = dict(zip(small_names, red))
    gsm = {}
    for n in REPLICATED:
        gsm[n] = red[n].reshape(w[n].shape)
    for n in SMALL_SHARDED:
        width = w[n].shape[-1]
        gsm[n] = lax.dynamic_slice_in_dim(red[n], q * width, width, axis=red[n].ndim - 1).reshape(w[n].shape)
    shapes = [w[n].shape for n in small_names]
    d_p, m_p, v_p = _adam_rows(_pack_rows([w[n] for n in small_names]), _pack_rows([gsm[n] for n in small_names]),
                               _pack_rows([m[n] for n in small_names]), _pack_rows([v[n] for n in small_names]),
                               name="adam_small")
    for n, dd, mm, vv in zip(small_names, _unpack_rows(d_p, shapes), _unpack_rows(m_p, shapes), _unpack_rows(v_p, shapes)):
        grad[n], delta[n], new_m[n], new_v[n] = gsm[n], dd, mm, vv
    return loss, grad_x, grad, delta, new_m, new_v


def kernel(x, ffn1_w_in, ffn1_w_out, w_in, mu_prev, mu_next, w0, w2, a0, a2, g2, k_k, k_a, r_k, lnx_g, lnx_b, conv_dw, conv_b, conv_ln_g, conv_ln_b, w_out, ffn2_w_in, ffn2_w_out, ln1_g, ln1_b, ln2_g, ln2_b, ln3_g, ln3_b, loss_target, m_ffn1_w_in, m_ffn1_w_out, m_w_in, m_mu_prev, m_mu_next, m_w0, m_w2, m_a0, m_a2, m_g2, m_k_k, m_k_a, m_r_k, m_lnx_g, m_lnx_b, m_conv_dw, m_conv_b, m_conv_ln_g, m_conv_ln_b, m_w_out, m_ffn2_w_in, m_ffn2_w_out, m_ln1_g, m_ln1_b, m_ln2_g, m_ln2_b, m_ln3_g, m_ln3_b, v_ffn1_w_in, v_ffn1_w_out, v_w_in, v_mu_prev, v_mu_next, v_w0, v_w2, v_a0, v_a2, v_g2, v_k_k, v_k_a, v_r_k, v_lnx_g, v_lnx_b, v_conv_dw, v_conv_b, v_conv_ln_g, v_conv_ln_b, v_w_out, v_ffn2_w_in, v_ffn2_w_out, v_ln1_g, v_ln1_b, v_ln2_g, v_ln2_b, v_ln3_g, v_ln3_b):
    args = dict(locals())
    w = {n: args[n] for n in WEIGHTS}
    m = {n: args["m_" + n] for n in WEIGHTS}
    v = {n: args["v_" + n] for n in WEIGHTS}
    seq = x.shape[1]
    loss, grad_x, grad, delta, new_m, new_v = _train_step(x, loss_target, w, m, v, tt=min(256, seq))
    return (loss, grad_x, *[grad[n] for n in WEIGHTS], *[delta[n] for n in WEIGHTS],
            *[new_m[n] for n in WEIGHTS], *[new_v[n] for n in WEIGHTS])
```

```python
import functools

import jax
import jax.numpy as jnp
from jax import lax
from jax.experimental import pallas as pl
from jax.experimental.pallas import tpu as pltpu

F32 = jnp.float32
BF16 = jnp.bfloat16

D_MODEL = 1024
RW = 512
HEAD = 64
CW = 512
CONV_K = 31
CONV_ROWS = 32
SHIFT_ROWS = 16
CONV_PAD = 15
D_FF = 2816
LORA = 64
GATE_LORA = 160
GATE_PAD = 256
SHIFT_COLS = 1952
SHIFT_PAD = 2048
IN_COLS = 2976
IN_PAD = 3072
LN_EPS = 1e-5
GN_EPS = 64e-5
NORM_EPS = 1e-12
ALPHA = 2.0 ** 0.25
DECAY_SCALE = 0.6065306597126334
ADAM_LR, ADAM_B1, ADAM_B2, ADAM_EPS, ADAM_WD, ADAM_STEP = 0.001, 0.9, 0.999, 1e-08, 0.01, 10
N_CHIPS = 4
VMEM_LIMIT = 56 * 1024 * 1024
TM_FFN = 256
TM_LN = 512

MESH = pl.DeviceIdType.MESH


def _cparams(sem=None, **kw):
    return pltpu.CompilerParams(dimension_semantics=sem, vmem_limit_bytes=VMEM_LIMIT, **kw)


LANES = 128


def _pick_tile(dim, want):
    for t in range(min(want, dim) // LANES * LANES, 0, -LANES):
        if dim % t == 0:
            return t
    return dim


def _after_operand(after):
    return ([], []) if after is None else ([pl.BlockSpec(memory_space=pl.ANY)], [after])


def _matmul(a, b, *, ta=False, tb=False, out_dtype=F32, tm=1024, tn=1024, tk=1024, scale=1.0, col_slabs=False,
            after=None, name):
    after_specs, after_args = _after_operand(after)
    if ta:
        k_dim, m_dim = a.shape
    else:
        m_dim, k_dim = a.shape
    n_dim = b.shape[0] if tb else b.shape[1]
    tm, tn, tk = _pick_tile(m_dim, tm), _pick_tile(n_dim, tn), _pick_tile(k_dim, tk)
    assert m_dim % tm == 0 and n_dim % tn == 0 and k_dim % tk == 0, (name, a.shape, b.shape, tm, tn, tk)
    nk = k_dim // tk
    dims = (((0,) if ta else (1,), (1,) if tb else (0,)), ((), ()))
    if col_slabs:
        out_shape = jax.ShapeDtypeStruct((n_dim // tn, m_dim, tn), out_dtype)
        out_spec = pl.BlockSpec((None, tm, tn), lambda i, j, k: (j, i, 0))
    else:
        out_shape = jax.ShapeDtypeStruct((m_dim, n_dim), out_dtype)
        out_spec = pl.BlockSpec((tm, tn), lambda i, j, k: (i, j))

    def body(a_ref, b_ref, *rest):
        o_ref, acc_ref = rest[-2:]
        kk = pl.program_id(2)

        @pl.when(kk == 0)
        def _():
            acc_ref[...] = jnp.zeros_like(acc_ref)

        acc_ref[...] += lax.dot_general(a_ref[...].astype(BF16), b_ref[...].astype(BF16), dims,
                                        preferred_element_type=F32)

        @pl.when(kk == nk - 1)
        def _():
            o_ref[...] = (acc_ref[...] * scale).astype(o_ref.dtype)

    a_spec = pl.BlockSpec((tk, tm), lambda i, j, k: (k, i)) if ta else pl.BlockSpec((tm, tk), lambda i, j, k: (i, k))
    b_spec = pl.BlockSpec((tn, tk), lambda i, j, k: (j, k)) if tb else pl.BlockSpec((tk, tn), lambda i, j, k: (k, j))
    return pl.pallas_call(
        body, name=name,
        out_shape=out_shape,
        grid=(m_dim // tm, n_dim // tn, nk),
        in_specs=[a_spec, b_spec] + after_specs,
        out_specs=out_spec,
        scratch_shapes=[pltpu.VMEM((tm, tn), F32)],
        compiler_params=_cparams(("parallel", "parallel", "arbitrary")),
    )(a, b, *after_args)


def _whole(shape):
    nd = len(shape)
    return pl.BlockSpec(shape, lambda i: (0,) * nd)


def _ffn_in(x, w, *, tm, after=None, name):
    n_tok = x.shape[0]
    sw = w.shape[2]
    tm = min(tm, n_tok)

    after_specs, after_args = _after_operand(after)

    def body(x_ref, w_ref, *rest):
        h_ref, a_ref = rest[-2:]
        xb = x_ref[...].astype(BF16)
        for s in range(2):
            g = jnp.dot(xb, w_ref[s], preferred_element_type=F32)
            u = jnp.dot(xb, w_ref[s + 2], preferred_element_type=F32)
            h_ref[:, s * sw:(s + 1) * sw] = g.astype(BF16)
            h_ref[:, (s + 2) * sw:(s + 3) * sw] = u.astype(BF16)
            a_ref[:, s * sw:(s + 1) * sw] = (_silu(g) * u).astype(BF16)

    return pl.pallas_call(
        body, name=name,
        out_shape=[jax.ShapeDtypeStruct((n_tok, 2 * D_FF), BF16), jax.ShapeDtypeStruct((n_tok, D_FF), BF16)],
        grid=(n_tok // tm,),
        in_specs=[pl.BlockSpec((tm, D_MODEL), lambda i: (i, 0)), _whole(w.shape)] + after_specs,
        out_specs=[pl.BlockSpec((tm, 2 * D_FF), lambda i: (i, 0)), pl.BlockSpec((tm, D_FF), lambda i: (i, 0))],
        compiler_params=_cparams(("parallel",)),
    )(x, w, *after_args)


def _mm_ln(a_list, w, xres, g, b, fscale, *, tm, name):
    n_tok = xres.shape[0]
    tm = min(tm, n_tok)
    na = len(a_list)

    def body(*refs):
        a_refs = refs[:na]
        w_ref, x_ref, g_ref, b_ref, z_o, y_o, yb_o = refs[na:]
        f, off = None, 0
        for a_ref in a_refs:
            k = a_ref.shape[1]
            t = jnp.dot(a_ref[...].astype(BF16), w_ref[off:off + k, :], preferred_element_type=F32)
            f = t if f is None else f + t
            off += k
        z = ALPHA * x_ref[...] + fscale * f
        y = _layer_norm(z, g_ref[...], b_ref[...])
        z_o[...] = z
        y_o[...] = y
        yb_o[...] = y.astype(BF16)

    tile = pl.BlockSpec((tm, D_MODEL), lambda i: (i, 0))
    return pl.pallas_call(
        body, name=name,
        out_shape=[jax.ShapeDtypeStruct((n_tok, D_MODEL), F32)] * 2 + [jax.ShapeDtypeStruct((n_tok, D_MODEL), BF16)],
        grid=(n_tok // tm,),
        in_specs=[pl.BlockSpec((tm, a.shape[1]), lambda i: (i, 0)) for a in a_list]
        + [_whole(w.shape), tile, _whole(g.shape), _whole(b.shape)],
        out_specs=[tile, tile, tile],
        compiler_params=_cparams(("parallel",)),
    )(*a_list, w, xres, g, b)


def _mm_ln_loss(a, w, xres, g, b, target, fscale, *, tm, name):
    n_tok = xres.shape[0]
    tm = min(tm, n_tok)

    def body(a_ref, w_ref, x_ref, g_ref, b_ref, t_ref, dz_o, dg_o, db_o, loss_o):
        i = pl.program_id(0)
        z = ALPHA * x_ref[...] + fscale * jnp.dot(a_ref[...].astype(BF16), w_ref[...], preferred_element_type=F32)
        y, vjp = jax.vjp(_layer_norm, z, g_ref[...], b_ref[...])
        e = y - t_ref[...]
        dz, dg, db = vjp(e * (1.0 / D_MODEL))

        @pl.when(i == 0)
        def _():
            dg_o[...] = jnp.zeros_like(dg_o)
            db_o[...] = jnp.zeros_like(db_o)
            loss_o[...] = jnp.zeros_like(loss_o)
        dz_o[...] = dz
        dg_o[...] += dg
        db_o[...] += db
        loss_o[...] += 0.5 * jnp.sum(jnp.mean(e * e, axis=-1, keepdims=True), axis=0, keepdims=True)

    tile = pl.BlockSpec((tm, D_MODEL), lambda i: (i, 0))
    row = pl.BlockSpec((1, D_MODEL), lambda i: (0, 0))
    return pl.pallas_call(
        body, name=name,
        out_shape=[jax.ShapeDtypeStruct((n_tok, D_MODEL), F32), jax.ShapeDtypeStruct((1, D_MODEL), F32),
                   jax.ShapeDtypeStruct((1, D_MODEL), F32), jax.ShapeDtypeStruct((8, LANES), F32)],
        grid=(n_tok // tm,),
        in_specs=[pl.BlockSpec((tm, a.shape[1]), lambda i: (i, 0)), _whole(w.shape), tile, row, row, tile],
        out_specs=[tile, row, row, pl.BlockSpec((8, LANES), lambda i: (0, 0))],
        compiler_params=_cparams(("arbitrary",)),
    )(a, w, xres, g, b, target)


def _ffn_out_bwd(dz, w, h, *, tm, after=None, name):
    n_tok = dz.shape[0]
    tm = min(tm, n_tok)
    cw = D_FF // 2
    after_specs, after_args = _after_operand(after)

    def body(dz_ref, w_ref, h_ref, *rest):
        dh_ref = rest[-1]
        dzb = dz_ref[...].astype(BF16)
        for s in range(2):
            dact = 0.5 * lax.dot_general(dzb, w_ref[s * cw:(s + 1) * cw, :], (((1,), (1,)), ((), ())),
                                         preferred_element_type=F32)
            gate = h_ref[:, s * cw:(s + 1) * cw].astype(F32)
            up = h_ref[:, D_FF + s * cw:D_FF + (s + 1) * cw].astype(F32)
            sg = _sigmoid(gate)
            dh_ref[:, s * cw:(s + 1) * cw] = (dact * up * sg * (1.0 + gate * (1.0 - sg))).astype(BF16)
            dh_ref[:, D_FF + s * cw:D_FF + (s + 1) * cw] = (dact * gate * sg).astype(BF16)

    wide = pl.BlockSpec((tm, 2 * D_FF), lambda i: (i, 0))
    return pl.pallas_call(
        body, name=name,
        out_shape=jax.ShapeDtypeStruct((n_tok, 2 * D_FF), BF16),
        grid=(n_tok // tm,),
        in_specs=[pl.BlockSpec((tm, D_MODEL), lambda i: (i, 0)), _whole(w.shape), wide] + after_specs,
        out_specs=wide,
        compiler_params=_cparams(("parallel",)),
    )(dz, w, h, *after_args)


def _mm_nt_res(a_list, w, dz, *, tm, ln=None, after=None, name):
    n_tok = dz.shape[0]
    tm = min(tm, n_tok)
    na = len(a_list)
    nt = (((1,), (1,)), ((), ()))
    after_specs, after_args = _after_operand(after)
    n_out = 1 if ln is None else 3

    def body(*refs):
        a_refs = refs[:na]
        w_ref, dz_ref, o_ref = refs[na], refs[na + 1], refs[-n_out]
        acc = ALPHA * dz_ref[...]
        if len(w_ref.shape) == 3:
            cw = w_ref.shape[2]
            for s in range(w_ref.shape[0]):
                acc = acc + lax.dot_general(a_refs[0][:, s * cw:(s + 1) * cw], w_ref[s], nt, preferred_element_type=F32)
        else:
            off = 0
            for a_ref in a_refs:
                k = a_ref.shape[1]
                acc = acc + lax.dot_general(a_ref[...], w_ref[:, off:off + k], nt, preferred_element_type=F32)
                off += k
        if ln is None:
            o_ref[...] = acc
            return
        z_ref, g_ref, b_ref = refs[na + 2:na + 5]
        dg_o, db_o = refs[-2:]
        _, vjp = jax.vjp(_layer_norm, z_ref[...], g_ref[...], b_ref[...])
        o_ref[...], dg, db = vjp(acc)

        @pl.when(pl.program_id(0) == 0)
        def _():
            dg_o[...] = jnp.zeros_like(dg_o)
            db_o[...] = jnp.zeros_like(db_o)
        dg_o[...] += dg
        db_o[...] += db

    tile = pl.BlockSpec((tm, D_MODEL), lambda i: (i, 0))
    row = pl.BlockSpec((1, D_MODEL), lambda i: (0, 0))
    out_shape = [jax.ShapeDtypeStruct((n_tok, D_MODEL), F32)]
    ln_specs, ln_args, out_specs = [], [], [tile]
    if ln is not None:
        ln_specs, ln_args = [tile, row, row], list(ln)
        out_shape += [jax.ShapeDtypeStruct((1, D_MODEL), F32)] * 2
        out_specs += [row, row]
    outs = pl.pallas_call(
        body, name=name,
        out_shape=out_shape,
        grid=(n_tok // tm,),
        in_specs=[pl.BlockSpec((tm, a.shape[1]), lambda i: (i, 0)) for a in a_list] + [_whole(w.shape), tile]
        + ln_specs + after_specs,
        out_specs=out_specs,
        compiler_params=_cparams(("parallel",) if ln is None else ("arbitrary",)),
    )(*a_list, w, dz, *ln_args, *after_args)
    return outs[0] if ln is None else outs


def _rowcall(fn, tok_in, full_in, tok_out, acc_out, *, tt, name):
    views = [a if isinstance(a, tuple) else (a, a.shape[1], 0) for a in tok_in]
    tok_in = [a for a, _, _ in views]
    n_tok = tok_in[0].shape[0]
    assert n_tok % tt == 0, (name, n_tok, tt)
    n_ti, n_fi, n_to = len(tok_in), len(full_in), len(tok_out)

    def body(*refs):
        i = pl.program_id(0)
        ins = [r[...] for r in refs[:n_ti + n_fi]]
        outs = fn(i, *ins)
        o_refs = refs[n_ti + n_fi:]
        for r, val in zip(o_refs[:n_to], outs[:n_to]):
            r[...] = val.astype(r.dtype)
        if acc_out:
            @pl.when(i == 0)
            def _():
                for r in o_refs[n_to:]:
                    r[...] = jnp.zeros_like(r)
            for r, val in zip(o_refs[n_to:], outs[n_to:]):
                r[...] += val.reshape(r.shape).astype(F32)

    in_specs = [pl.BlockSpec((tt, width), functools.partial(lambda k, i: (i, k), k)) for _, width, k in views]
    in_specs += [pl.BlockSpec(a.shape, lambda i: (0, 0)) for a in full_in]
    out_specs = [pl.BlockSpec((tt, c), lambda i: (i, 0)) for c, _ in tok_out]
    out_specs += [pl.BlockSpec(s, lambda i: (0, 0)) for s in acc_out]
    out_shape = [jax.ShapeDtypeStruct((n_tok, c), dt) for c, dt in tok_out]
    out_shape += [jax.ShapeDtypeStruct(s, F32) for s in acc_out]
    return pl.pallas_call(
        body, name=name, out_shape=out_shape, grid=(n_tok // tt,), in_specs=in_specs, out_specs=out_specs,
        compiler_params=_cparams(("arbitrary",) if acc_out else ("parallel",)),
    )(*tok_in, *full_in)


@jax.custom_vjp
def _bdot(a, b):
    return jnp.dot(a.astype(BF16), b.astype(BF16), preferred_element_type=F32)


def _bdot_fwd(a, b):
    return _bdot(a, b), (a, b)


def _bdot_bwd(res, g):
    a, b = res
    g16 = g.astype(BF16)
    da = lax.dot_general(g16, b.astype(BF16), (((1,), (1,)), ((), ())), preferred_element_type=F32)
    db = lax.dot_general(a.astype(BF16), g16, (((0,), (0,)), ((), ())), preferred_element_type=F32)
    return da, db


_bdot.defvjp(_bdot_fwd, _bdot_bwd)


def _split16(x):
    hi = x.astype(BF16)
    lo = (x - hi.astype(F32)).astype(BF16)
    return hi, lo


def _segsum_raw(x, e2):
    hi, lo = _split16(x)
    outs = []
    for c in range(x.shape[1] // 256):
        lhs = jnp.concatenate([hi[:, 256 * c:256 * (c + 1)], lo[:, 256 * c:256 * (c + 1)]], axis=1)
        outs.append(jnp.dot(lhs, e2, preferred_element_type=F32))
    return jnp.concatenate(outs, axis=1)


@jax.custom_vjp
def _segsum(x, e2):
    return _segsum_raw(x, e2)


def _segsum_fwd(x, e2):
    return _segsum_raw(x, e2), e2


def _segsum_bwd(e2, g):
    return _segsum_raw(g, e2), jnp.zeros_like(e2)


_segsum.defvjp(_segsum_fwd, _segsum_bwd)


def _head_ones():
    r = lax.broadcasted_iota(jnp.int32, (512, 256), 0) % 256
    c = lax.broadcasted_iota(jnp.int32, (512, 256), 1)
    return (r // HEAD == c // HEAD).astype(BF16)


def _sigmoid(x):
    return 1.0 / (1.0 + jnp.exp(-x))


def _silu(x):
    return x * _sigmoid(x)


def _layer_norm(z, g, b, eps=LN_EPS):
    mu = jnp.mean(z, axis=-1, keepdims=True)
    zc = z - mu
    var = jnp.mean(zc * zc, axis=-1, keepdims=True)
    return zc * lax.rsqrt(var + eps) * g + b


def _prep(ps, w2b, w0c, a2b, a0c, g2p, k_k, k_a, e2):
    r, k, v = ps[:, 0:512], ps[:, 512:1024], ps[:, 1024:1536]
    wd, ad, gd = ps[:, 1536:1664], ps[:, 1664:1792], ps[:, 1792:2048]
    lw = _bdot(jnp.tanh(wd), w2b) + w0c
    decay = -DECAY_SCALE * _sigmoid(lw)
    a = _sigmoid(_bdot(ad, a2b) + a0c)
    g = _bdot(_sigmoid(gd), g2p)
    kkr = k * k_k
    nrm = jnp.sqrt(_segsum(kkr * kkr, e2))
    kk = kkr / jnp.maximum(nrm, NORM_EPS)
    k2 = jnp.concatenate([k, k], axis=1)
    ka2 = jnp.concatenate([k_a, k_a], axis=1)
    kd = k2 * (1.0 + (a - 1.0) * ka2)
    b = jnp.concatenate([kk, kk], axis=1) * a
    return r, v, kk, decay, kd, b, g


def _post(y0, y1, r, v, kd, g, lnx_g, lnx_b, r_k, e2):
    y = y0 + y1
    mu = _segsum(y, e2) * (1.0 / HEAD)
    yc = y - mu
    var = _segsum(yc * yc, e2) * (1.0 / HEAD)
    yn = yc * lax.rsqrt(var + GN_EPS) * lnx_g + lnx_b
    bonus = _segsum(r * (kd[:, :RW] + kd[:, RW:]) * r_k, e2)
    return (yn + bonus * v) * g


def _conv_post(yc, ln_g, ln_b):
    return _silu(_layer_norm(yc, ln_g, ln_b))


def _halo_specs(cols_block, hb, tt, n_tok, col_idx):
    nb = n_tok // hb
    prev = pl.BlockSpec((hb, cols_block), lambda i: (jnp.maximum(i * (tt // hb) - 1, 0), col_idx))
    nxt = pl.BlockSpec((hb, cols_block), lambda i: (jnp.minimum((i + 1) * (tt // hb), nb - 1), col_idx))
    return prev, nxt


def _mix_prep(p, mu_p, mu_n, w2b, w0c, a2b, a0c, g2p, k_k, k_a, *, seq, tt, name):
    n_tok = p.shape[0]
    tps = seq // tt
    e2 = _head_ones()

    def body(p_ref, hp_ref, hn_ref, mup_ref, mun_ref, w2b_ref, w0c_ref, a2b_ref, a0c_ref, g2p_ref, kk_ref, ka_ref,
             e2_ref, r_o, v_o, kk_o, w_o, kd_o, b_o, g_o, ext):
        i = pl.program_id(0)
        first = (i % tps) == 0
        last = (i % tps) == tps - 1
        pv = p_ref[...]
        ext[pl.ds(0, 8), :] = jnp.where(first, 0.0, hp_ref[...])
        ext[pl.ds(8, tt), :] = pv
        ext[pl.ds(8 + tt, 8), :] = jnp.where(last, 0.0, hn_ref[...])
        prev = ext[pl.ds(7, tt), :]
        nxt = ext[pl.ds(9, tt), :]
        ps = pv + mup_ref[...] * (prev - pv) + mun_ref[...] * (nxt - pv)
        outs = _prep(ps, w2b_ref[...], w0c_ref[...], a2b_ref[...], a0c_ref[...], g2p_ref[...], kk_ref[...],
                     ka_ref[...], e2_ref[...])
        for o_ref, val in zip((r_o, v_o, kk_o, w_o, kd_o, b_o, g_o), outs):
            o_ref[...] = val

    hp, hn = _halo_specs(SHIFT_PAD, 8, tt, n_tok, 0)
    fulls = [mu_p, mu_n, w2b, w0c, a2b, a0c, g2p, k_k, k_a, e2]
    widths = (RW, RW, RW, 2 * RW, 2 * RW, 2 * RW, RW)
    return pl.pallas_call(
        body, name=name,
        out_shape=[jax.ShapeDtypeStruct((n_tok, c), F32) for c in widths],
        grid=(n_tok // tt,),
        in_specs=[pl.BlockSpec((tt, SHIFT_PAD), lambda i: (i, 0)), hp, hn]
        + [pl.BlockSpec(a.shape, lambda i: (0, 0)) for a in fulls],
        out_specs=[pl.BlockSpec((tt, c), lambda i: (i, 0)) for c in widths],
        scratch_shapes=[pltpu.VMEM((tt + 16, SHIFT_PAD), F32)],
        compiler_params=_cparams(("parallel",)),
    )(p, p, p, *fulls)


def _mix_prep_bwd(p, mu_p, mu_n, w2b, w0c, a2b, a0c, g2p, k_k, k_a, ct_terms, *, seq, tt, name):
    n_tok = p.shape[0]
    tps = seq // tt
    e2 = _head_ones()
    acc_shapes = [w2b.shape, w0c.shape, a2b.shape, a0c.shape, g2p.shape, k_k.shape, k_a.shape]
    cts = [a for terms in ct_terms for t in terms for a in (t if isinstance(t, tuple) else (t,))]

    def body(p_ref, hp_ref, hn_ref, mup_ref, mun_ref, w2b_ref, w0c_ref, a2b_ref, a0c_ref, g2p_ref, kk_ref, ka_ref,
             e2_ref, *rest):
        ct_refs, dps_o, acc_refs, ext = rest[:len(cts)], rest[len(cts)], rest[len(cts) + 1:-1], rest[-1]
        ct_it = iter(ct_refs)
        ct_vals = []
        for terms in ct_terms:
            total = None
            for t in terms:
                if isinstance(t, tuple):
                    val = jnp.concatenate([next(ct_it)[...] for _ in t], axis=1)
                else:
                    val = next(ct_it)[...]
                total = val if total is None else total + val
            ct_vals.append(total)
        i = pl.program_id(0)
        first = (i % tps) == 0
        last = (i % tps) == tps - 1
        pv = p_ref[...]
        ext[pl.ds(0, 8), :] = jnp.where(first, 0.0, hp_ref[...])
        ext[pl.ds(8, tt), :] = pv
        ext[pl.ds(8 + tt, 8), :] = jnp.where(last, 0.0, hn_ref[...])
        prev = ext[pl.ds(7, tt), :]
        nxt = ext[pl.ds(9, tt), :]
        ps = pv + mup_ref[...] * (prev - pv) + mun_ref[...] * (nxt - pv)
        e2v = e2_ref[...]
        _, vjp = jax.vjp(lambda *a: _prep(*a, e2v), ps, w2b_ref[...], w0c_ref[...], a2b_ref[...], a0c_ref[...],
                         g2p_ref[...], kk_ref[...], ka_ref[...])
        grads = vjp(tuple(ct_vals))
        dps_o[...] = grads[0]

        @pl.when(i == 0)
        def _():
            for r in acc_refs:
                r[...] = jnp.zeros_like(r)
        for r, val in zip(acc_refs, grads[1:]):
            r[...] += val

    hp, hn = _halo_specs(SHIFT_PAD, 8, tt, n_tok, 0)
    fulls = [mu_p, mu_n, w2b, w0c, a2b, a0c, g2p, k_k, k_a, e2]
    return pl.pallas_call(
        body, name=name,
        out_shape=[jax.ShapeDtypeStruct((n_tok, SHIFT_PAD), F32)] + [jax.ShapeDtypeStruct(s, F32) for s in acc_shapes],
        grid=(n_tok // tt,),
        in_specs=[pl.BlockSpec((tt, SHIFT_PAD), lambda i: (i, 0)), hp, hn]
        + [pl.BlockSpec(a.shape, lambda i: (0, 0)) for a in fulls]
        + [pl.BlockSpec((tt, c.shape[1]), lambda i: (i, 0)) for c in cts],
        out_specs=[pl.BlockSpec((tt, SHIFT_PAD), lambda i: (i, 0))] + [pl.BlockSpec(s, lambda i: (0, 0)) for s in acc_shapes],
        scratch_shapes=[pltpu.VMEM((tt + 16, SHIFT_PAD), F32)],
        compiler_params=_cparams(("arbitrary",)),
    )(p, p, p, *fulls, *cts)


def _shift_bwd(dps, p, mu_p, mu_n, *, seq, tt, name):
    n_tok = p.shape[0]
    tps = seq // tt

    def body(d_ref, dhp_ref, dhn_ref, p_ref, php_ref, phn_ref, mup_ref, mun_ref, dp_o, dmup_o, dmun_o, ext):
        i = pl.program_id(0)
        first = (i % tps) == 0
        last = (i % tps) == tps - 1
        mup, mun = mup_ref[...], mun_ref[...]
        rb = min(SHIFT_ROWS, tt)
        ext[pl.ds(0, 8), :] = jnp.where(first, 0.0, dhp_ref[...])
        ext[pl.ds(8, tt), :] = d_ref[...]
        ext[pl.ds(8 + tt, 8), :] = jnp.where(last, 0.0, dhn_ref[...])
        for r0 in range(0, tt, rb):
            dv = d_ref[pl.ds(r0, rb), :]
            dp_o[pl.ds(r0, rb), :] = (dv * (1.0 - mup - mun) + ext[pl.ds(r0 + 9, rb), :] * mup
                                      + ext[pl.ds(r0 + 7, rb), :] * mun).astype(dp_o.dtype)
        ext[pl.ds(0, 8), :] = jnp.where(first, 0.0, php_ref[...])
        ext[pl.ds(8, tt), :] = p_ref[...]
        ext[pl.ds(8 + tt, 8), :] = jnp.where(last, 0.0, phn_ref[...])

        @pl.when(i == 0)
        def _():
            dmup_o[...] = jnp.zeros_like(dmup_o)
            dmun_o[...] = jnp.zeros_like(dmun_o)
        sum_p = jnp.zeros_like(mup)
        sum_n = jnp.zeros_like(mun)
        for r0 in range(0, tt, rb):
            dv, pv = d_ref[pl.ds(r0, rb), :], p_ref[pl.ds(r0, rb), :]
            sum_p = sum_p + jnp.sum(dv * (ext[pl.ds(r0 + 7, rb), :] - pv), axis=0, keepdims=True)
            sum_n = sum_n + jnp.sum(dv * (ext[pl.ds(r0 + 9, rb), :] - pv), axis=0, keepdims=True)
        dmup_o[...] += sum_p
        dmun_o[...] += sum_n

    hp, hn = _halo_specs(SHIFT_PAD, 8, tt, n_tok, 0)
    tile = pl.BlockSpec((tt, SHIFT_PAD), lambda i: (i, 0))
    full = pl.BlockSpec((1, SHIFT_PAD), lambda i: (0, 0))
    return pl.pallas_call(
        body, name=name,
        out_shape=[jax.ShapeDtypeStruct((n_tok, SHIFT_PAD), BF16), jax.ShapeDtypeStruct((1, SHIFT_PAD), F32),
                   jax.ShapeDtypeStruct((1, SHIFT_PAD), F32)],
        grid=(n_tok // tt,),
        in_specs=[tile, hp, hn, tile, hp, hn, full, full],
        out_specs=[tile, full, full],
        scratch_shapes=[pltpu.VMEM((tt + 16, SHIFT_PAD), F32)],
        compiler_params=_cparams(("arbitrary",)),
    )(dps, dps, dps, p, p, p, mu_p, mu_n)


def _mix_post(y0, y1, r, v, kd, g, lnx_g, lnx_b, r_k, *, tt, name):
    e2 = _head_ones()
    return _rowcall(lambda i, *a: (_post(*a),), [y0, y1, r, v, kd, g], [lnx_g, lnx_b, r_k, e2], [(RW, BF16)], [],
                    tt=tt, name=name)[0]


def _mix_post_bwd(y0, y1, r, v, kd, g, lnx_g, lnx_b, r_k, dout, *, tt, name):
    e2 = _head_ones()

    def fn(i, y0v, y1v, rv, vv, kdv, gv, dov, lg, lb, rk, e2v):
        _, vjp = jax.vjp(lambda *a: _post(*a, e2v), y0v, y1v, rv, vv, kdv, gv, lg, lb, rk)
        gr = vjp(dov.astype(F32))
        return gr[0], gr[2], gr[3], gr[4], gr[5], gr[6], gr[7], gr[8]
    return _rowcall(fn, [y0, y1, r, v, kd, g, dout], [lnx_g, lnx_b, r_k, e2],
                    [(RW, F32), (RW, F32), (RW, F32), (2 * RW, F32), (RW, F32)], [(1, RW), (1, RW), (1, RW)],
                    tt=tt, name=name)


def _conv_fwd(p, dw, db, ln_g, ln_b, *, seq, tt, name):
    n_tok = p.shape[0]
    tps = seq // tt

    def glu(x, gate):
        return x * _sigmoid(gate)

    def body(u_ref, g_ref, uhp, ghp, uhn, ghn, dw_ref, db_ref, lg_ref, lb_ref, yc_o, y_o, ext):
        i = pl.program_id(0)
        first = (i % tps) == 0
        last = (i % tps) == tps - 1
        ext[pl.ds(0, 16), :] = jnp.where(first, 0.0, glu(uhp[...], ghp[...]))
        ext[pl.ds(16, tt), :] = glu(u_ref[...], g_ref[...])
        ext[pl.ds(16 + tt, 16), :] = jnp.where(last, 0.0, glu(uhn[...], ghn[...]))
        taps = [dw_ref[pl.ds(k, 1), :] for k in range(CONV_K)]
        for r0 in range(0, tt, CONV_ROWS):
            acc = jnp.zeros((CONV_ROWS, CW), F32) + db_ref[...]
            for k in range(CONV_K):
                acc = acc + ext[pl.ds(r0 + k + 1, CONV_ROWS), :] * taps[k]
            yc_o[pl.ds(r0, CONV_ROWS), :] = acc
        y_o[...] = _conv_post(yc_o[...], lg_ref[...], lb_ref[...]).astype(y_o.dtype)

    uhp_s, uhn_s = _halo_specs(CW, 16, tt, n_tok, 4)
    ghp_s, ghn_s = _halo_specs(CW, 16, tt, n_tok, 5)
    fulls = [dw, db, ln_g, ln_b]
    return pl.pallas_call(
        body, name=name,
        out_shape=[jax.ShapeDtypeStruct((n_tok, CW), F32), jax.ShapeDtypeStruct((n_tok, CW), BF16)],
        grid=(n_tok // tt,),
        in_specs=[pl.BlockSpec((tt, CW), lambda i: (i, 4)), pl.BlockSpec((tt, CW), lambda i: (i, 5)),
                  uhp_s, ghp_s, uhn_s, ghn_s] + [pl.BlockSpec(a.shape, lambda i: (0, 0)) for a in fulls],
        out_specs=[pl.BlockSpec((tt, CW), lambda i: (i, 0)), pl.BlockSpec((tt, CW), lambda i: (i, 0))],
        scratch_shapes=[pltpu.VMEM((tt + 32, CW), F32)],
        compiler_params=_cparams(("parallel",)),
    )(p, p, p, p, p, p, *fulls)


def _conv_post_bwd(yc, dy, ln_g, ln_b, *, tt, name):
    def fn(i, ycv, dyv, lg, lb):
        _, vjp = jax.vjp(_conv_post, ycv, lg, lb)
        dyc, dg, dbb = vjp(dyv.astype(F32))
        return dyc, dg, dbb, jnp.sum(dyc, axis=0, keepdims=True)
    return _rowcall(fn, [yc, dy], [ln_g, ln_b], [(CW, F32)], [(1, CW), (1, CW), (1, CW)], tt=tt, name=name)


def _conv_bwd(dyc, p, dw, *, seq, tt, name):
    n_tok = p.shape[0]
    tps = seq // tt

    def body(d_ref, dhp, dhn, u_ref, g_ref, uhp, ghp, uhn, ghn, dw_ref, dp_o, ddw_o, ext):
        i = pl.program_id(0)
        first = (i % tps) == 0
        last = (i % tps) == tps - 1
        dv = d_ref[...]
        ext[pl.ds(0, 16), :] = jnp.where(first, 0.0, dhp[...])
        ext[pl.ds(16, tt), :] = dv
        ext[pl.ds(16 + tt, 16), :] = jnp.where(last, 0.0, dhn[...])
        taps = [dw_ref[pl.ds(k, 1), :] for k in range(CONV_K)]
        for r0 in range(0, tt, CONV_ROWS):
            du = jnp.zeros((CONV_ROWS, CW), F32)
            for k in range(CONV_K):
                du = du + ext[pl.ds(r0 + 31 - k, CONV_ROWS), :] * taps[k]
            rows = pl.ds(r0, CONV_ROWS)
            sg_r = _sigmoid(g_ref[rows, :])
            dp_o[rows, 0:CW] = (du * sg_r).astype(dp_o.dtype)
            dp_o[rows, CW:2 * CW] = (du * u_ref[rows, :] * sg_r * (1.0 - sg_r)).astype(dp_o.dtype)
        uv, gv = u_ref[...], g_ref[...]
        sg = _sigmoid(gv)
        ext[pl.ds(0, 16), :] = jnp.where(first, 0.0, uhp[...] * _sigmoid(ghp[...]))
        ext[pl.ds(16, tt), :] = uv * sg
        ext[pl.ds(16 + tt, 16), :] = jnp.where(last, 0.0, uhn[...] * _sigmoid(ghn[...]))

        @pl.when(i == 0)
        def _():
            ddw_o[...] = jnp.zeros_like(ddw_o)
        for k in range(CONV_K):
            ddw_o[pl.ds(k, 1), :] += jnp.sum(dv * ext[pl.ds(k + 1, tt), :], axis=0, keepdims=True)

    dhp_s, dhn_s = _halo_specs(CW, 16, tt, n_tok, 0)
    uhp_s, uhn_s = _halo_specs(CW, 16, tt, n_tok, 4)
    ghp_s, ghn_s = _halo_specs(CW, 16, tt, n_tok, 5)
    return pl.pallas_call(
        body, name=name,
        out_shape=[jax.ShapeDtypeStruct((n_tok, 2 * CW), BF16), jax.ShapeDtypeStruct((32, CW), F32)],
        grid=(n_tok // tt,),
        in_specs=[pl.BlockSpec((tt, CW), lambda i: (i, 0)), dhp_s, dhn_s,
                  pl.BlockSpec((tt, CW), lambda i: (i, 4)), pl.BlockSpec((tt, CW), lambda i: (i, 5)),
                  uhp_s, ghp_s, uhn_s, ghn_s, pl.BlockSpec(dw.shape, lambda i: (0, 0))],
        out_specs=[pl.BlockSpec((tt, 2 * CW), lambda i: (i, 0)), pl.BlockSpec((32, CW), lambda i: (0, 0))],
        scratch_shapes=[pltpu.VMEM((tt + 32, CW), F32)],
        compiler_params=_cparams(("arbitrary",)),
    )(dyc, dyc, dyc, p, p, p, p, p, p, dw)


CHUNK = 64
_MM_DIMS = {"nn": (((2,), (1,)), ((0,), (0,))), "nt": (((2,), (2,)), ((0,), (0,))), "tn": (((1,), (1,)), ((0,), (0,)))}


def _mm16_raw(a, b, mode, fine):
    dot = lambda x, y: lax.dot_general(x, y, _MM_DIMS[mode], preferred_element_type=F32)
    if not fine:
        return dot(a.astype(BF16), b.astype(BF16))
    ah, (bh, bl) = a.astype(BF16), _split16(b)
    return dot(ah, bh) + dot(ah, bl)


@functools.partial(jax.custom_vjp, nondiff_argnums=(2, 3))
def _mm16(a, b, mode, fine=False):
    return _mm16_raw(a, b, mode, fine)


def _mm16_fwd(a, b, mode, fine):
    return _mm16_raw(a, b, mode, fine), (a, b)


def _mm16_bwd(mode, fine, res, g):
    a, b = res
    if mode == "nn":
        return _mm16_raw(g, b, "nt", fine), _mm16_raw(a, g, "tn", fine)
    if mode == "nt":
        return _mm16_raw(g, b, "nn", fine), _mm16_raw(g, a, "tn", fine)
    return _mm16_raw(b, g, "nt", fine), _mm16_raw(a, g, "nn", fine)


_mm16.defvjp(_mm16_fwd, _mm16_bwd)


def _tri_sum_raw(x, tri, mode):
    hi = x.astype(BF16)
    r1 = x - hi.astype(F32)
    mid = r1.astype(BF16)
    lo = (r1 - mid.astype(F32)).astype(BF16)
    dot = lambda p: lax.dot_general(tri, p, _MM_DIMS[mode], preferred_element_type=F32)
    return dot(hi) + dot(mid) + dot(lo)


@jax.custom_vjp
def _tri_sum(x, tri):
    return _tri_sum_raw(x, tri, "nn")


def _tri_sum_fwd(x, tri):
    return _tri_sum_raw(x, tri, "nn"), tri


def _tri_sum_bwd(tri, g):
    return _tri_sum_raw(g, tri, "tn"), jnp.zeros_like(tri)


_tri_sum.defvjp(_tri_sum_fwd, _tri_sum_bwd)


PAIR = 2 * HEAD
N_PAIRS = RW // PAIR


def _pair_rows(x):
    first = lax.broadcasted_iota(jnp.int32, x.shape, 2) < HEAD
    return jnp.concatenate([jnp.where(first, x, 0.0), jnp.where(first, 0.0, x)], axis=1)


def _chunk_step_pairs(s0, r, lw, k, v, kk, b, tri, rev):
    g, n, _ = r.shape
    row = lax.broadcasted_iota(jnp.int32, (g, n, PAIR), 1)
    col = lax.broadcasted_iota(jnp.int32, (g, n, PAIR), 2) % HEAD
    if rev:
        row, col = col, row
    diag = (lax.broadcasted_iota(jnp.int32, (g, PAIR, PAIR), 1) // HEAD
            == lax.broadcasted_iota(jnp.int32, (g, PAIR, PAIR), 2) // HEAD)
    cum = _tri_sum(lw, tri)
    up, down = jnp.exp(cum), jnp.exp(-cum)
    at, rt = -kk * jnp.exp(cum - lw), r * up
    kt, bt = k * down, b * down
    bt_rows, kt_rows = _pair_rows(bt), _pair_rows(kt)
    ar = jnp.concatenate([at, rt], axis=1)
    with_b = _mm16(ar, bt_rows, "nt")
    a_ab = jnp.where(col < row, with_b[:, :n], 0.0)
    a_rb = jnp.where(col <= row, with_b[:, n:], 0.0)
    a_ak = jnp.where(col < row, _mm16(at, kt_rows, "nt"), 0.0)
    a_rk = jnp.where(col <= row, _mm16(rt, kt_rows, "nt", True), 0.0)
    v_rows = _pair_rows(v)
    from_state = _mm16(ar, s0, "nt")
    u = from_state[:, :n] + _mm16(a_ak, v_rows, "nn")
    power = a_ab
    steps = n.bit_length() - 1
    for it in range(steps):
        if it + 1 < steps:
            both = _mm16(power, jnp.concatenate([_pair_rows(u), _pair_rows(power)], axis=2), "nn")
            u = u + both[:, :, :PAIR]
            power = both[:, :, PAIR:]
        else:
            u = u + _mm16(power, _pair_rows(u), "nn")
    y = from_state[:, n:] + _mm16(a_rk, v_rows, "nn") + _mm16(a_rb, _pair_rows(u), "nn")
    grown = s0 + jnp.where(diag, _mm16(jnp.concatenate([v, u], axis=1), jnp.concatenate([kt, bt], axis=1), "tn"), 0.0)
    return y, grown * jnp.exp(jnp.sum(lw, axis=1, keepdims=True))


SCAN_SEQS = 4


def _tri_ones(rev, nseq):
    shape = (nseq * N_PAIRS, CHUNK, CHUNK)
    row, col = lax.broadcasted_iota(jnp.int32, shape, 1), lax.broadcasted_iota(jnp.int32, shape, 2)
    return ((col >= row) if rev else (col <= row)).astype(BF16)


def _split_heads(ref):
    return jnp.stack([ref[q, :, pl.ds(h * PAIR, PAIR)] for q in range(ref.shape[0]) for h in range(N_PAIRS)])


def _merge_heads(ref, val):
    for q in range(ref.shape[0]):
        for h in range(N_PAIRS):
            ref[q, :, pl.ds(h * PAIR, PAIR)] = val[q * N_PAIRS + h]


def _chunk_specs(nseq, nc, rev, dcol):
    chunk = (lambda c: nc - 1 - c) if rev else (lambda c: c)
    shared = pl.BlockSpec((nseq, CHUNK, RW), lambda s, c: (s, chunk(c), 0))
    own = pl.BlockSpec((nseq, CHUNK, RW), lambda s, c: (s, chunk(c), dcol))
    return shared, own


def _wkv_chunk_fwd(r, lw, k, v, kk, b, *, rev, name):
    bsz, seq, _ = r.shape
    nc = seq // CHUNK
    nseq = SCAN_SEQS if bsz % SCAN_SEQS == 0 else 1
    shared, own = _chunk_specs(nseq, nc, rev, int(rev))

    def body(r_ref, lw_ref, k_ref, v_ref, kk_ref, b_ref, tri_ref, y_o, s0_o, s_ref):
        @pl.when(pl.program_id(1) == 0)
        def _():
            s_ref[...] = jnp.zeros_like(s_ref)

        s0 = s_ref[...]
        s0_o[:, 0] = s0.reshape(nseq, N_PAIRS, PAIR, PAIR)
        y, s_ref[...] = _chunk_step_pairs(s0, *[_split_heads(x) for x in (r_ref, lw_ref, k_ref, v_ref, kk_ref, b_ref)],
                                    tri_ref[...], rev)
        _merge_heads(y_o, y)

    return pl.pallas_call(
        body, name=name,
        out_shape=[jax.ShapeDtypeStruct((bsz, seq, RW), F32), jax.ShapeDtypeStruct((bsz, nc, N_PAIRS, PAIR, PAIR), F32)],
        grid=(bsz // nseq, nc),
        in_specs=[shared, own, own, shared, shared, own,
                  pl.BlockSpec((nseq * N_PAIRS, CHUNK, CHUNK), lambda s, c: (0, 0, 0))],
        out_specs=[shared, pl.BlockSpec((nseq, 1, N_PAIRS, PAIR, PAIR), lambda s, c: (s, c, 0, 0, 0))],
        scratch_shapes=[pltpu.VMEM((nseq * N_PAIRS, PAIR, PAIR), F32)],
        compiler_params=_cparams(("parallel", "arbitrary")),
    )(r, lw, k, v, kk, b, _tri_ones(rev, nseq))


def _wkv_chunk_bwd(r, lw, k, v, kk, b, dy, s0, *, rev, name):
    bsz, seq, _ = r.shape
    nc = seq // CHUNK
    nseq = SCAN_SEQS if bsz % SCAN_SEQS == 0 else 1
    shared, own = _chunk_specs(nseq, nc, not rev, int(rev))

    def body(r_ref, lw_ref, k_ref, v_ref, kk_ref, b_ref, dy_ref, s0_ref, tri_ref, *rest):
        outs, ds_ref = rest[:-1], rest[-1]

        @pl.when(pl.program_id(1) == 0)
        def _():
            ds_ref[...] = jnp.zeros_like(ds_ref)

        triv = tri_ref[...]
        _, vjp = jax.vjp(lambda *a: _chunk_step_pairs(*a, triv, rev), s0_ref[:, 0].reshape(nseq * N_PAIRS, PAIR, PAIR),
                         *[_split_heads(x) for x in (r_ref, lw_ref, k_ref, v_ref, kk_ref, b_ref)])
        grads = vjp((_split_heads(dy_ref), ds_ref[...]))
        ds_ref[...] = grads[0]
        for o, gval in zip(outs, grads[1:]):
            _merge_heads(o, gval)

    return pl.pallas_call(
        body, name=name,
        out_shape=[jax.ShapeDtypeStruct((bsz, seq, RW), F32)] * 6,
        grid=(bsz // nseq, nc),
        in_specs=[shared, own, own, shared, shared, own, shared,
                  pl.BlockSpec((nseq, 1, N_PAIRS, PAIR, PAIR), lambda s, c: (s, nc - 1 - c, 0, 0, 0)),
                  pl.BlockSpec((nseq * N_PAIRS, CHUNK, CHUNK), lambda s, c: (0, 0, 0))],
        out_specs=[shared] * 6,
        scratch_shapes=[pltpu.VMEM((nseq * N_PAIRS, PAIR, PAIR), F32)],
        compiler_params=_cparams(("parallel", "arbitrary")),
    )(r, lw, k, v, kk, b, dy, s0, _tri_ones(rev, nseq))


def _block_diag2(w):
    z = jnp.zeros_like(w[0])
    return jnp.concatenate([jnp.concatenate([w[0], z], axis=1), jnp.concatenate([z, w[1]], axis=1)], axis=0)


def _pad_in_cols(a):
    z = jnp.zeros(a.shape[:-1] + (SHIFT_PAD - SHIFT_COLS,), a.dtype)
    return jnp.concatenate([a[..., :SHIFT_COLS], z, a[..., SHIFT_COLS:]], axis=-1)


def _follow(small, token):
    return small if token is None else small + token[0:1, 0:1]


def _local_step(x, target, wts, *, tt, start_token=None, more_weights=None, grads_ready=None, small_ready=None):
    bsz, seq, _ = x.shape
    n_tok = bsz * seq
    row = lambda a: a.reshape(1, -1).astype(F32)
    x0 = x.reshape(n_tok, D_MODEL)
    tgt = target.reshape(n_tok, D_MODEL)
    ln = {k: row(wts[k]) for k in ("ln1_g", "ln1_b", "ln2_g", "ln2_b", "ln3_g", "ln3_b")}
    if grads_ready is None:
        grads_ready = lambda names, slabs: None

    w1i = wts["ffn1_w_in"]
    h1, act1 = _ffn_in(x0, w1i, tm=TM_FFN, after=start_token, name="ffn1_in")
    if more_weights is not None:
        wts = {**wts, **more_weights("ffn1_out", act1)}
    w1o = wts["ffn1_w_out"]
    z1, x1, x1b = _mm_ln([act1], w1o, x0, ln["ln1_g"], ln["ln1_b"], 0.5, tm=TM_LN, name="ffn1_out_ln1")
    if more_weights is not None:
        wts = {**wts, **more_weights("mix", x1b)}
    win = _pad_in_cols(wts["w_in"])
    zpad = jnp.zeros((1, SHIFT_PAD - SHIFT_COLS), F32)
    mu_p = jnp.concatenate([row(wts["mu_prev"]), zpad], axis=1)
    mu_n = jnp.concatenate([row(wts["mu_next"]), zpad], axis=1)
    w2b, a2b = _block_diag2(wts["w2"]), _block_diag2(wts["a2"])
    w0c, a0c = row(wts["w0"]), row(wts["a0"])
    g2p = jnp.concatenate([wts["g2"], jnp.zeros((GATE_PAD - GATE_LORA, RW), F32)], axis=0)
    k_k, k_a, r_k = row(wts["k_k"]), row(wts["k_a"]), row(wts["r_k"])
    lnx_g, lnx_b = row(wts["lnx_g"]), row(wts["lnx_b"])
    cdw, cb, clg, clb = wts["conv_dw"], row(wts["conv_b"]), row(wts["conv_ln_g"]), row(wts["conv_ln_b"])
    small = (mu_p, mu_n, w2b, w0c, a2b, a0c, g2p, k_k, k_a)
    seq3 = lambda a: a.reshape(bsz, seq, a.shape[-1])
    flat = lambda a: a.reshape(n_tok, a.shape[-1])

    p = _matmul(x1b, win, name="proj_in")
    r, v, kk, w, kd, b, g = _mix_prep(p, *small, seq=seq, tt=tt, name="mix_prep")
    scan_in = [seq3(a) for a in (r, w, kd, v, kk, b)]
    y0, s_chunks0 = _wkv_chunk_fwd(*scan_in, rev=False, name="wkv_fwd_dir0")
    y1, s_chunks1 = _wkv_chunk_fwd(*scan_in, rev=True, name="wkv_fwd_dir1")
    y0, y1 = flat(y0), flat(y1)
    yr = _mix_post(y0, y1, r, v, kd, g, lnx_g, lnx_b, r_k, tt=tt, name="mix_post")
    yc, yv = _conv_fwd(p, cdw, cb, clg, clb, seq=seq, tt=tt, name="conv_fwd")
    if more_weights is not None:
        wts = {**wts, **more_weights("out", yr)}
    wout, w2i, w2o = wts["w_out"], wts["ffn2_w_in"], wts["ffn2_w_out"]
    z2, x2, x2b = _mm_ln([yr, yv], wout, x1, ln["ln2_g"], ln["ln2_b"], 1.0, tm=TM_LN, name="proj_out_ln2")
    h2, act2 = _ffn_in(x2b, w2i, tm=TM_FFN, name="ffn2_in")

    gr = {}
    slab_rows = lambda a: a.reshape((N_CHIPS, a.shape[0] // N_CHIPS) + a.shape[1:])
    dw_kw = dict(ta=True, out_dtype=BF16)
    dz3, gr["ln3_g"], gr["ln3_b"], loss_part = _mm_ln_loss(act2, w2o, x2, ln["ln3_g"], ln["ln3_b"], tgt, 0.5, tm=TM_LN,
                                                           name="ffn2_out_ln3_loss")
    dh2 = _ffn_out_bwd(dz3, w2o, h2, tm=TM_FFN, name="ffn2_out_dx")
    gr["ffn2_w_out"] = slab_rows(_matmul(act2, dz3, scale=0.5, tm=D_FF // 2, name="ffn2_out_dw", **dw_kw))
    dz2, gr["ln2_g"], gr["ln2_b"] = _mm_nt_res([dh2], w2i, dz3, ln=(z2, ln["ln2_g"], ln["ln2_b"]), tm=TM_FFN,
                                               name="ffn2_in_dx_ln2")
    gr["ffn2_w_in"] = _matmul(x2b, dh2, col_slabs=True, tn=2 * D_FF // N_CHIPS, name="ffn2_in_dw", **dw_kw)
    dmix = _matmul(dz2, wout, tb=True, name="proj_out_dx")
    gr["w_out"] = slab_rows(jnp.concatenate([_matmul(yr, dz2, name="proj_out_dw_rwkv", **dw_kw),
                                             _matmul(yv, dz2, name="proj_out_dw_conv", **dw_kw)], axis=0))
    tok = grads_ready(("ffn2_w_out", "ffn2_w_in", "w_out"), [gr["ffn2_w_out"], gr["ffn2_w_in"], gr["w_out"]])
    dyr, dyv = (dmix, RW, 0), (dmix, RW, 1)
    dy, dr_p, dv_p, dkd_p, dg, gr["lnx_g"], gr["lnx_b"], gr["r_k"] = _mix_post_bwd(
        y0, y1, r, v, kd, g, _follow(lnx_g, tok), lnx_b, r_k, dyr, tt=tt, name="mix_post_bwd")
    dr0, dw0, dkd0, dv0, dk0, db0 = [flat(a) for a in _wkv_chunk_bwd(*scan_in, seq3(dy), s_chunks0, rev=False,
                                                                      name="wkv_bwd_dir0")]
    dr1, dw1, dkd1, dv1, dk1, db1 = [flat(a) for a in _wkv_chunk_bwd(*scan_in, seq3(dy), s_chunks1, rev=True,
                                                                      name="wkv_bwd_dir1")]
    ct_terms = [[dr_p, dr0, dr1], [dv_p, dv0, dv1], [dk0, dk1], [(dw0, dw1)], [dkd_p, (dkd0, dkd1)], [(db0, db1)], [dg]]
    dyc, gr["conv_ln_g"], gr["conv_ln_b"], gr["conv_b"] = _conv_post_bwd(yc, dyv, clg, clb, tt=tt, name="conv_post_bwd")
    dpc, ddw = _conv_bwd(dyc, p, cdw, seq=seq, tt=tt, name="conv_bwd")
    gr["conv_dw"] = ddw[:CONV_K]
    dps, dw2b, dw0c, da2b, da0c, dg2p, gr["k_k"], gr["k_a"] = _mix_prep_bwd(
        p, *small, ct_terms, seq=seq, tt=tt, name="mix_prep_bwd")
    gr["w2"] = jnp.stack([dw2b[:LORA, :RW], dw2b[LORA:, RW:]])
    gr["a2"] = jnp.stack([da2b[:LORA, :RW], da2b[LORA:, RW:]])
    gr["w0"], gr["a0"], gr["g2"] = dw0c.reshape(2, RW), da0c.reshape(2, RW), dg2p[:GATE_LORA]
    dpsh, dmu_p, dmu_n = _shift_bwd(dps, p, mu_p, mu_n, seq=seq, tt=tt, name="shift_bwd")
    gr["mu_prev"], gr["mu_next"] = dmu_p[:, :SHIFT_COLS], dmu_n[:, :SHIFT_COLS]
    dwin = jnp.concatenate([_matmul(x1b, dpsh, name="proj_in_dw_shift", **dw_kw)[:, :SHIFT_COLS],
                            _matmul(x1b, dpc, name="proj_in_dw_conv", **dw_kw)], axis=1)
    gr["w_in"] = jnp.moveaxis(dwin.reshape(D_MODEL, N_CHIPS, IN_COLS // N_CHIPS), 1, 0)
    tok = grads_ready(("w_in",), [gr["w_in"]])
    dz1, gr["ln1_g"], gr["ln1_b"] = _mm_nt_res([dpsh, dpc], win, dz2, ln=(z1, ln["ln1_g"], ln["ln1_b"]), tm=TM_FFN,
                                               after=tok, name="proj_in_dx_ln1")
    gr["loss"] = loss_part
    tok = small_ready(gr, loss_part) if small_ready is not None else None
    dh1 = _ffn_out_bwd(dz1, w1o, h1, tm=TM_FFN, after=tok, name="ffn1_out_dx")
    gr["ffn1_w_out"] = slab_rows(_matmul(act1, dz1, scale=0.5, tm=D_FF // 2, name="ffn1_out_dw", **dw_kw))
    tok = grads_ready(("ffn1_w_out",), [gr["ffn1_w_out"]])
    gr["ffn1_w_in"] = _matmul(x0, dh1, col_slabs=True, tn=2 * D_FF // N_CHIPS, after=tok, name="ffn1_in_dw", **dw_kw)
    tok = grads_ready(("ffn1_w_in",), [gr["ffn1_w_in"]])
    dx0 = _mm_nt_res([dh1], w1i, dz1, tm=TM_FFN, after=tok, name="ffn1_in_dx")
    return dx0.reshape(bsz, seq, D_MODEL), gr


def _mesh_pos():
    return lax.axis_index("x"), lax.axis_index("y"), lax.axis_index("c")


def _other_chips(x, y):
    return [(1 - x, y), (x, 1 - y), (1 - x, 1 - y)]


def _gather_chips(shards, *, name):
    n = len(shards)
    halves = [s.shape[0] // 2 for s in shards]
    assert all(2 * h == s.shape[0] for h, s in zip(halves, shards))

    def body(*refs):
        ins, outs = refs[:n], refs[n:2 * n]
        send_sems, recv_sems, fwd_send_sems, fwd_recv_sems, loc_sems = refs[2 * n:]
        x, y, c = _mesh_pos()
        q = 2 * x + y
        peers = _other_chips(x, y)
        local = [pltpu.make_async_copy(ins[a], outs[a].at[q], loc_sems.at[a]) for a in range(n)]
        for cp in local:
            cp.start()

        def half(a, chip, core):
            return outs[a].at[chip, pl.ds(core * halves[a], halves[a])]

        sends = [pltpu.make_async_remote_copy(ins[a].at[pl.ds(c * halves[a], halves[a])], half(a, q, c),
                                              send_sems.at[a, k], recv_sems.at[a, k],
                                              device_id=(px, py, c), device_id_type=MESH)
                 for a in range(n) for k, (px, py) in enumerate(peers)]
        for cp in sends:
            cp.start()
        passed = []
        for a in range(n):
            for k, (px, py) in enumerate(peers):
                mine = half(a, 2 * px + py, c)
                pltpu.make_async_remote_copy(mine, mine, send_sems.at[a, k], recv_sems.at[a, k],
                                             device_id=(px, py, c), device_id_type=MESH).wait_recv()
                cp = pltpu.make_async_remote_copy(mine, mine, fwd_send_sems.at[a, k], fwd_recv_sems.at[a, k],
                                                  device_id=(x, y, 1 - c), device_id_type=MESH)
                cp.start()
                passed.append(cp)
        for a in range(n):
            for k, (px, py) in enumerate(peers):
                theirs = half(a, 2 * px + py, 1 - c)
                pltpu.make_async_remote_copy(theirs, theirs, fwd_send_sems.at[a, k], fwd_recv_sems.at[a, k],
                                             device_id=(x, y, 1 - c), device_id_type=MESH).wait_recv()
        for cp in sends + passed:
            cp.wait_send()
        for cp in local:
            cp.wait()

    any_spec = pl.BlockSpec(memory_space=pl.ANY)
    return pl.pallas_call(
        body, name=name,
        out_shape=[jax.ShapeDtypeStruct((N_CHIPS,) + s.shape, s.dtype) for s in shards],
        in_specs=[any_spec] * n, out_specs=[any_spec] * n,
        scratch_shapes=[pltpu.SemaphoreType.DMA((n, 3))] * 4 + [pltpu.SemaphoreType.DMA((n,))],
        compiler_params=pltpu.CompilerParams(has_side_effects=True),
    )(*shards)


HBM_SPEC = pl.BlockSpec(memory_space=pltpu.HBM)
SEM_SPEC = pl.BlockSpec(memory_space=pltpu.SEMAPHORE)
ANY_SPEC = pl.BlockSpec(memory_space=pl.ANY)
SIDE_EFFECT = pltpu.SideEffectType.DATAFLOW_SIDE_EFFECTING


def _chip_copies(src_refs, land_refs, send_sems, recv_sems, scatter, arriving=False):
    x, y, c = _mesh_pos()
    cps = []
    for a, (src, land) in enumerate(zip(src_refs, land_refs)):
        for k, (px, py) in enumerate(_other_chips(x, y)):
            slot = k if scatter else (2 * px + py if arriving else 2 * x + y)
            cps.append(pltpu.make_async_remote_copy(src.at[2 * px + py] if scatter else src, land.at[slot],
                                                    send_sems.at[3 * a + k], recv_sems.at[3 * a + k],
                                                    device_id=(px, py, c), device_id_type=MESH))
    return cps


def _exchange_start(srcs, *, scatter, after, name):
    n = len(srcs)
    lands = [lax.empty((3,) + s.shape[1:] if scatter else (N_CHIPS,) + s.shape, s.dtype) for s in srcs]

    def body(*refs):
        src_refs, land_refs = refs[:n], refs[n:2 * n]
        send_sems, recv_sems = refs[2 * n + 1:2 * n + 3]
        token = refs[-1]
        for cp in _chip_copies(src_refs, land_refs, send_sems, recv_sems, scatter):
            cp.start()
        token[...] = jnp.zeros_like(token)

    hbm = lambda a: pltpu.with_memory_space_constraint(a, pltpu.HBM)
    outs = pl.pallas_call(
        body, name=name,
        out_shape=(pltpu.SemaphoreType.DMA((3 * n,)), pltpu.SemaphoreType.DMA((3 * n,)),
                   *[pltpu.HBM(a.shape, a.dtype) for a in srcs + lands], jax.ShapeDtypeStruct((8, LANES), F32)),
        in_specs=[HBM_SPEC] * (2 * n) + [ANY_SPEC],
        out_specs=(SEM_SPEC, SEM_SPEC, *[HBM_SPEC] * (2 * n), pl.BlockSpec(memory_space=pltpu.VMEM)),
        input_output_aliases={i: 2 + i for i in range(2 * n)},
        compiler_params=pltpu.CompilerParams(has_side_effects=SIDE_EFFECT),
    )(*[hbm(a) for a in srcs + lands], after)
    return outs[0], outs[1], list(outs[2:2 + n]), list(outs[2 + n:2 + 2 * n]), outs[-1]


def _exchange_wait(started, *, scatter, after, name):
    send_sems, recv_sems, srcs, lands, _ = started
    n = len(srcs)

    def body(*refs):
        src_refs, land_refs = refs[:n], refs[n:2 * n]
        send_s, recv_s = refs[2 * n:2 * n + 2]
        for cp in _chip_copies(src_refs, land_refs, send_s, recv_s, scatter, arriving=True):
            cp.wait_send()
            cp.wait_recv()

    outs = pl.pallas_call(
        body, name=name,
        out_shape=tuple(pltpu.HBM(a.shape, a.dtype) for a in srcs + lands),
        in_specs=[HBM_SPEC] * (2 * n) + [SEM_SPEC, SEM_SPEC, ANY_SPEC],
        out_specs=tuple([HBM_SPEC] * (2 * n)),
        input_output_aliases={i: i for i in range(2 * n)},
        compiler_params=pltpu.CompilerParams(has_side_effects=SIDE_EFFECT),
    )(*srcs, *lands, send_sems, recv_sems, after)
    return list(outs[:n]), list(outs[n:])


def _by_chip(own, land):
    xi, yi, _ = _mesh_pos()
    return lax.dynamic_update_index_in_dim(land, own, 2 * xi + yi, 0)


def _swap_sibling(arrs, *, name):
    n = len(arrs)

    def body(*refs):
        ins, outs = refs[:n], refs[n:2 * n]
        send_sems, recv_sems = refs[2 * n:]
        x, y, c = _mesh_pos()
        cps = [pltpu.make_async_remote_copy(ins[a], outs[a], send_sems.at[a], recv_sems.at[a],
                                            device_id=(x, y, 1 - c), device_id_type=MESH) for a in range(n)]
        for cp in cps:
            cp.start()
        for cp in cps:
            cp.wait_recv()
        for cp in cps:
            cp.wait_send()

    any_spec = pl.BlockSpec(memory_space=pl.ANY)
    return pl.pallas_call(
        body, name=name,
        out_shape=[jax.ShapeDtypeStruct(s.shape, s.dtype) for s in arrs],
        in_specs=[any_spec] * n, out_specs=[any_spec] * n,
        scratch_shapes=[pltpu.SemaphoreType.DMA((n,)), pltpu.SemaphoreType.DMA((n,))],
        compiler_params=pltpu.CompilerParams(has_side_effects=True),
    )(*arrs)


def _device_copies(v_ref, land_ref, send_sems, recv_sems, arriving=False):
    x, y, c = _mesh_pos()
    me = 4 * x + 2 * y + c
    cps = []
    for m in range(1, 8):
        px, py, pc = (x + ((m >> 2) & 1)) % 2, (y + ((m >> 1) & 1)) % 2, (c + (m & 1)) % 2
        slot = 4 * px + 2 * py + pc if arriving else me
        cps.append(pltpu.make_async_remote_copy(v_ref, land_ref.at[slot], send_sems.at[m - 1], recv_sems.at[m - 1],
                                                device_id=(px, py, pc), device_id_type=MESH))
    return cps


def _allsum_start(vec, *, after, name):
    land = lax.empty((8,) + vec.shape, F32)

    def body(v_ref, land_ref, _after, send_sems, recv_sems, v_thru, land_thru, token):
        for cp in _device_copies(v_ref, land_ref, send_sems, recv_sems):
            cp.start()
        token[...] = jnp.zeros_like(token)

    hbm = lambda a: pltpu.with_memory_space_constraint(a, pltpu.HBM)
    return pl.pallas_call(
        body, name=name,
        out_shape=(pltpu.SemaphoreType.DMA((7,)), pltpu.SemaphoreType.DMA((7,)), pltpu.HBM(vec.shape, F32),
                   pltpu.HBM(land.shape, F32), jax.ShapeDtypeStruct((8, LANES), F32)),
        in_specs=[HBM_SPEC, HBM_SPEC, ANY_SPEC],
        out_specs=(SEM_SPEC, SEM_SPEC, HBM_SPEC, HBM_SPEC, pl.BlockSpec(memory_space=pltpu.VMEM)),
        input_output_aliases={0: 2, 1: 3},
        compiler_params=pltpu.CompilerParams(has_side_effects=SIDE_EFFECT),
    )(hbm(vec), hbm(land), after)


def _allsum_wait(started, *, after, name):
    send_sems, recv_sems, vec, land, _ = started

    def body(v_ref, land_ref, send_s, recv_s, _after, v_dead, got):
        for cp in _device_copies(v_ref, land_ref, send_s, recv_s, arriving=True):
            cp.wait_send()
            cp.wait_recv()

    vec, land = pl.pallas_call(
        body, name=name,
        out_shape=(pltpu.HBM(vec.shape, F32), pltpu.HBM(land.shape, F32)),
        in_specs=[HBM_SPEC, HBM_SPEC, SEM_SPEC, SEM_SPEC, ANY_SPEC],
        out_specs=(HBM_SPEC, HBM_SPEC),
        input_output_aliases={0: 0, 1: 1},
        compiler_params=pltpu.CompilerParams(has_side_effects=SIDE_EFFECT),
    )(vec, land, send_sems, recv_sems, after)
    xi, yi, ci = _mesh_pos()
    every = lax.dynamic_update_index_in_dim(land, vec, 4 * xi + 2 * yi + ci, 0)

    def add(e_ref, o_ref):
        acc = e_ref[0]
        for d in range(1, 8):
            acc = acc + e_ref[d]
        o_ref[...] = acc

    vm = pl.BlockSpec(memory_space=pltpu.VMEM)
    return pl.pallas_call(add, name=name + "_sum", out_shape=jax.ShapeDtypeStruct(vec.shape, F32), in_specs=[vm],
                          out_specs=vm, compiler_params=_cparams())(every)


def _adamw(w, g, m, v):
    m = ADAM_B1 * m + (1.0 - ADAM_B1) * g
    v = ADAM_B2 * v + (1.0 - ADAM_B2) * (g * g)
    m_hat = m / (1.0 - ADAM_B1 ** ADAM_STEP)
    v_hat = v / (1.0 - ADAM_B2 ** ADAM_STEP)
    delta = -ADAM_LR * (m_hat / (jnp.sqrt(v_hat) + ADAM_EPS) + ADAM_WD * w)
    return delta, m, v


def _sum4(mine, land, *, name):
    rows, cols = mine.shape
    tr = _pick_rows(rows)

    def body(a_ref, l_ref, o_ref):
        o_ref[...] = (a_ref[...].astype(F32) + l_ref[0].astype(F32)) + (l_ref[1].astype(F32) + l_ref[2].astype(F32))

    return pl.pallas_call(
        body, name=name, out_shape=jax.ShapeDtypeStruct((rows, cols), F32), grid=(rows // tr,),
        in_specs=[pl.BlockSpec((tr, cols), lambda i: (i, 0)), pl.BlockSpec((3, tr, cols), lambda i: (0, i, 0))],
        out_specs=pl.BlockSpec((tr, cols), lambda i: (i, 0)),
        compiler_params=_cparams(("parallel",)),
    )(mine, land)


def _pick_rows(rows, want=256):
    for t in range(min(want, rows) // 8 * 8, 0, -8):
        if rows % t == 0:
            return t
    return rows


def _sum_adam(h_mine, h_sib, w, m, v, *, name):
    rows, cols = w.shape
    tr = _pick_rows(rows)

    def body(a_ref, b_ref, w_ref, m_ref, v_ref, g_o, d_o, m_o, v_o):
        g = a_ref[...] + b_ref[...]
        d, mn, vn = _adamw(w_ref[...], g, m_ref[...], v_ref[...])
        g_o[...], d_o[...], m_o[...], v_o[...] = g, d, mn, vn

    spec = pl.BlockSpec((tr, cols), lambda i: (i, 0))
    return pl.pallas_call(
        body, name=name, out_shape=[jax.ShapeDtypeStruct((rows, cols), F32)] * 4, grid=(rows // tr,),
        in_specs=[spec] * 5, out_specs=[spec] * 4, compiler_params=_cparams(("parallel",)),
    )(h_mine, h_sib, w, m, v)


def _adam_rows(w, g, m, v, *, name):
    def body(w_ref, g_ref, m_ref, v_ref, d_o, m_o, v_o):
        d_o[...], m_o[...], v_o[...] = _adamw(w_ref[...], g_ref[...], m_ref[...], v_ref[...])

    vm = pl.BlockSpec(memory_space=pltpu.VMEM)
    return pl.pallas_call(
        body, name=name, out_shape=[jax.ShapeDtypeStruct(w.shape, F32)] * 3,
        in_specs=[vm] * 4, out_specs=[vm] * 3, compiler_params=_cparams(),
    )(w, g, m, v)


def _size(shape):
    size = 1
    for d in shape:
        size *= d
    return size


def _pack_rows(arrs):
    blocks = []
    for a in arrs:
        flat = a.reshape(-1).astype(F32)
        flat = jnp.concatenate([flat, jnp.zeros((-flat.shape[0] % (8 * LANES),), F32)])
        blocks.append(flat.reshape(-1, LANES))
    return jnp.concatenate(blocks, axis=0)


def _unpack_rows(packed, shapes):
    out, row = [], 0
    for s in shapes:
        rows = -(-_size(s) // (8 * LANES)) * 8
        out.append(packed[row:row + rows].reshape(-1)[:_size(s)].reshape(s))
        row += rows
    return out


WEIGHTS = ['ffn1_w_in', 'ffn1_w_out', 'w_in', 'mu_prev', 'mu_next', 'w0', 'w2', 'a0', 'a2', 'g2', 'k_k', 'k_a', 'r_k',
           'lnx_g', 'lnx_b', 'conv_dw', 'conv_b', 'conv_ln_g', 'conv_ln_b', 'w_out', 'ffn2_w_in', 'ffn2_w_out',
           'ln1_g', 'ln1_b', 'ln2_g', 'ln2_b', 'ln3_g', 'ln3_b']
COL_SHARDED = ('ffn1_w_in', 'w_in', 'ffn2_w_in')
ROW_SHARDED = ('ffn1_w_out', 'w_out', 'ffn2_w_out')
BIG = COL_SHARDED + ROW_SHARDED
SMALL_SHARDED = ('w0', 'w2', 'a0', 'a2', 'g2', 'conv_dw')
REPLICATED = tuple(n for n in WEIGHTS if n not in BIG + SMALL_SHARDED)


def _train_step(x, target, w, m, v, *, tt):
    xi, yi, _ = _mesh_pos()
    q = 2 * xi + yi

    later = {"ffn1_out": ("ffn1_w_out",), "mix": ("w_in",) + SMALL_SHARDED, "out": ("w_out", "ffn2_w_in", "ffn2_w_out")}
    shard = lambda n: w[n][0].astype(BF16) if n in BIG else w[n][0]
    small_names = REPLICATED + SMALL_SHARDED

    def whole(n, slabs):
        if n in ROW_SHARDED:
            return slabs.reshape((-1,) + slabs.shape[2:])
        if n in ("ffn1_w_in", "ffn2_w_in"):
            return slabs
        return jnp.moveaxis(slabs, 0, -2).reshape(slabs.shape[1:-1] + (N_CHIPS * slabs.shape[-1],))

    full = {n: w[n][0] for n in REPLICATED}
    first = _gather_chips([shard("ffn1_w_in")], name="gather_ffn1_in")
    full["ffn1_w_in"] = whole("ffn1_w_in", first[0])
    started, token = {}, first[0]
    for stage, names in later.items():
        started[stage] = _exchange_start([shard(n) for n in names], scatter=False, after=token,
                                         name="gather_%s_start" % stage)
        token = started[stage][-1]

    def more_weights(stage, after):
        own, land = _exchange_wait(started[stage], scatter=False, after=after, name="gather_%s_wait" % stage)
        got = {n: whole(n, _by_chip(o, l)) for n, o, l in zip(later[stage], own, land)}
        full.update(got)
        return got

    small_sent = []

    def small_ready(gr, loss_part):
        vec = _pack_rows([gr[n] for n in small_names] + [loss_part[0:1, 0:1]])
        small_sent.append(_allsum_start(vec, after=vec, name="reduce_small_start"))
        return small_sent[0][-1]

    sent = []

    def grads_ready(names, slabs):
        started = _exchange_start(slabs, scatter=True, after=slabs[0], name="scatter_%s_start" % names[0])
        sent.append((names, started))
        return started[-1]

    grad_x, gr = _local_step(x, target, full, tt=tt, start_token=token, more_weights=more_weights,
                             grads_ready=grads_ready, small_ready=small_ready)

    halves = {}
    for names, started in sent:
        stacks, landed = _exchange_wait(started, scatter=True, after=grad_x, name="scatter_%s_wait" % names[0])
        for n, s, l in zip(names, stacks, landed):
            halves[n] = _sum4(lax.dynamic_index_in_dim(s, q, 0, keepdims=False), l, name="sum4_" + n)
    halves = [halves[n] for n in BIG]
    sib = _swap_sibling(halves, name="swap_halves")
    grad, delta, new_m, new_v = {}, {}, {}, {}
    for n, h, hs in zip(BIG, halves, sib):
        outs = _sum_adam(h, hs, w[n][0], m[n][0], v[n][0], name="adam_" + n)
        grad[n], delta[n], new_m[n], new_v[n] = [o[None] for o in outs]

    small_full_shapes = [full[n].shape for n in small_names]
    red = _allsum_wait(small_sent[0], after=grad_x, name="reduce_small_wait")
    *red, loss = _unpack_rows(red, small_full_shapes + [()])
    red = dict(zip(small_names, red))
    gsm = {}
    for n in REPLICATED:
        gsm[n] = red[n].reshape(w[n].shape)
    for n in SMALL_SHARDED:
        width = w[n].shape[-1]
        gsm[n] = lax.dynamic_slice_in_dim(red[n], q * width, width, axis=red[n].ndim - 1).reshape(w[n].shape)
    shapes = [w[n].shape for n in small_names]
    d_p, m_p, v_p = _adam_rows(_pack_rows([w[n] for n in small_names]), _pack_rows([gsm[n] for n in small_names]),
                               _pack_rows([m[n] for n in small_names]), _pack_rows([v[n] for n in small_names]),
                               name="adam_small")
    for n, dd, mm, vv in zip(small_names, _unpack_rows(d_p, shapes), _unpack_rows(m_p, shapes), _unpack_rows(v_p, shapes)):
        grad[n], delta[n], new_m[n], new_v[n] = gsm[n], dd, mm, vv
    return loss, grad_x, grad, delta, new_m, new_v


def kernel(x, ffn1_w_in, ffn1_w_out, w_in, mu_prev, mu_next, w0, w2, a0, a2, g2, k_k, k_a, r_k, lnx_g, lnx_b, conv_dw, conv_b, conv_ln_g, conv_ln_b, w_out, ffn2_w_in, ffn2_w_out, ln1_g, ln1_b, ln2_g, ln2_b, ln3_g, ln3_b, loss_target, m_ffn1_w_in, m_ffn1_w_out, m_w_in, m_mu_prev, m_mu_next, m_w0, m_w2, m_a0, m_a2, m_g2, m_k_k, m_k_a, m_r_k, m_lnx_g, m_lnx_b, m_conv_dw, m_conv_b, m_conv_ln_g, m_conv_ln_b, m_w_out, m_ffn2_w_in, m_ffn2_w_out, m_ln1_g, m_ln1_b, m_ln2_g, m_ln2_b, m_ln3_g, m_ln3_b, v_ffn1_w_in, v_ffn1_w_out, v_w_in, v_mu_prev, v_mu_next, v_w0, v_w2, v_a0, v_a2, v_g2, v_k_k, v_k_a, v_r_k, v_lnx_g, v_lnx_b, v_conv_dw, v_conv_b, v_conv_ln_g, v_conv_ln_b, v_w_out, v_ffn2_w_in, v_ffn2_w_out, v_ln1_g, v_ln1_b, v_ln2_g, v_ln2_b, v_ln3_g, v_ln3_b):
    args = dict(locals())
    w = {n: args[n] for n in WEIGHTS}
    m = {n: args["m_" + n] for n in WEIGHTS}
    v = {n: args["v_" + n] for n in WEIGHTS}
    seq = x.shape[1]
    loss, grad_x, grad, delta, new_m, new_v = _train_step(x, loss_target, w, m, v, tt=min(256, seq))
    return (loss, grad_x, *[grad[n] for n in WEIGHTS], *[delta[n] for n in WEIGHTS],
            *[new_m[n] for n in WEIGHTS], *[new_v[n] for n in WEIGHTS])
```

```python
import functools

import jax
import jax.numpy as jnp
from jax import lax
from jax.experimental import pallas as pl
from jax.experimental.pallas import tpu as pltpu

F32 = jnp.float32
BF16 = jnp.bfloat16

D_MODEL = 1024
RW = 512
HEAD = 64
CW = 512
CONV_K = 31
CONV_ROWS = 32
SHIFT_ROWS = 16
CONV_PAD = 15
D_FF = 2816
LORA = 64
GATE_LORA = 160
GATE_PAD = 256
SHIFT_COLS = 1952
SHIFT_PAD = 2048
IN_COLS = 2976
IN_PAD = 3072
LN_EPS = 1e-5
GN_EPS = 64e-5
NORM_EPS = 1e-12
ALPHA = 2.0 ** 0.25
DECAY_SCALE = 0.6065306597126334
ADAM_LR, ADAM_B1, ADAM_B2, ADAM_EPS, ADAM_WD, ADAM_STEP = 0.001, 0.9, 0.999, 1e-08, 0.01, 10
N_CHIPS = 4
VMEM_LIMIT = 56 * 1024 * 1024
TM_FFN = 256
TM_LN = 512

MESH = pl.DeviceIdType.MESH


def _cparams(sem=None, **kw):
    return pltpu.CompilerParams(dimension_semantics=sem, vmem_limit_bytes=VMEM_LIMIT, **kw)


LANES = 128


def _pick_tile(dim, want):
    for t in range(min(want, dim) // LANES * LANES, 0, -LANES):
        if dim % t == 0:
            return t
    return dim


def _after_operand(after):
    return ([], []) if after is None else ([pl.BlockSpec(memory_space=pl.ANY)], [after])


def _matmul(a, b, *, ta=False, tb=False, out_dtype=F32, tm=1024, tn=1024, tk=1024, scale=1.0, col_slabs=False,
            after=None, name):
    after_specs, after_args = _after_operand(after)
    if ta:
        k_dim, m_dim = a.shape
    else:
        m_dim, k_dim = a.shape
    n_dim = b.shape[0] if tb else b.shape[1]
    tm, tn, tk = _pick_tile(m_dim, tm), _pick_tile(n_dim, tn), _pick_tile(k_dim, tk)
    assert m_dim % tm == 0 and n_dim % tn == 0 and k_dim % tk == 0, (name, a.shape, b.shape, tm, tn, tk)
    nk = k_dim // tk
    dims = (((0,) if ta else (1,), (1,) if tb else (0,)), ((), ()))
    if col_slabs:
        out_shape = jax.ShapeDtypeStruct((n_dim // tn, m_dim, tn), out_dtype)
        out_spec = pl.BlockSpec((None, tm, tn), lambda i, j, k: (j, i, 0))
    else:
        out_shape = jax.ShapeDtypeStruct((m_dim, n_dim), out_dtype)
        out_spec = pl.BlockSpec((tm, tn), lambda i, j, k: (i, j))

    def body(a_ref, b_ref, *rest):
        o_ref, acc_ref = rest[-2:]
        kk = pl.program_id(2)

        @pl.when(kk == 0)
        def _():
            acc_ref[...] = jnp.zeros_like(acc_ref)

        acc_ref[...] += lax.dot_general(a_ref[...].astype(BF16), b_ref[...].astype(BF16), dims,
                                        preferred_element_type=F32)

        @pl.when(kk == nk - 1)
        def _():
            o_ref[...] = (acc_ref[...] * scale).astype(o_ref.dtype)

    a_spec = pl.BlockSpec((tk, tm), lambda i, j, k: (k, i)) if ta else pl.BlockSpec((tm, tk), lambda i, j, k: (i, k))
    b_spec = pl.BlockSpec((tn, tk), lambda i, j, k: (j, k)) if tb else pl.BlockSpec((tk, tn), lambda i, j, k: (k, j))
    return pl.pallas_call(
        body, name=name,
        out_shape=out_shape,
        grid=(m_dim // tm, n_dim // tn, nk),
        in_specs=[a_spec, b_spec] + after_specs,
        out_specs=out_spec,
        scratch_shapes=[pltpu.VMEM((tm, tn), F32)],
        compiler_params=_cparams(("parallel", "parallel", "arbitrary")),
    )(a, b, *after_args)


def _whole(shape):
    nd = len(shape)
    return pl.BlockSpec(shape, lambda i: (0,) * nd)


def _ffn_in(x, w, *, tm, after=None, name):
    n_tok = x.shape[0]
    sw = w.shape[2]
    tm = min(tm, n_tok)

    after_specs, after_args = _after_operand(after)

    def body(x_ref, w_ref, *rest):
        h_ref, a_ref = rest[-2:]
        xb = x_ref[...].astype(BF16)
        for s in range(2):
            g = jnp.dot(xb, w_ref[s], preferred_element_type=F32)
            u = jnp.dot(xb, w_ref[s + 2], preferred_element_type=F32)
            h_ref[:, s * sw:(s + 1) * sw] = g.astype(BF16)
            h_ref[:, (s + 2) * sw:(s + 3) * sw] = u.astype(BF16)
            a_ref[:, s * sw:(s + 1) * sw] = (_silu(g) * u).astype(BF16)

    return pl.pallas_call(
        body, name=name,
        out_shape=[jax.ShapeDtypeStruct((n_tok, 2 * D_FF), BF16), jax.ShapeDtypeStruct((n_tok, D_FF), BF16)],
        grid=(n_tok // tm,),
        in_specs=[pl.BlockSpec((tm, D_MODEL), lambda i: (i, 0)), _whole(w.shape)] + after_specs,
        out_specs=[pl.BlockSpec((tm, 2 * D_FF), lambda i: (i, 0)), pl.BlockSpec((tm, D_FF), lambda i: (i, 0))],
        compiler_params=_cparams(("parallel",)),
    )(x, w, *after_args)


def _mm_ln(a_list, w, xres, g, b, fscale, *, tm, name):
    n_tok = xres.shape[0]
    tm = min(tm, n_tok)
    na = len(a_list)

    def body(*refs):
        a_refs = refs[:na]
        w_ref, x_ref, g_ref, b_ref, z_o, y_o, yb_o = refs[na:]
        f, off = None, 0
        for a_ref in a_refs:
            k = a_ref.shape[1]
            t = jnp.dot(a_ref[...].astype(BF16), w_ref[off:off + k, :], preferred_element_type=F32)
            f = t if f is None else f + t
            off += k
        z = ALPHA * x_ref[...] + fscale * f
        y = _layer_norm(z, g_ref[...], b_ref[...])
        z_o[...] = z
        y_o[...] = y
        yb_o[...] = y.astype(BF16)

    tile = pl.BlockSpec((tm, D_MODEL), lambda i: (i, 0))
    return pl.pallas_call(
        body, name=name,
        out_shape=[jax.ShapeDtypeStruct((n_tok, D_MODEL), F32)] * 2 + [jax.ShapeDtypeStruct((n_tok, D_MODEL), BF16)],
        grid=(n_tok // tm,),
        in_specs=[pl.BlockSpec((tm, a.shape[1]), lambda i: (i, 0)) for a in a_list]
        + [_whole(w.shape), tile, _whole(g.shape), _whole(b.shape)],
        out_specs=[tile, tile, tile],
        compiler_params=_cparams(("parallel",)),
    )(*a_list, w, xres, g, b)


def _mm_ln_loss(a, w, xres, g, b, target, fscale, *, tm, name):
    n_tok = xres.shape[0]
    tm = min(tm, n_tok)

    def body(a_ref, w_ref, x_ref, g_ref, b_ref, t_ref, dz_o, dg_o, db_o, loss_o):
        i = pl.program_id(0)
        z = ALPHA * x_ref[...] + fscale * jnp.dot(a_ref[...].astype(BF16), w_ref[...], preferred_element_type=F32)
        y, vjp = jax.vjp(_layer_norm, z, g_ref[...], b_ref[...])
        e = y - t_ref[...]
        dz, dg, db = vjp(e * (1.0 / D_MODEL))

        @pl.when(i == 0)
        def _():
            dg_o[...] = jnp.zeros_like(dg_o)
            db_o[...] = jnp.zeros_like(db_o)
            loss_o[...] = jnp.zeros_like(loss_o)
        dz_o[...] = dz
        dg_o[...] += dg
        db_o[...] += db
        loss_o[...] += 0.5 * jnp.sum(jnp.mean(e * e, axis=-1, keepdims=True), axis=0, keepdims=True)

    tile = pl.BlockSpec((tm, D_MODEL), lambda i: (i, 0))
    row = pl.BlockSpec((1, D_MODEL), lambda i: (0, 0))
    return pl.pallas_call(
        body, name=name,
        out_shape=[jax.ShapeDtypeStruct((n_tok, D_MODEL), F32), jax.ShapeDtypeStruct((1, D_MODEL), F32),
                   jax.ShapeDtypeStruct((1, D_MODEL), F32), jax.ShapeDtypeStruct((8, LANES), F32)],
        grid=(n_tok // tm,),
        in_specs=[pl.BlockSpec((tm, a.shape[1]), lambda i: (i, 0)), _whole(w.shape), tile, row, row, tile],
        out_specs=[tile, row, row, pl.BlockSpec((8, LANES), lambda i: (0, 0))],
        compiler_params=_cparams(("arbitrary",)),
    )(a, w, xres, g, b, target)


def _ffn_out_bwd(dz, w, h, *, tm, after=None, name):
    n_tok = dz.shape[0]
    tm = min(tm, n_tok)
    cw = D_FF // 2
    after_specs, after_args = _after_operand(after)

    def body(dz_ref, w_ref, h_ref, *rest):
        dh_ref = rest[-1]
        dzb = dz_ref[...].astype(BF16)
        for s in range(2):
            dact = 0.5 * lax.dot_general(dzb, w_ref[s * cw:(s + 1) * cw, :], (((1,), (1,)), ((), ())),
                                         preferred_element_type=F32)
            gate = h_ref[:, s * cw:(s + 1) * cw].astype(F32)
            up = h_ref[:, D_FF + s * cw:D_FF + (s + 1) * cw].astype(F32)
            sg = _sigmoid(gate)
            dh_ref[:, s * cw:(s + 1) * cw] = (dact * up * sg * (1.0 + gate * (1.0 - sg))).astype(BF16)
            dh_ref[:, D_FF + s * cw:D_FF + (s + 1) * cw] = (dact * gate * sg).astype(BF16)

    wide = pl.BlockSpec((tm, 2 * D_FF), lambda i: (i, 0))
    return pl.pallas_call(
        body, name=name,
        out_shape=jax.ShapeDtypeStruct((n_tok, 2 * D_FF), BF16),
        grid=(n_tok // tm,),
        in_specs=[pl.BlockSpec((tm, D_MODEL), lambda i: (i, 0)), _whole(w.shape), wide] + after_specs,
        out_specs=wide,
        compiler_params=_cparams(("parallel",)),
    )(dz, w, h, *after_args)


def _mm_nt_res(a_list, w, dz, *, tm, ln=None, after=None, name):
    n_tok = dz.shape[0]
    tm = min(tm, n_tok)
    na = len(a_list)
    nt = (((1,), (1,)), ((), ()))
    after_specs, after_args = _after_operand(after)
    n_out = 1 if ln is None else 3

    def body(*refs):
        a_refs = refs[:na]
        w_ref, dz_ref, o_ref = refs[na], refs[na + 1], refs[-n_out]
        acc = ALPHA * dz_ref[...]
        if len(w_ref.shape) == 3:
            cw = w_ref.shape[2]
            for s in range(w_ref.shape[0]):
                acc = acc + lax.dot_general(a_refs[0][:, s * cw:(s + 1) * cw], w_ref[s], nt, preferred_element_type=F32)
        else:
            off = 0
            for a_ref in a_refs:
                k = a_ref.shape[1]
                acc = acc + lax.dot_general(a_ref[...], w_ref[:, off:off + k], nt, preferred_element_type=F32)
                off += k
        if ln is None:
            o_ref[...] = acc
            return
        z_ref, g_ref, b_ref = refs[na + 2:na + 5]
        dg_o, db_o = refs[-2:]
        _, vjp = jax.vjp(_layer_norm, z_ref[...], g_ref[...], b_ref[...])
        o_ref[...], dg, db = vjp(acc)

        @pl.when(pl.program_id(0) == 0)
        def _():
            dg_o[...] = jnp.zeros_like(dg_o)
            db_o[...] = jnp.zeros_like(db_o)
        dg_o[...] += dg
        db_o[...] += db

    tile = pl.BlockSpec((tm, D_MODEL), lambda i: (i, 0))
    row = pl.BlockSpec((1, D_MODEL), lambda i: (0, 0))
    out_shape = [jax.ShapeDtypeStruct((n_tok, D_MODEL), F32)]
    ln_specs, ln_args, out_specs = [], [], [tile]
    if ln is not None:
        ln_specs, ln_args = [tile, row, row], list(ln)
        out_shape += [jax.ShapeDtypeStruct((1, D_MODEL), F32)] * 2
        out_specs += [row, row]
    outs = pl.pallas_call(
        body, name=name,
        out_shape=out_shape,
        grid=(n_tok // tm,),
        in_specs=[pl.BlockSpec((tm, a.shape[1]), lambda i: (i, 0)) for a in a_list] + [_whole(w.shape), tile]
        + ln_specs + after_specs,
        out_specs=out_specs,
        compiler_params=_cparams(("parallel",) if ln is None else ("arbitrary",)),
    )(*a_list, w, dz, *ln_args, *after_args)
    return outs[0] if ln is None else outs


def _rowcall(fn, tok_in, full_in, tok_out, acc_out, *, tt, name):
    views = [a if isinstance(a, tuple) else (a, a.shape[1], 0) for a in tok_in]
    tok_in = [a for a, _, _ in views]
    n_tok = tok_in[0].shape[0]
    assert n_tok % tt == 0, (name, n_tok, tt)
    n_ti, n_fi, n_to = len(tok_in), len(full_in), len(tok_out)

    def body(*refs):
        i = pl.program_id(0)
        ins = [r[...] for r in refs[:n_ti + n_fi]]
        outs = fn(i, *ins)
        o_refs = refs[n_ti + n_fi:]
        for r, val in zip(o_refs[:n_to], outs[:n_to]):
            r[...] = val.astype(r.dtype)
        if acc_out:
            @pl.when(i == 0)
            def _():
                for r in o_refs[n_to:]:
                    r[...] = jnp.zeros_like(r)
            for r, val in zip(o_refs[n_to:], outs[n_to:]):
                r[...] += val.reshape(r.shape).astype(F32)

    in_specs = [pl.BlockSpec((tt, width), functools.partial(lambda k, i: (i, k), k)) for _, width, k in views]
    in_specs += [pl.BlockSpec(a.shape, lambda i: (0, 0)) for a in full_in]
    out_specs = [pl.BlockSpec((tt, c), lambda i: (i, 0)) for c, _ in tok_out]
    out_specs += [pl.BlockSpec(s, lambda i: (0, 0)) for s in acc_out]
    out_shape = [jax.ShapeDtypeStruct((n_tok, c), dt) for c, dt in tok_out]
    out_shape += [jax.ShapeDtypeStruct(s, F32) for s in acc_out]
    return pl.pallas_call(
        body, name=name, out_shape=out_shape, grid=(n_tok // tt,), in_specs=in_specs, out_specs=out_specs,
        compiler_params=_cparams(("arbitrary",) if acc_out else ("parallel",)),
    )(*tok_in, *full_in)


@jax.custom_vjp
def _bdot(a, b):
    return jnp.dot(a.astype(BF16), b.astype(BF16), preferred_element_type=F32)


def _bdot_fwd(a, b):
    return _bdot(a, b), (a, b)


def _bdot_bwd(res, g):
    a, b = res
    g16 = g.astype(BF16)
    da = lax.dot_general(g16, b.astype(BF16), (((1,), (1,)), ((), ())), preferred_element_type=F32)
    db = lax.dot_general(a.astype(BF16), g16, (((0,), (0,)), ((), ())), preferred_element_type=F32)
    return da, db


_bdot.defvjp(_bdot_fwd, _bdot_bwd)


def _split16(x):
    hi = x.astype(BF16)
    lo = (x - hi.astype(F32)).astype(BF16)
    return hi, lo


def _segsum_raw(x, e2):
    hi, lo = _split16(x)
    outs = []
    for c in range(x.shape[1] // 256):
        lhs = jnp.concatenate([hi[:, 256 * c:256 * (c + 1)], lo[:, 256 * c:256 * (c + 1)]], axis=1)
        outs.append(jnp.dot(lhs, e2, preferred_element_type=F32))
    return jnp.concatenate(outs, axis=1)


@jax.custom_vjp
def _segsum(x, e2):
    return _segsum_raw(x, e2)


def _segsum_fwd(x, e2):
    return _segsum_raw(x, e2), e2


def _segsum_bwd(e2, g):
    return _segsum_raw(g, e2), jnp.zeros_like(e2)


_segsum.defvjp(_segsum_fwd, _segsum_bwd)


def _head_ones():
    r = lax.broadcasted_iota(jnp.int32, (512, 256), 0) % 256
    c = lax.broadcasted_iota(jnp.int32, (512, 256), 1)
    return (r // HEAD == c // HEAD).astype(BF16)


def _sigmoid(x):
    return 1.0 / (1.0 + jnp.exp(-x))


def _silu(x):
    return x * _sigmoid(x)


def _layer_norm(z, g, b, eps=LN_EPS):
    mu = jnp.mean(z, axis=-1, keepdims=True)
    zc = z - mu
    var = jnp.mean(zc * zc, axis=-1, keepdims=True)
    return zc * lax.rsqrt(var + eps) * g + b


def _prep(ps, w2b, w0c, a2b, a0c, g2p, k_k, k_a, e2):
    r, k, v = ps[:, 0:512], ps[:, 512:1024], ps[:, 1024:1536]
    wd, ad, gd = ps[:, 1536:1664], ps[:, 1664:1792], ps[:, 1792:2048]
    lw = _bdot(jnp.tanh(wd), w2b) + w0c
    decay = -DECAY_SCALE * _sigmoid(lw)
    a = _sigmoid(_bdot(ad, a2b) + a0c)
    g = _bdot(_sigmoid(gd), g2p)
    kkr = k * k_k
    nrm = jnp.sqrt(_segsum(kkr * kkr, e2))
    kk = kkr / jnp.maximum(nrm, NORM_EPS)
    k2 = jnp.concatenate([k, k], axis=1)
    ka2 = jnp.concatenate([k_a, k_a], axis=1)
    kd = k2 * (1.0 + (a - 1.0) * ka2)
    b = jnp.concatenate([kk, kk], axis=1) * a
    return r, v, kk, decay, kd, b, g


def _post(y0, y1, r, v, kd, g, lnx_g, lnx_b, r_k, e2):
    y = y0 + y1
    mu = _segsum(y, e2) * (1.0 / HEAD)
    yc = y - mu
    var = _segsum(yc * yc, e2) * (1.0 / HEAD)
    yn = yc * lax.rsqrt(var + GN_EPS) * lnx_g + lnx_b
    bonus = _segsum(r * (kd[:, :RW] + kd[:, RW:]) * r_k, e2)
    return (yn + bonus * v) * g


def _conv_post(yc, ln_g, ln_b):
    return _silu(_layer_norm(yc, ln_g, ln_b))


def _halo_specs(cols_block, hb, tt, n_tok, col_idx):
    nb = n_tok // hb
    prev = pl.BlockSpec((hb, cols_block), lambda i: (jnp.maximum(i * (tt // hb) - 1, 0), col_idx))
    nxt = pl.BlockSpec((hb, cols_block), lambda i: (jnp.minimum((i + 1) * (tt // hb), nb - 1), col_idx))
    return prev, nxt


def _mix_prep(p, mu_p, mu_n, w2b, w0c, a2b, a0c, g2p, k_k, k_a, *, seq, tt, name):
    n_tok = p.shape[0]
    tps = seq // tt
    e2 = _head_ones()

    def body(p_ref, hp_ref, hn_ref, mup_ref, mun_ref, w2b_ref, w0c_ref, a2b_ref, a0c_ref, g2p_ref, kk_ref, ka_ref,
             e2_ref, r_o, v_o, kk_o, w_o, kd_o, b_o, g_o, ext):
        i = pl.program_id(0)
        first = (i % tps) == 0
        last = (i % tps) == tps - 1
        pv = p_ref[...]
        ext[pl.ds(0, 8), :] = jnp.where(first, 0.0, hp_ref[...])
        ext[pl.ds(8, tt), :] = pv
        ext[pl.ds(8 + tt, 8), :] = jnp.where(last, 0.0, hn_ref[...])
        prev = ext[pl.ds(7, tt), :]
        nxt = ext[pl.ds(9, tt), :]
        ps = pv + mup_ref[...] * (prev - pv) + mun_ref[...] * (nxt - pv)
        outs = _prep(ps, w2b_ref[...], w0c_ref[...], a2b_ref[...], a0c_ref[...], g2p_ref[...], kk_ref[...],
                     ka_ref[...], e2_ref[...])
        for o_ref, val in zip((r_o, v_o, kk_o, w_o, kd_o, b_o, g_o), outs):
            o_ref[...] = val

    hp, hn = _halo_specs(SHIFT_PAD, 8, tt, n_tok, 0)
    fulls = [mu_p, mu_n, w2b, w0c, a2b, a0c, g2p, k_k, k_a, e2]
    widths = (RW, RW, RW, 2 * RW, 2 * RW, 2 * RW, RW)
    return pl.pallas_call(
        body, name=name,
        out_shape=[jax.ShapeDtypeStruct((n_tok, c), F32) for c in widths],
        grid=(n_tok // tt,),
        in_specs=[pl.BlockSpec((tt, SHIFT_PAD), lambda i: (i, 0)), hp, hn]
        + [pl.BlockSpec(a.shape, lambda i: (0, 0)) for a in fulls],
        out_specs=[pl.BlockSpec((tt, c), lambda i: (i, 0)) for c in widths],
        scratch_shapes=[pltpu.VMEM((tt + 16, SHIFT_PAD), F32)],
        compiler_params=_cparams(("parallel",)),
    )(p, p, p, *fulls)


def _mix_prep_bwd(p, mu_p, mu_n, w2b, w0c, a2b, a0c, g2p, k_k, k_a, ct_terms, *, seq, tt, name):
    n_tok = p.shape[0]
    tps = seq // tt
    e2 = _head_ones()
    acc_shapes = [w2b.shape, w0c.shape, a2b.shape, a0c.shape, g2p.shape, k_k.shape, k_a.shape]
    cts = [a for terms in ct_terms for t in terms for a in (t if isinstance(t, tuple) else (t,))]

    def body(p_ref, hp_ref, hn_ref, mup_ref, mun_ref, w2b_ref, w0c_ref, a2b_ref, a0c_ref, g2p_ref, kk_ref, ka_ref,
             e2_ref, *rest):
        ct_refs, dps_o, acc_refs, ext = rest[:len(cts)], rest[len(cts)], rest[len(cts) + 1:-1], rest[-1]
        ct_it = iter(ct_refs)
        ct_vals = []
        for terms in ct_terms:
            total = None
            for t in terms:
                if isinstance(t, tuple):
                    val = jnp.concatenate([next(ct_it)[...] for _ in t], axis=1)
                else:
                    val = next(ct_it)[...]
                total = val if total is None else total + val
            ct_vals.append(total)
        i = pl.program_id(0)
        first = (i % tps) == 0
        last = (i % tps) == tps - 1
        pv = p_ref[...]
        ext[pl.ds(0, 8), :] = jnp.where(first, 0.0, hp_ref[...])
        ext[pl.ds(8, tt), :] = pv
        ext[pl.ds(8 + tt, 8), :] = jnp.where(last, 0.0, hn_ref[...])
        prev = ext[pl.ds(7, tt), :]
        nxt = ext[pl.ds(9, tt), :]
        ps = pv + mup_ref[...] * (prev - pv) + mun_ref[...] * (nxt - pv)
        e2v = e2_ref[...]
        _, vjp = jax.vjp(lambda *a: _prep(*a, e2v), ps, w2b_ref[...], w0c_ref[...], a2b_ref[...], a0c_ref[...],
                         g2p_ref[...], kk_ref[...], ka_ref[...])
        grads = vjp(tuple(ct_vals))
        dps_o[...] = grads[0]

        @pl.when(i == 0)
        def _():
            for r in acc_refs:
                r[...] = jnp.zeros_like(r)
        for r, val in zip(acc_refs, grads[1:]):
            r[...] += val

    hp, hn = _halo_specs(SHIFT_PAD, 8, tt, n_tok, 0)
    fulls = [mu_p, mu_n, w2b, w0c, a2b, a0c, g2p, k_k, k_a, e2]
    return pl.pallas_call(
        body, name=name,
        out_shape=[jax.ShapeDtypeStruct((n_tok, SHIFT_PAD), F32)] + [jax.ShapeDtypeStruct(s, F32) for s in acc_shapes],
        grid=(n_tok // tt,),
        in_specs=[pl.BlockSpec((tt, SHIFT_PAD), lambda i: (i, 0)), hp, hn]
        + [pl.BlockSpec(a.shape, lambda i: (0, 0)) for a in fulls]
        + [pl.BlockSpec((tt, c.shape[1]), lambda i: (i, 0)) for c in cts],
        out_specs=[pl.BlockSpec((tt, SHIFT_PAD), lambda i: (i, 0))] + [pl.BlockSpec(s, lambda i: (0, 0)) for s in acc_shapes],
        scratch_shapes=[pltpu.VMEM((tt + 16, SHIFT_PAD), F32)],
        compiler_params=_cparams(("arbitrary",)),
    )(p, p, p, *fulls, *cts)


def _shift_bwd(dps, p, mu_p, mu_n, *, seq, tt, name):
    n_tok = p.shape[0]
    tps = seq // tt

    def body(d_ref, dhp_ref, dhn_ref, p_ref, php_ref, phn_ref, mup_ref, mun_ref, dp_o, dmup_o, dmun_o, ext):
        i = pl.program_id(0)
        first = (i % tps) == 0
        last = (i % tps) == tps - 1
        mup, mun = mup_ref[...], mun_ref[...]
        rb = min(SHIFT_ROWS, tt)
        ext[pl.ds(0, 8), :] = jnp.where(first, 0.0, dhp_ref[...])
        ext[pl.ds(8, tt), :] = d_ref[...]
        ext[pl.ds(8 + tt, 8), :] = jnp.where(last, 0.0, dhn_ref[...])
        for r0 in range(0, tt, rb):
            dv = d_ref[pl.ds(r0, rb), :]
            dp_o[pl.ds(r0, rb), :] = (dv * (1.0 - mup - mun) + ext[pl.ds(r0 + 9, rb), :] * mup
                                      + ext[pl.ds(r0 + 7, rb), :] * mun).astype(dp_o.dtype)
        ext[pl.ds(0, 8), :] = jnp.where(first, 0.0, php_ref[...])
        ext[pl.ds(8, tt), :] = p_ref[...]
        ext[pl.ds(8 + tt, 8), :] = jnp.where(last, 0.0, phn_ref[...])

        @pl.when(i == 0)
        def _():
            dmup_o[...] = jnp.zeros_like(dmup_o)
            dmun_o[...] = jnp.zeros_like(dmun_o)
        sum_p = jnp.zeros_like(mup)
        sum_n = jnp.zeros_like(mun)
        for r0 in range(0, tt, rb):
            dv, pv = d_ref[pl.ds(r0, rb), :], p_ref[pl.ds(r0, rb), :]
            sum_p = sum_p + jnp.sum(dv * (ext[pl.ds(r0 + 7, rb), :] - pv), axis=0, keepdims=True)
            sum_n = sum_n + jnp.sum(dv * (ext[pl.ds(r0 + 9, rb), :] - pv), axis=0, keepdims=True)
        dmup_o[...] += sum_p
        dmun_o[...] += sum_n

    hp, hn = _halo_specs(SHIFT_PAD, 8, tt, n_tok, 0)
    tile = pl.BlockSpec((tt, SHIFT_PAD), lambda i: (i, 0))
    full = pl.BlockSpec((1, SHIFT_PAD), lambda i: (0, 0))
    return pl.pallas_call(
        body, name=name,
        out_shape=[jax.ShapeDtypeStruct((n_tok, SHIFT_PAD), BF16), jax.ShapeDtypeStruct((1, SHIFT_PAD), F32),
                   jax.ShapeDtypeStruct((1, SHIFT_PAD), F32)],
        grid=(n_tok // tt,),
        in_specs=[tile, hp, hn, tile, hp, hn, full, full],
        out_specs=[tile, full, full],
        scratch_shapes=[pltpu.VMEM((tt + 16, SHIFT_PAD), F32)],
        compiler_params=_cparams(("arbitrary",)),
    )(dps, dps, dps, p, p, p, mu_p, mu_n)


def _mix_post(y0, y1, r, v, kd, g, lnx_g, lnx_b, r_k, *, tt, name):
    e2 = _head_ones()
    return _rowcall(lambda i, *a: (_post(*a),), [y0, y1, r, v, kd, g], [lnx_g, lnx_b, r_k, e2], [(RW, BF16)], [],
                    tt=tt, name=name)[0]


def _mix_post_bwd(y0, y1, r, v, kd, g, lnx_g, lnx_b, r_k, dout, *, tt, name):
    e2 = _head_ones()

    def fn(i, y0v, y1v, rv, vv, kdv, gv, dov, lg, lb, rk, e2v):
        _, vjp = jax.vjp(lambda *a: _post(*a, e2v), y0v, y1v, rv, vv, kdv, gv, lg, lb, rk)
        gr = vjp(dov.astype(F32))
        return gr[0], gr[2], gr[3], gr[4], gr[5], gr[6], gr[7], gr[8]
    return _rowcall(fn, [y0, y1, r, v, kd, g, dout], [lnx_g, lnx_b, r_k, e2],
                    [(RW, F32), (RW, F32), (RW, F32), (2 * RW, F32), (RW, F32)], [(1, RW), (1, RW), (1, RW)],
                    tt=tt, name=name)


def _conv_fwd(p, dw, db, ln_g, ln_b, *, seq, tt, name):
    n_tok = p.shape[0]
    tps = seq // tt

    def glu(x, gate):
        return x * _sigmoid(gate)

    def body(u_ref, g_ref, uhp, ghp, uhn, ghn, dw_ref, db_ref, lg_ref, lb_ref, yc_o, y_o, ext):
        i = pl.program_id(0)
        first = (i % tps) == 0
        last = (i % tps) == tps - 1
        ext[pl.ds(0, 16), :] = jnp.where(first, 0.0, glu(uhp[...], ghp[...]))
        ext[pl.ds(16, tt), :] = glu(u_ref[...], g_ref[...])
        ext[pl.ds(16 + tt, 16), :] = jnp.where(last, 0.0, glu(uhn[...], ghn[...]))
        taps = [dw_ref[pl.ds(k, 1), :] for k in range(CONV_K)]
        for r0 in range(0, tt, CONV_ROWS):
            acc = jnp.zeros((CONV_ROWS, CW), F32) + db_ref[...]
            for k in range(CONV_K):
                acc = acc + ext[pl.ds(r0 + k + 1, CONV_ROWS), :] * taps[k]
            yc_o[pl.ds(r0, CONV_ROWS), :] = acc
        y_o[...] = _conv_post(yc_o[...], lg_ref[...], lb_ref[...]).astype(y_o.dtype)

    uhp_s, uhn_s = _halo_specs(CW, 16, tt, n_tok, 4)
    ghp_s, ghn_s = _halo_specs(CW, 16, tt, n_tok, 5)
    fulls = [dw, db, ln_g, ln_b]
    return pl.pallas_call(
        body, name=name,
        out_shape=[jax.ShapeDtypeStruct((n_tok, CW), F32), jax.ShapeDtypeStruct((n_tok, CW), BF16)],
        grid=(n_tok // tt,),
        in_specs=[pl.BlockSpec((tt, CW), lambda i: (i, 4)), pl.BlockSpec((tt, CW), lambda i: (i, 5)),
                  uhp_s, ghp_s, uhn_s, ghn_s] + [pl.BlockSpec(a.shape, lambda i: (0, 0)) for a in fulls],
        out_specs=[pl.BlockSpec((tt, CW), lambda i: (i, 0)), pl.BlockSpec((tt, CW), lambda i: (i, 0))],
        scratch_shapes=[pltpu.VMEM((tt + 32, CW), F32)],
        compiler_params=_cparams(("parallel",)),
    )(p, p, p, p, p, p, *fulls)


def _conv_post_bwd(yc, dy, ln_g, ln_b, *, tt, name):
    def fn(i, ycv, dyv, lg, lb):
        _, vjp = jax.vjp(_conv_post, ycv, lg, lb)
        dyc, dg, dbb = vjp(dyv.astype(F32))
        return dyc, dg, dbb, jnp.sum(dyc, axis=0, keepdims=True)
    return _rowcall(fn, [yc, dy], [ln_g, ln_b], [(CW, F32)], [(1, CW), (1, CW), (1, CW)], tt=tt, name=name)


def _conv_bwd(dyc, p, dw, *, seq, tt, name):
    n_tok = p.shape[0]
    tps = seq // tt

    def body(d_ref, dhp, dhn, u_ref, g_ref, uhp, ghp, uhn, ghn, dw_ref, dp_o, ddw_o, ext):
        i = pl.program_id(0)
        first = (i % tps) == 0
        last = (i % tps) == tps - 1
        dv = d_ref[...]
        ext[pl.ds(0, 16), :] = jnp.where(first, 0.0, dhp[...])
        ext[pl.ds(16, tt), :] = dv
        ext[pl.ds(16 + tt, 16), :] = jnp.where(last, 0.0, dhn[...])
        taps = [dw_ref[pl.ds(k, 1), :] for k in range(CONV_K)]
        for r0 in range(0, tt, CONV_ROWS):
            du = jnp.zeros((CONV_ROWS, CW), F32)
            for k in range(CONV_K):
                du = du + ext[pl.ds(r0 + 31 - k, CONV_ROWS), :] * taps[k]
            rows = pl.ds(r0, CONV_ROWS)
            sg_r = _sigmoid(g_ref[rows, :])
            dp_o[rows, 0:CW] = (du * sg_r).astype(dp_o.dtype)
            dp_o[rows, CW:2 * CW] = (du * u_ref[rows, :] * sg_r * (1.0 - sg_r)).astype(dp_o.dtype)
        uv, gv = u_ref[...], g_ref[...]
        sg = _sigmoid(gv)
        ext[pl.ds(0, 16), :] = jnp.where(first, 0.0, uhp[...] * _sigmoid(ghp[...]))
        ext[pl.ds(16, tt), :] = uv * sg
        ext[pl.ds(16 + tt, 16), :] = jnp.where(last, 0.0, uhn[...] * _sigmoid(ghn[...]))

        @pl.when(i == 0)
        def _():
            ddw_o[...] = jnp.zeros_like(ddw_o)
        for k in range(CONV_K):
            ddw_o[pl.ds(k, 1), :] += jnp.sum(dv * ext[pl.ds(k + 1, tt), :], axis=0, keepdims=True)

    dhp_s, dhn_s = _halo_specs(CW, 16, tt, n_tok, 0)
    uhp_s, uhn_s = _halo_specs(CW, 16, tt, n_tok, 4)
    ghp_s, ghn_s = _halo_specs(CW, 16, tt, n_tok, 5)
    return pl.pallas_call(
        body, name=name,
        out_shape=[jax.ShapeDtypeStruct((n_tok, 2 * CW), BF16), jax.ShapeDtypeStruct((32, CW), F32)],
        grid=(n_tok // tt,),
        in_specs=[pl.BlockSpec((tt, CW), lambda i: (i, 0)), dhp_s, dhn_s,
                  pl.BlockSpec((tt, CW), lambda i: (i, 4)), pl.BlockSpec((tt, CW), lambda i: (i, 5)),
                  uhp_s, ghp_s, uhn_s, ghn_s, pl.BlockSpec(dw.shape, lambda i: (0, 0))],
        out_specs=[pl.BlockSpec((tt, 2 * CW), lambda i: (i, 0)), pl.BlockSpec((32, CW), lambda i: (0, 0))],
        scratch_shapes=[pltpu.VMEM((tt + 32, CW), F32)],
        compiler_params=_cparams(("arbitrary",)),
    )(dyc, dyc, dyc, p, p, p, p, p, p, dw)


CHUNK = 64
_MM_DIMS = {"nn": (((2,), (1,)), ((0,), (0,))), "nt": (((2,), (2,)), ((0,), (0,))), "tn": (((1,), (1,)), ((0,), (0,)))}


def _mm16_raw(a, b, mode, fine):
    dot = lambda x, y: lax.dot_general(x, y, _MM_DIMS[mode], preferred_element_type=F32)
    if not fine:
        return dot(a.astype(BF16), b.astype(BF16))
    ah, (bh, bl) = a.astype(BF16), _split16(b)
    return dot(ah, bh) + dot(ah, bl)


@functools.partial(jax.custom_vjp, nondiff_argnums=(2, 3))
def _mm16(a, b, mode, fine=False):
    return _mm16_raw(a, b, mode, fine)


def _mm16_fwd(a, b, mode, fine):
    return _mm16_raw(a, b, mode, fine), (a, b)


def _mm16_bwd(mode, fine, res, g):
    a, b = res
    if mode == "nn":
        return _mm16_raw(g, b, "nt", fine), _mm16_raw(a, g, "tn", fine)
    if mode == "nt":
        return _mm16_raw(g, b, "nn", fine), _mm16_raw(g, a, "tn", fine)
    return _mm16_raw(b, g, "nt", fine), _mm16_raw(a, g, "nn", fine)


_mm16.defvjp(_mm16_fwd, _mm16_bwd)


def _tri_sum_raw(x, tri, mode):
    hi = x.astype(BF16)
    r1 = x - hi.astype(F32)
    mid = r1.astype(BF16)
    lo = (r1 - mid.astype(F32)).astype(BF16)
    dot = lambda p: lax.dot_general(tri, p, _MM_DIMS[mode], preferred_element_type=F32)
    return dot(hi) + dot(mid) + dot(lo)


@jax.custom_vjp
def _tri_sum(x, tri):
    return _tri_sum_raw(x, tri, "nn")


def _tri_sum_fwd(x, tri):
    return _tri_sum_raw(x, tri, "nn"), tri


def _tri_sum_bwd(tri, g):
    return _tri_sum_raw(g, tri, "tn"), jnp.zeros_like(tri)


_tri_sum.defvjp(_tri_sum_fwd, _tri_sum_bwd)


PAIR = 2 * HEAD
N_PAIRS = RW // PAIR


def _pair_rows(x):
    first = lax.broadcasted_iota(jnp.int32, x.shape, 2) < HEAD
    return jnp.concatenate([jnp.where(first, x, 0.0), jnp.where(first, 0.0, x)], axis=1)


def _chunk_step_pairs(s0, r, lw, k, v, kk, b, tri, rev):
    g, n, _ = r.shape
    row = lax.broadcasted_iota(jnp.int32, (g, n, PAIR), 1)
    col = lax.broadcasted_iota(jnp.int32, (g, n, PAIR), 2) % HEAD
    if rev:
        row, col = col, row
    diag = (lax.broadcasted_iota(jnp.int32, (g, PAIR, PAIR), 1) // HEAD
            == lax.broadcasted_iota(jnp.int32, (g, PAIR, PAIR), 2) // HEAD)
    cum = _tri_sum(lw, tri)
    up, down = jnp.exp(cum), jnp.exp(-cum)
    at, rt = -kk * jnp.exp(cum - lw), r * up
    kt, bt = k * down, b * down
    bt_rows, kt_rows = _pair_rows(bt), _pair_rows(kt)
    ar = jnp.concatenate([at, rt], axis=1)
    with_b = _mm16(ar, bt_rows, "nt")
    a_ab = jnp.where(col < row, with_b[:, :n], 0.0)
    a_rb = jnp.where(col <= row, with_b[:, n:], 0.0)
    with_k = _mm16(ar, kt_rows, "nt", True)
    a_ak = jnp.where(col < row, with_k[:, :n], 0.0)
    a_rk = jnp.where(col <= row, with_k[:, n:], 0.0)
    v_rows = _pair_rows(v)
    from_state = _mm16(ar, s0, "nt")
    u = from_state[:, :n] + _mm16(a_ak, v_rows, "nn")
    power = a_ab
    steps = n.bit_length() - 1
    for it in range(steps):
        if it + 1 < steps:
            both = _mm16(power, jnp.concatenate([_pair_rows(u), _pair_rows(power)], axis=2), "nn")
            u = u + both[:, :, :PAIR]
            power = both[:, :, PAIR:]
        else:
            u = u + _mm16(power, _pair_rows(u), "nn")
    y = from_state[:, n:] + _mm16(jnp.concatenate([a_rk, a_rb], axis=2),
                                  jnp.concatenate([v_rows, _pair_rows(u)], axis=1), "nn")
    grown = s0 + jnp.where(diag, _mm16(jnp.concatenate([v, u], axis=1), jnp.concatenate([kt, bt], axis=1), "tn"), 0.0)
    return y, grown * jnp.exp(jnp.sum(lw, axis=1, keepdims=True))


SCAN_SEQS = 4


def _tri_ones(rev, nseq):
    shape = (nseq * N_PAIRS, CHUNK, CHUNK)
    row, col = lax.broadcasted_iota(jnp.int32, shape, 1), lax.broadcasted_iota(jnp.int32, shape, 2)
    return ((col >= row) if rev else (col <= row)).astype(BF16)


def _split_heads(ref):
    return jnp.stack([ref[q, :, pl.ds(h * PAIR, PAIR)] for q in range(ref.shape[0]) for h in range(N_PAIRS)])


def _merge_heads(ref, val):
    for q in range(ref.shape[0]):
        for h in range(N_PAIRS):
            ref[q, :, pl.ds(h * PAIR, PAIR)] = val[q * N_PAIRS + h]


def _chunk_specs(nseq, nc, rev, dcol):
    chunk = (lambda c: nc - 1 - c) if rev else (lambda c: c)
    shared = pl.BlockSpec((nseq, CHUNK, RW), lambda s, c: (s, chunk(c), 0))
    own = pl.BlockSpec((nseq, CHUNK, RW), lambda s, c: (s, chunk(c), dcol))
    return shared, own


def _wkv_chunk_fwd(r, lw, k, v, kk, b, *, rev, name):
    bsz, seq, _ = r.shape
    nc = seq // CHUNK
    nseq = SCAN_SEQS if bsz % SCAN_SEQS == 0 else 1
    shared, own = _chunk_specs(nseq, nc, rev, int(rev))

    def body(r_ref, lw_ref, k_ref, v_ref, kk_ref, b_ref, tri_ref, y_o, s0_o, s_ref):
        @pl.when(pl.program_id(1) == 0)
        def _():
            s_ref[...] = jnp.zeros_like(s_ref)

        s0 = s_ref[...]
        s0_o[:, 0] = s0.reshape(nseq, N_PAIRS, PAIR, PAIR)
        y, s_ref[...] = _chunk_step_pairs(s0, *[_split_heads(x) for x in (r_ref, lw_ref, k_ref, v_ref, kk_ref, b_ref)],
                                    tri_ref[...], rev)
        _merge_heads(y_o, y)

    return pl.pallas_call(
        body, name=name,
        out_shape=[jax.ShapeDtypeStruct((bsz, seq, RW), F32), jax.ShapeDtypeStruct((bsz, nc, N_PAIRS, PAIR, PAIR), F32)],
        grid=(bsz // nseq, nc),
        in_specs=[shared, own, own, shared, shared, own,
                  pl.BlockSpec((nseq * N_PAIRS, CHUNK, CHUNK), lambda s, c: (0, 0, 0))],
        out_specs=[shared, pl.BlockSpec((nseq, 1, N_PAIRS, PAIR, PAIR), lambda s, c: (s, c, 0, 0, 0))],
        scratch_shapes=[pltpu.VMEM((nseq * N_PAIRS, PAIR, PAIR), F32)],
        compiler_params=_cparams(("parallel", "arbitrary")),
    )(r, lw, k, v, kk, b, _tri_ones(rev, nseq))


def _wkv_chunk_bwd(r, lw, k, v, kk, b, dy, s0, *, rev, name):
    bsz, seq, _ = r.shape
    nc = seq // CHUNK
    nseq = SCAN_SEQS if bsz % SCAN_SEQS == 0 else 1
    shared, own = _chunk_specs(nseq, nc, not rev, int(rev))

    def body(r_ref, lw_ref, k_ref, v_ref, kk_ref, b_ref, dy_ref, s0_ref, tri_ref, *rest):
        outs, ds_ref = rest[:-1], rest[-1]

        @pl.when(pl.program_id(1) == 0)
        def _():
            ds_ref[...] = jnp.zeros_like(ds_ref)

        triv = tri_ref[...]
        _, vjp = jax.vjp(lambda *a: _chunk_step_pairs(*a, triv, rev), s0_ref[:, 0].reshape(nseq * N_PAIRS, PAIR, PAIR),
                         *[_split_heads(x) for x in (r_ref, lw_ref, k_ref, v_ref, kk_ref, b_ref)])
        grads = vjp((_split_heads(dy_ref), ds_ref[...]))
        ds_ref[...] = grads[0]
        for o, gval in zip(outs, grads[1:]):
            _merge_heads(o, gval)

    return pl.pallas_call(
        body, name=name,
        out_shape=[jax.ShapeDtypeStruct((bsz, seq, RW), F32)] * 6,
        grid=(bsz // nseq, nc),
        in_specs=[shared, own, own, shared, shared, own, shared,
                  pl.BlockSpec((nseq, 1, N_PAIRS, PAIR, PAIR), lambda s, c: (s, nc - 1 - c, 0, 0, 0)),
                  pl.BlockSpec((nseq * N_PAIRS, CHUNK, CHUNK), lambda s, c: (0, 0, 0))],
        out_specs=[shared] * 6,
        scratch_shapes=[pltpu.VMEM((nseq * N_PAIRS, PAIR, PAIR), F32)],
        compiler_params=_cparams(("parallel", "arbitrary")),
    )(r, lw, k, v, kk, b, dy, s0, _tri_ones(rev, nseq))


def _block_diag2(w):
    z = jnp.zeros_like(w[0])
    return jnp.concatenate([jnp.concatenate([w[0], z], axis=1), jnp.concatenate([z, w[1]], axis=1)], axis=0)


def _pad_in_cols(a):
    z = jnp.zeros(a.shape[:-1] + (SHIFT_PAD - SHIFT_COLS,), a.dtype)
    return jnp.concatenate([a[..., :SHIFT_COLS], z, a[..., SHIFT_COLS:]], axis=-1)


def _follow(small, token):
    return small if token is None else small + token[0:1, 0:1]


def _local_step(x, target, wts, *, tt, start_token=None, more_weights=None, grads_ready=None, small_ready=None):
    bsz, seq, _ = x.shape
    n_tok = bsz * seq
    row = lambda a: a.reshape(1, -1).astype(F32)
    x0 = x.reshape(n_tok, D_MODEL)
    tgt = target.reshape(n_tok, D_MODEL)
    ln = {k: row(wts[k]) for k in ("ln1_g", "ln1_b", "ln2_g", "ln2_b", "ln3_g", "ln3_b")}
    if grads_ready is None:
        grads_ready = lambda names, slabs: None

    w1i = wts["ffn1_w_in"]
    h1, act1 = _ffn_in(x0, w1i, tm=TM_FFN, after=start_token, name="ffn1_in")
    if more_weights is not None:
        wts = {**wts, **more_weights("ffn1_out", act1)}
    w1o = wts["ffn1_w_out"]
    z1, x1, x1b = _mm_ln([act1], w1o, x0, ln["ln1_g"], ln["ln1_b"], 0.5, tm=TM_LN, name="ffn1_out_ln1")
    if more_weights is not None:
        wts = {**wts, **more_weights("mix", x1b)}
    win = _pad_in_cols(wts["w_in"])
    zpad = jnp.zeros((1, SHIFT_PAD - SHIFT_COLS), F32)
    mu_p = jnp.concatenate([row(wts["mu_prev"]), zpad], axis=1)
    mu_n = jnp.concatenate([row(wts["mu_next"]), zpad], axis=1)
    w2b, a2b = _block_diag2(wts["w2"]), _block_diag2(wts["a2"])
    w0c, a0c = row(wts["w0"]), row(wts["a0"])
    g2p = jnp.concatenate([wts["g2"], jnp.zeros((GATE_PAD - GATE_LORA, RW), F32)], axis=0)
    k_k, k_a, r_k = row(wts["k_k"]), row(wts["k_a"]), row(wts["r_k"])
    lnx_g, lnx_b = row(wts["lnx_g"]), row(wts["lnx_b"])
    cdw, cb, clg, clb = wts["conv_dw"], row(wts["conv_b"]), row(wts["conv_ln_g"]), row(wts["conv_ln_b"])
    small = (mu_p, mu_n, w2b, w0c, a2b, a0c, g2p, k_k, k_a)
    seq3 = lambda a: a.reshape(bsz, seq, a.shape[-1])
    flat = lambda a: a.reshape(n_tok, a.shape[-1])

    p = _matmul(x1b, win, name="proj_in")
    r, v, kk, w, kd, b, g = _mix_prep(p, *small, seq=seq, tt=tt, name="mix_prep")
    scan_in = [seq3(a) for a in (r, w, kd, v, kk, b)]
    y0, s_chunks0 = _wkv_chunk_fwd(*scan_in, rev=False, name="wkv_fwd_dir0")
    y1, s_chunks1 = _wkv_chunk_fwd(*scan_in, rev=True, name="wkv_fwd_dir1")
    y0, y1 = flat(y0), flat(y1)
    yr = _mix_post(y0, y1, r, v, kd, g, lnx_g, lnx_b, r_k, tt=tt, name="mix_post")
    yc, yv = _conv_fwd(p, cdw, cb, clg, clb, seq=seq, tt=tt, name="conv_fwd")
    if more_weights is not None:
        wts = {**wts, **more_weights("out", yr)}
    wout, w2i, w2o = wts["w_out"], wts["ffn2_w_in"], wts["ffn2_w_out"]
    z2, x2, x2b = _mm_ln([yr, yv], wout, x1, ln["ln2_g"], ln["ln2_b"], 1.0, tm=TM_LN, name="proj_out_ln2")
    h2, act2 = _ffn_in(x2b, w2i, tm=TM_FFN, name="ffn2_in")

    gr = {}
    slab_rows = lambda a: a.reshape((N_CHIPS, a.shape[0] // N_CHIPS) + a.shape[1:])
    dw_kw = dict(ta=True, out_dtype=BF16)
    dz3, gr["ln3_g"], gr["ln3_b"], loss_part = _mm_ln_loss(act2, w2o, x2, ln["ln3_g"], ln["ln3_b"], tgt, 0.5, tm=TM_LN,
                                                           name="ffn2_out_ln3_loss")
    dh2 = _ffn_out_bwd(dz3, w2o, h2, tm=TM_FFN, name="ffn2_out_dx")
    gr["ffn2_w_out"] = slab_rows(_matmul(act2, dz3, scale=0.5, tm=D_FF // 2, name="ffn2_out_dw", **dw_kw))
    dz2, gr["ln2_g"], gr["ln2_b"] = _mm_nt_res([dh2], w2i, dz3, ln=(z2, ln["ln2_g"], ln["ln2_b"]), tm=TM_FFN,
                                               name="ffn2_in_dx_ln2")
    gr["ffn2_w_in"] = _matmul(x2b, dh2, col_slabs=True, tn=2 * D_FF // N_CHIPS, name="ffn2_in_dw", **dw_kw)
    dmix = _matmul(dz2, wout, tb=True, name="proj_out_dx")
    gr["w_out"] = slab_rows(jnp.concatenate([_matmul(yr, dz2, name="proj_out_dw_rwkv", **dw_kw),
                                             _matmul(yv, dz2, name="proj_out_dw_conv", **dw_kw)], axis=0))
    tok = grads_ready(("ffn2_w_out", "ffn2_w_in", "w_out"), [gr["ffn2_w_out"], gr["ffn2_w_in"], gr["w_out"]])
    dyr, dyv = (dmix, RW, 0), (dmix, RW, 1)
    dy, dr_p, dv_p, dkd_p, dg, gr["lnx_g"], gr["lnx_b"], gr["r_k"] = _mix_post_bwd(
        y0, y1, r, v, kd, g, _follow(lnx_g, tok), lnx_b, r_k, dyr, tt=tt, name="mix_post_bwd")
    dr0, dw0, dkd0, dv0, dk0, db0 = [flat(a) for a in _wkv_chunk_bwd(*scan_in, seq3(dy), s_chunks0, rev=False,
                                                                      name="wkv_bwd_dir0")]
    dr1, dw1, dkd1, dv1, dk1, db1 = [flat(a) for a in _wkv_chunk_bwd(*scan_in, seq3(dy), s_chunks1, rev=True,
                                                                      name="wkv_bwd_dir1")]
    ct_terms = [[dr_p, dr0, dr1], [dv_p, dv0, dv1], [dk0, dk1], [(dw0, dw1)], [dkd_p, (dkd0, dkd1)], [(db0, db1)], [dg]]
    dyc, gr["conv_ln_g"], gr["conv_ln_b"], gr["conv_b"] = _conv_post_bwd(yc, dyv, clg, clb, tt=tt, name="conv_post_bwd")
    dpc, ddw = _conv_bwd(dyc, p, cdw, seq=seq, tt=tt, name="conv_bwd")
    gr["conv_dw"] = ddw[:CONV_K]
    dps, dw2b, dw0c, da2b, da0c, dg2p, gr["k_k"], gr["k_a"] = _mix_prep_bwd(
        p, *small, ct_terms, seq=seq, tt=tt, name="mix_prep_bwd")
    gr["w2"] = jnp.stack([dw2b[:LORA, :RW], dw2b[LORA:, RW:]])
    gr["a2"] = jnp.stack([da2b[:LORA, :RW], da2b[LORA:, RW:]])
    gr["w0"], gr["a0"], gr["g2"] = dw0c.reshape(2, RW), da0c.reshape(2, RW), dg2p[:GATE_LORA]
    dpsh, dmu_p, dmu_n = _shift_bwd(dps, p, mu_p, mu_n, seq=seq, tt=tt, name="shift_bwd")
    gr["mu_prev"], gr["mu_next"] = dmu_p[:, :SHIFT_COLS], dmu_n[:, :SHIFT_COLS]
    dwin = jnp.concatenate([_matmul(x1b, dpsh, name="proj_in_dw_shift", **dw_kw)[:, :SHIFT_COLS],
                            _matmul(x1b, dpc, name="proj_in_dw_conv", **dw_kw)], axis=1)
    gr["w_in"] = jnp.moveaxis(dwin.reshape(D_MODEL, N_CHIPS, IN_COLS // N_CHIPS), 1, 0)
    tok = grads_ready(("w_in",), [gr["w_in"]])
    dz1, gr["ln1_g"], gr["ln1_b"] = _mm_nt_res([dpsh, dpc], win, dz2, ln=(z1, ln["ln1_g"], ln["ln1_b"]), tm=TM_FFN,
                                               after=tok, name="proj_in_dx_ln1")
    gr["loss"] = loss_part
    tok = small_ready(gr, loss_part) if small_ready is not None else None
    dh1 = _ffn_out_bwd(dz1, w1o, h1, tm=TM_FFN, after=tok, name="ffn1_out_dx")
    gr["ffn1_w_out"] = slab_rows(_matmul(act1, dz1, scale=0.5, tm=D_FF // 2, name="ffn1_out_dw", **dw_kw))
    tok = grads_ready(("ffn1_w_out",), [gr["ffn1_w_out"]])
    gr["ffn1_w_in"] = _matmul(x0, dh1, col_slabs=True, tn=2 * D_FF // N_CHIPS, after=tok, name="ffn1_in_dw", **dw_kw)
    tok = grads_ready(("ffn1_w_in",), [gr["ffn1_w_in"]])
    dx0 = _mm_nt_res([dh1], w1i, dz1, tm=TM_FFN, after=tok, name="ffn1_in_dx")
    return dx0.reshape(bsz, seq, D_MODEL), gr


def _mesh_pos():
    return lax.axis_index("x"), lax.axis_index("y"), lax.axis_index("c")


def _other_chips(x, y):
    return [(1 - x, y), (x, 1 - y), (1 - x, 1 - y)]


def _gather_chips(shards, *, name):
    n = len(shards)
    halves = [s.shape[0] // 2 for s in shards]
    assert all(2 * h == s.shape[0] for h, s in zip(halves, shards))

    def body(*refs):
        ins, outs = refs[:n], refs[n:2 * n]
        send_sems, recv_sems, fwd_send_sems, fwd_recv_sems, loc_sems = refs[2 * n:]
        x, y, c = _mesh_pos()
        q = 2 * x + y
        peers = _other_chips(x, y)
        local = [pltpu.make_async_copy(ins[a], outs[a].at[q], loc_sems.at[a]) for a in range(n)]
        for cp in local:
            cp.start()

        def half(a, chip, core):
            return outs[a].at[chip, pl.ds(core * halves[a], halves[a])]

        sends = [pltpu.make_async_remote_copy(ins[a].at[pl.ds(c * halves[a], halves[a])], half(a, q, c),
                                              send_sems.at[a, k], recv_sems.at[a, k],
                                              device_id=(px, py, c), device_id_type=MESH)
                 for a in range(n) for k, (px, py) in enumerate(peers)]
        for cp in sends:
            cp.start()
        passed = []
        for a in range(n):
            for k, (px, py) in enumerate(peers):
                mine = half(a, 2 * px + py, c)
                pltpu.make_async_remote_copy(mine, mine, send_sems.at[a, k], recv_sems.at[a, k],
                                             device_id=(px, py, c), device_id_type=MESH).wait_recv()
                cp = pltpu.make_async_remote_copy(mine, mine, fwd_send_sems.at[a, k], fwd_recv_sems.at[a, k],
                                                  device_id=(x, y, 1 - c), device_id_type=MESH)
                cp.start()
                passed.append(cp)
        for a in range(n):
            for k, (px, py) in enumerate(peers):
                theirs = half(a, 2 * px + py, 1 - c)
                pltpu.make_async_remote_copy(theirs, theirs, fwd_send_sems.at[a, k], fwd_recv_sems.at[a, k],
                                             device_id=(x, y, 1 - c), device_id_type=MESH).wait_recv()
        for cp in sends + passed:
            cp.wait_send()
        for cp in local:
            cp.wait()

    any_spec = pl.BlockSpec(memory_space=pl.ANY)
    return pl.pallas_call(
        body, name=name,
        out_shape=[jax.ShapeDtypeStruct((N_CHIPS,) + s.shape, s.dtype) for s in shards],
        in_specs=[any_spec] * n, out_specs=[any_spec] * n,
        scratch_shapes=[pltpu.SemaphoreType.DMA((n, 3))] * 4 + [pltpu.SemaphoreType.DMA((n,))],
        compiler_params=pltpu.CompilerParams(has_side_effects=True),
    )(*shards)


HBM_SPEC = pl.BlockSpec(memory_space=pltpu.HBM)
SEM_SPEC = pl.BlockSpec(memory_space=pltpu.SEMAPHORE)
ANY_SPEC = pl.BlockSpec(memory_space=pl.ANY)
SIDE_EFFECT = pltpu.SideEffectType.DATAFLOW_SIDE_EFFECTING


def _chip_copies(src_refs, land_refs, send_sems, recv_sems, scatter, arriving=False):
    x, y, c = _mesh_pos()
    cps = []
    for a, (src, land) in enumerate(zip(src_refs, land_refs)):
        for k, (px, py) in enumerate(_other_chips(x, y)):
            slot = k if scatter else (2 * px + py if arriving else 2 * x + y)
            cps.append(pltpu.make_async_remote_copy(src.at[2 * px + py] if scatter else src, land.at[slot],
                                                    send_sems.at[3 * a + k], recv_sems.at[3 * a + k],
                                                    device_id=(px, py, c), device_id_type=MESH))
    return cps


def _exchange_start(srcs, *, scatter, after, name):
    n = len(srcs)
    lands = [lax.empty((3,) + s.shape[1:] if scatter else (N_CHIPS,) + s.shape, s.dtype) for s in srcs]

    def body(*refs):
        src_refs, land_refs = refs[:n], refs[n:2 * n]
        send_sems, recv_sems = refs[2 * n + 1:2 * n + 3]
        token = refs[-1]
        for cp in _chip_copies(src_refs, land_refs, send_sems, recv_sems, scatter):
            cp.start()
        token[...] = jnp.zeros_like(token)

    hbm = lambda a: pltpu.with_memory_space_constraint(a, pltpu.HBM)
    outs = pl.pallas_call(
        body, name=name,
        out_shape=(pltpu.SemaphoreType.DMA((3 * n,)), pltpu.SemaphoreType.DMA((3 * n,)),
                   *[pltpu.HBM(a.shape, a.dtype) for a in srcs + lands], jax.ShapeDtypeStruct((8, LANES), F32)),
        in_specs=[HBM_SPEC] * (2 * n) + [ANY_SPEC],
        out_specs=(SEM_SPEC, SEM_SPEC, *[HBM_SPEC] * (2 * n), pl.BlockSpec(memory_space=pltpu.VMEM)),
        input_output_aliases={i: 2 + i for i in range(2 * n)},
        compiler_params=pltpu.CompilerParams(has_side_effects=SIDE_EFFECT),
    )(*[hbm(a) for a in srcs + lands], after)
    return outs[0], outs[1], list(outs[2:2 + n]), list(outs[2 + n:2 + 2 * n]), outs[-1]


def _exchange_wait(started, *, scatter, after, name):
    send_sems, recv_sems, srcs, lands, _ = started
    n = len(srcs)

    def body(*refs):
        src_refs, land_refs = refs[:n], refs[n:2 * n]
        send_s, recv_s = refs[2 * n:2 * n + 2]
        for cp in _chip_copies(src_refs, land_refs, send_s, recv_s, scatter, arriving=True):
            cp.wait_send()
            cp.wait_recv()

    outs = pl.pallas_call(
        body, name=name,
        out_shape=tuple(pltpu.HBM(a.shape, a.dtype) for a in srcs + lands),
        in_specs=[HBM_SPEC] * (2 * n) + [SEM_SPEC, SEM_SPEC, ANY_SPEC],
        out_specs=tuple([HBM_SPEC] * (2 * n)),
        input_output_aliases={i: i for i in range(2 * n)},
        compiler_params=pltpu.CompilerParams(has_side_effects=SIDE_EFFECT),
    )(*srcs, *lands, send_sems, recv_sems, after)
    return list(outs[:n]), list(outs[n:])


def _by_chip(own, land):
    xi, yi, _ = _mesh_pos()
    return lax.dynamic_update_index_in_dim(land, own, 2 * xi + yi, 0)


def _swap_sibling(arrs, *, name):
    n = len(arrs)

    def body(*refs):
        ins, outs = refs[:n], refs[n:2 * n]
        send_sems, recv_sems = refs[2 * n:]
        x, y, c = _mesh_pos()
        cps = [pltpu.make_async_remote_copy(ins[a], outs[a], send_sems.at[a], recv_sems.at[a],
                                            device_id=(x, y, 1 - c), device_id_type=MESH) for a in range(n)]
        for cp in cps:
            cp.start()
        for cp in cps:
            cp.wait_recv()
        for cp in cps:
            cp.wait_send()

    any_spec = pl.BlockSpec(memory_space=pl.ANY)
    return pl.pallas_call(
        body, name=name,
        out_shape=[jax.ShapeDtypeStruct(s.shape, s.dtype) for s in arrs],
        in_specs=[any_spec] * n, out_specs=[any_spec] * n,
        scratch_shapes=[pltpu.SemaphoreType.DMA((n,)), pltpu.SemaphoreType.DMA((n,))],
        compiler_params=pltpu.CompilerParams(has_side_effects=True),
    )(*arrs)


def _device_copies(v_ref, land_ref, send_sems, recv_sems, arriving=False):
    x, y, c = _mesh_pos()
    me = 4 * x + 2 * y + c
    cps = []
    for m in range(1, 8):
        px, py, pc = (x + ((m >> 2) & 1)) % 2, (y + ((m >> 1) & 1)) % 2, (c + (m & 1)) % 2
        slot = 4 * px + 2 * py + pc if arriving else me
        cps.append(pltpu.make_async_remote_copy(v_ref, land_ref.at[slot], send_sems.at[m - 1], recv_sems.at[m - 1],
                                                device_id=(px, py, pc), device_id_type=MESH))
    return cps


def _allsum_start(vec, *, after, name):
    land = lax.empty((8,) + vec.shape, F32)

    def body(v_ref, land_ref, _after, send_sems, recv_sems, v_thru, land_thru, token):
        for cp in _device_copies(v_ref, land_ref, send_sems, recv_sems):
            cp.start()
        token[...] = jnp.zeros_like(token)

    hbm = lambda a: pltpu.with_memory_space_constraint(a, pltpu.HBM)
    return pl.pallas_call(
        body, name=name,
        out_shape=(pltpu.SemaphoreType.DMA((7,)), pltpu.SemaphoreType.DMA((7,)), pltpu.HBM(vec.shape, F32),
                   pltpu.HBM(land.shape, F32), jax.ShapeDtypeStruct((8, LANES), F32)),
        in_specs=[HBM_SPEC, HBM_SPEC, ANY_SPEC],
        out_specs=(SEM_SPEC, SEM_SPEC, HBM_SPEC, HBM_SPEC, pl.BlockSpec(memory_space=pltpu.VMEM)),
        input_output_aliases={0: 2, 1: 3},
        compiler_params=pltpu.CompilerParams(has_side_effects=SIDE_EFFECT),
    )(hbm(vec), hbm(land), after)


def _allsum_wait(started, *, after, name):
    send_sems, recv_sems, vec, land, _ = started

    def body(v_ref, land_ref, send_s, recv_s, _after, v_dead, got):
        for cp in _device_copies(v_ref, land_ref, send_s, recv_s, arriving=True):
            cp.wait_send()
            cp.wait_recv()

    vec, land = pl.pallas_call(
        body, name=name,
        out_shape=(pltpu.HBM(vec.shape, F32), pltpu.HBM(land.shape, F32)),
        in_specs=[HBM_SPEC, HBM_SPEC, SEM_SPEC, SEM_SPEC, ANY_SPEC],
        out_specs=(HBM_SPEC, HBM_SPEC),
        input_output_aliases={0: 0, 1: 1},
        compiler_params=pltpu.CompilerParams(has_side_effects=SIDE_EFFECT),
    )(vec, land, send_sems, recv_sems, after)
    xi, yi, ci = _mesh_pos()
    every = lax.dynamic_update_index_in_dim(land, vec, 4 * xi + 2 * yi + ci, 0)

    def add(e_ref, o_ref):
        acc = e_ref[0]
        for d in range(1, 8):
            acc = acc + e_ref[d]
        o_ref[...] = acc

    vm = pl.BlockSpec(memory_space=pltpu.VMEM)
    return pl.pallas_call(add, name=name + "_sum", out_shape=jax.ShapeDtypeStruct(vec.shape, F32), in_specs=[vm],
                          out_specs=vm, compiler_params=_cparams())(every)


def _adamw(w, g, m, v):
    m = ADAM_B1 * m + (1.0 - ADAM_B1) * g
    v = ADAM_B2 * v + (1.0 - ADAM_B2) * (g * g)
    m_hat = m / (1.0 - ADAM_B1 ** ADAM_STEP)
    v_hat = v / (1.0 - ADAM_B2 ** ADAM_STEP)
    delta = -ADAM_LR * (m_hat / (jnp.sqrt(v_hat) + ADAM_EPS) + ADAM_WD * w)
    return delta, m, v


def _sum4(mine, land, *, name):
    rows, cols = mine.shape
    tr = _pick_rows(rows)

    def body(a_ref, l_ref, o_ref):
        o_ref[...] = (a_ref[...].astype(F32) + l_ref[0].astype(F32)) + (l_ref[1].astype(F32) + l_ref[2].astype(F32))

    return pl.pallas_call(
        body, name=name, out_shape=jax.ShapeDtypeStruct((rows, cols), F32), grid=(rows // tr,),
        in_specs=[pl.BlockSpec((tr, cols), lambda i: (i, 0)), pl.BlockSpec((3, tr, cols), lambda i: (0, i, 0))],
        out_specs=pl.BlockSpec((tr, cols), lambda i: (i, 0)),
        compiler_params=_cparams(("parallel",)),
    )(mine, land)


def _pick_rows(rows, want=256):
    for t in range(min(want, rows) // 8 * 8, 0, -8):
        if rows % t == 0:
            return t
    return rows


def _sum_adam(h_mine, h_sib, w, m, v, *, name):
    rows, cols = w.shape
    tr = _pick_rows(rows)

    def body(a_ref, b_ref, w_ref, m_ref, v_ref, g_o, d_o, m_o, v_o):
        g = a_ref[...] + b_ref[...]
        d, mn, vn = _adamw(w_ref[...], g, m_ref[...], v_ref[...])
        g_o[...], d_o[...], m_o[...], v_o[...] = g, d, mn, vn

    spec = pl.BlockSpec((tr, cols), lambda i: (i, 0))
    return pl.pallas_call(
        body, name=name, out_shape=[jax.ShapeDtypeStruct((rows, cols), F32)] * 4, grid=(rows // tr,),
        in_specs=[spec] * 5, out_specs=[spec] * 4, compiler_params=_cparams(("parallel",)),
    )(h_mine, h_sib, w, m, v)


def _adam_rows(w, g, m, v, *, name):
    def body(w_ref, g_ref, m_ref, v_ref, d_o, m_o, v_o):
        d_o[...], m_o[...], v_o[...] = _adamw(w_ref[...], g_ref[...], m_ref[...], v_ref[...])

    vm = pl.BlockSpec(memory_space=pltpu.VMEM)
    return pl.pallas_call(
        body, name=name, out_shape=[jax.ShapeDtypeStruct(w.shape, F32)] * 3,
        in_specs=[vm] * 4, out_specs=[vm] * 3, compiler_params=_cparams(),
    )(w, g, m, v)


def _size(shape):
    size = 1
    for d in shape:
        size *= d
    return size


def _pack_rows(arrs):
    blocks = []
    for a in arrs:
        flat = a.reshape(-1).astype(F32)
        flat = jnp.concatenate([flat, jnp.zeros((-flat.shape[0] % (8 * LANES),), F32)])
        blocks.append(flat.reshape(-1, LANES))
    return jnp.concatenate(blocks, axis=0)


def _unpack_rows(packed, shapes):
    out, row = [], 0
    for s in shapes:
        rows = -(-_size(s) // (8 * LANES)) * 8
        out.append(packed[row:row + rows].reshape(-1)[:_size(s)].reshape(s))
        row += rows
    return out


WEIGHTS = ['ffn1_w_in', 'ffn1_w_out', 'w_in', 'mu_prev', 'mu_next', 'w0', 'w2', 'a0', 'a2', 'g2', 'k_k', 'k_a', 'r_k',
           'lnx_g', 'lnx_b', 'conv_dw', 'conv_b', 'conv_ln_g', 'conv_ln_b', 'w_out', 'ffn2_w_in', 'ffn2_w_out',
           'ln1_g', 'ln1_b', 'ln2_g', 'ln2_b', 'ln3_g', 'ln3_b']
COL_SHARDED = ('ffn1_w_in', 'w_in', 'ffn2_w_in')
ROW_SHARDED = ('ffn1_w_out', 'w_out', 'ffn2_w_out')
BIG = COL_SHARDED + ROW_SHARDED
SMALL_SHARDED = ('w0', 'w2', 'a0', 'a2', 'g2', 'conv_dw')
REPLICATED = tuple(n for n in WEIGHTS if n not in BIG + SMALL_SHARDED)


def _train_step(x, target, w, m, v, *, tt):
    xi, yi, _ = _mesh_pos()
    q = 2 * xi + yi

    later = {"ffn1_out": ("ffn1_w_out",), "mix": ("w_in",) + SMALL_SHARDED, "out": ("w_out", "ffn2_w_in", "ffn2_w_out")}
    shard = lambda n: w[n][0].astype(BF16) if n in BIG else w[n][0]
    small_names = REPLICATED + SMALL_SHARDED

    def whole(n, slabs):
        if n in ROW_SHARDED:
            return slabs.reshape((-1,) + slabs.shape[2:])
        if n in ("ffn1_w_in", "ffn2_w_in"):
            return slabs
        return jnp.moveaxis(slabs, 0, -2).reshape(slabs.shape[1:-1] + (N_CHIPS * slabs.shape[-1],))

    full = {n: w[n][0] for n in REPLICATED}
    first = _gather_chips([shard("ffn1_w_in")], name="gather_ffn1_in")
    full["ffn1_w_in"] = whole("ffn1_w_in", first[0])
    started, token = {}, first[0]
    for stage, names in later.items():
        started[stage] = _exchange_start([shard(n) for n in names], scatter=False, after=token,
                                         name="gather_%s_start" % stage)
        token = started[stage][-1]

    def more_weights(stage, after):
        own, land = _exchange_wait(started[stage], scatter=False, after=after, name="gather_%s_wait" % stage)
        got = {n: whole(n, _by_chip(o, l)) for n, o, l in zip(later[stage], own, land)}
        full.update(got)
        return got

    small_sent = []

    def small_ready(gr, loss_part):
        vec = _pack_rows([gr[n] for n in small_names] + [loss_part[0:1, 0:1]])
        small_sent.append(_allsum_start(vec, after=vec, name="reduce_small_start"))
        return small_sent[0][-1]

    sent = []

    def grads_ready(names, slabs):
        started = _exchange_start(slabs, scatter=True, after=slabs[0], name="scatter_%s_start" % names[0])
        sent.append((names, started))
        return started[-1]

    grad_x, gr = _local_step(x, target, full, tt=tt, start_token=token, more_weights=more_weights,
                             grads_ready=grads_ready, small_ready=small_ready)

    halves = {}
    for names, started in sent:
        stacks, landed = _exchange_wait(started, scatter=True, after=grad_x, name="scatter_%s_wait" % names[0])
        for n, s, l in zip(names, stacks, landed):
            halves[n] = _sum4(lax.dynamic_index_in_dim(s, q, 0, keepdims=False), l, name="sum4_" + n)
    halves = [halves[n] for n in BIG]
    sib = _swap_sibling(halves, name="swap_halves")
    grad, delta, new_m, new_v = {}, {}, {}, {}
    for n, h, hs in zip(BIG, halves, sib):
        outs = _sum_adam(h, hs, w[n][0], m[n][0], v[n][0], name="adam_" + n)
        grad[n], delta[n], new_m[n], new_v[n] = [o[None] for o in outs]

    small_full_shapes = [full[n].shape for n in small_names]
    red = _allsum_wait(small_sent[0], after=grad_x, name="reduce_small_wait")
    *red, loss = _unpack_rows(red, small_full_shapes + [()])
    red = dict(zip(small_names, red))
    gsm = {}
    for n in REPLICATED:
        gsm[n] = red[n].reshape(w[n].shape)
    for n in SMALL_SHARDED:
        width = w[n].shape[-1]
        gsm[n] = lax.dynamic_slice_in_dim(red[n], q * width, width, axis=red[n].ndim - 1).reshape(w[n].shape)
    shapes = [w[n].shape for n in small_names]
    d_p, m_p, v_p = _adam_rows(_pack_rows([w[n] for n in small_names]), _pack_rows([gsm[n] for n in small_names]),
                               _pack_rows([m[n] for n in small_names]), _pack_rows([v[n] for n in small_names]),
                               name="adam_small")
    for n, dd, mm, vv in zip(small_names, _unpack_rows(d_p, shapes), _unpack_rows(m_p, shapes), _unpack_rows(v_p, shapes)):
        grad[n], delta[n], new_m[n], new_v[n] = gsm[n], dd, mm, vv
    return loss, grad_x, grad, delta, new_m, new_v


def kernel(x, ffn1_w_in, ffn1_w_out, w_in, mu_prev, mu_next, w0, w2, a0, a2, g2, k_k, k_a, r_k, lnx_g, lnx_b, conv_dw, conv_b, conv_ln_g, conv_ln_b, w_out, ffn2_w_in, ffn2_w_out, ln1_g, ln1_b, ln2_g, ln2_b, ln3_g, ln3_b, loss_target, m_ffn1_w_in, m_ffn1_w_out, m_w_in, m_mu_prev, m_mu_next, m_w0, m_w2, m_a0, m_a2, m_g2, m_k_k, m_k_a, m_r_k, m_lnx_g, m_lnx_b, m_conv_dw, m_conv_b, m_conv_ln_g, m_conv_ln_b, m_w_out, m_ffn2_w_in, m_ffn2_w_out, m_ln1_g, m_ln1_b, m_ln2_g, m_ln2_b, m_ln3_g, m_ln3_b, v_ffn1_w_in, v_ffn1_w_out, v_w_in, v_mu_prev, v_mu_next, v_w0, v_w2, v_a0, v_a2, v_g2, v_k_k, v_k_a, v_r_k, v_lnx_g, v_lnx_b, v_conv_dw, v_conv_b, v_conv_ln_g, v_conv_ln_b, v_w_out, v_ffn2_w_in, v_ffn2_w_out, v_ln1_g, v_ln1_b, v_ln2_g, v_ln2_b, v_ln3_g, v_ln3_b):
    args = dict(locals())
    w = {n: args[n] for n in WEIGHTS}
    m = {n: args["m_" + n] for n in WEIGHTS}
    v = {n: args["v_" + n] for n in WEIGHTS}
    seq = x.shape[1]
    loss, grad_x, grad, delta, new_m, new_v = _train_step(x, loss_target, w, m, v, tt=min(256, seq))
    return (loss, grad_x, *[grad[n] for n in WEIGHTS], *[delta[n] for n in WEIGHTS],
            *[new_m[n] for n in WEIGHTS], *[new_v[n] for n in WEIGHTS])
```

```python
import functools

import jax
import jax.numpy as jnp
from jax import lax
from jax.experimental import pallas as pl
from jax.experimental.pallas import tpu as pltpu

F32 = jnp.float32
BF16 = jnp.bfloat16

D_MODEL = 1024
RW = 512
HEAD = 64
CW = 512
CONV_K = 31
CONV_ROWS = 32
SHIFT_ROWS = 16
CONV_PAD = 15
D_FF = 2816
LORA = 64
GATE_LORA = 160
GATE_PAD = 256
SHIFT_COLS = 1952
SHIFT_PAD = 2048
IN_COLS = 2976
IN_PAD = 3072
LN_EPS = 1e-5
GN_EPS = 64e-5
NORM_EPS = 1e-12
ALPHA = 2.0 ** 0.25
DECAY_SCALE = 0.6065306597126334
ADAM_LR, ADAM_B1, ADAM_B2, ADAM_EPS, ADAM_WD, ADAM_STEP = 0.001, 0.9, 0.999, 1e-08, 0.01, 10
N_CHIPS = 4
VMEM_LIMIT = 56 * 1024 * 1024
TM_FFN = 256
TM_LN = 512

MESH = pl.DeviceIdType.MESH


def _cparams(sem=None, **kw):
    return pltpu.CompilerParams(dimension_semantics=sem, vmem_limit_bytes=VMEM_LIMIT, **kw)


LANES = 128


def _pick_tile(dim, want):
    for t in range(min(want, dim) // LANES * LANES, 0, -LANES):
        if dim % t == 0:
            return t
    return dim


def _after_operand(after):
    return ([], []) if after is None else ([pl.BlockSpec(memory_space=pl.ANY)], [after])


def _matmul(a, b, *, ta=False, tb=False, out_dtype=F32, tm=1024, tn=1024, tk=1024, scale=1.0, col_slabs=False,
            after=None, name):
    after_specs, after_args = _after_operand(after)
    if ta:
        k_dim, m_dim = a.shape
    else:
        m_dim, k_dim = a.shape
    n_dim = b.shape[0] if tb else b.shape[1]
    tm, tn, tk = _pick_tile(m_dim, tm), _pick_tile(n_dim, tn), _pick_tile(k_dim, tk)
    assert m_dim % tm == 0 and n_dim % tn == 0 and k_dim % tk == 0, (name, a.shape, b.shape, tm, tn, tk)
    nk = k_dim // tk
    dims = (((0,) if ta else (1,), (1,) if tb else (0,)), ((), ()))
    if col_slabs:
        out_shape = jax.ShapeDtypeStruct((n_dim // tn, m_dim, tn), out_dtype)
        out_spec = pl.BlockSpec((None, tm, tn), lambda i, j, k: (j, i, 0))
    else:
        out_shape = jax.ShapeDtypeStruct((m_dim, n_dim), out_dtype)
        out_spec = pl.BlockSpec((tm, tn), lambda i, j, k: (i, j))

    def body(a_ref, b_ref, *rest):
        o_ref, acc_ref = rest[-2:]
        kk = pl.program_id(2)

        @pl.when(kk == 0)
        def _():
            acc_ref[...] = jnp.zeros_like(acc_ref)

        acc_ref[...] += lax.dot_general(a_ref[...].astype(BF16), b_ref[...].astype(BF16), dims,
                                        preferred_element_type=F32)

        @pl.when(kk == nk - 1)
        def _():
            o_ref[...] = (acc_ref[...] * scale).astype(o_ref.dtype)

    a_spec = pl.BlockSpec((tk, tm), lambda i, j, k: (k, i)) if ta else pl.BlockSpec((tm, tk), lambda i, j, k: (i, k))
    b_spec = pl.BlockSpec((tn, tk), lambda i, j, k: (j, k)) if tb else pl.BlockSpec((tk, tn), lambda i, j, k: (k, j))
    return pl.pallas_call(
        body, name=name,
        out_shape=out_shape,
        grid=(m_dim // tm, n_dim // tn, nk),
        in_specs=[a_spec, b_spec] + after_specs,
        out_specs=out_spec,
        scratch_shapes=[pltpu.VMEM((tm, tn), F32)],
        compiler_params=_cparams(("parallel", "parallel", "arbitrary")),
    )(a, b, *after_args)


def _whole(shape):
    nd = len(shape)
    return pl.BlockSpec(shape, lambda i: (0,) * nd)


def _ffn_in(x, w, *, tm, after=None, name):
    n_tok = x.shape[0]
    sw = w.shape[2]
    tm = min(tm, n_tok)

    after_specs, after_args = _after_operand(after)

    def body(x_ref, w_ref, *rest):
        h_ref, a_ref = rest[-2:]
        xb = x_ref[...].astype(BF16)
        for s in range(2):
            g = jnp.dot(xb, w_ref[s], preferred_element_type=F32)
            u = jnp.dot(xb, w_ref[s + 2], preferred_element_type=F32)
            h_ref[:, s * sw:(s + 1) * sw] = g.astype(BF16)
            h_ref[:, (s + 2) * sw:(s + 3) * sw] = u.astype(BF16)
            a_ref[:, s * sw:(s + 1) * sw] = (_silu(g) * u).astype(BF16)

    return pl.pallas_call(
        body, name=name,
        out_shape=[jax.ShapeDtypeStruct((n_tok, 2 * D_FF), BF16), jax.ShapeDtypeStruct((n_tok, D_FF), BF16)],
        grid=(n_tok // tm,),
        in_specs=[pl.BlockSpec((tm, D_MODEL), lambda i: (i, 0)), _whole(w.shape)] + after_specs,
        out_specs=[pl.BlockSpec((tm, 2 * D_FF), lambda i: (i, 0)), pl.BlockSpec((tm, D_FF), lambda i: (i, 0))],
        compiler_params=_cparams(("parallel",)),
    )(x, w, *after_args)


def _mm_ln(a_list, w, xres, g, b, fscale, *, tm, name):
    n_tok = xres.shape[0]
    tm = min(tm, n_tok)
    na = len(a_list)

    def body(*refs):
        a_refs = refs[:na]
        w_ref, x_ref, g_ref, b_ref, z_o, y_o, yb_o = refs[na:]
        f, off = None, 0
        for a_ref in a_refs:
            k = a_ref.shape[1]
            t = jnp.dot(a_ref[...].astype(BF16), w_ref[off:off + k, :], preferred_element_type=F32)
            f = t if f is None else f + t
            off += k
        z = ALPHA * x_ref[...] + fscale * f
        y = _layer_norm(z, g_ref[...], b_ref[...])
        z_o[...] = z
        y_o[...] = y
        yb_o[...] = y.astype(BF16)

    tile = pl.BlockSpec((tm, D_MODEL), lambda i: (i, 0))
    return pl.pallas_call(
        body, name=name,
        out_shape=[jax.ShapeDtypeStruct((n_tok, D_MODEL), F32)] * 2 + [jax.ShapeDtypeStruct((n_tok, D_MODEL), BF16)],
        grid=(n_tok // tm,),
        in_specs=[pl.BlockSpec((tm, a.shape[1]), lambda i: (i, 0)) for a in a_list]
        + [_whole(w.shape), tile, _whole(g.shape), _whole(b.shape)],
        out_specs=[tile, tile, tile],
        compiler_params=_cparams(("parallel",)),
    )(*a_list, w, xres, g, b)


def _mm_ln_loss(a, w, xres, g, b, target, fscale, *, tm, name):
    n_tok = xres.shape[0]
    tm = min(tm, n_tok)

    def body(a_ref, w_ref, x_ref, g_ref, b_ref, t_ref, dz_o, dg_o, db_o, loss_o):
        i = pl.program_id(0)
        z = ALPHA * x_ref[...] + fscale * jnp.dot(a_ref[...].astype(BF16), w_ref[...], preferred_element_type=F32)
        y, vjp = jax.vjp(_layer_norm, z, g_ref[...], b_ref[...])
        e = y - t_ref[...]
        dz, dg, db = vjp(e * (1.0 / D_MODEL))

        @pl.when(i == 0)
        def _():
            dg_o[...] = jnp.zeros_like(dg_o)
            db_o[...] = jnp.zeros_like(db_o)
            loss_o[...] = jnp.zeros_like(loss_o)
        dz_o[...] = dz
        dg_o[...] += dg
        db_o[...] += db
        loss_o[...] += 0.5 * jnp.sum(jnp.mean(e * e, axis=-1, keepdims=True), axis=0, keepdims=True)

    tile = pl.BlockSpec((tm, D_MODEL), lambda i: (i, 0))
    row = pl.BlockSpec((1, D_MODEL), lambda i: (0, 0))
    return pl.pallas_call(
        body, name=name,
        out_shape=[jax.ShapeDtypeStruct((n_tok, D_MODEL), F32), jax.ShapeDtypeStruct((1, D_MODEL), F32),
                   jax.ShapeDtypeStruct((1, D_MODEL), F32), jax.ShapeDtypeStruct((8, LANES), F32)],
        grid=(n_tok // tm,),
        in_specs=[pl.BlockSpec((tm, a.shape[1]), lambda i: (i, 0)), _whole(w.shape), tile, row, row, tile],
        out_specs=[tile, row, row, pl.BlockSpec((8, LANES), lambda i: (0, 0))],
        compiler_params=_cparams(("arbitrary",)),
    )(a, w, xres, g, b, target)


def _ffn_out_bwd(dz, w, h, *, tm, after=None, name):
    n_tok = dz.shape[0]
    tm = min(tm, n_tok)
    cw = D_FF // 2
    after_specs, after_args = _after_operand(after)

    def body(dz_ref, w_ref, h_ref, *rest):
        dh_ref = rest[-1]
        dzb = dz_ref[...].astype(BF16)
        for s in range(2):
            dact = 0.5 * lax.dot_general(dzb, w_ref[s * cw:(s + 1) * cw, :], (((1,), (1,)), ((), ())),
                                         preferred_element_type=F32)
            gate = h_ref[:, s * cw:(s + 1) * cw].astype(F32)
            up = h_ref[:, D_FF + s * cw:D_FF + (s + 1) * cw].astype(F32)
            sg = _sigmoid(gate)
            dh_ref[:, s * cw:(s + 1) * cw] = (dact * up * sg * (1.0 + gate * (1.0 - sg))).astype(BF16)
            dh_ref[:, D_FF + s * cw:D_FF + (s + 1) * cw] = (dact * gate * sg).astype(BF16)

    wide = pl.BlockSpec((tm, 2 * D_FF), lambda i: (i, 0))
    return pl.pallas_call(
        body, name=name,
        out_shape=jax.ShapeDtypeStruct((n_tok, 2 * D_FF), BF16),
        grid=(n_tok // tm,),
        in_specs=[pl.BlockSpec((tm, D_MODEL), lambda i: (i, 0)), _whole(w.shape), wide] + after_specs,
        out_specs=wide,
        compiler_params=_cparams(("parallel",)),
    )(dz, w, h, *after_args)


def _mm_nt_res(a_list, w, dz, *, tm, ln=None, after=None, name):
    n_tok = dz.shape[0]
    tm = min(tm, n_tok)
    na = len(a_list)
    nt = (((1,), (1,)), ((), ()))
    after_specs, after_args = _after_operand(after)
    n_out = 1 if ln is None else 3

    def body(*refs):
        a_refs = refs[:na]
        w_ref, dz_ref, o_ref = refs[na], refs[na + 1], refs[-n_out]
        acc = ALPHA * dz_ref[...]
        if len(w_ref.shape) == 3:
            cw = w_ref.shape[2]
            for s in range(w_ref.shape[0]):
                acc = acc + lax.dot_general(a_refs[0][:, s * cw:(s + 1) * cw], w_ref[s], nt, preferred_element_type=F32)
        else:
            off = 0
            for a_ref in a_refs:
                k = a_ref.shape[1]
                acc = acc + lax.dot_general(a_ref[...], w_ref[:, off:off + k], nt, preferred_element_type=F32)
                off += k
        if ln is None:
            o_ref[...] = acc
            return
        z_ref, g_ref, b_ref = refs[na + 2:na + 5]
        dg_o, db_o = refs[-2:]
        _, vjp = jax.vjp(_layer_norm, z_ref[...], g_ref[...], b_ref[...])
        o_ref[...], dg, db = vjp(acc)

        @pl.when(pl.program_id(0) == 0)
        def _():
            dg_o[...] = jnp.zeros_like(dg_o)
            db_o[...] = jnp.zeros_like(db_o)
        dg_o[...] += dg
        db_o[...] += db

    tile = pl.BlockSpec((tm, D_MODEL), lambda i: (i, 0))
    row = pl.BlockSpec((1, D_MODEL), lambda i: (0, 0))
    out_shape = [jax.ShapeDtypeStruct((n_tok, D_MODEL), F32)]
    ln_specs, ln_args, out_specs = [], [], [tile]
    if ln is not None:
        ln_specs, ln_args = [tile, row, row], list(ln)
        out_shape += [jax.ShapeDtypeStruct((1, D_MODEL), F32)] * 2
        out_specs += [row, row]
    outs = pl.pallas_call(
        body, name=name,
        out_shape=out_shape,
        grid=(n_tok // tm,),
        in_specs=[pl.BlockSpec((tm, a.shape[1]), lambda i: (i, 0)) for a in a_list] + [_whole(w.shape), tile]
        + ln_specs + after_specs,
        out_specs=out_specs,
        compiler_params=_cparams(("parallel",) if ln is None else ("arbitrary",)),
    )(*a_list, w, dz, *ln_args, *after_args)
    return outs[0] if ln is None else outs


def _rowcall(fn, tok_in, full_in, tok_out, acc_out, *, tt, name):
    views = [a if isinstance(a, tuple) else (a, a.shape[1], 0) for a in tok_in]
    tok_in = [a for a, _, _ in views]
    n_tok = tok_in[0].shape[0]
    assert n_tok % tt == 0, (name, n_tok, tt)
    n_ti, n_fi, n_to = len(tok_in), len(full_in), len(tok_out)

    def body(*refs):
        i = pl.program_id(0)
        ins = [r[...] for r in refs[:n_ti + n_fi]]
        outs = fn(i, *ins)
        o_refs = refs[n_ti + n_fi:]
        for r, val in zip(o_refs[:n_to], outs[:n_to]):
            r[...] = val.astype(r.dtype)
        if acc_out:
            @pl.when(i == 0)
            def _():
                for r in o_refs[n_to:]:
                    r[...] = jnp.zeros_like(r)
            for r, val in zip(o_refs[n_to:], outs[n_to:]):
                r[...] += val.reshape(r.shape).astype(F32)

    in_specs = [pl.BlockSpec((tt, width), functools.partial(lambda k, i: (i, k), k)) for _, width, k in views]
    in_specs += [pl.BlockSpec(a.shape, lambda i: (0, 0)) for a in full_in]
    out_specs = [pl.BlockSpec((tt, c), lambda i: (i, 0)) for c, _ in tok_out]
    out_specs += [pl.BlockSpec(s, lambda i: (0, 0)) for s in acc_out]
    out_shape = [jax.ShapeDtypeStruct((n_tok, c), dt) for c, dt in tok_out]
    out_shape += [jax.ShapeDtypeStruct(s, F32) for s in acc_out]
    return pl.pallas_call(
        body, name=name, out_shape=out_shape, grid=(n_tok // tt,), in_specs=in_specs, out_specs=out_specs,
        compiler_params=_cparams(("arbitrary",) if acc_out else ("parallel",)),
    )(*tok_in, *full_in)


@jax.custom_vjp
def _bdot(a, b):
    return jnp.dot(a.astype(BF16), b.astype(BF16), preferred_element_type=F32)


def _bdot_fwd(a, b):
    return _bdot(a, b), (a, b)


def _bdot_bwd(res, g):
    a, b = res
    g16 = g.astype(BF16)
    da = lax.dot_general(g16, b.astype(BF16), (((1,), (1,)), ((), ())), preferred_element_type=F32)
    db = lax.dot_general(a.astype(BF16), g16, (((0,), (0,)), ((), ())), preferred_element_type=F32)
    return da, db


_bdot.defvjp(_bdot_fwd, _bdot_bwd)


def _split16(x):
    hi = x.astype(BF16)
    lo = (x - hi.astype(F32)).astype(BF16)
    return hi, lo


def _segsum_raw(x, e2):
    hi, lo = _split16(x)
    outs = []
    for c in range(x.shape[1] // 256):
        lhs = jnp.concatenate([hi[:, 256 * c:256 * (c + 1)], lo[:, 256 * c:256 * (c + 1)]], axis=1)
        outs.append(jnp.dot(lhs, e2, preferred_element_type=F32))
    return jnp.concatenate(outs, axis=1)


@jax.custom_vjp
def _segsum(x, e2):
    return _segsum_raw(x, e2)


def _segsum_fwd(x, e2):
    return _segsum_raw(x, e2), e2


def _segsum_bwd(e2, g):
    return _segsum_raw(g, e2), jnp.zeros_like(e2)


_segsum.defvjp(_segsum_fwd, _segsum_bwd)


def _head_ones():
    r = lax.broadcasted_iota(jnp.int32, (512, 256), 0) % 256
    c = lax.broadcasted_iota(jnp.int32, (512, 256), 1)
    return (r // HEAD == c // HEAD).astype(BF16)


def _sigmoid(x):
    return 1.0 / (1.0 + jnp.exp(-x))


def _silu(x):
    return x * _sigmoid(x)


def _layer_norm(z, g, b, eps=LN_EPS):
    mu = jnp.mean(z, axis=-1, keepdims=True)
    zc = z - mu
    var = jnp.mean(zc * zc, axis=-1, keepdims=True)
    return zc * lax.rsqrt(var + eps) * g + b


def _prep(ps, w2b, w0c, a2b, a0c, g2p, k_k, k_a, e2):
    r, k, v = ps[:, 0:512], ps[:, 512:1024], ps[:, 1024:1536]
    wd, ad, gd = ps[:, 1536:1664], ps[:, 1664:1792], ps[:, 1792:2048]
    lw = _bdot(jnp.tanh(wd), w2b) + w0c
    decay = -DECAY_SCALE * _sigmoid(lw)
    a = _sigmoid(_bdot(ad, a2b) + a0c)
    g = _bdot(_sigmoid(gd), g2p)
    kkr = k * k_k
    nrm = jnp.sqrt(_segsum(kkr * kkr, e2))
    kk = kkr / jnp.maximum(nrm, NORM_EPS)
    k2 = jnp.concatenate([k, k], axis=1)
    ka2 = jnp.concatenate([k_a, k_a], axis=1)
    kd = k2 * (1.0 + (a - 1.0) * ka2)
    b = jnp.concatenate([kk, kk], axis=1) * a
    return r, v, kk, decay, kd, b, g


def _post(y0, y1, r, v, kd, g, lnx_g, lnx_b, r_k, e2):
    y = y0 + y1
    mu = _segsum(y, e2) * (1.0 / HEAD)
    yc = y - mu
    var = _segsum(yc * yc, e2) * (1.0 / HEAD)
    yn = yc * lax.rsqrt(var + GN_EPS) * lnx_g + lnx_b
    bonus = _segsum(r * (kd[:, :RW] + kd[:, RW:]) * r_k, e2)
    return (yn + bonus * v) * g


def _conv_post(yc, ln_g, ln_b):
    return _silu(_layer_norm(yc, ln_g, ln_b))


def _halo_specs(cols_block, hb, tt, n_tok, col_idx):
    nb = n_tok // hb
    prev = pl.BlockSpec((hb, cols_block), lambda i: (jnp.maximum(i * (tt // hb) - 1, 0), col_idx))
    nxt = pl.BlockSpec((hb, cols_block), lambda i: (jnp.minimum((i + 1) * (tt // hb), nb - 1), col_idx))
    return prev, nxt


def _mix_prep(p, mu_p, mu_n, w2b, w0c, a2b, a0c, g2p, k_k, k_a, *, seq, tt, name):
    n_tok = p.shape[0]
    tps = seq // tt
    e2 = _head_ones()

    def body(p_ref, hp_ref, hn_ref, mup_ref, mun_ref, w2b_ref, w0c_ref, a2b_ref, a0c_ref, g2p_ref, kk_ref, ka_ref,
             e2_ref, r_o, v_o, kk_o, w_o, kd_o, b_o, g_o, ext):
        i = pl.program_id(0)
        first = (i % tps) == 0
        last = (i % tps) == tps - 1
        pv = p_ref[...]
        ext[pl.ds(0, 8), :] = jnp.where(first, 0.0, hp_ref[...])
        ext[pl.ds(8, tt), :] = pv
        ext[pl.ds(8 + tt, 8), :] = jnp.where(last, 0.0, hn_ref[...])
        prev = ext[pl.ds(7, tt), :]
        nxt = ext[pl.ds(9, tt), :]
        ps = pv + mup_ref[...] * (prev - pv) + mun_ref[...] * (nxt - pv)
        outs = _prep(ps, w2b_ref[...], w0c_ref[...], a2b_ref[...], a0c_ref[...], g2p_ref[...], kk_ref[...],
                     ka_ref[...], e2_ref[...])
        for o_ref, val in zip((r_o, v_o, kk_o, w_o, kd_o, b_o, g_o), outs):
            o_ref[...] = val

    hp, hn = _halo_specs(SHIFT_PAD, 8, tt, n_tok, 0)
    fulls = [mu_p, mu_n, w2b, w0c, a2b, a0c, g2p, k_k, k_a, e2]
    widths = (RW, RW, RW, 2 * RW, 2 * RW, 2 * RW, RW)
    return pl.pallas_call(
        body, name=name,
        out_shape=[jax.ShapeDtypeStruct((n_tok, c), F32) for c in widths],
        grid=(n_tok // tt,),
        in_specs=[pl.BlockSpec((tt, SHIFT_PAD), lambda i: (i, 0)), hp, hn]
        + [pl.BlockSpec(a.shape, lambda i: (0, 0)) for a in fulls],
        out_specs=[pl.BlockSpec((tt, c), lambda i: (i, 0)) for c in widths],
        scratch_shapes=[pltpu.VMEM((tt + 16, SHIFT_PAD), F32)],
        compiler_params=_cparams(("parallel",)),
    )(p, p, p, *fulls)


def _mix_prep_bwd(p, mu_p, mu_n, w2b, w0c, a2b, a0c, g2p, k_k, k_a, ct_terms, *, seq, tt, name):
    n_tok = p.shape[0]
    tps = seq // tt
    e2 = _head_ones()
    acc_shapes = [w2b.shape, w0c.shape, a2b.shape, a0c.shape, g2p.shape, k_k.shape, k_a.shape]
    cts = [a for terms in ct_terms for t in terms for a in (t if isinstance(t, tuple) else (t,))]

    def body(p_ref, hp_ref, hn_ref, mup_ref, mun_ref, w2b_ref, w0c_ref, a2b_ref, a0c_ref, g2p_ref, kk_ref, ka_ref,
             e2_ref, *rest):
        ct_refs, dps_o, acc_refs, ext = rest[:len(cts)], rest[len(cts)], rest[len(cts) + 1:-1], rest[-1]
        ct_it = iter(ct_refs)
        ct_vals = []
        for terms in ct_terms:
            total = None
            for t in terms:
                if isinstance(t, tuple):
                    val = jnp.concatenate([next(ct_it)[...] for _ in t], axis=1)
                else:
                    val = next(ct_it)[...]
                total = val if total is None else total + val
            ct_vals.append(total)
        i = pl.program_id(0)
        first = (i % tps) == 0
        last = (i % tps) == tps - 1
        pv = p_ref[...]
        ext[pl.ds(0, 8), :] = jnp.where(first, 0.0, hp_ref[...])
        ext[pl.ds(8, tt), :] = pv
        ext[pl.ds(8 + tt, 8), :] = jnp.where(last, 0.0, hn_ref[...])
        prev = ext[pl.ds(7, tt), :]
        nxt = ext[pl.ds(9, tt), :]
        ps = pv + mup_ref[...] * (prev - pv) + mun_ref[...] * (nxt - pv)
        e2v = e2_ref[...]
        _, vjp = jax.vjp(lambda *a: _prep(*a, e2v), ps, w2b_ref[...], w0c_ref[...], a2b_ref[...], a0c_ref[...],
                         g2p_ref[...], kk_ref[...], ka_ref[...])
        grads = vjp(tuple(ct_vals))
        dps_o[...] = grads[0]

        @pl.when(i == 0)
        def _():
            for r in acc_refs:
                r[...] = jnp.zeros_like(r)
        for r, val in zip(acc_refs, grads[1:]):
            r[...] += val

    hp, hn = _halo_specs(SHIFT_PAD, 8, tt, n_tok, 0)
    fulls = [mu_p, mu_n, w2b, w0c, a2b, a0c, g2p, k_k, k_a, e2]
    return pl.pallas_call(
        body, name=name,
        out_shape=[jax.ShapeDtypeStruct((n_tok, SHIFT_PAD), F32)] + [jax.ShapeDtypeStruct(s, F32) for s in acc_shapes],
        grid=(n_tok // tt,),
        in_specs=[pl.BlockSpec((tt, SHIFT_PAD), lambda i: (i, 0)), hp, hn]
        + [pl.BlockSpec(a.shape, lambda i: (0, 0)) for a in fulls]
        + [pl.BlockSpec((tt, c.shape[1]), lambda i: (i, 0)) for c in cts],
        out_specs=[pl.BlockSpec((tt, SHIFT_PAD), lambda i: (i, 0))] + [pl.BlockSpec(s, lambda i: (0, 0)) for s in acc_shapes],
        scratch_shapes=[pltpu.VMEM((tt + 16, SHIFT_PAD), F32)],
        compiler_params=_cparams(("arbitrary",)),
    )(p, p, p, *fulls, *cts)


def _shift_bwd(dps, p, mu_p, mu_n, *, seq, tt, name):
    n_tok = p.shape[0]
    tps = seq // tt

    def body(d_ref, dhp_ref, dhn_ref, p_ref, php_ref, phn_ref, mup_ref, mun_ref, dp_o, dmup_o, dmun_o, ext):
        i = pl.program_id(0)
        first = (i % tps) == 0
        last = (i % tps) == tps - 1
        mup, mun = mup_ref[...], mun_ref[...]
        rb = min(SHIFT_ROWS, tt)
        ext[pl.ds(0, 8), :] = jnp.where(first, 0.0, dhp_ref[...])
        ext[pl.ds(8, tt), :] = d_ref[...]
        ext[pl.ds(8 + tt, 8), :] = jnp.where(last, 0.0, dhn_ref[...])
        for r0 in range(0, tt, rb):
            dv = d_ref[pl.ds(r0, rb), :]
            dp_o[pl.ds(r0, rb), :] = (dv * (1.0 - mup - mun) + ext[pl.ds(r0 + 9, rb), :] * mup
                                      + ext[pl.ds(r0 + 7, rb), :] * mun).astype(dp_o.dtype)
        ext[pl.ds(0, 8), :] = jnp.where(first, 0.0, php_ref[...])
        ext[pl.ds(8, tt), :] = p_ref[...]
        ext[pl.ds(8 + tt, 8), :] = jnp.where(last, 0.0, phn_ref[...])

        @pl.when(i == 0)
        def _():
            dmup_o[...] = jnp.zeros_like(dmup_o)
            dmun_o[...] = jnp.zeros_like(dmun_o)
        sum_p = jnp.zeros_like(mup)
        sum_n = jnp.zeros_like(mun)
        for r0 in range(0, tt, rb):
            dv, pv = d_ref[pl.ds(r0, rb), :], p_ref[pl.ds(r0, rb), :]
            sum_p = sum_p + jnp.sum(dv * (ext[pl.ds(r0 + 7, rb), :] - pv), axis=0, keepdims=True)
            sum_n = sum_n + jnp.sum(dv * (ext[pl.ds(r0 + 9, rb), :] - pv), axis=0, keepdims=True)
        dmup_o[...] += sum_p
        dmun_o[...] += sum_n

    hp, hn = _halo_specs(SHIFT_PAD, 8, tt, n_tok, 0)
    tile = pl.BlockSpec((tt, SHIFT_PAD), lambda i: (i, 0))
    full = pl.BlockSpec((1, SHIFT_PAD), lambda i: (0, 0))
    return pl.pallas_call(
        body, name=name,
        out_shape=[jax.ShapeDtypeStruct((n_tok, SHIFT_PAD), BF16), jax.ShapeDtypeStruct((1, SHIFT_PAD), F32),
                   jax.ShapeDtypeStruct((1, SHIFT_PAD), F32)],
        grid=(n_tok // tt,),
        in_specs=[tile, hp, hn, tile, hp, hn, full, full],
        out_specs=[tile, full, full],
        scratch_shapes=[pltpu.VMEM((tt + 16, SHIFT_PAD), F32)],
        compiler_params=_cparams(("arbitrary",)),
    )(dps, dps, dps, p, p, p, mu_p, mu_n)


def _mix_post(y0, y1, r, v, kd, g, lnx_g, lnx_b, r_k, *, tt, name):
    e2 = _head_ones()
    return _rowcall(lambda i, *a: (_post(*a),), [y0, y1, r, v, kd, g], [lnx_g, lnx_b, r_k, e2], [(RW, BF16)], [],
                    tt=tt, name=name)[0]


def _mix_post_bwd(y0, y1, r, v, kd, g, lnx_g, lnx_b, r_k, dout, *, tt, name):
    e2 = _head_ones()

    def fn(i, y0v, y1v, rv, vv, kdv, gv, dov, lg, lb, rk, e2v):
        _, vjp = jax.vjp(lambda *a: _post(*a, e2v), y0v, y1v, rv, vv, kdv, gv, lg, lb, rk)
        gr = vjp(dov.astype(F32))
        return gr[0], gr[2], gr[3], gr[4], gr[5], gr[6], gr[7], gr[8]
    return _rowcall(fn, [y0, y1, r, v, kd, g, dout], [lnx_g, lnx_b, r_k, e2],
                    [(RW, F32), (RW, F32), (RW, F32), (2 * RW, F32), (RW, F32)], [(1, RW), (1, RW), (1, RW)],
                    tt=tt, name=name)


def _conv_fwd(p, dw, db, ln_g, ln_b, *, seq, tt, name):
    n_tok = p.shape[0]
    tps = seq // tt

    def glu(x, gate):
        return x * _sigmoid(gate)

    def body(u_ref, g_ref, uhp, ghp, uhn, ghn, dw_ref, db_ref, lg_ref, lb_ref, yc_o, y_o, ext):
        i = pl.program_id(0)
        first = (i % tps) == 0
        last = (i % tps) == tps - 1
        ext[pl.ds(0, 16), :] = jnp.where(first, 0.0, glu(uhp[...], ghp[...]))
        ext[pl.ds(16, tt), :] = glu(u_ref[...], g_ref[...])
        ext[pl.ds(16 + tt, 16), :] = jnp.where(last, 0.0, glu(uhn[...], ghn[...]))
        taps = [dw_ref[pl.ds(k, 1), :] for k in range(CONV_K)]
        for r0 in range(0, tt, CONV_ROWS):
            acc = jnp.zeros((CONV_ROWS, CW), F32) + db_ref[...]
            for k in range(CONV_K):
                acc = acc + ext[pl.ds(r0 + k + 1, CONV_ROWS), :] * taps[k]
            yc_o[pl.ds(r0, CONV_ROWS), :] = acc
        y_o[...] = _conv_post(yc_o[...], lg_ref[...], lb_ref[...]).astype(y_o.dtype)

    uhp_s, uhn_s = _halo_specs(CW, 16, tt, n_tok, 4)
    ghp_s, ghn_s = _halo_specs(CW, 16, tt, n_tok, 5)
    fulls = [dw, db, ln_g, ln_b]
    return pl.pallas_call(
        body, name=name,
        out_shape=[jax.ShapeDtypeStruct((n_tok, CW), F32), jax.ShapeDtypeStruct((n_tok, CW), BF16)],
        grid=(n_tok // tt,),
        in_specs=[pl.BlockSpec((tt, CW), lambda i: (i, 4)), pl.BlockSpec((tt, CW), lambda i: (i, 5)),
                  uhp_s, ghp_s, uhn_s, ghn_s] + [pl.BlockSpec(a.shape, lambda i: (0, 0)) for a in fulls],
        out_specs=[pl.BlockSpec((tt, CW), lambda i: (i, 0)), pl.BlockSpec((tt, CW), lambda i: (i, 0))],
        scratch_shapes=[pltpu.VMEM((tt + 32, CW), F32)],
        compiler_params=_cparams(("parallel",)),
    )(p, p, p, p, p, p, *fulls)


def _conv_post_bwd(yc, dy, ln_g, ln_b, *, tt, name):
    def fn(i, ycv, dyv, lg, lb):
        _, vjp = jax.vjp(_conv_post, ycv, lg, lb)
        dyc, dg, dbb = vjp(dyv.astype(F32))
        return dyc, dg, dbb, jnp.sum(dyc, axis=0, keepdims=True)
    return _rowcall(fn, [yc, dy], [ln_g, ln_b], [(CW, F32)], [(1, CW), (1, CW), (1, CW)], tt=tt, name=name)


def _conv_bwd(dyc, p, dw, *, seq, tt, name):
    n_tok = p.shape[0]
    tps = seq // tt

    def body(d_ref, dhp, dhn, u_ref, g_ref, uhp, ghp, uhn, ghn, dw_ref, dp_o, ddw_o, ext):
        i = pl.program_id(0)
        first = (i % tps) == 0
        last = (i % tps) == tps - 1
        dv = d_ref[...]
        ext[pl.ds(0, 16), :] = jnp.where(first, 0.0, dhp[...])
        ext[pl.ds(16, tt), :] = dv
        ext[pl.ds(16 + tt, 16), :] = jnp.where(last, 0.0, dhn[...])
        taps = [dw_ref[pl.ds(k, 1), :] for k in range(CONV_K)]
        for r0 in range(0, tt, CONV_ROWS):
            du = jnp.zeros((CONV_ROWS, CW), F32)
            for k in range(CONV_K):
                du = du + ext[pl.ds(r0 + 31 - k, CONV_ROWS), :] * taps[k]
            rows = pl.ds(r0, CONV_ROWS)
            sg_r = _sigmoid(g_ref[rows, :])
            dp_o[rows, 0:CW] = (du * sg_r).astype(dp_o.dtype)
            dp_o[rows, CW:2 * CW] = (du * u_ref[rows, :] * sg_r * (1.0 - sg_r)).astype(dp_o.dtype)
        uv, gv = u_ref[...], g_ref[...]
        sg = _sigmoid(gv)
        ext[pl.ds(0, 16), :] = jnp.where(first, 0.0, uhp[...] * _sigmoid(ghp[...]))
        ext[pl.ds(16, tt), :] = uv * sg
        ext[pl.ds(16 + tt, 16), :] = jnp.where(last, 0.0, uhn[...] * _sigmoid(ghn[...]))

        @pl.when(i == 0)
        def _():
            ddw_o[...] = jnp.zeros_like(ddw_o)
        for k in range(CONV_K):
            ddw_o[pl.ds(k, 1), :] += jnp.sum(dv * ext[pl.ds(k + 1, tt), :], axis=0, keepdims=True)

    dhp_s, dhn_s = _halo_specs(CW, 16, tt, n_tok, 0)
    uhp_s, uhn_s = _halo_specs(CW, 16, tt, n_tok, 4)
    ghp_s, ghn_s = _halo_specs(CW, 16, tt, n_tok, 5)
    return pl.pallas_call(
        body, name=name,
        out_shape=[jax.ShapeDtypeStruct((n_tok, 2 * CW), BF16), jax.ShapeDtypeStruct((32, CW), F32)],
        grid=(n_tok // tt,),
        in_specs=[pl.BlockSpec((tt, CW), lambda i: (i, 0)), dhp_s, dhn_s,
                  pl.BlockSpec((tt, CW), lambda i: (i, 4)), pl.BlockSpec((tt, CW), lambda i: (i, 5)),
                  uhp_s, ghp_s, uhn_s, ghn_s, pl.BlockSpec(dw.shape, lambda i: (0, 0))],
        out_specs=[pl.BlockSpec((tt, 2 * CW), lambda i: (i, 0)), pl.BlockSpec((32, CW), lambda i: (0, 0))],
        scratch_shapes=[pltpu.VMEM((tt + 32, CW), F32)],
        compiler_params=_cparams(("arbitrary",)),
    )(dyc, dyc, dyc, p, p, p, p, p, p, dw)


CHUNK = 64
_MM_DIMS = {"nn": (((2,), (1,)), ((0,), (0,))), "nt": (((2,), (2,)), ((0,), (0,))), "tn": (((1,), (1,)), ((0,), (0,)))}


def _mm16_raw(a, b, mode, fine):
    dot = lambda x, y: lax.dot_general(x, y, _MM_DIMS[mode], preferred_element_type=F32)
    if not fine:
        return dot(a.astype(BF16), b.astype(BF16))
    ah, (bh, bl) = a.astype(BF16), _split16(b)
    return dot(ah, bh) + dot(ah, bl)


@functools.partial(jax.custom_vjp, nondiff_argnums=(2, 3))
def _mm16(a, b, mode, fine=False):
    return _mm16_raw(a, b, mode, fine)


def _mm16_fwd(a, b, mode, fine):
    return _mm16_raw(a, b, mode, fine), (a, b)


def _mm16_bwd(mode, fine, res, g):
    a, b = res
    if mode == "nn":
        return _mm16_raw(g, b, "nt", fine), _mm16_raw(a, g, "tn", fine)
    if mode == "nt":
        return _mm16_raw(g, b, "nn", fine), _mm16_raw(g, a, "tn", fine)
    return _mm16_raw(b, g, "nt", fine), _mm16_raw(a, g, "nn", fine)


_mm16.defvjp(_mm16_fwd, _mm16_bwd)


def _tri_sum_raw(x, tri, mode):
    hi = x.astype(BF16)
    r1 = x - hi.astype(F32)
    mid = r1.astype(BF16)
    lo = (r1 - mid.astype(F32)).astype(BF16)
    width = x.shape[2]
    out = lax.dot_general(tri, jnp.concatenate([hi, mid, lo], axis=2), _MM_DIMS[mode], preferred_element_type=F32)
    return out[:, :, :width] + out[:, :, width:2 * width] + out[:, :, 2 * width:]


@jax.custom_vjp
def _tri_sum(x, tri):
    return _tri_sum_raw(x, tri, "nn")


def _tri_sum_fwd(x, tri):
    return _tri_sum_raw(x, tri, "nn"), tri


def _tri_sum_bwd(tri, g):
    return _tri_sum_raw(g, tri, "tn"), jnp.zeros_like(tri)


_tri_sum.defvjp(_tri_sum_fwd, _tri_sum_bwd)


PAIR = 2 * HEAD
N_PAIRS = RW // PAIR


def _pair_rows(x):
    first = lax.broadcasted_iota(jnp.int32, x.shape, 2) < HEAD
    return jnp.concatenate([jnp.where(first, x, 0.0), jnp.where(first, 0.0, x)], axis=1)


def _chunk_step_pairs(s0, r, lw, k, v, kk, b, tri, rev):
    g, n, _ = r.shape
    row = lax.broadcasted_iota(jnp.int32, (g, n, PAIR), 1)
    col = lax.broadcasted_iota(jnp.int32, (g, n, PAIR), 2) % HEAD
    if rev:
        row, col = col, row
    diag = (lax.broadcasted_iota(jnp.int32, (g, PAIR, PAIR), 1) // HEAD
            == lax.broadcasted_iota(jnp.int32, (g, PAIR, PAIR), 2) // HEAD)
    cum = _tri_sum(lw, tri)
    up, down = jnp.exp(cum), jnp.exp(-cum)
    at, rt = -kk * jnp.exp(cum - lw), r * up
    kt, bt = k * down, b * down
    bt_rows, kt_rows = _pair_rows(bt), _pair_rows(kt)
    ar = jnp.concatenate([at, rt], axis=1)
    with_b = _mm16(ar, bt_rows, "nt")
    a_ab = jnp.where(col < row, with_b[:, :n], 0.0)
    a_rb = jnp.where(col <= row, with_b[:, n:], 0.0)
    with_k = _mm16(ar, kt_rows, "nt", True)
    a_ak = jnp.where(col < row, with_k[:, :n], 0.0)
    a_rk = jnp.where(col <= row, with_k[:, n:], 0.0)
    v_rows = _pair_rows(v)
    from_state = _mm16(ar, s0, "nt")
    u = from_state[:, :n] + _mm16(a_ak, v_rows, "nn")
    power = a_ab
    steps = n.bit_length() - 1
    for it in range(steps):
        if it + 1 < steps:
            both = _mm16(power, jnp.concatenate([_pair_rows(u), _pair_rows(power)], axis=2), "nn")
            u = u + both[:, :, :PAIR]
            power = both[:, :, PAIR:]
        else:
            u = u + _mm16(power, _pair_rows(u), "nn")
    y = from_state[:, n:] + _mm16(jnp.concatenate([a_rk, a_rb], axis=2),
                                  jnp.concatenate([v_rows, _pair_rows(u)], axis=1), "nn")
    grown = s0 + jnp.where(diag, _mm16(jnp.concatenate([v, u], axis=1), jnp.concatenate([kt, bt], axis=1), "tn"), 0.0)
    return y, grown * jnp.exp(jnp.sum(lw, axis=1, keepdims=True))


SCAN_SEQS = 4


def _tri_ones(rev, nseq):
    shape = (nseq * N_PAIRS, CHUNK, CHUNK)
    row, col = lax.broadcasted_iota(jnp.int32, shape, 1), lax.broadcasted_iota(jnp.int32, shape, 2)
    return ((col >= row) if rev else (col <= row)).astype(BF16)


def _split_heads(ref):
    return jnp.stack([ref[q, :, pl.ds(h * PAIR, PAIR)] for q in range(ref.shape[0]) for h in range(N_PAIRS)])


def _merge_heads(ref, val):
    for q in range(ref.shape[0]):
        for h in range(N_PAIRS):
            ref[q, :, pl.ds(h * PAIR, PAIR)] = val[q * N_PAIRS + h]


def _chunk_specs(nseq, nc, rev, dcol):
    chunk = (lambda c: nc - 1 - c) if rev else (lambda c: c)
    shared = pl.BlockSpec((nseq, CHUNK, RW), lambda s, c: (s, chunk(c), 0))
    own = pl.BlockSpec((nseq, CHUNK, RW), lambda s, c: (s, chunk(c), dcol))
    return shared, own


def _wkv_chunk_fwd(r, lw, k, v, kk, b, *, rev, name):
    bsz, seq, _ = r.shape
    nc = seq // CHUNK
    nseq = SCAN_SEQS if bsz % SCAN_SEQS == 0 else 1
    shared, own = _chunk_specs(nseq, nc, rev, int(rev))

    def body(r_ref, lw_ref, k_ref, v_ref, kk_ref, b_ref, tri_ref, y_o, s0_o, s_ref):
        @pl.when(pl.program_id(1) == 0)
        def _():
            s_ref[...] = jnp.zeros_like(s_ref)

        s0 = s_ref[...]
        s0_o[:, 0] = s0.reshape(nseq, N_PAIRS, PAIR, PAIR)
        y, s_ref[...] = _chunk_step_pairs(s0, *[_split_heads(x) for x in (r_ref, lw_ref, k_ref, v_ref, kk_ref, b_ref)],
                                    tri_ref[...], rev)
        _merge_heads(y_o, y)

    return pl.pallas_call(
        body, name=name,
        out_shape=[jax.ShapeDtypeStruct((bsz, seq, RW), F32), jax.ShapeDtypeStruct((bsz, nc, N_PAIRS, PAIR, PAIR), F32)],
        grid=(bsz // nseq, nc),
        in_specs=[shared, own, own, shared, shared, own,
                  pl.BlockSpec((nseq * N_PAIRS, CHUNK, CHUNK), lambda s, c: (0, 0, 0))],
        out_specs=[shared, pl.BlockSpec((nseq, 1, N_PAIRS, PAIR, PAIR), lambda s, c: (s, c, 0, 0, 0))],
        scratch_shapes=[pltpu.VMEM((nseq * N_PAIRS, PAIR, PAIR), F32)],
        compiler_params=_cparams(("parallel", "arbitrary")),
    )(r, lw, k, v, kk, b, _tri_ones(rev, nseq))


def _wkv_chunk_bwd(r, lw, k, v, kk, b, dy, s0, *, rev, name):
    bsz, seq, _ = r.shape
    nc = seq // CHUNK
    nseq = SCAN_SEQS if bsz % SCAN_SEQS == 0 else 1
    shared, own = _chunk_specs(nseq, nc, not rev, int(rev))

    def body(r_ref, lw_ref, k_ref, v_ref, kk_ref, b_ref, dy_ref, s0_ref, tri_ref, *rest):
        outs, ds_ref = rest[:-1], rest[-1]

        @pl.when(pl.program_id(1) == 0)
        def _():
            ds_ref[...] = jnp.zeros_like(ds_ref)

        triv = tri_ref[...]
        _, vjp = jax.vjp(lambda *a: _chunk_step_pairs(*a, triv, rev), s0_ref[:, 0].reshape(nseq * N_PAIRS, PAIR, PAIR),
                         *[_split_heads(x) for x in (r_ref, lw_ref, k_ref, v_ref, kk_ref, b_ref)])
        grads = vjp((_split_heads(dy_ref), ds_ref[...]))
        ds_ref[...] = grads[0]
        for o, gval in zip(outs, grads[1:]):
            _merge_heads(o, gval)

    return pl.pallas_call(
        body, name=name,
        out_shape=[jax.ShapeDtypeStruct((bsz, seq, RW), F32)] * 6,
        grid=(bsz // nseq, nc),
        in_specs=[shared, own, own, shared, shared, own, shared,
                  pl.BlockSpec((nseq, 1, N_PAIRS, PAIR, PAIR), lambda s, c: (s, nc - 1 - c, 0, 0, 0)),
                  pl.BlockSpec((nseq * N_PAIRS, CHUNK, CHUNK), lambda s, c: (0, 0, 0))],
        out_specs=[shared] * 6,
        scratch_shapes=[pltpu.VMEM((nseq * N_PAIRS, PAIR, PAIR), F32)],
        compiler_params=_cparams(("parallel", "arbitrary")),
    )(r, lw, k, v, kk, b, dy, s0, _tri_ones(rev, nseq))


def _block_diag2(w):
    z = jnp.zeros_like(w[0])
    return jnp.concatenate([jnp.concatenate([w[0], z], axis=1), jnp.concatenate([z, w[1]], axis=1)], axis=0)


def _pad_in_cols(a):
    z = jnp.zeros(a.shape[:-1] + (SHIFT_PAD - SHIFT_COLS,), a.dtype)
    return jnp.concatenate([a[..., :SHIFT_COLS], z, a[..., SHIFT_COLS:]], axis=-1)


def _follow(small, token):
    return small if token is None else small + token[0:1, 0:1]


def _local_step(x, target, wts, *, tt, start_token=None, more_weights=None, grads_ready=None, small_ready=None):
    bsz, seq, _ = x.shape
    n_tok = bsz * seq
    row = lambda a: a.reshape(1, -1).astype(F32)
    x0 = x.reshape(n_tok, D_MODEL)
    tgt = target.reshape(n_tok, D_MODEL)
    ln = {k: row(wts[k]) for k in ("ln1_g", "ln1_b", "ln2_g", "ln2_b", "ln3_g", "ln3_b")}
    if grads_ready is None:
        grads_ready = lambda names, slabs: None

    w1i = wts["ffn1_w_in"]
    h1, act1 = _ffn_in(x0, w1i, tm=TM_FFN, after=start_token, name="ffn1_in")
    if more_weights is not None:
        wts = {**wts, **more_weights("ffn1_out", act1)}
    w1o = wts["ffn1_w_out"]
    z1, x1, x1b = _mm_ln([act1], w1o, x0, ln["ln1_g"], ln["ln1_b"], 0.5, tm=TM_LN, name="ffn1_out_ln1")
    if more_weights is not None:
        wts = {**wts, **more_weights("mix", x1b)}
    win = _pad_in_cols(wts["w_in"])
    zpad = jnp.zeros((1, SHIFT_PAD - SHIFT_COLS), F32)
    mu_p = jnp.concatenate([row(wts["mu_prev"]), zpad], axis=1)
    mu_n = jnp.concatenate([row(wts["mu_next"]), zpad], axis=1)
    w2b, a2b = _block_diag2(wts["w2"]), _block_diag2(wts["a2"])
    w0c, a0c = row(wts["w0"]), row(wts["a0"])
    g2p = jnp.concatenate([wts["g2"], jnp.zeros((GATE_PAD - GATE_LORA, RW), F32)], axis=0)
    k_k, k_a, r_k = row(wts["k_k"]), row(wts["k_a"]), row(wts["r_k"])
    lnx_g, lnx_b = row(wts["lnx_g"]), row(wts["lnx_b"])
    cdw, cb, clg, clb = wts["conv_dw"], row(wts["conv_b"]), row(wts["conv_ln_g"]), row(wts["conv_ln_b"])
    small = (mu_p, mu_n, w2b, w0c, a2b, a0c, g2p, k_k, k_a)
    seq3 = lambda a: a.reshape(bsz, seq, a.shape[-1])
    flat = lambda a: a.reshape(n_tok, a.shape[-1])

    p = _matmul(x1b, win, name="proj_in")
    r, v, kk, w, kd, b, g = _mix_prep(p, *small, seq=seq, tt=tt, name="mix_prep")
    scan_in = [seq3(a) for a in (r, w, kd, v, kk, b)]
    y0, s_chunks0 = _wkv_chunk_fwd(*scan_in, rev=False, name="wkv_fwd_dir0")
    y1, s_chunks1 = _wkv_chunk_fwd(*scan_in, rev=True, name="wkv_fwd_dir1")
    y0, y1 = flat(y0), flat(y1)
    yr = _mix_post(y0, y1, r, v, kd, g, lnx_g, lnx_b, r_k, tt=tt, name="mix_post")
    yc, yv = _conv_fwd(p, cdw, cb, clg, clb, seq=seq, tt=tt, name="conv_fwd")
    if more_weights is not None:
        wts = {**wts, **more_weights("out", yr)}
    wout, w2i, w2o = wts["w_out"], wts["ffn2_w_in"], wts["ffn2_w_out"]
    z2, x2, x2b = _mm_ln([yr, yv], wout, x1, ln["ln2_g"], ln["ln2_b"], 1.0, tm=TM_LN, name="proj_out_ln2")
    h2, act2 = _ffn_in(x2b, w2i, tm=TM_FFN, name="ffn2_in")

    gr = {}
    slab_rows = lambda a: a.reshape((N_CHIPS, a.shape[0] // N_CHIPS) + a.shape[1:])
    dw_kw = dict(ta=True, out_dtype=BF16)
    dz3, gr["ln3_g"], gr["ln3_b"], loss_part = _mm_ln_loss(act2, w2o, x2, ln["ln3_g"], ln["ln3_b"], tgt, 0.5, tm=TM_LN,
                                                           name="ffn2_out_ln3_loss")
    dh2 = _ffn_out_bwd(dz3, w2o, h2, tm=TM_FFN, name="ffn2_out_dx")
    gr["ffn2_w_out"] = slab_rows(_matmul(act2, dz3, scale=0.5, tm=D_FF // 2, name="ffn2_out_dw", **dw_kw))
    dz2, gr["ln2_g"], gr["ln2_b"] = _mm_nt_res([dh2], w2i, dz3, ln=(z2, ln["ln2_g"], ln["ln2_b"]), tm=TM_FFN,
                                               name="ffn2_in_dx_ln2")
    gr["ffn2_w_in"] = _matmul(x2b, dh2, col_slabs=True, tn=2 * D_FF // N_CHIPS, name="ffn2_in_dw", **dw_kw)
    dmix = _matmul(dz2, wout, tb=True, name="proj_out_dx")
    gr["w_out"] = slab_rows(jnp.concatenate([_matmul(yr, dz2, name="proj_out_dw_rwkv", **dw_kw),
                                             _matmul(yv, dz2, name="proj_out_dw_conv", **dw_kw)], axis=0))
    tok = grads_ready(("ffn2_w_out", "ffn2_w_in", "w_out"), [gr["ffn2_w_out"], gr["ffn2_w_in"], gr["w_out"]])
    dyr, dyv = (dmix, RW, 0), (dmix, RW, 1)
    dy, dr_p, dv_p, dkd_p, dg, gr["lnx_g"], gr["lnx_b"], gr["r_k"] = _mix_post_bwd(
        y0, y1, r, v, kd, g, _follow(lnx_g, tok), lnx_b, r_k, dyr, tt=tt, name="mix_post_bwd")
    dr0, dw0, dkd0, dv0, dk0, db0 = [flat(a) for a in _wkv_chunk_bwd(*scan_in, seq3(dy), s_chunks0, rev=False,
                                                                      name="wkv_bwd_dir0")]
    dr1, dw1, dkd1, dv1, dk1, db1 = [flat(a) for a in _wkv_chunk_bwd(*scan_in, seq3(dy), s_chunks1, rev=True,
                                                                      name="wkv_bwd_dir1")]
    ct_terms = [[dr_p, dr0, dr1], [dv_p, dv0, dv1], [dk0, dk1], [(dw0, dw1)], [dkd_p, (dkd0, dkd1)], [(db0, db1)], [dg]]
    dyc, gr["conv_ln_g"], gr["conv_ln_b"], gr["conv_b"] = _conv_post_bwd(yc, dyv, clg, clb, tt=tt, name="conv_post_bwd")
    dpc, ddw = _conv_bwd(dyc, p, cdw, seq=seq, tt=tt, name="conv_bwd")
    gr["conv_dw"] = ddw[:CONV_K]
    dps, dw2b, dw0c, da2b, da0c, dg2p, gr["k_k"], gr["k_a"] = _mix_prep_bwd(
        p, *small, ct_terms, seq=seq, tt=tt, name="mix_prep_bwd")
    gr["w2"] = jnp.stack([dw2b[:LORA, :RW], dw2b[LORA:, RW:]])
    gr["a2"] = jnp.stack([da2b[:LORA, :RW], da2b[LORA:, RW:]])
    gr["w0"], gr["a0"], gr["g2"] = dw0c.reshape(2, RW), da0c.reshape(2, RW), dg2p[:GATE_LORA]
    dpsh, dmu_p, dmu_n = _shift_bwd(dps, p, mu_p, mu_n, seq=seq, tt=tt, name="shift_bwd")
    gr["mu_prev"], gr["mu_next"] = dmu_p[:, :SHIFT_COLS], dmu_n[:, :SHIFT_COLS]
    dwin = jnp.concatenate([_matmul(x1b, dpsh, name="proj_in_dw_shift", **dw_kw)[:, :SHIFT_COLS],
                            _matmul(x1b, dpc, name="proj_in_dw_conv", **dw_kw)], axis=1)
    gr["w_in"] = jnp.moveaxis(dwin.reshape(D_MODEL, N_CHIPS, IN_COLS // N_CHIPS), 1, 0)
    tok = grads_ready(("w_in",), [gr["w_in"]])
    dz1, gr["ln1_g"], gr["ln1_b"] = _mm_nt_res([dpsh, dpc], win, dz2, ln=(z1, ln["ln1_g"], ln["ln1_b"]), tm=TM_FFN,
                                               after=tok, name="proj_in_dx_ln1")
    gr["loss"] = loss_part
    tok = small_ready(gr, loss_part) if small_ready is not None else None
    dh1 = _ffn_out_bwd(dz1, w1o, h1, tm=TM_FFN, after=tok, name="ffn1_out_dx")
    gr["ffn1_w_out"] = slab_rows(_matmul(act1, dz1, scale=0.5, tm=D_FF // 2, name="ffn1_out_dw", **dw_kw))
    tok = grads_ready(("ffn1_w_out",), [gr["ffn1_w_out"]])
    gr["ffn1_w_in"] = _matmul(x0, dh1, col_slabs=True, tn=2 * D_FF // N_CHIPS, after=tok, name="ffn1_in_dw", **dw_kw)
    tok = grads_ready(("ffn1_w_in",), [gr["ffn1_w_in"]])
    dx0 = _mm_nt_res([dh1], w1i, dz1, tm=TM_FFN, after=tok, name="ffn1_in_dx")
    return dx0.reshape(bsz, seq, D_MODEL), gr


def _mesh_pos():
    return lax.axis_index("x"), lax.axis_index("y"), lax.axis_index("c")


def _other_chips(x, y):
    return [(1 - x, y), (x, 1 - y), (1 - x, 1 - y)]


def _gather_chips(shards, *, name):
    n = len(shards)
    halves = [s.shape[0] // 2 for s in shards]
    assert all(2 * h == s.shape[0] for h, s in zip(halves, shards))

    def body(*refs):
        ins, outs = refs[:n], refs[n:2 * n]
        send_sems, recv_sems, fwd_send_sems, fwd_recv_sems, loc_sems = refs[2 * n:]
        x, y, c = _mesh_pos()
        q = 2 * x + y
        peers = _other_chips(x, y)
        local = [pltpu.make_async_copy(ins[a], outs[a].at[q], loc_sems.at[a]) for a in range(n)]
        for cp in local:
            cp.start()

        def half(a, chip, core):
            return outs[a].at[chip, pl.ds(core * halves[a], halves[a])]

        sends = [pltpu.make_async_remote_copy(ins[a].at[pl.ds(c * halves[a], halves[a])], half(a, q, c),
                                              send_sems.at[a, k], recv_sems.at[a, k],
                                              device_id=(px, py, c), device_id_type=MESH)
                 for a in range(n) for k, (px, py) in enumerate(peers)]
        for cp in sends:
            cp.start()
        passed = []
        for a in range(n):
            for k, (px, py) in enumerate(peers):
                mine = half(a, 2 * px + py, c)
                pltpu.make_async_remote_copy(mine, mine, send_sems.at[a, k], recv_sems.at[a, k],
                                             device_id=(px, py, c), device_id_type=MESH).wait_recv()
                cp = pltpu.make_async_remote_copy(mine, mine, fwd_send_sems.at[a, k], fwd_recv_sems.at[a, k],
                                                  device_id=(x, y, 1 - c), device_id_type=MESH)
                cp.start()
                passed.append(cp)
        for a in range(n):
            for k, (px, py) in enumerate(peers):
                theirs = half(a, 2 * px + py, 1 - c)
                pltpu.make_async_remote_copy(theirs, theirs, fwd_send_sems.at[a, k], fwd_recv_sems.at[a, k],
                                             device_id=(x, y, 1 - c), device_id_type=MESH).wait_recv()
        for cp in sends + passed:
            cp.wait_send()
        for cp in local:
            cp.wait()

    any_spec = pl.BlockSpec(memory_space=pl.ANY)
    return pl.pallas_call(
        body, name=name,
        out_shape=[jax.ShapeDtypeStruct((N_CHIPS,) + s.shape, s.dtype) for s in shards],
        in_specs=[any_spec] * n, out_specs=[any_spec] * n,
        scratch_shapes=[pltpu.SemaphoreType.DMA((n, 3))] * 4 + [pltpu.SemaphoreType.DMA((n,))],
        compiler_params=pltpu.CompilerParams(has_side_effects=True),
    )(*shards)


HBM_SPEC = pl.BlockSpec(memory_space=pltpu.HBM)
SEM_SPEC = pl.BlockSpec(memory_space=pltpu.SEMAPHORE)
ANY_SPEC = pl.BlockSpec(memory_space=pl.ANY)
SIDE_EFFECT = pltpu.SideEffectType.DATAFLOW_SIDE_EFFECTING


def _chip_copies(src_refs, land_refs, send_sems, recv_sems, scatter, arriving=False):
    x, y, c = _mesh_pos()
    cps = []
    for a, (src, land) in enumerate(zip(src_refs, land_refs)):
        for k, (px, py) in enumerate(_other_chips(x, y)):
            slot = k if scatter else (2 * px + py if arriving else 2 * x + y)
            cps.append(pltpu.make_async_remote_copy(src.at[2 * px + py] if scatter else src, land.at[slot],
                                                    send_sems.at[3 * a + k], recv_sems.at[3 * a + k],
                                                    device_id=(px, py, c), device_id_type=MESH))
    return cps


def _exchange_start(srcs, *, scatter, after, name):
    n = len(srcs)
    lands = [lax.empty((3,) + s.shape[1:] if scatter else (N_CHIPS,) + s.shape, s.dtype) for s in srcs]

    def body(*refs):
        src_refs, land_refs = refs[:n], refs[n:2 * n]
        send_sems, recv_sems = refs[2 * n + 1:2 * n + 3]
        token = refs[-1]
        for cp in _chip_copies(src_refs, land_refs, send_sems, recv_sems, scatter):
            cp.start()
        token[...] = jnp.zeros_like(token)

    hbm = lambda a: pltpu.with_memory_space_constraint(a, pltpu.HBM)
    outs = pl.pallas_call(
        body, name=name,
        out_shape=(pltpu.SemaphoreType.DMA((3 * n,)), pltpu.SemaphoreType.DMA((3 * n,)),
                   *[pltpu.HBM(a.shape, a.dtype) for a in srcs + lands], jax.ShapeDtypeStruct((8, LANES), F32)),
        in_specs=[HBM_SPEC] * (2 * n) + [ANY_SPEC],
        out_specs=(SEM_SPEC, SEM_SPEC, *[HBM_SPEC] * (2 * n), pl.BlockSpec(memory_space=pltpu.VMEM)),
        input_output_aliases={i: 2 + i for i in range(2 * n)},
        compiler_params=pltpu.CompilerParams(has_side_effects=SIDE_EFFECT),
    )(*[hbm(a) for a in srcs + lands], after)
    return outs[0], outs[1], list(outs[2:2 + n]), list(outs[2 + n:2 + 2 * n]), outs[-1]


def _exchange_wait(started, *, scatter, after, name):
    send_sems, recv_sems, srcs, lands, _ = started
    n = len(srcs)

    def body(*refs):
        src_refs, land_refs = refs[:n], refs[n:2 * n]
        send_s, recv_s = refs[2 * n:2 * n + 2]
        for cp in _chip_copies(src_refs, land_refs, send_s, recv_s, scatter, arriving=True):
            cp.wait_send()
            cp.wait_recv()

    outs = pl.pallas_call(
        body, name=name,
        out_shape=tuple(pltpu.HBM(a.shape, a.dtype) for a in srcs + lands),
        in_specs=[HBM_SPEC] * (2 * n) + [SEM_SPEC, SEM_SPEC, ANY_SPEC],
        out_specs=tuple([HBM_SPEC] * (2 * n)),
        input_output_aliases={i: i for i in range(2 * n)},
        compiler_params=pltpu.CompilerParams(has_side_effects=SIDE_EFFECT),
    )(*srcs, *lands, send_sems, recv_sems, after)
    return list(outs[:n]), list(outs[n:])


def _by_chip(own, land):
    xi, yi, _ = _mesh_pos()
    return lax.dynamic_update_index_in_dim(land, own, 2 * xi + yi, 0)


def _swap_sibling(arrs, *, name):
    n = len(arrs)

    def body(*refs):
        ins, outs = refs[:n], refs[n:2 * n]
        send_sems, recv_sems = refs[2 * n:]
        x, y, c = _mesh_pos()
        cps = [pltpu.make_async_remote_copy(ins[a], outs[a], send_sems.at[a], recv_sems.at[a],
                                            device_id=(x, y, 1 - c), device_id_type=MESH) for a in range(n)]
        for cp in cps:
            cp.start()
        for cp in cps:
            cp.wait_recv()
        for cp in cps:
            cp.wait_send()

    any_spec = pl.BlockSpec(memory_space=pl.ANY)
    return pl.pallas_call(
        body, name=name,
        out_shape=[jax.ShapeDtypeStruct(s.shape, s.dtype) for s in arrs],
        in_specs=[any_spec] * n, out_specs=[any_spec] * n,
        scratch_shapes=[pltpu.SemaphoreType.DMA((n,)), pltpu.SemaphoreType.DMA((n,))],
        compiler_params=pltpu.CompilerParams(has_side_effects=True),
    )(*arrs)


def _device_copies(v_ref, land_ref, send_sems, recv_sems, arriving=False):
    x, y, c = _mesh_pos()
    me = 4 * x + 2 * y + c
    cps = []
    for m in range(1, 8):
        px, py, pc = (x + ((m >> 2) & 1)) % 2, (y + ((m >> 1) & 1)) % 2, (c + (m & 1)) % 2
        slot = 4 * px + 2 * py + pc if arriving else me
        cps.append(pltpu.make_async_remote_copy(v_ref, land_ref.at[slot], send_sems.at[m - 1], recv_sems.at[m - 1],
                                                device_id=(px, py, pc), device_id_type=MESH))
    return cps


def _allsum_start(vec, *, after, name):
    land = lax.empty((8,) + vec.shape, F32)

    def body(v_ref, land_ref, _after, send_sems, recv_sems, v_thru, land_thru, token):
        for cp in _device_copies(v_ref, land_ref, send_sems, recv_sems):
            cp.start()
        token[...] = jnp.zeros_like(token)

    hbm = lambda a: pltpu.with_memory_space_constraint(a, pltpu.HBM)
    return pl.pallas_call(
        body, name=name,
        out_shape=(pltpu.SemaphoreType.DMA((7,)), pltpu.SemaphoreType.DMA((7,)), pltpu.HBM(vec.shape, F32),
                   pltpu.HBM(land.shape, F32), jax.ShapeDtypeStruct((8, LANES), F32)),
        in_specs=[HBM_SPEC, HBM_SPEC, ANY_SPEC],
        out_specs=(SEM_SPEC, SEM_SPEC, HBM_SPEC, HBM_SPEC, pl.BlockSpec(memory_space=pltpu.VMEM)),
        input_output_aliases={0: 2, 1: 3},
        compiler_params=pltpu.CompilerParams(has_side_effects=SIDE_EFFECT),
    )(hbm(vec), hbm(land), after)


def _allsum_wait(started, *, after, name):
    send_sems, recv_sems, vec, land, _ = started

    def body(v_ref, land_ref, send_s, recv_s, _after, v_dead, got):
        for cp in _device_copies(v_ref, land_ref, send_s, recv_s, arriving=True):
            cp.wait_send()
            cp.wait_recv()

    vec, land = pl.pallas_call(
        body, name=name,
        out_shape=(pltpu.HBM(vec.shape, F32), pltpu.HBM(land.shape, F32)),
        in_specs=[HBM_SPEC, HBM_SPEC, SEM_SPEC, SEM_SPEC, ANY_SPEC],
        out_specs=(HBM_SPEC, HBM_SPEC),
        input_output_aliases={0: 0, 1: 1},
        compiler_params=pltpu.CompilerParams(has_side_effects=SIDE_EFFECT),
    )(vec, land, send_sems, recv_sems, after)
    xi, yi, ci = _mesh_pos()
    every = lax.dynamic_update_index_in_dim(land, vec, 4 * xi + 2 * yi + ci, 0)

    def add(e_ref, o_ref):
        acc = e_ref[0]
        for d in range(1, 8):
            acc = acc + e_ref[d]
        o_ref[...] = acc

    vm = pl.BlockSpec(memory_space=pltpu.VMEM)
    return pl.pallas_call(add, name=name + "_sum", out_shape=jax.ShapeDtypeStruct(vec.shape, F32), in_specs=[vm],
                          out_specs=vm, compiler_params=_cparams())(every)


def _adamw(w, g, m, v):
    m = ADAM_B1 * m + (1.0 - ADAM_B1) * g
    v = ADAM_B2 * v + (1.0 - ADAM_B2) * (g * g)
    m_hat = m / (1.0 - ADAM_B1 ** ADAM_STEP)
    v_hat = v / (1.0 - ADAM_B2 ** ADAM_STEP)
    delta = -ADAM_LR * (m_hat / (jnp.sqrt(v_hat) + ADAM_EPS) + ADAM_WD * w)
    return delta, m, v


def _sum4(mine, land, *, name):
    rows, cols = mine.shape
    tr = _pick_rows(rows)

    def body(a_ref, l_ref, o_ref):
        o_ref[...] = (a_ref[...].astype(F32) + l_ref[0].astype(F32)) + (l_ref[1].astype(F32) + l_ref[2].astype(F32))

    return pl.pallas_call(
        body, name=name, out_shape=jax.ShapeDtypeStruct((rows, cols), F32), grid=(rows // tr,),
        in_specs=[pl.BlockSpec((tr, cols), lambda i: (i, 0)), pl.BlockSpec((3, tr, cols), lambda i: (0, i, 0))],
        out_specs=pl.BlockSpec((tr, cols), lambda i: (i, 0)),
        compiler_params=_cparams(("parallel",)),
    )(mine, land)


def _pick_rows(rows, want=256):
    for t in range(min(want, rows) // 8 * 8, 0, -8):
        if rows % t == 0:
            return t
    return rows


def _sum_adam(h_mine, h_sib, w, m, v, *, name):
    rows, cols = w.shape
    tr = _pick_rows(rows)

    def body(a_ref, b_ref, w_ref, m_ref, v_ref, g_o, d_o, m_o, v_o):
        g = a_ref[...] + b_ref[...]
        d, mn, vn = _adamw(w_ref[...], g, m_ref[...], v_ref[...])
        g_o[...], d_o[...], m_o[...], v_o[...] = g, d, mn, vn

    spec = pl.BlockSpec((tr, cols), lambda i: (i, 0))
    return pl.pallas_call(
        body, name=name, out_shape=[jax.ShapeDtypeStruct((rows, cols), F32)] * 4, grid=(rows // tr,),
        in_specs=[spec] * 5, out_specs=[spec] * 4, compiler_params=_cparams(("parallel",)),
    )(h_mine, h_sib, w, m, v)


def _adam_rows(w, g, m, v, *, name):
    def body(w_ref, g_ref, m_ref, v_ref, d_o, m_o, v_o):
        d_o[...], m_o[...], v_o[...] = _adamw(w_ref[...], g_ref[...], m_ref[...], v_ref[...])

    vm = pl.BlockSpec(memory_space=pltpu.VMEM)
    return pl.pallas_call(
        body, name=name, out_shape=[jax.ShapeDtypeStruct(w.shape, F32)] * 3,
        in_specs=[vm] * 4, out_specs=[vm] * 3, compiler_params=_cparams(),
    )(w, g, m, v)


def _size(shape):
    size = 1
    for d in shape:
        size *= d
    return size


def _pack_rows(arrs):
    blocks = []
    for a in arrs:
        flat = a.reshape(-1).astype(F32)
        flat = jnp.concatenate([flat, jnp.zeros((-flat.shape[0] % (8 * LANES),), F32)])
        blocks.append(flat.reshape(-1, LANES))
    return jnp.concatenate(blocks, axis=0)


def _unpack_rows(packed, shapes):
    out, row = [], 0
    for s in shapes:
        rows = -(-_size(s) // (8 * LANES)) * 8
        out.append(packed[row:row + rows].reshape(-1)[:_size(s)].reshape(s))
        row += rows
    return out


WEIGHTS = ['ffn1_w_in', 'ffn1_w_out', 'w_in', 'mu_prev', 'mu_next', 'w0', 'w2', 'a0', 'a2', 'g2', 'k_k', 'k_a', 'r_k',
           'lnx_g', 'lnx_b', 'conv_dw', 'conv_b', 'conv_ln_g', 'conv_ln_b', 'w_out', 'ffn2_w_in', 'ffn2_w_out',
           'ln1_g', 'ln1_b', 'ln2_g', 'ln2_b', 'ln3_g', 'ln3_b']
COL_SHARDED = ('ffn1_w_in', 'w_in', 'ffn2_w_in')
ROW_SHARDED = ('ffn1_w_out', 'w_out', 'ffn2_w_out')
BIG = COL_SHARDED + ROW_SHARDED
SMALL_SHARDED = ('w0', 'w2', 'a0', 'a2', 'g2', 'conv_dw')
REPLICATED = tuple(n for n in WEIGHTS if n not in BIG + SMALL_SHARDED)


def _train_step(x, target, w, m, v, *, tt):
    xi, yi, _ = _mesh_pos()
    q = 2 * xi + yi

    later = {"ffn1_out": ("ffn1_w_out",), "mix": ("w_in",) + SMALL_SHARDED, "out": ("w_out", "ffn2_w_in", "ffn2_w_out")}
    shard = lambda n: w[n][0].astype(BF16) if n in BIG else w[n][0]
    small_names = REPLICATED + SMALL_SHARDED

    def whole(n, slabs):
        if n in ROW_SHARDED:
            return slabs.reshape((-1,) + slabs.shape[2:])
        if n in ("ffn1_w_in", "ffn2_w_in"):
            return slabs
        return jnp.moveaxis(slabs, 0, -2).reshape(slabs.shape[1:-1] + (N_CHIPS * slabs.shape[-1],))

    full = {n: w[n][0] for n in REPLICATED}
    first = _gather_chips([shard("ffn1_w_in")], name="gather_ffn1_in")
    full["ffn1_w_in"] = whole("ffn1_w_in", first[0])
    started, token = {}, first[0]
    for stage, names in later.items():
        started[stage] = _exchange_start([shard(n) for n in names], scatter=False, after=token,
                                         name="gather_%s_start" % stage)
        token = started[stage][-1]

    def more_weights(stage, after):
        own, land = _exchange_wait(started[stage], scatter=False, after=after, name="gather_%s_wait" % stage)
        got = {n: whole(n, _by_chip(o, l)) for n, o, l in zip(later[stage], own, land)}
        full.update(got)
        return got

    small_sent = []

    def small_ready(gr, loss_part):
        vec = _pack_rows([gr[n] for n in small_names] + [loss_part[0:1, 0:1]])
        small_sent.append(_allsum_start(vec, after=vec, name="reduce_small_start"))
        return small_sent[0][-1]

    sent = []

    def grads_ready(names, slabs):
        started = _exchange_start(slabs, scatter=True, after=slabs[0], name="scatter_%s_start" % names[0])
        sent.append((names, started))
        return started[-1]

    grad_x, gr = _local_step(x, target, full, tt=tt, start_token=token, more_weights=more_weights,
                             grads_ready=grads_ready, small_ready=small_ready)

    halves = {}
    for names, started in sent:
        stacks, landed = _exchange_wait(started, scatter=True, after=grad_x, name="scatter_%s_wait" % names[0])
        for n, s, l in zip(names, stacks, landed):
            halves[n] = _sum4(lax.dynamic_index_in_dim(s, q, 0, keepdims=False), l, name="sum4_" + n)
    halves = [halves[n] for n in BIG]
    sib = _swap_sibling(halves, name="swap_halves")
    grad, delta, new_m, new_v = {}, {}, {}, {}
    for n, h, hs in zip(BIG, halves, sib):
        outs = _sum_adam(h, hs, w[n][0], m[n][0], v[n][0], name="adam_" + n)
        grad[n], delta[n], new_m[n], new_v[n] = [o[None] for o in outs]

    small_full_shapes = [full[n].shape for n in small_names]
    red = _allsum_wait(small_sent[0], after=grad_x, name="reduce_small_wait")
    *red, loss = _unpack_rows(red, small_full_shapes + [()])
    red = dict(zip(small_names, red))
    gsm = {}
    for n in REPLICATED:
        gsm[n] = red[n].reshape(w[n].shape)
    for n in SMALL_SHARDED:
        width = w[n].shape[-1]
        gsm[n] = lax.dynamic_slice_in_dim(red[n], q * width, width, axis=red[n].ndim - 1).reshape(w[n].shape)
    shapes = [w[n].shape for n in small_names]
    d_p, m_p, v_p = _adam_rows(_pack_rows([w[n] for n in small_names]), _pack_rows([gsm[n] for n in small_names]),
                               _pack_rows([m[n] for n in small_names]), _pack_rows([v[n] for n in small_names]),
                               name="adam_small")
    for n, dd, mm, vv in zip(small_names, _unpack_rows(d_p, shapes), _unpack_rows(m_p, shapes), _unpack_rows(v_p, shapes)):
        grad[n], delta[n], new_m[n], new_v[n] = gsm[n], dd, mm, vv
    return loss, grad_x, grad, delta, new_m, new_v


def kernel(x, ffn1_w_in, ffn1_w_out, w_in, mu_prev, mu_next, w0, w2, a0, a2, g2, k_k, k_a, r_k, lnx_g, lnx_b, conv_dw, conv_b, conv_ln_g, conv_ln_b, w_out, ffn2_w_in, ffn2_w_out, ln1_g, ln1_b, ln2_g, ln2_b, ln3_g, ln3_b, loss_target, m_ffn1_w_in, m_ffn1_w_out, m_w_in, m_mu_prev, m_mu_next, m_w0, m_w2, m_a0, m_a2, m_g2, m_k_k, m_k_a, m_r_k, m_lnx_g, m_lnx_b, m_conv_dw, m_conv_b, m_conv_ln_g, m_conv_ln_b, m_w_out, m_ffn2_w_in, m_ffn2_w_out, m_ln1_g, m_ln1_b, m_ln2_g, m_ln2_b, m_ln3_g, m_ln3_b, v_ffn1_w_in, v_ffn1_w_out, v_w_in, v_mu_prev, v_mu_next, v_w0, v_w2, v_a0, v_a2, v_g2, v_k_k, v_k_a, v_r_k, v_lnx_g, v_lnx_b, v_conv_dw, v_conv_b, v_conv_ln_g, v_conv_ln_b, v_w_out, v_ffn2_w_in, v_ffn2_w_out, v_ln1_g, v_ln1_b, v_ln2_g, v_ln2_b, v_ln3_g, v_ln3_b):
    args = dict(locals())
    w = {n: args[n] for n in WEIGHTS}
    m = {n: args["m_" + n] for n in WEIGHTS}
    v = {n: args["v_" + n] for n in WEIGHTS}
    seq = x.shape[1]
    loss, grad_x, grad, delta, new_m, new_v = _train_step(x, loss_target, w, m, v, tt=min(256, seq))
    return (loss, grad_x, *[grad[n] for n in WEIGHTS], *[delta[n] for n in WEIGHTS],
            *[new_m[n] for n in WEIGHTS], *[new_v[n] for n in WEIGHTS])
```
